```python
import math
import jax, jax.numpy as jnp
from jax import lax
import numpy as np

D_MODEL = 1024
BATCH = 8
SEQ = 8192
DEPTH = 2

N_META = 16
N_A_LAYERS = DEPTH // 2
N_B_LAYERS = DEPTH - N_A_LAYERS
NORM_EPS = 1e-6

GDN_QK_HEADS = 8
GDN_V_HEADS = 16
GDN_DK = 128
GDN_DV = 128
GDN_CONV = 4
GDN_CHUNK = 64
GDN_QK_W = GDN_QK_HEADS * GDN_DK
GDN_V_W = GDN_V_HEADS * GDN_DV
GDN_CONV_W = 2 * GDN_QK_W + GDN_V_W
GDN_IN_W = GDN_CONV_W + GDN_V_W + 2 * GDN_V_HEADS

MLA_HEADS = 16
MLA_NOPE = 128
MLA_ROPE = 64
MLA_V = 128
MLA_Q_RANK = 256
MLA_KV_RANK = 128
MLA_QK = MLA_NOPE + MLA_ROPE
MLA_V_W = MLA_HEADS * MLA_V
MLA_IN_W = MLA_Q_RANK + MLA_V_W
ROPE_THETA = 10000.0
Q_BLOCK = 128

kernel_name = "yoco_gdn_mla_hybrid"


def rmsnorm(x, g):
    xf = x.astype(jnp.float32)
    y = xf * lax.rsqrt(jnp.mean(xf * xf, axis=-1, keepdims=True) + NORM_EPS)
    return (y * g.astype(jnp.float32)).astype(x.dtype)


def l2norm(x):
    xf = x.astype(jnp.float32)
    return (xf * lax.rsqrt(jnp.sum(xf * xf, axis=-1, keepdims=True) + NORM_EPS)).astype(x.dtype)


def causal_depthwise_conv(x, w):
    k_len, ch = w.shape
    return lax.conv_general_dilated(x, w[:, None, :].astype(x.dtype), window_strides=(1,),
                                    padding=[(k_len - 1, 0)],
                                    dimension_numbers=('NWC', 'WIO', 'NWC'),
                                    feature_group_count=ch)


def rope_tables(length):
    inv = ROPE_THETA ** (-jnp.arange(0, MLA_ROPE, 2, dtype=jnp.float32) / MLA_ROPE)
    ang = jnp.arange(length, dtype=jnp.float32)[:, None] * inv[None, :]
    return jnp.cos(ang), jnp.sin(ang)


def apply_rope(x, cos, sin):
    xf = x.astype(jnp.float32)
    half = MLA_ROPE // 2
    x1, x2 = xf[..., :half], xf[..., half:]
    return jnp.concatenate([x1 * cos - x2 * sin, x2 * cos + x1 * sin], axis=-1).astype(x.dtype)


def gated_delta_rule_chunked(q, k, v, beta, g):
    B, L, H, dk = q.shape
    dv = v.shape[-1]
    C = GDN_CHUNK
    pad = (-L) % C
    n_chunks = (L + pad) // C

    def blocks(t):
        t = jnp.pad(t.astype(jnp.float32), [(0, 0), (pad, 0)] + [(0, 0)] * (t.ndim - 2))
        t = t.reshape((B, n_chunks, C) + t.shape[2:])
        return jnp.moveaxis(t, 3, 1)

    q, k, v, beta, g = blocks(q), blocks(k), blocks(v), blocks(beta), blocks(g)
    gc = jnp.cumsum(g, axis=-1)
    idx = jnp.arange(C)
    incl = idx[:, None] >= idx[None, :]
    strict = idx[:, None] > idx[None, :]
    decay = jnp.exp(jnp.where(incl, gc[..., :, None] - gc[..., None, :], -jnp.inf))

    kb = k * beta[..., None]
    vb = v * beta[..., None]
    m = jnp.einsum('bhnid,bhnjd->bhnij', kb, k) * jnp.where(strict, decay, 0.0)
    eye = jnp.eye(C, dtype=jnp.float32)
    rhs = jnp.concatenate([vb, kb * jnp.exp(gc)[..., None]], axis=-1)
    sol = lax.linalg.triangular_solve(m + eye, rhs, left_side=True, lower=True, unit_diagonal=True)
    u, w = sol[..., :dv], sol[..., dv:]

    attn = jnp.einsum('bhnid,bhnjd->bhnij', q, k) * decay
    q_dec = q * jnp.exp(gc)[..., None]
    k_dec = k * jnp.exp(gc[..., -1:] - gc)[..., None]
    g_last = jnp.exp(gc[..., -1])

    def step(state, xs):
        u_c, w_c, qd_c, kd_c, a_c, gl_c = xs
        v_new = u_c - jnp.einsum('bhcd,bhde->bhce', w_c, state)
        o_c = jnp.einsum('bhcd,bhde->bhce', qd_c, state) + jnp.einsum('bhij,bhje->bhie', a_c, v_new)
        state = state * gl_c[..., None, None] + jnp.einsum('bhcd,bhce->bhde', kd_c, v_new)
        return state, o_c

    xs = tuple(jnp.moveaxis(t, 2, 0) for t in (u, w, q_dec, k_dec, attn, g_last))
    s0 = jnp.zeros((B, H, dk, dv), jnp.float32)
    _, o = lax.scan(step, s0, xs)
    o = jnp.transpose(o, (1, 0, 3, 2, 4)).reshape(B, n_chunks * C, H, dv)
    return o[:, pad:]


def gdn_mixer(h, w_in, conv_w, a_log, dt_bias, out_norm, w_out):
    B, L, _ = h.shape
    proj = h @ w_in
    s1, s2, s3 = GDN_CONV_W, GDN_CONV_W + GDN_V_W, GDN_CONV_W + GDN_V_W + GDN_V_HEADS
    qkv, z, b, a = proj[..., :s1], proj[..., s1:s2], proj[..., s2:s3], proj[..., s3:]
    qkv = jax.nn.silu(causal_depthwise_conv(qkv, conv_w))
    q = l2norm(qkv[..., :GDN_QK_W].reshape(B, L, GDN_QK_HEADS, GDN_DK)) * (GDN_DK ** -0.5)
    k = l2norm(qkv[..., GDN_QK_W:2 * GDN_QK_W].reshape(B, L, GDN_QK_HEADS, GDN_DK))
    v = qkv[..., 2 * GDN_QK_W:].reshape(B, L, GDN_V_HEADS, GDN_DV)
    rep = GDN_V_HEADS // GDN_QK_HEADS
    q = jnp.repeat(q, rep, axis=2)
    k = jnp.repeat(k, rep, axis=2)
    beta = jax.nn.sigmoid(b.astype(jnp.float32))
    g = -jnp.exp(a_log.astype(jnp.float32)) * jax.nn.softplus(a.astype(jnp.float32) + dt_bias.astype(jnp.float32))
    o = gated_delta_rule_chunked(q, k, v, beta, g)
    o = rmsnorm(o, out_norm) * jax.nn.silu(z.astype(jnp.float32).reshape(B, L, GDN_V_HEADS, GDN_DV))
    return o.reshape(B, L, GDN_V_W).astype(h.dtype) @ w_out


def mla_shared_kv(h, kv_norm, kv_w_down, kv_latent_norm, kv_w_up, cos, sin):
    B, L, _ = h.shape
    ckr = rmsnorm(h, kv_norm) @ kv_w_down
    c_kv = rmsnorm(ckr[..., :MLA_KV_RANK], kv_latent_norm)
    k_rope = apply_rope(ckr[..., MLA_KV_RANK:], cos, sin)
    kv = (c_kv @ kv_w_up).reshape(B, L, MLA_HEADS, MLA_NOPE + MLA_V)
    return kv[..., :MLA_NOPE], k_rope, kv[..., MLA_NOPE:]


def causal_block_attention(q_nope, q_rope, k_nope, k_rope, v):
    B, L, H, _ = q_nope.shape
    n_blocks = -(-L // Q_BLOCK)
    pad = n_blocks * Q_BLOCK - L

    def blocks(t):
        t = jnp.pad(t, [(0, 0), (0, pad)] + [(0, 0)] * (t.ndim - 2))
        return jnp.moveaxis(t.reshape((B, n_blocks, Q_BLOCK) + t.shape[2:]), 1, 0)

    scale = MLA_QK ** -0.5
    k_pos = jnp.arange(L)

    def one_block(args):
        qn_b, qr_b, blk = args
        s = (jnp.einsum('bqhd,bkhd->bhqk', qn_b, k_nope, preferred_element_type=jnp.float32)
             + jnp.einsum('bqhr,bkr->bhqk', qr_b, k_rope, preferred_element_type=jnp.float32))
        q_pos = blk * Q_BLOCK + jnp.arange(Q_BLOCK)
        s = jnp.where(q_pos[:, None] >= k_pos[None, :], s * scale, -jnp.inf)
        p = jax.nn.softmax(s, axis=-1)
        return jnp.einsum('bhqk,bkhd->bqhd', p.astype(v.dtype), v)

    o = lax.map(one_block, (blocks(q_nope), blocks(q_rope), jnp.arange(n_blocks)))
    return jnp.moveaxis(o, 0, 1).reshape(B, n_blocks * Q_BLOCK, H, MLA_V)[:, :L]


def mla_mixer(h, w_in, q_latent_norm, w_q_up, w_out, k_nope, k_rope, v, cos, sin):
    B, L, _ = h.shape
    proj = h @ w_in
    c_q = rmsnorm(proj[..., :MLA_Q_RANK], q_latent_norm)
    z = proj[..., MLA_Q_RANK:]
    q = (c_q @ w_q_up).reshape(B, L, MLA_HEADS, MLA_QK)
    q_nope = q[..., :MLA_NOPE]
    q_rope = apply_rope(q[..., MLA_NOPE:], cos[:, None, :], sin[:, None, :])
    o = causal_block_attention(q_nope, q_rope, k_nope, k_rope, v).reshape(B, L, MLA_V_W)
    return (o * jax.nn.silu(z)) @ w_out


def _fwd_setup_inputs(seed: int = 0) -> dict:
    key = jax.random.key(seed)
    ks = jax.random.split(key, 20)
    nrm = lambda k, shape, s: jax.random.normal(k, shape, jnp.float32) * s
    gain = lambda k, shape: 1.0 + 0.02 * jax.random.normal(k, shape, jnp.float32)
    dt = jnp.exp(jax.random.uniform(ks[5], (N_A_LAYERS, GDN_V_HEADS), jnp.float32,
                                    math.log(1e-3), math.log(1e-1)))
    return {
        "x": nrm(ks[0], (BATCH, SEQ, D_MODEL), 1.0),
        "meta_tokens": nrm(ks[1], (N_META, D_MODEL), 1.0),
        "pre_norm": gain(ks[2], (DEPTH, D_MODEL)),
        "post_norm": gain(ks[3], (DEPTH, D_MODEL)),
        "gdn_w_in": nrm(ks[4], (N_A_LAYERS, D_MODEL, GDN_IN_W), D_MODEL ** -0.5),
        "gdn_conv_w": nrm(ks[6], (N_A_LAYERS, GDN_CONV, GDN_CONV_W), GDN_CONV ** -0.5),
        "gdn_a_log": jnp.log(jax.random.uniform(ks[7], (N_A_LAYERS, GDN_V_HEADS), jnp.float32, 1.0, 16.0)),
        "gdn_dt_bias": dt + jnp.log(-jnp.expm1(-dt)),
        "gdn_out_norm": gain(ks[8], (N_A_LAYERS, GDN_DV)),
        "gdn_w_out": nrm(ks[9], (N_A_LAYERS, GDN_V_W, D_MODEL), GDN_V_W ** -0.5),
        "kv_norm": gain(ks[10], (D_MODEL,)),
        "kv_w_down": nrm(ks[11], (D_MODEL, MLA_KV_RANK + MLA_ROPE), D_MODEL ** -0.5),
        "kv_latent_norm": gain(ks[12], (MLA_KV_RANK,)),
        "kv_w_up": nrm(ks[13], (MLA_KV_RANK, MLA_HEADS * (MLA_NOPE + MLA_V)), MLA_KV_RANK ** -0.5),
        "mla_w_in": nrm(ks[14], (N_B_LAYERS, D_MODEL, MLA_IN_W), D_MODEL ** -0.5),
        "mla_q_latent_norm": gain(ks[15], (N_B_LAYERS, MLA_Q_RANK)),
        "mla_w_q_up": nrm(ks[16], (N_B_LAYERS, MLA_Q_RANK, MLA_HEADS * MLA_QK), MLA_Q_RANK ** -0.5),
        "mla_w_out": nrm(ks[17], (N_B_LAYERS, MLA_V_W, D_MODEL), MLA_V_W ** -0.5),
    }


def _fwd_reference(x, meta_tokens, pre_norm, post_norm, gdn_w_in, gdn_conv_w, gdn_a_log, gdn_dt_bias,
              gdn_out_norm, gdn_w_out, kv_norm, kv_w_down, kv_latent_norm, kv_w_up,
              mla_w_in, mla_q_latent_norm, mla_w_q_up, mla_w_out):
    B = x.shape[0]
    meta = jnp.broadcast_to(meta_tokens[None].astype(x.dtype), (B, N_META, D_MODEL))
    h = jnp.concatenate([meta, x], axis=1)
    cos, sin = rope_tables(h.shape[1])
    shared_kv = None
    for layer in range(DEPTH):
        hn = rmsnorm(h, pre_norm[layer])
        if layer < N_A_LAYERS:
            y = gdn_mixer(hn, gdn_w_in[layer], gdn_conv_w[layer], gdn_a_log[layer], gdn_dt_bias[layer],
                          gdn_out_norm[layer], gdn_w_out[layer])
        else:
            if layer == N_A_LAYERS:
                shared_kv = mla_shared_kv(h, kv_norm, kv_w_down, kv_latent_norm, kv_w_up, cos, sin)
            j = layer - N_A_LAYERS
            k_nope, k_rope, v = shared_kv
            y = mla_mixer(hn, mla_w_in[j], mla_q_latent_norm[j], mla_w_q_up[j], mla_w_out[j],
                          k_nope, k_rope, v, cos, sin)
        h = h + rmsnorm(y, post_norm[layer])
    return h[:, N_META:]


import jax as _jax
import jax.numpy as _jnp

TWIN_FORMAT = 'train_step'
FWD_PARAMS = ['x', 'meta_tokens', 'pre_norm', 'post_norm', 'gdn_w_in', 'gdn_conv_w', 'gdn_a_log', 'gdn_dt_bias', 'gdn_out_norm', 'gdn_w_out', 'kv_norm', 'kv_w_down', 'kv_latent_norm', 'kv_w_up', 'mla_w_in', 'mla_q_latent_norm', 'mla_w_q_up', 'mla_w_out']
TWIN_WEIGHTS = ['meta_tokens', 'pre_norm', 'post_norm', 'gdn_w_in', 'gdn_conv_w', 'gdn_a_log', 'gdn_dt_bias', 'gdn_out_norm', 'gdn_w_out', 'kv_norm', 'kv_w_down', 'kv_latent_norm', 'kv_w_up', 'mla_w_in', 'mla_q_latent_norm', 'mla_w_q_up', 'mla_w_out']
TWIN_DIFF_INPUT = 'x'
TWIN_INPUTS = ['x', 'meta_tokens', 'pre_norm', 'post_norm', 'gdn_w_in', 'gdn_conv_w', 'gdn_a_log', 'gdn_dt_bias', 'gdn_out_norm', 'gdn_w_out', 'kv_norm', 'kv_w_down', 'kv_latent_norm', 'kv_w_up', 'mla_w_in', 'mla_q_latent_norm', 'mla_w_q_up', 'mla_w_out', 'loss_target', 'm_meta_tokens', 'm_pre_norm', 'm_post_norm', 'm_gdn_w_in', 'm_gdn_conv_w', 'm_gdn_a_log', 'm_gdn_dt_bias', 'm_gdn_out_norm', 'm_gdn_w_out', 'm_kv_norm', 'm_kv_w_down', 'm_kv_latent_norm', 'm_kv_w_up', 'm_mla_w_in', 'm_mla_q_latent_norm', 'm_mla_w_q_up', 'm_mla_w_out', 'v_meta_tokens', 'v_pre_norm', 'v_post_norm', 'v_gdn_w_in', 'v_gdn_conv_w', 'v_gdn_a_log', 'v_gdn_dt_bias', 'v_gdn_out_norm', 'v_gdn_w_out', 'v_kv_norm', 'v_kv_w_down', 'v_kv_latent_norm', 'v_kv_w_up', 'v_mla_w_in', 'v_mla_q_latent_norm', 'v_mla_w_q_up', 'v_mla_w_out']
TWIN_OUTPUTS = ['loss', 'grad_x', 'grad_meta_tokens', 'grad_pre_norm', 'grad_post_norm', 'grad_gdn_w_in', 'grad_gdn_conv_w', 'grad_gdn_a_log', 'grad_gdn_dt_bias', 'grad_gdn_out_norm', 'grad_gdn_w_out', 'grad_kv_norm', 'grad_kv_w_down', 'grad_kv_latent_norm', 'grad_kv_w_up', 'grad_mla_w_in', 'grad_mla_q_latent_norm', 'grad_mla_w_q_up', 'grad_mla_w_out', 'delta_meta_tokens', 'delta_pre_norm', 'delta_post_norm', 'delta_gdn_w_in', 'delta_gdn_conv_w', 'delta_gdn_a_log', 'delta_gdn_dt_bias', 'delta_gdn_out_norm', 'delta_gdn_w_out', 'delta_kv_norm', 'delta_kv_w_down', 'delta_kv_latent_norm', 'delta_kv_w_up', 'delta_mla_w_in', 'delta_mla_q_latent_norm', 'delta_mla_w_q_up', 'delta_mla_w_out', 'new_m_meta_tokens', 'new_m_pre_norm', 'new_m_post_norm', 'new_m_gdn_w_in', 'new_m_gdn_conv_w', 'new_m_gdn_a_log', 'new_m_gdn_dt_bias', 'new_m_gdn_out_norm', 'new_m_gdn_w_out', 'new_m_kv_norm', 'new_m_kv_w_down', 'new_m_kv_latent_norm', 'new_m_kv_w_up', 'new_m_mla_w_in', 'new_m_mla_q_latent_norm', 'new_m_mla_w_q_up', 'new_m_mla_w_out', 'new_v_meta_tokens', 'new_v_pre_norm', 'new_v_post_norm', 'new_v_gdn_w_in', 'new_v_gdn_conv_w', 'new_v_gdn_a_log', 'new_v_gdn_dt_bias', 'new_v_gdn_out_norm', 'new_v_gdn_w_out', 'new_v_kv_norm', 'new_v_kv_w_down', 'new_v_kv_latent_norm', 'new_v_kv_w_up', 'new_v_mla_w_in', 'new_v_mla_q_latent_norm', 'new_v_mla_w_q_up', 'new_v_mla_w_out']
TWIN_LEAF_KINDS = {'loss': 'loss', 'grad_x': 'grad_x', 'grad_meta_tokens': 'grad_w', 'grad_pre_norm': 'grad_w', 'grad_post_norm': 'grad_w', 'grad_gdn_w_in': 'grad_w', 'grad_gdn_conv_w': 'grad_w', 'grad_gdn_a_log': 'grad_w', 'grad_gdn_dt_bias': 'grad_w', 'grad_gdn_out_norm': 'grad_w', 'grad_gdn_w_out': 'grad_w', 'grad_kv_norm': 'grad_w', 'grad_kv_w_down': 'grad_w', 'grad_kv_latent_norm': 'grad_w', 'grad_kv_w_up': 'grad_w', 'grad_mla_w_in': 'grad_w', 'grad_mla_q_latent_norm': 'grad_w', 'grad_mla_w_q_up': 'grad_w', 'grad_mla_w_out': 'grad_w', 'delta_meta_tokens': 'delta_w', 'delta_pre_norm': 'delta_w', 'delta_post_norm': 'delta_w', 'delta_gdn_w_in': 'delta_w', 'delta_gdn_conv_w': 'delta_w', 'delta_gdn_a_log': 'delta_w', 'delta_gdn_dt_bias': 'delta_w', 'delta_gdn_out_norm': 'delta_w', 'delta_gdn_w_out': 'delta_w', 'delta_kv_norm': 'delta_w', 'delta_kv_w_down': 'delta_w', 'delta_kv_latent_norm': 'delta_w', 'delta_kv_w_up': 'delta_w', 'delta_mla_w_in': 'delta_w', 'delta_mla_q_latent_norm': 'delta_w', 'delta_mla_w_q_up': 'delta_w', 'delta_mla_w_out': 'delta_w', 'new_m_meta_tokens': 'new_m', 'new_m_pre_norm': 'new_m', 'new_m_post_norm': 'new_m', 'new_m_gdn_w_in': 'new_m', 'new_m_gdn_conv_w': 'new_m', 'new_m_gdn_a_log': 'new_m', 'new_m_gdn_dt_bias': 'new_m', 'new_m_gdn_out_norm': 'new_m', 'new_m_gdn_w_out': 'new_m', 'new_m_kv_norm': 'new_m', 'new_m_kv_w_down': 'new_m', 'new_m_kv_latent_norm': 'new_m', 'new_m_kv_w_up': 'new_m', 'new_m_mla_w_in': 'new_m', 'new_m_mla_q_latent_norm': 'new_m', 'new_m_mla_w_q_up': 'new_m', 'new_m_mla_w_out': 'new_m', 'new_v_meta_tokens': 'new_v', 'new_v_pre_norm': 'new_v', 'new_v_post_norm': 'new_v', 'new_v_gdn_w_in': 'new_v', 'new_v_gdn_conv_w': 'new_v', 'new_v_gdn_a_log': 'new_v', 'new_v_gdn_dt_bias': 'new_v', 'new_v_gdn_out_norm': 'new_v', 'new_v_gdn_w_out': 'new_v', 'new_v_kv_norm': 'new_v', 'new_v_kv_w_down': 'new_v', 'new_v_kv_latent_norm': 'new_v', 'new_v_kv_w_up': 'new_v', 'new_v_mla_w_in': 'new_v', 'new_v_mla_q_latent_norm': 'new_v', 'new_v_mla_w_q_up': 'new_v', 'new_v_mla_w_out': 'new_v'}


def _forward(args):
    return _fwd_reference(*[args[k] for k in FWD_PARAMS])


def _output_shape():
    def fwd():
        inp = _fwd_setup_inputs(0)
        return _fwd_reference(*[inp[k] for k in FWD_PARAMS])
    out = _jax.eval_shape(fwd)
    return out.shape, out.dtype

N_MICROBATCH = 1
ADAM_LR = 0.001
ADAM_B1 = 0.9
ADAM_B2 = 0.999
ADAM_EPS = 1e-08
ADAM_WD = 0.01
ADAM_STEP = 10
PER_EXAMPLE_BATCH_AXIS = {'x': 0, 'loss_target': 0}
SHARED_INPUTS = []
_WEIGHT_DTYPES = {'meta_tokens': _jnp.float32, 'pre_norm': _jnp.float32, 'post_norm': _jnp.float32, 'gdn_w_in': _jnp.float32, 'gdn_conv_w': _jnp.float32, 'gdn_a_log': _jnp.float32, 'gdn_dt_bias': _jnp.float32, 'gdn_out_norm': _jnp.float32, 'gdn_w_out': _jnp.float32, 'kv_norm': _jnp.float32, 'kv_w_down': _jnp.float32, 'kv_latent_norm': _jnp.float32, 'kv_w_up': _jnp.float32, 'mla_w_in': _jnp.float32, 'mla_q_latent_norm': _jnp.float32, 'mla_w_q_up': _jnp.float32, 'mla_w_out': _jnp.float32}
MOMENT_SCALE = {'meta_tokens': 9.827744e-02, 'pre_norm': 1.441085e+00, 'post_norm': 6.458684e+01, 'gdn_w_in': 4.835545e-01, 'gdn_conv_w': 1.734948e+00, 'gdn_a_log': 3.852169e+00, 'gdn_dt_bias': 3.660974e+00, 'gdn_out_norm': 1.707232e+01, 'gdn_w_out': 6.018674e+00, 'kv_norm': 3.964642e+00, 'kv_w_down': 8.925949e+00, 'kv_latent_norm': 1.042892e+01, 'kv_w_up': 2.338469e+00, 'mla_w_in': 1.428959e+00, 'mla_q_latent_norm': 6.544324e-01, 'mla_w_q_up': 2.053778e-01, 'mla_w_out': 4.441948e+00}


def _to_microbatches(a, axis):
    t = _jnp.moveaxis(a, axis, 0)
    t = t.reshape((N_MICROBATCH, t.shape[0] // N_MICROBATCH) + t.shape[1:])
    return _jnp.moveaxis(t, 1, axis + 1)


def setup_inputs(seed: int = 0) -> dict:
    inp = _fwd_setup_inputs(seed)
    key = _jax.random.fold_in(_jax.random.key(seed), 7919)
    shape, _ = _output_shape()
    out = dict(inp)
    out["loss_target"] = _jax.random.normal(_jax.random.fold_in(key, 0), shape, _jnp.float32)
    for i, name in enumerate(TWIN_WEIGHTS):
        w = inp[name].astype(_jnp.float32)
        if MOMENT_SCALE is None:
            s = _jnp.sqrt(_jnp.mean(_jnp.square(w)) + 1e-30)
        else:
            s = MOMENT_SCALE[name]
        km, kv = _jax.random.split(_jax.random.fold_in(key, i + 1))
        out[name] = w
        out["m_" + name] = s * _jax.random.normal(km, w.shape, _jnp.float32)
        out["v_" + name] = (s * s) * _jax.random.uniform(kv, w.shape, _jnp.float32, 0.5, 1.5)
    if N_MICROBATCH > 1:
        for name, axis in PER_EXAMPLE_BATCH_AXIS.items():
            out[name] = _to_microbatches(out[name], axis)
    return {'x': out['x'], 'meta_tokens': out['meta_tokens'], 'pre_norm': out['pre_norm'], 'post_norm': out['post_norm'], 'gdn_w_in': out['gdn_w_in'], 'gdn_conv_w': out['gdn_conv_w'], 'gdn_a_log': out['gdn_a_log'], 'gdn_dt_bias': out['gdn_dt_bias'], 'gdn_out_norm': out['gdn_out_norm'], 'gdn_w_out': out['gdn_w_out'], 'kv_norm': out['kv_norm'], 'kv_w_down': out['kv_w_down'], 'kv_latent_norm': out['kv_latent_norm'], 'kv_w_up': out['kv_w_up'], 'mla_w_in': out['mla_w_in'], 'mla_q_latent_norm': out['mla_q_latent_norm'], 'mla_w_q_up': out['mla_w_q_up'], 'mla_w_out': out['mla_w_out'], 'loss_target': out['loss_target'], 'm_meta_tokens': out['m_meta_tokens'], 'm_pre_norm': out['m_pre_norm'], 'm_post_norm': out['m_post_norm'], 'm_gdn_w_in': out['m_gdn_w_in'], 'm_gdn_conv_w': out['m_gdn_conv_w'], 'm_gdn_a_log': out['m_gdn_a_log'], 'm_gdn_dt_bias': out['m_gdn_dt_bias'], 'm_gdn_out_norm': out['m_gdn_out_norm'], 'm_gdn_w_out': out['m_gdn_w_out'], 'm_kv_norm': out['m_kv_norm'], 'm_kv_w_down': out['m_kv_w_down'], 'm_kv_latent_norm': out['m_kv_latent_norm'], 'm_kv_w_up': out['m_kv_w_up'], 'm_mla_w_in': out['m_mla_w_in'], 'm_mla_q_latent_norm': out['m_mla_q_latent_norm'], 'm_mla_w_q_up': out['m_mla_w_q_up'], 'm_mla_w_out': out['m_mla_w_out'], 'v_meta_tokens': out['v_meta_tokens'], 'v_pre_norm': out['v_pre_norm'], 'v_post_norm': out['v_post_norm'], 'v_gdn_w_in': out['v_gdn_w_in'], 'v_gdn_conv_w': out['v_gdn_conv_w'], 'v_gdn_a_log': out['v_gdn_a_log'], 'v_gdn_dt_bias': out['v_gdn_dt_bias'], 'v_gdn_out_norm': out['v_gdn_out_norm'], 'v_gdn_w_out': out['v_gdn_w_out'], 'v_kv_norm': out['v_kv_norm'], 'v_kv_w_down': out['v_kv_w_down'], 'v_kv_latent_norm': out['v_kv_latent_norm'], 'v_kv_w_up': out['v_kv_w_up'], 'v_mla_w_in': out['v_mla_w_in'], 'v_mla_q_latent_norm': out['v_mla_q_latent_norm'], 'v_mla_w_q_up': out['v_mla_w_q_up'], 'v_mla_w_out': out['v_mla_w_out']}


def _loss(weights, diff, rest, loss_target):
    with _jax.named_scope("forward"):
        args = {**rest, TWIN_DIFF_INPUT: diff, **{k: w.astype(_WEIGHT_DTYPES[k]) for k, w in weights.items()}}
        y = _forward(args)
    with _jax.named_scope("loss_head"):
        err = _jnp.square(y.astype(_jnp.float32) - loss_target)
        return 0.5 * _jnp.sum(_jnp.mean(err, axis=-1)) if err.ndim else 0.5 * err


def _adamw(w, g, m, v):
    m = ADAM_B1 * m + (1.0 - ADAM_B1) * g
    v = ADAM_B2 * v + (1.0 - ADAM_B2) * _jnp.square(g)
    m_hat = m / (1.0 - ADAM_B1 ** ADAM_STEP)
    v_hat = v / (1.0 - ADAM_B2 ** ADAM_STEP)
    delta = -ADAM_LR * (m_hat / (_jnp.sqrt(v_hat) + ADAM_EPS) + ADAM_WD * w)
    return delta, m, v


def reference(x, meta_tokens, pre_norm, post_norm, gdn_w_in, gdn_conv_w, gdn_a_log, gdn_dt_bias, gdn_out_norm, gdn_w_out, kv_norm, kv_w_down, kv_latent_norm, kv_w_up, mla_w_in, mla_q_latent_norm, mla_w_q_up, mla_w_out, loss_target, m_meta_tokens, m_pre_norm, m_post_norm, m_gdn_w_in, m_gdn_conv_w, m_gdn_a_log, m_gdn_dt_bias, m_gdn_out_norm, m_gdn_w_out, m_kv_norm, m_kv_w_down, m_kv_latent_norm, m_kv_w_up, m_mla_w_in, m_mla_q_latent_norm, m_mla_w_q_up, m_mla_w_out, v_meta_tokens, v_pre_norm, v_post_norm, v_gdn_w_in, v_gdn_conv_w, v_gdn_a_log, v_gdn_dt_bias, v_gdn_out_norm, v_gdn_w_out, v_kv_norm, v_kv_w_down, v_kv_latent_norm, v_kv_w_up, v_mla_w_in, v_mla_q_latent_norm, v_mla_w_q_up, v_mla_w_out):
    given = dict(x=x, meta_tokens=meta_tokens, pre_norm=pre_norm, post_norm=post_norm, gdn_w_in=gdn_w_in, gdn_conv_w=gdn_conv_w, gdn_a_log=gdn_a_log, gdn_dt_bias=gdn_dt_bias, gdn_out_norm=gdn_out_norm, gdn_w_out=gdn_w_out, kv_norm=kv_norm, kv_w_down=kv_w_down, kv_latent_norm=kv_latent_norm, kv_w_up=kv_w_up, mla_w_in=mla_w_in, mla_q_latent_norm=mla_q_latent_norm, mla_w_q_up=mla_w_q_up, mla_w_out=mla_w_out, loss_target=loss_target, m_meta_tokens=m_meta_tokens, m_pre_norm=m_pre_norm, m_post_norm=m_post_norm, m_gdn_w_in=m_gdn_w_in, m_gdn_conv_w=m_gdn_conv_w, m_gdn_a_log=m_gdn_a_log, m_gdn_dt_bias=m_gdn_dt_bias, m_gdn_out_norm=m_gdn_out_norm, m_gdn_w_out=m_gdn_w_out, m_kv_norm=m_kv_norm, m_kv_w_down=m_kv_w_down, m_kv_latent_norm=m_kv_latent_norm, m_kv_w_up=m_kv_w_up, m_mla_w_in=m_mla_w_in, m_mla_q_latent_norm=m_mla_q_latent_norm, m_mla_w_q_up=m_mla_w_q_up, m_mla_w_out=m_mla_w_out, v_meta_tokens=v_meta_tokens, v_pre_norm=v_pre_norm, v_post_norm=v_post_norm, v_gdn_w_in=v_gdn_w_in, v_gdn_conv_w=v_gdn_conv_w, v_gdn_a_log=v_gdn_a_log, v_gdn_dt_bias=v_gdn_dt_bias, v_gdn_out_norm=v_gdn_out_norm, v_gdn_w_out=v_gdn_w_out, v_kv_norm=v_kv_norm, v_kv_w_down=v_kv_w_down, v_kv_latent_norm=v_kv_latent_norm, v_kv_w_up=v_kv_w_up, v_mla_w_in=v_mla_w_in, v_mla_q_latent_norm=v_mla_q_latent_norm, v_mla_w_q_up=v_mla_w_q_up, v_mla_w_out=v_mla_w_out)
    weights = {n: given[n] for n in TWIN_WEIGHTS}
    shared = {n: given[n] for n in SHARED_INPUTS}
    per_example = {n: given[n] for n in ['x']}
    grad_fn = _jax.value_and_grad(_loss, argnums=(0, 1))

    def one_microbatch(ex, loss_target):
        ex = dict(ex)
        diff = ex.pop(TWIN_DIFF_INPUT)
        return grad_fn(weights, diff, {**shared, **ex}, loss_target)

    if N_MICROBATCH == 1:
        loss, (grad_w, grad_x) = one_microbatch(per_example, given["loss_target"])
    else:
        def body(carry, xs):
            loss_sum, grad_sum = carry
            l_k, (gw_k, gx_k) = one_microbatch(xs[0], xs[1])
            with _jax.named_scope("update"):
                return (loss_sum + l_k, _jax.tree.map(_jnp.add, grad_sum, gw_k)), gx_k

        init = (_jnp.zeros((), _jnp.float32), _jax.tree.map(_jnp.zeros_like, weights))
        (loss, grad_w), grad_x = _jax.lax.scan(body, init, (per_example, given["loss_target"]))
    with _jax.named_scope("update"):
        delta_w, new_m, new_v = {}, {}, {}
        for n in TWIN_WEIGHTS:
            delta_w[n], new_m[n], new_v[n] = _adamw(weights[n], grad_w[n], given["m_" + n], given["v_" + n])
    return (loss, grad_x, *[grad_w[n] for n in TWIN_WEIGHTS], *[delta_w[n] for n in TWIN_WEIGHTS],
            *[new_m[n] for n in TWIN_WEIGHTS], *[new_v[n] for n in TWIN_WEIGHTS])
```

```python
import functools
import math

import jax
import jax.numpy as jnp
from jax import lax
from jax.experimental import pallas as pl
from jax.experimental.pallas import tpu as pltpu

f32, bf16 = jnp.float32, jnp.bfloat16
HIGHEST = lax.Precision.HIGHEST
MESH_ID = pl.DeviceIdType.MESH

N_DEV = 8
D_MODEL = 1024
N_META = 16
NORM_EPS = 1e-6
PAD_FRONT = 48
ROW0 = PAD_FRONT + N_META
GDN_QK_HEADS, GDN_V_HEADS, GDN_D = 8, 16, 128
GDN_CHUNK = 64
GDN_QK_W, GDN_V_W = GDN_QK_HEADS * GDN_D, GDN_V_HEADS * GDN_D
GDN_CONV_W = 2 * GDN_QK_W + GDN_V_W
GDN_IN_W = GDN_CONV_W + GDN_V_W + 2 * GDN_V_HEADS
GDN_IN_WP = GDN_CONV_W + GDN_V_W + 2 * 128
MLA_HEADS, MLA_NOPE, MLA_ROPE, MLA_V = 16, 128, 64, 128
MLA_Q_RANK, MLA_KV_RANK = 256, 128
MLA_QK = MLA_NOPE + MLA_ROPE
MLA_QKP = 256
MLA_V_W = MLA_HEADS * MLA_V
ROPE_THETA = 10000.0
NEG = -1e30
ROW_TILE = 256
ATT_TILE = 256

ADAM_LR, ADAM_B1, ADAM_B2, ADAM_EPS, ADAM_WD, ADAM_STEP = 0.001, 0.9, 0.999, 1e-08, 0.01, 10

NN = ((1,), (0,))
NT = ((1,), (1,))
TN = ((0,), (0,))


def _pick(dim, prefs):
    for p in prefs:
        if dim % p == 0:
            return p
    return dim


def _make_dots(full_precision):
    def d(a, b, dims):
        if full_precision:
            return lax.dot_general(a, b, (dims, ((), ())), precision=HIGHEST, preferred_element_type=f32)
        return lax.dot_general(a.astype(bf16), b.astype(bf16), (dims, ((), ())), preferred_element_type=f32)

    @jax.custom_vjp
    def nn(a, b):
        return d(a, b, NN)
    nn.defvjp(lambda a, b: (d(a, b, NN), (a, b)), lambda r, ct: (d(ct, r[1], NT), d(r[0], ct, TN)))

    @jax.custom_vjp
    def nt(a, b):
        return d(a, b, NT)
    nt.defvjp(lambda a, b: (d(a, b, NT), (a, b)), lambda r, ct: (d(ct, r[1], NN), d(ct, r[0], TN)))

    @jax.custom_vjp
    def tn(a, b):
        return d(a, b, TN)
    tn.defvjp(lambda a, b: (d(a, b, TN), (a, b)), lambda r, ct: (d(r[1], ct, NT), d(r[0], ct, NN)))
    return d, nn, nt, tn


_dlo, _nn, _nt, _tn = _make_dots(False)
_dhi, _nnh, _nth, _tnh = _make_dots(True)


@jax.custom_vjp
def _inv_unit_lower(m):
    n = m.shape[0]
    eye = (lax.broadcasted_iota(jnp.int32, (n, n), 0) == lax.broadcasted_iota(jnp.int32, (n, n), 1)).astype(f32)
    a = -m
    t = eye + a
    p = a
    for _ in range(int(math.log2(n)) - 1):
        p = _dhi(p, p, NN)
        t = t + _dhi(t, p, NN)
    return t


def _inv_fwd(m):
    t = _inv_unit_lower(m)
    return t, t


def _inv_bwd(t, ct):
    return (-_dhi(_dhi(t, ct, TN), t, NT),)


_inv_unit_lower.defvjp(_inv_fwd, _inv_bwd)


def _matmul(a, b, *, ta=False, tb=False, out_dtype=f32, name):
    assert not (ta and tb)
    if ta:
        kdim, m = a.shape
    else:
        m, kdim = a.shape
    n = b.shape[0] if tb else b.shape[1]
    assert (b.shape[1] if tb else b.shape[0]) == kdim
    tm = _pick(m, (768, 512, 384, 256, 128))
    tn = _pick(n, (1024, 768, 640, 512, 256, 128))
    tk = _pick(kdim, (1024, 768, 640, 512, 256, 128))
    nk = kdim // tk
    dims = TN if ta else (NT if tb else NN)

    def body(a_ref, b_ref, o_ref, acc_ref):
        k = pl.program_id(2)

        @pl.when(k == 0)
        def _():
            acc_ref[...] = jnp.zeros_like(acc_ref)

        acc_ref[...] += lax.dot_general(a_ref[...].astype(bf16), b_ref[...].astype(bf16), (dims, ((), ())),
                                        preferred_element_type=f32)

        @pl.when(k == nk - 1)
        def _():
            o_ref[...] = acc_ref[...].astype(o_ref.dtype)

    a_spec = pl.BlockSpec((tk, tm), lambda i, j, k: (k, i)) if ta else pl.BlockSpec((tm, tk), lambda i, j, k: (i, k))
    b_spec = pl.BlockSpec((tn, tk), lambda i, j, k: (j, k)) if tb else pl.BlockSpec((tk, tn), lambda i, j, k: (k, j))
    return pl.pallas_call(
        body, name=name, grid=(m // tm, n // tn, nk),
        in_specs=[a_spec, b_spec], out_specs=pl.BlockSpec((tm, tn), lambda i, j, k: (i, j)),
        out_shape=jax.ShapeDtypeStruct((m, n), out_dtype),
        scratch_shapes=[pltpu.VMEM((tm, tn), f32)],
        compiler_params=pltpu.CompilerParams(dimension_semantics=("parallel", "parallel", "arbitrary")),
    )(a, b)


def _row_spec(item, tr):
    a, bc, off = item
    if bc is None:
        return pl.BlockSpec((tr, a.shape[1]), lambda i, j: (i, 0))
    return pl.BlockSpec((tr, bc), lambda i, j, off=off: (i, j + off))


def _param_spec(p):
    return pl.BlockSpec(p.shape, lambda i, j: (0, 0))


def _rowwise(fn, rows, nodiff, params, outs, *, ncol=1, name, tr=ROW_TILE):
    lp = rows[0][0].shape[0]
    nr, nd = len(rows), len(nodiff)

    def body(*refs):
        rv = [r[...].astype(f32) for r in refs[:nr]]
        nv = [r[...] for r in refs[nr:nr + nd]]
        pv = [r[...] for r in refs[nr + nd:nr + nd + len(params)]]
        res = fn(rv, nv, pv)
        for ref, val in zip(refs[nr + nd + len(params):], res):
            ref[...] = val.astype(ref.dtype)

    out_specs = [pl.BlockSpec((tr, c if bc is None else bc), (lambda i, j: (i, 0)) if bc is None else (lambda i, j: (i, j)))
                 for (c, _, bc) in outs]
    return pl.pallas_call(
        body, name=name, grid=(lp // tr, ncol),
        in_specs=[_row_spec(it, tr) for it in rows + nodiff] + [_param_spec(p) for p in params],
        out_specs=out_specs,
        out_shape=[jax.ShapeDtypeStruct((lp, c), dt) for (c, dt, _) in outs],
    )(*[it[0] for it in rows + nodiff], *params)


def _rowwise_vjp(fn, rows, nodiff, params, cts, *, ncol=1, name, ct_pre=None, extra=None, grad_dtypes=None, tr=ROW_TILE):
    lp = rows[0][0].shape[0]
    nr, nd, npar, nct = len(rows), len(nodiff), len(params), len(cts)
    extra = extra or [None] * nr
    grad_dtypes = grad_dtypes or [f32] * nr
    ex_items = [(e, rows[k][1], 0) for k, e in enumerate(extra) if e is not None]
    ex_pos = [k for k, e in enumerate(extra) if e is not None]
    for (a, bc, _) in rows:
        assert bc is not None or ncol == 1

    def body(*refs):
        pos = 0
        rv = [r[...].astype(f32) for r in refs[pos:pos + nr]]; pos += nr
        nv = [r[...] for r in refs[pos:pos + nd]]; pos += nd
        pv = [r[...] for r in refs[pos:pos + npar]]; pos += npar
        cv = [r[...].astype(f32) for r in refs[pos:pos + nct]]; pos += nct
        ev = [r[...].astype(f32) for r in refs[pos:pos + len(ex_items)]]; pos += len(ex_items)
        drow_refs = refs[pos:pos + nr]; pos += nr
        dpar_refs = refs[pos:pos + npar]
        outs, vjp_fn = jax.vjp(lambda rr, pp: fn(rr, nv, pp), rv, pv)
        ctv = ct_pre(cv) if ct_pre is not None else cv
        drow, dpar = vjp_fn([c.astype(o.dtype) for c, o in zip(ctv, outs)])
        for k, e in zip(ex_pos, ev):
            drow[k] = drow[k] + e
        for ref, val in zip(drow_refs, drow):
            ref[...] = val.astype(ref.dtype)
        first = jnp.logical_and(pl.program_id(0) == 0, pl.program_id(1) == 0)

        @pl.when(first)
        def _():
            for ref, val in zip(dpar_refs, dpar):
                ref[...] = val

        @pl.when(jnp.logical_not(first))
        def _():
            for ref, val in zip(dpar_refs, dpar):
                ref[...] += val

    drow_shapes, drow_specs = [], []
    for (a, bc, _), dt in zip(rows, grad_dtypes):
        if bc is None:
            drow_shapes.append(jax.ShapeDtypeStruct((lp, a.shape[1]), dt))
            drow_specs.append(pl.BlockSpec((tr, a.shape[1]), lambda i, j: (i, 0)))
        else:
            drow_shapes.append(jax.ShapeDtypeStruct((lp, ncol * bc), dt))
            drow_specs.append(pl.BlockSpec((tr, bc), lambda i, j: (i, j)))
    res = pl.pallas_call(
        body, name=name, grid=(lp // tr, ncol),
        in_specs=[_row_spec(it, tr) for it in rows + nodiff] + [_param_spec(p) for p in params]
        + [_row_spec(it, tr) for it in cts + ex_items],
        out_specs=drow_specs + [_param_spec(p) for p in params],
        out_shape=drow_shapes + [jax.ShapeDtypeStruct(p.shape, f32) for p in params],
        compiler_params=pltpu.CompilerParams(dimension_semantics=("arbitrary", "arbitrary")),
    )(*[it[0] for it in rows + nodiff], *params, *[it[0] for it in cts + ex_items])
    return res[:nr], res[nr:]


def _rms(x, g):
    return x * lax.rsqrt(jnp.mean(x * x, axis=-1, keepdims=True) + NORM_EPS) * g


def _l2n(x):
    return x * lax.rsqrt(jnp.sum(x * x, axis=-1, keepdims=True) + NORM_EPS)


def _sigmoid(x):
    return 1.0 / (1.0 + jnp.exp(-x))


def _silu(x):
    return x * _sigmoid(x)


def _softplus(x):
    return jnp.maximum(x, 0.0) + jnp.log(1.0 + jnp.exp(-jnp.abs(x)))


def _row_ids(shape, tr):
    return pl.program_id(0) * tr + lax.broadcasted_iota(jnp.int32, shape, 0)


def _st_prenorm(r, n, p):
    return [_rms(r[0], p[0])]


def _st_gdn_q(r, n, p):
    return [_l2n(_silu(r[0])) * (GDN_D ** -0.5)]


def _st_gdn_k(r, n, p):
    return [_l2n(_silu(r[0]))]


def _st_gdn_v(r, n, p):
    return [_silu(r[0])]


def _st_gdn_gate(r, n, p):
    real = _row_ids(r[0].shape, ROW_TILE) >= PAD_FRONT
    beta = jnp.where(real, _sigmoid(r[0]), 0.0)
    g = jnp.where(real, -jnp.exp(p[0]) * _softplus(r[1] + p[1]), 0.0)
    return [beta, g]


def _st_gdn_out(r, n, p):
    return [_rms(r[0], p[0]) * _silu(r[1])]


def _st_mid(r, n, p):
    h1 = r[0] + _rms(r[1], p[0])
    return [h1, _rms(h1, p[1]), _rms(h1, p[2])]


def _st_latent(r, n, p):
    ckr, cq = r
    c_kv = _rms(ckr[:, :MLA_KV_RANK], p[0])
    k_rope = ckr[:, 128:256] * n[0] + ckr[:, 384:512] * n[1]
    return [c_kv, k_rope, _rms(cq, p[1])]


def _st_q_rope(r, n, p):
    return [(r[0] * n[0] + r[1] * n[1]) * (MLA_QK ** -0.5)]


def _st_gate(r, n, p):
    return [r[0] * _silu(r[1])]


def _make_st_loss(n_tokens):
    def st(r, n, p):
        h2 = r[0] + _rms(r[1], p[0])
        rows = _row_ids((r[0].shape[0], 1), ROW_TILE)
        real = jnp.logical_and(rows >= ROW0, rows < ROW0 + n_tokens)
        err = h2 - n[0]
        return [jnp.where(real, 0.5 * jnp.mean(err * err, axis=-1, keepdims=True), 0.0)]
    return st


CONV_BC = 512


def _conv_fwd(x, w, *, col_blocks, name, tr=ROW_TILE):
    lp = x.shape[0]

    def body(x_ref, xp_ref, w_ref, o_ref):
        i = pl.program_id(0)
        prev = jnp.where(i > 0, xp_ref[...], 0.0)
        xc = jnp.concatenate([prev, x_ref[...]], axis=0)
        wv = w_ref[...]
        acc = wv[3:4, :] * x_ref[...]
        for j in range(3):
            acc = acc + wv[j:j + 1, :] * pltpu.roll(xc, 3 - j, 0)[8:, :]
        o_ref[...] = acc

    return pl.pallas_call(
        body, name=name, grid=(lp // tr, col_blocks),
        in_specs=[pl.BlockSpec((tr, CONV_BC), lambda i, j: (i, j)),
                  pl.BlockSpec((8, CONV_BC), lambda i, j: (jnp.maximum(i * (tr // 8) - 1, 0), j)),
                  pl.BlockSpec((4, CONV_BC), lambda i, j: (0, j))],
        out_specs=pl.BlockSpec((tr, CONV_BC), lambda i, j: (i, j)),
        out_shape=jax.ShapeDtypeStruct((lp, col_blocks * CONV_BC), f32),
    )(x, x, w)


def _conv_bwd(dc, x, w, *, x_off, w_off, name, tr=ROW_TILE):
    lp, width = dc.shape
    ncb, nrow = width // CONV_BC, lp // tr

    def body(dc_ref, dcn_ref, x_ref, xp_ref, w_ref, dx_ref, dw_ref):
        i = pl.program_id(1)
        nxt = jnp.where(i < nrow - 1, dcn_ref[...], 0.0)
        dcv = dc_ref[...]
        dcc = jnp.concatenate([dcv, nxt], axis=0)
        prev = jnp.where(i > 0, xp_ref[...], 0.0)
        xc = jnp.concatenate([prev, x_ref[...]], axis=0)
        wv = w_ref[...]
        dx = wv[3:4, :] * dcv
        dws = [None] * 4
        dws[3] = jnp.sum(dcv * x_ref[...], axis=0, keepdims=True)
        for j in range(3):
            dx = dx + wv[j:j + 1, :] * pltpu.roll(dcc, tr + 8 - (3 - j), 0)[:tr, :]
            dws[j] = jnp.sum(dcv * pltpu.roll(xc, 3 - j, 0)[8:, :], axis=0, keepdims=True)
        dx_ref[...] = dx.astype(dx_ref.dtype)

        @pl.when(i == 0)
        def _():
            for j in range(4):
                dw_ref[j:j + 1, :] = dws[j]

        @pl.when(i > 0)
        def _():
            for j in range(4):
                dw_ref[j:j + 1, :] += dws[j]

    last8 = lp // 8 - 1
    return pl.pallas_call(
        body, name=name, grid=(ncb, nrow),
        in_specs=[pl.BlockSpec((tr, CONV_BC), lambda j, i: (i, j)),
                  pl.BlockSpec((8, CONV_BC), lambda j, i: (jnp.minimum((i + 1) * (tr // 8), last8), j)),
                  pl.BlockSpec((tr, CONV_BC), lambda j, i: (i, j + x_off)),
                  pl.BlockSpec((8, CONV_BC), lambda j, i: (jnp.maximum(i * (tr // 8) - 1, 0), j + x_off)),
                  pl.BlockSpec((4, CONV_BC), lambda j, i: (0, j + w_off))],
        out_specs=[pl.BlockSpec((tr, CONV_BC), lambda j, i: (i, j)),
                   pl.BlockSpec((4, CONV_BC), lambda j, i: (0, j))],
        out_shape=[jax.ShapeDtypeStruct((lp, width), bf16), jax.ShapeDtypeStruct((4, width), f32)],
        compiler_params=pltpu.CompilerParams(dimension_semantics=("arbitrary", "arbitrary")),
    )(dc, dc, x, x, w)


def _gdn_head(s, q, k, v, beta, gcol, grow):
    c = q.shape[0]
    ii = lax.broadcasted_iota(jnp.int32, (c, c), 0)
    jj = lax.broadcasted_iota(jnp.int32, (c, c), 1)
    dec = jnp.exp(jnp.where(ii >= jj, gcol - grow, NEG))
    dec_strict = jnp.where(ii > jj, dec, 0.0)
    rid = lax.broadcasted_iota(jnp.int32, (c, 1), 0)
    glast = jnp.sum(jnp.where(rid == c - 1, gcol, 0.0), axis=0, keepdims=True)
    eg = jnp.exp(gcol)
    kb = k * beta
    vb = v * beta
    t = _inv_unit_lower(_nt(kb, k) * dec_strict)
    u = _nnh(t, vb)
    w = _nnh(t, kb * eg)
    attn = _nt(q, k) * dec
    v_new = u - _nn(w, s)
    o = _nn(q * eg, s) + _nn(attn, v_new)
    s_new = s * jnp.exp(glast) + _tn(k * jnp.exp(glast - gcol), v_new)
    return o, s_new


def _lane_pick(x, h):
    lane = lax.broadcasted_iota(jnp.int32, x.shape, 1)
    return jnp.sum(jnp.where(lane == h, x, 0.0), axis=1, keepdims=True)


def _cum_log_decay(g):
    c = g.shape[0]
    ii = lax.broadcasted_iota(jnp.int32, (c, c), 0)
    jj = lax.broadcasted_iota(jnp.int32, (c, c), 1)
    lower = (ii >= jj).astype(f32)
    upper = (ii <= jj).astype(f32)
    return _dhi(lower, g, NN), _dhi(g, upper, TN)


def _gdn_fwd(qn, kn, v, beta, g, *, n_real_chunks, name):
    lp = qn.shape[0]
    nchunk = lp // GDN_CHUNK
    C, H, D = GDN_CHUNK, GDN_V_HEADS, GDN_D

    def body(q_ref, k_ref, v_ref, b_ref, g_ref, o_ref, st_ref, s_s, gc_s, gct_s):
        ci = pl.program_id(0)

        @pl.when(ci == 0)
        def _():
            s_s[...] = jnp.zeros_like(s_s)

        @pl.when(ci >= n_real_chunks)
        def _():
            o_ref[...] = jnp.zeros_like(o_ref)
            st_ref[...] = jnp.zeros_like(st_ref)

        @pl.when(ci < n_real_chunks)
        def _():
            gc, gct = _cum_log_decay(g_ref[...])
            gc_s[...] = gc
            gct_s[...] = gct

            def head(h, carry):
                qk_off = pl.multiple_of((h // 2) * D, D)
                v_off = pl.multiple_of(h * D, D)
                s_prev = s_s[h]
                o, s_new = _gdn_head(s_prev, q_ref[:, pl.ds(qk_off, D)], k_ref[:, pl.ds(qk_off, D)],
                                     v_ref[:, pl.ds(v_off, D)], _lane_pick(b_ref[...], h),
                                     _lane_pick(gc_s[...], h), gct_s[pl.ds(h, 1), :])
                st_ref[h] = s_prev
                s_s[h] = s_new
                o_ref[:, pl.ds(v_off, D)] = o
                return carry

            lax.fori_loop(0, H, head, 0)

    return pl.pallas_call(
        body, name=name, grid=(nchunk,),
        in_specs=[pl.BlockSpec((C, GDN_QK_W), lambda c: (c, 0)), pl.BlockSpec((C, GDN_QK_W), lambda c: (c, 0)),
                  pl.BlockSpec((C, GDN_V_W), lambda c: (c, 0)), pl.BlockSpec((C, 128), lambda c: (c, 0)),
                  pl.BlockSpec((C, 128), lambda c: (c, 0))],
        out_specs=[pl.BlockSpec((C, GDN_V_W), lambda c: (c, 0)),
                   pl.BlockSpec((None, H, D, D), lambda c: (c, 0, 0, 0))],
        out_shape=[jax.ShapeDtypeStruct((lp, GDN_V_W), f32), jax.ShapeDtypeStruct((nchunk, H, D, D), f32)],
        scratch_shapes=[pltpu.VMEM((H, D, D), f32), pltpu.VMEM((C, 128), f32), pltpu.VMEM((128, C), f32)],
        compiler_params=pltpu.CompilerParams(dimension_semantics=("arbitrary",)),
    )(qn, kn, v, beta, g)


def _gdn_bwd(qn, kn, v, beta, g, states, do, *, n_real_chunks, name):
    lp = qn.shape[0]
    nchunk = lp // GDN_CHUNK
    C, H, D = GDN_CHUNK, GDN_V_HEADS, GDN_D
    rev = lambda i: (nchunk - 1 - i, 0)

    def body(q_ref, k_ref, v_ref, b_ref, g_ref, st_ref, do_ref,
             dq_ref, dk_ref, dv_ref, db_ref, dg_ref, ds_s, gc_s, gct_s, dgc_s, dgct_s, dbeta_s):
        step = pl.program_id(0)
        ci = nchunk - 1 - step

        @pl.when(step == 0)
        def _():
            ds_s[...] = jnp.zeros_like(ds_s)

        @pl.when(ci >= n_real_chunks)
        def _():
            for r in (dq_ref, dk_ref, dv_ref, db_ref, dg_ref):
                r[...] = jnp.zeros_like(r)

        @pl.when(ci < n_real_chunks)
        def _():
            gc, gct = _cum_log_decay(g_ref[...])
            gc_s[...] = gc
            gct_s[...] = gct
            dgc_s[...] = jnp.zeros_like(dgc_s)
            dgct_s[...] = jnp.zeros_like(dgct_s)
            dbeta_s[...] = jnp.zeros_like(dbeta_s)

            def head(h, carry):
                qk_off = pl.multiple_of((h // 2) * D, D)
                v_off = pl.multiple_of(h * D, D)
                args = (st_ref[h], q_ref[:, pl.ds(qk_off, D)], k_ref[:, pl.ds(qk_off, D)], v_ref[:, pl.ds(v_off, D)],
                        _lane_pick(b_ref[...], h), _lane_pick(gc_s[...], h), gct_s[pl.ds(h, 1), :])
                _, vjp_fn = jax.vjp(_gdn_head, *args)
                dsp, dq, dk, dv, dbcol, dgcol, dgrow = vjp_fn((do_ref[:, pl.ds(v_off, D)], ds_s[h]))
                ds_s[h] = dsp
                dq_ref[:, pl.ds(v_off, D)] = dq
                dk_ref[:, pl.ds(v_off, D)] = dk
                dv_ref[:, pl.ds(v_off, D)] = dv
                sel = lax.broadcasted_iota(jnp.int32, (C, 128), 1) == h
                dbeta_s[...] += jnp.where(sel, dbcol, 0.0)
                dgc_s[...] += jnp.where(sel, dgcol, 0.0)
                dgct_s[pl.ds(h, 1), :] = dgrow
                return carry

            lax.fori_loop(0, H, head, 0)
            eye = (lax.broadcasted_iota(jnp.int32, (128, 128), 0) == lax.broadcasted_iota(jnp.int32, (128, 128), 1)).astype(f32)
            dgc = dgc_s[...] + _dhi(dgct_s[...], eye, TN)
            upper = (lax.broadcasted_iota(jnp.int32, (C, C), 0) <= lax.broadcasted_iota(jnp.int32, (C, C), 1)).astype(f32)
            dg_ref[...] = _dhi(upper, dgc, NN)
            db_ref[...] = dbeta_s[...]

    return pl.pallas_call(
        body, name=name, grid=(nchunk,),
        in_specs=[pl.BlockSpec((C, GDN_QK_W), rev), pl.BlockSpec((C, GDN_QK_W), rev), pl.BlockSpec((C, GDN_V_W), rev),
                  pl.BlockSpec((C, 128), rev), pl.BlockSpec((C, 128), rev),
                  pl.BlockSpec((None, H, D, D), lambda i: (nchunk - 1 - i, 0, 0, 0)), pl.BlockSpec((C, GDN_V_W), rev)],
        out_specs=[pl.BlockSpec((C, GDN_V_W), rev), pl.BlockSpec((C, GDN_V_W), rev), pl.BlockSpec((C, GDN_V_W), rev),
                   pl.BlockSpec((C, 128), rev), pl.BlockSpec((C, 128), rev)],
        out_shape=[jax.ShapeDtypeStruct((lp, GDN_V_W), f32)] * 3 + [jax.ShapeDtypeStruct((lp, 128), f32)] * 2,
        scratch_shapes=[pltpu.VMEM((H, D, D), f32), pltpu.VMEM((C, 128), f32), pltpu.VMEM((128, C), f32),
                        pltpu.VMEM((C, 128), f32), pltpu.VMEM((128, C), f32), pltpu.VMEM((C, 128), f32)],
        compiler_params=pltpu.CompilerParams(dimension_semantics=("arbitrary",)),
    )(qn, kn, v, beta, g, states, do)


def _att_mask(qi, ki, t):
    rows = qi * t + lax.broadcasted_iota(jnp.int32, (t, t), 0)
    cols = ki * t + lax.broadcasted_iota(jnp.int32, (t, t), 1)
    return jnp.logical_and(rows >= cols, cols >= PAD_FRONT)


def _attention_fwd(q, kvu, kr, *, name, t=ATT_TILE):
    lp = q.shape[0]
    H, nb = MLA_HEADS, lp // t

    def body(q_ref, kn_ref, kr_ref, v_ref, o_ref, lse_ref, m_s, l_s, acc_s):
        qi, ki = pl.program_id(1), pl.program_id(2)

        @pl.when(ki == 0)
        def _():
            m_s[...] = jnp.full_like(m_s, NEG)
            l_s[...] = jnp.zeros_like(l_s)
            acc_s[...] = jnp.zeros_like(acc_s)

        @pl.when(ki <= qi)
        def _():
            k = jnp.concatenate([kn_ref[...], kr_ref[...]], axis=1)
            s = lax.dot_general(q_ref[...], k, (NT, ((), ())), preferred_element_type=f32)
            s = jnp.where(_att_mask(qi, ki, t), s, NEG)
            m_prev = m_s[...]
            m_new = jnp.maximum(m_prev, jnp.max(s, axis=1, keepdims=True))
            alpha = jnp.exp(m_prev - m_new)
            p = jnp.exp(s - m_new)
            l_s[...] = alpha * l_s[...] + jnp.sum(p, axis=1, keepdims=True)
            acc_s[...] = alpha * acc_s[...] + jnp.dot(p.astype(bf16), v_ref[...], preferred_element_type=f32)
            m_s[...] = m_new

        @pl.when(ki == qi)
        def _():
            o_ref[...] = acc_s[...] / l_s[...]
            lse_ref[...] = jnp.broadcast_to(m_s[...] + jnp.log(l_s[...]), lse_ref.shape)

    kmap = lambda h, qi, ki: (jnp.minimum(ki, qi), h)
    return pl.pallas_call(
        body, name=name, grid=(H, nb, nb),
        in_specs=[pl.BlockSpec((t, MLA_QKP), lambda h, qi, ki: (qi, h)),
                  pl.BlockSpec((t, 128), kmap),
                  pl.BlockSpec((t, 128), lambda h, qi, ki: (jnp.minimum(ki, qi), 0)),
                  pl.BlockSpec((t, 128), lambda h, qi, ki: (jnp.minimum(ki, qi), H + h))],
        out_specs=[pl.BlockSpec((t, 128), lambda h, qi, ki: (qi, h)),
                   pl.BlockSpec((None, t, 128), lambda h, qi, ki: (h, qi, 0))],
        out_shape=[jax.ShapeDtypeStruct((lp, MLA_V_W), f32), jax.ShapeDtypeStruct((H, lp, 128), f32)],
        scratch_shapes=[pltpu.VMEM((t, 1), f32), pltpu.VMEM((t, 1), f32), pltpu.VMEM((t, 128), f32)],
        compiler_params=pltpu.CompilerParams(dimension_semantics=("parallel", "parallel", "arbitrary")),
    )(q, kvu, kr, kvu)


def _attention_bwd(q, kvu, kr, o, lse, do, *, name, t=ATT_TILE):
    lp = q.shape[0]
    H, nb = MLA_HEADS, lp // t

    def body(q_ref, kn_ref, kr_ref, v_ref, o_ref, lse_ref, do_ref, dq_ref, dkn_ref, dv_ref, dkr_ref, dk_s, dv_s):
        ki, qi = pl.program_id(1), pl.program_id(2)

        @pl.when(qi == 0)
        def _():
            dk_s[...] = jnp.zeros_like(dk_s)
            dv_s[...] = jnp.zeros_like(dv_s)

        @pl.when(qi >= ki)
        def _():
            qv = q_ref[...]
            k = jnp.concatenate([kn_ref[...], kr_ref[...]], axis=1)
            s = lax.dot_general(qv, k, (NT, ((), ())), preferred_element_type=f32)
            s = jnp.where(_att_mask(qi, ki, t), s, NEG)
            p = jnp.exp(s - lse_ref[:, 0:1])
            dov = do_ref[...]
            dob = dov.astype(bf16)
            dp = lax.dot_general(dob, v_ref[...], (NT, ((), ())), preferred_element_type=f32)
            delta = jnp.sum(dov * o_ref[...], axis=1, keepdims=True)
            ds = (p * (dp - delta)).astype(bf16)
            dv_s[...] += lax.dot_general(p.astype(bf16), dob, (TN, ((), ())), preferred_element_type=f32)
            dk_s[...] += lax.dot_general(ds, qv, (TN, ((), ())), preferred_element_type=f32)
            dq = jnp.dot(ds, k, preferred_element_type=f32)
            rows = pl.ds(pl.multiple_of(qi * t, t), t)

            @pl.when(ki == 0)
            def _():
                dq_ref[rows, :] = dq

            @pl.when(ki > 0)
            def _():
                dq_ref[rows, :] += dq

        @pl.when(qi == nb - 1)
        def _():
            dkn_ref[...] = dk_s[:, :128].astype(dkn_ref.dtype)
            dkr_ref[...] = dk_s[:, 128:]
            dv_ref[...] = dv_s[...].astype(dv_ref.dtype)

    qmap = lambda h, ki, qi: (jnp.maximum(qi, ki), h)
    res = pl.pallas_call(
        body, name=name, grid=(H, nb, nb),
        in_specs=[pl.BlockSpec((t, MLA_QKP), qmap),
                  pl.BlockSpec((t, 128), lambda h, ki, qi: (ki, h)),
                  pl.BlockSpec((t, 128), lambda h, ki, qi: (ki, 0)),
                  pl.BlockSpec((t, 128), lambda h, ki, qi: (ki, H + h)),
                  pl.BlockSpec((t, 128), qmap),
                  pl.BlockSpec((None, t, 128), lambda h, ki, qi: (h, jnp.maximum(qi, ki), 0)),
                  pl.BlockSpec((t, 128), qmap)],
        out_specs=[pl.BlockSpec((lp, MLA_QKP), lambda h, ki, qi: (0, h)),
                   pl.BlockSpec((t, 128), lambda h, ki, qi: (ki, h)),
                   pl.BlockSpec((t, 128), lambda h, ki, qi: (ki, h)),
                   pl.BlockSpec((t, 128), lambda h, ki, qi: (ki, h))],
        out_shape=[jax.ShapeDtypeStruct((lp, H * MLA_QKP), f32), jax.ShapeDtypeStruct((lp, MLA_V_W), bf16),
                   jax.ShapeDtypeStruct((lp, MLA_V_W), bf16), jax.ShapeDtypeStruct((lp, MLA_V_W), f32)],
        scratch_shapes=[pltpu.VMEM((t, MLA_QKP), f32), pltpu.VMEM((t, 128), f32)],
        compiler_params=pltpu.CompilerParams(dimension_semantics=("arbitrary", "arbitrary", "arbitrary")),
    )(q, kvu, kr, kvu, o, lse, do)
    return res


def _exchange(x, *, gather, name):
    blk = x.shape if gather else x.shape[1:]

    def body(x_ref, o_ref, send_sems, recv_sems, local_sem):
        mx, my, mc = lax.axis_index("x"), lax.axis_index("y"), lax.axis_index("c")
        me = 4 * mx + 2 * my + mc
        own = pltpu.make_async_copy(x_ref if gather else x_ref.at[me], o_ref.at[me], local_sem)
        own.start()
        sends, peers = [], []
        for k in range(1, N_DEV):
            px = 1 - mx if k & 4 else mx
            py = 1 - my if k & 2 else my
            pc = 1 - mc if k & 1 else mc
            peer = 4 * px + 2 * py + pc
            cp = pltpu.make_async_remote_copy(
                src_ref=x_ref if gather else x_ref.at[peer], dst_ref=o_ref.at[me],
                send_sem=send_sems.at[k - 1], recv_sem=recv_sems.at[k - 1],
                device_id=(px, py, pc), device_id_type=MESH_ID)
            cp.start()
            sends.append(cp)
            peers.append(peer)
        for k in range(1, N_DEV):
            pltpu.make_async_remote_copy(
                src_ref=o_ref.at[peers[k - 1]], dst_ref=o_ref.at[peers[k - 1]],
                send_sem=send_sems.at[k - 1], recv_sem=recv_sems.at[k - 1],
                device_id=(mx, my, mc), device_id_type=MESH_ID).wait_recv()
        for cp in sends:
            cp.wait_send()
        own.wait()

    return pl.pallas_call(
        body, name=name,
        in_specs=[pl.BlockSpec(memory_space=pltpu.HBM)], out_specs=pl.BlockSpec(memory_space=pltpu.HBM),
        out_shape=jax.ShapeDtypeStruct((N_DEV,) + tuple(blk), x.dtype),
        scratch_shapes=[pltpu.SemaphoreType.DMA((N_DEV - 1,)), pltpu.SemaphoreType.DMA((N_DEV - 1,)),
                        pltpu.SemaphoreType.DMA],
    )(x)


def _reduce_adamw(parts, w, m, v, *, name):
    r = w.shape[0]
    tr = _pick(r, (1280, 1024, 512, 256, 128, 64, 48, 32, 16, 8))

    def body(p_ref, w_ref, m_ref, v_ref, g_ref, d_ref, nm_ref, nv_ref):
        g = p_ref[0]
        for s in range(1, N_DEV):
            g = g + p_ref[s]
        mm = ADAM_B1 * m_ref[...] + (1.0 - ADAM_B1) * g
        vv = ADAM_B2 * v_ref[...] + (1.0 - ADAM_B2) * (g * g)
        m_hat = mm / (1.0 - ADAM_B1 ** ADAM_STEP)
        v_hat = vv / (1.0 - ADAM_B2 ** ADAM_STEP)
        g_ref[...] = g
        d_ref[...] = -ADAM_LR * (m_hat / (jnp.sqrt(v_hat) + ADAM_EPS) + ADAM_WD * w_ref[...])
        nm_ref[...] = mm
        nv_ref[...] = vv

    spec = pl.BlockSpec((tr, 128), lambda i: (i, 0))
    return pl.pallas_call(
        body, name=name, grid=(r // tr,),
        in_specs=[pl.BlockSpec((N_DEV, tr, 128), lambda i: (0, i, 0)), spec, spec, spec],
        out_specs=[spec] * 4, out_shape=[jax.ShapeDtypeStruct((r, 128), f32)] * 4,
    )(parts, w, m, v)


_SHARDED = ("gdn_w_in", "gdn_w_out", "kv_w_down", "kv_w_up", "mla_w_in", "mla_w_q_up", "mla_w_out", "meta_tokens", "gdn_conv_w")
_COL_SHARDED = {"gdn_w_in", "kv_w_up", "mla_w_in", "mla_w_q_up", "meta_tokens", "gdn_conv_w"}
_N_BF16 = 7
_REPLICATED = ("pre_norm", "post_norm", "gdn_a_log", "gdn_dt_bias", "gdn_out_norm", "kv_norm", "kv_latent_norm",
               "mla_q_latent_norm")


def _rows128(a):
    flat = a.reshape(-1)
    pad = (-flat.shape[0]) % 128
    if pad:
        flat = jnp.pad(flat, (0, pad))
    return flat.reshape(-1, 128)


def _pack(arrs, row_multiple):
    parts = [_rows128(a) for a in arrs]
    buf = jnp.concatenate(parts, axis=0)
    pad = (-buf.shape[0]) % row_multiple
    if pad:
        buf = jnp.pad(buf, ((0, pad), (0, 0)))
    return buf


def _unpack(buf, shapes):
    out, r = [], 0
    for shp in shapes:
        n = math.prod(shp)
        rows = -(-n // 128)
        out.append(buf[r:r + rows].reshape(-1)[:n].reshape(shp))
        r += rows
    return out


def _unshard(g, full_shape, col):
    if col:
        return jnp.transpose(g, (1, 0, 2)).reshape(full_shape)
    return g.reshape(full_shape)


def _to_shards(a, col):
    r, c = a.shape
    if col:
        return jnp.transpose(a.reshape(r, N_DEV, c // N_DEV), (1, 0, 2))
    return a.reshape(N_DEV, r // N_DEV, c)


def _pad_cols(a, width):
    return jnp.pad(a, ((0, 0), (0, width - a.shape[1])))


def _rope_tables(lp):
    inv = ROPE_THETA ** (-jnp.arange(0, MLA_ROPE, 2, dtype=f32) / MLA_ROPE)
    pos = (jnp.arange(lp, dtype=jnp.int32) - PAD_FRONT).astype(f32)
    ang = pos[:, None] * inv[None, :]
    cos, sin = jnp.cos(ang), jnp.sin(ang)
    z = jnp.zeros((lp, 64), f32)
    return jnp.concatenate([cos, cos, z], axis=1), jnp.concatenate([-sin, sin, z], axis=1)


def kernel(x, meta_tokens, pre_norm, post_norm, gdn_w_in, gdn_conv_w, gdn_a_log, gdn_dt_bias, gdn_out_norm, gdn_w_out, kv_norm, kv_w_down, kv_latent_norm, kv_w_up, mla_w_in, mla_q_latent_norm, mla_w_q_up, mla_w_out, loss_target, m_meta_tokens, m_pre_norm, m_post_norm, m_gdn_w_in, m_gdn_conv_w, m_gdn_a_log, m_gdn_dt_bias, m_gdn_out_norm, m_gdn_w_out, m_kv_norm, m_kv_w_down, m_kv_latent_norm, m_kv_w_up, m_mla_w_in, m_mla_q_latent_norm, m_mla_w_q_up, m_mla_w_out, v_meta_tokens, v_pre_norm, v_post_norm, v_gdn_w_in, v_gdn_conv_w, v_gdn_a_log, v_gdn_dt_bias, v_gdn_out_norm, v_gdn_w_out, v_kv_norm, v_kv_w_down, v_kv_latent_norm, v_kv_w_up, v_mla_w_in, v_mla_q_latent_norm, v_mla_w_q_up, v_mla_w_out):
    W = dict(meta_tokens=meta_tokens, pre_norm=pre_norm, post_norm=post_norm, gdn_w_in=gdn_w_in, gdn_conv_w=gdn_conv_w,
             gdn_a_log=gdn_a_log, gdn_dt_bias=gdn_dt_bias, gdn_out_norm=gdn_out_norm, gdn_w_out=gdn_w_out, kv_norm=kv_norm,
             kv_w_down=kv_w_down, kv_latent_norm=kv_latent_norm, kv_w_up=kv_w_up, mla_w_in=mla_w_in,
             mla_q_latent_norm=mla_q_latent_norm, mla_w_q_up=mla_w_q_up, mla_w_out=mla_w_out)
    M = dict(meta_tokens=m_meta_tokens, pre_norm=m_pre_norm, post_norm=m_post_norm, gdn_w_in=m_gdn_w_in, gdn_conv_w=m_gdn_conv_w,
             gdn_a_log=m_gdn_a_log, gdn_dt_bias=m_gdn_dt_bias, gdn_out_norm=m_gdn_out_norm, gdn_w_out=m_gdn_w_out, kv_norm=m_kv_norm,
             kv_w_down=m_kv_w_down, kv_latent_norm=m_kv_latent_norm, kv_w_up=m_kv_w_up, mla_w_in=m_mla_w_in,
             mla_q_latent_norm=m_mla_q_latent_norm, mla_w_q_up=m_mla_w_q_up, mla_w_out=m_mla_w_out)
    V = dict(meta_tokens=v_meta_tokens, pre_norm=v_pre_norm, post_norm=v_post_norm, gdn_w_in=v_gdn_w_in, gdn_conv_w=v_gdn_conv_w,
             gdn_a_log=v_gdn_a_log, gdn_dt_bias=v_gdn_dt_bias, gdn_out_norm=v_gdn_out_norm, gdn_w_out=v_gdn_w_out, kv_norm=v_kv_norm,
             kv_w_down=v_kv_w_down, kv_latent_norm=v_kv_latent_norm, kv_w_up=v_kv_w_up, mla_w_in=v_mla_w_in,
             mla_q_latent_norm=v_mla_q_latent_norm, mla_w_q_up=v_mla_w_q_up, mla_w_out=v_mla_w_out)
    order = list(W)

    n_tok = x.shape[1]
    assert n_tok % GDN_CHUNK == 0
    n_real = ROW0 + n_tok
    lp = -(-n_real // ROW_TILE) * ROW_TILE
    n_real_chunks = n_real // GDN_CHUNK

    shard2d = {n: W[n].reshape(W[n].shape[-2:]) for n in _SHARDED}
    full_shape = {n: ((s.shape[0], s.shape[1] * N_DEV) if n in _COL_SHARDED else (s.shape[0] * N_DEV, s.shape[1]))
                  for n, s in shard2d.items()}
    big, small = _SHARDED[:_N_BF16], _SHARDED[_N_BF16:]
    g_big = _exchange(_pack([shard2d[n].astype(bf16) for n in big], 16), gather=True, name="gather_weights")
    g_small = _exchange(_pack([shard2d[n] for n in small], 8), gather=True, name="gather_meta_conv")
    full = {}
    for names, buf in ((big, g_big), (small, g_small)):
        r = 0
        for n in names:
            shp = shard2d[n].shape
            rows = math.prod(shp) // 128
            blocks = buf[:, r:r + rows].reshape((N_DEV,) + shp)
            full[n] = _unshard(blocks, full_shape[n], n in _COL_SHARDED)
            r += rows

    w_in = full["gdn_w_in"]
    s1 = GDN_CONV_W + GDN_V_W
    w_in_p = jnp.concatenate([w_in[:, :s1], _pad_cols(w_in[:, s1:s1 + 16], 128), _pad_cols(w_in[:, s1 + 16:], 128)], axis=1)
    wd = full["kv_w_down"]
    zc = jnp.zeros((D_MODEL, 64), bf16)
    wd2 = jnp.concatenate([wd, zc, jnp.zeros((D_MODEL, 128), bf16), wd[:, 160:192], wd[:, 128:160], zc], axis=1)
    wup_p = jnp.transpose(full["kv_w_up"].reshape(MLA_KV_RANK, MLA_HEADS, 2, 128), (0, 2, 1, 3)).reshape(MLA_KV_RANK, 2 * MLA_V_W)
    wq = full["mla_w_q_up"].reshape(MLA_Q_RANK, MLA_HEADS, MLA_QK)
    zq64 = jnp.zeros((MLA_Q_RANK, MLA_HEADS, 64), bf16)
    wq_plain = jnp.concatenate([wq, zq64], axis=2).reshape(MLA_Q_RANK, MLA_HEADS * MLA_QKP)
    wq_swap = jnp.concatenate([jnp.zeros((MLA_Q_RANK, MLA_HEADS, 128), bf16), wq[:, :, 160:192], wq[:, :, 128:160], zq64],
                              axis=2).reshape(MLA_Q_RANK, MLA_HEADS * MLA_QKP)
    wq2 = jnp.concatenate([wq_plain, wq_swap], axis=1)
    w_mla_in, w_gdn_out, w_mla_out = full["mla_w_in"], full["gdn_w_out"], full["mla_w_out"]
    conv_w = full["gdn_conv_w"]
    pre0, pre1 = pre_norm[0:1], pre_norm[1:2]
    post0, post1 = post_norm[0:1], post_norm[1:2]
    alog_p, dtb_p = _pad_cols(gdn_a_log, 128), _pad_cols(gdn_dt_bias, 128)
    kvn, kvln = kv_norm.reshape(1, -1), kv_latent_norm.reshape(1, -1)

    h0 = jnp.concatenate([jnp.zeros((PAD_FRONT, D_MODEL), f32), full["meta_tokens"], x[0],
                          jnp.zeros((lp - n_real, D_MODEL), f32)], axis=0)
    tgt = jnp.concatenate([jnp.zeros((ROW0, D_MODEL), f32), loss_target[0], jnp.zeros((lp - n_real, D_MODEL), f32)], axis=0)
    cos_k, sin_k = _rope_tables(lp)
    one = jnp.ones((lp, 128), f32)
    cos_q = jnp.concatenate([one, cos_k], axis=1)
    sin_q = jnp.concatenate([jnp.zeros((lp, 128), f32), sin_k], axis=1)

    (hn0,) = _rowwise(_st_prenorm, [(h0, None, 0)], [], [pre0], [(D_MODEL, bf16, None)], name="f_prenorm0")
    proj = _matmul(hn0, w_in_p, name="f_gdn_in")
    conv = _conv_fwd(proj, conv_w, col_blocks=GDN_CONV_W // CONV_BC, name="f_conv")
    (qn,) = _rowwise(_st_gdn_q, [(conv, 128, 0)], [], [], [(GDN_QK_W, f32, 128)], ncol=GDN_QK_HEADS, name="f_gdn_q")
    (kn,) = _rowwise(_st_gdn_k, [(conv, 128, GDN_QK_HEADS)], [], [], [(GDN_QK_W, f32, 128)], ncol=GDN_QK_HEADS, name="f_gdn_k")
    (vv,) = _rowwise(_st_gdn_v, [(conv, 128, 2 * GDN_QK_HEADS)], [], [], [(GDN_V_W, f32, 128)], ncol=GDN_V_HEADS, name="f_gdn_v")
    gate_rows = [(proj, 128, s1 // 128), (proj, 128, s1 // 128 + 1)]
    beta, gdec = _rowwise(_st_gdn_gate, gate_rows, [], [alog_p, dtb_p], [(128, f32, None)] * 2, name="f_gdn_gate")
    o_gdn, states = _gdn_fwd(qn, kn, vv, beta, gdec, n_real_chunks=n_real_chunks, name="f_gdn")
    out_rows = [(o_gdn, 128, 0), (proj, 128, GDN_CONV_W // 128)]
    (og,) = _rowwise(_st_gdn_out, out_rows, [], [gdn_out_norm], [(GDN_V_W, bf16, 128)], ncol=GDN_V_HEADS, name="f_gdn_out")
    y0 = _matmul(og, w_gdn_out, name="f_gdn_wout")
    mid_rows = [(h0, None, 0), (y0, None, 0)]
    h1, hn1, hkv = _rowwise(_st_mid, mid_rows, [], [post0, pre1, kvn],
                            [(D_MODEL, f32, None), (D_MODEL, bf16, None), (D_MODEL, bf16, None)], name="f_mid")
    ckr = _matmul(hkv, wd2, name="f_kv_down")
    proj2 = _matmul(hn1, w_mla_in, name="f_mla_in")
    lat_rows = [(ckr, None, 0), (proj2, MLA_Q_RANK, 0)]
    lat_nd = [(cos_k, None, 0), (sin_k, None, 0)]
    c_kv, k_rope, c_q = _rowwise(_st_latent, lat_rows, lat_nd, [kvln, mla_q_latent_norm],
                                 [(128, bf16, None), (128, bf16, None), (MLA_Q_RANK, bf16, None)], name="f_latent")
    kvu = _matmul(c_kv, wup_p, out_dtype=bf16, name="f_kv_up")
    qq = _matmul(c_q, wq2, name="f_q_up")
    q_rows = [(qq, MLA_QKP, 0), (qq, MLA_QKP, MLA_HEADS)]
    q_nd = [(cos_q, None, 0), (sin_q, None, 0)]
    (q_att,) = _rowwise(_st_q_rope, q_rows, q_nd, [], [(MLA_HEADS * MLA_QKP, bf16, MLA_QKP)], ncol=MLA_HEADS, name="f_q_rope")
    o_att, lse = _attention_fwd(q_att, kvu, k_rope, name="f_attention")
    gate2_rows = [(o_att, 128, 0), (proj2, 128, MLA_Q_RANK // 128)]
    (og2,) = _rowwise(_st_gate, gate2_rows, [], [], [(MLA_V_W, bf16, 128)], ncol=MLA_HEADS, name="f_mla_gate")
    y1 = _matmul(og2, w_mla_out, name="f_mla_wout")
    st_loss = _make_st_loss(n_tok)
    loss_rows_in = [(h1, None, 0), (y1, None, 0)]
    (loss_rows,) = _rowwise(st_loss, loss_rows_in, [(tgt, None, 0)], [post1], [(1, f32, None)], name="f_loss")
    loss = lax.psum(jnp.sum(loss_rows), ("x", "y", "c"))

    ones_ct = jnp.ones((lp, 1), f32)
    (dh1_a, dy1), (dpost1,) = _rowwise_vjp(st_loss, loss_rows_in, [(tgt, None, 0)], [post1], [(ones_ct, None, 0)], name="b_loss")
    dog2 = _matmul(dy1, w_mla_out, tb=True, name="b_mla_wout_x")
    dw_mla_out = _matmul(og2, dy1, ta=True, name="b_mla_wout_w")
    (do_att, dz2), _ = _rowwise_vjp(_st_gate, gate2_rows, [], [], [(dog2, 128, 0)], ncol=MLA_HEADS, name="b_mla_gate",
                                    grad_dtypes=[f32, bf16])
    dq_att, dkn, dvv, dkr_h = _attention_bwd(q_att, kvu, k_rope, o_att, lse, do_att, name="b_attention")
    (dqa, dqb), _ = _rowwise_vjp(_st_q_rope, q_rows, q_nd, [], [(dq_att, MLA_QKP, 0)], ncol=MLA_HEADS, name="b_q_rope",
                                 grad_dtypes=[bf16, bf16])
    dqq = jnp.concatenate([dqa, dqb], axis=1)
    dc_q = _matmul(dqq, wq2, tb=True, name="b_q_up_x")
    dwq2 = _matmul(c_q, dqq, ta=True, name="b_q_up_w")
    dkvu = jnp.concatenate([dkn, dvv], axis=1)
    dc_kv = _matmul(dkvu, wup_p, tb=True, name="b_kv_up_x")
    dwup_p = _matmul(c_kv, dkvu, ta=True, name="b_kv_up_w")

    def lat_ct(cv):
        dkr = cv[1][:, 0:128]
        for h in range(1, MLA_HEADS):
            dkr = dkr + cv[1][:, h * 128:(h + 1) * 128]
        return [cv[0], dkr, cv[2]]

    (dckr, dcq_pre), (dkvln, dqln) = _rowwise_vjp(
        _st_latent, lat_rows, lat_nd, [kvln, mla_q_latent_norm],
        [(dc_kv, None, 0), (dkr_h, None, 0), (dc_q, None, 0)], ct_pre=lat_ct, name="b_latent", grad_dtypes=[bf16, bf16])
    dproj2 = jnp.concatenate([dcq_pre, dz2], axis=1)
    dhn1 = _matmul(dproj2, w_mla_in, tb=True, name="b_mla_in_x")
    dw_mla_in = _matmul(hn1, dproj2, ta=True, name="b_mla_in_w")
    dhkv = _matmul(dckr, wd2, tb=True, name="b_kv_down_x")
    dwd2 = _matmul(hkv, dckr, ta=True, name="b_kv_down_w")
    (dh0_a, dy0), (dpost0, dpre1, dkvn) = _rowwise_vjp(
        _st_mid, mid_rows, [], [post0, pre1, kvn], [(dh1_a, None, 0), (dhn1, None, 0), (dhkv, None, 0)], name="b_mid")
    dog = _matmul(dy0, w_gdn_out, tb=True, name="b_gdn_wout_x")
    dw_gdn_out = _matmul(og, dy0, ta=True, name="b_gdn_wout_w")
    (do_gdn, dz), (dout_norm,) = _rowwise_vjp(_st_gdn_out, out_rows, [], [gdn_out_norm], [(dog, 128, 0)], ncol=GDN_V_HEADS,
                                              name="b_gdn_out", grad_dtypes=[f32, bf16])
    dq16, dk16, dv_g, dbeta, dgdec = _gdn_bwd(qn, kn, vv, beta, gdec, states, do_gdn, n_real_chunks=n_real_chunks, name="b_gdn")
    (db_col, da_col), (dalog_p, ddtb_p) = _rowwise_vjp(
        _st_gdn_gate, gate_rows, [], [alog_p, dtb_p], [(dbeta, None, 0), (dgdec, None, 0)], name="b_gdn_gate",
        grad_dtypes=[bf16, bf16])
    pair = lambda cv: [cv[0][:, :128] + cv[0][:, 128:]]
    (dconv_q,), _ = _rowwise_vjp(_st_gdn_q, [(conv, 128, 0)], [], [], [(dq16, 256, 0)], ncol=GDN_QK_HEADS, ct_pre=pair, name="b_gdn_q")
    (dconv_k,), _ = _rowwise_vjp(_st_gdn_k, [(conv, 128, GDN_QK_HEADS)], [], [], [(dk16, 256, 0)], ncol=GDN_QK_HEADS, ct_pre=pair,
                                 name="b_gdn_k")
    (dconv_v,), _ = _rowwise_vjp(_st_gdn_v, [(conv, 128, 2 * GDN_QK_HEADS)], [], [], [(dv_g, 128, 0)], ncol=GDN_V_HEADS, name="b_gdn_v")
    nq_b = GDN_QK_W // CONV_BC
    dpre_q, dcw_q = _conv_bwd(dconv_q, proj, conv_w, x_off=0, w_off=0, name="b_conv_q")
    dpre_k, dcw_k = _conv_bwd(dconv_k, proj, conv_w, x_off=nq_b, w_off=nq_b, name="b_conv_k")
    dpre_v, dcw_v = _conv_bwd(dconv_v, proj, conv_w, x_off=2 * nq_b, w_off=2 * nq_b, name="b_conv_v")
    dproj = jnp.concatenate([dpre_q, dpre_k, dpre_v, dz, db_col, da_col], axis=1)
    dhn0 = _matmul(dproj, w_in_p, tb=True, name="b_gdn_in_x")
    dw_in_p = _matmul(hn0, dproj, ta=True, name="b_gdn_in_w")
    (dh0,), (dpre0,) = _rowwise_vjp(_st_prenorm, [(h0, None, 0)], [], [pre0], [(dhn0, None, 0)], extra=[dh0_a], name="b_prenorm0")

    grad_x = dh0[ROW0:n_real][None]
    G = {}
    G["meta_tokens"] = dh0[PAD_FRONT:ROW0]
    G["gdn_w_in"] = jnp.concatenate([dw_in_p[:, :s1 + 16], dw_in_p[:, s1 + 128:s1 + 144]], axis=1)
    G["gdn_conv_w"] = jnp.concatenate([dcw_q, dcw_k, dcw_v], axis=1)
    G["gdn_w_out"] = dw_gdn_out
    G["kv_w_down"] = jnp.concatenate([dwd2[:, :128], dwd2[:, 128:160] + dwd2[:, 416:448], dwd2[:, 160:192] + dwd2[:, 384:416]], axis=1)
    G["kv_w_up"] = jnp.transpose(dwup_p.reshape(MLA_KV_RANK, 2, MLA_HEADS, 128), (0, 2, 1, 3)).reshape(MLA_KV_RANK, 2 * MLA_V_W)
    G["mla_w_in"] = dw_mla_in
    dqp = dwq2[:, :MLA_HEADS * MLA_QKP].reshape(MLA_Q_RANK, MLA_HEADS, MLA_QKP)
    dqs = dwq2[:, MLA_HEADS * MLA_QKP:].reshape(MLA_Q_RANK, MLA_HEADS, MLA_QKP)
    G["mla_w_q_up"] = jnp.concatenate([dqp[:, :, :128], dqp[:, :, 128:160] + dqs[:, :, 160:192],
                                       dqp[:, :, 160:192] + dqs[:, :, 128:160]], axis=2).reshape(MLA_Q_RANK, MLA_HEADS * MLA_QK)
    G["mla_w_out"] = dw_mla_out
    G["pre_norm"] = jnp.concatenate([dpre0, dpre1], axis=0)
    G["post_norm"] = jnp.concatenate([dpost0, dpost1], axis=0)
    G["gdn_a_log"] = dalog_p[:, :GDN_V_HEADS]
    G["gdn_dt_bias"] = ddtb_p[:, :GDN_V_HEADS]
    G["gdn_out_norm"] = dout_norm
    G["kv_norm"] = dkvn.reshape(-1)
    G["kv_latent_norm"] = dkvln.reshape(-1)
    G["mla_q_latent_norm"] = dqln

    send = jnp.concatenate([_to_shards(G[n], n in _COL_SHARDED).reshape(N_DEV, -1, 128) for n in _SHARDED], axis=1)
    parts = _exchange(send, gather=False, name="scatter_grads")
    w_s, m_s, v_s = (_pack([d[n] for n in _SHARDED], 8) for d in (W, M, V))
    res_s = _reduce_adamw(parts, w_s, m_s, v_s, name="adamw_sharded")
    small_send = _pack([G[n] for n in _REPLICATED], 8)
    parts_r = _exchange(small_send, gather=True, name="gather_small_grads")
    w_r, m_r, v_r = (_pack([d[n] for n in _REPLICATED], 8) for d in (W, M, V))
    res_r = _reduce_adamw(parts_r, w_r, m_r, v_r, name="adamw_replicated")

    outs = {}
    for kind, bs, br in zip(("grad", "delta", "new_m", "new_v"), res_s, res_r):
        for n, a in zip(_SHARDED, _unpack(bs, [W[n].shape for n in _SHARDED])):
            outs[kind, n] = a
        for n, a in zip(_REPLICATED, _unpack(br, [W[n].shape for n in _REPLICATED])):
            outs[kind, n] = a
    return (loss, grad_x, *[outs[k, n] for k in ("grad", "delta", "new_m", "new_v") for n in order])
```

```python
import functools
import math

import jax
import jax.numpy as jnp
from jax import lax
from jax.experimental import pallas as pl
from jax.experimental.pallas import tpu as pltpu

f32, bf16 = jnp.float32, jnp.bfloat16
HIGHEST = lax.Precision.HIGHEST
MESH_ID = pl.DeviceIdType.MESH

N_DEV = 8
D_MODEL = 1024
N_META = 16
NORM_EPS = 1e-6
PAD_FRONT = 48
ROW0 = PAD_FRONT + N_META
GDN_QK_HEADS, GDN_V_HEADS, GDN_D = 8, 16, 128
GDN_CHUNK = 64
GDN_FWD_UNROLL, GDN_BWD_UNROLL = 4, 2
GDN_QK_W, GDN_V_W = GDN_QK_HEADS * GDN_D, GDN_V_HEADS * GDN_D
GDN_CONV_W = 2 * GDN_QK_W + GDN_V_W
GDN_IN_W = GDN_CONV_W + GDN_V_W + 2 * GDN_V_HEADS
GDN_IN_WP = GDN_CONV_W + GDN_V_W + 2 * 128
MLA_HEADS, MLA_NOPE, MLA_ROPE, MLA_V = 16, 128, 64, 128
MLA_Q_RANK, MLA_KV_RANK = 256, 128
MLA_QK = MLA_NOPE + MLA_ROPE
MLA_QKP = 256
MLA_V_W = MLA_HEADS * MLA_V
ROPE_THETA = 10000.0
NEG = -1e30
ROW_TILE = 256
ATT_TILE = 256

ADAM_LR, ADAM_B1, ADAM_B2, ADAM_EPS, ADAM_WD, ADAM_STEP = 0.001, 0.9, 0.999, 1e-08, 0.01, 10

NN = ((1,), (0,))
NT = ((1,), (1,))
TN = ((0,), (0,))


def _pick(dim, prefs):
    for p in prefs:
        if dim % p == 0:
            return p
    return dim


def _make_dots(full_precision):
    def d(a, b, dims):
        if full_precision:
            return lax.dot_general(a, b, (dims, ((), ())), precision=HIGHEST, preferred_element_type=f32)
        return lax.dot_general(a.astype(bf16), b.astype(bf16), (dims, ((), ())), preferred_element_type=f32)

    @jax.custom_vjp
    def nn(a, b):
        return d(a, b, NN)
    nn.defvjp(lambda a, b: (d(a, b, NN), (a, b)), lambda r, ct: (d(ct, r[1], NT), d(r[0], ct, TN)))

    @jax.custom_vjp
    def nt(a, b):
        return d(a, b, NT)
    nt.defvjp(lambda a, b: (d(a, b, NT), (a, b)), lambda r, ct: (d(ct, r[1], NN), d(ct, r[0], TN)))

    @jax.custom_vjp
    def tn(a, b):
        return d(a, b, TN)
    tn.defvjp(lambda a, b: (d(a, b, TN), (a, b)), lambda r, ct: (d(r[1], ct, NT), d(r[0], ct, NN)))
    return d, nn, nt, tn


_dlo, _nn, _nt, _tn = _make_dots(False)
_dhi, _nnh, _nth, _tnh = _make_dots(True)


def _split(x):
    hi = x.astype(bf16)
    return hi, (x - hi.astype(f32)).astype(bf16)


def _dot_split(a, b, dims):
    (ah, al), (bh, bl) = a, b
    d = lambda p, q: lax.dot_general(p, q, (dims, ((), ())), preferred_element_type=f32)
    return d(ah, bh) + (d(ah, bl) + d(al, bh))


def _make_dots3():
    d = lambda a, b, dims: _dot_split(_split(a), _split(b), dims)

    @jax.custom_vjp
    def nn(a, b):
        return d(a, b, NN)
    nn.defvjp(lambda a, b: (d(a, b, NN), (a, b)), lambda r, ct: (d(ct, r[1], NT), d(r[0], ct, TN)))
    return d, nn


_d3, _nn3 = _make_dots3()


@jax.custom_vjp
def _inv_unit_lower(m):
    n = m.shape[0]
    eye = (lax.broadcasted_iota(jnp.int32, (n, n), 0) == lax.broadcasted_iota(jnp.int32, (n, n), 1)).astype(f32)
    a = -m
    t = eye + a
    ps = _split(a)
    for _ in range(int(math.log2(n)) - 1):
        ps = _split(_dot_split(ps, ps, NN))
        t = t + _dot_split(_split(t), ps, NN)
    return t


def _inv_fwd(m):
    t = _inv_unit_lower(m)
    return t, t


def _inv_bwd(t, ct):
    ts = _split(t)
    return (-_dot_split(_split(_dot_split(ts, _split(ct), TN)), ts, NT),)


_inv_unit_lower.defvjp(_inv_fwd, _inv_bwd)


def _matmul(a, b, *, ta=False, tb=False, out_dtype=f32, name):
    assert not (ta and tb)
    if ta:
        kdim, m = a.shape
    else:
        m, kdim = a.shape
    n = b.shape[0] if tb else b.shape[1]
    assert (b.shape[1] if tb else b.shape[0]) == kdim
    tm = _pick(m, (768, 512, 384, 256, 128))
    tn = _pick(n, (1024, 768, 640, 512, 256, 128))
    tk = _pick(kdim, (1024, 768, 640, 512, 256, 128))
    nk = kdim // tk
    dims = TN if ta else (NT if tb else NN)

    def body(a_ref, b_ref, o_ref, acc_ref):
        k = pl.program_id(2)

        @pl.when(k == 0)
        def _():
            acc_ref[...] = jnp.zeros_like(acc_ref)

        acc_ref[...] += lax.dot_general(a_ref[...].astype(bf16), b_ref[...].astype(bf16), (dims, ((), ())),
                                        preferred_element_type=f32)

        @pl.when(k == nk - 1)
        def _():
            o_ref[...] = acc_ref[...].astype(o_ref.dtype)

    a_spec = pl.BlockSpec((tk, tm), lambda i, j, k: (k, i)) if ta else pl.BlockSpec((tm, tk), lambda i, j, k: (i, k))
    b_spec = pl.BlockSpec((tn, tk), lambda i, j, k: (j, k)) if tb else pl.BlockSpec((tk, tn), lambda i, j, k: (k, j))
    return pl.pallas_call(
        body, name=name, grid=(m // tm, n // tn, nk),
        in_specs=[a_spec, b_spec], out_specs=pl.BlockSpec((tm, tn), lambda i, j, k: (i, j)),
        out_shape=jax.ShapeDtypeStruct((m, n), out_dtype),
        scratch_shapes=[pltpu.VMEM((tm, tn), f32)],
        compiler_params=pltpu.CompilerParams(dimension_semantics=("parallel", "parallel", "arbitrary")),
    )(a, b)


def _row_spec(item, tr):
    a, bc, off = item
    if bc is None:
        return pl.BlockSpec((tr, a.shape[1]), lambda i, j: (i, 0))
    return pl.BlockSpec((tr, bc), lambda i, j, off=off: (i, j + off))


def _param_spec(p):
    return pl.BlockSpec(p.shape, lambda i, j: (0, 0))


def _row_tile(lp, ncol):
    return ROW_TILE if ncol == 1 else _pick(lp, (768, 512, 256))


def _rowwise(fn, rows, nodiff, params, outs, *, ncol=1, name):
    lp = rows[0][0].shape[0]
    tr = _row_tile(lp, ncol)
    nr, nd = len(rows), len(nodiff)

    def body(*refs):
        rv = [r[...].astype(f32) for r in refs[:nr]]
        nv = [r[...] for r in refs[nr:nr + nd]]
        pv = [r[...] for r in refs[nr + nd:nr + nd + len(params)]]
        res = fn(rv, nv, pv)
        for ref, val in zip(refs[nr + nd + len(params):], res):
            ref[...] = val.astype(ref.dtype)

    out_specs = [pl.BlockSpec((tr, c if bc is None else bc), (lambda i, j: (i, 0)) if bc is None else (lambda i, j: (i, j)))
                 for (c, _, bc) in outs]
    return pl.pallas_call(
        body, name=name, grid=(lp // tr, ncol),
        in_specs=[_row_spec(it, tr) for it in rows + nodiff] + [_param_spec(p) for p in params],
        out_specs=out_specs,
        out_shape=[jax.ShapeDtypeStruct((lp, c), dt) for (c, dt, _) in outs],
    )(*[it[0] for it in rows + nodiff], *params)


def _rowwise_vjp(fn, rows, nodiff, params, cts, *, ncol=1, name, ct_pre=None, extra=None, grad_dtypes=None):
    lp = rows[0][0].shape[0]
    tr = _row_tile(lp, ncol)
    nr, nd, npar, nct = len(rows), len(nodiff), len(params), len(cts)
    extra = extra or [None] * nr
    grad_dtypes = grad_dtypes or [f32] * nr
    ex_items = [(e, rows[k][1], 0) for k, e in enumerate(extra) if e is not None]
    ex_pos = [k for k, e in enumerate(extra) if e is not None]
    for (a, bc, _) in rows:
        assert bc is not None or ncol == 1

    def body(*refs):
        pos = 0
        rv = [r[...].astype(f32) for r in refs[pos:pos + nr]]; pos += nr
        nv = [r[...] for r in refs[pos:pos + nd]]; pos += nd
        pv = [r[...] for r in refs[pos:pos + npar]]; pos += npar
        cv = [r[...].astype(f32) for r in refs[pos:pos + nct]]; pos += nct
        ev = [r[...].astype(f32) for r in refs[pos:pos + len(ex_items)]]; pos += len(ex_items)
        drow_refs = refs[pos:pos + nr]; pos += nr
        dpar_refs = refs[pos:pos + npar]
        outs, vjp_fn = jax.vjp(lambda rr, pp: fn(rr, nv, pp), rv, pv)
        ctv = ct_pre(cv) if ct_pre is not None else cv
        drow, dpar = vjp_fn([c.astype(o.dtype) for c, o in zip(ctv, outs)])
        for k, e in zip(ex_pos, ev):
            drow[k] = drow[k] + e
        for ref, val in zip(drow_refs, drow):
            ref[...] = val.astype(ref.dtype)
        first = jnp.logical_and(pl.program_id(0) == 0, pl.program_id(1) == 0)

        @pl.when(first)
        def _():
            for ref, val in zip(dpar_refs, dpar):
                ref[...] = val

        @pl.when(jnp.logical_not(first))
        def _():
            for ref, val in zip(dpar_refs, dpar):
                ref[...] += val

    drow_shapes, drow_specs = [], []
    for (a, bc, _), dt in zip(rows, grad_dtypes):
        if bc is None:
            drow_shapes.append(jax.ShapeDtypeStruct((lp, a.shape[1]), dt))
            drow_specs.append(pl.BlockSpec((tr, a.shape[1]), lambda i, j: (i, 0)))
        else:
            drow_shapes.append(jax.ShapeDtypeStruct((lp, ncol * bc), dt))
            drow_specs.append(pl.BlockSpec((tr, bc), lambda i, j: (i, j)))
    res = pl.pallas_call(
        body, name=name, grid=(lp // tr, ncol),
        in_specs=[_row_spec(it, tr) for it in rows + nodiff] + [_param_spec(p) for p in params]
        + [_row_spec(it, tr) for it in cts + ex_items],
        out_specs=drow_specs + [_param_spec(p) for p in params],
        out_shape=drow_shapes + [jax.ShapeDtypeStruct(p.shape, f32) for p in params],
        compiler_params=pltpu.CompilerParams(dimension_semantics=("arbitrary", "arbitrary")),
    )(*[it[0] for it in rows + nodiff], *params, *[it[0] for it in cts + ex_items])
    return res[:nr], res[nr:]


def _rms(x, g):
    return x * lax.rsqrt(jnp.mean(x * x, axis=-1, keepdims=True) + NORM_EPS) * g


def _l2n(x):
    return x * lax.rsqrt(jnp.sum(x * x, axis=-1, keepdims=True) + NORM_EPS)


def _sigmoid(x):
    return 1.0 / (1.0 + jnp.exp(-x))


def _silu(x):
    return x * _sigmoid(x)


def _softplus(x):
    return jnp.maximum(x, 0.0) + jnp.log(1.0 + jnp.exp(-jnp.abs(x)))


def _row_ids(shape, tr):
    return pl.program_id(0) * tr + lax.broadcasted_iota(jnp.int32, shape, 0)


def _st_prenorm(r, n, p):
    return [_rms(r[0], p[0])]


def _st_gdn_q(r, n, p):
    return [_l2n(_silu(r[0])) * (GDN_D ** -0.5)]


def _st_gdn_k(r, n, p):
    return [_l2n(_silu(r[0]))]


def _st_gdn_v(r, n, p):
    return [_silu(r[0])]


def _st_gdn_gate(r, n, p):
    real = _row_ids(r[0].shape, ROW_TILE) >= PAD_FRONT
    beta = jnp.where(real, _sigmoid(r[0]), 0.0)
    g = jnp.where(real, -jnp.exp(p[0]) * _softplus(r[1] + p[1]), 0.0)
    return [beta, g]


def _st_gdn_out(r, n, p):
    return [_rms(r[0], p[0]) * _silu(r[1])]


def _st_mid(r, n, p):
    h1 = r[0] + _rms(r[1], p[0])
    return [h1, _rms(h1, p[1]), _rms(h1, p[2])]


def _st_latent(r, n, p):
    ckr, cq = r
    c_kv = _rms(ckr[:, :MLA_KV_RANK], p[0])
    k_rope = ckr[:, 128:256] * n[0] + ckr[:, 384:512] * n[1]
    return [c_kv, k_rope, _rms(cq, p[1])]


def _st_q_rope(r, n, p):
    return [(r[0] * n[0] + r[1] * n[1]) * (MLA_QK ** -0.5)]


def _st_gate(r, n, p):
    return [r[0] * _silu(r[1])]


def _make_st_loss(n_tokens):
    def st(r, n, p):
        h2 = r[0] + _rms(r[1], p[0])
        rows = _row_ids((r[0].shape[0], 1), ROW_TILE)
        real = jnp.logical_and(rows >= ROW0, rows < ROW0 + n_tokens)
        err = h2 - n[0]
        return [jnp.where(real, 0.5 * jnp.mean(err * err, axis=-1, keepdims=True), 0.0)]
    return st


CONV_BC = 512


def _conv_fwd(x, w, *, col_blocks, name, tr=ROW_TILE):
    lp = x.shape[0]

    def body(x_ref, xp_ref, w_ref, o_ref):
        i = pl.program_id(0)
        prev = jnp.where(i > 0, xp_ref[...], 0.0)
        xc = jnp.concatenate([prev, x_ref[...]], axis=0)
        wv = w_ref[...]
        acc = wv[3:4, :] * x_ref[...]
        for j in range(3):
            acc = acc + wv[j:j + 1, :] * pltpu.roll(xc, 3 - j, 0)[8:, :]
        o_ref[...] = acc

    return pl.pallas_call(
        body, name=name, grid=(lp // tr, col_blocks),
        in_specs=[pl.BlockSpec((tr, CONV_BC), lambda i, j: (i, j)),
                  pl.BlockSpec((8, CONV_BC), lambda i, j: (jnp.maximum(i * (tr // 8) - 1, 0), j)),
                  pl.BlockSpec((4, CONV_BC), lambda i, j: (0, j))],
        out_specs=pl.BlockSpec((tr, CONV_BC), lambda i, j: (i, j)),
        out_shape=jax.ShapeDtypeStruct((lp, col_blocks * CONV_BC), f32),
    )(x, x, w)


def _conv_bwd(dc, x, w, *, x_off, w_off, name, tr=ROW_TILE):
    lp, width = dc.shape
    ncb, nrow = width // CONV_BC, lp // tr

    def body(dc_ref, dcn_ref, x_ref, xp_ref, w_ref, dx_ref, dw_ref):
        i = pl.program_id(1)
        nxt = jnp.where(i < nrow - 1, dcn_ref[...], 0.0)
        dcv = dc_ref[...]
        dcc = jnp.concatenate([dcv, nxt], axis=0)
        prev = jnp.where(i > 0, xp_ref[...], 0.0)
        xc = jnp.concatenate([prev, x_ref[...]], axis=0)
        wv = w_ref[...]
        dx = wv[3:4, :] * dcv
        dws = [None] * 4
        dws[3] = jnp.sum(dcv * x_ref[...], axis=0, keepdims=True)
        for j in range(3):
            dx = dx + wv[j:j + 1, :] * pltpu.roll(dcc, tr + 8 - (3 - j), 0)[:tr, :]
            dws[j] = jnp.sum(dcv * pltpu.roll(xc, 3 - j, 0)[8:, :], axis=0, keepdims=True)
        dx_ref[...] = dx.astype(dx_ref.dtype)

        @pl.when(i == 0)
        def _():
            for j in range(4):
                dw_ref[j:j + 1, :] = dws[j]

        @pl.when(i > 0)
        def _():
            for j in range(4):
                dw_ref[j:j + 1, :] += dws[j]

    last8 = lp // 8 - 1
    return pl.pallas_call(
        body, name=name, grid=(ncb, nrow),
        in_specs=[pl.BlockSpec((tr, CONV_BC), lambda j, i: (i, j)),
                  pl.BlockSpec((8, CONV_BC), lambda j, i: (jnp.minimum((i + 1) * (tr // 8), last8), j)),
                  pl.BlockSpec((tr, CONV_BC), lambda j, i: (i, j + x_off)),
                  pl.BlockSpec((8, CONV_BC), lambda j, i: (jnp.maximum(i * (tr // 8) - 1, 0), j + x_off)),
                  pl.BlockSpec((4, CONV_BC), lambda j, i: (0, j + w_off))],
        out_specs=[pl.BlockSpec((tr, CONV_BC), lambda j, i: (i, j)),
                   pl.BlockSpec((4, CONV_BC), lambda j, i: (0, j))],
        out_shape=[jax.ShapeDtypeStruct((lp, width), bf16), jax.ShapeDtypeStruct((4, width), f32)],
        compiler_params=pltpu.CompilerParams(dimension_semantics=("arbitrary", "arbitrary")),
    )(dc, dc, x, x, w)


def _gdn_head(s, q, k, v, beta, gcol, grow):
    c = q.shape[0]
    ii = lax.broadcasted_iota(jnp.int32, (c, c), 0)
    jj = lax.broadcasted_iota(jnp.int32, (c, c), 1)
    dec = jnp.exp(jnp.where(ii >= jj, gcol - grow, NEG))
    dec_strict = jnp.where(ii > jj, dec, 0.0)
    rid = lax.broadcasted_iota(jnp.int32, (c, 1), 0)
    glast = jnp.sum(jnp.where(rid == c - 1, gcol, 0.0), axis=0, keepdims=True)
    eg = jnp.exp(gcol)
    kb = k * beta
    vb = v * beta
    t = _inv_unit_lower(_nt(kb, k) * dec_strict)
    u = _nn3(t, vb)
    w = _nn3(t, kb * eg)
    attn = _nt(q, k) * dec
    v_new = u - _nn(w, s)
    o = _nn(q * eg, s) + _nn(attn, v_new)
    s_new = s * jnp.exp(glast) + _tn(k * jnp.exp(glast - gcol), v_new)
    return o, s_new


def _lane_pick(x, h):
    lane = lax.broadcasted_iota(jnp.int32, x.shape, 1)
    return jnp.sum(jnp.where(lane == h, x, 0.0), axis=1, keepdims=True)


def _cum_log_decay(g):
    c = g.shape[0]
    ii = lax.broadcasted_iota(jnp.int32, (c, c), 0)
    jj = lax.broadcasted_iota(jnp.int32, (c, c), 1)
    lower = (ii >= jj).astype(f32)
    upper = (ii <= jj).astype(f32)
    return _dhi(lower, g, NN), _dhi(g, upper, TN)


def _gdn_fwd(qn, kn, v, beta, g, *, n_real_chunks, name):
    lp = qn.shape[0]
    nchunk = lp // GDN_CHUNK
    C, H, D = GDN_CHUNK, GDN_V_HEADS, GDN_D

    def body(q_ref, k_ref, v_ref, b_ref, g_ref, o_ref, st_ref, s_s, gc_s, gct_s):
        ci = pl.program_id(0)

        @pl.when(ci == 0)
        def _():
            s_s[...] = jnp.zeros_like(s_s)

        @pl.when(ci >= n_real_chunks)
        def _():
            o_ref[...] = jnp.zeros_like(o_ref)
            st_ref[...] = jnp.zeros_like(st_ref)

        @pl.when(ci < n_real_chunks)
        def _():
            gc, gct = _cum_log_decay(g_ref[...])
            gc_s[...] = gc
            gct_s[...] = gct

            def group(gi, carry):
                heads = [gi * GDN_FWD_UNROLL + u for u in range(GDN_FWD_UNROLL)]
                bv, gcv = b_ref[...], gc_s[...]
                args = []
                for h in heads:
                    qk_off = pl.multiple_of((h // 2) * D, D)
                    v_off = pl.multiple_of(h * D, D)
                    args.append((s_s[h], q_ref[:, pl.ds(qk_off, D)], k_ref[:, pl.ds(qk_off, D)], v_ref[:, pl.ds(v_off, D)],
                                 _lane_pick(bv, h), _lane_pick(gcv, h), gct_s[pl.ds(h, 1), :]))
                res = [_gdn_head(*a) for a in args]
                for h, a, (o, s_new) in zip(heads, args, res):
                    st_ref[h] = a[0]
                    s_s[h] = s_new
                    o_ref[:, pl.ds(pl.multiple_of(h * D, D), D)] = o
                return carry

            lax.fori_loop(0, H // GDN_FWD_UNROLL, group, 0)

    return pl.pallas_call(
        body, name=name, grid=(nchunk,),
        in_specs=[pl.BlockSpec((C, GDN_QK_W), lambda c: (c, 0)), pl.BlockSpec((C, GDN_QK_W), lambda c: (c, 0)),
                  pl.BlockSpec((C, GDN_V_W), lambda c: (c, 0)), pl.BlockSpec((C, 128), lambda c: (c, 0)),
                  pl.BlockSpec((C, 128), lambda c: (c, 0))],
        out_specs=[pl.BlockSpec((C, GDN_V_W), lambda c: (c, 0)),
                   pl.BlockSpec((None, H, D, D), lambda c: (c, 0, 0, 0))],
        out_shape=[jax.ShapeDtypeStruct((lp, GDN_V_W), f32), jax.ShapeDtypeStruct((nchunk, H, D, D), f32)],
        scratch_shapes=[pltpu.VMEM((H, D, D), f32), pltpu.VMEM((C, 128), f32), pltpu.VMEM((128, C), f32)],
        compiler_params=pltpu.CompilerParams(dimension_semantics=("arbitrary",)),
    )(qn, kn, v, beta, g)


def _gdn_bwd(qn, kn, v, beta, g, states, do, *, n_real_chunks, name):
    lp = qn.shape[0]
    nchunk = lp // GDN_CHUNK
    C, H, D = GDN_CHUNK, GDN_V_HEADS, GDN_D
    rev = lambda i: (nchunk - 1 - i, 0)

    def body(q_ref, k_ref, v_ref, b_ref, g_ref, st_ref, do_ref,
             dq_ref, dk_ref, dv_ref, db_ref, dg_ref, ds_s, gc_s, gct_s, dgc_s, dgct_s, dbeta_s):
        step = pl.program_id(0)
        ci = nchunk - 1 - step

        @pl.when(step == 0)
        def _():
            ds_s[...] = jnp.zeros_like(ds_s)

        @pl.when(ci >= n_real_chunks)
        def _():
            for r in (dq_ref, dk_ref, dv_ref, db_ref, dg_ref):
                r[...] = jnp.zeros_like(r)

        @pl.when(ci < n_real_chunks)
        def _():
            gc, gct = _cum_log_decay(g_ref[...])
            gc_s[...] = gc
            gct_s[...] = gct
            dgc_s[...] = jnp.zeros_like(dgc_s)
            dgct_s[...] = jnp.zeros_like(dgct_s)
            dbeta_s[...] = jnp.zeros_like(dbeta_s)

            def group(gi, carry):
                heads = [gi * GDN_BWD_UNROLL + u for u in range(GDN_BWD_UNROLL)]
                bv, gcv = b_ref[...], gc_s[...]
                args, cts = [], []
                for h in heads:
                    qk_off = pl.multiple_of((h // 2) * D, D)
                    v_off = pl.multiple_of(h * D, D)
                    args.append((st_ref[h], q_ref[:, pl.ds(qk_off, D)], k_ref[:, pl.ds(qk_off, D)], v_ref[:, pl.ds(v_off, D)],
                                 _lane_pick(bv, h), _lane_pick(gcv, h), gct_s[pl.ds(h, 1), :]))
                    cts.append((do_ref[:, pl.ds(v_off, D)], ds_s[h]))
                grads = [jax.vjp(_gdn_head, *a)[1](c) for a, c in zip(args, cts)]
                lane = lax.broadcasted_iota(jnp.int32, (C, 128), 1)
                dbeta_acc, dgc_acc = dbeta_s[...], dgc_s[...]
                for h, (dsp, dq, dk, dv, dbcol, dgcol, dgrow) in zip(heads, grads):
                    v_off = pl.multiple_of(h * D, D)
                    ds_s[h] = dsp
                    dq_ref[:, pl.ds(v_off, D)] = dq
                    dk_ref[:, pl.ds(v_off, D)] = dk
                    dv_ref[:, pl.ds(v_off, D)] = dv
                    dbeta_acc = dbeta_acc + jnp.where(lane == h, dbcol, 0.0)
                    dgc_acc = dgc_acc + jnp.where(lane == h, dgcol, 0.0)
                    dgct_s[pl.ds(h, 1), :] = dgrow
                dbeta_s[...] = dbeta_acc
                dgc_s[...] = dgc_acc
                return carry

            lax.fori_loop(0, H // GDN_BWD_UNROLL, group, 0)
            eye = (lax.broadcasted_iota(jnp.int32, (128, 128), 0) == lax.broadcasted_iota(jnp.int32, (128, 128), 1)).astype(f32)
            dgc = dgc_s[...] + _dhi(dgct_s[...], eye, TN)
            upper = (lax.broadcasted_iota(jnp.int32, (C, C), 0) <= lax.broadcasted_iota(jnp.int32, (C, C), 1)).astype(f32)
            dg_ref[...] = _dhi(upper, dgc, NN)
            db_ref[...] = dbeta_s[...]

    return pl.pallas_call(
        body, name=name, grid=(nchunk,),
        in_specs=[pl.BlockSpec((C, GDN_QK_W), rev), pl.BlockSpec((C, GDN_QK_W), rev), pl.BlockSpec((C, GDN_V_W), rev),
                  pl.BlockSpec((C, 128), rev), pl.BlockSpec((C, 128), rev),
                  pl.BlockSpec((None, H, D, D), lambda i: (nchunk - 1 - i, 0, 0, 0)), pl.BlockSpec((C, GDN_V_W), rev)],
        out_specs=[pl.BlockSpec((C, GDN_V_W), rev), pl.BlockSpec((C, GDN_V_W), rev), pl.BlockSpec((C, GDN_V_W), rev),
                   pl.BlockSpec((C, 128), rev), pl.BlockSpec((C, 128), rev)],
        out_shape=[jax.ShapeDtypeStruct((lp, GDN_V_W), f32)] * 3 + [jax.ShapeDtypeStruct((lp, 128), f32)] * 2,
        scratch_shapes=[pltpu.VMEM((H, D, D), f32), pltpu.VMEM((C, 128), f32), pltpu.VMEM((128, C), f32),
                        pltpu.VMEM((C, 128), f32), pltpu.VMEM((128, C), f32), pltpu.VMEM((C, 128), f32)],
        compiler_params=pltpu.CompilerParams(dimension_semantics=("arbitrary",)),
    )(qn, kn, v, beta, g, states, do)


def _att_mask_t(k0, q0, tk, tq):
    kcol = k0 + lax.broadcasted_iota(jnp.int32, (tk, tq), 0)
    qrow = q0 + lax.broadcasted_iota(jnp.int32, (tk, tq), 1)
    return jnp.logical_and(qrow >= kcol, kcol >= PAD_FRONT)


def _attention_fwd(q, kvu, kr, *, name, tk=ATT_TILE):
    lp = q.shape[0]
    H = MLA_HEADS
    tq = _pick(lp, (768, 512, 256))
    r = tq // tk

    def body(q_ref, kn_ref, kr_ref, v_ref, o_ref, lse_ref, m_s, l_s, acc_s):
        qi = pl.program_id(1)
        qv = q_ref[...]
        m_s[...] = jnp.full_like(m_s, NEG)
        l_s[...] = jnp.zeros_like(l_s)
        acc_s[...] = jnp.zeros_like(acc_s)

        def step(ki, masked):
            k0 = pl.multiple_of(ki * tk, tk)
            k = jnp.concatenate([kn_ref[pl.ds(k0, tk), :], kr_ref[pl.ds(k0, tk), :]], axis=1)
            st = lax.dot_general(k, qv, (NT, ((), ())), preferred_element_type=f32)
            if masked:
                st = jnp.where(_att_mask_t(k0, qi * tq, tk, tq), st, NEG)
            m_prev = m_s[...]
            m_new = jnp.maximum(m_prev, jnp.max(st, axis=0, keepdims=True))
            alpha = jnp.exp(m_prev - m_new)
            p = jnp.exp(st - m_new)
            l_s[...] = alpha * l_s[...] + jnp.sum(p, axis=0, keepdims=True)
            acc_s[...] = alpha * acc_s[...] + lax.dot_general(v_ref[pl.ds(k0, tk), :], p.astype(bf16), (TN, ((), ())),
                                                              preferred_element_type=f32)
            m_s[...] = m_new

        n_full = qi * r

        @pl.when(qi > 0)
        def _():
            step(0, True)

        def plain(ki, carry):
            step(ki, False)
            return carry

        lax.fori_loop(1, n_full, plain, 0)
        for d in range(r):
            step(n_full + d, True)
        o_ref[...] = jnp.transpose(acc_s[...] / l_s[...])
        lse_ref[...] = m_s[...] + jnp.log(l_s[...])

    return pl.pallas_call(
        body, name=name, grid=(H, lp // tq),
        in_specs=[pl.BlockSpec((tq, MLA_QKP), lambda h, qi: (qi, h)),
                  pl.BlockSpec((lp, 128), lambda h, qi: (0, h)),
                  pl.BlockSpec((lp, 128), lambda h, qi: (0, 0)),
                  pl.BlockSpec((lp, 128), lambda h, qi: (0, H + h))],
        out_specs=[pl.BlockSpec((tq, 128), lambda h, qi: (qi, h)),
                   pl.BlockSpec((None, 1, tq), lambda h, qi: (h, 0, qi))],
        out_shape=[jax.ShapeDtypeStruct((lp, MLA_V_W), f32), jax.ShapeDtypeStruct((H, 1, lp), f32)],
        scratch_shapes=[pltpu.VMEM((1, tq), f32), pltpu.VMEM((1, tq), f32), pltpu.VMEM((128, tq), f32)],
        compiler_params=pltpu.CompilerParams(dimension_semantics=("arbitrary", "arbitrary")),
    )(q, kvu, kr, kvu)


def _attention_delta(o, do, *, name):
    lp = o.shape[0]
    H = MLA_HEADS
    tq = _pick(lp, (768, 512, 256))

    def body(o_ref, do_ref, d_ref):
        prod = o_ref[...] * do_ref[...].astype(f32)
        d_ref[...] = jnp.sum(jnp.transpose(prod), axis=0, keepdims=True)

    return pl.pallas_call(
        body, name=name, grid=(H, lp // tq),
        in_specs=[pl.BlockSpec((tq, 128), lambda h, qi: (qi, h)), pl.BlockSpec((tq, 128), lambda h, qi: (qi, h))],
        out_specs=pl.BlockSpec((None, 1, tq), lambda h, qi: (h, 0, qi)),
        out_shape=jax.ShapeDtypeStruct((H, 1, lp), f32),
    )(o, do)


def _attention_bwd(q, kvu, kr, lse, delta, do, *, name, t=ATT_TILE):
    lp = q.shape[0]
    H, nb = MLA_HEADS, lp // t
    tq = _pick(lp, (768, 512, 256))
    r, nq = tq // t, lp // tq

    def body(q_ref, kn_ref, kr_ref, v_ref, lse_ref, dl_ref, do_ref, dq_ref, dkn_ref, dv_ref, dkr_ref, dk_s, dv_s):
        ki = pl.program_id(1)
        k0 = ki * t
        k = jnp.concatenate([kn_ref[...], kr_ref[...]], axis=1)
        vv = v_ref[...]
        dk_s[...] = jnp.zeros_like(dk_s)
        dv_s[...] = jnp.zeros_like(dv_s)

        def pair(qi, masked, first):
            q0 = pl.multiple_of(qi * tq, tq)
            qv = q_ref[pl.ds(q0, tq), :]
            dob = do_ref[pl.ds(q0, tq), :]
            st = lax.dot_general(k, qv, (NT, ((), ())), preferred_element_type=f32)
            if masked:
                st = jnp.where(_att_mask_t(k0, q0, t, tq), st, NEG)
            p = jnp.exp(st - lse_ref[:, pl.ds(q0, tq)])
            dv_s[...] += jnp.dot(p.astype(bf16), dob, preferred_element_type=f32)
            dp = lax.dot_general(vv, dob, (NT, ((), ())), preferred_element_type=f32)
            ds = (p * (dp - dl_ref[:, pl.ds(q0, tq)])).astype(bf16)
            dk_s[...] += jnp.dot(ds, qv, preferred_element_type=f32)
            dq = lax.dot_general(ds, k, (TN, ((), ())), preferred_element_type=f32)
            if first:
                dq_ref[pl.ds(q0, tq), :] = dq
            else:
                dq_ref[pl.ds(q0, tq), :] += dq

        @pl.when(ki == 0)
        def _():
            def every(qi, carry):
                pair(qi, True, True)
                return carry
            lax.fori_loop(0, nq, every, 0)

        @pl.when(ki > 0)
        def _():
            pair(ki // r, True, False)

            def below(qi, carry):
                pair(qi, False, False)
                return carry
            lax.fori_loop(ki // r + 1, nq, below, 0)

        dkn_ref[...] = dk_s[:, :128].astype(dkn_ref.dtype)
        dkr_ref[...] = dk_s[:, 128:]
        dv_ref[...] = dv_s[...].astype(dv_ref.dtype)

    return pl.pallas_call(
        body, name=name, grid=(H, nb),
        in_specs=[pl.BlockSpec((lp, MLA_QKP), lambda h, ki: (0, h)),
                  pl.BlockSpec((t, 128), lambda h, ki: (ki, h)),
                  pl.BlockSpec((t, 128), lambda h, ki: (ki, 0)),
                  pl.BlockSpec((t, 128), lambda h, ki: (ki, H + h)),
                  pl.BlockSpec((None, 1, lp), lambda h, ki: (h, 0, 0)),
                  pl.BlockSpec((None, 1, lp), lambda h, ki: (h, 0, 0)),
                  pl.BlockSpec((lp, 128), lambda h, ki: (0, h))],
        out_specs=[pl.BlockSpec((lp, MLA_QKP), lambda h, ki: (0, h)),
                   pl.BlockSpec((t, 128), lambda h, ki: (ki, h)),
                   pl.BlockSpec((t, 128), lambda h, ki: (ki, h)),
                   pl.BlockSpec((t, 128), lambda h, ki: (ki, h))],
        out_shape=[jax.ShapeDtypeStruct((lp, H * MLA_QKP), f32), jax.ShapeDtypeStruct((lp, MLA_V_W), bf16),
                   jax.ShapeDtypeStruct((lp, MLA_V_W), bf16), jax.ShapeDtypeStruct((lp, MLA_V_W), f32)],
        scratch_shapes=[pltpu.VMEM((t, MLA_QKP), f32), pltpu.VMEM((t, 128), f32)],
        compiler_params=pltpu.CompilerParams(dimension_semantics=("arbitrary", "arbitrary")),
    )(q, kvu, kr, kvu, lse, delta, do)


def _exchange(x, *, gather, name):
    blk = x.shape if gather else x.shape[1:]

    def body(x_ref, o_ref, send_sems, recv_sems, local_sem):
        mx, my, mc = lax.axis_index("x"), lax.axis_index("y"), lax.axis_index("c")
        me = 4 * mx + 2 * my + mc
        own = pltpu.make_async_copy(x_ref if gather else x_ref.at[me], o_ref.at[me], local_sem)
        own.start()
        sends, peers = [], []
        for k in range(1, N_DEV):
            px = 1 - mx if k & 4 else mx
            py = 1 - my if k & 2 else my
            pc = 1 - mc if k & 1 else mc
            peer = 4 * px + 2 * py + pc
            cp = pltpu.make_async_remote_copy(
                src_ref=x_ref if gather else x_ref.at[peer], dst_ref=o_ref.at[me],
                send_sem=send_sems.at[k - 1], recv_sem=recv_sems.at[k - 1],
                device_id=(px, py, pc), device_id_type=MESH_ID)
            cp.start()
            sends.append(cp)
            peers.append(peer)
        for k in range(1, N_DEV):
            pltpu.make_async_remote_copy(
                src_ref=o_ref.at[peers[k - 1]], dst_ref=o_ref.at[peers[k - 1]],
                send_sem=send_sems.at[k - 1], recv_sem=recv_sems.at[k - 1],
                device_id=(mx, my, mc), device_id_type=MESH_ID).wait_recv()
        for cp in sends:
            cp.wait_send()
        own.wait()

    return pl.pallas_call(
        body, name=name,
        in_specs=[pl.BlockSpec(memory_space=pltpu.HBM)], out_specs=pl.BlockSpec(memory_space=pltpu.HBM),
        out_shape=jax.ShapeDtypeStruct((N_DEV,) + tuple(blk), x.dtype),
        scratch_shapes=[pltpu.SemaphoreType.DMA((N_DEV - 1,)), pltpu.SemaphoreType.DMA((N_DEV - 1,)),
                        pltpu.SemaphoreType.DMA],
    )(x)


def _reduce_adamw(parts, w, m, v, *, name):
    r = w.shape[0]
    tr = _pick(r, (1280, 1024, 512, 256, 128, 64, 48, 32, 16, 8))

    def body(p_ref, w_ref, m_ref, v_ref, g_ref, d_ref, nm_ref, nv_ref):
        g = p_ref[0]
        for s in range(1, N_DEV):
            g = g + p_ref[s]
        mm = ADAM_B1 * m_ref[...] + (1.0 - ADAM_B1) * g
        vv = ADAM_B2 * v_ref[...] + (1.0 - ADAM_B2) * (g * g)
        m_hat = mm / (1.0 - ADAM_B1 ** ADAM_STEP)
        v_hat = vv / (1.0 - ADAM_B2 ** ADAM_STEP)
        g_ref[...] = g
        d_ref[...] = -ADAM_LR * (m_hat / (jnp.sqrt(v_hat) + ADAM_EPS) + ADAM_WD * w_ref[...])
        nm_ref[...] = mm
        nv_ref[...] = vv

    spec = pl.BlockSpec((tr, 128), lambda i: (i, 0))
    return pl.pallas_call(
        body, name=name, grid=(r // tr,),
        in_specs=[pl.BlockSpec((N_DEV, tr, 128), lambda i: (0, i, 0)), spec, spec, spec],
        out_specs=[spec] * 4, out_shape=[jax.ShapeDtypeStruct((r, 128), f32)] * 4,
    )(parts, w, m, v)


_SHARDED = ("gdn_w_in", "gdn_w_out", "kv_w_down", "kv_w_up", "mla_w_in", "mla_w_q_up", "mla_w_out", "meta_tokens", "gdn_conv_w")
_COL_SHARDED = {"gdn_w_in", "kv_w_up", "mla_w_in", "mla_w_q_up", "meta_tokens", "gdn_conv_w"}
_N_BF16 = 7
_REPLICATED = ("pre_norm", "post_norm", "gdn_a_log", "gdn_dt_bias", "gdn_out_norm", "kv_norm", "kv_latent_norm",
               "mla_q_latent_norm")


def _rows128(a):
    flat = a.reshape(-1)
    pad = (-flat.shape[0]) % 128
    if pad:
        flat = jnp.pad(flat, (0, pad))
    return flat.reshape(-1, 128)


def _pack(arrs, row_multiple):
    parts = [_rows128(a) for a in arrs]
    buf = jnp.concatenate(parts, axis=0)
    pad = (-buf.shape[0]) % row_multiple
    if pad:
        buf = jnp.pad(buf, ((0, pad), (0, 0)))
    return buf


def _unpack(buf, shapes):
    out, r = [], 0
    for shp in shapes:
        n = math.prod(shp)
        rows = -(-n // 128)
        out.append(buf[r:r + rows].reshape(-1)[:n].reshape(shp))
        r += rows
    return out


def _unshard(g, full_shape, col):
    if col:
        return jnp.transpose(g, (1, 0, 2)).reshape(full_shape)
    return g.reshape(full_shape)


def _to_shards(a, col):
    r, c = a.shape
    if col:
        return jnp.transpose(a.reshape(r, N_DEV, c // N_DEV), (1, 0, 2))
    return a.reshape(N_DEV, r // N_DEV, c)


def _pad_cols(a, width):
    return jnp.pad(a, ((0, 0), (0, width - a.shape[1])))


def _rope_tables(lp):
    inv = ROPE_THETA ** (-jnp.arange(0, MLA_ROPE, 2, dtype=f32) / MLA_ROPE)
    pos = (jnp.arange(lp, dtype=jnp.int32) - PAD_FRONT).astype(f32)
    ang = pos[:, None] * inv[None, :]
    cos, sin = jnp.cos(ang), jnp.sin(ang)
    z = jnp.zeros((lp, 64), f32)
    return jnp.concatenate([cos, cos, z], axis=1), jnp.concatenate([-sin, sin, z], axis=1)


def kernel(x, meta_tokens, pre_norm, post_norm, gdn_w_in, gdn_conv_w, gdn_a_log, gdn_dt_bias, gdn_out_norm, gdn_w_out, kv_norm, kv_w_down, kv_latent_norm, kv_w_up, mla_w_in, mla_q_latent_norm, mla_w_q_up, mla_w_out, loss_target, m_meta_tokens, m_pre_norm, m_post_norm, m_gdn_w_in, m_gdn_conv_w, m_gdn_a_log, m_gdn_dt_bias, m_gdn_out_norm, m_gdn_w_out, m_kv_norm, m_kv_w_down, m_kv_latent_norm, m_kv_w_up, m_mla_w_in, m_mla_q_latent_norm, m_mla_w_q_up, m_mla_w_out, v_meta_tokens, v_pre_norm, v_post_norm, v_gdn_w_in, v_gdn_conv_w, v_gdn_a_log, v_gdn_dt_bias, v_gdn_out_norm, v_gdn_w_out, v_kv_norm, v_kv_w_down, v_kv_latent_norm, v_kv_w_up, v_mla_w_in, v_mla_q_latent_norm, v_mla_w_q_up, v_mla_w_out):
    W = dict(meta_tokens=meta_tokens, pre_norm=pre_norm, post_norm=post_norm, gdn_w_in=gdn_w_in, gdn_conv_w=gdn_conv_w,
             gdn_a_log=gdn_a_log, gdn_dt_bias=gdn_dt_bias, gdn_out_norm=gdn_out_norm, gdn_w_out=gdn_w_out, kv_norm=kv_norm,
             kv_w_down=kv_w_down, kv_latent_norm=kv_latent_norm, kv_w_up=kv_w_up, mla_w_in=mla_w_in,
             mla_q_latent_norm=mla_q_latent_norm, mla_w_q_up=mla_w_q_up, mla_w_out=mla_w_out)
    M = dict(meta_tokens=m_meta_tokens, pre_norm=m_pre_norm, post_norm=m_post_norm, gdn_w_in=m_gdn_w_in, gdn_conv_w=m_gdn_conv_w,
             gdn_a_log=m_gdn_a_log, gdn_dt_bias=m_gdn_dt_bias, gdn_out_norm=m_gdn_out_norm, gdn_w_out=m_gdn_w_out, kv_norm=m_kv_norm,
             kv_w_down=m_kv_w_down, kv_latent_norm=m_kv_latent_norm, kv_w_up=m_kv_w_up, mla_w_in=m_mla_w_in,
             mla_q_latent_norm=m_mla_q_latent_norm, mla_w_q_up=m_mla_w_q_up, mla_w_out=m_mla_w_out)
    V = dict(meta_tokens=v_meta_tokens, pre_norm=v_pre_norm, post_norm=v_post_norm, gdn_w_in=v_gdn_w_in, gdn_conv_w=v_gdn_conv_w,
             gdn_a_log=v_gdn_a_log, gdn_dt_bias=v_gdn_dt_bias, gdn_out_norm=v_gdn_out_norm, gdn_w_out=v_gdn_w_out, kv_norm=v_kv_norm,
             kv_w_down=v_kv_w_down, kv_latent_norm=v_kv_latent_norm, kv_w_up=v_kv_w_up, mla_w_in=v_mla_w_in,
             mla_q_latent_norm=v_mla_q_latent_norm, mla_w_q_up=v_mla_w_q_up, mla_w_out=v_mla_w_out)
    order = list(W)

    n_tok = x.shape[1]
    assert n_tok % GDN_CHUNK == 0
    n_real = ROW0 + n_tok
    lp = -(-n_real // ROW_TILE) * ROW_TILE
    n_real_chunks = n_real // GDN_CHUNK

    shard2d = {n: W[n].reshape(W[n].shape[-2:]) for n in _SHARDED}
    full_shape = {n: ((s.shape[0], s.shape[1] * N_DEV) if n in _COL_SHARDED else (s.shape[0] * N_DEV, s.shape[1]))
                  for n, s in shard2d.items()}
    big, small = _SHARDED[:_N_BF16], _SHARDED[_N_BF16:]
    g_big = _exchange(_pack([shard2d[n].astype(bf16) for n in big], 16), gather=True, name="gather_weights")
    g_small = _exchange(_pack([shard2d[n] for n in small], 8), gather=True, name="gather_meta_conv")
    full = {}
    for names, buf in ((big, g_big), (small, g_small)):
        r = 0
        for n in names:
            shp = shard2d[n].shape
            rows = math.prod(shp) // 128
            blocks = buf[:, r:r + rows].reshape((N_DEV,) + shp)
            full[n] = _unshard(blocks, full_shape[n], n in _COL_SHARDED)
            r += rows

    w_in = full["gdn_w_in"]
    s1 = GDN_CONV_W + GDN_V_W
    w_in_p = jnp.concatenate([w_in[:, :s1], _pad_cols(w_in[:, s1:s1 + 16], 128), _pad_cols(w_in[:, s1 + 16:], 128)], axis=1)
    wd = full["kv_w_down"]
    zc = jnp.zeros((D_MODEL, 64), bf16)
    wd2 = jnp.concatenate([wd, zc, jnp.zeros((D_MODEL, 128), bf16), wd[:, 160:192], wd[:, 128:160], zc], axis=1)
    wup_p = jnp.transpose(full["kv_w_up"].reshape(MLA_KV_RANK, MLA_HEADS, 2, 128), (0, 2, 1, 3)).reshape(MLA_KV_RANK, 2 * MLA_V_W)
    wq = full["mla_w_q_up"].reshape(MLA_Q_RANK, MLA_HEADS, MLA_QK)
    zq64 = jnp.zeros((MLA_Q_RANK, MLA_HEADS, 64), bf16)
    wq_plain = jnp.concatenate([wq, zq64], axis=2).reshape(MLA_Q_RANK, MLA_HEADS * MLA_QKP)
    wq_swap = jnp.concatenate([jnp.zeros((MLA_Q_RANK, MLA_HEADS, 128), bf16), wq[:, :, 160:192], wq[:, :, 128:160], zq64],
                              axis=2).reshape(MLA_Q_RANK, MLA_HEADS * MLA_QKP)
    wq2 = jnp.concatenate([wq_plain, wq_swap], axis=1)
    w_mla_in, w_gdn_out, w_mla_out = full["mla_w_in"], full["gdn_w_out"], full["mla_w_out"]
    conv_w = full["gdn_conv_w"]
    pre0, pre1 = pre_norm[0:1], pre_norm[1:2]
    post0, post1 = post_norm[0:1], post_norm[1:2]
    alog_p, dtb_p = _pad_cols(gdn_a_log, 128), _pad_cols(gdn_dt_bias, 128)
    kvn, kvln = kv_norm.reshape(1, -1), kv_latent_norm.reshape(1, -1)

    h0 = jnp.concatenate([jnp.zeros((PAD_FRONT, D_MODEL), f32), full["meta_tokens"], x[0],
                          jnp.zeros((lp - n_real, D_MODEL), f32)], axis=0)
    tgt = jnp.concatenate([jnp.zeros((ROW0, D_MODEL), f32), loss_target[0], jnp.zeros((lp - n_real, D_MODEL), f32)], axis=0)
    cos_k, sin_k = _rope_tables(lp)
    one = jnp.ones((lp, 128), f32)
    cos_q = jnp.concatenate([one, cos_k], axis=1)
    sin_q = jnp.concatenate([jnp.zeros((lp, 128), f32), sin_k], axis=1)

    (hn0,) = _rowwise(_st_prenorm, [(h0, None, 0)], [], [pre0], [(D_MODEL, bf16, None)], name="f_prenorm0")
    proj = _matmul(hn0, w_in_p, name="f_gdn_in")
    conv = _conv_fwd(proj, conv_w, col_blocks=GDN_CONV_W // CONV_BC, name="f_conv")
    (qn,) = _rowwise(_st_gdn_q, [(conv, 128, 0)], [], [], [(GDN_QK_W, f32, 128)], ncol=GDN_QK_HEADS, name="f_gdn_q")
    (kn,) = _rowwise(_st_gdn_k, [(conv, 128, GDN_QK_HEADS)], [], [], [(GDN_QK_W, f32, 128)], ncol=GDN_QK_HEADS, name="f_gdn_k")
    (vv,) = _rowwise(_st_gdn_v, [(conv, 128, 2 * GDN_QK_HEADS)], [], [], [(GDN_V_W, f32, 128)], ncol=GDN_V_HEADS, name="f_gdn_v")
    gate_rows = [(proj, 128, s1 // 128), (proj, 128, s1 // 128 + 1)]
    beta, gdec = _rowwise(_st_gdn_gate, gate_rows, [], [alog_p, dtb_p], [(128, f32, None)] * 2, name="f_gdn_gate")
    o_gdn, states = _gdn_fwd(qn, kn, vv, beta, gdec, n_real_chunks=n_real_chunks, name="f_gdn")
    out_rows = [(o_gdn, 128, 0), (proj, 128, GDN_CONV_W // 128)]
    (og,) = _rowwise(_st_gdn_out, out_rows, [], [gdn_out_norm], [(GDN_V_W, bf16, 128)], ncol=GDN_V_HEADS, name="f_gdn_out")
    y0 = _matmul(og, w_gdn_out, name="f_gdn_wout")
    mid_rows = [(h0, None, 0), (y0, None, 0)]
    h1, hn1, hkv = _rowwise(_st_mid, mid_rows, [], [post0, pre1, kvn],
                            [(D_MODEL, f32, None), (D_MODEL, bf16, None), (D_MODEL, bf16, None)], name="f_mid")
    ckr = _matmul(hkv, wd2, name="f_kv_down")
    proj2 = _matmul(hn1, w_mla_in, name="f_mla_in")
    lat_rows = [(ckr, None, 0), (proj2, MLA_Q_RANK, 0)]
    lat_nd = [(cos_k, None, 0), (sin_k, None, 0)]
    c_kv, k_rope, c_q = _rowwise(_st_latent, lat_rows, lat_nd, [kvln, mla_q_latent_norm],
                                 [(128, bf16, None), (128, bf16, None), (MLA_Q_RANK, bf16, None)], name="f_latent")
    kvu = _matmul(c_kv, wup_p, out_dtype=bf16, name="f_kv_up")
    qq = _matmul(c_q, wq2, name="f_q_up")
    q_rows = [(qq, MLA_QKP, 0), (qq, MLA_QKP, MLA_HEADS)]
    q_nd = [(cos_q, None, 0), (sin_q, None, 0)]
    (q_att,) = _rowwise(_st_q_rope, q_rows, q_nd, [], [(MLA_HEADS * MLA_QKP, bf16, MLA_QKP)], ncol=MLA_HEADS, name="f_q_rope")
    o_att, lse = _attention_fwd(q_att, kvu, k_rope, name="f_attention")
    gate2_rows = [(o_att, 128, 0), (proj2, 128, MLA_Q_RANK // 128)]
    (og2,) = _rowwise(_st_gate, gate2_rows, [], [], [(MLA_V_W, bf16, 128)], ncol=MLA_HEADS, name="f_mla_gate")
    y1 = _matmul(og2, w_mla_out, name="f_mla_wout")
    st_loss = _make_st_loss(n_tok)
    loss_rows_in = [(h1, None, 0), (y1, None, 0)]
    (loss_rows,) = _rowwise(st_loss, loss_rows_in, [(tgt, None, 0)], [post1], [(1, f32, None)], name="f_loss")
    loss = lax.psum(jnp.sum(loss_rows), ("x", "y", "c"))

    ones_ct = jnp.ones((lp, 1), f32)
    (dh1_a, dy1), (dpost1,) = _rowwise_vjp(st_loss, loss_rows_in, [(tgt, None, 0)], [post1], [(ones_ct, None, 0)], name="b_loss")
    dog2 = _matmul(dy1, w_mla_out, tb=True, name="b_mla_wout_x")
    dw_mla_out = _matmul(og2, dy1, ta=True, name="b_mla_wout_w")
    (do_att, dz2), _ = _rowwise_vjp(_st_gate, gate2_rows, [], [], [(dog2, 128, 0)], ncol=MLA_HEADS, name="b_mla_gate",
                                    grad_dtypes=[bf16, bf16])
    delta = _attention_delta(o_att, do_att, name="b_attention_delta")
    dq_att, dkn, dvv, dkr_h = _attention_bwd(q_att, kvu, k_rope, lse, delta, do_att, name="b_attention")
    (dqa, dqb), _ = _rowwise_vjp(_st_q_rope, q_rows, q_nd, [], [(dq_att, MLA_QKP, 0)], ncol=MLA_HEADS, name="b_q_rope",
                                 grad_dtypes=[bf16, bf16])
    dqq = jnp.concatenate([dqa, dqb], axis=1)
    dc_q = _matmul(dqq, wq2, tb=True, name="b_q_up_x")
    dwq2 = _matmul(c_q, dqq, ta=True, name="b_q_up_w")
    dkvu = jnp.concatenate([dkn, dvv], axis=1)
    dc_kv = _matmul(dkvu, wup_p, tb=True, name="b_kv_up_x")
    dwup_p = _matmul(c_kv, dkvu, ta=True, name="b_kv_up_w")

    def lat_ct(cv):
        dkr = cv[1][:, 0:128]
        for h in range(1, MLA_HEADS):
            dkr = dkr + cv[1][:, h * 128:(h + 1) * 128]
        return [cv[0], dkr, cv[2]]

    (dckr, dcq_pre), (dkvln, dqln) = _rowwise_vjp(
        _st_latent, lat_rows, lat_nd, [kvln, mla_q_latent_norm],
        [(dc_kv, None, 0), (dkr_h, None, 0), (dc_q, None, 0)], ct_pre=lat_ct, name="b_latent", grad_dtypes=[bf16, bf16])
    dproj2 = jnp.concatenate([dcq_pre, dz2], axis=1)
    dhn1 = _matmul(dproj2, w_mla_in, tb=True, name="b_mla_in_x")
    dw_mla_in = _matmul(hn1, dproj2, ta=True, name="b_mla_in_w")
    dhkv = _matmul(dckr, wd2, tb=True, name="b_kv_down_x")
    dwd2 = _matmul(hkv, dckr, ta=True, name="b_kv_down_w")
    (dh0_a, dy0), (dpost0, dpre1, dkvn) = _rowwise_vjp(
        _st_mid, mid_rows, [], [post0, pre1, kvn], [(dh1_a, None, 0), (dhn1, None, 0), (dhkv, None, 0)], name="b_mid")
    dog = _matmul(dy0, w_gdn_out, tb=True, name="b_gdn_wout_x")
    dw_gdn_out = _matmul(og, dy0, ta=True, name="b_gdn_wout_w")
    (do_gdn, dz), (dout_norm,) = _rowwise_vjp(_st_gdn_out, out_rows, [], [gdn_out_norm], [(dog, 128, 0)], ncol=GDN_V_HEADS,
                                              name="b_gdn_out", grad_dtypes=[f32, bf16])
    dq16, dk16, dv_g, dbeta, dgdec = _gdn_bwd(qn, kn, vv, beta, gdec, states, do_gdn, n_real_chunks=n_real_chunks, name="b_gdn")
    (db_col, da_col), (dalog_p, ddtb_p) = _rowwise_vjp(
        _st_gdn_gate, gate_rows, [], [alog_p, dtb_p], [(dbeta, None, 0), (dgdec, None, 0)], name="b_gdn_gate",
        grad_dtypes=[bf16, bf16])
    pair = lambda cv: [cv[0][:, :128] + cv[0][:, 128:]]
    (dconv_q,), _ = _rowwise_vjp(_st_gdn_q, [(conv, 128, 0)], [], [], [(dq16, 256, 0)], ncol=GDN_QK_HEADS, ct_pre=pair, name="b_gdn_q")
    (dconv_k,), _ = _rowwise_vjp(_st_gdn_k, [(conv, 128, GDN_QK_HEADS)], [], [], [(dk16, 256, 0)], ncol=GDN_QK_HEADS, ct_pre=pair,
                                 name="b_gdn_k")
    (dconv_v,), _ = _rowwise_vjp(_st_gdn_v, [(conv, 128, 2 * GDN_QK_HEADS)], [], [], [(dv_g, 128, 0)], ncol=GDN_V_HEADS, name="b_gdn_v")
    nq_b = GDN_QK_W // CONV_BC
    dpre_q, dcw_q = _conv_bwd(dconv_q, proj, conv_w, x_off=0, w_off=0, name="b_conv_q")
    dpre_k, dcw_k = _conv_bwd(dconv_k, proj, conv_w, x_off=nq_b, w_off=nq_b, name="b_conv_k")
    dpre_v, dcw_v = _conv_bwd(dconv_v, proj, conv_w, x_off=2 * nq_b, w_off=2 * nq_b, name="b_conv_v")
    dproj = jnp.concatenate([dpre_q, dpre_k, dpre_v, dz, db_col, da_col], axis=1)
    dhn0 = _matmul(dproj, w_in_p, tb=True, name="b_gdn_in_x")
    dw_in_p = _matmul(hn0, dproj, ta=True, name="b_gdn_in_w")
    (dh0,), (dpre0,) = _rowwise_vjp(_st_prenorm, [(h0, None, 0)], [], [pre0], [(dhn0, None, 0)], extra=[dh0_a], name="b_prenorm0")

    grad_x = dh0[ROW0:n_real][None]
    G = {}
    G["meta_tokens"] = dh0[PAD_FRONT:ROW0]
    G["gdn_w_in"] = jnp.concatenate([dw_in_p[:, :s1 + 16], dw_in_p[:, s1 + 128:s1 + 144]], axis=1)
    G["gdn_conv_w"] = jnp.concatenate([dcw_q, dcw_k, dcw_v], axis=1)
    G["gdn_w_out"] = dw_gdn_out
    G["kv_w_down"] = jnp.concatenate([dwd2[:, :128], dwd2[:, 128:160] + dwd2[:, 416:448], dwd2[:, 160:192] + dwd2[:, 384:416]], axis=1)
    G["kv_w_up"] = jnp.transpose(dwup_p.reshape(MLA_KV_RANK, 2, MLA_HEADS, 128), (0, 2, 1, 3)).reshape(MLA_KV_RANK, 2 * MLA_V_W)
    G["mla_w_in"] = dw_mla_in
    dqp = dwq2[:, :MLA_HEADS * MLA_QKP].reshape(MLA_Q_RANK, MLA_HEADS, MLA_QKP)
    dqs = dwq2[:, MLA_HEADS * MLA_QKP:].reshape(MLA_Q_RANK, MLA_HEADS, MLA_QKP)
    G["mla_w_q_up"] = jnp.concatenate([dqp[:, :, :128], dqp[:, :, 128:160] + dqs[:, :, 160:192],
                                       dqp[:, :, 160:192] + dqs[:, :, 128:160]], axis=2).reshape(MLA_Q_RANK, MLA_HEADS * MLA_QK)
    G["mla_w_out"] = dw_mla_out
    G["pre_norm"] = jnp.concatenate([dpre0, dpre1], axis=0)
    G["post_norm"] = jnp.concatenate([dpost0, dpost1], axis=0)
    G["gdn_a_log"] = dalog_p[:, :GDN_V_HEADS]
    G["gdn_dt_bias"] = ddtb_p[:, :GDN_V_HEADS]
    G["gdn_out_norm"] = dout_norm
    G["kv_norm"] = dkvn.reshape(-1)
    G["kv_latent_norm"] = dkvln.reshape(-1)
    G["mla_q_latent_norm"] = dqln

    send = jnp.concatenate([_to_shards(G[n], n in _COL_SHARDED).reshape(N_DEV, -1, 128) for n in _SHARDED], axis=1)
    parts = _exchange(send, gather=False, name="scatter_grads")
    w_s, m_s, v_s = (_pack([d[n] for n in _SHARDED], 8) for d in (W, M, V))
    res_s = _reduce_adamw(parts, w_s, m_s, v_s, name="adamw_sharded")
    small_send = _pack([G[n] for n in _REPLICATED], 8)
    parts_r = _exchange(small_send, gather=True, name="gather_small_grads")
    w_r, m_r, v_r = (_pack([d[n] for n in _REPLICATED], 8) for d in (W, M, V))
    res_r = _reduce_adamw(parts_r, w_r, m_r, v_r, name="adamw_replicated")

    outs = {}
    for kind, bs, br in zip(("grad", "delta", "new_m", "new_v"), res_s, res_r):
        for n, a in zip(_SHARDED, _unpack(bs, [W[n].shape for n in _SHARDED])):
            outs[kind, n] = a
        for n, a in zip(_REPLICATED, _unpack(br, [W[n].shape for n in _REPLICATED])):
            outs[kind, n] = a
    return (loss, grad_x, *[outs[k, n] for k in ("grad", "delta", "new_m", "new_v") for n in order])
```

```python
import functools
import math

import jax
import jax.numpy as jnp
from jax import lax
from jax.experimental import pallas as pl
from jax.experimental.pallas import tpu as pltpu

f32, bf16 = jnp.float32, jnp.bfloat16
HIGHEST = lax.Precision.HIGHEST
MESH_ID = pl.DeviceIdType.MESH

N_DEV = 8
D_MODEL = 1024
N_META = 16
NORM_EPS = 1e-6
PAD_FRONT = 48
ROW0 = PAD_FRONT + N_META
GDN_QK_HEADS, GDN_V_HEADS, GDN_D = 8, 16, 128
GDN_CHUNK = 64
GDN_QK_W, GDN_V_W = GDN_QK_HEADS * GDN_D, GDN_V_HEADS * GDN_D
GDN_CONV_W = 2 * GDN_QK_W + GDN_V_W
GDN_IN_W = GDN_CONV_W + GDN_V_W + 2 * GDN_V_HEADS
GDN_IN_WP = GDN_CONV_W + GDN_V_W + 2 * 128
MLA_HEADS, MLA_NOPE, MLA_ROPE, MLA_V = 16, 128, 64, 128
MLA_Q_RANK, MLA_KV_RANK = 256, 128
MLA_QK = MLA_NOPE + MLA_ROPE
MLA_QKP = 256
MLA_V_W = MLA_HEADS * MLA_V
ROPE_THETA = 10000.0
NEG = -1e30
ROW_TILE = 256
ATT_TILE = 256

ADAM_LR, ADAM_B1, ADAM_B2, ADAM_EPS, ADAM_WD, ADAM_STEP = 0.001, 0.9, 0.999, 1e-08, 0.01, 10

NN = ((1,), (0,))
NT = ((1,), (1,))
TN = ((0,), (0,))


def _pick(dim, prefs):
    for p in prefs:
        if dim % p == 0:
            return p
    return dim


def _dlo(a, b, dims):
    return lax.dot_general(a.astype(bf16), b.astype(bf16), (dims, ((), ())), preferred_element_type=f32)


def _dhi(a, b, dims):
    return lax.dot_general(a, b, (dims, ((), ())), precision=HIGHEST, preferred_element_type=f32)


def _matmul(a, b, *, ta=False, tb=False, out_dtype=f32, name):
    assert not (ta and tb)
    if ta:
        kdim, m = a.shape
    else:
        m, kdim = a.shape
    n = b.shape[0] if tb else b.shape[1]
    assert (b.shape[1] if tb else b.shape[0]) == kdim
    tm = _pick(m, (768, 512, 384, 256, 128))
    tn = _pick(n, (1024, 768, 640, 512, 256, 128))
    tk = _pick(kdim, (1024, 768, 640, 512, 256, 128))
    nk = kdim // tk
    dims = TN if ta else (NT if tb else NN)

    def body(a_ref, b_ref, o_ref, acc_ref):
        k = pl.program_id(2)

        @pl.when(k == 0)
        def _():
            acc_ref[...] = jnp.zeros_like(acc_ref)

        acc_ref[...] += lax.dot_general(a_ref[...].astype(bf16), b_ref[...].astype(bf16), (dims, ((), ())),
                                        preferred_element_type=f32)

        @pl.when(k == nk - 1)
        def _():
            o_ref[...] = acc_ref[...].astype(o_ref.dtype)

    a_spec = pl.BlockSpec((tk, tm), lambda i, j, k: (k, i)) if ta else pl.BlockSpec((tm, tk), lambda i, j, k: (i, k))
    b_spec = pl.BlockSpec((tn, tk), lambda i, j, k: (j, k)) if tb else pl.BlockSpec((tk, tn), lambda i, j, k: (k, j))
    return pl.pallas_call(
        body, name=name, grid=(m // tm, n // tn, nk),
        in_specs=[a_spec, b_spec], out_specs=pl.BlockSpec((tm, tn), lambda i, j, k: (i, j)),
        out_shape=jax.ShapeDtypeStruct((m, n), out_dtype),
        scratch_shapes=[pltpu.VMEM((tm, tn), f32)],
        compiler_params=pltpu.CompilerParams(dimension_semantics=("parallel", "parallel", "arbitrary")),
    )(a, b)


def _row_spec(item, tr):
    a, bc, off = item
    if bc is None:
        return pl.BlockSpec((tr, a.shape[1]), lambda i, j: (i, 0))
    return pl.BlockSpec((tr, bc), lambda i, j, off=off: (i, j + off))


def _param_spec(p):
    return pl.BlockSpec(p.shape, lambda i, j: (0, 0))


def _row_tile(lp, ncol):
    return ROW_TILE if ncol == 1 else _pick(lp, (768, 512, 256))


def _rowwise(fn, rows, nodiff, params, outs, *, ncol=1, name):
    lp = rows[0][0].shape[0]
    tr = _row_tile(lp, ncol)
    nr, nd = len(rows), len(nodiff)

    def body(*refs):
        rv = [r[...].astype(f32) for r in refs[:nr]]
        nv = [r[...] for r in refs[nr:nr + nd]]
        pv = [r[...] for r in refs[nr + nd:nr + nd + len(params)]]
        res = fn(rv, nv, pv)
        for ref, val in zip(refs[nr + nd + len(params):], res):
            ref[...] = val.astype(ref.dtype)

    out_specs = [pl.BlockSpec((tr, c if bc is None else bc), (lambda i, j: (i, 0)) if bc is None else (lambda i, j: (i, j)))
                 for (c, _, bc) in outs]
    return pl.pallas_call(
        body, name=name, grid=(lp // tr, ncol),
        in_specs=[_row_spec(it, tr) for it in rows + nodiff] + [_param_spec(p) for p in params],
        out_specs=out_specs,
        out_shape=[jax.ShapeDtypeStruct((lp, c), dt) for (c, dt, _) in outs],
    )(*[it[0] for it in rows + nodiff], *params)


def _rowwise_vjp(fn, rows, nodiff, params, cts, *, ncol=1, name, ct_pre=None, extra=None, grad_dtypes=None):
    lp = rows[0][0].shape[0]
    tr = _row_tile(lp, ncol)
    nr, nd, npar, nct = len(rows), len(nodiff), len(params), len(cts)
    extra = extra or [None] * nr
    grad_dtypes = grad_dtypes or [f32] * nr
    ex_items = [(e, rows[k][1], 0) for k, e in enumerate(extra) if e is not None]
    ex_pos = [k for k, e in enumerate(extra) if e is not None]
    for (a, bc, _) in rows:
        assert bc is not None or ncol == 1

    def body(*refs):
        pos = 0
        rv = [r[...].astype(f32) for r in refs[pos:pos + nr]]; pos += nr
        nv = [r[...] for r in refs[pos:pos + nd]]; pos += nd
        pv = [r[...] for r in refs[pos:pos + npar]]; pos += npar
        cv = [r[...].astype(f32) for r in refs[pos:pos + nct]]; pos += nct
        ev = [r[...].astype(f32) for r in refs[pos:pos + len(ex_items)]]; pos += len(ex_items)
        drow_refs = refs[pos:pos + nr]; pos += nr
        dpar_refs = refs[pos:pos + npar]
        outs, vjp_fn = jax.vjp(lambda rr, pp: fn(rr, nv, pp), rv, pv)
        ctv = ct_pre(cv) if ct_pre is not None else cv
        drow, dpar = vjp_fn([c.astype(o.dtype) for c, o in zip(ctv, outs)])
        for k, e in zip(ex_pos, ev):
            drow[k] = drow[k] + e
        for ref, val in zip(drow_refs, drow):
            ref[...] = val.astype(ref.dtype)
        first = jnp.logical_and(pl.program_id(0) == 0, pl.program_id(1) == 0)

        @pl.when(first)
        def _():
            for ref, val in zip(dpar_refs, dpar):
                ref[...] = val

        @pl.when(jnp.logical_not(first))
        def _():
            for ref, val in zip(dpar_refs, dpar):
                ref[...] += val

    drow_shapes, drow_specs = [], []
    for (a, bc, _), dt in zip(rows, grad_dtypes):
        if bc is None:
            drow_shapes.append(jax.ShapeDtypeStruct((lp, a.shape[1]), dt))
            drow_specs.append(pl.BlockSpec((tr, a.shape[1]), lambda i, j: (i, 0)))
        else:
            drow_shapes.append(jax.ShapeDtypeStruct((lp, ncol * bc), dt))
            drow_specs.append(pl.BlockSpec((tr, bc), lambda i, j: (i, j)))
    res = pl.pallas_call(
        body, name=name, grid=(lp // tr, ncol),
        in_specs=[_row_spec(it, tr) for it in rows + nodiff] + [_param_spec(p) for p in params]
        + [_row_spec(it, tr) for it in cts + ex_items],
        out_specs=drow_specs + [_param_spec(p) for p in params],
        out_shape=drow_shapes + [jax.ShapeDtypeStruct(p.shape, f32) for p in params],
        compiler_params=pltpu.CompilerParams(dimension_semantics=("arbitrary", "arbitrary")),
    )(*[it[0] for it in rows + nodiff], *params, *[it[0] for it in cts + ex_items])
    return res[:nr], res[nr:]


def _rms(x, g):
    return x * lax.rsqrt(jnp.mean(x * x, axis=-1, keepdims=True) + NORM_EPS) * g


def _l2n(x):
    return x * lax.rsqrt(jnp.sum(x * x, axis=-1, keepdims=True) + NORM_EPS)


def _sigmoid(x):
    return 1.0 / (1.0 + jnp.exp(-x))


def _silu(x):
    return x * _sigmoid(x)


def _softplus(x):
    return jnp.maximum(x, 0.0) + jnp.log(1.0 + jnp.exp(-jnp.abs(x)))


def _row_ids(shape, tr):
    return pl.program_id(0) * tr + lax.broadcasted_iota(jnp.int32, shape, 0)


def _st_prenorm(r, n, p):
    return [_rms(r[0], p[0])]


def _st_gdn_q(r, n, p):
    return [_l2n(_silu(r[0])) * (GDN_D ** -0.5)]


def _st_gdn_k(r, n, p):
    return [_l2n(_silu(r[0]))]


def _st_gdn_v(r, n, p):
    return [_silu(r[0])]


def _st_gdn_gate(r, n, p):
    real = _row_ids(r[0].shape, ROW_TILE) >= PAD_FRONT
    beta = jnp.where(real, _sigmoid(r[0]), 0.0)
    g = jnp.where(real, -jnp.exp(p[0]) * _softplus(r[1] + p[1]), 0.0)
    return [beta, g]


def _st_gdn_out(r, n, p):
    return [_rms(r[0], p[0]) * _silu(r[1])]


def _st_mid(r, n, p):
    h1 = r[0] + _rms(r[1], p[0])
    return [h1, _rms(h1, p[1]), _rms(h1, p[2])]


def _st_latent(r, n, p):
    ckr, cq = r
    c_kv = _rms(ckr[:, :MLA_KV_RANK], p[0])
    k_rope = ckr[:, 128:256] * n[0] + ckr[:, 384:512] * n[1]
    return [c_kv, k_rope, _rms(cq, p[1])]


def _st_q_rope(r, n, p):
    return [(r[0] * n[0] + r[1] * n[1]) * (MLA_QK ** -0.5)]


def _st_gate(r, n, p):
    return [r[0] * _silu(r[1])]


def _make_st_loss(n_tokens):
    def st(r, n, p):
        h2 = r[0] + _rms(r[1], p[0])
        rows = _row_ids((r[0].shape[0], 1), ROW_TILE)
        real = jnp.logical_and(rows >= ROW0, rows < ROW0 + n_tokens)
        err = h2 - n[0]
        return [jnp.where(real, 0.5 * jnp.mean(err * err, axis=-1, keepdims=True), 0.0)]
    return st


CONV_BC = 512


def _conv_fwd(x, w, *, col_blocks, name, tr=ROW_TILE):
    lp = x.shape[0]

    def body(x_ref, xp_ref, w_ref, o_ref):
        i = pl.program_id(0)
        prev = jnp.where(i > 0, xp_ref[...], 0.0)
        xc = jnp.concatenate([prev, x_ref[...]], axis=0)
        wv = w_ref[...]
        acc = wv[3:4, :] * x_ref[...]
        for j in range(3):
            acc = acc + wv[j:j + 1, :] * pltpu.roll(xc, 3 - j, 0)[8:, :]
        o_ref[...] = acc

    return pl.pallas_call(
        body, name=name, grid=(lp // tr, col_blocks),
        in_specs=[pl.BlockSpec((tr, CONV_BC), lambda i, j: (i, j)),
                  pl.BlockSpec((8, CONV_BC), lambda i, j: (jnp.maximum(i * (tr // 8) - 1, 0), j)),
                  pl.BlockSpec((4, CONV_BC), lambda i, j: (0, j))],
        out_specs=pl.BlockSpec((tr, CONV_BC), lambda i, j: (i, j)),
        out_shape=jax.ShapeDtypeStruct((lp, col_blocks * CONV_BC), f32),
    )(x, x, w)


def _conv_bwd(dc, x, w, *, x_off, w_off, name, tr=ROW_TILE):
    lp, width = dc.shape
    ncb, nrow = width // CONV_BC, lp // tr

    def body(dc_ref, dcn_ref, x_ref, xp_ref, w_ref, dx_ref, dw_ref):
        i = pl.program_id(1)
        nxt = jnp.where(i < nrow - 1, dcn_ref[...], 0.0)
        dcv = dc_ref[...]
        dcc = jnp.concatenate([dcv, nxt], axis=0)
        prev = jnp.where(i > 0, xp_ref[...], 0.0)
        xc = jnp.concatenate([prev, x_ref[...]], axis=0)
        wv = w_ref[...]
        dx = wv[3:4, :] * dcv
        dws = [None] * 4
        dws[3] = jnp.sum(dcv * x_ref[...], axis=0, keepdims=True)
        for j in range(3):
            dx = dx + wv[j:j + 1, :] * pltpu.roll(dcc, tr + 8 - (3 - j), 0)[:tr, :]
            dws[j] = jnp.sum(dcv * pltpu.roll(xc, 3 - j, 0)[8:, :], axis=0, keepdims=True)
        dx_ref[...] = dx.astype(dx_ref.dtype)

        @pl.when(i == 0)
        def _():
            for j in range(4):
                dw_ref[j:j + 1, :] = dws[j]

        @pl.when(i > 0)
        def _():
            for j in range(4):
                dw_ref[j:j + 1, :] += dws[j]

    last8 = lp // 8 - 1
    return pl.pallas_call(
        body, name=name, grid=(ncb, nrow),
        in_specs=[pl.BlockSpec((tr, CONV_BC), lambda j, i: (i, j)),
                  pl.BlockSpec((8, CONV_BC), lambda j, i: (jnp.minimum((i + 1) * (tr // 8), last8), j)),
                  pl.BlockSpec((tr, CONV_BC), lambda j, i: (i, j + x_off)),
                  pl.BlockSpec((8, CONV_BC), lambda j, i: (jnp.maximum(i * (tr // 8) - 1, 0), j + x_off)),
                  pl.BlockSpec((4, CONV_BC), lambda j, i: (0, j + w_off))],
        out_specs=[pl.BlockSpec((tr, CONV_BC), lambda j, i: (i, j)),
                   pl.BlockSpec((4, CONV_BC), lambda j, i: (0, j))],
        out_shape=[jax.ShapeDtypeStruct((lp, width), bf16), jax.ShapeDtypeStruct((4, width), f32)],
        compiler_params=pltpu.CompilerParams(dimension_semantics=("arbitrary", "arbitrary")),
    )(dc, dc, x, x, w)


GDN_PACK = 4


def _bd(x, cb):
    r = x.shape[0]
    tall = jnp.concatenate([x] * GDN_PACK, axis=0)
    rows = lax.broadcasted_iota(jnp.int32, tall.shape, 0) // r
    cols = lax.broadcasted_iota(jnp.int32, tall.shape, 1) // cb
    return jnp.where(rows == cols, tall, jnp.zeros_like(tall))


def _diag(full, r, cb):
    cols = lax.broadcasted_iota(jnp.int32, (r, full.shape[1]), 1) // cb
    out = jnp.where(cols == 0, full[0:r, :], 0.0)
    for a in range(1, GDN_PACK):
        out = out + jnp.where(cols == a, full[a * r:(a + 1) * r, :], 0.0)
    return out


def _stack(x, cb):
    return jnp.concatenate([x[:, a * cb:(a + 1) * cb] for a in range(GDN_PACK)], axis=0)


def _make_packed(dot):
    @jax.custom_vjp
    def pmm(x, y):
        return dot(x, _bd(y, y.shape[1] // GDN_PACK), NN)

    @jax.custom_vjp
    def pnt(x, y):
        k = x.shape[1] // GDN_PACK
        return _diag(dot(_stack(x, k), _stack(y, k), NT), x.shape[0], y.shape[0])

    @jax.custom_vjp
    def ptn(x, y):
        return _diag(dot(x, y, TN), x.shape[1] // GDN_PACK, y.shape[1] // GDN_PACK)

    def pmm_bwd(res, ct):
        x, y = res
        cb = y.shape[1] // GDN_PACK
        return dot(ct, _bd(y, cb), NT), _diag(dot(x, ct, TN), y.shape[0], cb)

    pmm.defvjp(lambda x, y: (pmm(x, y), (x, y)), pmm_bwd)
    pnt.defvjp(lambda x, y: (pnt(x, y), (x, y)), lambda res, ct: (pmm(ct, res[1]), ptn(ct, res[0])))
    ptn.defvjp(lambda x, y: (ptn(x, y), (x, y)), lambda res, ct: (pnt(res[1], ct), pmm(res[0], ct)))
    return pmm, pnt, ptn


_pmm, _pnt, _ptn = _make_packed(_dlo)


@jax.custom_vjp
def _inv_packed(m):
    c = m.shape[0]
    ii = lax.broadcasted_iota(jnp.int32, m.shape, 0)
    jj = lax.broadcasted_iota(jnp.int32, m.shape, 1) % c
    t = jnp.where(ii == jj, 1.0, 0.0) - m
    p = (-m).astype(bf16)
    for _ in range(int(math.log2(c)) - 1):
        p = _dlo(p, _bd(p, c), NN).astype(bf16)
        t = t + _dlo(t, _bd(p, c), NN)
    return t


def _inv_packed_fwd(m):
    t = _inv_packed(m)
    return t, t


def _inv_packed_bwd(t, ct):
    c = t.shape[0]
    return (-_dlo(_diag(_dlo(t, ct, TN), c, c), _bd(t.astype(bf16), c), NT),)


_inv_packed.defvjp(_inv_packed_fwd, _inv_packed_bwd)


def _gdn_group(s, q2, k2, v4, bcols, gcols, grows):
    c, d = v4.shape[0], GDN_D
    q4 = jnp.concatenate([q2[:, :d], q2[:, :d], q2[:, d:], q2[:, d:]], axis=1)
    k4 = jnp.concatenate([k2[:, :d], k2[:, :d], k2[:, d:], k2[:, d:]], axis=1)
    beta4 = jnp.concatenate([jnp.broadcast_to(b, (c, d)) for b in bcols], axis=1)
    gc4 = jnp.concatenate([jnp.broadcast_to(g, (c, d)) for g in gcols], axis=1)
    low = lax.broadcasted_iota(jnp.int32, (c, 128), 1) < c
    gi = jnp.concatenate([jnp.where(low, gcols[0], gcols[1]), jnp.where(low, gcols[2], gcols[3])], axis=1)
    gj = jnp.concatenate([jnp.where(low, grows[0], grows[1]), jnp.where(low, grows[2], grows[3])], axis=1)
    ii = lax.broadcasted_iota(jnp.int32, gi.shape, 0)
    jj = lax.broadcasted_iota(jnp.int32, gi.shape, 1) % c
    dec = jnp.exp(jnp.where(ii >= jj, gi - gj, NEG))
    dec_strict = jnp.where(ii > jj, dec, 0.0)
    rid = lax.broadcasted_iota(jnp.int32, gc4.shape, 0)
    glast = jnp.sum(jnp.where(rid == c - 1, gc4, 0.0), axis=0, keepdims=True)
    eg = jnp.exp(gc4)
    kb = k4 * beta4
    t = _inv_packed(_pnt(kb, k4) * dec_strict)
    u = _pmm(t, v4 * beta4)
    w = _pmm(t, kb * eg)
    attn = _pnt(q4, k4) * dec
    ws_qs = _pmm(jnp.concatenate([w, q4 * eg], axis=0), s)
    v_new = u - ws_qs[:c]
    o = ws_qs[c:] + _pmm(attn, v_new)
    s_new = s * jnp.exp(glast) + _ptn(k4 * jnp.exp(glast - gc4), v_new)
    return o, s_new


def _lane_pick(x, h):
    lane = lax.broadcasted_iota(jnp.int32, x.shape, 1)
    return jnp.sum(jnp.where(lane == h, x, 0.0), axis=1, keepdims=True)


def _cum_log_decay(g):
    c = g.shape[0]
    lower = (lax.broadcasted_iota(jnp.int32, (c, c), 0) >= lax.broadcasted_iota(jnp.int32, (c, c), 1)).astype(f32)
    upper2 = (lax.broadcasted_iota(jnp.int32, (c, 128), 0) <= lax.broadcasted_iota(jnp.int32, (c, 128), 1) % c).astype(f32)
    return _dhi(lower, g, NN), _dhi(g, upper2, TN)


def _group_operands(gi, s_ref, q_ref, k_ref, v_ref, bv, gcv, gct_s):
    heads = [gi * GDN_PACK + u for u in range(GDN_PACK)]
    qk_off = pl.multiple_of(gi * 2 * GDN_D, 2 * GDN_D)
    v_off = pl.multiple_of(gi * GDN_PACK * GDN_D, GDN_PACK * GDN_D)
    return (s_ref[gi], q_ref[:, pl.ds(qk_off, 2 * GDN_D)], k_ref[:, pl.ds(qk_off, 2 * GDN_D)],
            v_ref[:, pl.ds(v_off, GDN_PACK * GDN_D)],
            [_lane_pick(bv, h) for h in heads], [_lane_pick(gcv, h) for h in heads],
            [gct_s[pl.ds(h, 1), :] for h in heads]), heads, qk_off, v_off


def _gdn_fwd(qn, kn, v, beta, g, *, n_real_chunks, name):
    lp = qn.shape[0]
    nchunk = lp // GDN_CHUNK
    C, D = GDN_CHUNK, GDN_D
    NG, SW = GDN_V_HEADS // GDN_PACK, GDN_PACK * GDN_D

    def body(q_ref, k_ref, v_ref, b_ref, g_ref, o_ref, st_ref, s_s, gc_s, gct_s):
        ci = pl.program_id(0)

        @pl.when(ci == 0)
        def _():
            s_s[...] = jnp.zeros_like(s_s)

        @pl.when(ci >= n_real_chunks)
        def _():
            o_ref[...] = jnp.zeros_like(o_ref)
            st_ref[...] = jnp.zeros_like(st_ref)

        @pl.when(ci < n_real_chunks)
        def _():
            gc, gct = _cum_log_decay(g_ref[...])
            gc_s[...] = gc
            gct_s[...] = gct

            def group(gi, carry):
                args, _, _, v_off = _group_operands(gi, s_s, q_ref, k_ref, v_ref, b_ref[...], gc_s[...], gct_s)
                o, s_new = _gdn_group(*args)
                st_ref[gi] = args[0]
                s_s[gi] = s_new
                o_ref[:, pl.ds(v_off, SW)] = o
                return carry

            lax.fori_loop(0, NG, group, 0)

    return pl.pallas_call(
        body, name=name, grid=(nchunk,),
        in_specs=[pl.BlockSpec((C, GDN_QK_W), lambda c: (c, 0)), pl.BlockSpec((C, GDN_QK_W), lambda c: (c, 0)),
                  pl.BlockSpec((C, GDN_V_W), lambda c: (c, 0)), pl.BlockSpec((C, 128), lambda c: (c, 0)),
                  pl.BlockSpec((C, 128), lambda c: (c, 0))],
        out_specs=[pl.BlockSpec((C, GDN_V_W), lambda c: (c, 0)),
                   pl.BlockSpec((None, NG, D, SW), lambda c: (c, 0, 0, 0))],
        out_shape=[jax.ShapeDtypeStruct((lp, GDN_V_W), f32), jax.ShapeDtypeStruct((nchunk, NG, D, SW), f32)],
        scratch_shapes=[pltpu.VMEM((NG, D, SW), f32), pltpu.VMEM((C, 128), f32), pltpu.VMEM((128, 128), f32)],
        compiler_params=pltpu.CompilerParams(dimension_semantics=("arbitrary",)),
    )(qn, kn, v, beta, g)


def _gdn_bwd(qn, kn, v, beta, g, states, do, *, n_real_chunks, name):
    lp = qn.shape[0]
    nchunk = lp // GDN_CHUNK
    C, D = GDN_CHUNK, GDN_D
    NG, SW = GDN_V_HEADS // GDN_PACK, GDN_PACK * GDN_D
    rev = lambda i: (nchunk - 1 - i, 0)

    def body(q_ref, k_ref, v_ref, b_ref, g_ref, st_ref, do_ref,
             dq_ref, dk_ref, dv_ref, db_ref, dg_ref, ds_s, gc_s, gct_s, dgc_s, dgct_s, dbeta_s):
        step = pl.program_id(0)
        ci = nchunk - 1 - step

        @pl.when(step == 0)
        def _():
            ds_s[...] = jnp.zeros_like(ds_s)

        @pl.when(ci >= n_real_chunks)
        def _():
            for r in (dq_ref, dk_ref, dv_ref, db_ref, dg_ref):
                r[...] = jnp.zeros_like(r)

        @pl.when(ci < n_real_chunks)
        def _():
            gc, gct = _cum_log_decay(g_ref[...])
            gc_s[...] = gc
            gct_s[...] = gct
            dgc_s[...] = jnp.zeros_like(dgc_s)
            dgct_s[...] = jnp.zeros_like(dgct_s)
            dbeta_s[...] = jnp.zeros_like(dbeta_s)

            def group(gi, carry):
                args, heads, qk_off, v_off = _group_operands(gi, st_ref, q_ref, k_ref, v_ref, b_ref[...], gc_s[...], gct_s)
                _, vjp_fn = jax.vjp(_gdn_group, *args)
                dsp, dq2, dk2, dv4, dbcols, dgcols, dgrows = vjp_fn((do_ref[:, pl.ds(v_off, SW)], ds_s[gi]))
                ds_s[gi] = dsp
                dq_ref[:, pl.ds(qk_off, 2 * D)] = dq2
                dk_ref[:, pl.ds(qk_off, 2 * D)] = dk2
                dv_ref[:, pl.ds(v_off, SW)] = dv4
                lane = lax.broadcasted_iota(jnp.int32, (C, 128), 1)
                dbeta_acc, dgc_acc = dbeta_s[...], dgc_s[...]
                for h, dbcol, dgcol, dgrow in zip(heads, dbcols, dgcols, dgrows):
                    dbeta_acc = dbeta_acc + jnp.where(lane == h, dbcol, 0.0)
                    dgc_acc = dgc_acc + jnp.where(lane == h, dgcol, 0.0)
                    dgct_s[pl.ds(h, 1), :] = dgrow
                dbeta_s[...] = dbeta_acc
                dgc_s[...] = dgc_acc
                return carry

            lax.fori_loop(0, NG, group, 0)
            fold = (lax.broadcasted_iota(jnp.int32, (128, C), 0) % C == lax.broadcasted_iota(jnp.int32, (128, C), 1)).astype(f32)
            eye = (lax.broadcasted_iota(jnp.int32, (128, 128), 0) == lax.broadcasted_iota(jnp.int32, (128, 128), 1)).astype(f32)
            dgc = dgc_s[...] + _dhi(_dhi(dgct_s[...], fold, NN), eye, TN)
            upper = (lax.broadcasted_iota(jnp.int32, (C, C), 0) <= lax.broadcasted_iota(jnp.int32, (C, C), 1)).astype(f32)
            dg_ref[...] = _dhi(upper, dgc, NN)
            db_ref[...] = dbeta_s[...]

    return pl.pallas_call(
        body, name=name, grid=(nchunk,),
        in_specs=[pl.BlockSpec((C, GDN_QK_W), rev), pl.BlockSpec((C, GDN_QK_W), rev), pl.BlockSpec((C, GDN_V_W), rev),
                  pl.BlockSpec((C, 128), rev), pl.BlockSpec((C, 128), rev),
                  pl.BlockSpec((None, NG, D, SW), lambda i: (nchunk - 1 - i, 0, 0, 0)), pl.BlockSpec((C, GDN_V_W), rev)],
        out_specs=[pl.BlockSpec((C, GDN_QK_W), rev), pl.BlockSpec((C, GDN_QK_W), rev), pl.BlockSpec((C, GDN_V_W), rev),
                   pl.BlockSpec((C, 128), rev), pl.BlockSpec((C, 128), rev)],
        out_shape=[jax.ShapeDtypeStruct((lp, GDN_QK_W), f32)] * 2 + [jax.ShapeDtypeStruct((lp, GDN_V_W), f32)]
        + [jax.ShapeDtypeStruct((lp, 128), f32)] * 2,
        scratch_shapes=[pltpu.VMEM((NG, D, SW), f32), pltpu.VMEM((C, 128), f32), pltpu.VMEM((128, 128), f32),
                        pltpu.VMEM((C, 128), f32), pltpu.VMEM((128, 128), f32), pltpu.VMEM((C, 128), f32)],
        compiler_params=pltpu.CompilerParams(dimension_semantics=("arbitrary",)),
    )(qn, kn, v, beta, g, states, do)


def _att_mask_t(k0, q0, tk, tq):
    kcol = k0 + lax.broadcasted_iota(jnp.int32, (tk, tq), 0)
    qrow = q0 + lax.broadcasted_iota(jnp.int32, (tk, tq), 1)
    return jnp.logical_and(qrow >= kcol, kcol >= PAD_FRONT)


def _attention_fwd(q, kvu, kr, *, name, tk=ATT_TILE):
    lp = q.shape[0]
    H = MLA_HEADS
    tq = _pick(lp, (768, 512, 256))
    r = tq // tk

    def body(q_ref, kn_ref, kr_ref, v_ref, o_ref, lse_ref, m_s, l_s, acc_s):
        qi = pl.program_id(1)
        qv = q_ref[...]
        m_s[...] = jnp.full_like(m_s, NEG)
        l_s[...] = jnp.zeros_like(l_s)
        acc_s[...] = jnp.zeros_like(acc_s)

        def step(ki, masked):
            k0 = pl.multiple_of(ki * tk, tk)
            k = jnp.concatenate([kn_ref[pl.ds(k0, tk), :], kr_ref[pl.ds(k0, tk), :]], axis=1)
            st = lax.dot_general(k, qv, (NT, ((), ())), preferred_element_type=f32)
            if masked:
                st = jnp.where(_att_mask_t(k0, qi * tq, tk, tq), st, NEG)
            m_prev = m_s[...]
            m_new = jnp.maximum(m_prev, jnp.max(st, axis=0, keepdims=True))
            alpha = jnp.exp(m_prev - m_new)
            p = jnp.exp(st - m_new)
            l_s[...] = alpha * l_s[...] + jnp.sum(p, axis=0, keepdims=True)
            acc_s[...] = alpha * acc_s[...] + lax.dot_general(v_ref[pl.ds(k0, tk), :], p.astype(bf16), (TN, ((), ())),
                                                              preferred_element_type=f32)
            m_s[...] = m_new

        n_full = qi * r

        @pl.when(qi > 0)
        def _():
            step(0, True)

        def plain(ki, carry):
            step(ki, False)
            return carry

        lax.fori_loop(1, n_full, plain, 0)
        for d in range(r):
            step(n_full + d, True)
        o_ref[...] = jnp.transpose(acc_s[...] / l_s[...])
        lse_ref[...] = m_s[...] + jnp.log(l_s[...])

    return pl.pallas_call(
        body, name=name, grid=(H, lp // tq),
        in_specs=[pl.BlockSpec((tq, MLA_QKP), lambda h, qi: (qi, h)),
                  pl.BlockSpec((lp, 128), lambda h, qi: (0, h)),
                  pl.BlockSpec((lp, 128), lambda h, qi: (0, 0)),
                  pl.BlockSpec((lp, 128), lambda h, qi: (0, H + h))],
        out_specs=[pl.BlockSpec((tq, 128), lambda h, qi: (qi, h)),
                   pl.BlockSpec((None, 1, tq), lambda h, qi: (h, 0, qi))],
        out_shape=[jax.ShapeDtypeStruct((lp, MLA_V_W), f32), jax.ShapeDtypeStruct((H, 1, lp), f32)],
        scratch_shapes=[pltpu.VMEM((1, tq), f32), pltpu.VMEM((1, tq), f32), pltpu.VMEM((128, tq), f32)],
        compiler_params=pltpu.CompilerParams(dimension_semantics=("arbitrary", "arbitrary")),
    )(q, kvu, kr, kvu)


def _attention_delta(o, do, *, name):
    lp = o.shape[0]
    H = MLA_HEADS
    tq = _pick(lp, (768, 512, 256))

    def body(o_ref, do_ref, d_ref):
        prod = o_ref[...] * do_ref[...].astype(f32)
        d_ref[...] = jnp.sum(jnp.transpose(prod), axis=0, keepdims=True)

    return pl.pallas_call(
        body, name=name, grid=(H, lp // tq),
        in_specs=[pl.BlockSpec((tq, 128), lambda h, qi: (qi, h)), pl.BlockSpec((tq, 128), lambda h, qi: (qi, h))],
        out_specs=pl.BlockSpec((None, 1, tq), lambda h, qi: (h, 0, qi)),
        out_shape=jax.ShapeDtypeStruct((H, 1, lp), f32),
    )(o, do)


def _attention_bwd(q, kvu, kr, lse, delta, do, *, name, t=ATT_TILE):
    lp = q.shape[0]
    H, nb = MLA_HEADS, lp // t
    tq = _pick(lp, (768, 512, 256))
    r, nq = tq // t, lp // tq

    def body(q_ref, kn_ref, kr_ref, v_ref, lse_ref, dl_ref, do_ref, dq_ref, dkn_ref, dv_ref, dkr_ref, dk_s, dv_s):
        ki = pl.program_id(1)
        k0 = ki * t
        k = jnp.concatenate([kn_ref[...], kr_ref[...]], axis=1)
        vv = v_ref[...]
        dk_s[...] = jnp.zeros_like(dk_s)
        dv_s[...] = jnp.zeros_like(dv_s)

        def pair(qi, masked, first):
            q0 = pl.multiple_of(qi * tq, tq)
            qv = q_ref[pl.ds(q0, tq), :]
            dob = do_ref[pl.ds(q0, tq), :]
            st = lax.dot_general(k, qv, (NT, ((), ())), preferred_element_type=f32)
            if masked:
                st = jnp.where(_att_mask_t(k0, q0, t, tq), st, NEG)
            p = jnp.exp(st - lse_ref[:, pl.ds(q0, tq)])
            dv_s[...] += jnp.dot(p.astype(bf16), dob, preferred_element_type=f32)
            dp = lax.dot_general(vv, dob, (NT, ((), ())), preferred_element_type=f32)
            ds = (p * (dp - dl_ref[:, pl.ds(q0, tq)])).astype(bf16)
            dk_s[...] += jnp.dot(ds, qv, preferred_element_type=f32)
            dq = lax.dot_general(ds, k, (TN, ((), ())), preferred_element_type=f32)
            if first:
                dq_ref[pl.ds(q0, tq), :] = dq
            else:
                dq_ref[pl.ds(q0, tq), :] += dq

        @pl.when(ki == 0)
        def _():
            def every(qi, carry):
                pair(qi, True, True)
                return carry
            lax.fori_loop(0, nq, every, 0)

        @pl.when(ki > 0)
        def _():
            pair(ki // r, True, False)

            def below(qi, carry):
                pair(qi, False, False)
                return carry
            lax.fori_loop(ki // r + 1, nq, below, 0)

        dkn_ref[...] = dk_s[:, :128].astype(dkn_ref.dtype)
        dkr_ref[...] = dk_s[:, 128:]
        dv_ref[...] = dv_s[...].astype(dv_ref.dtype)

    return pl.pallas_call(
        body, name=name, grid=(H, nb),
        in_specs=[pl.BlockSpec((lp, MLA_QKP), lambda h, ki: (0, h)),
                  pl.BlockSpec((t, 128), lambda h, ki: (ki, h)),
                  pl.BlockSpec((t, 128), lambda h, ki: (ki, 0)),
                  pl.BlockSpec((t, 128), lambda h, ki: (ki, H + h)),
                  pl.BlockSpec((None, 1, lp), lambda h, ki: (h, 0, 0)),
                  pl.BlockSpec((None, 1, lp), lambda h, ki: (h, 0, 0)),
                  pl.BlockSpec((lp, 128), lambda h, ki: (0, h))],
        out_specs=[pl.BlockSpec((lp, MLA_QKP), lambda h, ki: (0, h)),
                   pl.BlockSpec((t, 128), lambda h, ki: (ki, h)),
                   pl.BlockSpec((t, 128), lambda h, ki: (ki, h)),
                   pl.BlockSpec((t, 128), lambda h, ki: (ki, h))],
        out_shape=[jax.ShapeDtypeStruct((lp, H * MLA_QKP), f32), jax.ShapeDtypeStruct((lp, MLA_V_W), bf16),
                   jax.ShapeDtypeStruct((lp, MLA_V_W), bf16), jax.ShapeDtypeStruct((lp, MLA_V_W), f32)],
        scratch_shapes=[pltpu.VMEM((t, MLA_QKP), f32), pltpu.VMEM((t, 128), f32)],
        compiler_params=pltpu.CompilerParams(dimension_semantics=("arbitrary", "arbitrary")),
    )(q, kvu, kr, kvu, lse, delta, do)


def _exchange(x, *, gather, name):
    blk = x.shape if gather else x.shape[1:]

    def body(x_ref, o_ref, send_sems, recv_sems, local_sem):
        mx, my, mc = lax.axis_index("x"), lax.axis_index("y"), lax.axis_index("c")
        me = 4 * mx + 2 * my + mc
        own = pltpu.make_async_copy(x_ref if gather else x_ref.at[me], o_ref.at[me], local_sem)
        own.start()
        sends, peers = [], []
        for k in range(1, N_DEV):
            px = 1 - mx if k & 4 else mx
            py = 1 - my if k & 2 else my
            pc = 1 - mc if k & 1 else mc
            peer = 4 * px + 2 * py + pc
            cp = pltpu.make_async_remote_copy(
                src_ref=x_ref if gather else x_ref.at[peer], dst_ref=o_ref.at[me],
                send_sem=send_sems.at[k - 1], recv_sem=recv_sems.at[k - 1],
                device_id=(px, py, pc), device_id_type=MESH_ID)
            cp.start()
            sends.append(cp)
            peers.append(peer)
        for k in range(1, N_DEV):
            pltpu.make_async_remote_copy(
                src_ref=o_ref.at[peers[k - 1]], dst_ref=o_ref.at[peers[k - 1]],
                send_sem=send_sems.at[k - 1], recv_sem=recv_sems.at[k - 1],
                device_id=(mx, my, mc), device_id_type=MESH_ID).wait_recv()
        for cp in sends:
            cp.wait_send()
        own.wait()

    return pl.pallas_call(
        body, name=name,
        in_specs=[pl.BlockSpec(memory_space=pltpu.HBM)], out_specs=pl.BlockSpec(memory_space=pltpu.HBM),
        out_shape=jax.ShapeDtypeStruct((N_DEV,) + tuple(blk), x.dtype),
        scratch_shapes=[pltpu.SemaphoreType.DMA((N_DEV - 1,)), pltpu.SemaphoreType.DMA((N_DEV - 1,)),
                        pltpu.SemaphoreType.DMA],
    )(x)


def _reduce_adamw(parts, w, m, v, *, name):
    r = w.shape[0]
    tr = _pick(r, (1280, 1024, 512, 256, 128, 64, 48, 32, 16, 8))

    def body(p_ref, w_ref, m_ref, v_ref, g_ref, d_ref, nm_ref, nv_ref):
        g = p_ref[0]
        for s in range(1, N_DEV):
            g = g + p_ref[s]
        mm = ADAM_B1 * m_ref[...] + (1.0 - ADAM_B1) * g
        vv = ADAM_B2 * v_ref[...] + (1.0 - ADAM_B2) * (g * g)
        m_hat = mm / (1.0 - ADAM_B1 ** ADAM_STEP)
        v_hat = vv / (1.0 - ADAM_B2 ** ADAM_STEP)
        g_ref[...] = g
        d_ref[...] = -ADAM_LR * (m_hat / (jnp.sqrt(v_hat) + ADAM_EPS) + ADAM_WD * w_ref[...])
        nm_ref[...] = mm
        nv_ref[...] = vv

    spec = pl.BlockSpec((tr, 128), lambda i: (i, 0))
    return pl.pallas_call(
        body, name=name, grid=(r // tr,),
        in_specs=[pl.BlockSpec((N_DEV, tr, 128), lambda i: (0, i, 0)), spec, spec, spec],
        out_specs=[spec] * 4, out_shape=[jax.ShapeDtypeStruct((r, 128), f32)] * 4,
    )(parts, w, m, v)


_SHARDED = ("gdn_w_in", "gdn_w_out", "kv_w_down", "kv_w_up", "mla_w_in", "mla_w_q_up", "mla_w_out", "meta_tokens", "gdn_conv_w")
_COL_SHARDED = {"gdn_w_in", "kv_w_up", "mla_w_in", "mla_w_q_up", "meta_tokens", "gdn_conv_w"}
_N_BF16 = 7
_REPLICATED = ("pre_norm", "post_norm", "gdn_a_log", "gdn_dt_bias", "gdn_out_norm", "kv_norm", "kv_latent_norm",
               "mla_q_latent_norm")


def _rows128(a):
    flat = a.reshape(-1)
    pad = (-flat.shape[0]) % 128
    if pad:
        flat = jnp.pad(flat, (0, pad))
    return flat.reshape(-1, 128)


def _pack(arrs, row_multiple):
    parts = [_rows128(a) for a in arrs]
    buf = jnp.concatenate(parts, axis=0)
    pad = (-buf.shape[0]) % row_multiple
    if pad:
        buf = jnp.pad(buf, ((0, pad), (0, 0)))
    return buf


def _unpack(buf, shapes):
    out, r = [], 0
    for shp in shapes:
        n = math.prod(shp)
        rows = -(-n // 128)
        out.append(buf[r:r + rows].reshape(-1)[:n].reshape(shp))
        r += rows
    return out


def _unshard(g, full_shape, col):
    if col:
        return jnp.transpose(g, (1, 0, 2)).reshape(full_shape)
    return g.reshape(full_shape)


def _to_shards(a, col):
    r, c = a.shape
    if col:
        return jnp.transpose(a.reshape(r, N_DEV, c // N_DEV), (1, 0, 2))
    return a.reshape(N_DEV, r // N_DEV, c)


def _pad_cols(a, width):
    return jnp.pad(a, ((0, 0), (0, width - a.shape[1])))


def _rope_tables(lp):
    inv = ROPE_THETA ** (-jnp.arange(0, MLA_ROPE, 2, dtype=f32) / MLA_ROPE)
    pos = (jnp.arange(lp, dtype=jnp.int32) - PAD_FRONT).astype(f32)
    ang = pos[:, None] * inv[None, :]
    cos, sin = jnp.cos(ang), jnp.sin(ang)
    z = jnp.zeros((lp, 64), f32)
    return jnp.concatenate([cos, cos, z], axis=1), jnp.concatenate([-sin, sin, z], axis=1)


def kernel(x, meta_tokens, pre_norm, post_norm, gdn_w_in, gdn_conv_w, gdn_a_log, gdn_dt_bias, gdn_out_norm, gdn_w_out, kv_norm, kv_w_down, kv_latent_norm, kv_w_up, mla_w_in, mla_q_latent_norm, mla_w_q_up, mla_w_out, loss_target, m_meta_tokens, m_pre_norm, m_post_norm, m_gdn_w_in, m_gdn_conv_w, m_gdn_a_log, m_gdn_dt_bias, m_gdn_out_norm, m_gdn_w_out, m_kv_norm, m_kv_w_down, m_kv_latent_norm, m_kv_w_up, m_mla_w_in, m_mla_q_latent_norm, m_mla_w_q_up, m_mla_w_out, v_meta_tokens, v_pre_norm, v_post_norm, v_gdn_w_in, v_gdn_conv_w, v_gdn_a_log, v_gdn_dt_bias, v_gdn_out_norm, v_gdn_w_out, v_kv_norm, v_kv_w_down, v_kv_latent_norm, v_kv_w_up, v_mla_w_in, v_mla_q_latent_norm, v_mla_w_q_up, v_mla_w_out):
    W = dict(meta_tokens=meta_tokens, pre_norm=pre_norm, post_norm=post_norm, gdn_w_in=gdn_w_in, gdn_conv_w=gdn_conv_w,
             gdn_a_log=gdn_a_log, gdn_dt_bias=gdn_dt_bias, gdn_out_norm=gdn_out_norm, gdn_w_out=gdn_w_out, kv_norm=kv_norm,
             kv_w_down=kv_w_down, kv_latent_norm=kv_latent_norm, kv_w_up=kv_w_up, mla_w_in=mla_w_in,
             mla_q_latent_norm=mla_q_latent_norm, mla_w_q_up=mla_w_q_up, mla_w_out=mla_w_out)
    M = dict(meta_tokens=m_meta_tokens, pre_norm=m_pre_norm, post_norm=m_post_norm, gdn_w_in=m_gdn_w_in, gdn_conv_w=m_gdn_conv_w,
             gdn_a_log=m_gdn_a_log, gdn_dt_bias=m_gdn_dt_bias, gdn_out_norm=m_gdn_out_norm, gdn_w_out=m_gdn_w_out, kv_norm=m_kv_norm,
             kv_w_down=m_kv_w_down, kv_latent_norm=m_kv_latent_norm, kv_w_up=m_kv_w_up, mla_w_in=m_mla_w_in,
             mla_q_latent_norm=m_mla_q_latent_norm, mla_w_q_up=m_mla_w_q_up, mla_w_out=m_mla_w_out)
    V = dict(meta_tokens=v_meta_tokens, pre_norm=v_pre_norm, post_norm=v_post_norm, gdn_w_in=v_gdn_w_in, gdn_conv_w=v_gdn_conv_w,
             gdn_a_log=v_gdn_a_log, gdn_dt_bias=v_gdn_dt_bias, gdn_out_norm=v_gdn_out_norm, gdn_w_out=v_gdn_w_out, kv_norm=v_kv_norm,
             kv_w_down=v_kv_w_down, kv_latent_norm=v_kv_latent_norm, kv_w_up=v_kv_w_up, mla_w_in=v_mla_w_in,
             mla_q_latent_norm=v_mla_q_latent_norm, mla_w_q_up=v_mla_w_q_up, mla_w_out=v_mla_w_out)
    order = list(W)

    n_tok = x.shape[1]
    assert n_tok % GDN_CHUNK == 0
    n_real = ROW0 + n_tok
    lp = -(-n_real // ROW_TILE) * ROW_TILE
    n_real_chunks = n_real // GDN_CHUNK

    shard2d = {n: W[n].reshape(W[n].shape[-2:]) for n in _SHARDED}
    full_shape = {n: ((s.shape[0], s.shape[1] * N_DEV) if n in _COL_SHARDED else (s.shape[0] * N_DEV, s.shape[1]))
                  for n, s in shard2d.items()}
    big, small = _SHARDED[:_N_BF16], _SHARDED[_N_BF16:]
    g_big = _exchange(_pack([shard2d[n].astype(bf16) for n in big], 16), gather=True, name="gather_weights")
    g_small = _exchange(_pack([shard2d[n] for n in small], 8), gather=True, name="gather_meta_conv")
    full = {}
    for names, buf in ((big, g_big), (small, g_small)):
        r = 0
        for n in names:
            shp = shard2d[n].shape
            rows = math.prod(shp) // 128
            blocks = buf[:, r:r + rows].reshape((N_DEV,) + shp)
            full[n] = _unshard(blocks, full_shape[n], n in _COL_SHARDED)
            r += rows

    w_in = full["gdn_w_in"]
    s1 = GDN_CONV_W + GDN_V_W
    w_in_p = jnp.concatenate([w_in[:, :s1], _pad_cols(w_in[:, s1:s1 + 16], 128), _pad_cols(w_in[:, s1 + 16:], 128)], axis=1)
    wd = full["kv_w_down"]
    zc = jnp.zeros((D_MODEL, 64), bf16)
    wd2 = jnp.concatenate([wd, zc, jnp.zeros((D_MODEL, 128), bf16), wd[:, 160:192], wd[:, 128:160], zc], axis=1)
    wup_p = jnp.transpose(full["kv_w_up"].reshape(MLA_KV_RANK, MLA_HEADS, 2, 128), (0, 2, 1, 3)).reshape(MLA_KV_RANK, 2 * MLA_V_W)
    wq = full["mla_w_q_up"].reshape(MLA_Q_RANK, MLA_HEADS, MLA_QK)
    zq64 = jnp.zeros((MLA_Q_RANK, MLA_HEADS, 64), bf16)
    wq_plain = jnp.concatenate([wq, zq64], axis=2).reshape(MLA_Q_RANK, MLA_HEADS * MLA_QKP)
    wq_swap = jnp.concatenate([jnp.zeros((MLA_Q_RANK, MLA_HEADS, 128), bf16), wq[:, :, 160:192], wq[:, :, 128:160], zq64],
                              axis=2).reshape(MLA_Q_RANK, MLA_HEADS * MLA_QKP)
    wq2 = jnp.concatenate([wq_plain, wq_swap], axis=1)
    w_mla_in, w_gdn_out, w_mla_out = full["mla_w_in"], full["gdn_w_out"], full["mla_w_out"]
    conv_w = full["gdn_conv_w"]
    pre0, pre1 = pre_norm[0:1], pre_norm[1:2]
    post0, post1 = post_norm[0:1], post_norm[1:2]
    alog_p, dtb_p = _pad_cols(gdn_a_log, 128), _pad_cols(gdn_dt_bias, 128)
    kvn, kvln = kv_norm.reshape(1, -1), kv_latent_norm.reshape(1, -1)

    h0 = jnp.concatenate([jnp.zeros((PAD_FRONT, D_MODEL), f32), full["meta_tokens"], x[0],
                          jnp.zeros((lp - n_real, D_MODEL), f32)], axis=0)
    tgt = jnp.concatenate([jnp.zeros((ROW0, D_MODEL), f32), loss_target[0], jnp.zeros((lp - n_real, D_MODEL), f32)], axis=0)
    cos_k, sin_k = _rope_tables(lp)
    one = jnp.ones((lp, 128), f32)
    cos_q = jnp.concatenate([one, cos_k], axis=1)
    sin_q = jnp.concatenate([jnp.zeros((lp, 128), f32), sin_k], axis=1)

    (hn0,) = _rowwise(_st_prenorm, [(h0, None, 0)], [], [pre0], [(D_MODEL, bf16, None)], name="f_prenorm0")
    proj = _matmul(hn0, w_in_p, name="f_gdn_in")
    conv = _conv_fwd(proj, conv_w, col_blocks=GDN_CONV_W // CONV_BC, name="f_conv")
    (qn,) = _rowwise(_st_gdn_q, [(conv, 128, 0)], [], [], [(GDN_QK_W, f32, 128)], ncol=GDN_QK_HEADS, name="f_gdn_q")
    (kn,) = _rowwise(_st_gdn_k, [(conv, 128, GDN_QK_HEADS)], [], [], [(GDN_QK_W, f32, 128)], ncol=GDN_QK_HEADS, name="f_gdn_k")
    (vv,) = _rowwise(_st_gdn_v, [(conv, 128, 2 * GDN_QK_HEADS)], [], [], [(GDN_V_W, f32, 128)], ncol=GDN_V_HEADS, name="f_gdn_v")
    gate_rows = [(proj, 128, s1 // 128), (proj, 128, s1 // 128 + 1)]
    beta, gdec = _rowwise(_st_gdn_gate, gate_rows, [], [alog_p, dtb_p], [(128, f32, None)] * 2, name="f_gdn_gate")
    o_gdn, states = _gdn_fwd(qn, kn, vv, beta, gdec, n_real_chunks=n_real_chunks, name="f_gdn")
    out_rows = [(o_gdn, 128, 0), (proj, 128, GDN_CONV_W // 128)]
    (og,) = _rowwise(_st_gdn_out, out_rows, [], [gdn_out_norm], [(GDN_V_W, bf16, 128)], ncol=GDN_V_HEADS, name="f_gdn_out")
    y0 = _matmul(og, w_gdn_out, name="f_gdn_wout")
    mid_rows = [(h0, None, 0), (y0, None, 0)]
    h1, hn1, hkv = _rowwise(_st_mid, mid_rows, [], [post0, pre1, kvn],
                            [(D_MODEL, f32, None), (D_MODEL, bf16, None), (D_MODEL, bf16, None)], name="f_mid")
    ckr = _matmul(hkv, wd2, name="f_kv_down")
    proj2 = _matmul(hn1, w_mla_in, name="f_mla_in")
    lat_rows = [(ckr, None, 0), (proj2, MLA_Q_RANK, 0)]
    lat_nd = [(cos_k, None, 0), (sin_k, None, 0)]
    c_kv, k_rope, c_q = _rowwise(_st_latent, lat_rows, lat_nd, [kvln, mla_q_latent_norm],
                                 [(128, bf16, None), (128, bf16, None), (MLA_Q_RANK, bf16, None)], name="f_latent")
    kvu = _matmul(c_kv, wup_p, out_dtype=bf16, name="f_kv_up")
    qq = _matmul(c_q, wq2, name="f_q_up")
    q_rows = [(qq, MLA_QKP, 0), (qq, MLA_QKP, MLA_HEADS)]
    q_nd = [(cos_q, None, 0), (sin_q, None, 0)]
    (q_att,) = _rowwise(_st_q_rope, q_rows, q_nd, [], [(MLA_HEADS * MLA_QKP, bf16, MLA_QKP)], ncol=MLA_HEADS, name="f_q_rope")
    o_att, lse = _attention_fwd(q_att, kvu, k_rope, name="f_attention")
    gate2_rows = [(o_att, 128, 0), (proj2, 128, MLA_Q_RANK // 128)]
    (og2,) = _rowwise(_st_gate, gate2_rows, [], [], [(MLA_V_W, bf16, 128)], ncol=MLA_HEADS, name="f_mla_gate")
    y1 = _matmul(og2, w_mla_out, name="f_mla_wout")
    st_loss = _make_st_loss(n_tok)
    loss_rows_in = [(h1, None, 0), (y1, None, 0)]
    (loss_rows,) = _rowwise(st_loss, loss_rows_in, [(tgt, None, 0)], [post1], [(1, f32, None)], name="f_loss")
    loss = lax.psum(jnp.sum(loss_rows), ("x", "y", "c"))

    ones_ct = jnp.ones((lp, 1), f32)
    (dh1_a, dy1), (dpost1,) = _rowwise_vjp(st_loss, loss_rows_in, [(tgt, None, 0)], [post1], [(ones_ct, None, 0)], name="b_loss")
    dog2 = _matmul(dy1, w_mla_out, tb=True, name="b_mla_wout_x")
    dw_mla_out = _matmul(og2, dy1, ta=True, name="b_mla_wout_w")
    (do_att, dz2), _ = _rowwise_vjp(_st_gate, gate2_rows, [], [], [(dog2, 128, 0)], ncol=MLA_HEADS, name="b_mla_gate",
                                    grad_dtypes=[bf16, bf16])
    delta = _attention_delta(o_att, do_att, name="b_attention_delta")
    dq_att, dkn, dvv, dkr_h = _attention_bwd(q_att, kvu, k_rope, lse, delta, do_att, name="b_attention")
    (dqa, dqb), _ = _rowwise_vjp(_st_q_rope, q_rows, q_nd, [], [(dq_att, MLA_QKP, 0)], ncol=MLA_HEADS, name="b_q_rope",
                                 grad_dtypes=[bf16, bf16])
    dqq = jnp.concatenate([dqa, dqb], axis=1)
    dc_q = _matmul(dqq, wq2, tb=True, name="b_q_up_x")
    dwq2 = _matmul(c_q, dqq, ta=True, name="b_q_up_w")
    dkvu = jnp.concatenate([dkn, dvv], axis=1)
    dc_kv = _matmul(dkvu, wup_p, tb=True, name="b_kv_up_x")
    dwup_p = _matmul(c_kv, dkvu, ta=True, name="b_kv_up_w")

    def lat_ct(cv):
        dkr = cv[1][:, 0:128]
        for h in range(1, MLA_HEADS):
            dkr = dkr + cv[1][:, h * 128:(h + 1) * 128]
        return [cv[0], dkr, cv[2]]

    (dckr, dcq_pre), (dkvln, dqln) = _rowwise_vjp(
        _st_latent, lat_rows, lat_nd, [kvln, mla_q_latent_norm],
        [(dc_kv, None, 0), (dkr_h, None, 0), (dc_q, None, 0)], ct_pre=lat_ct, name="b_latent", grad_dtypes=[bf16, bf16])
    dproj2 = jnp.concatenate([dcq_pre, dz2], axis=1)
    dhn1 = _matmul(dproj2, w_mla_in, tb=True, name="b_mla_in_x")
    dw_mla_in = _matmul(hn1, dproj2, ta=True, name="b_mla_in_w")
    dhkv = _matmul(dckr, wd2, tb=True, name="b_kv_down_x")
    dwd2 = _matmul(hkv, dckr, ta=True, name="b_kv_down_w")
    (dh0_a, dy0), (dpost0, dpre1, dkvn) = _rowwise_vjp(
        _st_mid, mid_rows, [], [post0, pre1, kvn], [(dh1_a, None, 0), (dhn1, None, 0), (dhkv, None, 0)], name="b_mid")
    dog = _matmul(dy0, w_gdn_out, tb=True, name="b_gdn_wout_x")
    dw_gdn_out = _matmul(og, dy0, ta=True, name="b_gdn_wout_w")
    (do_gdn, dz), (dout_norm,) = _rowwise_vjp(_st_gdn_out, out_rows, [], [gdn_out_norm], [(dog, 128, 0)], ncol=GDN_V_HEADS,
                                              name="b_gdn_out", grad_dtypes=[f32, bf16])
    dq_g, dk_g, dv_g, dbeta, dgdec = _gdn_bwd(qn, kn, vv, beta, gdec, states, do_gdn, n_real_chunks=n_real_chunks, name="b_gdn")
    (db_col, da_col), (dalog_p, ddtb_p) = _rowwise_vjp(
        _st_gdn_gate, gate_rows, [], [alog_p, dtb_p], [(dbeta, None, 0), (dgdec, None, 0)], name="b_gdn_gate",
        grad_dtypes=[bf16, bf16])
    (dconv_q,), _ = _rowwise_vjp(_st_gdn_q, [(conv, 128, 0)], [], [], [(dq_g, 128, 0)], ncol=GDN_QK_HEADS, name="b_gdn_q")
    (dconv_k,), _ = _rowwise_vjp(_st_gdn_k, [(conv, 128, GDN_QK_HEADS)], [], [], [(dk_g, 128, 0)], ncol=GDN_QK_HEADS, name="b_gdn_k")
    (dconv_v,), _ = _rowwise_vjp(_st_gdn_v, [(conv, 128, 2 * GDN_QK_HEADS)], [], [], [(dv_g, 128, 0)], ncol=GDN_V_HEADS, name="b_gdn_v")
    nq_b = GDN_QK_W // CONV_BC
    dpre_q, dcw_q = _conv_bwd(dconv_q, proj, conv_w, x_off=0, w_off=0, name="b_conv_q")
    dpre_k, dcw_k = _conv_bwd(dconv_k, proj, conv_w, x_off=nq_b, w_off=nq_b, name="b_conv_k")
    dpre_v, dcw_v = _conv_bwd(dconv_v, proj, conv_w, x_off=2 * nq_b, w_off=2 * nq_b, name="b_conv_v")
    dproj = jnp.concatenate([dpre_q, dpre_k, dpre_v, dz, db_col, da_col], axis=1)
    dhn0 = _matmul(dproj, w_in_p, tb=True, name="b_gdn_in_x")
    dw_in_p = _matmul(hn0, dproj, ta=True, name="b_gdn_in_w")
    (dh0,), (dpre0,) = _rowwise_vjp(_st_prenorm, [(h0, None, 0)], [], [pre0], [(dhn0, None, 0)], extra=[dh0_a], name="b_prenorm0")

    grad_x = dh0[ROW0:n_real][None]
    G = {}
    G["meta_tokens"] = dh0[PAD_FRONT:ROW0]
    G["gdn_w_in"] = jnp.concatenate([dw_in_p[:, :s1 + 16], dw_in_p[:, s1 + 128:s1 + 144]], axis=1)
    G["gdn_conv_w"] = jnp.concatenate([dcw_q, dcw_k, dcw_v], axis=1)
    G["gdn_w_out"] = dw_gdn_out
    G["kv_w_down"] = jnp.concatenate([dwd2[:, :128], dwd2[:, 128:160] + dwd2[:, 416:448], dwd2[:, 160:192] + dwd2[:, 384:416]], axis=1)
    G["kv_w_up"] = jnp.transpose(dwup_p.reshape(MLA_KV_RANK, 2, MLA_HEADS, 128), (0, 2, 1, 3)).reshape(MLA_KV_RANK, 2 * MLA_V_W)
    G["mla_w_in"] = dw_mla_in
    dqp = dwq2[:, :MLA_HEADS * MLA_QKP].reshape(MLA_Q_RANK, MLA_HEADS, MLA_QKP)
    dqs = dwq2[:, MLA_HEADS * MLA_QKP:].reshape(MLA_Q_RANK, MLA_HEADS, MLA_QKP)
    G["mla_w_q_up"] = jnp.concatenate([dqp[:, :, :128], dqp[:, :, 128:160] + dqs[:, :, 160:192],
                                       dqp[:, :, 160:192] + dqs[:, :, 128:160]], axis=2).reshape(MLA_Q_RANK, MLA_HEADS * MLA_QK)
    G["mla_w_out"] = dw_mla_out
    G["pre_norm"] = jnp.concatenate([dpre0, dpre1], axis=0)
    G["post_norm"] = jnp.concatenate([dpost0, dpost1], axis=0)
    G["gdn_a_log"] = dalog_p[:, :GDN_V_HEADS]
    G["gdn_dt_bias"] = ddtb_p[:, :GDN_V_HEADS]
    G["gdn_out_norm"] = dout_norm
    G["kv_norm"] = dkvn.reshape(-1)
    G["kv_latent_norm"] = dkvln.reshape(-1)
    G["mla_q_latent_norm"] = dqln

    send = jnp.concatenate([_to_shards(G[n], n in _COL_SHARDED).reshape(N_DEV, -1, 128) for n in _SHARDED], axis=1)
    parts = _exchange(send, gather=False, name="scatter_grads")
    w_s, m_s, v_s = (_pack([d[n] for n in _SHARDED], 8) for d in (W, M, V))
    res_s = _reduce_adamw(parts, w_s, m_s, v_s, name="adamw_sharded")
    small_send = _pack([G[n] for n in _REPLICATED], 8)
    parts_r = _exchange(small_send, gather=True, name="gather_small_grads")
    w_r, m_r, v_r = (_pack([d[n] for n in _REPLICATED], 8) for d in (W, M, V))
    res_r = _reduce_adamw(parts_r, w_r, m_r, v_r, name="adamw_replicated")

    outs = {}
    for kind, bs, br in zip(("grad", "delta", "new_m", "new_v"), res_s, res_r):
        for n, a in zip(_SHARDED, _unpack(bs, [W[n].shape for n in _SHARDED])):
            outs[kind, n] = a
        for n, a in zip(_REPLICATED, _unpack(br, [W[n].shape for n in _REPLICATED])):
            outs[kind, n] = a
    return (loss, grad_x, *[outs[k, n] for k in ("grad", "delta", "new_m", "new_v") for n in order])
```

```python
import functools
import math

import jax
import jax.numpy as jnp
from jax import lax
from jax.experimental import pallas as pl
from jax.experimental.pallas import tpu as pltpu

f32, bf16 = jnp.float32, jnp.bfloat16
HIGHEST = lax.Precision.HIGHEST
MESH_ID = pl.DeviceIdType.MESH

N_DEV = 8
D_MODEL = 1024
N_META = 16
NORM_EPS = 1e-6
PAD_FRONT = 48
ROW0 = PAD_FRONT + N_META
GDN_QK_HEADS, GDN_V_HEADS, GDN_D = 8, 16, 128
GDN_CHUNK = 64
GDN_QK_W, GDN_V_W = GDN_QK_HEADS * GDN_D, GDN_V_HEADS * GDN_D
GDN_CONV_W = 2 * GDN_QK_W + GDN_V_W
GDN_IN_W = GDN_CONV_W + GDN_V_W + 2 * GDN_V_HEADS
GDN_IN_WP = GDN_CONV_W + GDN_V_W + 2 * 128
MLA_HEADS, MLA_NOPE, MLA_ROPE, MLA_V = 16, 128, 64, 128
MLA_Q_RANK, MLA_KV_RANK = 256, 128
MLA_QK = MLA_NOPE + MLA_ROPE
MLA_QKP = 256
MLA_V_W = MLA_HEADS * MLA_V
ROPE_THETA = 10000.0
NEG = -1e30
ROW_TILE = 256
ATT_TILE = 256

ADAM_LR, ADAM_B1, ADAM_B2, ADAM_EPS, ADAM_WD, ADAM_STEP = 0.001, 0.9, 0.999, 1e-08, 0.01, 10

NN = ((1,), (0,))
NT = ((1,), (1,))
TN = ((0,), (0,))


def _pick(dim, prefs):
    for p in prefs:
        if dim % p == 0:
            return p
    return dim


def _dlo(a, b, dims):
    return lax.dot_general(a.astype(bf16), b.astype(bf16), (dims, ((), ())), preferred_element_type=f32)


def _dhi(a, b, dims):
    return lax.dot_general(a, b, (dims, ((), ())), precision=HIGHEST, preferred_element_type=f32)


def _matmul(a, b, *, ta=False, tb=False, out_dtype=f32, name):
    assert not (ta and tb)
    if ta:
        kdim, m = a.shape
    else:
        m, kdim = a.shape
    n = b.shape[0] if tb else b.shape[1]
    assert (b.shape[1] if tb else b.shape[0]) == kdim
    tm = _pick(m, (768, 512, 384, 256, 128))
    tn = _pick(n, (1024, 768, 640, 512, 256, 128))
    tk = _pick(kdim, (1024, 768, 640, 512, 256, 128))
    nk = kdim // tk
    dims = TN if ta else (NT if tb else NN)

    def body(a_ref, b_ref, o_ref, acc_ref):
        k = pl.program_id(2)

        @pl.when(k == 0)
        def _():
            acc_ref[...] = jnp.zeros_like(acc_ref)

        acc_ref[...] += lax.dot_general(a_ref[...].astype(bf16), b_ref[...].astype(bf16), (dims, ((), ())),
                                        preferred_element_type=f32)

        @pl.when(k == nk - 1)
        def _():
            o_ref[...] = acc_ref[...].astype(o_ref.dtype)

    a_spec = pl.BlockSpec((tk, tm), lambda i, j, k: (k, i)) if ta else pl.BlockSpec((tm, tk), lambda i, j, k: (i, k))
    b_spec = pl.BlockSpec((tn, tk), lambda i, j, k: (j, k)) if tb else pl.BlockSpec((tk, tn), lambda i, j, k: (k, j))
    return pl.pallas_call(
        body, name=name, grid=(m // tm, n // tn, nk),
        in_specs=[a_spec, b_spec], out_specs=pl.BlockSpec((tm, tn), lambda i, j, k: (i, j)),
        out_shape=jax.ShapeDtypeStruct((m, n), out_dtype),
        scratch_shapes=[pltpu.VMEM((tm, tn), f32)],
        compiler_params=pltpu.CompilerParams(dimension_semantics=("parallel", "parallel", "arbitrary")),
    )(a, b)


def _row_spec(item, tr):
    a, bc, off = item
    if bc is None:
        return pl.BlockSpec((tr, a.shape[1]), lambda i, j: (i, 0))
    return pl.BlockSpec((tr, bc), lambda i, j, off=off: (i, j + off))


def _param_spec(p):
    return pl.BlockSpec(p.shape, lambda i, j: (0, 0))


def _row_tile(lp, items):
    widest = max(a.shape[1] if bc is None else bc for (a, bc, _) in items)
    return ROW_TILE if widest >= 1024 else _pick(lp, (768, 512, 256))


def _head_cols(tiles, h, heads):
    return [x[:, h * (x.shape[1] // heads):(h + 1) * (x.shape[1] // heads)] for x in tiles]


def _rowwise(fn, rows, nodiff, params, outs, *, ncol=1, heads=1, name):
    lp = rows[0][0].shape[0]
    tr = _row_tile(lp, rows)
    nr, nd = len(rows), len(nodiff)

    def body(*refs):
        rv = [r[...].astype(f32) for r in refs[:nr]]
        nv = [r[...] for r in refs[nr:nr + nd]]
        pv = [r[...] for r in refs[nr + nd:nr + nd + len(params)]]
        per_head = [fn(_head_cols(rv, h, heads), nv, pv) for h in range(heads)]
        res = [jnp.concatenate(list(vals), axis=1) if heads > 1 else vals[0] for vals in zip(*per_head)]
        for ref, val in zip(refs[nr + nd + len(params):], res):
            ref[...] = val.astype(ref.dtype)

    out_specs = [pl.BlockSpec((tr, c if bc is None else bc), (lambda i, j: (i, 0)) if bc is None else (lambda i, j: (i, j)))
                 for (c, _, bc) in outs]
    return pl.pallas_call(
        body, name=name, grid=(lp // tr, ncol),
        in_specs=[_row_spec(it, tr) for it in rows + nodiff] + [_param_spec(p) for p in params],
        out_specs=out_specs,
        out_shape=[jax.ShapeDtypeStruct((lp, c), dt) for (c, dt, _) in outs],
    )(*[it[0] for it in rows + nodiff], *params)


def _rowwise_vjp(fn, rows, nodiff, params, cts, *, ncol=1, heads=1, name, ct_pre=None, extra=None, grad_dtypes=None):
    lp = rows[0][0].shape[0]
    tr = _row_tile(lp, rows)
    nr, nd, npar, nct = len(rows), len(nodiff), len(params), len(cts)
    extra = extra or [None] * nr
    grad_dtypes = grad_dtypes or [f32] * nr
    ex_items = [(e, rows[k][1], 0) for k, e in enumerate(extra) if e is not None]
    ex_pos = [k for k, e in enumerate(extra) if e is not None]
    for (a, bc, _) in rows:
        assert bc is not None or ncol == 1

    def body(*refs):
        pos = 0
        rv = [r[...].astype(f32) for r in refs[pos:pos + nr]]; pos += nr
        nv = [r[...] for r in refs[pos:pos + nd]]; pos += nd
        pv = [r[...] for r in refs[pos:pos + npar]]; pos += npar
        cv = [r[...].astype(f32) for r in refs[pos:pos + nct]]; pos += nct
        ev = [r[...].astype(f32) for r in refs[pos:pos + len(ex_items)]]; pos += len(ex_items)
        drow_refs = refs[pos:pos + nr]; pos += nr
        dpar_refs = refs[pos:pos + npar]
        ctv = ct_pre(cv) if ct_pre is not None else cv
        drow_h, dpar = [], None
        for h in range(heads):
            outs, vjp_fn = jax.vjp(lambda rr, pp: fn(rr, nv, pp), _head_cols(rv, h, heads), pv)
            dr, dp = vjp_fn([c.astype(o.dtype) for c, o in zip(_head_cols(ctv, h, heads), outs)])
            drow_h.append(dr)
            dpar = dp if dpar is None else [a + b for a, b in zip(dpar, dp)]
        drow = [jnp.concatenate(list(vals), axis=1) if heads > 1 else vals[0] for vals in zip(*drow_h)]
        for k, e in zip(ex_pos, ev):
            drow[k] = drow[k] + e
        for ref, val in zip(drow_refs, drow):
            ref[...] = val.astype(ref.dtype)
        first = jnp.logical_and(pl.program_id(0) == 0, pl.program_id(1) == 0)

        @pl.when(first)
        def _():
            for ref, val in zip(dpar_refs, dpar):
                ref[...] = val

        @pl.when(jnp.logical_not(first))
        def _():
            for ref, val in zip(dpar_refs, dpar):
                ref[...] += val

    drow_shapes, drow_specs = [], []
    for (a, bc, _), dt in zip(rows, grad_dtypes):
        if bc is None:
            drow_shapes.append(jax.ShapeDtypeStruct((lp, a.shape[1]), dt))
            drow_specs.append(pl.BlockSpec((tr, a.shape[1]), lambda i, j: (i, 0)))
        else:
            drow_shapes.append(jax.ShapeDtypeStruct((lp, ncol * bc), dt))
            drow_specs.append(pl.BlockSpec((tr, bc), lambda i, j: (i, j)))
    res = pl.pallas_call(
        body, name=name, grid=(lp // tr, ncol),
        in_specs=[_row_spec(it, tr) for it in rows + nodiff] + [_param_spec(p) for p in params]
        + [_row_spec(it, tr) for it in cts + ex_items],
        out_specs=drow_specs + [_param_spec(p) for p in params],
        out_shape=drow_shapes + [jax.ShapeDtypeStruct(p.shape, f32) for p in params],
        compiler_params=pltpu.CompilerParams(dimension_semantics=("arbitrary", "arbitrary")),
    )(*[it[0] for it in rows + nodiff], *params, *[it[0] for it in cts + ex_items])
    return res[:nr], res[nr:]


def _rms(x, g):
    return x * lax.rsqrt(jnp.mean(x * x, axis=-1, keepdims=True) + NORM_EPS) * g


def _l2n(x):
    return x * lax.rsqrt(jnp.sum(x * x, axis=-1, keepdims=True) + NORM_EPS)


def _sigmoid(x):
    return 1.0 / (1.0 + jnp.exp(-x))


def _silu(x):
    return x * _sigmoid(x)


def _softplus(x):
    return jnp.maximum(x, 0.0) + jnp.log(1.0 + jnp.exp(-jnp.abs(x)))


def _row_ids(shape):
    return pl.program_id(0) * shape[0] + lax.broadcasted_iota(jnp.int32, shape, 0)


def _st_prenorm(r, n, p):
    return [_rms(r[0], p[0])]


def _st_gdn_q(r, n, p):
    return [_l2n(_silu(r[0])) * (GDN_D ** -0.5)]


def _st_gdn_k(r, n, p):
    return [_l2n(_silu(r[0]))]


def _st_gdn_v(r, n, p):
    return [_silu(r[0])]


def _st_gdn_gate(r, n, p):
    real = _row_ids(r[0].shape) >= PAD_FRONT
    beta = jnp.where(real, _sigmoid(r[0]), 0.0)
    g = jnp.where(real, -jnp.exp(p[0]) * _softplus(r[1] + p[1]), 0.0)
    return [beta, g]


def _st_gdn_out(r, n, p):
    return [_rms(r[0], p[0]) * _silu(r[1])]


def _st_mid(r, n, p):
    h1 = r[0] + _rms(r[1], p[0])
    return [h1, _rms(h1, p[1]), _rms(h1, p[2])]


def _st_latent(r, n, p):
    ckr, cq = r
    c_kv = _rms(ckr[:, :MLA_KV_RANK], p[0])
    k_rope = ckr[:, 128:256] * n[0] + ckr[:, 384:512] * n[1]
    return [c_kv, k_rope, _rms(cq, p[1])]


def _st_q_rope(r, n, p):
    return [(r[0] * n[0] + r[1] * n[1]) * (MLA_QK ** -0.5)]


def _st_gate(r, n, p):
    return [r[0] * _silu(r[1])]


def _make_st_loss(n_tokens):
    def st(r, n, p):
        h2 = r[0] + _rms(r[1], p[0])
        rows = _row_ids((r[0].shape[0], 1))
        real = jnp.logical_and(rows >= ROW0, rows < ROW0 + n_tokens)
        err = h2 - n[0]
        return [jnp.where(real, 0.5 * jnp.mean(err * err, axis=-1, keepdims=True), 0.0)]
    return st


CONV_BC = 512


def _conv_fwd(x, w, *, col_blocks, name, tr=ROW_TILE):
    lp = x.shape[0]

    def body(x_ref, xp_ref, w_ref, o_ref):
        i = pl.program_id(0)
        prev = jnp.where(i > 0, xp_ref[...], 0.0)
        xc = jnp.concatenate([prev, x_ref[...]], axis=0)
        wv = w_ref[...]
        acc = wv[3:4, :] * x_ref[...]
        for j in range(3):
            acc = acc + wv[j:j + 1, :] * pltpu.roll(xc, 3 - j, 0)[8:, :]
        o_ref[...] = acc

    return pl.pallas_call(
        body, name=name, grid=(lp // tr, col_blocks),
        in_specs=[pl.BlockSpec((tr, CONV_BC), lambda i, j: (i, j)),
                  pl.BlockSpec((8, CONV_BC), lambda i, j: (jnp.maximum(i * (tr // 8) - 1, 0), j)),
                  pl.BlockSpec((4, CONV_BC), lambda i, j: (0, j))],
        out_specs=pl.BlockSpec((tr, CONV_BC), lambda i, j: (i, j)),
        out_shape=jax.ShapeDtypeStruct((lp, col_blocks * CONV_BC), f32),
    )(x, x, w)


def _conv_bwd(dc, x, w, *, x_off, w_off, name, tr=ROW_TILE):
    lp, width = dc.shape
    ncb, nrow = width // CONV_BC, lp // tr

    def body(dc_ref, dcn_ref, x_ref, xp_ref, w_ref, dx_ref, dw_ref):
        i = pl.program_id(1)
        nxt = jnp.where(i < nrow - 1, dcn_ref[...], 0.0)
        dcv = dc_ref[...]
        dcc = jnp.concatenate([dcv, nxt], axis=0)
        prev = jnp.where(i > 0, xp_ref[...], 0.0)
        xc = jnp.concatenate([prev, x_ref[...]], axis=0)
        wv = w_ref[...]
        dx = wv[3:4, :] * dcv
        dws = [None] * 4
        dws[3] = jnp.sum(dcv * x_ref[...], axis=0, keepdims=True)
        for j in range(3):
            dx = dx + wv[j:j + 1, :] * pltpu.roll(dcc, tr + 8 - (3 - j), 0)[:tr, :]
            dws[j] = jnp.sum(dcv * pltpu.roll(xc, 3 - j, 0)[8:, :], axis=0, keepdims=True)
        dx_ref[...] = dx.astype(dx_ref.dtype)

        @pl.when(i == 0)
        def _():
            for j in range(4):
                dw_ref[j:j + 1, :] = dws[j]

        @pl.when(i > 0)
        def _():
            for j in range(4):
                dw_ref[j:j + 1, :] += dws[j]

    last8 = lp // 8 - 1
    return pl.pallas_call(
        body, name=name, grid=(ncb, nrow),
        in_specs=[pl.BlockSpec((tr, CONV_BC), lambda j, i: (i, j)),
                  pl.BlockSpec((8, CONV_BC), lambda j, i: (jnp.minimum((i + 1) * (tr // 8), last8), j)),
                  pl.BlockSpec((tr, CONV_BC), lambda j, i: (i, j + x_off)),
                  pl.BlockSpec((8, CONV_BC), lambda j, i: (jnp.maximum(i * (tr // 8) - 1, 0), j + x_off)),
                  pl.BlockSpec((4, CONV_BC), lambda j, i: (0, j + w_off))],
        out_specs=[pl.BlockSpec((tr, CONV_BC), lambda j, i: (i, j)),
                   pl.BlockSpec((4, CONV_BC), lambda j, i: (0, j))],
        out_shape=[jax.ShapeDtypeStruct((lp, width), bf16), jax.ShapeDtypeStruct((4, width), f32)],
        compiler_params=pltpu.CompilerParams(dimension_semantics=("arbitrary", "arbitrary")),
    )(dc, dc, x, x, w)


GDN_PACK = 4


def _bd(x, cb):
    r = x.shape[0]
    tall = jnp.concatenate([x] * GDN_PACK, axis=0)
    rows = lax.broadcasted_iota(jnp.int32, tall.shape, 0) // r
    cols = lax.broadcasted_iota(jnp.int32, tall.shape, 1) // cb
    return jnp.where(rows == cols, tall, jnp.zeros_like(tall))


def _diag(full, r, cb):
    cols = lax.broadcasted_iota(jnp.int32, (r, full.shape[1]), 1) // cb
    out = jnp.where(cols == 0, full[0:r, :], 0.0)
    for a in range(1, GDN_PACK):
        out = out + jnp.where(cols == a, full[a * r:(a + 1) * r, :], 0.0)
    return out


def _stack(x, cb):
    return jnp.concatenate([x[:, a * cb:(a + 1) * cb] for a in range(GDN_PACK)], axis=0)


def _make_packed(dot):
    @jax.custom_vjp
    def pmm(x, y):
        return dot(x, _bd(y, y.shape[1] // GDN_PACK), NN)

    @jax.custom_vjp
    def pnt(x, y):
        k = x.shape[1] // GDN_PACK
        return _diag(dot(_stack(x, k), _stack(y, k), NT), x.shape[0], y.shape[0])

    @jax.custom_vjp
    def ptn(x, y):
        return _diag(dot(x, y, TN), x.shape[1] // GDN_PACK, y.shape[1] // GDN_PACK)

    def pmm_bwd(res, ct):
        x, y = res
        cb = y.shape[1] // GDN_PACK
        return dot(ct, _bd(y, cb), NT), _diag(dot(x, ct, TN), y.shape[0], cb)

    pmm.defvjp(lambda x, y: (pmm(x, y), (x, y)), pmm_bwd)
    pnt.defvjp(lambda x, y: (pnt(x, y), (x, y)), lambda res, ct: (pmm(ct, res[1]), ptn(ct, res[0])))
    ptn.defvjp(lambda x, y: (ptn(x, y), (x, y)), lambda res, ct: (pnt(res[1], ct), pmm(res[0], ct)))
    return pmm, pnt, ptn


_pmm, _pnt, _ptn = _make_packed(_dlo)


@jax.custom_vjp
def _inv_packed(m):
    c = m.shape[0]
    ii = lax.broadcasted_iota(jnp.int32, m.shape, 0)
    jj = lax.broadcasted_iota(jnp.int32, m.shape, 1) % c
    t = jnp.where(ii == jj, 1.0, 0.0) - m
    p = (-m).astype(bf16)
    for _ in range(int(math.log2(c)) - 1):
        p = _dlo(p, _bd(p, c), NN).astype(bf16)
        t = t + _dlo(t, _bd(p, c), NN)
    return t


def _inv_packed_fwd(m):
    t = _inv_packed(m)
    return t, t


def _inv_packed_bwd(t, ct):
    c = t.shape[0]
    return (-_dlo(_diag(_dlo(t, ct, TN), c, c), _bd(t.astype(bf16), c), NT),)


_inv_packed.defvjp(_inv_packed_fwd, _inv_packed_bwd)


def _gdn_group(s, q2, k2, v4, bcols, gcols, grows):
    c, d = v4.shape[0], GDN_D
    q4 = jnp.concatenate([q2[:, :d], q2[:, :d], q2[:, d:], q2[:, d:]], axis=1)
    k4 = jnp.concatenate([k2[:, :d], k2[:, :d], k2[:, d:], k2[:, d:]], axis=1)
    beta4 = jnp.concatenate([jnp.broadcast_to(b, (c, d)) for b in bcols], axis=1)
    gc4 = jnp.concatenate([jnp.broadcast_to(g, (c, d)) for g in gcols], axis=1)
    low = lax.broadcasted_iota(jnp.int32, (c, 128), 1) < c
    gi = jnp.concatenate([jnp.where(low, gcols[0], gcols[1]), jnp.where(low, gcols[2], gcols[3])], axis=1)
    gj = jnp.concatenate([jnp.where(low, grows[0], grows[1]), jnp.where(low, grows[2], grows[3])], axis=1)
    ii = lax.broadcasted_iota(jnp.int32, gi.shape, 0)
    jj = lax.broadcasted_iota(jnp.int32, gi.shape, 1) % c
    dec = jnp.exp(jnp.where(ii >= jj, gi - gj, NEG))
    dec_strict = jnp.where(ii > jj, dec, 0.0)
    rid = lax.broadcasted_iota(jnp.int32, gc4.shape, 0)
    glast = jnp.sum(jnp.where(rid == c - 1, gc4, 0.0), axis=0, keepdims=True)
    eg = jnp.exp(gc4)
    kb = k4 * beta4
    t = _inv_packed(_pnt(kb, k4) * dec_strict)
    u = _pmm(t, v4 * beta4)
    w = _pmm(t, kb * eg)
    attn = _pnt(q4, k4) * dec
    ws_qs = _pmm(jnp.concatenate([w, q4 * eg], axis=0), s)
    v_new = u - ws_qs[:c]
    o = ws_qs[c:] + _pmm(attn, v_new)
    s_new = s * jnp.exp(glast) + _ptn(k4 * jnp.exp(glast - gc4), v_new)
    return o, s_new


def _lane_pick(x, h):
    lane = lax.broadcasted_iota(jnp.int32, x.shape, 1)
    return jnp.sum(jnp.where(lane == h, x, 0.0), axis=1, keepdims=True)


def _cum_log_decay(g):
    c = g.shape[0]
    lower = (lax.broadcasted_iota(jnp.int32, (c, c), 0) >= lax.broadcasted_iota(jnp.int32, (c, c), 1)).astype(f32)
    upper2 = (lax.broadcasted_iota(jnp.int32, (c, 128), 0) <= lax.broadcasted_iota(jnp.int32, (c, 128), 1) % c).astype(f32)
    return _dhi(lower, g, NN), _dhi(g, upper2, TN)


def _group_operands(gi, s_ref, q_ref, k_ref, v_ref, bv, gcv, gct_s):
    heads = [gi * GDN_PACK + u for u in range(GDN_PACK)]
    qk_off = pl.multiple_of(gi * 2 * GDN_D, 2 * GDN_D)
    v_off = pl.multiple_of(gi * GDN_PACK * GDN_D, GDN_PACK * GDN_D)
    return (s_ref[gi], q_ref[:, pl.ds(qk_off, 2 * GDN_D)], k_ref[:, pl.ds(qk_off, 2 * GDN_D)],
            v_ref[:, pl.ds(v_off, GDN_PACK * GDN_D)],
            [_lane_pick(bv, h) for h in heads], [_lane_pick(gcv, h) for h in heads],
            [gct_s[pl.ds(h, 1), :] for h in heads]), heads, qk_off, v_off


def _gdn_fwd(qn, kn, v, beta, g, *, n_real_chunks, name):
    lp = qn.shape[0]
    nchunk = lp // GDN_CHUNK
    C, D = GDN_CHUNK, GDN_D
    NG, SW = GDN_V_HEADS // GDN_PACK, GDN_PACK * GDN_D

    def body(q_ref, k_ref, v_ref, b_ref, g_ref, o_ref, st_ref, s_s, gc_s, gct_s):
        ci = pl.program_id(0)

        @pl.when(ci == 0)
        def _():
            s_s[...] = jnp.zeros_like(s_s)

        @pl.when(ci >= n_real_chunks)
        def _():
            o_ref[...] = jnp.zeros_like(o_ref)
            st_ref[...] = jnp.zeros_like(st_ref)

        @pl.when(ci < n_real_chunks)
        def _():
            gc, gct = _cum_log_decay(g_ref[...])
            gc_s[...] = gc
            gct_s[...] = gct

            def group(gi, carry):
                args, _, _, v_off = _group_operands(gi, s_s, q_ref, k_ref, v_ref, b_ref[...], gc_s[...], gct_s)
                o, s_new = _gdn_group(*args)
                st_ref[gi] = args[0]
                s_s[gi] = s_new
                o_ref[:, pl.ds(v_off, SW)] = o
                return carry

            lax.fori_loop(0, NG, group, 0)

    return pl.pallas_call(
        body, name=name, grid=(nchunk,),
        in_specs=[pl.BlockSpec((C, GDN_QK_W), lambda c: (c, 0)), pl.BlockSpec((C, GDN_QK_W), lambda c: (c, 0)),
                  pl.BlockSpec((C, GDN_V_W), lambda c: (c, 0)), pl.BlockSpec((C, 128), lambda c: (c, 0)),
                  pl.BlockSpec((C, 128), lambda c: (c, 0))],
        out_specs=[pl.BlockSpec((C, GDN_V_W), lambda c: (c, 0)),
                   pl.BlockSpec((None, NG, D, SW), lambda c: (c, 0, 0, 0))],
        out_shape=[jax.ShapeDtypeStruct((lp, GDN_V_W), f32), jax.ShapeDtypeStruct((nchunk, NG, D, SW), f32)],
        scratch_shapes=[pltpu.VMEM((NG, D, SW), f32), pltpu.VMEM((C, 128), f32), pltpu.VMEM((128, 128), f32)],
        compiler_params=pltpu.CompilerParams(dimension_semantics=("arbitrary",)),
    )(qn, kn, v, beta, g)


def _gdn_bwd(qn, kn, v, beta, g, states, do, *, n_real_chunks, name):
    lp = qn.shape[0]
    nchunk = lp // GDN_CHUNK
    C, D = GDN_CHUNK, GDN_D
    NG, SW = GDN_V_HEADS // GDN_PACK, GDN_PACK * GDN_D
    rev = lambda i: (nchunk - 1 - i, 0)

    def body(q_ref, k_ref, v_ref, b_ref, g_ref, st_ref, do_ref,
             dq_ref, dk_ref, dv_ref, db_ref, dg_ref, ds_s, gc_s, gct_s, dgc_s, dgct_s, dbeta_s):
        step = pl.program_id(0)
        ci = nchunk - 1 - step

        @pl.when(step == 0)
        def _():
            ds_s[...] = jnp.zeros_like(ds_s)

        @pl.when(ci >= n_real_chunks)
        def _():
            for r in (dq_ref, dk_ref, dv_ref, db_ref, dg_ref):
                r[...] = jnp.zeros_like(r)

        @pl.when(ci < n_real_chunks)
        def _():
            gc, gct = _cum_log_decay(g_ref[...])
            gc_s[...] = gc
            gct_s[...] = gct
            dgc_s[...] = jnp.zeros_like(dgc_s)
            dgct_s[...] = jnp.zeros_like(dgct_s)
            dbeta_s[...] = jnp.zeros_like(dbeta_s)

            def group(gi, carry):
                args, heads, qk_off, v_off = _group_operands(gi, st_ref, q_ref, k_ref, v_ref, b_ref[...], gc_s[...], gct_s)
                _, vjp_fn = jax.vjp(_gdn_group, *args)
                dsp, dq2, dk2, dv4, dbcols, dgcols, dgrows = vjp_fn((do_ref[:, pl.ds(v_off, SW)], ds_s[gi]))
                ds_s[gi] = dsp
                dq_ref[:, pl.ds(qk_off, 2 * D)] = dq2
                dk_ref[:, pl.ds(qk_off, 2 * D)] = dk2
                dv_ref[:, pl.ds(v_off, SW)] = dv4
                lane = lax.broadcasted_iota(jnp.int32, (C, 128), 1)
                dbeta_acc, dgc_acc = dbeta_s[...], dgc_s[...]
                for h, dbcol, dgcol, dgrow in zip(heads, dbcols, dgcols, dgrows):
                    dbeta_acc = dbeta_acc + jnp.where(lane == h, dbcol, 0.0)
                    dgc_acc = dgc_acc + jnp.where(lane == h, dgcol, 0.0)
                    dgct_s[pl.ds(h, 1), :] = dgrow
                dbeta_s[...] = dbeta_acc
                dgc_s[...] = dgc_acc
                return carry

            lax.fori_loop(0, NG, group, 0)
            fold = (lax.broadcasted_iota(jnp.int32, (128, C), 0) % C == lax.broadcasted_iota(jnp.int32, (128, C), 1)).astype(f32)
            eye = (lax.broadcasted_iota(jnp.int32, (128, 128), 0) == lax.broadcasted_iota(jnp.int32, (128, 128), 1)).astype(f32)
            dgc = dgc_s[...] + _dhi(_dhi(dgct_s[...], fold, NN), eye, TN)
            upper = (lax.broadcasted_iota(jnp.int32, (C, C), 0) <= lax.broadcasted_iota(jnp.int32, (C, C), 1)).astype(f32)
            dg_ref[...] = _dhi(upper, dgc, NN)
            db_ref[...] = dbeta_s[...]

    return pl.pallas_call(
        body, name=name, grid=(nchunk,),
        in_specs=[pl.BlockSpec((C, GDN_QK_W), rev), pl.BlockSpec((C, GDN_QK_W), rev), pl.BlockSpec((C, GDN_V_W), rev),
                  pl.BlockSpec((C, 128), rev), pl.BlockSpec((C, 128), rev),
                  pl.BlockSpec((None, NG, D, SW), lambda i: (nchunk - 1 - i, 0, 0, 0)), pl.BlockSpec((C, GDN_V_W), rev)],
        out_specs=[pl.BlockSpec((C, GDN_QK_W), rev), pl.BlockSpec((C, GDN_QK_W), rev), pl.BlockSpec((C, GDN_V_W), rev),
                   pl.BlockSpec((C, 128), rev), pl.BlockSpec((C, 128), rev)],
        out_shape=[jax.ShapeDtypeStruct((lp, GDN_QK_W), f32)] * 2 + [jax.ShapeDtypeStruct((lp, GDN_V_W), f32)]
        + [jax.ShapeDtypeStruct((lp, 128), f32)] * 2,
        scratch_shapes=[pltpu.VMEM((NG, D, SW), f32), pltpu.VMEM((C, 128), f32), pltpu.VMEM((128, 128), f32),
                        pltpu.VMEM((C, 128), f32), pltpu.VMEM((128, 128), f32), pltpu.VMEM((C, 128), f32)],
        compiler_params=pltpu.CompilerParams(dimension_semantics=("arbitrary",)),
    )(qn, kn, v, beta, g, states, do)


def _att_mask_t(k0, q0, tk, tq):
    kcol = k0 + lax.broadcasted_iota(jnp.int32, (tk, tq), 0)
    qrow = q0 + lax.broadcasted_iota(jnp.int32, (tk, tq), 1)
    return jnp.logical_and(qrow >= kcol, kcol >= PAD_FRONT)


def _attention_fwd(q, kvu, kr, *, name, tk=ATT_TILE):
    lp = q.shape[0]
    H = MLA_HEADS
    tq = _pick(lp, (768, 512, 256))
    r = tq // tk

    def body(q_ref, kn_ref, kr_ref, v_ref, o_ref, lse_ref, m_s, l_s, acc_s, sa_s, sb_s):
        qi = pl.program_id(1)
        m_s[...] = jnp.full_like(m_s, NEG)
        l_s[...] = jnp.zeros_like(l_s)
        acc_s[...] = jnp.zeros_like(acc_s)

        def scores(ki):
            k0 = pl.multiple_of(ki * tk, tk)
            k = jnp.concatenate([kn_ref[pl.ds(k0, tk), :], kr_ref[pl.ds(k0, tk), :]], axis=1)
            return lax.dot_general(k, q_ref[...], (NT, ((), ())), preferred_element_type=f32)

        def consume(st, ki, masked):
            k0 = pl.multiple_of(ki * tk, tk)
            if masked:
                st = jnp.where(_att_mask_t(k0, qi * tq, tk, tq), st, NEG)
            m_prev = m_s[...]
            m_new = jnp.maximum(m_prev, jnp.max(st, axis=0, keepdims=True))
            alpha = jnp.exp(m_prev - m_new)
            p = jnp.exp(st - m_new)
            l_s[...] = alpha * l_s[...] + jnp.sum(p, axis=0, keepdims=True)
            acc_s[...] = alpha * acc_s[...] + lax.dot_general(v_ref[pl.ds(k0, tk), :], p.astype(bf16), (TN, ((), ())),
                                                              preferred_element_type=f32)
            m_s[...] = m_new

        n_full = qi * r

        def chain(blocks):
            bufs = (sa_s, sb_s)
            for j, (ki, masked) in enumerate(blocks):
                if j + 1 < len(blocks):
                    bufs[(j + 1) % 2][...] = scores(blocks[j + 1][0])
                consume(bufs[j % 2][...], ki, masked)

        diagonal = [(n_full + d, True) for d in range(r)]

        @pl.when(qi == 0)
        def _():
            sa_s[...] = scores(0)
            chain(diagonal)

        @pl.when(qi > 0)
        def _():
            sb_s[...] = scores(0)
            sa_s[...] = scores(1)
            consume(sb_s[...], 0, True)
            n_pairs = (n_full - 1) // 2

            def two(pi, carry):
                ki = 1 + 2 * pi
                sb_s[...] = scores(ki + 1)
                consume(sa_s[...], ki, False)
                sa_s[...] = scores(ki + 2)
                consume(sb_s[...], ki + 1, False)
                return carry

            lax.fori_loop(0, n_pairs, two, 0)
            nxt = 1 + 2 * n_pairs

            @pl.when(nxt < n_full)
            def _():
                chain([(nxt, False)] + diagonal)

            @pl.when(nxt == n_full)
            def _():
                chain(diagonal)
        o_ref[...] = jnp.transpose(acc_s[...] / l_s[...])
        lse_ref[...] = m_s[...] + jnp.log(l_s[...])

    return pl.pallas_call(
        body, name=name, grid=(H, lp // tq),
        in_specs=[pl.BlockSpec((tq, MLA_QKP), lambda h, qi: (qi, h)),
                  pl.BlockSpec((lp, 128), lambda h, qi: (0, h)),
                  pl.BlockSpec((lp, 128), lambda h, qi: (0, 0)),
                  pl.BlockSpec((lp, 128), lambda h, qi: (0, H + h))],
        out_specs=[pl.BlockSpec((tq, 128), lambda h, qi: (qi, h)),
                   pl.BlockSpec((None, 1, tq), lambda h, qi: (h, 0, qi))],
        out_shape=[jax.ShapeDtypeStruct((lp, MLA_V_W), f32), jax.ShapeDtypeStruct((H, 1, lp), f32)],
        scratch_shapes=[pltpu.VMEM((1, tq), f32), pltpu.VMEM((1, tq), f32), pltpu.VMEM((128, tq), f32),
                        pltpu.VMEM((tk, tq), f32), pltpu.VMEM((tk, tq), f32)],
        compiler_params=pltpu.CompilerParams(dimension_semantics=("arbitrary", "arbitrary")),
    )(q, kvu, kr, kvu)


def _attention_delta(o, do, *, name):
    lp = o.shape[0]
    H = MLA_HEADS
    tq = _pick(lp, (768, 512, 256))

    def body(o_ref, do_ref, d_ref):
        prod = o_ref[...] * do_ref[...].astype(f32)
        d_ref[...] = jnp.sum(jnp.transpose(prod), axis=0, keepdims=True)

    return pl.pallas_call(
        body, name=name, grid=(H, lp // tq),
        in_specs=[pl.BlockSpec((tq, 128), lambda h, qi: (qi, h)), pl.BlockSpec((tq, 128), lambda h, qi: (qi, h))],
        out_specs=pl.BlockSpec((None, 1, tq), lambda h, qi: (h, 0, qi)),
        out_shape=jax.ShapeDtypeStruct((H, 1, lp), f32),
    )(o, do)


def _attention_bwd(q, kvu, kr, lse, delta, do, *, name, t=ATT_TILE):
    lp = q.shape[0]
    H, nb = MLA_HEADS, lp // t
    tq = _pick(lp, (768, 512, 256))
    r, nq = tq // t, lp // tq

    def body(q_ref, kn_ref, kr_ref, v_ref, lse_ref, dl_ref, do_ref, dq_ref, dkn_ref, dv_ref, dkr_ref, dk_s, dv_s,
             sa_s, da_s, sb_s, db_s):
        ki = pl.program_id(1)
        k0 = ki * t
        k = jnp.concatenate([kn_ref[...], kr_ref[...]], axis=1)
        vv = v_ref[...]
        dk_s[...] = jnp.zeros_like(dk_s)
        dv_s[...] = jnp.zeros_like(dv_s)

        def products(qi, s_ref, d_ref):
            q0 = pl.multiple_of(qi * tq, tq)
            s_ref[...] = lax.dot_general(k, q_ref[pl.ds(q0, tq), :], (NT, ((), ())), preferred_element_type=f32)
            d_ref[...] = lax.dot_general(vv, do_ref[pl.ds(q0, tq), :], (NT, ((), ())), preferred_element_type=f32)

        def accumulate(s_ref, d_ref, qi, masked, first):
            q0 = pl.multiple_of(qi * tq, tq)
            qv = q_ref[pl.ds(q0, tq), :]
            dob = do_ref[pl.ds(q0, tq), :]
            st = s_ref[...]
            if masked:
                st = jnp.where(_att_mask_t(k0, q0, t, tq), st, NEG)
            p = jnp.exp(st - lse_ref[:, pl.ds(q0, tq)])
            dv_s[...] += jnp.dot(p.astype(bf16), dob, preferred_element_type=f32)
            ds = (p * (d_ref[...] - dl_ref[:, pl.ds(q0, tq)])).astype(bf16)
            dk_s[...] += jnp.dot(ds, qv, preferred_element_type=f32)
            dq = lax.dot_general(ds, k, (TN, ((), ())), preferred_element_type=f32)
            if first:
                dq_ref[pl.ds(q0, tq), :] = dq
            else:
                dq_ref[pl.ds(q0, tq), :] += dq

        def sweep(qd, mask_all, first):
            last = nq - 1
            products(qd, sa_s, da_s)
            products(jnp.minimum(qd + 1, last), sb_s, db_s)
            accumulate(sa_s, da_s, qd, True, first)
            n = last - qd

            def two(pi, carry):
                i = qd + 1 + 2 * pi
                products(i + 1, sa_s, da_s)
                accumulate(sb_s, db_s, i, mask_all, first)
                products(jnp.minimum(i + 2, last), sb_s, db_s)
                accumulate(sa_s, da_s, i + 1, mask_all, first)
                return carry

            lax.fori_loop(0, n // 2, two, 0)

            @pl.when(n % 2 == 1)
            def _():
                accumulate(sb_s, db_s, last, mask_all, first)

        @pl.when(ki == 0)
        def _():
            sweep(0, True, True)

        @pl.when(ki > 0)
        def _():
            sweep(ki // r, False, False)

        dkn_ref[...] = dk_s[:, :128].astype(dkn_ref.dtype)
        dkr_ref[...] = dk_s[:, 128:]
        dv_ref[...] = dv_s[...].astype(dv_ref.dtype)

    return pl.pallas_call(
        body, name=name, grid=(H, nb),
        in_specs=[pl.BlockSpec((lp, MLA_QKP), lambda h, ki: (0, h)),
                  pl.BlockSpec((t, 128), lambda h, ki: (ki, h)),
                  pl.BlockSpec((t, 128), lambda h, ki: (ki, 0)),
                  pl.BlockSpec((t, 128), lambda h, ki: (ki, H + h)),
                  pl.BlockSpec((None, 1, lp), lambda h, ki: (h, 0, 0)),
                  pl.BlockSpec((None, 1, lp), lambda h, ki: (h, 0, 0)),
                  pl.BlockSpec((lp, 128), lambda h, ki: (0, h))],
        out_specs=[pl.BlockSpec((lp, MLA_QKP), lambda h, ki: (0, h)),
                   pl.BlockSpec((t, 128), lambda h, ki: (ki, h)),
                   pl.BlockSpec((t, 128), lambda h, ki: (ki, h)),
                   pl.BlockSpec((t, 128), lambda h, ki: (ki, h))],
        out_shape=[jax.ShapeDtypeStruct((lp, H * MLA_QKP), f32), jax.ShapeDtypeStruct((lp, MLA_V_W), bf16),
                   jax.ShapeDtypeStruct((lp, MLA_V_W), bf16), jax.ShapeDtypeStruct((lp, MLA_V_W), f32)],
        scratch_shapes=[pltpu.VMEM((t, MLA_QKP), f32), pltpu.VMEM((t, 128), f32)] + [pltpu.VMEM((t, tq), f32)] * 4,
        compiler_params=pltpu.CompilerParams(dimension_semantics=("arbitrary", "arbitrary")),
    )(q, kvu, kr, kvu, lse, delta, do)


def _exchange(x, *, gather, name):
    blk = x.shape if gather else x.shape[1:]

    def body(x_ref, o_ref, send_sems, recv_sems, local_sem):
        mx, my, mc = lax.axis_index("x"), lax.axis_index("y"), lax.axis_index("c")
        me = 4 * mx + 2 * my + mc
        own = pltpu.make_async_copy(x_ref if gather else x_ref.at[me], o_ref.at[me], local_sem)
        own.start()
        sends, peers = [], []
        for k in range(1, N_DEV):
            px = 1 - mx if k & 4 else mx
            py = 1 - my if k & 2 else my
            pc = 1 - mc if k & 1 else mc
            peer = 4 * px + 2 * py + pc
            cp = pltpu.make_async_remote_copy(
                src_ref=x_ref if gather else x_ref.at[peer], dst_ref=o_ref.at[me],
                send_sem=send_sems.at[k - 1], recv_sem=recv_sems.at[k - 1],
                device_id=(px, py, pc), device_id_type=MESH_ID)
            cp.start()
            sends.append(cp)
            peers.append(peer)
        for k in range(1, N_DEV):
            pltpu.make_async_remote_copy(
                src_ref=o_ref.at[peers[k - 1]], dst_ref=o_ref.at[peers[k - 1]],
                send_sem=send_sems.at[k - 1], recv_sem=recv_sems.at[k - 1],
                device_id=(mx, my, mc), device_id_type=MESH_ID).wait_recv()
        for cp in sends:
            cp.wait_send()
        own.wait()

    return pl.pallas_call(
        body, name=name,
        in_specs=[pl.BlockSpec(memory_space=pltpu.HBM)], out_specs=pl.BlockSpec(memory_space=pltpu.HBM),
        out_shape=jax.ShapeDtypeStruct((N_DEV,) + tuple(blk), x.dtype),
        scratch_shapes=[pltpu.SemaphoreType.DMA((N_DEV - 1,)), pltpu.SemaphoreType.DMA((N_DEV - 1,)),
                        pltpu.SemaphoreType.DMA],
    )(x)


def _reduce_adamw(parts, w, m, v, *, name):
    r = w.shape[0]
    tr = _pick(r, (1280, 1024, 512, 256, 128, 64, 48, 32, 16, 8))

    def body(p_ref, w_ref, m_ref, v_ref, g_ref, d_ref, nm_ref, nv_ref):
        g = p_ref[0].astype(f32)
        for s in range(1, N_DEV):
            g = g + p_ref[s].astype(f32)
        mm = ADAM_B1 * m_ref[...] + (1.0 - ADAM_B1) * g
        vv = ADAM_B2 * v_ref[...] + (1.0 - ADAM_B2) * (g * g)
        m_hat = mm / (1.0 - ADAM_B1 ** ADAM_STEP)
        v_hat = vv / (1.0 - ADAM_B2 ** ADAM_STEP)
        g_ref[...] = g
        d_ref[...] = -ADAM_LR * (m_hat / (jnp.sqrt(v_hat) + ADAM_EPS) + ADAM_WD * w_ref[...])
        nm_ref[...] = mm
        nv_ref[...] = vv

    spec = pl.BlockSpec((tr, 128), lambda i: (i, 0))
    return pl.pallas_call(
        body, name=name, grid=(r // tr,),
        in_specs=[pl.BlockSpec((N_DEV, tr, 128), lambda i: (0, i, 0)), spec, spec, spec],
        out_specs=[spec] * 4, out_shape=[jax.ShapeDtypeStruct((r, 128), f32)] * 4,
    )(parts, w, m, v)


_SHARDED = ("gdn_w_in", "gdn_w_out", "kv_w_down", "kv_w_up", "mla_w_in", "mla_w_q_up", "mla_w_out", "meta_tokens", "gdn_conv_w")
_COL_SHARDED = {"gdn_w_in", "kv_w_up", "mla_w_in", "mla_w_q_up", "meta_tokens", "gdn_conv_w"}
_N_BF16 = 7
_REPLICATED = ("pre_norm", "post_norm", "gdn_a_log", "gdn_dt_bias", "gdn_out_norm", "kv_norm", "kv_latent_norm",
               "mla_q_latent_norm")


def _rows128(a):
    flat = a.reshape(-1)
    pad = (-flat.shape[0]) % 128
    if pad:
        flat = jnp.pad(flat, (0, pad))
    return flat.reshape(-1, 128)


def _pack(arrs, row_multiple):
    parts = [_rows128(a) for a in arrs]
    buf = jnp.concatenate(parts, axis=0)
    pad = (-buf.shape[0]) % row_multiple
    if pad:
        buf = jnp.pad(buf, ((0, pad), (0, 0)))
    return buf


def _unpack(buf, shapes):
    out, r = [], 0
    for shp in shapes:
        n = math.prod(shp)
        rows = -(-n // 128)
        out.append(buf[r:r + rows].reshape(-1)[:n].reshape(shp))
        r += rows
    return out


def _unshard(g, full_shape, col):
    if col:
        return jnp.transpose(g, (1, 0, 2)).reshape(full_shape)
    return g.reshape(full_shape)


def _to_shards(a, col):
    r, c = a.shape
    if col:
        return jnp.transpose(a.reshape(r, N_DEV, c // N_DEV), (1, 0, 2))
    return a.reshape(N_DEV, r // N_DEV, c)


def _pad_cols(a, width):
    return jnp.pad(a, ((0, 0), (0, width - a.shape[1])))


def _rope_tables(lp):
    inv = ROPE_THETA ** (-jnp.arange(0, MLA_ROPE, 2, dtype=f32) / MLA_ROPE)
    pos = (jnp.arange(lp, dtype=jnp.int32) - PAD_FRONT).astype(f32)
    ang = pos[:, None] * inv[None, :]
    cos, sin = jnp.cos(ang), jnp.sin(ang)
    z = jnp.zeros((lp, 64), f32)
    return jnp.concatenate([cos, cos, z], axis=1), jnp.concatenate([-sin, sin, z], axis=1)


def kernel(x, meta_tokens, pre_norm, post_norm, gdn_w_in, gdn_conv_w, gdn_a_log, gdn_dt_bias, gdn_out_norm, gdn_w_out, kv_norm, kv_w_down, kv_latent_norm, kv_w_up, mla_w_in, mla_q_latent_norm, mla_w_q_up, mla_w_out, loss_target, m_meta_tokens, m_pre_norm, m_post_norm, m_gdn_w_in, m_gdn_conv_w, m_gdn_a_log, m_gdn_dt_bias, m_gdn_out_norm, m_gdn_w_out, m_kv_norm, m_kv_w_down, m_kv_latent_norm, m_kv_w_up, m_mla_w_in, m_mla_q_latent_norm, m_mla_w_q_up, m_mla_w_out, v_meta_tokens, v_pre_norm, v_post_norm, v_gdn_w_in, v_gdn_conv_w, v_gdn_a_log, v_gdn_dt_bias, v_gdn_out_norm, v_gdn_w_out, v_kv_norm, v_kv_w_down, v_kv_latent_norm, v_kv_w_up, v_mla_w_in, v_mla_q_latent_norm, v_mla_w_q_up, v_mla_w_out):
    W = dict(meta_tokens=meta_tokens, pre_norm=pre_norm, post_norm=post_norm, gdn_w_in=gdn_w_in, gdn_conv_w=gdn_conv_w,
             gdn_a_log=gdn_a_log, gdn_dt_bias=gdn_dt_bias, gdn_out_norm=gdn_out_norm, gdn_w_out=gdn_w_out, kv_norm=kv_norm,
             kv_w_down=kv_w_down, kv_latent_norm=kv_latent_norm, kv_w_up=kv_w_up, mla_w_in=mla_w_in,
             mla_q_latent_norm=mla_q_latent_norm, mla_w_q_up=mla_w_q_up, mla_w_out=mla_w_out)
    M = dict(meta_tokens=m_meta_tokens, pre_norm=m_pre_norm, post_norm=m_post_norm, gdn_w_in=m_gdn_w_in, gdn_conv_w=m_gdn_conv_w,
             gdn_a_log=m_gdn_a_log, gdn_dt_bias=m_gdn_dt_bias, gdn_out_norm=m_gdn_out_norm, gdn_w_out=m_gdn_w_out, kv_norm=m_kv_norm,
             kv_w_down=m_kv_w_down, kv_latent_norm=m_kv_latent_norm, kv_w_up=m_kv_w_up, mla_w_in=m_mla_w_in,
             mla_q_latent_norm=m_mla_q_latent_norm, mla_w_q_up=m_mla_w_q_up, mla_w_out=m_mla_w_out)
    V = dict(meta_tokens=v_meta_tokens, pre_norm=v_pre_norm, post_norm=v_post_norm, gdn_w_in=v_gdn_w_in, gdn_conv_w=v_gdn_conv_w,
             gdn_a_log=v_gdn_a_log, gdn_dt_bias=v_gdn_dt_bias, gdn_out_norm=v_gdn_out_norm, gdn_w_out=v_gdn_w_out, kv_norm=v_kv_norm,
             kv_w_down=v_kv_w_down, kv_latent_norm=v_kv_latent_norm, kv_w_up=v_kv_w_up, mla_w_in=v_mla_w_in,
             mla_q_latent_norm=v_mla_q_latent_norm, mla_w_q_up=v_mla_w_q_up, mla_w_out=v_mla_w_out)
    order = list(W)

    n_tok = x.shape[1]
    assert n_tok % GDN_CHUNK == 0
    n_real = ROW0 + n_tok
    lp = -(-n_real // ROW_TILE) * ROW_TILE
    n_real_chunks = n_real // GDN_CHUNK

    shard2d = {n: W[n].reshape(W[n].shape[-2:]) for n in _SHARDED}
    full_shape = {n: ((s.shape[0], s.shape[1] * N_DEV) if n in _COL_SHARDED else (s.shape[0] * N_DEV, s.shape[1]))
                  for n, s in shard2d.items()}
    big, small = _SHARDED[:_N_BF16], _SHARDED[_N_BF16:]
    g_big = _exchange(_pack([shard2d[n].astype(bf16) for n in big], 16), gather=True, name="gather_weights")
    g_small = _exchange(_pack([shard2d[n] for n in small], 8), gather=True, name="gather_meta_conv")
    full = {}
    for names, buf in ((big, g_big), (small, g_small)):
        r = 0
        for n in names:
            shp = shard2d[n].shape
            rows = math.prod(shp) // 128
            blocks = buf[:, r:r + rows].reshape((N_DEV,) + shp)
            full[n] = _unshard(blocks, full_shape[n], n in _COL_SHARDED)
            r += rows

    w_in = full["gdn_w_in"]
    s1 = GDN_CONV_W + GDN_V_W
    w_in_p = jnp.concatenate([w_in[:, :s1], _pad_cols(w_in[:, s1:s1 + 16], 128), _pad_cols(w_in[:, s1 + 16:], 128)], axis=1)
    wd = full["kv_w_down"]
    zc = jnp.zeros((D_MODEL, 64), bf16)
    wd2 = jnp.concatenate([wd, zc, jnp.zeros((D_MODEL, 128), bf16), wd[:, 160:192], wd[:, 128:160], zc], axis=1)
    wup_p = jnp.transpose(full["kv_w_up"].reshape(MLA_KV_RANK, MLA_HEADS, 2, 128), (0, 2, 1, 3)).reshape(MLA_KV_RANK, 2 * MLA_V_W)
    wq = full["mla_w_q_up"].reshape(MLA_Q_RANK, MLA_HEADS, MLA_QK)
    zq64 = jnp.zeros((MLA_Q_RANK, MLA_HEADS, 64), bf16)
    wq_plain = jnp.concatenate([wq, zq64], axis=2).reshape(MLA_Q_RANK, MLA_HEADS * MLA_QKP)
    wq_swap = jnp.concatenate([jnp.zeros((MLA_Q_RANK, MLA_HEADS, 128), bf16), wq[:, :, 160:192], wq[:, :, 128:160], zq64],
                              axis=2).reshape(MLA_Q_RANK, MLA_HEADS * MLA_QKP)
    wq2 = jnp.concatenate([wq_plain, wq_swap], axis=1)
    w_mla_in, w_gdn_out, w_mla_out = full["mla_w_in"], full["gdn_w_out"], full["mla_w_out"]
    conv_w = full["gdn_conv_w"]
    pre0, pre1 = pre_norm[0:1], pre_norm[1:2]
    post0, post1 = post_norm[0:1], post_norm[1:2]
    alog_p, dtb_p = _pad_cols(gdn_a_log, 128), _pad_cols(gdn_dt_bias, 128)
    kvn, kvln = kv_norm.reshape(1, -1), kv_latent_norm.reshape(1, -1)

    h0 = jnp.concatenate([jnp.zeros((PAD_FRONT, D_MODEL), f32), full["meta_tokens"], x[0],
                          jnp.zeros((lp - n_real, D_MODEL), f32)], axis=0)
    tgt = jnp.concatenate([jnp.zeros((ROW0, D_MODEL), f32), loss_target[0], jnp.zeros((lp - n_real, D_MODEL), f32)], axis=0)
    cos_k, sin_k = _rope_tables(lp)
    one = jnp.ones((lp, 128), f32)
    cos_q = jnp.concatenate([one, cos_k], axis=1)
    sin_q = jnp.concatenate([jnp.zeros((lp, 128), f32), sin_k], axis=1)

    (hn0,) = _rowwise(_st_prenorm, [(h0, None, 0)], [], [pre0], [(D_MODEL, bf16, None)], name="f_prenorm0")
    proj = _matmul(hn0, w_in_p, name="f_gdn_in")
    conv = _conv_fwd(proj, conv_w, col_blocks=GDN_CONV_W // CONV_BC, name="f_conv")
    (qn,) = _rowwise(_st_gdn_q, [(conv, GDN_QK_W, 0)], [], [], [(GDN_QK_W, f32, GDN_QK_W)], heads=GDN_QK_HEADS, name="f_gdn_q")
    (kn,) = _rowwise(_st_gdn_k, [(conv, GDN_QK_W, 1)], [], [], [(GDN_QK_W, f32, GDN_QK_W)], heads=GDN_QK_HEADS, name="f_gdn_k")
    (vv,) = _rowwise(_st_gdn_v, [(conv, GDN_V_W, 1)], [], [], [(GDN_V_W, f32, GDN_V_W)], name="f_gdn_v")
    gate_rows = [(proj, 128, s1 // 128), (proj, 128, s1 // 128 + 1)]
    beta, gdec = _rowwise(_st_gdn_gate, gate_rows, [], [alog_p, dtb_p], [(128, f32, None)] * 2, name="f_gdn_gate")
    o_gdn, states = _gdn_fwd(qn, kn, vv, beta, gdec, n_real_chunks=n_real_chunks, name="f_gdn")
    out_rows = [(o_gdn, GDN_V_W, 0), (proj, GDN_V_W, GDN_CONV_W // GDN_V_W)]
    (og,) = _rowwise(_st_gdn_out, out_rows, [], [gdn_out_norm], [(GDN_V_W, bf16, GDN_V_W)], heads=GDN_V_HEADS, name="f_gdn_out")
    y0 = _matmul(og, w_gdn_out, name="f_gdn_wout")
    mid_rows = [(h0, None, 0), (y0, None, 0)]
    h1, hn1, hkv = _rowwise(_st_mid, mid_rows, [], [post0, pre1, kvn],
                            [(D_MODEL, f32, None), (D_MODEL, bf16, None), (D_MODEL, bf16, None)], name="f_mid")
    ckr = _matmul(hkv, wd2, name="f_kv_down")
    proj2 = _matmul(hn1, w_mla_in, name="f_mla_in")
    lat_rows = [(ckr, None, 0), (proj2, MLA_Q_RANK, 0)]
    lat_nd = [(cos_k, None, 0), (sin_k, None, 0)]
    c_kv, k_rope, c_q = _rowwise(_st_latent, lat_rows, lat_nd, [kvln, mla_q_latent_norm],
                                 [(128, bf16, None), (128, bf16, None), (MLA_Q_RANK, bf16, None)], name="f_latent")
    kvu = _matmul(c_kv, wup_p, out_dtype=bf16, name="f_kv_up")
    qq = _matmul(c_q, wq2, name="f_q_up")
    q_half = MLA_HEADS * MLA_QKP // 2
    q_rows = [(qq, q_half, 0), (qq, q_half, 2)]
    q_nd = [(cos_q, None, 0), (sin_q, None, 0)]
    (q_att,) = _rowwise(_st_q_rope, q_rows, q_nd, [], [(2 * q_half, bf16, q_half)], ncol=2, heads=MLA_HEADS // 2, name="f_q_rope")
    o_att, lse = _attention_fwd(q_att, kvu, k_rope, name="f_attention")
    gate2_rows = [(o_att, MLA_Q_RANK, 0), (proj2, MLA_Q_RANK, 1)]
    n_gate = MLA_V_W // MLA_Q_RANK
    (og2,) = _rowwise(_st_gate, gate2_rows, [], [], [(MLA_V_W, bf16, MLA_Q_RANK)], ncol=n_gate, name="f_mla_gate")
    y1 = _matmul(og2, w_mla_out, name="f_mla_wout")
    st_loss = _make_st_loss(n_tok)
    loss_rows_in = [(h1, None, 0), (y1, None, 0)]
    (loss_rows,) = _rowwise(st_loss, loss_rows_in, [(tgt, None, 0)], [post1], [(1, f32, None)], name="f_loss")
    loss = lax.psum(jnp.sum(loss_rows), ("x", "y", "c"))

    ones_ct = jnp.ones((lp, 1), f32)
    (dh1_a, dy1), (dpost1,) = _rowwise_vjp(st_loss, loss_rows_in, [(tgt, None, 0)], [post1], [(ones_ct, None, 0)], name="b_loss")
    dog2 = _matmul(dy1, w_mla_out, tb=True, name="b_mla_wout_x")
    dw_mla_out = _matmul(og2, dy1, ta=True, name="b_mla_wout_w")
    (do_att, dz2), _ = _rowwise_vjp(_st_gate, gate2_rows, [], [], [(dog2, MLA_Q_RANK, 0)], ncol=n_gate, name="b_mla_gate",
                                    grad_dtypes=[bf16, bf16])
    delta = _attention_delta(o_att, do_att, name="b_attention_delta")
    dq_att, dkn, dvv, dkr_h = _attention_bwd(q_att, kvu, k_rope, lse, delta, do_att, name="b_attention")
    (dqa, dqb), _ = _rowwise_vjp(_st_q_rope, q_rows, q_nd, [], [(dq_att, q_half, 0)], ncol=2, heads=MLA_HEADS // 2, name="b_q_rope",
                                 grad_dtypes=[bf16, bf16])
    dqq = jnp.concatenate([dqa, dqb], axis=1)
    dc_q = _matmul(dqq, wq2, tb=True, name="b_q_up_x")
    dwq2 = _matmul(c_q, dqq, ta=True, name="b_q_up_w")
    dkvu = jnp.concatenate([dkn, dvv], axis=1)
    dc_kv = _matmul(dkvu, wup_p, tb=True, name="b_kv_up_x")
    dwup_p = _matmul(c_kv, dkvu, ta=True, name="b_kv_up_w")

    def lat_ct(cv):
        dkr = cv[1][:, 0:128]
        for h in range(1, MLA_HEADS):
            dkr = dkr + cv[1][:, h * 128:(h + 1) * 128]
        return [cv[0], dkr, cv[2]]

    (dckr, dcq_pre), (dkvln, dqln) = _rowwise_vjp(
        _st_latent, lat_rows, lat_nd, [kvln, mla_q_latent_norm],
        [(dc_kv, None, 0), (dkr_h, None, 0), (dc_q, None, 0)], ct_pre=lat_ct, name="b_latent", grad_dtypes=[bf16, bf16])
    dproj2 = jnp.concatenate([dcq_pre, dz2], axis=1)
    dhn1 = _matmul(dproj2, w_mla_in, tb=True, name="b_mla_in_x")
    dw_mla_in = _matmul(hn1, dproj2, ta=True, name="b_mla_in_w")
    dhkv = _matmul(dckr, wd2, tb=True, name="b_kv_down_x")
    dwd2 = _matmul(hkv, dckr, ta=True, name="b_kv_down_w")
    (dh0_a, dy0), (dpost0, dpre1, dkvn) = _rowwise_vjp(
        _st_mid, mid_rows, [], [post0, pre1, kvn], [(dh1_a, None, 0), (dhn1, None, 0), (dhkv, None, 0)], name="b_mid")
    dog = _matmul(dy0, w_gdn_out, tb=True, name="b_gdn_wout_x")
    dw_gdn_out = _matmul(og, dy0, ta=True, name="b_gdn_wout_w")
    (do_gdn, dz), (dout_norm,) = _rowwise_vjp(_st_gdn_out, out_rows, [], [gdn_out_norm], [(dog, GDN_V_W, 0)], heads=GDN_V_HEADS,
                                              name="b_gdn_out", grad_dtypes=[f32, bf16])
    dq_g, dk_g, dv_g, dbeta, dgdec = _gdn_bwd(qn, kn, vv, beta, gdec, states, do_gdn, n_real_chunks=n_real_chunks, name="b_gdn")
    (db_col, da_col), (dalog_p, ddtb_p) = _rowwise_vjp(
        _st_gdn_gate, gate_rows, [], [alog_p, dtb_p], [(dbeta, None, 0), (dgdec, None, 0)], name="b_gdn_gate",
        grad_dtypes=[bf16, bf16])
    (dconv_q,), _ = _rowwise_vjp(_st_gdn_q, [(conv, GDN_QK_W, 0)], [], [], [(dq_g, GDN_QK_W, 0)], heads=GDN_QK_HEADS, name="b_gdn_q")
    (dconv_k,), _ = _rowwise_vjp(_st_gdn_k, [(conv, GDN_QK_W, 1)], [], [], [(dk_g, GDN_QK_W, 0)], heads=GDN_QK_HEADS, name="b_gdn_k")
    (dconv_v,), _ = _rowwise_vjp(_st_gdn_v, [(conv, GDN_V_W, 1)], [], [], [(dv_g, GDN_V_W, 0)], name="b_gdn_v")
    nq_b = GDN_QK_W // CONV_BC
    dpre_q, dcw_q = _conv_bwd(dconv_q, proj, conv_w, x_off=0, w_off=0, name="b_conv_q")
    dpre_k, dcw_k = _conv_bwd(dconv_k, proj, conv_w, x_off=nq_b, w_off=nq_b, name="b_conv_k")
    dpre_v, dcw_v = _conv_bwd(dconv_v, proj, conv_w, x_off=2 * nq_b, w_off=2 * nq_b, name="b_conv_v")
    dproj = jnp.concatenate([dpre_q, dpre_k, dpre_v, dz, db_col, da_col], axis=1)
    dhn0 = _matmul(dproj, w_in_p, tb=True, name="b_gdn_in_x")
    dw_in_p = _matmul(hn0, dproj, ta=True, name="b_gdn_in_w")
    (dh0,), (dpre0,) = _rowwise_vjp(_st_prenorm, [(h0, None, 0)], [], [pre0], [(dhn0, None, 0)], extra=[dh0_a], name="b_prenorm0")

    grad_x = dh0[ROW0:n_real][None]
    G = {}
    G["meta_tokens"] = dh0[PAD_FRONT:ROW0]
    G["gdn_w_in"] = jnp.concatenate([dw_in_p[:, :s1 + 16], dw_in_p[:, s1 + 128:s1 + 144]], axis=1)
    G["gdn_conv_w"] = jnp.concatenate([dcw_q, dcw_k, dcw_v], axis=1)
    G["gdn_w_out"] = dw_gdn_out
    G["kv_w_down"] = jnp.concatenate([dwd2[:, :128], dwd2[:, 128:160] + dwd2[:, 416:448], dwd2[:, 160:192] + dwd2[:, 384:416]], axis=1)
    G["kv_w_up"] = jnp.transpose(dwup_p.reshape(MLA_KV_RANK, 2, MLA_HEADS, 128), (0, 2, 1, 3)).reshape(MLA_KV_RANK, 2 * MLA_V_W)
    G["mla_w_in"] = dw_mla_in
    dqp = dwq2[:, :MLA_HEADS * MLA_QKP].reshape(MLA_Q_RANK, MLA_HEADS, MLA_QKP)
    dqs = dwq2[:, MLA_HEADS * MLA_QKP:].reshape(MLA_Q_RANK, MLA_HEADS, MLA_QKP)
    G["mla_w_q_up"] = jnp.concatenate([dqp[:, :, :128], dqp[:, :, 128:160] + dqs[:, :, 160:192],
                                       dqp[:, :, 160:192] + dqs[:, :, 128:160]], axis=2).reshape(MLA_Q_RANK, MLA_HEADS * MLA_QK)
    G["mla_w_out"] = dw_mla_out
    G["pre_norm"] = jnp.concatenate([dpre0, dpre1], axis=0)
    G["post_norm"] = jnp.concatenate([dpost0, dpost1], axis=0)
    G["gdn_a_log"] = dalog_p[:, :GDN_V_HEADS]
    G["gdn_dt_bias"] = ddtb_p[:, :GDN_V_HEADS]
    G["gdn_out_norm"] = dout_norm
    G["kv_norm"] = dkvn.reshape(-1)
    G["kv_latent_norm"] = dkvln.reshape(-1)
    G["mla_q_latent_norm"] = dqln

    send = jnp.concatenate([_to_shards(G[n], n in _COL_SHARDED).reshape(N_DEV, -1, 128).astype(bf16) for n in _SHARDED], axis=1)
    parts = _exchange(send, gather=False, name="scatter_grads")
    w_s, m_s, v_s = (_pack([d[n] for n in _SHARDED], 8) for d in (W, M, V))
    res_s = _reduce_adamw(parts, w_s, m_s, v_s, name="adamw_sharded")
    small_send = _pack([G[n] for n in _REPLICATED], 8)
    parts_r = _exchange(small_send, gather=True, name="gather_small_grads")
    w_r, m_r, v_r = (_pack([d[n] for n in _REPLICATED], 8) for d in (W, M, V))
    res_r = _reduce_adamw(parts_r, w_r, m_r, v_r, name="adamw_replicated")

    outs = {}
    for kind, bs, br in zip(("grad", "delta", "new_m", "new_v"), res_s, res_r):
        for n, a in zip(_SHARDED, _unpack(bs, [W[n].shape for n in _SHARDED])):
            outs[kind, n] = a
        for n, a in zip(_REPLICATED, _unpack(br, [W[n].shape for n in _REPLICATED])):
            outs[kind, n] = a
    return (loss, grad_x, *[outs[k, n] for k in ("grad", "delta", "new_m", "new_v") for n in order])
```

```python
import functools
import math

import jax
import jax.numpy as jnp
from jax import lax
from jax.experimental import pallas as pl
from jax.experimental.pallas import tpu as pltpu

f32, bf16 = jnp.float32, jnp.bfloat16
HIGHEST = lax.Precision.HIGHEST
MESH_ID = pl.DeviceIdType.MESH

N_DEV = 8
D_MODEL = 1024
N_META = 16
NORM_EPS = 1e-6
PAD_FRONT = 48
ROW0 = PAD_FRONT + N_META
GDN_QK_HEADS, GDN_V_HEADS, GDN_D = 8, 16, 128
GDN_CHUNK = 64
GDN_QK_W, GDN_V_W = GDN_QK_HEADS * GDN_D, GDN_V_HEADS * GDN_D
GDN_CONV_W = 2 * GDN_QK_W + GDN_V_W
GDN_IN_W = GDN_CONV_W + GDN_V_W + 2 * GDN_V_HEADS
GDN_IN_WP = GDN_CONV_W + GDN_V_W + 2 * 128
MLA_HEADS, MLA_NOPE, MLA_ROPE, MLA_V = 16, 128, 64, 128
MLA_Q_RANK, MLA_KV_RANK = 256, 128
MLA_QK = MLA_NOPE + MLA_ROPE
MLA_QKP = 256
MLA_V_W = MLA_HEADS * MLA_V
ROPE_THETA = 10000.0
NEG = -1e30
ROW_TILE = 256
ATT_TILE = 256

ADAM_LR, ADAM_B1, ADAM_B2, ADAM_EPS, ADAM_WD, ADAM_STEP = 0.001, 0.9, 0.999, 1e-08, 0.01, 10

NN = ((1,), (0,))
NT = ((1,), (1,))
TN = ((0,), (0,))


def _pick(dim, prefs):
    for p in prefs:
        if dim % p == 0:
            return p
    return dim


def _dlo(a, b, dims):
    return lax.dot_general(a.astype(bf16), b.astype(bf16), (dims, ((), ())), preferred_element_type=f32)


def _dhi(a, b, dims):
    return lax.dot_general(a, b, (dims, ((), ())), precision=HIGHEST, preferred_element_type=f32)


def _matmul(a, b, *, ta=False, tb=False, out_dtype=f32, name):
    assert not (ta and tb)
    if ta:
        kdim, m = a.shape
    else:
        m, kdim = a.shape
    n = b.shape[0] if tb else b.shape[1]
    assert (b.shape[1] if tb else b.shape[0]) == kdim
    tm = _pick(m, (1024, 768, 512, 384, 256, 128))
    tn = _pick(n, (1024, 768, 640, 512, 256, 128))
    tk = _pick(kdim, (1024, 768, 640, 512, 256, 128))
    nk = kdim // tk
    dims = TN if ta else (NT if tb else NN)

    def body(a_ref, b_ref, o_ref, acc_ref):
        k = pl.program_id(2)

        @pl.when(k == 0)
        def _():
            acc_ref[...] = jnp.zeros_like(acc_ref)

        acc_ref[...] += lax.dot_general(a_ref[...].astype(bf16), b_ref[...].astype(bf16), (dims, ((), ())),
                                        preferred_element_type=f32)

        @pl.when(k == nk - 1)
        def _():
            o_ref[...] = acc_ref[...].astype(o_ref.dtype)

    a_spec = pl.BlockSpec((tk, tm), lambda i, j, k: (k, i)) if ta else pl.BlockSpec((tm, tk), lambda i, j, k: (i, k))
    b_spec = pl.BlockSpec((tn, tk), lambda i, j, k: (j, k)) if tb else pl.BlockSpec((tk, tn), lambda i, j, k: (k, j))
    return pl.pallas_call(
        body, name=name, grid=(m // tm, n // tn, nk),
        in_specs=[a_spec, b_spec], out_specs=pl.BlockSpec((tm, tn), lambda i, j, k: (i, j)),
        out_shape=jax.ShapeDtypeStruct((m, n), out_dtype),
        scratch_shapes=[pltpu.VMEM((tm, tn), f32)],
        compiler_params=pltpu.CompilerParams(dimension_semantics=("parallel", "parallel", "arbitrary")),
    )(a, b)


def _row_spec(item, tr):
    a, bc, off = item
    if bc is None:
        return pl.BlockSpec((tr, a.shape[1]), lambda i, j: (i, 0))
    return pl.BlockSpec((tr, bc), lambda i, j, off=off: (i, j + off))


def _param_spec(p):
    return pl.BlockSpec(p.shape, lambda i, j: (0, 0))


def _row_tile(lp, items):
    widest = max(a.shape[1] if bc is None else bc for (a, bc, _) in items)
    return ROW_TILE if widest >= 1024 else _pick(lp, (768, 512, 256))


def _head_cols(tiles, h, heads):
    return [x[:, h * (x.shape[1] // heads):(h + 1) * (x.shape[1] // heads)] for x in tiles]


def _rowwise(fn, rows, nodiff, params, outs, *, ncol=1, heads=1, name):
    lp = rows[0][0].shape[0]
    tr = _row_tile(lp, rows)
    nr, nd = len(rows), len(nodiff)

    def body(*refs):
        rv = [r[...].astype(f32) for r in refs[:nr]]
        nv = [r[...] for r in refs[nr:nr + nd]]
        pv = [r[...] for r in refs[nr + nd:nr + nd + len(params)]]
        per_head = [fn(_head_cols(rv, h, heads), nv, pv) for h in range(heads)]
        res = [jnp.concatenate(list(vals), axis=1) if heads > 1 else vals[0] for vals in zip(*per_head)]
        for ref, val in zip(refs[nr + nd + len(params):], res):
            ref[...] = val.astype(ref.dtype)

    out_specs = [pl.BlockSpec((tr, c if bc is None else bc), (lambda i, j: (i, 0)) if bc is None else (lambda i, j: (i, j)))
                 for (c, _, bc) in outs]
    return pl.pallas_call(
        body, name=name, grid=(lp // tr, ncol),
        in_specs=[_row_spec(it, tr) for it in rows + nodiff] + [_param_spec(p) for p in params],
        out_specs=out_specs,
        out_shape=[jax.ShapeDtypeStruct((lp, c), dt) for (c, dt, _) in outs],
    )(*[it[0] for it in rows + nodiff], *params)


def _rowwise_vjp(fn, rows, nodiff, params, cts, *, ncol=1, heads=1, name, ct_pre=None, extra=None, grad_dtypes=None):
    lp = rows[0][0].shape[0]
    tr = _row_tile(lp, rows)
    nr, nd, npar, nct = len(rows), len(nodiff), len(params), len(cts)
    extra = extra or [None] * nr
    grad_dtypes = grad_dtypes or [f32] * nr
    ex_items = [(e, rows[k][1], 0) for k, e in enumerate(extra) if e is not None]
    ex_pos = [k for k, e in enumerate(extra) if e is not None]
    for (a, bc, _) in rows:
        assert bc is not None or ncol == 1

    def body(*refs):
        pos = 0
        rv = [r[...].astype(f32) for r in refs[pos:pos + nr]]; pos += nr
        nv = [r[...] for r in refs[pos:pos + nd]]; pos += nd
        pv = [r[...] for r in refs[pos:pos + npar]]; pos += npar
        cv = [r[...].astype(f32) for r in refs[pos:pos + nct]]; pos += nct
        ev = [r[...].astype(f32) for r in refs[pos:pos + len(ex_items)]]; pos += len(ex_items)
        drow_refs = refs[pos:pos + nr]; pos += nr
        dpar_refs = refs[pos:pos + npar]
        ctv = ct_pre(cv) if ct_pre is not None else cv
        drow_h, dpar = [], None
        for h in range(heads):
            outs, vjp_fn = jax.vjp(lambda rr, pp: fn(rr, nv, pp), _head_cols(rv, h, heads), pv)
            dr, dp = vjp_fn([c.astype(o.dtype) for c, o in zip(_head_cols(ctv, h, heads), outs)])
            drow_h.append(dr)
            dpar = dp if dpar is None else [a + b for a, b in zip(dpar, dp)]
        drow = [jnp.concatenate(list(vals), axis=1) if heads > 1 else vals[0] for vals in zip(*drow_h)]
        for k, e in zip(ex_pos, ev):
            drow[k] = drow[k] + e
        for ref, val in zip(drow_refs, drow):
            ref[...] = val.astype(ref.dtype)
        first = jnp.logical_and(pl.program_id(0) == 0, pl.program_id(1) == 0)

        @pl.when(first)
        def _():
            for ref, val in zip(dpar_refs, dpar):
                ref[...] = val

        @pl.when(jnp.logical_not(first))
        def _():
            for ref, val in zip(dpar_refs, dpar):
                ref[...] += val

    drow_shapes, drow_specs = [], []
    for (a, bc, _), dt in zip(rows, grad_dtypes):
        if bc is None:
            drow_shapes.append(jax.ShapeDtypeStruct((lp, a.shape[1]), dt))
            drow_specs.append(pl.BlockSpec((tr, a.shape[1]), lambda i, j: (i, 0)))
        else:
            drow_shapes.append(jax.ShapeDtypeStruct((lp, ncol * bc), dt))
            drow_specs.append(pl.BlockSpec((tr, bc), lambda i, j: (i, j)))
    res = pl.pallas_call(
        body, name=name, grid=(lp // tr, ncol),
        in_specs=[_row_spec(it, tr) for it in rows + nodiff] + [_param_spec(p) for p in params]
        + [_row_spec(it, tr) for it in cts + ex_items],
        out_specs=drow_specs + [_param_spec(p) for p in params],
        out_shape=drow_shapes + [jax.ShapeDtypeStruct(p.shape, f32) for p in params],
        compiler_params=pltpu.CompilerParams(dimension_semantics=("arbitrary", "arbitrary")),
    )(*[it[0] for it in rows + nodiff], *params, *[it[0] for it in cts + ex_items])
    return res[:nr], res[nr:]


def _rms(x, g):
    return x * lax.rsqrt(jnp.mean(x * x, axis=-1, keepdims=True) + NORM_EPS) * g


def _l2n(x):
    return x * lax.rsqrt(jnp.sum(x * x, axis=-1, keepdims=True) + NORM_EPS)


def _sigmoid(x):
    return 1.0 / (1.0 + jnp.exp(-x))


def _silu(x):
    return x * _sigmoid(x)


def _softplus(x):
    return jnp.maximum(x, 0.0) + jnp.log(1.0 + jnp.exp(-jnp.abs(x)))


def _row_ids(shape):
    return pl.program_id(0) * shape[0] + lax.broadcasted_iota(jnp.int32, shape, 0)


def _st_prenorm(r, n, p):
    return [_rms(r[0], p[0])]


def _st_gdn_q(r, n, p):
    return [_l2n(_silu(r[0])) * (GDN_D ** -0.5)]


def _st_gdn_k(r, n, p):
    return [_l2n(_silu(r[0]))]


def _st_gdn_v(r, n, p):
    return [_silu(r[0])]


def _st_gdn_gate(r, n, p):
    real = _row_ids(r[0].shape) >= PAD_FRONT
    beta = jnp.where(real, _sigmoid(r[0]), 0.0)
    g = jnp.where(real, -jnp.exp(p[0]) * _softplus(r[1] + p[1]), 0.0)
    return [beta, g]


def _st_gdn_out(r, n, p):
    return [_rms(r[0], p[0]) * _silu(r[1])]


def _st_mid(r, n, p):
    h1 = r[0] + _rms(r[1], p[0])
    return [h1, _rms(h1, p[1]), _rms(h1, p[2])]


def _st_latent(r, n, p):
    ckr, cq = r
    c_kv = _rms(ckr[:, :MLA_KV_RANK], p[0])
    k_rope = ckr[:, 128:256] * n[0] + ckr[:, 384:512] * n[1]
    return [c_kv, k_rope, _rms(cq, p[1])]


def _st_q_rope(r, n, p):
    return [(r[0] * n[0] + r[1] * n[1]) * (MLA_QK ** -0.5)]


def _st_gate(r, n, p):
    return [r[0] * _silu(r[1])]


def _make_st_loss(n_tokens):
    def st(r, n, p):
        h2 = r[0] + _rms(r[1], p[0])
        rows = _row_ids((r[0].shape[0], 1))
        real = jnp.logical_and(rows >= ROW0, rows < ROW0 + n_tokens)
        err = h2 - n[0]
        return [jnp.where(real, 0.5 * jnp.mean(err * err, axis=-1, keepdims=True), 0.0)]
    return st


CONV_BC = 1024


def _conv_fwd(x, w, *, col_blocks, name, tr=ROW_TILE):
    lp = x.shape[0]

    def body(x_ref, xp_ref, w_ref, o_ref):
        i = pl.program_id(0)
        prev = jnp.where(i > 0, xp_ref[...], 0.0)
        xc = jnp.concatenate([prev, x_ref[...]], axis=0)
        wv = w_ref[...]
        acc = wv[3:4, :] * x_ref[...]
        for j in range(3):
            acc = acc + wv[j:j + 1, :] * pltpu.roll(xc, 3 - j, 0)[8:, :]
        o_ref[...] = acc

    return pl.pallas_call(
        body, name=name, grid=(lp // tr, col_blocks),
        in_specs=[pl.BlockSpec((tr, CONV_BC), lambda i, j: (i, j)),
                  pl.BlockSpec((8, CONV_BC), lambda i, j: (jnp.maximum(i * (tr // 8) - 1, 0), j)),
                  pl.BlockSpec((4, CONV_BC), lambda i, j: (0, j))],
        out_specs=pl.BlockSpec((tr, CONV_BC), lambda i, j: (i, j)),
        out_shape=jax.ShapeDtypeStruct((lp, col_blocks * CONV_BC), f32),
    )(x, x, w)


def _conv_bwd(dc, x, w, *, x_off, w_off, name, tr=ROW_TILE):
    lp, width = dc.shape
    ncb, nrow = width // CONV_BC, lp // tr

    def body(dc_ref, dcn_ref, x_ref, xp_ref, w_ref, dx_ref, dw_ref):
        i = pl.program_id(1)
        nxt = jnp.where(i < nrow - 1, dcn_ref[...], 0.0)
        dcv = dc_ref[...]
        dcc = jnp.concatenate([dcv, nxt], axis=0)
        prev = jnp.where(i > 0, xp_ref[...], 0.0)
        xc = jnp.concatenate([prev, x_ref[...]], axis=0)
        wv = w_ref[...]
        dx = wv[3:4, :] * dcv
        dws = [None] * 4
        dws[3] = jnp.sum(dcv * x_ref[...], axis=0, keepdims=True)
        for j in range(3):
            dx = dx + wv[j:j + 1, :] * pltpu.roll(dcc, tr + 8 - (3 - j), 0)[:tr, :]
            dws[j] = jnp.sum(dcv * pltpu.roll(xc, 3 - j, 0)[8:, :], axis=0, keepdims=True)
        dx_ref[...] = dx.astype(dx_ref.dtype)

        @pl.when(i == 0)
        def _():
            for j in range(4):
                dw_ref[j:j + 1, :] = dws[j]

        @pl.when(i > 0)
        def _():
            for j in range(4):
                dw_ref[j:j + 1, :] += dws[j]

    last8 = lp // 8 - 1
    return pl.pallas_call(
        body, name=name, grid=(ncb, nrow),
        in_specs=[pl.BlockSpec((tr, CONV_BC), lambda j, i: (i, j)),
                  pl.BlockSpec((8, CONV_BC), lambda j, i: (jnp.minimum((i + 1) * (tr // 8), last8), j)),
                  pl.BlockSpec((tr, CONV_BC), lambda j, i: (i, j + x_off)),
                  pl.BlockSpec((8, CONV_BC), lambda j, i: (jnp.maximum(i * (tr // 8) - 1, 0), j + x_off)),
                  pl.BlockSpec((4, CONV_BC), lambda j, i: (0, j + w_off))],
        out_specs=[pl.BlockSpec((tr, CONV_BC), lambda j, i: (i, j)),
                   pl.BlockSpec((4, CONV_BC), lambda j, i: (0, j))],
        out_shape=[jax.ShapeDtypeStruct((lp, width), bf16), jax.ShapeDtypeStruct((4, width), f32)],
        compiler_params=pltpu.CompilerParams(dimension_semantics=("arbitrary", "arbitrary")),
    )(dc, dc, x, x, w)


GDN_PACK = 4
GDN_FWD_INTERLEAVE, GDN_BWD_INTERLEAVE = 4, 2


def _bd(x, cb):
    r = x.shape[0]
    tall = jnp.concatenate([x] * GDN_PACK, axis=0)
    rows = lax.broadcasted_iota(jnp.int32, tall.shape, 0) // r
    cols = lax.broadcasted_iota(jnp.int32, tall.shape, 1) // cb
    return jnp.where(rows == cols, tall, jnp.zeros_like(tall))


def _diag(full, r, cb):
    cols = lax.broadcasted_iota(jnp.int32, (r, full.shape[1]), 1) // cb
    out = jnp.where(cols == 0, full[0:r, :], 0.0)
    for a in range(1, GDN_PACK):
        out = out + jnp.where(cols == a, full[a * r:(a + 1) * r, :], 0.0)
    return out


def _stack(x, cb):
    return jnp.concatenate([x[:, a * cb:(a + 1) * cb] for a in range(GDN_PACK)], axis=0)


def _make_packed(dot):
    @jax.custom_vjp
    def pmm(x, y):
        return dot(x, _bd(y, y.shape[1] // GDN_PACK), NN)

    @jax.custom_vjp
    def pnt(x, y):
        k = x.shape[1] // GDN_PACK
        return _diag(dot(_stack(x, k), _stack(y, k), NT), x.shape[0], y.shape[0])

    @jax.custom_vjp
    def ptn(x, y):
        return _diag(dot(x, y, TN), x.shape[1] // GDN_PACK, y.shape[1] // GDN_PACK)

    def pmm_bwd(res, ct):
        x, y = res
        cb = y.shape[1] // GDN_PACK
        return dot(ct, _bd(y, cb), NT), _diag(dot(x, ct, TN), y.shape[0], cb)

    pmm.defvjp(lambda x, y: (pmm(x, y), (x, y)), pmm_bwd)
    pnt.defvjp(lambda x, y: (pnt(x, y), (x, y)), lambda res, ct: (pmm(ct, res[1]), ptn(ct, res[0])))
    ptn.defvjp(lambda x, y: (ptn(x, y), (x, y)), lambda res, ct: (pnt(res[1], ct), pmm(res[0], ct)))
    return pmm, pnt, ptn


_pmm, _pnt, _ptn = _make_packed(_dlo)


@jax.custom_vjp
def _inv_packed(ms):
    c = ms[0].shape[0]
    ii = lax.broadcasted_iota(jnp.int32, ms[0].shape, 0)
    jj = lax.broadcasted_iota(jnp.int32, ms[0].shape, 1) % c
    ts = [jnp.where(ii == jj, 1.0, 0.0) - m for m in ms]
    ps = [(-m).astype(bf16) for m in ms]
    for _ in range(int(math.log2(c)) - 1):
        ps = [_dlo(p, _bd(p, c), NN).astype(bf16) for p in ps]
        ts = [t + _dlo(t, _bd(p, c), NN) for t, p in zip(ts, ps)]
    return tuple(ts)


def _inv_packed_fwd(ms):
    ts = _inv_packed(ms)
    return ts, ts


def _inv_packed_bwd(ts, cts):
    c = ts[0].shape[0]
    ys = [_diag(_dlo(t, ct, TN), c, c) for t, ct in zip(ts, cts)]
    return (tuple(-_dlo(y, _bd(t.astype(bf16), c), NT) for y, t in zip(ys, ts)),)


_inv_packed.defvjp(_inv_packed_fwd, _inv_packed_bwd)


def _gdn_prep(q2, k2, v4, bcols, gcols, grows):
    c, d = v4.shape[0], GDN_D
    q4 = jnp.concatenate([q2[:, :d], q2[:, :d], q2[:, d:], q2[:, d:]], axis=1)
    k4 = jnp.concatenate([k2[:, :d], k2[:, :d], k2[:, d:], k2[:, d:]], axis=1)
    beta4 = jnp.concatenate([jnp.broadcast_to(b, (c, d)) for b in bcols], axis=1)
    gc4 = jnp.concatenate([jnp.broadcast_to(g, (c, d)) for g in gcols], axis=1)
    low = lax.broadcasted_iota(jnp.int32, (c, 128), 1) < c
    gi = jnp.concatenate([jnp.where(low, gcols[0], gcols[1]), jnp.where(low, gcols[2], gcols[3])], axis=1)
    gj = jnp.concatenate([jnp.where(low, grows[0], grows[1]), jnp.where(low, grows[2], grows[3])], axis=1)
    ii = lax.broadcasted_iota(jnp.int32, gi.shape, 0)
    jj = lax.broadcasted_iota(jnp.int32, gi.shape, 1) % c
    dec = jnp.exp(jnp.where(ii >= jj, gi - gj, NEG))
    rid = lax.broadcasted_iota(jnp.int32, gc4.shape, 0)
    glast = jnp.sum(jnp.where(rid == c - 1, gc4, 0.0), axis=0, keepdims=True)
    eg = jnp.exp(gc4)
    kb = k4 * beta4
    return dict(q=q4, k=k4, kb=kb, vb=v4 * beta4, kbe=kb * eg, qe=q4 * eg, dec=dec, dec_strict=jnp.where(ii > jj, dec, 0.0),
                sdecay=jnp.exp(glast), kd=k4 * jnp.exp(glast - gc4))


def _gdn_groups(groups):
    c = groups[0][3].shape[0]
    ss = [g[0] for g in groups]
    e = [_gdn_prep(*g[1:]) for g in groups]
    ts = _inv_packed(tuple(_pnt(x["kb"], x["k"]) * x["dec_strict"] for x in e))
    us = [_pmm(t, x["vb"]) for t, x in zip(ts, e)]
    ws = [_pmm(t, x["kbe"]) for t, x in zip(ts, e)]
    attns = [_pnt(x["q"], x["k"]) * x["dec"] for x in e]
    ws_qs = [_pmm(jnp.concatenate([w, x["qe"]], axis=0), s) for w, x, s in zip(ws, e, ss)]
    v_news = [u - y[:c] for u, y in zip(us, ws_qs)]
    os = [y[c:] + _pmm(a, vn) for y, a, vn in zip(ws_qs, attns, v_news)]
    s_news = [s * x["sdecay"] + _ptn(x["kd"], vn) for s, x, vn in zip(ss, e, v_news)]
    return list(zip(os, s_news))


def _lane_pick(x, h):
    lane = lax.broadcasted_iota(jnp.int32, x.shape, 1)
    return jnp.sum(jnp.where(lane == h, x, 0.0), axis=1, keepdims=True)


def _cum_log_decay(g):
    c = g.shape[0]
    lower = (lax.broadcasted_iota(jnp.int32, (c, c), 0) >= lax.broadcasted_iota(jnp.int32, (c, c), 1)).astype(f32)
    upper2 = (lax.broadcasted_iota(jnp.int32, (c, 128), 0) <= lax.broadcasted_iota(jnp.int32, (c, 128), 1) % c).astype(f32)
    return _dhi(lower, g, NN), _dhi(g, upper2, TN)


def _group_operands(gi, s_ref, q_ref, k_ref, v_ref, bv, gcv, gct_s):
    heads = [gi * GDN_PACK + u for u in range(GDN_PACK)]
    qk_off = pl.multiple_of(gi * 2 * GDN_D, 2 * GDN_D)
    v_off = pl.multiple_of(gi * GDN_PACK * GDN_D, GDN_PACK * GDN_D)
    return (s_ref[gi], q_ref[:, pl.ds(qk_off, 2 * GDN_D)], k_ref[:, pl.ds(qk_off, 2 * GDN_D)],
            v_ref[:, pl.ds(v_off, GDN_PACK * GDN_D)],
            [_lane_pick(bv, h) for h in heads], [_lane_pick(gcv, h) for h in heads],
            [gct_s[pl.ds(h, 1), :] for h in heads]), heads, qk_off, v_off


def _gdn_fwd(qn, kn, v, beta, g, *, n_real_chunks, name):
    lp = qn.shape[0]
    nchunk = lp // GDN_CHUNK
    C, D = GDN_CHUNK, GDN_D
    NG, SW = GDN_V_HEADS // GDN_PACK, GDN_PACK * GDN_D

    def body(q_ref, k_ref, v_ref, b_ref, g_ref, o_ref, st_ref, s_s, gc_s, gct_s):
        ci = pl.program_id(0)

        @pl.when(ci == 0)
        def _():
            s_s[...] = jnp.zeros_like(s_s)

        @pl.when(ci >= n_real_chunks)
        def _():
            o_ref[...] = jnp.zeros_like(o_ref)
            st_ref[...] = jnp.zeros_like(st_ref)

        @pl.when(ci < n_real_chunks)
        def _():
            gc, gct = _cum_log_decay(g_ref[...])
            gc_s[...] = gc
            gct_s[...] = gct

            def some_groups(it, carry):
                ids = [it * GDN_FWD_INTERLEAVE + u for u in range(GDN_FWD_INTERLEAVE)]
                ops = [_group_operands(gi, s_s, q_ref, k_ref, v_ref, b_ref[...], gc_s[...], gct_s) for gi in ids]
                res = _gdn_groups([op[0] for op in ops])
                for gi, op, (o, s_new) in zip(ids, ops, res):
                    st_ref[gi] = op[0][0]
                    s_s[gi] = s_new
                    o_ref[:, pl.ds(op[3], SW)] = o
                return carry

            lax.fori_loop(0, NG // GDN_FWD_INTERLEAVE, some_groups, 0)

    return pl.pallas_call(
        body, name=name, grid=(nchunk,),
        in_specs=[pl.BlockSpec((C, GDN_QK_W), lambda c: (c, 0)), pl.BlockSpec((C, GDN_QK_W), lambda c: (c, 0)),
                  pl.BlockSpec((C, GDN_V_W), lambda c: (c, 0)), pl.BlockSpec((C, 128), lambda c: (c, 0)),
                  pl.BlockSpec((C, 128), lambda c: (c, 0))],
        out_specs=[pl.BlockSpec((C, GDN_V_W), lambda c: (c, 0)),
                   pl.BlockSpec((None, NG, D, SW), lambda c: (c, 0, 0, 0))],
        out_shape=[jax.ShapeDtypeStruct((lp, GDN_V_W), f32), jax.ShapeDtypeStruct((nchunk, NG, D, SW), f32)],
        scratch_shapes=[pltpu.VMEM((NG, D, SW), f32), pltpu.VMEM((C, 128), f32), pltpu.VMEM((128, 128), f32)],
        compiler_params=pltpu.CompilerParams(dimension_semantics=("arbitrary",)),
    )(qn, kn, v, beta, g)


def _gdn_bwd(qn, kn, v, beta, g, states, do, *, n_real_chunks, name):
    lp = qn.shape[0]
    nchunk = lp // GDN_CHUNK
    C, D = GDN_CHUNK, GDN_D
    NG, SW = GDN_V_HEADS // GDN_PACK, GDN_PACK * GDN_D
    rev = lambda i: (nchunk - 1 - i, 0)

    def body(q_ref, k_ref, v_ref, b_ref, g_ref, st_ref, do_ref,
             dq_ref, dk_ref, dv_ref, db_ref, dg_ref, ds_s, gc_s, gct_s, dgc_s, dgct_s, dbeta_s):
        step = pl.program_id(0)
        ci = nchunk - 1 - step

        @pl.when(step == 0)
        def _():
            ds_s[...] = jnp.zeros_like(ds_s)

        @pl.when(ci >= n_real_chunks)
        def _():
            for r in (dq_ref, dk_ref, dv_ref, db_ref, dg_ref):
                r[...] = jnp.zeros_like(r)

        @pl.when(ci < n_real_chunks)
        def _():
            gc, gct = _cum_log_decay(g_ref[...])
            gc_s[...] = gc
            gct_s[...] = gct
            dgc_s[...] = jnp.zeros_like(dgc_s)
            dgct_s[...] = jnp.zeros_like(dgct_s)
            dbeta_s[...] = jnp.zeros_like(dbeta_s)

            def some_groups(it, carry):
                ids = [it * GDN_BWD_INTERLEAVE + u for u in range(GDN_BWD_INTERLEAVE)]
                ops = [_group_operands(gi, st_ref, q_ref, k_ref, v_ref, b_ref[...], gc_s[...], gct_s) for gi in ids]
                cts = [(do_ref[:, pl.ds(op[3], SW)], ds_s[gi]) for gi, op in zip(ids, ops)]
                _, vjp_fn = jax.vjp(_gdn_groups, [op[0] for op in ops])
                (grads,) = vjp_fn(cts)
                lane = lax.broadcasted_iota(jnp.int32, (C, 128), 1)
                dbeta_acc, dgc_acc = dbeta_s[...], dgc_s[...]
                for gi, (_, heads, qk_off, v_off), (dsp, dq2, dk2, dv4, dbcols, dgcols, dgrows) in zip(ids, ops, grads):
                    ds_s[gi] = dsp
                    dq_ref[:, pl.ds(qk_off, 2 * D)] = dq2
                    dk_ref[:, pl.ds(qk_off, 2 * D)] = dk2
                    dv_ref[:, pl.ds(v_off, SW)] = dv4
                    for h, dbcol, dgcol, dgrow in zip(heads, dbcols, dgcols, dgrows):
                        dbeta_acc = dbeta_acc + jnp.where(lane == h, dbcol, 0.0)
                        dgc_acc = dgc_acc + jnp.where(lane == h, dgcol, 0.0)
                        dgct_s[pl.ds(h, 1), :] = dgrow
                dbeta_s[...] = dbeta_acc
                dgc_s[...] = dgc_acc
                return carry

            lax.fori_loop(0, NG // GDN_BWD_INTERLEAVE, some_groups, 0)
            fold = (lax.broadcasted_iota(jnp.int32, (128, C), 0) % C == lax.broadcasted_iota(jnp.int32, (128, C), 1)).astype(f32)
            eye = (lax.broadcasted_iota(jnp.int32, (128, 128), 0) == lax.broadcasted_iota(jnp.int32, (128, 128), 1)).astype(f32)
            dgc = dgc_s[...] + _dhi(_dhi(dgct_s[...], fold, NN), eye, TN)
            upper = (lax.broadcasted_iota(jnp.int32, (C, C), 0) <= lax.broadcasted_iota(jnp.int32, (C, C), 1)).astype(f32)
            dg_ref[...] = _dhi(upper, dgc, NN)
            db_ref[...] = dbeta_s[...]

    return pl.pallas_call(
        body, name=name, grid=(nchunk,),
        in_specs=[pl.BlockSpec((C, GDN_QK_W), rev), pl.BlockSpec((C, GDN_QK_W), rev), pl.BlockSpec((C, GDN_V_W), rev),
                  pl.BlockSpec((C, 128), rev), pl.BlockSpec((C, 128), rev),
                  pl.BlockSpec((None, NG, D, SW), lambda i: (nchunk - 1 - i, 0, 0, 0)), pl.BlockSpec((C, GDN_V_W), rev)],
        out_specs=[pl.BlockSpec((C, GDN_QK_W), rev), pl.BlockSpec((C, GDN_QK_W), rev), pl.BlockSpec((C, GDN_V_W), rev),
                   pl.BlockSpec((C, 128), rev), pl.BlockSpec((C, 128), rev)],
        out_shape=[jax.ShapeDtypeStruct((lp, GDN_QK_W), f32)] * 2 + [jax.ShapeDtypeStruct((lp, GDN_V_W), f32)]
        + [jax.ShapeDtypeStruct((lp, 128), f32)] * 2,
        scratch_shapes=[pltpu.VMEM((NG, D, SW), f32), pltpu.VMEM((C, 128), f32), pltpu.VMEM((128, 128), f32),
                        pltpu.VMEM((C, 128), f32), pltpu.VMEM((128, 128), f32), pltpu.VMEM((C, 128), f32)],
        compiler_params=pltpu.CompilerParams(dimension_semantics=("arbitrary",)),
    )(qn, kn, v, beta, g, states, do)


def _att_mask_t(k0, q0, tk, tq):
    kcol = k0 + lax.broadcasted_iota(jnp.int32, (tk, tq), 0)
    qrow = q0 + lax.broadcasted_iota(jnp.int32, (tk, tq), 1)
    return jnp.logical_and(qrow >= kcol, kcol >= PAD_FRONT)


def _attention_fwd(q, kvu, kr, *, name, tk=ATT_TILE):
    lp = q.shape[0]
    H = MLA_HEADS
    tq = _pick(lp, (768, 512, 256))
    r = tq // tk

    def body(q_ref, kn_ref, kr_ref, v_ref, o_ref, lse_ref, m_s, l_s, acc_s, sa_s, sb_s):
        qi = pl.program_id(1)
        m_s[...] = jnp.full_like(m_s, NEG)
        l_s[...] = jnp.zeros_like(l_s)
        acc_s[...] = jnp.zeros_like(acc_s)

        def scores(ki):
            k0 = pl.multiple_of(ki * tk, tk)
            k = jnp.concatenate([kn_ref[pl.ds(k0, tk), :], kr_ref[pl.ds(k0, tk), :]], axis=1)
            return lax.dot_general(k, q_ref[...], (NT, ((), ())), preferred_element_type=f32)

        def consume(st, ki, masked):
            k0 = pl.multiple_of(ki * tk, tk)
            if masked:
                st = jnp.where(_att_mask_t(k0, qi * tq, tk, tq), st, NEG)
            m_prev = m_s[...]
            m_new = jnp.maximum(m_prev, jnp.max(st, axis=0, keepdims=True))
            alpha = jnp.exp(m_prev - m_new)
            p = jnp.exp(st - m_new)
            l_s[...] = alpha * l_s[...] + jnp.sum(p, axis=0, keepdims=True)
            acc_s[...] = alpha * acc_s[...] + lax.dot_general(v_ref[pl.ds(k0, tk), :], p.astype(bf16), (TN, ((), ())),
                                                              preferred_element_type=f32)
            m_s[...] = m_new

        n_full = qi * r

        def chain(blocks):
            bufs = (sa_s, sb_s)
            for j, (ki, masked) in enumerate(blocks):
                if j + 1 < len(blocks):
                    bufs[(j + 1) % 2][...] = scores(blocks[j + 1][0])
                consume(bufs[j % 2][...], ki, masked)

        diagonal = [(n_full + d, True) for d in range(r)]

        @pl.when(qi == 0)
        def _():
            sa_s[...] = scores(0)
            chain(diagonal)

        @pl.when(qi > 0)
        def _():
            sb_s[...] = scores(0)
            sa_s[...] = scores(1)
            consume(sb_s[...], 0, True)
            n_pairs = (n_full - 1) // 2

            def two(pi, carry):
                ki = 1 + 2 * pi
                sb_s[...] = scores(ki + 1)
                consume(sa_s[...], ki, False)
                sa_s[...] = scores(ki + 2)
                consume(sb_s[...], ki + 1, False)
                return carry

            lax.fori_loop(0, n_pairs, two, 0)
            nxt = 1 + 2 * n_pairs

            @pl.when(nxt < n_full)
            def _():
                chain([(nxt, False)] + diagonal)

            @pl.when(nxt == n_full)
            def _():
                chain(diagonal)
        o_ref[...] = jnp.transpose(acc_s[...] / l_s[...])
        lse_ref[...] = m_s[...] + jnp.log(l_s[...])

    return pl.pallas_call(
        body, name=name, grid=(H, lp // tq),
        in_specs=[pl.BlockSpec((tq, MLA_QKP), lambda h, qi: (qi, h)),
                  pl.BlockSpec((lp, 128), lambda h, qi: (0, h)),
                  pl.BlockSpec((lp, 128), lambda h, qi: (0, 0)),
                  pl.BlockSpec((lp, 128), lambda h, qi: (0, H + h))],
        out_specs=[pl.BlockSpec((tq, 128), lambda h, qi: (qi, h)),
                   pl.BlockSpec((None, 1, tq), lambda h, qi: (h, 0, qi))],
        out_shape=[jax.ShapeDtypeStruct((lp, MLA_V_W), f32), jax.ShapeDtypeStruct((H, 1, lp), f32)],
        scratch_shapes=[pltpu.VMEM((1, tq), f32), pltpu.VMEM((1, tq), f32), pltpu.VMEM((128, tq), f32),
                        pltpu.VMEM((tk, tq), f32), pltpu.VMEM((tk, tq), f32)],
        compiler_params=pltpu.CompilerParams(dimension_semantics=("arbitrary", "arbitrary")),
    )(q, kvu, kr, kvu)


def _attention_delta(o, do, *, name):
    lp = o.shape[0]
    H = MLA_HEADS
    tq = _pick(lp, (768, 512, 256))

    def body(o_ref, do_ref, d_ref):
        prod = o_ref[...] * do_ref[...].astype(f32)
        d_ref[...] = jnp.sum(jnp.transpose(prod), axis=0, keepdims=True)

    return pl.pallas_call(
        body, name=name, grid=(H, lp // tq),
        in_specs=[pl.BlockSpec((tq, 128), lambda h, qi: (qi, h)), pl.BlockSpec((tq, 128), lambda h, qi: (qi, h))],
        out_specs=pl.BlockSpec((None, 1, tq), lambda h, qi: (h, 0, qi)),
        out_shape=jax.ShapeDtypeStruct((H, 1, lp), f32),
    )(o, do)


def _attention_bwd(q, kvu, kr, lse, delta, do, *, name, t=ATT_TILE):
    lp = q.shape[0]
    H, nb = MLA_HEADS, lp // t
    tq = _pick(lp, (768, 512, 256))
    r, nq = tq // t, lp // tq

    def body(q_ref, kn_ref, kr_ref, v_ref, lse_ref, dl_ref, do_ref, dq_ref, dkn_ref, dv_ref, dkr_ref, dk_s, dv_s,
             sa_s, da_s, sb_s, db_s):
        ki = pl.program_id(1)
        k0 = ki * t
        k = jnp.concatenate([kn_ref[...], kr_ref[...]], axis=1)
        vv = v_ref[...]
        dk_s[...] = jnp.zeros_like(dk_s)
        dv_s[...] = jnp.zeros_like(dv_s)

        def products(qi, s_ref, d_ref):
            q0 = pl.multiple_of(qi * tq, tq)
            s_ref[...] = lax.dot_general(k, q_ref[pl.ds(q0, tq), :], (NT, ((), ())), preferred_element_type=f32)
            d_ref[...] = lax.dot_general(vv, do_ref[pl.ds(q0, tq), :], (NT, ((), ())), preferred_element_type=f32)

        def accumulate(s_ref, d_ref, qi, masked, first):
            q0 = pl.multiple_of(qi * tq, tq)
            qv = q_ref[pl.ds(q0, tq), :]
            dob = do_ref[pl.ds(q0, tq), :]
            st = s_ref[...]
            if masked:
                st = jnp.where(_att_mask_t(k0, q0, t, tq), st, NEG)
            p = jnp.exp(st - lse_ref[:, pl.ds(q0, tq)])
            dv_s[...] += jnp.dot(p.astype(bf16), dob, preferred_element_type=f32)
            ds = (p * (d_ref[...] - dl_ref[:, pl.ds(q0, tq)])).astype(bf16)
            dk_s[...] += jnp.dot(ds, qv, preferred_element_type=f32)
            dq = lax.dot_general(ds, k, (TN, ((), ())), preferred_element_type=f32)
            if first:
                dq_ref[pl.ds(q0, tq), :] = dq
            else:
                dq_ref[pl.ds(q0, tq), :] += dq

        def sweep(qd, mask_all, first):
            last = nq - 1
            products(qd, sa_s, da_s)
            products(jnp.minimum(qd + 1, last), sb_s, db_s)
            accumulate(sa_s, da_s, qd, True, first)
            n = last - qd

            def two(pi, carry):
                i = qd + 1 + 2 * pi
                products(i + 1, sa_s, da_s)
                accumulate(sb_s, db_s, i, mask_all, first)
                products(jnp.minimum(i + 2, last), sb_s, db_s)
                accumulate(sa_s, da_s, i + 1, mask_all, first)
                return carry

            lax.fori_loop(0, n // 2, two, 0)

            @pl.when(n % 2 == 1)
            def _():
                accumulate(sb_s, db_s, last, mask_all, first)

        @pl.when(ki == 0)
        def _():
            sweep(0, True, True)

        @pl.when(ki > 0)
        def _():
            sweep(ki // r, False, False)

        dkn_ref[...] = dk_s[:, :128].astype(dkn_ref.dtype)
        dkr_ref[...] = dk_s[:, 128:]
        dv_ref[...] = dv_s[...].astype(dv_ref.dtype)

    return pl.pallas_call(
        body, name=name, grid=(H, nb),
        in_specs=[pl.BlockSpec((lp, MLA_QKP), lambda h, ki: (0, h)),
                  pl.BlockSpec((t, 128), lambda h, ki: (ki, h)),
                  pl.BlockSpec((t, 128), lambda h, ki: (ki, 0)),
                  pl.BlockSpec((t, 128), lambda h, ki: (ki, H + h)),
                  pl.BlockSpec((None, 1, lp), lambda h, ki: (h, 0, 0)),
                  pl.BlockSpec((None, 1, lp), lambda h, ki: (h, 0, 0)),
                  pl.BlockSpec((lp, 128), lambda h, ki: (0, h))],
        out_specs=[pl.BlockSpec((lp, MLA_QKP), lambda h, ki: (0, h)),
                   pl.BlockSpec((t, 128), lambda h, ki: (ki, h)),
                   pl.BlockSpec((t, 128), lambda h, ki: (ki, h)),
                   pl.BlockSpec((t, 128), lambda h, ki: (ki, h))],
        out_shape=[jax.ShapeDtypeStruct((lp, H * MLA_QKP), f32), jax.ShapeDtypeStruct((lp, MLA_V_W), bf16),
                   jax.ShapeDtypeStruct((lp, MLA_V_W), bf16), jax.ShapeDtypeStruct((lp, MLA_V_W), f32)],
        scratch_shapes=[pltpu.VMEM((t, MLA_QKP), f32), pltpu.VMEM((t, 128), f32)] + [pltpu.VMEM((t, tq), f32)] * 4,
        compiler_params=pltpu.CompilerParams(dimension_semantics=("arbitrary", "arbitrary")),
    )(q, kvu, kr, kvu, lse, delta, do)


def _exchange(x, *, gather, name):
    blk = x.shape if gather else x.shape[1:]

    def body(x_ref, o_ref, send_sems, recv_sems, local_sem):
        mx, my, mc = lax.axis_index("x"), lax.axis_index("y"), lax.axis_index("c")
        me = 4 * mx + 2 * my + mc
        own = pltpu.make_async_copy(x_ref if gather else x_ref.at[me], o_ref.at[me], local_sem)
        own.start()
        sends, peers = [], []
        for k in range(1, N_DEV):
            px = 1 - mx if k & 4 else mx
            py = 1 - my if k & 2 else my
            pc = 1 - mc if k & 1 else mc
            peer = 4 * px + 2 * py + pc
            cp = pltpu.make_async_remote_copy(
                src_ref=x_ref if gather else x_ref.at[peer], dst_ref=o_ref.at[me],
                send_sem=send_sems.at[k - 1], recv_sem=recv_sems.at[k - 1],
                device_id=(px, py, pc), device_id_type=MESH_ID)
            cp.start()
            sends.append(cp)
            peers.append(peer)
        for k in range(1, N_DEV):
            pltpu.make_async_remote_copy(
                src_ref=o_ref.at[peers[k - 1]], dst_ref=o_ref.at[peers[k - 1]],
                send_sem=send_sems.at[k - 1], recv_sem=recv_sems.at[k - 1],
                device_id=(mx, my, mc), device_id_type=MESH_ID).wait_recv()
        for cp in sends:
            cp.wait_send()
        own.wait()

    return pl.pallas_call(
        body, name=name,
        in_specs=[pl.BlockSpec(memory_space=pltpu.HBM)], out_specs=pl.BlockSpec(memory_space=pltpu.HBM),
        out_shape=jax.ShapeDtypeStruct((N_DEV,) + tuple(blk), x.dtype),
        scratch_shapes=[pltpu.SemaphoreType.DMA((N_DEV - 1,)), pltpu.SemaphoreType.DMA((N_DEV - 1,)),
                        pltpu.SemaphoreType.DMA],
    )(x)


def _reduce_adamw(parts, w, m, v, *, name):
    r = w.shape[0]
    tr = _pick(r, (1280, 1024, 512, 256, 128, 64, 48, 32, 16, 8))

    def body(p_ref, w_ref, m_ref, v_ref, g_ref, d_ref, nm_ref, nv_ref):
        g = p_ref[0].astype(f32)
        for s in range(1, N_DEV):
            g = g + p_ref[s].astype(f32)
        mm = ADAM_B1 * m_ref[...] + (1.0 - ADAM_B1) * g
        vv = ADAM_B2 * v_ref[...] + (1.0 - ADAM_B2) * (g * g)
        m_hat = mm / (1.0 - ADAM_B1 ** ADAM_STEP)
        v_hat = vv / (1.0 - ADAM_B2 ** ADAM_STEP)
        g_ref[...] = g
        d_ref[...] = -ADAM_LR * (m_hat / (jnp.sqrt(v_hat) + ADAM_EPS) + ADAM_WD * w_ref[...])
        nm_ref[...] = mm
        nv_ref[...] = vv

    spec = pl.BlockSpec((tr, 128), lambda i: (i, 0))
    return pl.pallas_call(
        body, name=name, grid=(r // tr,),
        in_specs=[pl.BlockSpec((N_DEV, tr, 128), lambda i: (0, i, 0)), spec, spec, spec],
        out_specs=[spec] * 4, out_shape=[jax.ShapeDtypeStruct((r, 128), f32)] * 4,
    )(parts, w, m, v)


_SHARDED = ("gdn_w_in", "gdn_w_out", "kv_w_down", "kv_w_up", "mla_w_in", "mla_w_q_up", "mla_w_out", "meta_tokens", "gdn_conv_w")
_COL_SHARDED = {"gdn_w_in", "kv_w_up", "mla_w_in", "mla_w_q_up", "meta_tokens", "gdn_conv_w"}
_N_BF16 = 7
_REPLICATED = ("pre_norm", "post_norm", "gdn_a_log", "gdn_dt_bias", "gdn_out_norm", "kv_norm", "kv_latent_norm",
               "mla_q_latent_norm")


def _rows128(a):
    flat = a.reshape(-1)
    pad = (-flat.shape[0]) % 128
    if pad:
        flat = jnp.pad(flat, (0, pad))
    return flat.reshape(-1, 128)


def _pack(arrs, row_multiple):
    parts = [_rows128(a) for a in arrs]
    buf = jnp.concatenate(parts, axis=0)
    pad = (-buf.shape[0]) % row_multiple
    if pad:
        buf = jnp.pad(buf, ((0, pad), (0, 0)))
    return buf


def _unpack(buf, shapes):
    out, r = [], 0
    for shp in shapes:
        n = math.prod(shp)
        rows = -(-n // 128)
        out.append(buf[r:r + rows].reshape(-1)[:n].reshape(shp))
        r += rows
    return out


def _unshard(g, full_shape, col):
    if col:
        return jnp.transpose(g, (1, 0, 2)).reshape(full_shape)
    return g.reshape(full_shape)


def _to_shards(a, col):
    r, c = a.shape
    if col:
        return jnp.transpose(a.reshape(r, N_DEV, c // N_DEV), (1, 0, 2))
    return a.reshape(N_DEV, r // N_DEV, c)


def _pad_cols(a, width):
    return jnp.pad(a, ((0, 0), (0, width - a.shape[1])))


def _rope_tables(lp):
    inv = ROPE_THETA ** (-jnp.arange(0, MLA_ROPE, 2, dtype=f32) / MLA_ROPE)
    pos = (jnp.arange(lp, dtype=jnp.int32) - PAD_FRONT).astype(f32)
    ang = pos[:, None] * inv[None, :]
    cos, sin = jnp.cos(ang), jnp.sin(ang)
    z = jnp.zeros((lp, 64), f32)
    return jnp.concatenate([cos, cos, z], axis=1), jnp.concatenate([-sin, sin, z], axis=1)


def kernel(x, meta_tokens, pre_norm, post_norm, gdn_w_in, gdn_conv_w, gdn_a_log, gdn_dt_bias, gdn_out_norm, gdn_w_out, kv_norm, kv_w_down, kv_latent_norm, kv_w_up, mla_w_in, mla_q_latent_norm, mla_w_q_up, mla_w_out, loss_target, m_meta_tokens, m_pre_norm, m_post_norm, m_gdn_w_in, m_gdn_conv_w, m_gdn_a_log, m_gdn_dt_bias, m_gdn_out_norm, m_gdn_w_out, m_kv_norm, m_kv_w_down, m_kv_latent_norm, m_kv_w_up, m_mla_w_in, m_mla_q_latent_norm, m_mla_w_q_up, m_mla_w_out, v_meta_tokens, v_pre_norm, v_post_norm, v_gdn_w_in, v_gdn_conv_w, v_gdn_a_log, v_gdn_dt_bias, v_gdn_out_norm, v_gdn_w_out, v_kv_norm, v_kv_w_down, v_kv_latent_norm, v_kv_w_up, v_mla_w_in, v_mla_q_latent_norm, v_mla_w_q_up, v_mla_w_out):
    W = dict(meta_tokens=meta_tokens, pre_norm=pre_norm, post_norm=post_norm, gdn_w_in=gdn_w_in, gdn_conv_w=gdn_conv_w,
             gdn_a_log=gdn_a_log, gdn_dt_bias=gdn_dt_bias, gdn_out_norm=gdn_out_norm, gdn_w_out=gdn_w_out, kv_norm=kv_norm,
             kv_w_down=kv_w_down, kv_latent_norm=kv_latent_norm, kv_w_up=kv_w_up, mla_w_in=mla_w_in,
             mla_q_latent_norm=mla_q_latent_norm, mla_w_q_up=mla_w_q_up, mla_w_out=mla_w_out)
    M = dict(meta_tokens=m_meta_tokens, pre_norm=m_pre_norm, post_norm=m_post_norm, gdn_w_in=m_gdn_w_in, gdn_conv_w=m_gdn_conv_w,
             gdn_a_log=m_gdn_a_log, gdn_dt_bias=m_gdn_dt_bias, gdn_out_norm=m_gdn_out_norm, gdn_w_out=m_gdn_w_out, kv_norm=m_kv_norm,
             kv_w_down=m_kv_w_down, kv_latent_norm=m_kv_latent_norm, kv_w_up=m_kv_w_up, mla_w_in=m_mla_w_in,
             mla_q_latent_norm=m_mla_q_latent_norm, mla_w_q_up=m_mla_w_q_up, mla_w_out=m_mla_w_out)
    V = dict(meta_tokens=v_meta_tokens, pre_norm=v_pre_norm, post_norm=v_post_norm, gdn_w_in=v_gdn_w_in, gdn_conv_w=v_gdn_conv_w,
             gdn_a_log=v_gdn_a_log, gdn_dt_bias=v_gdn_dt_bias, gdn_out_norm=v_gdn_out_norm, gdn_w_out=v_gdn_w_out, kv_norm=v_kv_norm,
             kv_w_down=v_kv_w_down, kv_latent_norm=v_kv_latent_norm, kv_w_up=v_kv_w_up, mla_w_in=v_mla_w_in,
             mla_q_latent_norm=v_mla_q_latent_norm, mla_w_q_up=v_mla_w_q_up, mla_w_out=v_mla_w_out)
    order = list(W)

    n_tok = x.shape[1]
    assert n_tok % GDN_CHUNK == 0
    n_real = ROW0 + n_tok
    lp = -(-n_real // ROW_TILE) * ROW_TILE
    n_real_chunks = n_real // GDN_CHUNK

    shard2d = {n: W[n].reshape(W[n].shape[-2:]) for n in _SHARDED}
    full_shape = {n: ((s.shape[0], s.shape[1] * N_DEV) if n in _COL_SHARDED else (s.shape[0] * N_DEV, s.shape[1]))
                  for n, s in shard2d.items()}
    big, small = _SHARDED[:_N_BF16], _SHARDED[_N_BF16:]
    g_big = _exchange(_pack([shard2d[n].astype(bf16) for n in big], 16), gather=True, name="gather_weights")
    g_small = _exchange(_pack([shard2d[n] for n in small], 8), gather=True, name="gather_meta_conv")
    full = {}
    for names, buf in ((big, g_big), (small, g_small)):
        r = 0
        for n in names:
            shp = shard2d[n].shape
            rows = math.prod(shp) // 128
            blocks = buf[:, r:r + rows].reshape((N_DEV,) + shp)
            full[n] = _unshard(blocks, full_shape[n], n in _COL_SHARDED)
            r += rows

    w_in = full["gdn_w_in"]
    s1 = GDN_CONV_W + GDN_V_W
    w_in_p = jnp.concatenate([w_in[:, :s1], _pad_cols(w_in[:, s1:s1 + 16], 128), _pad_cols(w_in[:, s1 + 16:], 128)], axis=1)
    wd = full["kv_w_down"]
    zc = jnp.zeros((D_MODEL, 64), bf16)
    wd2 = jnp.concatenate([wd, zc, jnp.zeros((D_MODEL, 128), bf16), wd[:, 160:192], wd[:, 128:160], zc], axis=1)
    wup_p = jnp.transpose(full["kv_w_up"].reshape(MLA_KV_RANK, MLA_HEADS, 2, 128), (0, 2, 1, 3)).reshape(MLA_KV_RANK, 2 * MLA_V_W)
    wq = full["mla_w_q_up"].reshape(MLA_Q_RANK, MLA_HEADS, MLA_QK)
    zq64 = jnp.zeros((MLA_Q_RANK, MLA_HEADS, 64), bf16)
    wq_plain = jnp.concatenate([wq, zq64], axis=2).reshape(MLA_Q_RANK, MLA_HEADS * MLA_QKP)
    wq_swap = jnp.concatenate([jnp.zeros((MLA_Q_RANK, MLA_HEADS, 128), bf16), wq[:, :, 160:192], wq[:, :, 128:160], zq64],
                              axis=2).reshape(MLA_Q_RANK, MLA_HEADS * MLA_QKP)
    wq2 = jnp.concatenate([wq_plain, wq_swap], axis=1)
    w_mla_in, w_gdn_out, w_mla_out = full["mla_w_in"], full["gdn_w_out"], full["mla_w_out"]
    conv_w = full["gdn_conv_w"]
    pre0, pre1 = pre_norm[0:1], pre_norm[1:2]
    post0, post1 = post_norm[0:1], post_norm[1:2]
    alog_p, dtb_p = _pad_cols(gdn_a_log, 128), _pad_cols(gdn_dt_bias, 128)
    kvn, kvln = kv_norm.reshape(1, -1), kv_latent_norm.reshape(1, -1)

    h0 = jnp.concatenate([jnp.zeros((PAD_FRONT, D_MODEL), f32), full["meta_tokens"], x[0],
                          jnp.zeros((lp - n_real, D_MODEL), f32)], axis=0)
    tgt = jnp.concatenate([jnp.zeros((ROW0, D_MODEL), f32), loss_target[0], jnp.zeros((lp - n_real, D_MODEL), f32)], axis=0)
    cos_k, sin_k = _rope_tables(lp)
    one = jnp.ones((lp, 128), f32)
    cos_q = jnp.concatenate([one, cos_k], axis=1)
    sin_q = jnp.concatenate([jnp.zeros((lp, 128), f32), sin_k], axis=1)

    (hn0,) = _rowwise(_st_prenorm, [(h0, None, 0)], [], [pre0], [(D_MODEL, bf16, None)], name="f_prenorm0")
    proj = _matmul(hn0, w_in_p, name="f_gdn_in")
    conv = _conv_fwd(proj, conv_w, col_blocks=GDN_CONV_W // CONV_BC, name="f_conv")
    (qn,) = _rowwise(_st_gdn_q, [(conv, GDN_QK_W, 0)], [], [], [(GDN_QK_W, f32, GDN_QK_W)], heads=GDN_QK_HEADS, name="f_gdn_q")
    (kn,) = _rowwise(_st_gdn_k, [(conv, GDN_QK_W, 1)], [], [], [(GDN_QK_W, f32, GDN_QK_W)], heads=GDN_QK_HEADS, name="f_gdn_k")
    (vv,) = _rowwise(_st_gdn_v, [(conv, GDN_V_W, 1)], [], [], [(GDN_V_W, f32, GDN_V_W)], name="f_gdn_v")
    gate_rows = [(proj, 128, s1 // 128), (proj, 128, s1 // 128 + 1)]
    beta, gdec = _rowwise(_st_gdn_gate, gate_rows, [], [alog_p, dtb_p], [(128, f32, None)] * 2, name="f_gdn_gate")
    o_gdn, states = _gdn_fwd(qn, kn, vv, beta, gdec, n_real_chunks=n_real_chunks, name="f_gdn")
    out_rows = [(o_gdn, GDN_V_W, 0), (proj, GDN_V_W, GDN_CONV_W // GDN_V_W)]
    (og,) = _rowwise(_st_gdn_out, out_rows, [], [gdn_out_norm], [(GDN_V_W, bf16, GDN_V_W)], heads=GDN_V_HEADS, name="f_gdn_out")
    y0 = _matmul(og, w_gdn_out, name="f_gdn_wout")
    mid_rows = [(h0, None, 0), (y0, None, 0)]
    h1, hn1, hkv = _rowwise(_st_mid, mid_rows, [], [post0, pre1, kvn],
                            [(D_MODEL, f32, None), (D_MODEL, bf16, None), (D_MODEL, bf16, None)], name="f_mid")
    ckr = _matmul(hkv, wd2, name="f_kv_down")
    proj2 = _matmul(hn1, w_mla_in, name="f_mla_in")
    lat_rows = [(ckr, None, 0), (proj2, MLA_Q_RANK, 0)]
    lat_nd = [(cos_k, None, 0), (sin_k, None, 0)]
    c_kv, k_rope, c_q = _rowwise(_st_latent, lat_rows, lat_nd, [kvln, mla_q_latent_norm],
                                 [(128, bf16, None), (128, bf16, None), (MLA_Q_RANK, bf16, None)], name="f_latent")
    kvu = _matmul(c_kv, wup_p, out_dtype=bf16, name="f_kv_up")
    qq = _matmul(c_q, wq2, name="f_q_up")
    q_half = MLA_HEADS * MLA_QKP // 2
    q_rows = [(qq, q_half, 0), (qq, q_half, 2)]
    q_nd = [(cos_q, None, 0), (sin_q, None, 0)]
    (q_att,) = _rowwise(_st_q_rope, q_rows, q_nd, [], [(2 * q_half, bf16, q_half)], ncol=2, heads=MLA_HEADS // 2, name="f_q_rope")
    o_att, lse = _attention_fwd(q_att, kvu, k_rope, name="f_attention")
    gate2_rows = [(o_att, MLA_Q_RANK, 0), (proj2, MLA_Q_RANK, 1)]
    n_gate = MLA_V_W // MLA_Q_RANK
    (og2,) = _rowwise(_st_gate, gate2_rows, [], [], [(MLA_V_W, bf16, MLA_Q_RANK)], ncol=n_gate, name="f_mla_gate")
    y1 = _matmul(og2, w_mla_out, name="f_mla_wout")
    st_loss = _make_st_loss(n_tok)
    loss_rows_in = [(h1, None, 0), (y1, None, 0)]
    (loss_rows,) = _rowwise(st_loss, loss_rows_in, [(tgt, None, 0)], [post1], [(1, f32, None)], name="f_loss")
    loss = lax.psum(jnp.sum(loss_rows), ("x", "y", "c"))

    ones_ct = jnp.ones((lp, 1), f32)
    (dh1_a, dy1), (dpost1,) = _rowwise_vjp(st_loss, loss_rows_in, [(tgt, None, 0)], [post1], [(ones_ct, None, 0)], name="b_loss")
    dog2 = _matmul(dy1, w_mla_out, tb=True, name="b_mla_wout_x")
    dw_mla_out = _matmul(og2, dy1, ta=True, name="b_mla_wout_w")
    (do_att, dz2), _ = _rowwise_vjp(_st_gate, gate2_rows, [], [], [(dog2, MLA_Q_RANK, 0)], ncol=n_gate, name="b_mla_gate",
                                    grad_dtypes=[bf16, bf16])
    delta = _attention_delta(o_att, do_att, name="b_attention_delta")
    dq_att, dkn, dvv, dkr_h = _attention_bwd(q_att, kvu, k_rope, lse, delta, do_att, name="b_attention")
    (dqa, dqb), _ = _rowwise_vjp(_st_q_rope, q_rows, q_nd, [], [(dq_att, q_half, 0)], ncol=2, heads=MLA_HEADS // 2, name="b_q_rope",
                                 grad_dtypes=[bf16, bf16])
    dqq = jnp.concatenate([dqa, dqb], axis=1)
    dc_q = _matmul(dqq, wq2, tb=True, name="b_q_up_x")
    dwq2 = _matmul(c_q, dqq, ta=True, name="b_q_up_w")
    dkvu = jnp.concatenate([dkn, dvv], axis=1)
    dc_kv = _matmul(dkvu, wup_p, tb=True, name="b_kv_up_x")
    dwup_p = _matmul(c_kv, dkvu, ta=True, name="b_kv_up_w")

    def lat_ct(cv):
        dkr = cv[1][:, 0:128]
        for h in range(1, MLA_HEADS):
            dkr = dkr + cv[1][:, h * 128:(h + 1) * 128]
        return [cv[0], dkr, cv[2]]

    (dckr, dcq_pre), (dkvln, dqln) = _rowwise_vjp(
        _st_latent, lat_rows, lat_nd, [kvln, mla_q_latent_norm],
        [(dc_kv, None, 0), (dkr_h, None, 0), (dc_q, None, 0)], ct_pre=lat_ct, name="b_latent", grad_dtypes=[bf16, bf16])
    dproj2 = jnp.concatenate([dcq_pre, dz2], axis=1)
    dhn1 = _matmul(dproj2, w_mla_in, tb=True, name="b_mla_in_x")
    dw_mla_in = _matmul(hn1, dproj2, ta=True, name="b_mla_in_w")
    dhkv = _matmul(dckr, wd2, tb=True, name="b_kv_down_x")
    dwd2 = _matmul(hkv, dckr, ta=True, name="b_kv_down_w")
    (dh0_a, dy0), (dpost0, dpre1, dkvn) = _rowwise_vjp(
        _st_mid, mid_rows, [], [post0, pre1, kvn], [(dh1_a, None, 0), (dhn1, None, 0), (dhkv, None, 0)], name="b_mid")
    dog = _matmul(dy0, w_gdn_out, tb=True, name="b_gdn_wout_x")
    dw_gdn_out = _matmul(og, dy0, ta=True, name="b_gdn_wout_w")
    (do_gdn, dz), (dout_norm,) = _rowwise_vjp(_st_gdn_out, out_rows, [], [gdn_out_norm], [(dog, GDN_V_W, 0)], heads=GDN_V_HEADS,
                                              name="b_gdn_out", grad_dtypes=[f32, bf16])
    dq_g, dk_g, dv_g, dbeta, dgdec = _gdn_bwd(qn, kn, vv, beta, gdec, states, do_gdn, n_real_chunks=n_real_chunks, name="b_gdn")
    (db_col, da_col), (dalog_p, ddtb_p) = _rowwise_vjp(
        _st_gdn_gate, gate_rows, [], [alog_p, dtb_p], [(dbeta, None, 0), (dgdec, None, 0)], name="b_gdn_gate",
        grad_dtypes=[bf16, bf16])
    (dconv_q,), _ = _rowwise_vjp(_st_gdn_q, [(conv, GDN_QK_W, 0)], [], [], [(dq_g, GDN_QK_W, 0)], heads=GDN_QK_HEADS, name="b_gdn_q")
    (dconv_k,), _ = _rowwise_vjp(_st_gdn_k, [(conv, GDN_QK_W, 1)], [], [], [(dk_g, GDN_QK_W, 0)], heads=GDN_QK_HEADS, name="b_gdn_k")
    (dconv_v,), _ = _rowwise_vjp(_st_gdn_v, [(conv, GDN_V_W, 1)], [], [], [(dv_g, GDN_V_W, 0)], name="b_gdn_v")
    nq_b = GDN_QK_W // CONV_BC
    dpre_q, dcw_q = _conv_bwd(dconv_q, proj, conv_w, x_off=0, w_off=0, name="b_conv_q")
    dpre_k, dcw_k = _conv_bwd(dconv_k, proj, conv_w, x_off=nq_b, w_off=nq_b, name="b_conv_k")
    dpre_v, dcw_v = _conv_bwd(dconv_v, proj, conv_w, x_off=2 * nq_b, w_off=2 * nq_b, name="b_conv_v")
    dproj = jnp.concatenate([dpre_q, dpre_k, dpre_v, dz, db_col, da_col], axis=1)
    dhn0 = _matmul(dproj, w_in_p, tb=True, name="b_gdn_in_x")
    dw_in_p = _matmul(hn0, dproj, ta=True, name="b_gdn_in_w")
    (dh0,), (dpre0,) = _rowwise_vjp(_st_prenorm, [(h0, None, 0)], [], [pre0], [(dhn0, None, 0)], extra=[dh0_a], name="b_prenorm0")

    grad_x = dh0[ROW0:n_real][None]
    G = {}
    G["meta_tokens"] = dh0[PAD_FRONT:ROW0]
    G["gdn_w_in"] = jnp.concatenate([dw_in_p[:, :s1 + 16], dw_in_p[:, s1 + 128:s1 + 144]], axis=1)
    G["gdn_conv_w"] = jnp.concatenate([dcw_q, dcw_k, dcw_v], axis=1)
    G["gdn_w_out"] = dw_gdn_out
    G["kv_w_down"] = jnp.concatenate([dwd2[:, :128], dwd2[:, 128:160] + dwd2[:, 416:448], dwd2[:, 160:192] + dwd2[:, 384:416]], axis=1)
    G["kv_w_up"] = jnp.transpose(dwup_p.reshape(MLA_KV_RANK, 2, MLA_HEADS, 128), (0, 2, 1, 3)).reshape(MLA_KV_RANK, 2 * MLA_V_W)
    G["mla_w_in"] = dw_mla_in
    dqp = dwq2[:, :MLA_HEADS * MLA_QKP].reshape(MLA_Q_RANK, MLA_HEADS, MLA_QKP)
    dqs = dwq2[:, MLA_HEADS * MLA_QKP:].reshape(MLA_Q_RANK, MLA_HEADS, MLA_QKP)
    G["mla_w_q_up"] = jnp.concatenate([dqp[:, :, :128], dqp[:, :, 128:160] + dqs[:, :, 160:192],
                                       dqp[:, :, 160:192] + dqs[:, :, 128:160]], axis=2).reshape(MLA_Q_RANK, MLA_HEADS * MLA_QK)
    G["mla_w_out"] = dw_mla_out
    G["pre_norm"] = jnp.concatenate([dpre0, dpre1], axis=0)
    G["post_norm"] = jnp.concatenate([dpost0, dpost1], axis=0)
    G["gdn_a_log"] = dalog_p[:, :GDN_V_HEADS]
    G["gdn_dt_bias"] = ddtb_p[:, :GDN_V_HEADS]
    G["gdn_out_norm"] = dout_norm
    G["kv_norm"] = dkvn.reshape(-1)
    G["kv_latent_norm"] = dkvln.reshape(-1)
    G["mla_q_latent_norm"] = dqln

    send = jnp.concatenate([_to_shards(G[n], n in _COL_SHARDED).reshape(N_DEV, -1, 128).astype(bf16) for n in _SHARDED], axis=1)
    parts = _exchange(send, gather=False, name="scatter_grads")
    w_s, m_s, v_s = (_pack([d[n] for n in _SHARDED], 8) for d in (W, M, V))
    res_s = _reduce_adamw(parts, w_s, m_s, v_s, name="adamw_sharded")
    small_send = _pack([G[n] for n in _REPLICATED], 8)
    parts_r = _exchange(small_send, gather=True, name="gather_small_grads")
    w_r, m_r, v_r = (_pack([d[n] for n in _REPLICATED], 8) for d in (W, M, V))
    res_r = _reduce_adamw(parts_r, w_r, m_r, v_r, name="adamw_replicated")

    outs = {}
    for kind, bs, br in zip(("grad", "delta", "new_m", "new_v"), res_s, res_r):
        for n, a in zip(_SHARDED, _unpack(bs, [W[n].shape for n in _SHARDED])):
            outs[kind, n] = a
        for n, a in zip(_REPLICATED, _unpack(br, [W[n].shape for n in _REPLICATED])):
            outs[kind, n] = a
    return (loss, grad_x, *[outs[k, n] for k in ("grad", "delta", "new_m", "new_v") for n in order])
```

```python
import functools
import math

import jax
import jax.numpy as jnp
from jax import lax
from jax.experimental import pallas as pl
from jax.experimental.pallas import tpu as pltpu

f32, bf16 = jnp.float32, jnp.bfloat16
HIGHEST = lax.Precision.HIGHEST
MESH_ID = pl.DeviceIdType.MESH

N_DEV = 8
D_MODEL = 1024
N_META = 16
NORM_EPS = 1e-6
PAD_FRONT = 48
ROW0 = PAD_FRONT + N_META
GDN_QK_HEADS, GDN_V_HEADS, GDN_D = 8, 16, 128
GDN_CHUNK = 64
GDN_QK_W, GDN_V_W = GDN_QK_HEADS * GDN_D, GDN_V_HEADS * GDN_D
GDN_CONV_W = 2 * GDN_QK_W + GDN_V_W
GDN_IN_W = GDN_CONV_W + GDN_V_W + 2 * GDN_V_HEADS
GDN_IN_WP = GDN_CONV_W + GDN_V_W + 2 * 128
MLA_HEADS, MLA_NOPE, MLA_ROPE, MLA_V = 16, 128, 64, 128
MLA_Q_RANK, MLA_KV_RANK = 256, 128
MLA_QK = MLA_NOPE + MLA_ROPE
MLA_QKP = 256
MLA_V_W = MLA_HEADS * MLA_V
ROPE_THETA = 10000.0
NEG = -1e30
ROW_TILE = 256
ATT_TILE = 256

ADAM_LR, ADAM_B1, ADAM_B2, ADAM_EPS, ADAM_WD, ADAM_STEP = 0.001, 0.9, 0.999, 1e-08, 0.01, 10

NN = ((1,), (0,))
NT = ((1,), (1,))
TN = ((0,), (0,))


def _pick(dim, prefs):
    for p in prefs:
        if dim % p == 0:
            return p
    return dim


def _dlo(a, b, dims):
    return lax.dot_general(a.astype(bf16), b.astype(bf16), (dims, ((), ())), preferred_element_type=f32)


def _dhi(a, b, dims):
    return lax.dot_general(a, b, (dims, ((), ())), precision=HIGHEST, preferred_element_type=f32)


def _matmul(a, b, *, ta=False, tb=False, out_dtype=f32, name):
    assert not (ta and tb)
    if ta:
        kdim, m = a.shape
    else:
        m, kdim = a.shape
    n = b.shape[0] if tb else b.shape[1]
    assert (b.shape[1] if tb else b.shape[0]) == kdim
    tm = _pick(m, (1024, 768, 512, 384, 256, 128))
    tn = _pick(n, (1024, 768, 640, 512, 256, 128))
    tk = _pick(kdim, (1024, 768, 640, 512, 256, 128))
    nk = kdim // tk
    dims = TN if ta else (NT if tb else NN)

    def body(a_ref, b_ref, o_ref, acc_ref):
        k = pl.program_id(2)

        @pl.when(k == 0)
        def _():
            acc_ref[...] = jnp.zeros_like(acc_ref)

        acc_ref[...] += lax.dot_general(a_ref[...].astype(bf16), b_ref[...].astype(bf16), (dims, ((), ())),
                                        preferred_element_type=f32)

        @pl.when(k == nk - 1)
        def _():
            o_ref[...] = acc_ref[...].astype(o_ref.dtype)

    a_spec = pl.BlockSpec((tk, tm), lambda i, j, k: (k, i)) if ta else pl.BlockSpec((tm, tk), lambda i, j, k: (i, k))
    b_spec = pl.BlockSpec((tn, tk), lambda i, j, k: (j, k)) if tb else pl.BlockSpec((tk, tn), lambda i, j, k: (k, j))
    return pl.pallas_call(
        body, name=name, grid=(m // tm, n // tn, nk),
        in_specs=[a_spec, b_spec], out_specs=pl.BlockSpec((tm, tn), lambda i, j, k: (i, j)),
        out_shape=jax.ShapeDtypeStruct((m, n), out_dtype),
        scratch_shapes=[pltpu.VMEM((tm, tn), f32)],
        compiler_params=pltpu.CompilerParams(dimension_semantics=("parallel", "parallel", "arbitrary")),
    )(a, b)


def _row_spec(item, tr):
    a, bc, off = item
    if bc is None:
        return pl.BlockSpec((tr, a.shape[1]), lambda i, j: (i, 0))
    return pl.BlockSpec((tr, bc), lambda i, j, off=off: (i, j + off))


def _param_spec(p):
    return pl.BlockSpec(p.shape, lambda i, j: (0, 0))


def _row_tile(lp, items):
    widest = max(a.shape[1] if bc is None else bc for (a, bc, _) in items)
    return ROW_TILE if widest >= 1024 else _pick(lp, (768, 512, 256))


def _head_cols(tiles, h, heads):
    return [x[:, h * (x.shape[1] // heads):(h + 1) * (x.shape[1] // heads)] for x in tiles]


def _rowwise(fn, rows, nodiff, params, outs, *, ncol=1, heads=1, name):
    lp = rows[0][0].shape[0]
    tr = _row_tile(lp, rows)
    nr, nd = len(rows), len(nodiff)

    def body(*refs):
        rv = [r[...].astype(f32) for r in refs[:nr]]
        nv = [r[...] for r in refs[nr:nr + nd]]
        pv = [r[...] for r in refs[nr + nd:nr + nd + len(params)]]
        per_head = [fn(_head_cols(rv, h, heads), nv, pv) for h in range(heads)]
        res = [jnp.concatenate(list(vals), axis=1) if heads > 1 else vals[0] for vals in zip(*per_head)]
        for ref, val in zip(refs[nr + nd + len(params):], res):
            ref[...] = val.astype(ref.dtype)

    out_specs = [pl.BlockSpec((tr, c if bc is None else bc), (lambda i, j: (i, 0)) if bc is None else (lambda i, j: (i, j)))
                 for (c, _, bc) in outs]
    return pl.pallas_call(
        body, name=name, grid=(lp // tr, ncol),
        in_specs=[_row_spec(it, tr) for it in rows + nodiff] + [_param_spec(p) for p in params],
        out_specs=out_specs,
        out_shape=[jax.ShapeDtypeStruct((lp, c), dt) for (c, dt, _) in outs],
    )(*[it[0] for it in rows + nodiff], *params)


def _rowwise_vjp(fn, rows, nodiff, params, cts, *, ncol=1, heads=1, name, ct_pre=None, extra=None, grad_dtypes=None):
    lp = rows[0][0].shape[0]
    tr = _row_tile(lp, rows)
    nr, nd, npar, nct = len(rows), len(nodiff), len(params), len(cts)
    extra = extra or [None] * nr
    grad_dtypes = grad_dtypes or [f32] * nr
    ex_items = [(e, rows[k][1], 0) for k, e in enumerate(extra) if e is not None]
    ex_pos = [k for k, e in enumerate(extra) if e is not None]
    for (a, bc, _) in rows:
        assert bc is not None or ncol == 1

    def body(*refs):
        pos = 0
        rv = [r[...].astype(f32) for r in refs[pos:pos + nr]]; pos += nr
        nv = [r[...] for r in refs[pos:pos + nd]]; pos += nd
        pv = [r[...] for r in refs[pos:pos + npar]]; pos += npar
        cv = [r[...].astype(f32) for r in refs[pos:pos + nct]]; pos += nct
        ev = [r[...].astype(f32) for r in refs[pos:pos + len(ex_items)]]; pos += len(ex_items)
        drow_refs = refs[pos:pos + nr]; pos += nr
        dpar_refs = refs[pos:pos + npar]
        ctv = ct_pre(cv) if ct_pre is not None else cv
        drow_h, dpar = [], None
        for h in range(heads):
            outs, vjp_fn = jax.vjp(lambda rr, pp: fn(rr, nv, pp), _head_cols(rv, h, heads), pv)
            dr, dp = vjp_fn([c.astype(o.dtype) for c, o in zip(_head_cols(ctv, h, heads), outs)])
            drow_h.append(dr)
            dpar = dp if dpar is None else [a + b for a, b in zip(dpar, dp)]
        drow = [jnp.concatenate(list(vals), axis=1) if heads > 1 else vals[0] for vals in zip(*drow_h)]
        for k, e in zip(ex_pos, ev):
            drow[k] = drow[k] + e
        for ref, val in zip(drow_refs, drow):
            ref[...] = val.astype(ref.dtype)
        first = jnp.logical_and(pl.program_id(0) == 0, pl.program_id(1) == 0)

        @pl.when(first)
        def _():
            for ref, val in zip(dpar_refs, dpar):
                ref[...] = val

        @pl.when(jnp.logical_not(first))
        def _():
            for ref, val in zip(dpar_refs, dpar):
                ref[...] += val

    drow_shapes, drow_specs = [], []
    for (a, bc, _), dt in zip(rows, grad_dtypes):
        if bc is None:
            drow_shapes.append(jax.ShapeDtypeStruct((lp, a.shape[1]), dt))
            drow_specs.append(pl.BlockSpec((tr, a.shape[1]), lambda i, j: (i, 0)))
        else:
            drow_shapes.append(jax.ShapeDtypeStruct((lp, ncol * bc), dt))
            drow_specs.append(pl.BlockSpec((tr, bc), lambda i, j: (i, j)))
    res = pl.pallas_call(
        body, name=name, grid=(lp // tr, ncol),
        in_specs=[_row_spec(it, tr) for it in rows + nodiff] + [_param_spec(p) for p in params]
        + [_row_spec(it, tr) for it in cts + ex_items],
        out_specs=drow_specs + [_param_spec(p) for p in params],
        out_shape=drow_shapes + [jax.ShapeDtypeStruct(p.shape, f32) for p in params],
        compiler_params=pltpu.CompilerParams(dimension_semantics=("arbitrary", "arbitrary")),
    )(*[it[0] for it in rows + nodiff], *params, *[it[0] for it in cts + ex_items])
    return res[:nr], res[nr:]


def _rms(x, g):
    return x * lax.rsqrt(jnp.mean(x * x, axis=-1, keepdims=True) + NORM_EPS) * g


def _l2n(x):
    return x * lax.rsqrt(jnp.sum(x * x, axis=-1, keepdims=True) + NORM_EPS)


def _sigmoid(x):
    return 1.0 / (1.0 + jnp.exp(-x))


def _silu(x):
    return x * _sigmoid(x)


def _softplus(x):
    return jnp.maximum(x, 0.0) + jnp.log(1.0 + jnp.exp(-jnp.abs(x)))


def _row_ids(shape):
    return pl.program_id(0) * shape[0] + lax.broadcasted_iota(jnp.int32, shape, 0)


def _st_prenorm(r, n, p):
    return [_rms(r[0], p[0])]


def _st_gdn_q(r, n, p):
    return [_l2n(_silu(r[0])) * (GDN_D ** -0.5)]


def _st_gdn_k(r, n, p):
    return [_l2n(_silu(r[0]))]


def _st_gdn_v(r, n, p):
    return [_silu(r[0])]


def _st_gdn_gate(r, n, p):
    real = _row_ids(r[0].shape) >= PAD_FRONT
    beta = jnp.where(real, _sigmoid(r[0]), 0.0)
    g = jnp.where(real, -jnp.exp(p[0]) * _softplus(r[1] + p[1]), 0.0)
    return [beta, g]


def _st_gdn_out(r, n, p):
    return [_rms(r[0], p[0]) * _silu(r[1])]


def _st_mid(r, n, p):
    h1 = r[0] + _rms(r[1], p[0])
    return [h1, _rms(h1, p[1]), _rms(h1, p[2])]


def _st_latent(r, n, p):
    ckr, cq = r
    c_kv = _rms(ckr[:, :MLA_KV_RANK], p[0])
    k_rope = ckr[:, 128:256] * n[0] + ckr[:, 384:512] * n[1]
    return [c_kv, k_rope, _rms(cq, p[1])]


Q_GROUP = 8


def _st_q_rope(r, n, p):
    half = Q_GROUP * MLA_QKP
    out = []
    for h in range(Q_GROUP):
        cols = slice(h * MLA_QKP, (h + 1) * MLA_QKP)
        out.append((r[0][:, :half][:, cols] * n[0] + r[0][:, half:][:, cols] * n[1]) * (MLA_QK ** -0.5))
    return [jnp.concatenate(out, axis=1)]


def _st_q_rope_t(r, n, p):
    plain, swapped = [], []
    for h in range(Q_GROUP):
        ct = r[0][:, h * MLA_QKP:(h + 1) * MLA_QKP] * (MLA_QK ** -0.5)
        plain.append(ct * n[0])
        swapped.append(ct * n[1])
    return [jnp.concatenate(plain + swapped, axis=1)]


def _st_gate(r, n, p):
    return [r[0] * _silu(r[1])]


def _make_st_loss(n_tokens):
    def st(r, n, p):
        h2 = r[0] + _rms(r[1], p[0])
        rows = _row_ids((r[0].shape[0], 1))
        real = jnp.logical_and(rows >= ROW0, rows < ROW0 + n_tokens)
        err = h2 - n[0]
        return [jnp.where(real, 0.5 * jnp.mean(err * err, axis=-1, keepdims=True), 0.0)]
    return st


CONV_BC = 1024


def _conv_fwd(x, w, *, col_blocks, name, tr=ROW_TILE):
    lp = x.shape[0]

    def body(x_ref, xp_ref, w_ref, o_ref):
        i = pl.program_id(0)
        prev = jnp.where(i > 0, xp_ref[...], 0.0)
        xc = jnp.concatenate([prev, x_ref[...]], axis=0)
        wv = w_ref[...]
        acc = wv[3:4, :] * x_ref[...]
        for j in range(3):
            acc = acc + wv[j:j + 1, :] * pltpu.roll(xc, 3 - j, 0)[8:, :]
        o_ref[...] = acc

    return pl.pallas_call(
        body, name=name, grid=(lp // tr, col_blocks),
        in_specs=[pl.BlockSpec((tr, CONV_BC), lambda i, j: (i, j)),
                  pl.BlockSpec((8, CONV_BC), lambda i, j: (jnp.maximum(i * (tr // 8) - 1, 0), j)),
                  pl.BlockSpec((4, CONV_BC), lambda i, j: (0, j))],
        out_specs=pl.BlockSpec((tr, CONV_BC), lambda i, j: (i, j)),
        out_shape=jax.ShapeDtypeStruct((lp, col_blocks * CONV_BC), f32),
    )(x, x, w)


def _conv_bwd(dc, x, w, *, x_off, w_off, name, tr=ROW_TILE):
    lp, width = dc.shape
    ncb, nrow = width // CONV_BC, lp // tr

    def body(dc_ref, dcn_ref, x_ref, xp_ref, w_ref, dx_ref, dw_ref):
        i = pl.program_id(1)
        nxt = jnp.where(i < nrow - 1, dcn_ref[...], 0.0)
        dcv = dc_ref[...]
        dcc = jnp.concatenate([dcv, nxt], axis=0)
        prev = jnp.where(i > 0, xp_ref[...], 0.0)
        xc = jnp.concatenate([prev, x_ref[...]], axis=0)
        wv = w_ref[...]
        dx = wv[3:4, :] * dcv
        dws = [None] * 4
        dws[3] = jnp.sum(dcv * x_ref[...], axis=0, keepdims=True)
        for j in range(3):
            dx = dx + wv[j:j + 1, :] * pltpu.roll(dcc, tr + 8 - (3 - j), 0)[:tr, :]
            dws[j] = jnp.sum(dcv * pltpu.roll(xc, 3 - j, 0)[8:, :], axis=0, keepdims=True)
        dx_ref[...] = dx.astype(dx_ref.dtype)

        @pl.when(i == 0)
        def _():
            for j in range(4):
                dw_ref[j:j + 1, :] = dws[j]

        @pl.when(i > 0)
        def _():
            for j in range(4):
                dw_ref[j:j + 1, :] += dws[j]

    last8 = lp // 8 - 1
    return pl.pallas_call(
        body, name=name, grid=(ncb, nrow),
        in_specs=[pl.BlockSpec((tr, CONV_BC), lambda j, i: (i, j)),
                  pl.BlockSpec((8, CONV_BC), lambda j, i: (jnp.minimum((i + 1) * (tr // 8), last8), j)),
                  pl.BlockSpec((tr, CONV_BC), lambda j, i: (i, j + x_off)),
                  pl.BlockSpec((8, CONV_BC), lambda j, i: (jnp.maximum(i * (tr // 8) - 1, 0), j + x_off)),
                  pl.BlockSpec((4, CONV_BC), lambda j, i: (0, j + w_off))],
        out_specs=[pl.BlockSpec((tr, CONV_BC), lambda j, i: (i, j)),
                   pl.BlockSpec((4, CONV_BC), lambda j, i: (0, j))],
        out_shape=[jax.ShapeDtypeStruct((lp, width), bf16), jax.ShapeDtypeStruct((4, width), f32)],
        compiler_params=pltpu.CompilerParams(dimension_semantics=("arbitrary", "arbitrary")),
    )(dc, dc, x, x, w)


GDN_PACK = 4
GDN_FWD_INTERLEAVE, GDN_BWD_INTERLEAVE = 4, 4


def _bd(x, cb):
    r = x.shape[0]
    tall = jnp.concatenate([x] * GDN_PACK, axis=0)
    rows = lax.broadcasted_iota(jnp.int32, tall.shape, 0) // r
    cols = lax.broadcasted_iota(jnp.int32, tall.shape, 1) // cb
    return jnp.where(rows == cols, tall, jnp.zeros_like(tall))


def _diag(full, r, cb):
    cols = lax.broadcasted_iota(jnp.int32, (r, full.shape[1]), 1) // cb
    out = jnp.where(cols == 0, full[0:r, :], 0.0)
    for a in range(1, GDN_PACK):
        out = out + jnp.where(cols == a, full[a * r:(a + 1) * r, :], 0.0)
    return out


def _stack(x, cb):
    return jnp.concatenate([x[:, a * cb:(a + 1) * cb] for a in range(GDN_PACK)], axis=0)


def _make_packed(dot):
    @jax.custom_vjp
    def pmm(x, y):
        return dot(x, _bd(y, y.shape[1] // GDN_PACK), NN)

    @jax.custom_vjp
    def pnt(x, y):
        k = x.shape[1] // GDN_PACK
        return _diag(dot(_stack(x, k), _stack(y, k), NT), x.shape[0], y.shape[0])

    @jax.custom_vjp
    def ptn(x, y):
        return _diag(dot(x, y, TN), x.shape[1] // GDN_PACK, y.shape[1] // GDN_PACK)

    def pmm_bwd(res, ct):
        x, y = res
        cb = y.shape[1] // GDN_PACK
        return dot(ct, _bd(y, cb), NT), _diag(dot(x, ct, TN), y.shape[0], cb)

    pmm.defvjp(lambda x, y: (pmm(x, y), (x, y)), pmm_bwd)
    pnt.defvjp(lambda x, y: (pnt(x, y), (x, y)), lambda res, ct: (pmm(ct, res[1]), ptn(ct, res[0])))
    ptn.defvjp(lambda x, y: (ptn(x, y), (x, y)), lambda res, ct: (pnt(res[1], ct), pmm(res[0], ct)))
    return pmm, pnt, ptn


_pmm, _pnt, _ptn = _make_packed(_dlo)


@jax.custom_vjp
def _inv_packed(ms):
    c = ms[0].shape[0]
    ii = lax.broadcasted_iota(jnp.int32, ms[0].shape, 0)
    jj = lax.broadcasted_iota(jnp.int32, ms[0].shape, 1) % c
    ts = [jnp.where(ii == jj, 1.0, 0.0) - m for m in ms]
    ps = [(-m).astype(bf16) for m in ms]
    for _ in range(int(math.log2(c)) - 1):
        ps = [_dlo(p, _bd(p, c), NN).astype(bf16) for p in ps]
        ts = [t + _dlo(t, _bd(p, c), NN) for t, p in zip(ts, ps)]
    return tuple(ts)


def _inv_packed_fwd(ms):
    ts = _inv_packed(ms)
    return ts, ts


def _inv_packed_bwd(ts, cts):
    c = ts[0].shape[0]
    ys = [_diag(_dlo(t, ct, TN), c, c) for t, ct in zip(ts, cts)]
    return (tuple(-_dlo(y, _bd(t.astype(bf16), c), NT) for y, t in zip(ys, ts)),)


_inv_packed.defvjp(_inv_packed_fwd, _inv_packed_bwd)


def _gdn_prep(q2, k2, v4, bcols, gcols, grows):
    c, d = v4.shape[0], GDN_D
    q4 = jnp.concatenate([q2[:, :d], q2[:, :d], q2[:, d:], q2[:, d:]], axis=1)
    k4 = jnp.concatenate([k2[:, :d], k2[:, :d], k2[:, d:], k2[:, d:]], axis=1)
    beta4 = jnp.concatenate([jnp.broadcast_to(b, (c, d)) for b in bcols], axis=1)
    gc4 = jnp.concatenate([jnp.broadcast_to(g, (c, d)) for g in gcols], axis=1)
    low = lax.broadcasted_iota(jnp.int32, (c, 128), 1) < c
    gi = jnp.concatenate([jnp.where(low, gcols[0], gcols[1]), jnp.where(low, gcols[2], gcols[3])], axis=1)
    gj = jnp.concatenate([jnp.where(low, grows[0], grows[1]), jnp.where(low, grows[2], grows[3])], axis=1)
    ii = lax.broadcasted_iota(jnp.int32, gi.shape, 0)
    jj = lax.broadcasted_iota(jnp.int32, gi.shape, 1) % c
    dec = jnp.exp(jnp.where(ii >= jj, gi - gj, NEG))
    rid = lax.broadcasted_iota(jnp.int32, gc4.shape, 0)
    glast = jnp.sum(jnp.where(rid == c - 1, gc4, 0.0), axis=0, keepdims=True)
    eg = jnp.exp(gc4)
    kb = k4 * beta4
    return dict(q=q4, k=k4, kb=kb, vb=v4 * beta4, kbe=kb * eg, qe=q4 * eg, dec=dec, dec_strict=jnp.where(ii > jj, dec, 0.0),
                sdecay=jnp.exp(glast), kd=k4 * jnp.exp(glast - gc4))


def _gdn_groups(groups):
    c = groups[0][3].shape[0]
    ss = [g[0] for g in groups]
    e = [_gdn_prep(*g[1:]) for g in groups]
    ts = _inv_packed(tuple(_pnt(x["kb"], x["k"]) * x["dec_strict"] for x in e))
    us = [_pmm(t, x["vb"]) for t, x in zip(ts, e)]
    ws = [_pmm(t, x["kbe"]) for t, x in zip(ts, e)]
    attns = [_pnt(x["q"], x["k"]) * x["dec"] for x in e]
    ws_qs = [_pmm(jnp.concatenate([w, x["qe"]], axis=0), s) for w, x, s in zip(ws, e, ss)]
    v_news = [u - y[:c] for u, y in zip(us, ws_qs)]
    os = [y[c:] + _pmm(a, vn) for y, a, vn in zip(ws_qs, attns, v_news)]
    s_news = [s * x["sdecay"] + _ptn(x["kd"], vn) for s, x, vn in zip(ss, e, v_news)]
    return list(zip(os, s_news))


def _lane_pick(x, h):
    lane = lax.broadcasted_iota(jnp.int32, x.shape, 1)
    return jnp.sum(jnp.where(lane == h, x, 0.0), axis=1, keepdims=True)


def _cum_log_decay(g):
    c = g.shape[0]
    lower = (lax.broadcasted_iota(jnp.int32, (c, c), 0) >= lax.broadcasted_iota(jnp.int32, (c, c), 1)).astype(f32)
    upper2 = (lax.broadcasted_iota(jnp.int32, (c, 128), 0) <= lax.broadcasted_iota(jnp.int32, (c, 128), 1) % c).astype(f32)
    return _dhi(lower, g, NN), _dhi(g, upper2, TN)


def _group_operands(gi, s_ref, q_ref, k_ref, v_ref, bv, gcv, gct_s):
    heads = [gi * GDN_PACK + u for u in range(GDN_PACK)]
    qk_off = pl.multiple_of(gi * 2 * GDN_D, 2 * GDN_D)
    v_off = pl.multiple_of(gi * GDN_PACK * GDN_D, GDN_PACK * GDN_D)
    return (s_ref[gi], q_ref[:, pl.ds(qk_off, 2 * GDN_D)], k_ref[:, pl.ds(qk_off, 2 * GDN_D)],
            v_ref[:, pl.ds(v_off, GDN_PACK * GDN_D)],
            [_lane_pick(bv, h) for h in heads], [_lane_pick(gcv, h) for h in heads],
            [gct_s[pl.ds(h, 1), :] for h in heads]), heads, qk_off, v_off


def _gdn_fwd(qn, kn, v, beta, g, *, n_real_chunks, name):
    lp = qn.shape[0]
    nchunk = lp // GDN_CHUNK
    C, D = GDN_CHUNK, GDN_D
    NG, SW = GDN_V_HEADS // GDN_PACK, GDN_PACK * GDN_D

    def body(q_ref, k_ref, v_ref, b_ref, g_ref, o_ref, st_ref, s_s, gc_s, gct_s):
        ci = pl.program_id(0)

        @pl.when(ci == 0)
        def _():
            s_s[...] = jnp.zeros_like(s_s)

        @pl.when(ci >= n_real_chunks)
        def _():
            o_ref[...] = jnp.zeros_like(o_ref)
            st_ref[...] = jnp.zeros_like(st_ref)

        @pl.when(ci < n_real_chunks)
        def _():
            gc, gct = _cum_log_decay(g_ref[...])
            gc_s[...] = gc
            gct_s[...] = gct

            def some_groups(it, carry):
                ids = [it * GDN_FWD_INTERLEAVE + u for u in range(GDN_FWD_INTERLEAVE)]
                ops = [_group_operands(gi, s_s, q_ref, k_ref, v_ref, b_ref[...], gc_s[...], gct_s) for gi in ids]
                res = _gdn_groups([op[0] for op in ops])
                for gi, op, (o, s_new) in zip(ids, ops, res):
                    st_ref[gi] = op[0][0]
                    s_s[gi] = s_new
                    o_ref[:, pl.ds(op[3], SW)] = o
                return carry

            lax.fori_loop(0, NG // GDN_FWD_INTERLEAVE, some_groups, 0)

    return pl.pallas_call(
        body, name=name, grid=(nchunk,),
        in_specs=[pl.BlockSpec((C, GDN_QK_W), lambda c: (c, 0)), pl.BlockSpec((C, GDN_QK_W), lambda c: (c, 0)),
                  pl.BlockSpec((C, GDN_V_W), lambda c: (c, 0)), pl.BlockSpec((C, 128), lambda c: (c, 0)),
                  pl.BlockSpec((C, 128), lambda c: (c, 0))],
        out_specs=[pl.BlockSpec((C, GDN_V_W), lambda c: (c, 0)),
                   pl.BlockSpec((None, NG, D, SW), lambda c: (c, 0, 0, 0))],
        out_shape=[jax.ShapeDtypeStruct((lp, GDN_V_W), f32), jax.ShapeDtypeStruct((nchunk, NG, D, SW), f32)],
        scratch_shapes=[pltpu.VMEM((NG, D, SW), f32), pltpu.VMEM((C, 128), f32), pltpu.VMEM((128, 128), f32)],
        compiler_params=pltpu.CompilerParams(dimension_semantics=("arbitrary",)),
    )(qn, kn, v, beta, g)


def _gdn_bwd(qn, kn, v, beta, g, states, do, *, n_real_chunks, name):
    lp = qn.shape[0]
    nchunk = lp // GDN_CHUNK
    C, D = GDN_CHUNK, GDN_D
    NG, SW = GDN_V_HEADS // GDN_PACK, GDN_PACK * GDN_D
    rev = lambda i: (nchunk - 1 - i, 0)

    def body(q_ref, k_ref, v_ref, b_ref, g_ref, st_ref, do_ref,
             dq_ref, dk_ref, dv_ref, db_ref, dg_ref, ds_s, gc_s, gct_s, dgc_s, dgct_s, dbeta_s):
        step = pl.program_id(0)
        ci = nchunk - 1 - step

        @pl.when(step == 0)
        def _():
            ds_s[...] = jnp.zeros_like(ds_s)

        @pl.when(ci >= n_real_chunks)
        def _():
            for r in (dq_ref, dk_ref, dv_ref, db_ref, dg_ref):
                r[...] = jnp.zeros_like(r)

        @pl.when(ci < n_real_chunks)
        def _():
            gc, gct = _cum_log_decay(g_ref[...])
            gc_s[...] = gc
            gct_s[...] = gct
            dgc_s[...] = jnp.zeros_like(dgc_s)
            dgct_s[...] = jnp.zeros_like(dgct_s)
            dbeta_s[...] = jnp.zeros_like(dbeta_s)

            def some_groups(it, carry):
                ids = [it * GDN_BWD_INTERLEAVE + u for u in range(GDN_BWD_INTERLEAVE)]
                ops = [_group_operands(gi, st_ref, q_ref, k_ref, v_ref, b_ref[...], gc_s[...], gct_s) for gi in ids]
                cts = [(do_ref[:, pl.ds(op[3], SW)], ds_s[gi]) for gi, op in zip(ids, ops)]
                _, vjp_fn = jax.vjp(_gdn_groups, [op[0] for op in ops])
                (grads,) = vjp_fn(cts)
                lane = lax.broadcasted_iota(jnp.int32, (C, 128), 1)
                dbeta_acc, dgc_acc = dbeta_s[...], dgc_s[...]
                for gi, (_, heads, qk_off, v_off), (dsp, dq2, dk2, dv4, dbcols, dgcols, dgrows) in zip(ids, ops, grads):
                    ds_s[gi] = dsp
                    dq_ref[:, pl.ds(qk_off, 2 * D)] = dq2
                    dk_ref[:, pl.ds(qk_off, 2 * D)] = dk2
                    dv_ref[:, pl.ds(v_off, SW)] = dv4
                    for h, dbcol, dgcol, dgrow in zip(heads, dbcols, dgcols, dgrows):
                        dbeta_acc = dbeta_acc + jnp.where(lane == h, dbcol, 0.0)
                        dgc_acc = dgc_acc + jnp.where(lane == h, dgcol, 0.0)
                        dgct_s[pl.ds(h, 1), :] = dgrow
                dbeta_s[...] = dbeta_acc
                dgc_s[...] = dgc_acc
                return carry

            lax.fori_loop(0, NG // GDN_BWD_INTERLEAVE, some_groups, 0)
            fold = (lax.broadcasted_iota(jnp.int32, (128, C), 0) % C == lax.broadcasted_iota(jnp.int32, (128, C), 1)).astype(f32)
            eye = (lax.broadcasted_iota(jnp.int32, (128, 128), 0) == lax.broadcasted_iota(jnp.int32, (128, 128), 1)).astype(f32)
            dgc = dgc_s[...] + _dhi(_dhi(dgct_s[...], fold, NN), eye, TN)
            upper = (lax.broadcasted_iota(jnp.int32, (C, C), 0) <= lax.broadcasted_iota(jnp.int32, (C, C), 1)).astype(f32)
            dg_ref[...] = _dhi(upper, dgc, NN)
            db_ref[...] = dbeta_s[...]

    return pl.pallas_call(
        body, name=name, grid=(nchunk,),
        in_specs=[pl.BlockSpec((C, GDN_QK_W), rev), pl.BlockSpec((C, GDN_QK_W), rev), pl.BlockSpec((C, GDN_V_W), rev),
                  pl.BlockSpec((C, 128), rev), pl.BlockSpec((C, 128), rev),
                  pl.BlockSpec((None, NG, D, SW), lambda i: (nchunk - 1 - i, 0, 0, 0)), pl.BlockSpec((C, GDN_V_W), rev)],
        out_specs=[pl.BlockSpec((C, GDN_QK_W), rev), pl.BlockSpec((C, GDN_QK_W), rev), pl.BlockSpec((C, GDN_V_W), rev),
                   pl.BlockSpec((C, 128), rev), pl.BlockSpec((C, 128), rev)],
        out_shape=[jax.ShapeDtypeStruct((lp, GDN_QK_W), f32)] * 2 + [jax.ShapeDtypeStruct((lp, GDN_V_W), f32)]
        + [jax.ShapeDtypeStruct((lp, 128), f32)] * 2,
        scratch_shapes=[pltpu.VMEM((NG, D, SW), f32), pltpu.VMEM((C, 128), f32), pltpu.VMEM((128, 128), f32),
                        pltpu.VMEM((C, 128), f32), pltpu.VMEM((128, 128), f32), pltpu.VMEM((C, 128), f32)],
        compiler_params=pltpu.CompilerParams(dimension_semantics=("arbitrary",)),
    )(qn, kn, v, beta, g, states, do)


def _att_mask_t(k0, q0, tk, tq):
    kcol = k0 + lax.broadcasted_iota(jnp.int32, (tk, tq), 0)
    qrow = q0 + lax.broadcasted_iota(jnp.int32, (tk, tq), 1)
    return jnp.logical_and(qrow >= kcol, kcol >= PAD_FRONT)


def _attention_fwd(q, kvu, kr, *, name, tk=ATT_TILE):
    lp = q.shape[0]
    H = MLA_HEADS
    tq = _pick(lp, (768, 512, 256))
    r = tq // tk

    def body(q_ref, kn_ref, kr_ref, v_ref, o_ref, lse_ref, m_s, l_s, acc_s, sa_s, sb_s):
        qi = pl.program_id(1)
        m_s[...] = jnp.full_like(m_s, NEG)
        l_s[...] = jnp.zeros_like(l_s)
        acc_s[...] = jnp.zeros_like(acc_s)

        def scores(ki):
            k0 = pl.multiple_of(ki * tk, tk)
            k = jnp.concatenate([kn_ref[pl.ds(k0, tk), :], kr_ref[pl.ds(k0, tk), :]], axis=1)
            return lax.dot_general(k, q_ref[...], (NT, ((), ())), preferred_element_type=f32)

        def consume(st, ki, masked):
            k0 = pl.multiple_of(ki * tk, tk)
            if masked:
                st = jnp.where(_att_mask_t(k0, qi * tq, tk, tq), st, NEG)
            m_prev = m_s[...]
            m_new = jnp.maximum(m_prev, jnp.max(st, axis=0, keepdims=True))
            alpha = jnp.exp(m_prev - m_new)
            p = jnp.exp(st - m_new)
            l_s[...] = alpha * l_s[...] + jnp.sum(p, axis=0, keepdims=True)
            acc_s[...] = alpha * acc_s[...] + lax.dot_general(v_ref[pl.ds(k0, tk), :], p.astype(bf16), (TN, ((), ())),
                                                              preferred_element_type=f32)
            m_s[...] = m_new

        n_full = qi * r

        def chain(blocks):
            bufs = (sa_s, sb_s)
            for j, (ki, masked) in enumerate(blocks):
                if j + 1 < len(blocks):
                    bufs[(j + 1) % 2][...] = scores(blocks[j + 1][0])
                consume(bufs[j % 2][...], ki, masked)

        diagonal = [(n_full + d, True) for d in range(r)]

        @pl.when(qi == 0)
        def _():
            sa_s[...] = scores(0)
            chain(diagonal)

        @pl.when(qi > 0)
        def _():
            sb_s[...] = scores(0)
            sa_s[...] = scores(1)
            consume(sb_s[...], 0, True)
            n_pairs = (n_full - 1) // 2

            def two(pi, carry):
                ki = 1 + 2 * pi
                sb_s[...] = scores(ki + 1)
                consume(sa_s[...], ki, False)
                sa_s[...] = scores(ki + 2)
                consume(sb_s[...], ki + 1, False)
                return carry

            lax.fori_loop(0, n_pairs, two, 0)
            nxt = 1 + 2 * n_pairs

            @pl.when(nxt < n_full)
            def _():
                chain([(nxt, False)] + diagonal)

            @pl.when(nxt == n_full)
            def _():
                chain(diagonal)
        o_ref[...] = jnp.transpose(acc_s[...] / l_s[...])
        lse_ref[...] = m_s[...] + jnp.log(l_s[...])

    return pl.pallas_call(
        body, name=name, grid=(H, lp // tq),
        in_specs=[pl.BlockSpec((tq, MLA_QKP), lambda h, qi: (qi, h)),
                  pl.BlockSpec((lp, 128), lambda h, qi: (0, h)),
                  pl.BlockSpec((lp, 128), lambda h, qi: (0, 0)),
                  pl.BlockSpec((lp, 128), lambda h, qi: (0, H + h))],
        out_specs=[pl.BlockSpec((tq, 128), lambda h, qi: (qi, h)),
                   pl.BlockSpec((None, 1, tq), lambda h, qi: (h, 0, qi))],
        out_shape=[jax.ShapeDtypeStruct((lp, MLA_V_W), f32), jax.ShapeDtypeStruct((H, 1, lp), f32)],
        scratch_shapes=[pltpu.VMEM((1, tq), f32), pltpu.VMEM((1, tq), f32), pltpu.VMEM((128, tq), f32),
                        pltpu.VMEM((tk, tq), f32), pltpu.VMEM((tk, tq), f32)],
        compiler_params=pltpu.CompilerParams(dimension_semantics=("arbitrary", "arbitrary")),
    )(q, kvu, kr, kvu)


def _attention_delta(o, do, *, name):
    lp = o.shape[0]
    H = MLA_HEADS
    tq = _pick(lp, (768, 512, 256))

    def body(o_ref, do_ref, d_ref):
        prod = o_ref[...] * do_ref[...].astype(f32)
        d_ref[...] = jnp.sum(jnp.transpose(prod), axis=0, keepdims=True)

    return pl.pallas_call(
        body, name=name, grid=(H, lp // tq),
        in_specs=[pl.BlockSpec((tq, 128), lambda h, qi: (qi, h)), pl.BlockSpec((tq, 128), lambda h, qi: (qi, h))],
        out_specs=pl.BlockSpec((None, 1, tq), lambda h, qi: (h, 0, qi)),
        out_shape=jax.ShapeDtypeStruct((H, 1, lp), f32),
    )(o, do)


def _attention_bwd(q, kvu, kr, lse, delta, do, *, name, t=ATT_TILE):
    lp = q.shape[0]
    H, nb = MLA_HEADS, lp // t
    tq = _pick(lp, (768, 512, 256))
    r, nq = tq // t, lp // tq

    def body(q_ref, kn_ref, kr_ref, v_ref, lse_ref, dl_ref, do_ref, dq_ref, dkn_ref, dv_ref, dkr_ref, dk_s, dv_s,
             sa_s, da_s, sb_s, db_s):
        ki = pl.program_id(1)
        k0 = ki * t
        k = jnp.concatenate([kn_ref[...], kr_ref[...]], axis=1)
        vv = v_ref[...]
        dk_s[...] = jnp.zeros_like(dk_s)
        dv_s[...] = jnp.zeros_like(dv_s)

        def products(qi, s_ref, d_ref):
            q0 = pl.multiple_of(qi * tq, tq)
            s_ref[...] = lax.dot_general(k, q_ref[pl.ds(q0, tq), :], (NT, ((), ())), preferred_element_type=f32)
            d_ref[...] = lax.dot_general(vv, do_ref[pl.ds(q0, tq), :], (NT, ((), ())), preferred_element_type=f32)

        def accumulate(s_ref, d_ref, qi, masked, first):
            q0 = pl.multiple_of(qi * tq, tq)
            qv = q_ref[pl.ds(q0, tq), :]
            dob = do_ref[pl.ds(q0, tq), :]
            st = s_ref[...]
            if masked:
                st = jnp.where(_att_mask_t(k0, q0, t, tq), st, NEG)
            p = jnp.exp(st - lse_ref[:, pl.ds(q0, tq)])
            dv_s[...] += jnp.dot(p.astype(bf16), dob, preferred_element_type=f32)
            ds = (p * (d_ref[...] - dl_ref[:, pl.ds(q0, tq)])).astype(bf16)
            dk_s[...] += jnp.dot(ds, qv, preferred_element_type=f32)
            dq = lax.dot_general(ds, k, (TN, ((), ())), preferred_element_type=f32)
            if first:
                dq_ref[pl.ds(q0, tq), :] = dq
            else:
                dq_ref[pl.ds(q0, tq), :] += dq

        def sweep(qd, mask_all, first):
            last = nq - 1
            products(qd, sa_s, da_s)
            products(jnp.minimum(qd + 1, last), sb_s, db_s)
            accumulate(sa_s, da_s, qd, True, first)
            n = last - qd

            def two(pi, carry):
                i = qd + 1 + 2 * pi
                products(i + 1, sa_s, da_s)
                accumulate(sb_s, db_s, i, mask_all, first)
                products(jnp.minimum(i + 2, last), sb_s, db_s)
                accumulate(sa_s, da_s, i + 1, mask_all, first)
                return carry

            lax.fori_loop(0, n // 2, two, 0)

            @pl.when(n % 2 == 1)
            def _():
                accumulate(sb_s, db_s, last, mask_all, first)

        @pl.when(ki == 0)
        def _():
            sweep(0, True, True)

        @pl.when(ki > 0)
        def _():
            sweep(ki // r, False, False)

        dkn_ref[...] = dk_s[:, :128].astype(dkn_ref.dtype)
        dkr_ref[...] = dk_s[:, 128:]
        dv_ref[...] = dv_s[...].astype(dv_ref.dtype)

    return pl.pallas_call(
        body, name=name, grid=(H, nb),
        in_specs=[pl.BlockSpec((lp, MLA_QKP), lambda h, ki: (0, h)),
                  pl.BlockSpec((t, 128), lambda h, ki: (ki, h)),
                  pl.BlockSpec((t, 128), lambda h, ki: (ki, 0)),
                  pl.BlockSpec((t, 128), lambda h, ki: (ki, H + h)),
                  pl.BlockSpec((None, 1, lp), lambda h, ki: (h, 0, 0)),
                  pl.BlockSpec((None, 1, lp), lambda h, ki: (h, 0, 0)),
                  pl.BlockSpec((lp, 128), lambda h, ki: (0, h))],
        out_specs=[pl.BlockSpec((lp, MLA_QKP), lambda h, ki: (0, h)),
                   pl.BlockSpec((t, 128), lambda h, ki: (ki, h)),
                   pl.BlockSpec((t, 128), lambda h, ki: (ki, h)),
                   pl.BlockSpec((t, 128), lambda h, ki: (ki, h))],
        out_shape=[jax.ShapeDtypeStruct((lp, H * MLA_QKP), f32), jax.ShapeDtypeStruct((lp, MLA_V_W), bf16),
                   jax.ShapeDtypeStruct((lp, MLA_V_W), bf16), jax.ShapeDtypeStruct((lp, MLA_V_W), f32)],
        scratch_shapes=[pltpu.VMEM((t, MLA_QKP), f32), pltpu.VMEM((t, 128), f32)] + [pltpu.VMEM((t, tq), f32)] * 4,
        compiler_params=pltpu.CompilerParams(dimension_semantics=("arbitrary", "arbitrary")),
    )(q, kvu, kr, kvu, lse, delta, do)


def _exchange(x, *, gather, name):
    blk = x.shape if gather else x.shape[1:]

    def body(x_ref, o_ref, send_sems, recv_sems, local_sem):
        mx, my, mc = lax.axis_index("x"), lax.axis_index("y"), lax.axis_index("c")
        me = 4 * mx + 2 * my + mc
        own = pltpu.make_async_copy(x_ref if gather else x_ref.at[me], o_ref.at[me], local_sem)
        own.start()
        sends, peers = [], []
        for k in range(1, N_DEV):
            px = 1 - mx if k & 4 else mx
            py = 1 - my if k & 2 else my
            pc = 1 - mc if k & 1 else mc
            peer = 4 * px + 2 * py + pc
            cp = pltpu.make_async_remote_copy(
                src_ref=x_ref if gather else x_ref.at[peer], dst_ref=o_ref.at[me],
                send_sem=send_sems.at[k - 1], recv_sem=recv_sems.at[k - 1],
                device_id=(px, py, pc), device_id_type=MESH_ID)
            cp.start()
            sends.append(cp)
            peers.append(peer)
        for k in range(1, N_DEV):
            pltpu.make_async_remote_copy(
                src_ref=o_ref.at[peers[k - 1]], dst_ref=o_ref.at[peers[k - 1]],
                send_sem=send_sems.at[k - 1], recv_sem=recv_sems.at[k - 1],
                device_id=(mx, my, mc), device_id_type=MESH_ID).wait_recv()
        for cp in sends:
            cp.wait_send()
        own.wait()

    return pl.pallas_call(
        body, name=name,
        in_specs=[pl.BlockSpec(memory_space=pltpu.HBM)], out_specs=pl.BlockSpec(memory_space=pltpu.HBM),
        out_shape=jax.ShapeDtypeStruct((N_DEV,) + tuple(blk), x.dtype),
        scratch_shapes=[pltpu.SemaphoreType.DMA((N_DEV - 1,)), pltpu.SemaphoreType.DMA((N_DEV - 1,)),
                        pltpu.SemaphoreType.DMA],
    )(x)


def _reduce_adamw(parts, w, m, v, *, name):
    r = w.shape[0]
    tr = _pick(r, (1280, 1024, 512, 256, 128, 64, 48, 32, 16, 8))

    def body(p_ref, w_ref, m_ref, v_ref, g_ref, d_ref, nm_ref, nv_ref):
        g = p_ref[0].astype(f32)
        for s in range(1, N_DEV):
            g = g + p_ref[s].astype(f32)
        mm = ADAM_B1 * m_ref[...] + (1.0 - ADAM_B1) * g
        vv = ADAM_B2 * v_ref[...] + (1.0 - ADAM_B2) * (g * g)
        m_hat = mm / (1.0 - ADAM_B1 ** ADAM_STEP)
        v_hat = vv / (1.0 - ADAM_B2 ** ADAM_STEP)
        g_ref[...] = g
        d_ref[...] = -ADAM_LR * (m_hat / (jnp.sqrt(v_hat) + ADAM_EPS) + ADAM_WD * w_ref[...])
        nm_ref[...] = mm
        nv_ref[...] = vv

    spec = pl.BlockSpec((tr, 128), lambda i: (i, 0))
    return pl.pallas_call(
        body, name=name, grid=(r // tr,),
        in_specs=[pl.BlockSpec((N_DEV, tr, 128), lambda i: (0, i, 0)), spec, spec, spec],
        out_specs=[spec] * 4, out_shape=[jax.ShapeDtypeStruct((r, 128), f32)] * 4,
    )(parts, w, m, v)


_SHARDED = ("gdn_w_in", "gdn_w_out", "kv_w_down", "kv_w_up", "mla_w_in", "mla_w_q_up", "mla_w_out", "meta_tokens", "gdn_conv_w")
_COL_SHARDED = {"gdn_w_in", "kv_w_up", "mla_w_in", "mla_w_q_up", "meta_tokens", "gdn_conv_w"}
_N_BF16 = 7
_REPLICATED = ("pre_norm", "post_norm", "gdn_a_log", "gdn_dt_bias", "gdn_out_norm", "kv_norm", "kv_latent_norm",
               "mla_q_latent_norm")


def _rows128(a):
    flat = a.reshape(-1)
    pad = (-flat.shape[0]) % 128
    if pad:
        flat = jnp.pad(flat, (0, pad))
    return flat.reshape(-1, 128)


def _pack(arrs, row_multiple):
    parts = [_rows128(a) for a in arrs]
    buf = jnp.concatenate(parts, axis=0)
    pad = (-buf.shape[0]) % row_multiple
    if pad:
        buf = jnp.pad(buf, ((0, pad), (0, 0)))
    return buf


def _unpack(buf, shapes):
    out, r = [], 0
    for shp in shapes:
        n = math.prod(shp)
        rows = -(-n // 128)
        out.append(buf[r:r + rows].reshape(-1)[:n].reshape(shp))
        r += rows
    return out


def _unshard(g, full_shape, col):
    if col:
        return jnp.transpose(g, (1, 0, 2)).reshape(full_shape)
    return g.reshape(full_shape)


def _to_shards(a, col):
    r, c = a.shape
    if col:
        return jnp.transpose(a.reshape(r, N_DEV, c // N_DEV), (1, 0, 2))
    return a.reshape(N_DEV, r // N_DEV, c)


def _pad_cols(a, width):
    return jnp.pad(a, ((0, 0), (0, width - a.shape[1])))


def _rope_tables(lp):
    inv = ROPE_THETA ** (-jnp.arange(0, MLA_ROPE, 2, dtype=f32) / MLA_ROPE)
    pos = (jnp.arange(lp, dtype=jnp.int32) - PAD_FRONT).astype(f32)
    ang = pos[:, None] * inv[None, :]
    cos, sin = jnp.cos(ang), jnp.sin(ang)
    z = jnp.zeros((lp, 64), f32)
    return jnp.concatenate([cos, cos, z], axis=1), jnp.concatenate([-sin, sin, z], axis=1)


def kernel(x, meta_tokens, pre_norm, post_norm, gdn_w_in, gdn_conv_w, gdn_a_log, gdn_dt_bias, gdn_out_norm, gdn_w_out, kv_norm, kv_w_down, kv_latent_norm, kv_w_up, mla_w_in, mla_q_latent_norm, mla_w_q_up, mla_w_out, loss_target, m_meta_tokens, m_pre_norm, m_post_norm, m_gdn_w_in, m_gdn_conv_w, m_gdn_a_log, m_gdn_dt_bias, m_gdn_out_norm, m_gdn_w_out, m_kv_norm, m_kv_w_down, m_kv_latent_norm, m_kv_w_up, m_mla_w_in, m_mla_q_latent_norm, m_mla_w_q_up, m_mla_w_out, v_meta_tokens, v_pre_norm, v_post_norm, v_gdn_w_in, v_gdn_conv_w, v_gdn_a_log, v_gdn_dt_bias, v_gdn_out_norm, v_gdn_w_out, v_kv_norm, v_kv_w_down, v_kv_latent_norm, v_kv_w_up, v_mla_w_in, v_mla_q_latent_norm, v_mla_w_q_up, v_mla_w_out):
    W = dict(meta_tokens=meta_tokens, pre_norm=pre_norm, post_norm=post_norm, gdn_w_in=gdn_w_in, gdn_conv_w=gdn_conv_w,
             gdn_a_log=gdn_a_log, gdn_dt_bias=gdn_dt_bias, gdn_out_norm=gdn_out_norm, gdn_w_out=gdn_w_out, kv_norm=kv_norm,
             kv_w_down=kv_w_down, kv_latent_norm=kv_latent_norm, kv_w_up=kv_w_up, mla_w_in=mla_w_in,
             mla_q_latent_norm=mla_q_latent_norm, mla_w_q_up=mla_w_q_up, mla_w_out=mla_w_out)
    M = dict(meta_tokens=m_meta_tokens, pre_norm=m_pre_norm, post_norm=m_post_norm, gdn_w_in=m_gdn_w_in, gdn_conv_w=m_gdn_conv_w,
             gdn_a_log=m_gdn_a_log, gdn_dt_bias=m_gdn_dt_bias, gdn_out_norm=m_gdn_out_norm, gdn_w_out=m_gdn_w_out, kv_norm=m_kv_norm,
             kv_w_down=m_kv_w_down, kv_latent_norm=m_kv_latent_norm, kv_w_up=m_kv_w_up, mla_w_in=m_mla_w_in,
             mla_q_latent_norm=m_mla_q_latent_norm, mla_w_q_up=m_mla_w_q_up, mla_w_out=m_mla_w_out)
    V = dict(meta_tokens=v_meta_tokens, pre_norm=v_pre_norm, post_norm=v_post_norm, gdn_w_in=v_gdn_w_in, gdn_conv_w=v_gdn_conv_w,
             gdn_a_log=v_gdn_a_log, gdn_dt_bias=v_gdn_dt_bias, gdn_out_norm=v_gdn_out_norm, gdn_w_out=v_gdn_w_out, kv_norm=v_kv_norm,
             kv_w_down=v_kv_w_down, kv_latent_norm=v_kv_latent_norm, kv_w_up=v_kv_w_up, mla_w_in=v_mla_w_in,
             mla_q_latent_norm=v_mla_q_latent_norm, mla_w_q_up=v_mla_w_q_up, mla_w_out=v_mla_w_out)
    order = list(W)

    n_tok = x.shape[1]
    assert n_tok % GDN_CHUNK == 0
    n_real = ROW0 + n_tok
    lp = -(-n_real // ROW_TILE) * ROW_TILE
    n_real_chunks = n_real // GDN_CHUNK

    shard2d = {n: W[n].reshape(W[n].shape[-2:]) for n in _SHARDED}
    full_shape = {n: ((s.shape[0], s.shape[1] * N_DEV) if n in _COL_SHARDED else (s.shape[0] * N_DEV, s.shape[1]))
                  for n, s in shard2d.items()}
    big, small = _SHARDED[:_N_BF16], _SHARDED[_N_BF16:]
    g_big = _exchange(_pack([shard2d[n].astype(bf16) for n in big], 16), gather=True, name="gather_weights")
    g_small = _exchange(_pack([shard2d[n] for n in small], 8), gather=True, name="gather_meta_conv")
    full = {}
    for names, buf in ((big, g_big), (small, g_small)):
        r = 0
        for n in names:
            shp = shard2d[n].shape
            rows = math.prod(shp) // 128
            blocks = buf[:, r:r + rows].reshape((N_DEV,) + shp)
            full[n] = _unshard(blocks, full_shape[n], n in _COL_SHARDED)
            r += rows

    w_in = full["gdn_w_in"]
    s1 = GDN_CONV_W + GDN_V_W
    w_in_p = jnp.concatenate([w_in[:, :s1], _pad_cols(w_in[:, s1:s1 + 16], 128), _pad_cols(w_in[:, s1 + 16:], 128)], axis=1)
    wd = full["kv_w_down"]
    zc = jnp.zeros((D_MODEL, 64), bf16)
    wd2 = jnp.concatenate([wd, zc, jnp.zeros((D_MODEL, 128), bf16), wd[:, 160:192], wd[:, 128:160], zc], axis=1)
    wup_p = jnp.transpose(full["kv_w_up"].reshape(MLA_KV_RANK, MLA_HEADS, 2, 128), (0, 2, 1, 3)).reshape(MLA_KV_RANK, 2 * MLA_V_W)
    wq = full["mla_w_q_up"].reshape(MLA_Q_RANK, MLA_HEADS, MLA_QK)
    zq64 = jnp.zeros((MLA_Q_RANK, MLA_HEADS, 64), bf16)
    wq_plain = jnp.concatenate([wq, zq64], axis=2).reshape(MLA_Q_RANK, MLA_HEADS * MLA_QKP)
    wq_swap = jnp.concatenate([jnp.zeros((MLA_Q_RANK, MLA_HEADS, 128), bf16), wq[:, :, 160:192], wq[:, :, 128:160], zq64],
                              axis=2).reshape(MLA_Q_RANK, MLA_HEADS * MLA_QKP)
    q_half = Q_GROUP * MLA_QKP
    wq2 = jnp.concatenate([wq_plain[:, :q_half], wq_swap[:, :q_half], wq_plain[:, q_half:], wq_swap[:, q_half:]], axis=1)
    w_mla_in, w_gdn_out, w_mla_out = full["mla_w_in"], full["gdn_w_out"], full["mla_w_out"]
    conv_w = full["gdn_conv_w"]
    pre0, pre1 = pre_norm[0:1], pre_norm[1:2]
    post0, post1 = post_norm[0:1], post_norm[1:2]
    alog_p, dtb_p = _pad_cols(gdn_a_log, 128), _pad_cols(gdn_dt_bias, 128)
    kvn, kvln = kv_norm.reshape(1, -1), kv_latent_norm.reshape(1, -1)

    h0 = jnp.concatenate([jnp.zeros((PAD_FRONT, D_MODEL), f32), full["meta_tokens"], x[0],
                          jnp.zeros((lp - n_real, D_MODEL), f32)], axis=0)
    tgt = jnp.concatenate([jnp.zeros((ROW0, D_MODEL), f32), loss_target[0], jnp.zeros((lp - n_real, D_MODEL), f32)], axis=0)
    cos_k, sin_k = _rope_tables(lp)
    one = jnp.ones((lp, 128), f32)
    cos_q = jnp.concatenate([one, cos_k], axis=1)
    sin_q = jnp.concatenate([jnp.zeros((lp, 128), f32), sin_k], axis=1)

    (hn0,) = _rowwise(_st_prenorm, [(h0, None, 0)], [], [pre0], [(D_MODEL, bf16, None)], name="f_prenorm0")
    proj = _matmul(hn0, w_in_p, name="f_gdn_in")
    conv = _conv_fwd(proj, conv_w, col_blocks=GDN_CONV_W // CONV_BC, name="f_conv")
    (qn,) = _rowwise(_st_gdn_q, [(conv, GDN_QK_W, 0)], [], [], [(GDN_QK_W, f32, GDN_QK_W)], heads=GDN_QK_HEADS, name="f_gdn_q")
    (kn,) = _rowwise(_st_gdn_k, [(conv, GDN_QK_W, 1)], [], [], [(GDN_QK_W, f32, GDN_QK_W)], heads=GDN_QK_HEADS, name="f_gdn_k")
    (vv,) = _rowwise(_st_gdn_v, [(conv, GDN_V_W, 1)], [], [], [(GDN_V_W, f32, GDN_V_W)], name="f_gdn_v")
    gate_rows = [(proj, 128, s1 // 128), (proj, 128, s1 // 128 + 1)]
    beta, gdec = _rowwise(_st_gdn_gate, gate_rows, [], [alog_p, dtb_p], [(128, f32, None)] * 2, name="f_gdn_gate")
    o_gdn, states = _gdn_fwd(qn, kn, vv, beta, gdec, n_real_chunks=n_real_chunks, name="f_gdn")
    out_rows = [(o_gdn, GDN_V_W, 0), (proj, GDN_V_W, GDN_CONV_W // GDN_V_W)]
    (og,) = _rowwise(_st_gdn_out, out_rows, [], [gdn_out_norm], [(GDN_V_W, bf16, GDN_V_W)], heads=GDN_V_HEADS, name="f_gdn_out")
    y0 = _matmul(og, w_gdn_out, name="f_gdn_wout")
    mid_rows = [(h0, None, 0), (y0, None, 0)]
    h1, hn1, hkv = _rowwise(_st_mid, mid_rows, [], [post0, pre1, kvn],
                            [(D_MODEL, f32, None), (D_MODEL, bf16, None), (D_MODEL, bf16, None)], name="f_mid")
    ckr = _matmul(hkv, wd2, name="f_kv_down")
    proj2 = _matmul(hn1, w_mla_in, name="f_mla_in")
    lat_rows = [(ckr, None, 0), (proj2, MLA_Q_RANK, 0)]
    lat_nd = [(cos_k, None, 0), (sin_k, None, 0)]
    c_kv, k_rope, c_q = _rowwise(_st_latent, lat_rows, lat_nd, [kvln, mla_q_latent_norm],
                                 [(128, bf16, None), (128, bf16, None), (MLA_Q_RANK, bf16, None)], name="f_latent")
    kvu = _matmul(c_kv, wup_p, out_dtype=bf16, name="f_kv_up")
    qq = _matmul(c_q, wq2, out_dtype=bf16, name="f_q_up")
    q_nd = [(cos_q, None, 0), (sin_q, None, 0)]
    (q_att,) = _rowwise(_st_q_rope, [(qq, 2 * q_half, 0)], q_nd, [], [(2 * q_half, bf16, q_half)], ncol=2, name="f_q_rope")
    o_att, lse = _attention_fwd(q_att, kvu, k_rope, name="f_attention")
    gate2_rows = [(o_att, MLA_Q_RANK, 0), (proj2, MLA_Q_RANK, 1)]
    n_gate = MLA_V_W // MLA_Q_RANK
    (og2,) = _rowwise(_st_gate, gate2_rows, [], [], [(MLA_V_W, bf16, MLA_Q_RANK)], ncol=n_gate, name="f_mla_gate")
    y1 = _matmul(og2, w_mla_out, name="f_mla_wout")
    st_loss = _make_st_loss(n_tok)
    loss_rows_in = [(h1, None, 0), (y1, None, 0)]
    (loss_rows,) = _rowwise(st_loss, loss_rows_in, [(tgt, None, 0)], [post1], [(1, f32, None)], name="f_loss")
    loss = lax.psum(jnp.sum(loss_rows), ("x", "y", "c"))

    ones_ct = jnp.ones((lp, 1), f32)
    (dh1_a, dy1), (dpost1,) = _rowwise_vjp(st_loss, loss_rows_in, [(tgt, None, 0)], [post1], [(ones_ct, None, 0)], name="b_loss")
    dog2 = _matmul(dy1, w_mla_out, tb=True, name="b_mla_wout_x")
    dw_mla_out = _matmul(og2, dy1, ta=True, name="b_mla_wout_w")
    (do_att, dz2), _ = _rowwise_vjp(_st_gate, gate2_rows, [], [], [(dog2, MLA_Q_RANK, 0)], ncol=n_gate, name="b_mla_gate",
                                    grad_dtypes=[bf16, bf16])
    delta = _attention_delta(o_att, do_att, name="b_attention_delta")
    dq_att, dkn, dvv, dkr_h = _attention_bwd(q_att, kvu, k_rope, lse, delta, do_att, name="b_attention")
    (dqq,) = _rowwise(_st_q_rope_t, [(dq_att, q_half, 0)], q_nd, [], [(4 * q_half, bf16, 2 * q_half)], ncol=2, name="b_q_rope")
    dc_q = _matmul(dqq, wq2, tb=True, name="b_q_up_x")
    dwq2 = _matmul(c_q, dqq, ta=True, name="b_q_up_w")
    dkvu = jnp.concatenate([dkn, dvv], axis=1)
    dc_kv = _matmul(dkvu, wup_p, tb=True, name="b_kv_up_x")
    dwup_p = _matmul(c_kv, dkvu, ta=True, name="b_kv_up_w")

    def lat_ct(cv):
        dkr = cv[1][:, 0:128]
        for h in range(1, MLA_HEADS):
            dkr = dkr + cv[1][:, h * 128:(h + 1) * 128]
        return [cv[0], dkr, cv[2]]

    (dckr, dcq_pre), (dkvln, dqln) = _rowwise_vjp(
        _st_latent, lat_rows, lat_nd, [kvln, mla_q_latent_norm],
        [(dc_kv, None, 0), (dkr_h, None, 0), (dc_q, None, 0)], ct_pre=lat_ct, name="b_latent", grad_dtypes=[bf16, bf16])
    dproj2 = jnp.concatenate([dcq_pre, dz2], axis=1)
    dhn1 = _matmul(dproj2, w_mla_in, tb=True, name="b_mla_in_x")
    dw_mla_in = _matmul(hn1, dproj2, ta=True, name="b_mla_in_w")
    dhkv = _matmul(dckr, wd2, tb=True, name="b_kv_down_x")
    dwd2 = _matmul(hkv, dckr, ta=True, name="b_kv_down_w")
    (dh0_a, dy0), (dpost0, dpre1, dkvn) = _rowwise_vjp(
        _st_mid, mid_rows, [], [post0, pre1, kvn], [(dh1_a, None, 0), (dhn1, None, 0), (dhkv, None, 0)], name="b_mid")
    dog = _matmul(dy0, w_gdn_out, tb=True, name="b_gdn_wout_x")
    dw_gdn_out = _matmul(og, dy0, ta=True, name="b_gdn_wout_w")
    (do_gdn, dz), (dout_norm,) = _rowwise_vjp(_st_gdn_out, out_rows, [], [gdn_out_norm], [(dog, GDN_V_W, 0)], heads=GDN_V_HEADS,
                                              name="b_gdn_out", grad_dtypes=[f32, bf16])
    dq_g, dk_g, dv_g, dbeta, dgdec = _gdn_bwd(qn, kn, vv, beta, gdec, states, do_gdn, n_real_chunks=n_real_chunks, name="b_gdn")
    (db_col, da_col), (dalog_p, ddtb_p) = _rowwise_vjp(
        _st_gdn_gate, gate_rows, [], [alog_p, dtb_p], [(dbeta, None, 0), (dgdec, None, 0)], name="b_gdn_gate",
        grad_dtypes=[bf16, bf16])
    (dconv_q,), _ = _rowwise_vjp(_st_gdn_q, [(conv, GDN_QK_W, 0)], [], [], [(dq_g, GDN_QK_W, 0)], heads=GDN_QK_HEADS, name="b_gdn_q")
    (dconv_k,), _ = _rowwise_vjp(_st_gdn_k, [(conv, GDN_QK_W, 1)], [], [], [(dk_g, GDN_QK_W, 0)], heads=GDN_QK_HEADS, name="b_gdn_k")
    (dconv_v,), _ = _rowwise_vjp(_st_gdn_v, [(conv, GDN_V_W, 1)], [], [], [(dv_g, GDN_V_W, 0)], name="b_gdn_v")
    nq_b = GDN_QK_W // CONV_BC
    dpre_q, dcw_q = _conv_bwd(dconv_q, proj, conv_w, x_off=0, w_off=0, name="b_conv_q")
    dpre_k, dcw_k = _conv_bwd(dconv_k, proj, conv_w, x_off=nq_b, w_off=nq_b, name="b_conv_k")
    dpre_v, dcw_v = _conv_bwd(dconv_v, proj, conv_w, x_off=2 * nq_b, w_off=2 * nq_b, name="b_conv_v")
    dproj = jnp.concatenate([dpre_q, dpre_k, dpre_v, dz, db_col, da_col], axis=1)
    dhn0 = _matmul(dproj, w_in_p, tb=True, name="b_gdn_in_x")
    dw_in_p = _matmul(hn0, dproj, ta=True, name="b_gdn_in_w")
    (dh0,), (dpre0,) = _rowwise_vjp(_st_prenorm, [(h0, None, 0)], [], [pre0], [(dhn0, None, 0)], extra=[dh0_a], name="b_prenorm0")

    grad_x = dh0[ROW0:n_real][None]
    G = {}
    G["meta_tokens"] = dh0[PAD_FRONT:ROW0]
    G["gdn_w_in"] = jnp.concatenate([dw_in_p[:, :s1 + 16], dw_in_p[:, s1 + 128:s1 + 144]], axis=1)
    G["gdn_conv_w"] = jnp.concatenate([dcw_q, dcw_k, dcw_v], axis=1)
    G["gdn_w_out"] = dw_gdn_out
    G["kv_w_down"] = jnp.concatenate([dwd2[:, :128], dwd2[:, 128:160] + dwd2[:, 416:448], dwd2[:, 160:192] + dwd2[:, 384:416]], axis=1)
    G["kv_w_up"] = jnp.transpose(dwup_p.reshape(MLA_KV_RANK, 2, MLA_HEADS, 128), (0, 2, 1, 3)).reshape(MLA_KV_RANK, 2 * MLA_V_W)
    G["mla_w_in"] = dw_mla_in
    dq4 = dwq2.reshape(MLA_Q_RANK, 2, 2, Q_GROUP, MLA_QKP)
    dqp = dq4[:, :, 0].reshape(MLA_Q_RANK, MLA_HEADS, MLA_QKP)
    dqs = dq4[:, :, 1].reshape(MLA_Q_RANK, MLA_HEADS, MLA_QKP)
    G["mla_w_q_up"] = jnp.concatenate([dqp[:, :, :128], dqp[:, :, 128:160] + dqs[:, :, 160:192],
                                       dqp[:, :, 160:192] + dqs[:, :, 128:160]], axis=2).reshape(MLA_Q_RANK, MLA_HEADS * MLA_QK)
    G["mla_w_out"] = dw_mla_out
    G["pre_norm"] = jnp.concatenate([dpre0, dpre1], axis=0)
    G["post_norm"] = jnp.concatenate([dpost0, dpost1], axis=0)
    G["gdn_a_log"] = dalog_p[:, :GDN_V_HEADS]
    G["gdn_dt_bias"] = ddtb_p[:, :GDN_V_HEADS]
    G["gdn_out_norm"] = dout_norm
    G["kv_norm"] = dkvn.reshape(-1)
    G["kv_latent_norm"] = dkvln.reshape(-1)
    G["mla_q_latent_norm"] = dqln

    send = jnp.concatenate([_to_shards(G[n], n in _COL_SHARDED).reshape(N_DEV, -1, 128).astype(bf16) for n in _SHARDED], axis=1)
    parts = _exchange(send, gather=False, name="scatter_grads")
    w_s, m_s, v_s = (_pack([d[n] for n in _SHARDED], 8) for d in (W, M, V))
    res_s = _reduce_adamw(parts, w_s, m_s, v_s, name="adamw_sharded")
    small_send = _pack([G[n] for n in _REPLICATED], 8)
    parts_r = _exchange(small_send, gather=True, name="gather_small_grads")
    w_r, m_r, v_r = (_pack([d[n] for n in _REPLICATED], 8) for d in (W, M, V))
    res_r = _reduce_adamw(parts_r, w_r, m_r, v_r, name="adamw_replicated")

    outs = {}
    for kind, bs, br in zip(("grad", "delta", "new_m", "new_v"), res_s, res_r):
        for n, a in zip(_SHARDED, _unpack(bs, [W[n].shape for n in _SHARDED])):
            outs[kind, n] = a
        for n, a in zip(_REPLICATED, _unpack(br, [W[n].shape for n in _REPLICATED])):
            outs[kind, n] = a
    return (loss, grad_x, *[outs[k, n] for k in ("grad", "delta", "new_m", "new_v") for n in order])
```

```python
import functools
import math

import jax
import jax.numpy as jnp
from jax import lax
from jax.experimental import pallas as pl
from jax.experimental.pallas import tpu as pltpu

f32, bf16 = jnp.float32, jnp.bfloat16
HIGHEST = lax.Precision.HIGHEST
MESH_ID = pl.DeviceIdType.MESH

N_DEV = 8
D_MODEL = 1024
N_META = 16
NORM_EPS = 1e-6
PAD_FRONT = 48
ROW0 = PAD_FRONT + N_META
GDN_QK_HEADS, GDN_V_HEADS, GDN_D = 8, 16, 128
GDN_CHUNK = 64
GDN_QK_W, GDN_V_W = GDN_QK_HEADS * GDN_D, GDN_V_HEADS * GDN_D
GDN_CONV_W = 2 * GDN_QK_W + GDN_V_W
GDN_IN_W = GDN_CONV_W + GDN_V_W + 2 * GDN_V_HEADS
GDN_IN_WP = GDN_CONV_W + GDN_V_W + 2 * 128
MLA_HEADS, MLA_NOPE, MLA_ROPE, MLA_V = 16, 128, 64, 128
MLA_Q_RANK, MLA_KV_RANK = 256, 128
MLA_QK = MLA_NOPE + MLA_ROPE
MLA_QKP = 256
MLA_V_W = MLA_HEADS * MLA_V
ROPE_THETA = 10000.0
NEG = -1e30
ROW_TILE = 256
ATT_KEY_TILES = (384, 256)

ADAM_LR, ADAM_B1, ADAM_B2, ADAM_EPS, ADAM_WD, ADAM_STEP = 0.001, 0.9, 0.999, 1e-08, 0.01, 10

NN = ((1,), (0,))
NT = ((1,), (1,))
TN = ((0,), (0,))


def _pick(dim, prefs):
    for p in prefs:
        if dim % p == 0:
            return p
    return dim


def _dlo(a, b, dims):
    return lax.dot_general(a.astype(bf16), b.astype(bf16), (dims, ((), ())), preferred_element_type=f32)


def _dhi(a, b, dims):
    return lax.dot_general(a, b, (dims, ((), ())), precision=HIGHEST, preferred_element_type=f32)


def _matmul(a, b, *, ta=False, tb=False, out_dtype=f32, name, side=None):
    assert not (ta and tb)
    if ta:
        kdim, m = a.shape
    else:
        m, kdim = a.shape
    n = b.shape[0] if tb else b.shape[1]
    assert (b.shape[1] if tb else b.shape[0]) == kdim
    tm = _pick(m, (1024, 768, 512, 384, 256, 128))
    tn = _pick(n, (1024, 768, 640, 512, 256, 128))
    tk = _pick(kdim, (1024, 768, 640, 512, 256, 128))
    nk = kdim // tk
    dims = TN if ta else (NT if tb else NN)

    grid = (m // tm, n // tn, nk)

    def product(a_ref, b_ref, o_ref, acc_ref):
        k = pl.program_id(2)

        @pl.when(k == 0)
        def _():
            acc_ref[...] = jnp.zeros_like(acc_ref)

        acc_ref[...] += lax.dot_general(a_ref[...].astype(bf16), b_ref[...].astype(bf16), (dims, ((), ())),
                                        preferred_element_type=f32)

        @pl.when(k == nk - 1)
        def _():
            o_ref[...] = acc_ref[...].astype(o_ref.dtype)

    a_spec = pl.BlockSpec((tk, tm), lambda i, j, k: (k, i)) if ta else pl.BlockSpec((tm, tk), lambda i, j, k: (i, k))
    b_spec = pl.BlockSpec((tn, tk), lambda i, j, k: (j, k)) if tb else pl.BlockSpec((tk, tn), lambda i, j, k: (k, j))
    o_spec = pl.BlockSpec((tm, tn), lambda i, j, k: (i, j))
    o_shape = jax.ShapeDtypeStruct((m, n), out_dtype)
    if side is None:
        def body(a_ref, b_ref, o_ref, acc_ref):
            product(a_ref, b_ref, o_ref, acc_ref)

        return pl.pallas_call(
            body, name=name, grid=grid, in_specs=[a_spec, b_spec], out_specs=o_spec, out_shape=o_shape,
            scratch_shapes=[pltpu.VMEM((tm, tn), f32)],
            compiler_params=pltpu.CompilerParams(dimension_semantics=("parallel", "parallel", "arbitrary")),
        )(a, b)

    x, gather = side
    blk = x.shape if gather else x.shape[1:]

    def body_with_exchange(a_ref, b_ref, x_ref, o_ref, xo_ref, acc_ref, send_sems, recv_sems, local_sem):
        step = (pl.program_id(0) * grid[1] + pl.program_id(1)) * grid[2] + pl.program_id(2)
        copies = _exchange_copies(x_ref, xo_ref, send_sems, recv_sems, local_sem, gather)

        @pl.when(step == 0)
        def _():
            _exchange_start(copies)

        product(a_ref, b_ref, o_ref, acc_ref)

        @pl.when(step == grid[0] * grid[1] * grid[2] - 1)
        def _():
            _exchange_wait(copies)

    hbm = pl.BlockSpec(memory_space=pltpu.HBM)
    return pl.pallas_call(
        body_with_exchange, name=name, grid=grid, in_specs=[a_spec, b_spec, hbm], out_specs=[o_spec, hbm],
        out_shape=[o_shape, jax.ShapeDtypeStruct((N_DEV,) + tuple(blk), x.dtype)],
        scratch_shapes=[pltpu.VMEM((tm, tn), f32)] + list(_EXCHANGE_SCRATCH),
        compiler_params=pltpu.CompilerParams(dimension_semantics=("arbitrary", "arbitrary", "arbitrary")),
    )(a, b, x)


def _row_spec(item, tr):
    a, bc, off = item
    if bc is None:
        return pl.BlockSpec((tr, a.shape[1]), lambda i, j: (i, 0))
    return pl.BlockSpec((tr, bc), lambda i, j, off=off: (i, j + off))


def _param_spec(p):
    return pl.BlockSpec(p.shape, lambda i, j: (0, 0))


def _row_tile(lp, items):
    widest = max(a.shape[1] if bc is None else bc for (a, bc, _) in items)
    return ROW_TILE if widest >= 1024 else _pick(lp, (768, 512, 256))


def _head_cols(tiles, h, heads):
    return [x[:, h * (x.shape[1] // heads):(h + 1) * (x.shape[1] // heads)] for x in tiles]


def _rowwise(fn, rows, nodiff, params, outs, *, ncol=1, heads=1, name):
    lp = rows[0][0].shape[0]
    tr = _row_tile(lp, rows)
    nr, nd = len(rows), len(nodiff)

    def body(*refs):
        rv = [r[...].astype(f32) for r in refs[:nr]]
        nv = [r[...] for r in refs[nr:nr + nd]]
        pv = [r[...] for r in refs[nr + nd:nr + nd + len(params)]]
        per_head = [fn(_head_cols(rv, h, heads), nv, pv) for h in range(heads)]
        res = [jnp.concatenate(list(vals), axis=1) if heads > 1 else vals[0] for vals in zip(*per_head)]
        for ref, val in zip(refs[nr + nd + len(params):], res):
            ref[...] = val.astype(ref.dtype)

    out_specs = [pl.BlockSpec((tr, c if bc is None else bc), (lambda i, j: (i, 0)) if bc is None else (lambda i, j: (i, j)))
                 for (c, _, bc) in outs]
    return pl.pallas_call(
        body, name=name, grid=(lp // tr, ncol),
        in_specs=[_row_spec(it, tr) for it in rows + nodiff] + [_param_spec(p) for p in params],
        out_specs=out_specs,
        out_shape=[jax.ShapeDtypeStruct((lp, c), dt) for (c, dt, _) in outs],
    )(*[it[0] for it in rows + nodiff], *params)


def _rowwise_vjp(fn, rows, nodiff, params, cts, *, ncol=1, heads=1, name, ct_pre=None, extra=None, grad_dtypes=None):
    lp = rows[0][0].shape[0]
    tr = _row_tile(lp, rows)
    nr, nd, npar, nct = len(rows), len(nodiff), len(params), len(cts)
    extra = extra or [None] * nr
    grad_dtypes = grad_dtypes or [f32] * nr
    ex_items = [(e, rows[k][1], 0) for k, e in enumerate(extra) if e is not None]
    ex_pos = [k for k, e in enumerate(extra) if e is not None]
    for (a, bc, _) in rows:
        assert bc is not None or ncol == 1

    def body(*refs):
        pos = 0
        rv = [r[...].astype(f32) for r in refs[pos:pos + nr]]; pos += nr
        nv = [r[...] for r in refs[pos:pos + nd]]; pos += nd
        pv = [r[...] for r in refs[pos:pos + npar]]; pos += npar
        cv = [r[...].astype(f32) for r in refs[pos:pos + nct]]; pos += nct
        ev = [r[...].astype(f32) for r in refs[pos:pos + len(ex_items)]]; pos += len(ex_items)
        drow_refs = refs[pos:pos + nr]; pos += nr
        dpar_refs = refs[pos:pos + npar]
        ctv = ct_pre(cv) if ct_pre is not None else cv
        drow_h, dpar = [], None
        for h in range(heads):
            outs, vjp_fn = jax.vjp(lambda rr, pp: fn(rr, nv, pp), _head_cols(rv, h, heads), pv)
            dr, dp = vjp_fn([c.astype(o.dtype) for c, o in zip(_head_cols(ctv, h, heads), outs)])
            drow_h.append(dr)
            dpar = dp if dpar is None else [a + b for a, b in zip(dpar, dp)]
        drow = [jnp.concatenate(list(vals), axis=1) if heads > 1 else vals[0] for vals in zip(*drow_h)]
        for k, e in zip(ex_pos, ev):
            drow[k] = drow[k] + e
        for ref, val in zip(drow_refs, drow):
            ref[...] = val.astype(ref.dtype)
        first = jnp.logical_and(pl.program_id(0) == 0, pl.program_id(1) == 0)

        @pl.when(first)
        def _():
            for ref, val in zip(dpar_refs, dpar):
                ref[...] = val

        @pl.when(jnp.logical_not(first))
        def _():
            for ref, val in zip(dpar_refs, dpar):
                ref[...] += val

    drow_shapes, drow_specs = [], []
    for (a, bc, _), dt in zip(rows, grad_dtypes):
        if bc is None:
            drow_shapes.append(jax.ShapeDtypeStruct((lp, a.shape[1]), dt))
            drow_specs.append(pl.BlockSpec((tr, a.shape[1]), lambda i, j: (i, 0)))
        else:
            drow_shapes.append(jax.ShapeDtypeStruct((lp, ncol * bc), dt))
            drow_specs.append(pl.BlockSpec((tr, bc), lambda i, j: (i, j)))
    res = pl.pallas_call(
        body, name=name, grid=(lp // tr, ncol),
        in_specs=[_row_spec(it, tr) for it in rows + nodiff] + [_param_spec(p) for p in params]
        + [_row_spec(it, tr) for it in cts + ex_items],
        out_specs=drow_specs + [_param_spec(p) for p in params],
        out_shape=drow_shapes + [jax.ShapeDtypeStruct(p.shape, f32) for p in params],
        compiler_params=pltpu.CompilerParams(dimension_semantics=("arbitrary", "arbitrary")),
    )(*[it[0] for it in rows + nodiff], *params, *[it[0] for it in cts + ex_items])
    return res[:nr], res[nr:]


def _rms(x, g):
    return x * lax.rsqrt(jnp.mean(x * x, axis=-1, keepdims=True) + NORM_EPS) * g


def _l2n(x):
    return x * lax.rsqrt(jnp.sum(x * x, axis=-1, keepdims=True) + NORM_EPS)


def _sigmoid(x):
    return 1.0 / (1.0 + jnp.exp(-x))


def _silu(x):
    return x * _sigmoid(x)


def _softplus(x):
    return jnp.maximum(x, 0.0) + jnp.log(1.0 + jnp.exp(-jnp.abs(x)))


def _row_ids(shape):
    return pl.program_id(0) * shape[0] + lax.broadcasted_iota(jnp.int32, shape, 0)


def _st_prenorm(r, n, p):
    return [_rms(r[0], p[0])]


def _st_gdn_q(r, n, p):
    return [_l2n(_silu(r[0])) * (GDN_D ** -0.5)]


def _st_gdn_k(r, n, p):
    return [_l2n(_silu(r[0]))]


def _st_gdn_v(r, n, p):
    return [_silu(r[0])]


def _st_gdn_gate(r, n, p):
    real = _row_ids(r[0].shape) >= PAD_FRONT
    beta = jnp.where(real, _sigmoid(r[0]), 0.0)
    g = jnp.where(real, -jnp.exp(p[0]) * _softplus(r[1] + p[1]), 0.0)
    return [beta, g]


def _st_gdn_out(r, n, p):
    return [_rms(r[0], p[0]) * _silu(r[1])]


def _st_mid(r, n, p):
    h1 = r[0] + _rms(r[1], p[0])
    return [h1, _rms(h1, p[1]), _rms(h1, p[2])]


def _st_latent(r, n, p):
    ckr, cq = r
    c_kv = _rms(ckr[:, :MLA_KV_RANK], p[0])
    k_rope = ckr[:, 128:256] * n[0] + ckr[:, 384:512] * n[1]
    return [c_kv, k_rope, _rms(cq, p[1])]


Q_GROUP = 8


def _st_q_rope(r, n, p):
    half = Q_GROUP * MLA_QKP
    out = []
    for h in range(Q_GROUP):
        cols = slice(h * MLA_QKP, (h + 1) * MLA_QKP)
        out.append((r[0][:, :half][:, cols] * n[0] + r[0][:, half:][:, cols] * n[1]) * (MLA_QK ** -0.5))
    return [jnp.concatenate(out, axis=1)]


def _st_q_rope_t(r, n, p):
    plain, swapped = [], []
    for h in range(Q_GROUP):
        ct = r[0][:, h * MLA_QKP:(h + 1) * MLA_QKP] * (MLA_QK ** -0.5)
        plain.append(ct * n[0])
        swapped.append(ct * n[1])
    return [jnp.concatenate(plain + swapped, axis=1)]


def _st_gate(r, n, p):
    return [r[0] * _silu(r[1])]


def _make_st_loss(n_tokens):
    def st(r, n, p):
        h2 = r[0] + _rms(r[1], p[0])
        rows = _row_ids((r[0].shape[0], 1))
        real = jnp.logical_and(rows >= ROW0, rows < ROW0 + n_tokens)
        err = h2 - n[0]
        return [jnp.where(real, 0.5 * jnp.mean(err * err, axis=-1, keepdims=True), 0.0)]
    return st


CONV_BC = 1024


def _conv_fwd(x, w, *, col_blocks, name, tr=ROW_TILE):
    lp = x.shape[0]

    def body(x_ref, xp_ref, w_ref, o_ref):
        i = pl.program_id(0)
        prev = jnp.where(i > 0, xp_ref[...], 0.0)
        xc = jnp.concatenate([prev, x_ref[...]], axis=0)
        wv = w_ref[...]
        acc = wv[3:4, :] * x_ref[...]
        for j in range(3):
            acc = acc + wv[j:j + 1, :] * pltpu.roll(xc, 3 - j, 0)[8:, :]
        o_ref[...] = acc

    return pl.pallas_call(
        body, name=name, grid=(lp // tr, col_blocks),
        in_specs=[pl.BlockSpec((tr, CONV_BC), lambda i, j: (i, j)),
                  pl.BlockSpec((8, CONV_BC), lambda i, j: (jnp.maximum(i * (tr // 8) - 1, 0), j)),
                  pl.BlockSpec((4, CONV_BC), lambda i, j: (0, j))],
        out_specs=pl.BlockSpec((tr, CONV_BC), lambda i, j: (i, j)),
        out_shape=jax.ShapeDtypeStruct((lp, col_blocks * CONV_BC), f32),
    )(x, x, w)


def _conv_bwd(dc, x, w, *, x_off, w_off, name, tr=ROW_TILE):
    lp, width = dc.shape
    ncb, nrow = width // CONV_BC, lp // tr

    def body(dc_ref, dcn_ref, x_ref, xp_ref, w_ref, dx_ref, dw_ref):
        i = pl.program_id(1)
        nxt = jnp.where(i < nrow - 1, dcn_ref[...], 0.0)
        dcv = dc_ref[...]
        dcc = jnp.concatenate([dcv, nxt], axis=0)
        prev = jnp.where(i > 0, xp_ref[...], 0.0)
        xc = jnp.concatenate([prev, x_ref[...]], axis=0)
        wv = w_ref[...]
        dx = wv[3:4, :] * dcv
        dws = [None] * 4
        dws[3] = jnp.sum(dcv * x_ref[...], axis=0, keepdims=True)
        for j in range(3):
            dx = dx + wv[j:j + 1, :] * pltpu.roll(dcc, tr + 8 - (3 - j), 0)[:tr, :]
            dws[j] = jnp.sum(dcv * pltpu.roll(xc, 3 - j, 0)[8:, :], axis=0, keepdims=True)
        dx_ref[...] = dx.astype(dx_ref.dtype)

        @pl.when(i == 0)
        def _():
            for j in range(4):
                dw_ref[j:j + 1, :] = dws[j]

        @pl.when(i > 0)
        def _():
            for j in range(4):
                dw_ref[j:j + 1, :] += dws[j]

    last8 = lp // 8 - 1
    return pl.pallas_call(
        body, name=name, grid=(ncb, nrow),
        in_specs=[pl.BlockSpec((tr, CONV_BC), lambda j, i: (i, j)),
                  pl.BlockSpec((8, CONV_BC), lambda j, i: (jnp.minimum((i + 1) * (tr // 8), last8), j)),
                  pl.BlockSpec((tr, CONV_BC), lambda j, i: (i, j + x_off)),
                  pl.BlockSpec((8, CONV_BC), lambda j, i: (jnp.maximum(i * (tr // 8) - 1, 0), j + x_off)),
                  pl.BlockSpec((4, CONV_BC), lambda j, i: (0, j + w_off))],
        out_specs=[pl.BlockSpec((tr, CONV_BC), lambda j, i: (i, j)),
                   pl.BlockSpec((4, CONV_BC), lambda j, i: (0, j))],
        out_shape=[jax.ShapeDtypeStruct((lp, width), bf16), jax.ShapeDtypeStruct((4, width), f32)],
        compiler_params=pltpu.CompilerParams(dimension_semantics=("arbitrary", "arbitrary")),
    )(dc, dc, x, x, w)


GDN_PACK = 4
GDN_FWD_INTERLEAVE, GDN_BWD_INTERLEAVE = 4, 4


def _bd(x, cb):
    r = x.shape[0]
    tall = jnp.concatenate([x] * GDN_PACK, axis=0)
    rows = lax.broadcasted_iota(jnp.int32, tall.shape, 0) // r
    cols = lax.broadcasted_iota(jnp.int32, tall.shape, 1) // cb
    return jnp.where(rows == cols, tall, jnp.zeros_like(tall))


def _diag(full, r, cb):
    cols = lax.broadcasted_iota(jnp.int32, (r, full.shape[1]), 1) // cb
    out = jnp.where(cols == 0, full[0:r, :], 0.0)
    for a in range(1, GDN_PACK):
        out = out + jnp.where(cols == a, full[a * r:(a + 1) * r, :], 0.0)
    return out


def _stack(x, cb):
    return jnp.concatenate([x[:, a * cb:(a + 1) * cb] for a in range(GDN_PACK)], axis=0)


def _make_packed(dot):
    @jax.custom_vjp
    def pmm(x, y):
        return dot(x, _bd(y, y.shape[1] // GDN_PACK), NN)

    @jax.custom_vjp
    def pnt(x, y):
        k = x.shape[1] // GDN_PACK
        return _diag(dot(_stack(x, k), _stack(y, k), NT), x.shape[0], y.shape[0])

    @jax.custom_vjp
    def ptn(x, y):
        return _diag(dot(x, y, TN), x.shape[1] // GDN_PACK, y.shape[1] // GDN_PACK)

    def pmm_bwd(res, ct):
        x, y = res
        cb = y.shape[1] // GDN_PACK
        return dot(ct, _bd(y, cb), NT), _diag(dot(x, ct, TN), y.shape[0], cb)

    pmm.defvjp(lambda x, y: (pmm(x, y), (x, y)), pmm_bwd)
    pnt.defvjp(lambda x, y: (pnt(x, y), (x, y)), lambda res, ct: (pmm(ct, res[1]), ptn(ct, res[0])))
    ptn.defvjp(lambda x, y: (ptn(x, y), (x, y)), lambda res, ct: (pnt(res[1], ct), pmm(res[0], ct)))
    return pmm, pnt, ptn


_pmm, _pnt, _ptn = _make_packed(_dlo)


@jax.custom_vjp
def _inv_packed(ms):
    c = ms[0].shape[0]
    ii = lax.broadcasted_iota(jnp.int32, ms[0].shape, 0)
    jj = lax.broadcasted_iota(jnp.int32, ms[0].shape, 1) % c
    ts = [jnp.where(ii == jj, 1.0, 0.0) - m for m in ms]
    ps = [(-m).astype(bf16) for m in ms]
    for _ in range(int(math.log2(c)) - 1):
        ps = [_dlo(p, _bd(p, c), NN).astype(bf16) for p in ps]
        ts = [t + _dlo(t, _bd(p, c), NN) for t, p in zip(ts, ps)]
    return tuple(ts)


def _inv_packed_fwd(ms):
    ts = _inv_packed(ms)
    return ts, ts


def _inv_packed_bwd(ts, cts):
    c = ts[0].shape[0]
    ys = [_diag(_dlo(t, ct, TN), c, c) for t, ct in zip(ts, cts)]
    return (tuple(-_dlo(y, _bd(t.astype(bf16), c), NT) for y, t in zip(ys, ts)),)


_inv_packed.defvjp(_inv_packed_fwd, _inv_packed_bwd)


def _gdn_prep(q2, k2, v4, bcols, gcols, grows):
    c, d = v4.shape[0], GDN_D
    q4 = jnp.concatenate([q2[:, :d], q2[:, :d], q2[:, d:], q2[:, d:]], axis=1)
    k4 = jnp.concatenate([k2[:, :d], k2[:, :d], k2[:, d:], k2[:, d:]], axis=1)
    beta4 = jnp.concatenate([jnp.broadcast_to(b, (c, d)) for b in bcols], axis=1)
    gc4 = jnp.concatenate([jnp.broadcast_to(g, (c, d)) for g in gcols], axis=1)
    low = lax.broadcasted_iota(jnp.int32, (c, 128), 1) < c
    gi = jnp.concatenate([jnp.where(low, gcols[0], gcols[1]), jnp.where(low, gcols[2], gcols[3])], axis=1)
    gj = jnp.concatenate([jnp.where(low, grows[0], grows[1]), jnp.where(low, grows[2], grows[3])], axis=1)
    ii = lax.broadcasted_iota(jnp.int32, gi.shape, 0)
    jj = lax.broadcasted_iota(jnp.int32, gi.shape, 1) % c
    dec = jnp.exp(jnp.where(ii >= jj, gi - gj, NEG))
    rid = lax.broadcasted_iota(jnp.int32, gc4.shape, 0)
    glast = jnp.sum(jnp.where(rid == c - 1, gc4, 0.0), axis=0, keepdims=True)
    eg = jnp.exp(gc4)
    kb = k4 * beta4
    return dict(q=q4, k=k4, kb=kb, vb=v4 * beta4, kbe=kb * eg, qe=q4 * eg, dec=dec, dec_strict=jnp.where(ii > jj, dec, 0.0),
                sdecay=jnp.exp(glast), kd=k4 * jnp.exp(glast - gc4))


def _gdn_groups(groups):
    c = groups[0][3].shape[0]
    ss = [g[0] for g in groups]
    e = [_gdn_prep(*g[1:]) for g in groups]
    ts = _inv_packed(tuple(_pnt(x["kb"], x["k"]) * x["dec_strict"] for x in e))
    us = [_pmm(t, x["vb"]) for t, x in zip(ts, e)]
    ws = [_pmm(t, x["kbe"]) for t, x in zip(ts, e)]
    attns = [_pnt(x["q"], x["k"]) * x["dec"] for x in e]
    ws_qs = [_pmm(jnp.concatenate([w, x["qe"]], axis=0), s) for w, x, s in zip(ws, e, ss)]
    v_news = [u - y[:c] for u, y in zip(us, ws_qs)]
    os = [y[c:] + _pmm(a, vn) for y, a, vn in zip(ws_qs, attns, v_news)]
    s_news = [s * x["sdecay"] + _ptn(x["kd"], vn) for s, x, vn in zip(ss, e, v_news)]
    return list(zip(os, s_news))


def _lane_pick(x, h):
    lane = lax.broadcasted_iota(jnp.int32, x.shape, 1)
    return jnp.sum(jnp.where(lane == h, x, 0.0), axis=1, keepdims=True)


def _cum_log_decay(g):
    c = g.shape[0]
    lower = (lax.broadcasted_iota(jnp.int32, (c, c), 0) >= lax.broadcasted_iota(jnp.int32, (c, c), 1)).astype(f32)
    upper2 = (lax.broadcasted_iota(jnp.int32, (c, 128), 0) <= lax.broadcasted_iota(jnp.int32, (c, 128), 1) % c).astype(f32)
    return _dhi(lower, g, NN), _dhi(g, upper2, TN)


def _group_operands(gi, s_ref, q_ref, k_ref, v_ref, bv, gcv, gct_s):
    heads = [gi * GDN_PACK + u for u in range(GDN_PACK)]
    qk_off = pl.multiple_of(gi * 2 * GDN_D, 2 * GDN_D)
    v_off = pl.multiple_of(gi * GDN_PACK * GDN_D, GDN_PACK * GDN_D)
    return (s_ref[gi], q_ref[:, pl.ds(qk_off, 2 * GDN_D)], k_ref[:, pl.ds(qk_off, 2 * GDN_D)],
            v_ref[:, pl.ds(v_off, GDN_PACK * GDN_D)],
            [_lane_pick(bv, h) for h in heads], [_lane_pick(gcv, h) for h in heads],
            [gct_s[pl.ds(h, 1), :] for h in heads]), heads, qk_off, v_off


def _gdn_fwd(qn, kn, v, beta, g, *, n_real_chunks, name):
    lp = qn.shape[0]
    nchunk = lp // GDN_CHUNK
    C, D = GDN_CHUNK, GDN_D
    NG, SW = GDN_V_HEADS // GDN_PACK, GDN_PACK * GDN_D

    def body(q_ref, k_ref, v_ref, b_ref, g_ref, o_ref, st_ref, s_s, gc_s, gct_s):
        ci = pl.program_id(0)

        @pl.when(ci == 0)
        def _():
            s_s[...] = jnp.zeros_like(s_s)

        @pl.when(ci >= n_real_chunks)
        def _():
            o_ref[...] = jnp.zeros_like(o_ref)
            st_ref[...] = jnp.zeros_like(st_ref)

        @pl.when(ci < n_real_chunks)
        def _():
            gc, gct = _cum_log_decay(g_ref[...])
            gc_s[...] = gc
            gct_s[...] = gct

            def some_groups(it, carry):
                ids = [it * GDN_FWD_INTERLEAVE + u for u in range(GDN_FWD_INTERLEAVE)]
                ops = [_group_operands(gi, s_s, q_ref, k_ref, v_ref, b_ref[...], gc_s[...], gct_s) for gi in ids]
                res = _gdn_groups([op[0] for op in ops])
                for gi, op, (o, s_new) in zip(ids, ops, res):
                    st_ref[gi] = op[0][0]
                    s_s[gi] = s_new
                    o_ref[:, pl.ds(op[3], SW)] = o
                return carry

            lax.fori_loop(0, NG // GDN_FWD_INTERLEAVE, some_groups, 0)

    return pl.pallas_call(
        body, name=name, grid=(nchunk,),
        in_specs=[pl.BlockSpec((C, GDN_QK_W), lambda c: (c, 0)), pl.BlockSpec((C, GDN_QK_W), lambda c: (c, 0)),
                  pl.BlockSpec((C, GDN_V_W), lambda c: (c, 0)), pl.BlockSpec((C, 128), lambda c: (c, 0)),
                  pl.BlockSpec((C, 128), lambda c: (c, 0))],
        out_specs=[pl.BlockSpec((C, GDN_V_W), lambda c: (c, 0)),
                   pl.BlockSpec((None, NG, D, SW), lambda c: (c, 0, 0, 0))],
        out_shape=[jax.ShapeDtypeStruct((lp, GDN_V_W), f32), jax.ShapeDtypeStruct((nchunk, NG, D, SW), f32)],
        scratch_shapes=[pltpu.VMEM((NG, D, SW), f32), pltpu.VMEM((C, 128), f32), pltpu.VMEM((128, 128), f32)],
        compiler_params=pltpu.CompilerParams(dimension_semantics=("arbitrary",)),
    )(qn, kn, v, beta, g)


def _gdn_bwd(qn, kn, v, beta, g, states, do, *, n_real_chunks, name):
    lp = qn.shape[0]
    nchunk = lp // GDN_CHUNK
    C, D = GDN_CHUNK, GDN_D
    NG, SW = GDN_V_HEADS // GDN_PACK, GDN_PACK * GDN_D
    rev = lambda i: (nchunk - 1 - i, 0)

    def body(q_ref, k_ref, v_ref, b_ref, g_ref, st_ref, do_ref,
             dq_ref, dk_ref, dv_ref, db_ref, dg_ref, ds_s, gc_s, gct_s, dgc_s, dgct_s, dbeta_s):
        step = pl.program_id(0)
        ci = nchunk - 1 - step

        @pl.when(step == 0)
        def _():
            ds_s[...] = jnp.zeros_like(ds_s)

        @pl.when(ci >= n_real_chunks)
        def _():
            for r in (dq_ref, dk_ref, dv_ref, db_ref, dg_ref):
                r[...] = jnp.zeros_like(r)

        @pl.when(ci < n_real_chunks)
        def _():
            gc, gct = _cum_log_decay(g_ref[...])
            gc_s[...] = gc
            gct_s[...] = gct
            dgc_s[...] = jnp.zeros_like(dgc_s)
            dgct_s[...] = jnp.zeros_like(dgct_s)
            dbeta_s[...] = jnp.zeros_like(dbeta_s)

            def some_groups(it, carry):
                ids = [it * GDN_BWD_INTERLEAVE + u for u in range(GDN_BWD_INTERLEAVE)]
                ops = [_group_operands(gi, st_ref, q_ref, k_ref, v_ref, b_ref[...], gc_s[...], gct_s) for gi in ids]
                cts = [(do_ref[:, pl.ds(op[3], SW)], ds_s[gi]) for gi, op in zip(ids, ops)]
                _, vjp_fn = jax.vjp(_gdn_groups, [op[0] for op in ops])
                (grads,) = vjp_fn(cts)
                lane = lax.broadcasted_iota(jnp.int32, (C, 128), 1)
                dbeta_acc, dgc_acc = dbeta_s[...], dgc_s[...]
                for gi, (_, heads, qk_off, v_off), (dsp, dq2, dk2, dv4, dbcols, dgcols, dgrows) in zip(ids, ops, grads):
                    ds_s[gi] = dsp
                    dq_ref[:, pl.ds(qk_off, 2 * D)] = dq2
                    dk_ref[:, pl.ds(qk_off, 2 * D)] = dk2
                    dv_ref[:, pl.ds(v_off, SW)] = dv4
                    for h, dbcol, dgcol, dgrow in zip(heads, dbcols, dgcols, dgrows):
                        dbeta_acc = dbeta_acc + jnp.where(lane == h, dbcol, 0.0)
                        dgc_acc = dgc_acc + jnp.where(lane == h, dgcol, 0.0)
                        dgct_s[pl.ds(h, 1), :] = dgrow
                dbeta_s[...] = dbeta_acc
                dgc_s[...] = dgc_acc
                return carry

            lax.fori_loop(0, NG // GDN_BWD_INTERLEAVE, some_groups, 0)
            fold = (lax.broadcasted_iota(jnp.int32, (128, C), 0) % C == lax.broadcasted_iota(jnp.int32, (128, C), 1)).astype(f32)
            eye = (lax.broadcasted_iota(jnp.int32, (128, 128), 0) == lax.broadcasted_iota(jnp.int32, (128, 128), 1)).astype(f32)
            dgc = dgc_s[...] + _dhi(_dhi(dgct_s[...], fold, NN), eye, TN)
            upper = (lax.broadcasted_iota(jnp.int32, (C, C), 0) <= lax.broadcasted_iota(jnp.int32, (C, C), 1)).astype(f32)
            dg_ref[...] = _dhi(upper, dgc, NN)
            db_ref[...] = dbeta_s[...]

    return pl.pallas_call(
        body, name=name, grid=(nchunk,),
        in_specs=[pl.BlockSpec((C, GDN_QK_W), rev), pl.BlockSpec((C, GDN_QK_W), rev), pl.BlockSpec((C, GDN_V_W), rev),
                  pl.BlockSpec((C, 128), rev), pl.BlockSpec((C, 128), rev),
                  pl.BlockSpec((None, NG, D, SW), lambda i: (nchunk - 1 - i, 0, 0, 0)), pl.BlockSpec((C, GDN_V_W), rev)],
        out_specs=[pl.BlockSpec((C, GDN_QK_W), rev), pl.BlockSpec((C, GDN_QK_W), rev), pl.BlockSpec((C, GDN_V_W), rev),
                   pl.BlockSpec((C, 128), rev), pl.BlockSpec((C, 128), rev)],
        out_shape=[jax.ShapeDtypeStruct((lp, GDN_QK_W), f32)] * 2 + [jax.ShapeDtypeStruct((lp, GDN_V_W), f32)]
        + [jax.ShapeDtypeStruct((lp, 128), f32)] * 2,
        scratch_shapes=[pltpu.VMEM((NG, D, SW), f32), pltpu.VMEM((C, 128), f32), pltpu.VMEM((128, 128), f32),
                        pltpu.VMEM((C, 128), f32), pltpu.VMEM((128, 128), f32), pltpu.VMEM((C, 128), f32)],
        compiler_params=pltpu.CompilerParams(dimension_semantics=("arbitrary",)),
    )(qn, kn, v, beta, g, states, do)


def _att_mask_t(k0, q0, tk, tq):
    kcol = k0 + lax.broadcasted_iota(jnp.int32, (tk, tq), 0)
    qrow = q0 + lax.broadcasted_iota(jnp.int32, (tk, tq), 1)
    return jnp.logical_and(qrow >= kcol, kcol >= PAD_FRONT)


def _att_tiles(lp):
    tq = _pick(lp, (768, 512, 256))
    return tq, _pick(tq, ATT_KEY_TILES)


def _attention_fwd(q, kvu, kr, *, name):
    lp = q.shape[0]
    H = MLA_HEADS
    tq, tk = _att_tiles(lp)
    r = tq // tk

    def body(q_ref, kn_ref, kr_ref, v_ref, o_ref, lse_ref, m_s, l_s, acc_s, sa_s, sb_s):
        qi = pl.program_id(1)
        m_s[...] = jnp.full_like(m_s, NEG)
        l_s[...] = jnp.zeros_like(l_s)
        acc_s[...] = jnp.zeros_like(acc_s)

        def scores(ki):
            k0 = pl.multiple_of(ki * tk, tk)
            k = jnp.concatenate([kn_ref[pl.ds(k0, tk), :], kr_ref[pl.ds(k0, tk), :]], axis=1)
            return lax.dot_general(k, q_ref[...], (NT, ((), ())), preferred_element_type=f32)

        def consume(st, ki, masked):
            k0 = pl.multiple_of(ki * tk, tk)
            if masked:
                st = jnp.where(_att_mask_t(k0, qi * tq, tk, tq), st, NEG)
            m_prev = m_s[...]
            m_new = jnp.maximum(m_prev, jnp.max(st, axis=0, keepdims=True))
            alpha = jnp.exp(m_prev - m_new)
            p = jnp.exp(st - m_new)
            l_s[...] = alpha * l_s[...] + jnp.sum(p, axis=0, keepdims=True)
            acc_s[...] = alpha * acc_s[...] + lax.dot_general(v_ref[pl.ds(k0, tk), :], p.astype(bf16), (TN, ((), ())),
                                                              preferred_element_type=f32)
            m_s[...] = m_new

        n_full = qi * r

        def chain(blocks):
            bufs = (sa_s, sb_s)
            for j, (ki, masked) in enumerate(blocks):
                if j + 1 < len(blocks):
                    bufs[(j + 1) % 2][...] = scores(blocks[j + 1][0])
                consume(bufs[j % 2][...], ki, masked)

        diagonal = [(n_full + d, True) for d in range(r)]

        @pl.when(qi == 0)
        def _():
            sa_s[...] = scores(0)
            chain(diagonal)

        @pl.when(qi > 0)
        def _():
            sb_s[...] = scores(0)
            sa_s[...] = scores(1)
            consume(sb_s[...], 0, True)
            n_pairs = (n_full - 1) // 2

            def two(pi, carry):
                ki = 1 + 2 * pi
                sb_s[...] = scores(ki + 1)
                consume(sa_s[...], ki, False)
                sa_s[...] = scores(ki + 2)
                consume(sb_s[...], ki + 1, False)
                return carry

            lax.fori_loop(0, n_pairs, two, 0)
            nxt = 1 + 2 * n_pairs

            @pl.when(nxt < n_full)
            def _():
                chain([(nxt, False)] + diagonal)

            @pl.when(nxt == n_full)
            def _():
                chain(diagonal)
        o_ref[...] = jnp.transpose(acc_s[...] / l_s[...])
        lse_ref[...] = m_s[...] + jnp.log(l_s[...])

    return pl.pallas_call(
        body, name=name, grid=(H, lp // tq),
        in_specs=[pl.BlockSpec((tq, MLA_QKP), lambda h, qi: (qi, h)),
                  pl.BlockSpec((lp, 128), lambda h, qi: (0, h)),
                  pl.BlockSpec((lp, 128), lambda h, qi: (0, 0)),
                  pl.BlockSpec((lp, 128), lambda h, qi: (0, H + h))],
        out_specs=[pl.BlockSpec((tq, 128), lambda h, qi: (qi, h)),
                   pl.BlockSpec((None, 1, tq), lambda h, qi: (h, 0, qi))],
        out_shape=[jax.ShapeDtypeStruct((lp, MLA_V_W), f32), jax.ShapeDtypeStruct((H, 1, lp), f32)],
        scratch_shapes=[pltpu.VMEM((1, tq), f32), pltpu.VMEM((1, tq), f32), pltpu.VMEM((128, tq), f32),
                        pltpu.VMEM((tk, tq), f32), pltpu.VMEM((tk, tq), f32)],
        compiler_params=pltpu.CompilerParams(dimension_semantics=("arbitrary", "arbitrary")),
    )(q, kvu, kr, kvu)


def _attention_delta(o, do, *, name):
    lp = o.shape[0]
    H = MLA_HEADS
    tq = _pick(lp, (768, 512, 256))

    def body(o_ref, do_ref, d_ref):
        prod = o_ref[...] * do_ref[...].astype(f32)
        d_ref[...] = jnp.sum(jnp.transpose(prod), axis=0, keepdims=True)

    return pl.pallas_call(
        body, name=name, grid=(H, lp // tq),
        in_specs=[pl.BlockSpec((tq, 128), lambda h, qi: (qi, h)), pl.BlockSpec((tq, 128), lambda h, qi: (qi, h))],
        out_specs=pl.BlockSpec((None, 1, tq), lambda h, qi: (h, 0, qi)),
        out_shape=jax.ShapeDtypeStruct((H, 1, lp), f32),
    )(o, do)


def _attention_bwd(q, kvu, kr, lse, delta, do, *, name):
    lp = q.shape[0]
    tq, t = _att_tiles(lp)
    H, nb = MLA_HEADS, lp // t
    r, nq = tq // t, lp // tq

    def body(q_ref, kn_ref, kr_ref, v_ref, lse_ref, dl_ref, do_ref, dq_ref, dkn_ref, dv_ref, dkr_ref, dk_s, dv_s,
             sa_s, da_s, sb_s, db_s):
        ki = pl.program_id(1)
        k0 = ki * t
        k = jnp.concatenate([kn_ref[...], kr_ref[...]], axis=1)
        vv = v_ref[...]
        dk_s[...] = jnp.zeros_like(dk_s)
        dv_s[...] = jnp.zeros_like(dv_s)

        def products(qi, s_ref, d_ref):
            q0 = pl.multiple_of(qi * tq, tq)
            s_ref[...] = lax.dot_general(k, q_ref[pl.ds(q0, tq), :], (NT, ((), ())), preferred_element_type=f32)
            d_ref[...] = lax.dot_general(vv, do_ref[pl.ds(q0, tq), :], (NT, ((), ())), preferred_element_type=f32)

        def accumulate(s_ref, d_ref, qi, masked, first):
            q0 = pl.multiple_of(qi * tq, tq)
            qv = q_ref[pl.ds(q0, tq), :]
            dob = do_ref[pl.ds(q0, tq), :]
            st = s_ref[...]
            if masked:
                st = jnp.where(_att_mask_t(k0, q0, t, tq), st, NEG)
            p = jnp.exp(st - lse_ref[:, pl.ds(q0, tq)])
            dv_s[...] += jnp.dot(p.astype(bf16), dob, preferred_element_type=f32)
            ds = (p * (d_ref[...] - dl_ref[:, pl.ds(q0, tq)])).astype(bf16)
            dk_s[...] += jnp.dot(ds, qv, preferred_element_type=f32)
            dq = lax.dot_general(ds, k, (TN, ((), ())), preferred_element_type=f32)
            if first:
                dq_ref[pl.ds(q0, tq), :] = dq
            else:
                dq_ref[pl.ds(q0, tq), :] += dq

        def sweep(qd, mask_all, first):
            last = nq - 1
            products(qd, sa_s, da_s)
            products(jnp.minimum(qd + 1, last), sb_s, db_s)
            accumulate(sa_s, da_s, qd, True, first)
            n = last - qd

            def two(pi, carry):
                i = qd + 1 + 2 * pi
                products(i + 1, sa_s, da_s)
                accumulate(sb_s, db_s, i, mask_all, first)
                products(jnp.minimum(i + 2, last), sb_s, db_s)
                accumulate(sa_s, da_s, i + 1, mask_all, first)
                return carry

            lax.fori_loop(0, n // 2, two, 0)

            @pl.when(n % 2 == 1)
            def _():
                accumulate(sb_s, db_s, last, mask_all, first)

        @pl.when(ki == 0)
        def _():
            sweep(0, True, True)

        @pl.when(ki > 0)
        def _():
            sweep(ki // r, False, False)

        dkn_ref[...] = dk_s[:, :128].astype(dkn_ref.dtype)
        dkr_ref[...] = dk_s[:, 128:]
        dv_ref[...] = dv_s[...].astype(dv_ref.dtype)

    return pl.pallas_call(
        body, name=name, grid=(H, nb),
        in_specs=[pl.BlockSpec((lp, MLA_QKP), lambda h, ki: (0, h)),
                  pl.BlockSpec((t, 128), lambda h, ki: (ki, h)),
                  pl.BlockSpec((t, 128), lambda h, ki: (ki, 0)),
                  pl.BlockSpec((t, 128), lambda h, ki: (ki, H + h)),
                  pl.BlockSpec((None, 1, lp), lambda h, ki: (h, 0, 0)),
                  pl.BlockSpec((None, 1, lp), lambda h, ki: (h, 0, 0)),
                  pl.BlockSpec((lp, 128), lambda h, ki: (0, h))],
        out_specs=[pl.BlockSpec((lp, MLA_QKP), lambda h, ki: (0, h)),
                   pl.BlockSpec((t, 128), lambda h, ki: (ki, h)),
                   pl.BlockSpec((t, 128), lambda h, ki: (ki, h)),
                   pl.BlockSpec((t, 128), lambda h, ki: (ki, h))],
        out_shape=[jax.ShapeDtypeStruct((lp, H * MLA_QKP), f32), jax.ShapeDtypeStruct((lp, MLA_V_W), bf16),
                   jax.ShapeDtypeStruct((lp, MLA_V_W), bf16), jax.ShapeDtypeStruct((lp, MLA_V_W), f32)],
        scratch_shapes=[pltpu.VMEM((t, MLA_QKP), f32), pltpu.VMEM((t, 128), f32)] + [pltpu.VMEM((t, tq), f32)] * 4,
        compiler_params=pltpu.CompilerParams(dimension_semantics=("arbitrary", "arbitrary")),
    )(q, kvu, kr, kvu, lse, delta, do)


def _exchange_copies(x_ref, o_ref, send_sems, recv_sems, local_sem, gather):
    mx, my, mc = lax.axis_index("x"), lax.axis_index("y"), lax.axis_index("c")
    me = 4 * mx + 2 * my + mc
    own = pltpu.make_async_copy(x_ref if gather else x_ref.at[me], o_ref.at[me], local_sem)
    sends, arrivals = [], []
    for k in range(1, N_DEV):
        px = 1 - mx if k & 4 else mx
        py = 1 - my if k & 2 else my
        pc = 1 - mc if k & 1 else mc
        peer = 4 * px + 2 * py + pc
        sends.append(pltpu.make_async_remote_copy(
            src_ref=x_ref if gather else x_ref.at[peer], dst_ref=o_ref.at[me],
            send_sem=send_sems.at[k - 1], recv_sem=recv_sems.at[k - 1],
            device_id=(px, py, pc), device_id_type=MESH_ID))
        arrivals.append(pltpu.make_async_remote_copy(
            src_ref=o_ref.at[peer], dst_ref=o_ref.at[peer],
            send_sem=send_sems.at[k - 1], recv_sem=recv_sems.at[k - 1],
            device_id=(mx, my, mc), device_id_type=MESH_ID))
    return own, sends, arrivals


def _exchange_start(copies):
    own, sends, _ = copies
    own.start()
    for cp in sends:
        cp.start()


def _exchange_wait(copies):
    own, sends, arrivals = copies
    for cp in arrivals:
        cp.wait_recv()
    for cp in sends:
        cp.wait_send()
    own.wait()


_EXCHANGE_SCRATCH = [pltpu.SemaphoreType.DMA((N_DEV - 1,)), pltpu.SemaphoreType.DMA((N_DEV - 1,)), pltpu.SemaphoreType.DMA]


def _exchange(x, *, gather, name):
    blk = x.shape if gather else x.shape[1:]

    def body(x_ref, o_ref, send_sems, recv_sems, local_sem):
        copies = _exchange_copies(x_ref, o_ref, send_sems, recv_sems, local_sem, gather)
        _exchange_start(copies)
        _exchange_wait(copies)

    return pl.pallas_call(
        body, name=name,
        in_specs=[pl.BlockSpec(memory_space=pltpu.HBM)], out_specs=pl.BlockSpec(memory_space=pltpu.HBM),
        out_shape=jax.ShapeDtypeStruct((N_DEV,) + tuple(blk), x.dtype),
        scratch_shapes=list(_EXCHANGE_SCRATCH),
    )(x)


def _reduce_adamw(parts, w, m, v, *, name):
    r = w.shape[0]
    tr = _pick(r, (1376, 1280, 1024, 832, 512, 256, 128, 64, 48, 32, 16, 8))

    def body(p_ref, w_ref, m_ref, v_ref, g_ref, d_ref, nm_ref, nv_ref):
        g = p_ref[0].astype(f32)
        for s in range(1, N_DEV):
            g = g + p_ref[s].astype(f32)
        mm = ADAM_B1 * m_ref[...] + (1.0 - ADAM_B1) * g
        vv = ADAM_B2 * v_ref[...] + (1.0 - ADAM_B2) * (g * g)
        m_hat = mm / (1.0 - ADAM_B1 ** ADAM_STEP)
        v_hat = vv / (1.0 - ADAM_B2 ** ADAM_STEP)
        g_ref[...] = g
        d_ref[...] = -ADAM_LR * (m_hat / (jnp.sqrt(v_hat) + ADAM_EPS) + ADAM_WD * w_ref[...])
        nm_ref[...] = mm
        nv_ref[...] = vv

    spec = pl.BlockSpec((tr, 128), lambda i: (i, 0))
    return pl.pallas_call(
        body, name=name, grid=(r // tr,),
        in_specs=[pl.BlockSpec((N_DEV, tr, 128), lambda i: (0, i, 0)), spec, spec, spec],
        out_specs=[spec] * 4, out_shape=[jax.ShapeDtypeStruct((r, 128), f32)] * 4,
    )(parts, w, m, v)


_SHARDED = ("gdn_w_in", "gdn_w_out", "kv_w_down", "kv_w_up", "mla_w_in", "mla_w_q_up", "mla_w_out", "meta_tokens", "gdn_conv_w")
_COL_SHARDED = {"gdn_w_in", "kv_w_up", "mla_w_in", "mla_w_q_up", "meta_tokens", "gdn_conv_w"}
_GATHER_FIRST = ("gdn_w_in",)
_GATHER_F32 = ("meta_tokens", "gdn_conv_w")
_GATHER_REST = ("gdn_w_out", "kv_w_down", "kv_w_up", "mla_w_in", "mla_w_q_up", "mla_w_out")
_SCATTER_EARLY = ("kv_w_down", "kv_w_up", "mla_w_in", "mla_w_q_up", "mla_w_out")
_SCATTER_LATE = ("gdn_w_in", "gdn_w_out", "meta_tokens", "gdn_conv_w")
_REPLICATED = ("pre_norm", "post_norm", "gdn_a_log", "gdn_dt_bias", "gdn_out_norm", "kv_norm", "kv_latent_norm",
               "mla_q_latent_norm")


def _rows128(a):
    flat = a.reshape(-1)
    pad = (-flat.shape[0]) % 128
    if pad:
        flat = jnp.pad(flat, (0, pad))
    return flat.reshape(-1, 128)


def _pack(arrs, row_multiple):
    parts = [_rows128(a) for a in arrs]
    buf = jnp.concatenate(parts, axis=0)
    pad = (-buf.shape[0]) % row_multiple
    if pad:
        buf = jnp.pad(buf, ((0, pad), (0, 0)))
    return buf


def _unpack(buf, shapes):
    out, r = [], 0
    for shp in shapes:
        n = math.prod(shp)
        rows = -(-n // 128)
        out.append(buf[r:r + rows].reshape(-1)[:n].reshape(shp))
        r += rows
    return out


def _unshard(g, full_shape, col):
    if col:
        return jnp.transpose(g, (1, 0, 2)).reshape(full_shape)
    return g.reshape(full_shape)


def _to_shards(a, col):
    r, c = a.shape
    if col:
        return jnp.transpose(a.reshape(r, N_DEV, c // N_DEV), (1, 0, 2))
    return a.reshape(N_DEV, r // N_DEV, c)


def _pad_cols(a, width):
    return jnp.pad(a, ((0, 0), (0, width - a.shape[1])))


def _rope_tables(lp):
    inv = ROPE_THETA ** (-jnp.arange(0, MLA_ROPE, 2, dtype=f32) / MLA_ROPE)
    pos = (jnp.arange(lp, dtype=jnp.int32) - PAD_FRONT).astype(f32)
    ang = pos[:, None] * inv[None, :]
    cos, sin = jnp.cos(ang), jnp.sin(ang)
    z = jnp.zeros((lp, 64), f32)
    return jnp.concatenate([cos, cos, z], axis=1), jnp.concatenate([-sin, sin, z], axis=1)


def kernel(x, meta_tokens, pre_norm, post_norm, gdn_w_in, gdn_conv_w, gdn_a_log, gdn_dt_bias, gdn_out_norm, gdn_w_out, kv_norm, kv_w_down, kv_latent_norm, kv_w_up, mla_w_in, mla_q_latent_norm, mla_w_q_up, mla_w_out, loss_target, m_meta_tokens, m_pre_norm, m_post_norm, m_gdn_w_in, m_gdn_conv_w, m_gdn_a_log, m_gdn_dt_bias, m_gdn_out_norm, m_gdn_w_out, m_kv_norm, m_kv_w_down, m_kv_latent_norm, m_kv_w_up, m_mla_w_in, m_mla_q_latent_norm, m_mla_w_q_up, m_mla_w_out, v_meta_tokens, v_pre_norm, v_post_norm, v_gdn_w_in, v_gdn_conv_w, v_gdn_a_log, v_gdn_dt_bias, v_gdn_out_norm, v_gdn_w_out, v_kv_norm, v_kv_w_down, v_kv_latent_norm, v_kv_w_up, v_mla_w_in, v_mla_q_latent_norm, v_mla_w_q_up, v_mla_w_out):
    W = dict(meta_tokens=meta_tokens, pre_norm=pre_norm, post_norm=post_norm, gdn_w_in=gdn_w_in, gdn_conv_w=gdn_conv_w,
             gdn_a_log=gdn_a_log, gdn_dt_bias=gdn_dt_bias, gdn_out_norm=gdn_out_norm, gdn_w_out=gdn_w_out, kv_norm=kv_norm,
             kv_w_down=kv_w_down, kv_latent_norm=kv_latent_norm, kv_w_up=kv_w_up, mla_w_in=mla_w_in,
             mla_q_latent_norm=mla_q_latent_norm, mla_w_q_up=mla_w_q_up, mla_w_out=mla_w_out)
    M = dict(meta_tokens=m_meta_tokens, pre_norm=m_pre_norm, post_norm=m_post_norm, gdn_w_in=m_gdn_w_in, gdn_conv_w=m_gdn_conv_w,
             gdn_a_log=m_gdn_a_log, gdn_dt_bias=m_gdn_dt_bias, gdn_out_norm=m_gdn_out_norm, gdn_w_out=m_gdn_w_out, kv_norm=m_kv_norm,
             kv_w_down=m_kv_w_down, kv_latent_norm=m_kv_latent_norm, kv_w_up=m_kv_w_up, mla_w_in=m_mla_w_in,
             mla_q_latent_norm=m_mla_q_latent_norm, mla_w_q_up=m_mla_w_q_up, mla_w_out=m_mla_w_out)
    V = dict(meta_tokens=v_meta_tokens, pre_norm=v_pre_norm, post_norm=v_post_norm, gdn_w_in=v_gdn_w_in, gdn_conv_w=v_gdn_conv_w,
             gdn_a_log=v_gdn_a_log, gdn_dt_bias=v_gdn_dt_bias, gdn_out_norm=v_gdn_out_norm, gdn_w_out=v_gdn_w_out, kv_norm=v_kv_norm,
             kv_w_down=v_kv_w_down, kv_latent_norm=v_kv_latent_norm, kv_w_up=v_kv_w_up, mla_w_in=v_mla_w_in,
             mla_q_latent_norm=v_mla_q_latent_norm, mla_w_q_up=v_mla_w_q_up, mla_w_out=v_mla_w_out)
    order = list(W)

    n_tok = x.shape[1]
    assert n_tok % GDN_CHUNK == 0
    n_real = ROW0 + n_tok
    lp = -(-n_real // ROW_TILE) * ROW_TILE
    n_real_chunks = n_real // GDN_CHUNK

    shard2d = {n: W[n].reshape(W[n].shape[-2:]) for n in _SHARDED}
    full_shape = {n: ((s.shape[0], s.shape[1] * N_DEV) if n in _COL_SHARDED else (s.shape[0] * N_DEV, s.shape[1]))
                  for n, s in shard2d.items()}
    full = {}

    def unpack_gathered(names, buf):
        r = 0
        for n in names:
            shp = shard2d[n].shape
            rows = math.prod(shp) // 128
            blocks = buf[:, r:r + rows].reshape((N_DEV,) + shp)
            full[n] = _unshard(blocks, full_shape[n], n in _COL_SHARDED)
            r += rows

    unpack_gathered(_GATHER_FIRST, _exchange(_pack([shard2d[n].astype(bf16) for n in _GATHER_FIRST], 16), gather=True,
                                             name="gather_w_in"))
    unpack_gathered(_GATHER_F32, _exchange(_pack([shard2d[n] for n in _GATHER_F32], 8), gather=True, name="gather_meta_conv"))
    rest_shards = _pack([shard2d[n].astype(bf16) for n in _GATHER_REST], 16)

    h0 = jnp.concatenate([jnp.zeros((PAD_FRONT, D_MODEL), f32), full["meta_tokens"], x[0],
                          jnp.zeros((lp - n_real, D_MODEL), f32)], axis=0)
    tgt = jnp.concatenate([jnp.zeros((ROW0, D_MODEL), f32), loss_target[0], jnp.zeros((lp - n_real, D_MODEL), f32)], axis=0)
    cos_k, sin_k = _rope_tables(lp)
    one = jnp.ones((lp, 128), f32)
    cos_q = jnp.concatenate([one, cos_k], axis=1)
    sin_q = jnp.concatenate([jnp.zeros((lp, 128), f32), sin_k], axis=1)

    w_in = full["gdn_w_in"]
    s1 = GDN_CONV_W + GDN_V_W
    w_in_p = jnp.concatenate([w_in[:, :s1], _pad_cols(w_in[:, s1:s1 + 16], 128), _pad_cols(w_in[:, s1 + 16:], 128)], axis=1)
    pre0, pre1 = pre_norm[0:1], pre_norm[1:2]
    post0, post1 = post_norm[0:1], post_norm[1:2]
    (hn0,) = _rowwise(_st_prenorm, [(h0, None, 0)], [], [pre0], [(D_MODEL, bf16, None)], name="f_prenorm0")
    proj, g_rest = _matmul(hn0, w_in_p, name="f_gdn_in", side=(rest_shards, True))
    unpack_gathered(_GATHER_REST, g_rest)

    wd = full["kv_w_down"]
    zc = jnp.zeros((D_MODEL, 64), bf16)
    wd2 = jnp.concatenate([wd, zc, jnp.zeros((D_MODEL, 128), bf16), wd[:, 160:192], wd[:, 128:160], zc], axis=1)
    wup_p = jnp.transpose(full["kv_w_up"].reshape(MLA_KV_RANK, MLA_HEADS, 2, 128), (0, 2, 1, 3)).reshape(MLA_KV_RANK, 2 * MLA_V_W)
    wq = full["mla_w_q_up"].reshape(MLA_Q_RANK, MLA_HEADS, MLA_QK)
    zq64 = jnp.zeros((MLA_Q_RANK, MLA_HEADS, 64), bf16)
    wq_plain = jnp.concatenate([wq, zq64], axis=2).reshape(MLA_Q_RANK, MLA_HEADS * MLA_QKP)
    wq_swap = jnp.concatenate([jnp.zeros((MLA_Q_RANK, MLA_HEADS, 128), bf16), wq[:, :, 160:192], wq[:, :, 128:160], zq64],
                              axis=2).reshape(MLA_Q_RANK, MLA_HEADS * MLA_QKP)
    q_half = Q_GROUP * MLA_QKP
    wq2 = jnp.concatenate([wq_plain[:, :q_half], wq_swap[:, :q_half], wq_plain[:, q_half:], wq_swap[:, q_half:]], axis=1)
    w_mla_in, w_gdn_out, w_mla_out = full["mla_w_in"], full["gdn_w_out"], full["mla_w_out"]
    conv_w = full["gdn_conv_w"]
    alog_p, dtb_p = _pad_cols(gdn_a_log, 128), _pad_cols(gdn_dt_bias, 128)
    kvn, kvln = kv_norm.reshape(1, -1), kv_latent_norm.reshape(1, -1)

    conv = _conv_fwd(proj, conv_w, col_blocks=GDN_CONV_W // CONV_BC, name="f_conv")
    (qn,) = _rowwise(_st_gdn_q, [(conv, GDN_QK_W, 0)], [], [], [(GDN_QK_W, f32, GDN_QK_W)], heads=GDN_QK_HEADS, name="f_gdn_q")
    (kn,) = _rowwise(_st_gdn_k, [(conv, GDN_QK_W, 1)], [], [], [(GDN_QK_W, f32, GDN_QK_W)], heads=GDN_QK_HEADS, name="f_gdn_k")
    (vv,) = _rowwise(_st_gdn_v, [(conv, GDN_V_W, 1)], [], [], [(GDN_V_W, f32, GDN_V_W)], name="f_gdn_v")
    gate_rows = [(proj, 128, s1 // 128), (proj, 128, s1 // 128 + 1)]
    beta, gdec = _rowwise(_st_gdn_gate, gate_rows, [], [alog_p, dtb_p], [(128, f32, None)] * 2, name="f_gdn_gate")
    o_gdn, states = _gdn_fwd(qn, kn, vv, beta, gdec, n_real_chunks=n_real_chunks, name="f_gdn")
    out_rows = [(o_gdn, GDN_V_W, 0), (proj, GDN_V_W, GDN_CONV_W // GDN_V_W)]
    (og,) = _rowwise(_st_gdn_out, out_rows, [], [gdn_out_norm], [(GDN_V_W, bf16, GDN_V_W)], heads=GDN_V_HEADS, name="f_gdn_out")
    y0 = _matmul(og, w_gdn_out, name="f_gdn_wout")
    mid_rows = [(h0, None, 0), (y0, None, 0)]
    h1, hn1, hkv = _rowwise(_st_mid, mid_rows, [], [post0, pre1, kvn],
                            [(D_MODEL, f32, None), (D_MODEL, bf16, None), (D_MODEL, bf16, None)], name="f_mid")
    ckr = _matmul(hkv, wd2, name="f_kv_down")
    proj2 = _matmul(hn1, w_mla_in, name="f_mla_in")
    lat_rows = [(ckr, None, 0), (proj2, MLA_Q_RANK, 0)]
    lat_nd = [(cos_k, None, 0), (sin_k, None, 0)]
    c_kv, k_rope, c_q = _rowwise(_st_latent, lat_rows, lat_nd, [kvln, mla_q_latent_norm],
                                 [(128, bf16, None), (128, bf16, None), (MLA_Q_RANK, bf16, None)], name="f_latent")
    kvu = _matmul(c_kv, wup_p, out_dtype=bf16, name="f_kv_up")
    qq = _matmul(c_q, wq2, out_dtype=bf16, name="f_q_up")
    q_nd = [(cos_q, None, 0), (sin_q, None, 0)]
    (q_att,) = _rowwise(_st_q_rope, [(qq, 2 * q_half, 0)], q_nd, [], [(2 * q_half, bf16, q_half)], ncol=2, name="f_q_rope")
    o_att, lse = _attention_fwd(q_att, kvu, k_rope, name="f_attention")
    gate2_rows = [(o_att, MLA_Q_RANK, 0), (proj2, MLA_Q_RANK, 1)]
    n_gate = MLA_V_W // MLA_Q_RANK
    (og2,) = _rowwise(_st_gate, gate2_rows, [], [], [(MLA_V_W, bf16, MLA_Q_RANK)], ncol=n_gate, name="f_mla_gate")
    y1 = _matmul(og2, w_mla_out, name="f_mla_wout")
    st_loss = _make_st_loss(n_tok)
    loss_rows_in = [(h1, None, 0), (y1, None, 0)]
    (loss_rows,) = _rowwise(st_loss, loss_rows_in, [(tgt, None, 0)], [post1], [(1, f32, None)], name="f_loss")
    loss = lax.psum(jnp.sum(loss_rows), ("x", "y", "c"))

    ones_ct = jnp.ones((lp, 1), f32)
    (dh1_a, dy1), (dpost1,) = _rowwise_vjp(st_loss, loss_rows_in, [(tgt, None, 0)], [post1], [(ones_ct, None, 0)], name="b_loss")
    dog2 = _matmul(dy1, w_mla_out, tb=True, name="b_mla_wout_x")
    dw_mla_out = _matmul(og2, dy1, ta=True, name="b_mla_wout_w")
    (do_att, dz2), _ = _rowwise_vjp(_st_gate, gate2_rows, [], [], [(dog2, MLA_Q_RANK, 0)], ncol=n_gate, name="b_mla_gate",
                                    grad_dtypes=[bf16, bf16])
    delta = _attention_delta(o_att, do_att, name="b_attention_delta")
    dq_att, dkn, dvv, dkr_h = _attention_bwd(q_att, kvu, k_rope, lse, delta, do_att, name="b_attention")
    (dqq,) = _rowwise(_st_q_rope_t, [(dq_att, q_half, 0)], q_nd, [], [(4 * q_half, bf16, 2 * q_half)], ncol=2, name="b_q_rope")
    dc_q = _matmul(dqq, wq2, tb=True, name="b_q_up_x")
    dwq2 = _matmul(c_q, dqq, ta=True, name="b_q_up_w")
    dkvu = jnp.concatenate([dkn, dvv], axis=1)
    dc_kv = _matmul(dkvu, wup_p, tb=True, name="b_kv_up_x")
    dwup_p = _matmul(c_kv, dkvu, ta=True, name="b_kv_up_w")

    def lat_ct(cv):
        dkr = cv[1][:, 0:128]
        for h in range(1, MLA_HEADS):
            dkr = dkr + cv[1][:, h * 128:(h + 1) * 128]
        return [cv[0], dkr, cv[2]]

    (dckr, dcq_pre), (dkvln, dqln) = _rowwise_vjp(
        _st_latent, lat_rows, lat_nd, [kvln, mla_q_latent_norm],
        [(dc_kv, None, 0), (dkr_h, None, 0), (dc_q, None, 0)], ct_pre=lat_ct, name="b_latent", grad_dtypes=[bf16, bf16])
    dproj2 = jnp.concatenate([dcq_pre, dz2], axis=1)
    dhn1 = _matmul(dproj2, w_mla_in, tb=True, name="b_mla_in_x")
    dw_mla_in = _matmul(hn1, dproj2, ta=True, name="b_mla_in_w")
    dhkv = _matmul(dckr, wd2, tb=True, name="b_kv_down_x")
    dwd2 = _matmul(hkv, dckr, ta=True, name="b_kv_down_w")
    (dh0_a, dy0), (dpost0, dpre1, dkvn) = _rowwise_vjp(
        _st_mid, mid_rows, [], [post0, pre1, kvn], [(dh1_a, None, 0), (dhn1, None, 0), (dhkv, None, 0)], name="b_mid")
    dog = _matmul(dy0, w_gdn_out, tb=True, name="b_gdn_wout_x")
    dw_gdn_out = _matmul(og, dy0, ta=True, name="b_gdn_wout_w")
    (do_gdn, dz), (dout_norm,) = _rowwise_vjp(_st_gdn_out, out_rows, [], [gdn_out_norm], [(dog, GDN_V_W, 0)], heads=GDN_V_HEADS,
                                              name="b_gdn_out", grad_dtypes=[f32, bf16])
    dq_g, dk_g, dv_g, dbeta, dgdec = _gdn_bwd(qn, kn, vv, beta, gdec, states, do_gdn, n_real_chunks=n_real_chunks, name="b_gdn")
    (db_col, da_col), (dalog_p, ddtb_p) = _rowwise_vjp(
        _st_gdn_gate, gate_rows, [], [alog_p, dtb_p], [(dbeta, None, 0), (dgdec, None, 0)], name="b_gdn_gate",
        grad_dtypes=[bf16, bf16])
    (dconv_q,), _ = _rowwise_vjp(_st_gdn_q, [(conv, GDN_QK_W, 0)], [], [], [(dq_g, GDN_QK_W, 0)], heads=GDN_QK_HEADS, name="b_gdn_q")
    (dconv_k,), _ = _rowwise_vjp(_st_gdn_k, [(conv, GDN_QK_W, 1)], [], [], [(dk_g, GDN_QK_W, 0)], heads=GDN_QK_HEADS, name="b_gdn_k")
    (dconv_v,), _ = _rowwise_vjp(_st_gdn_v, [(conv, GDN_V_W, 1)], [], [], [(dv_g, GDN_V_W, 0)], name="b_gdn_v")
    nq_b = GDN_QK_W // CONV_BC
    dpre_q, dcw_q = _conv_bwd(dconv_q, proj, conv_w, x_off=0, w_off=0, name="b_conv_q")
    dpre_k, dcw_k = _conv_bwd(dconv_k, proj, conv_w, x_off=nq_b, w_off=nq_b, name="b_conv_k")
    dpre_v, dcw_v = _conv_bwd(dconv_v, proj, conv_w, x_off=2 * nq_b, w_off=2 * nq_b, name="b_conv_v")
    dproj = jnp.concatenate([dpre_q, dpre_k, dpre_v, dz, db_col, da_col], axis=1)
    G = {}
    G["kv_w_down"] = jnp.concatenate([dwd2[:, :128], dwd2[:, 128:160] + dwd2[:, 416:448], dwd2[:, 160:192] + dwd2[:, 384:416]], axis=1)
    G["kv_w_up"] = jnp.transpose(dwup_p.reshape(MLA_KV_RANK, 2, MLA_HEADS, 128), (0, 2, 1, 3)).reshape(MLA_KV_RANK, 2 * MLA_V_W)
    G["mla_w_in"] = dw_mla_in
    dq4 = dwq2.reshape(MLA_Q_RANK, 2, 2, Q_GROUP, MLA_QKP)
    dqp = dq4[:, :, 0].reshape(MLA_Q_RANK, MLA_HEADS, MLA_QKP)
    dqs = dq4[:, :, 1].reshape(MLA_Q_RANK, MLA_HEADS, MLA_QKP)
    G["mla_w_q_up"] = jnp.concatenate([dqp[:, :, :128], dqp[:, :, 128:160] + dqs[:, :, 160:192],
                                       dqp[:, :, 160:192] + dqs[:, :, 128:160]], axis=2).reshape(MLA_Q_RANK, MLA_HEADS * MLA_QK)
    G["mla_w_out"] = dw_mla_out

    def shards_to_send(names):
        return jnp.concatenate([_to_shards(G[n], n in _COL_SHARDED).reshape(N_DEV, -1, 128).astype(bf16) for n in names], axis=1)

    dhn0, parts_early = _matmul(dproj, w_in_p, tb=True, name="b_gdn_in_x", side=(shards_to_send(_SCATTER_EARLY), False))
    dw_in_p = _matmul(hn0, dproj, ta=True, name="b_gdn_in_w")
    (dh0,), (dpre0,) = _rowwise_vjp(_st_prenorm, [(h0, None, 0)], [], [pre0], [(dhn0, None, 0)], extra=[dh0_a], name="b_prenorm0")

    grad_x = dh0[ROW0:n_real][None]
    G["meta_tokens"] = dh0[PAD_FRONT:ROW0]
    G["gdn_w_in"] = jnp.concatenate([dw_in_p[:, :s1 + 16], dw_in_p[:, s1 + 128:s1 + 144]], axis=1)
    G["gdn_conv_w"] = jnp.concatenate([dcw_q, dcw_k, dcw_v], axis=1)
    G["gdn_w_out"] = dw_gdn_out
    G["pre_norm"] = jnp.concatenate([dpre0, dpre1], axis=0)
    G["post_norm"] = jnp.concatenate([dpost0, dpost1], axis=0)
    G["gdn_a_log"] = dalog_p[:, :GDN_V_HEADS]
    G["gdn_dt_bias"] = ddtb_p[:, :GDN_V_HEADS]
    G["gdn_out_norm"] = dout_norm
    G["kv_norm"] = dkvn.reshape(-1)
    G["kv_latent_norm"] = dkvln.reshape(-1)
    G["mla_q_latent_norm"] = dqln

    parts_late = _exchange(shards_to_send(_SCATTER_LATE), gather=False, name="scatter_grads")
    parts_r = _exchange(_pack([G[n] for n in _REPLICATED], 8), gather=True, name="gather_small_grads")
    outs = {}
    for names, parts, tag in ((_SCATTER_EARLY, parts_early, "attention"), (_SCATTER_LATE, parts_late, "gdn"),
                              (_REPLICATED, parts_r, "replicated")):
        w_p, m_p, v_p = (_pack([d[n] for n in names], 8) for d in (W, M, V))
        res = _reduce_adamw(parts, w_p, m_p, v_p, name="adamw_" + tag)
        for kind, buf in zip(("grad", "delta", "new_m", "new_v"), res):
            for n, a in zip(names, _unpack(buf, [W[n].shape for n in names])):
                outs[kind, n] = a
    return (loss, grad_x, *[outs[k, n] for k in ("grad", "delta", "new_m", "new_v") for n in order])
```

```python
import functools
import math

import jax
import jax.numpy as jnp
from jax import lax
from jax.experimental import pallas as pl
from jax.experimental.pallas import tpu as pltpu

f32, bf16 = jnp.float32, jnp.bfloat16
HIGHEST = lax.Precision.HIGHEST
MESH_ID = pl.DeviceIdType.MESH

N_DEV = 8
D_MODEL = 1024
N_META = 16
NORM_EPS = 1e-6
PAD_FRONT = 48
ROW0 = PAD_FRONT + N_META
GDN_QK_HEADS, GDN_V_HEADS, GDN_D = 8, 16, 128
GDN_CHUNK = 64
GDN_QK_W, GDN_V_W = GDN_QK_HEADS * GDN_D, GDN_V_HEADS * GDN_D
GDN_CONV_W = 2 * GDN_QK_W + GDN_V_W
GDN_IN_W = GDN_CONV_W + GDN_V_W + 2 * GDN_V_HEADS
GDN_IN_WP = GDN_CONV_W + GDN_V_W + 2 * 128
MLA_HEADS, MLA_NOPE, MLA_ROPE, MLA_V = 16, 128, 64, 128
MLA_Q_RANK, MLA_KV_RANK = 256, 128
MLA_QK = MLA_NOPE + MLA_ROPE
MLA_QKP = 256
MLA_V_W = MLA_HEADS * MLA_V
ROPE_THETA = 10000.0
NEG = -1e30
ROW_TILE = 256
ATT_KEY_TILES = (384, 256)

ADAM_LR, ADAM_B1, ADAM_B2, ADAM_EPS, ADAM_WD, ADAM_STEP = 0.001, 0.9, 0.999, 1e-08, 0.01, 10

NN = ((1,), (0,))
NT = ((1,), (1,))
TN = ((0,), (0,))


def _pick(dim, prefs):
    for p in prefs:
        if dim % p == 0:
            return p
    return dim


def _dlo(a, b, dims):
    return lax.dot_general(a.astype(bf16), b.astype(bf16), (dims, ((), ())), preferred_element_type=f32)


def _dhi(a, b, dims):
    return lax.dot_general(a, b, (dims, ((), ())), precision=HIGHEST, preferred_element_type=f32)


def _matmul(a, b, *, ta=False, tb=False, out_dtype=f32, name, side=None):
    assert not (ta and tb)
    if ta:
        kdim, m = a.shape
    else:
        m, kdim = a.shape
    n = b.shape[0] if tb else b.shape[1]
    assert (b.shape[1] if tb else b.shape[0]) == kdim
    tm = _pick(m, (1024, 768, 512, 384, 256, 128))
    tn = _pick(n, (1024, 768, 640, 512, 256, 128))
    tk = _pick(kdim, (1024, 768, 640, 512, 256, 128))
    nk = kdim // tk
    dims = TN if ta else (NT if tb else NN)

    grid = (m // tm, n // tn, nk)

    def product(a_ref, b_ref, o_ref, acc_ref):
        k = pl.program_id(2)

        @pl.when(k == 0)
        def _():
            acc_ref[...] = jnp.zeros_like(acc_ref)

        acc_ref[...] += lax.dot_general(a_ref[...].astype(bf16), b_ref[...].astype(bf16), (dims, ((), ())),
                                        preferred_element_type=f32)

        @pl.when(k == nk - 1)
        def _():
            o_ref[...] = acc_ref[...].astype(o_ref.dtype)

    a_spec = pl.BlockSpec((tk, tm), lambda i, j, k: (k, i)) if ta else pl.BlockSpec((tm, tk), lambda i, j, k: (i, k))
    b_spec = pl.BlockSpec((tn, tk), lambda i, j, k: (j, k)) if tb else pl.BlockSpec((tk, tn), lambda i, j, k: (k, j))
    o_spec = pl.BlockSpec((tm, tn), lambda i, j, k: (i, j))
    o_shape = jax.ShapeDtypeStruct((m, n), out_dtype)
    if side is None:
        def body(a_ref, b_ref, o_ref, acc_ref):
            product(a_ref, b_ref, o_ref, acc_ref)

        return pl.pallas_call(
            body, name=name, grid=grid, in_specs=[a_spec, b_spec], out_specs=o_spec, out_shape=o_shape,
            scratch_shapes=[pltpu.VMEM((tm, tn), f32)],
            compiler_params=pltpu.CompilerParams(dimension_semantics=("parallel", "parallel", "arbitrary")),
        )(a, b)

    x, gather = side
    blk = x.shape if gather else x.shape[1:]

    def body_with_exchange(a_ref, b_ref, x_ref, o_ref, xo_ref, acc_ref, send_sems, recv_sems, local_sem):
        step = (pl.program_id(0) * grid[1] + pl.program_id(1)) * grid[2] + pl.program_id(2)
        copies = _exchange_copies(x_ref, xo_ref, send_sems, recv_sems, local_sem, gather)

        @pl.when(step == 0)
        def _():
            _exchange_start(copies)

        product(a_ref, b_ref, o_ref, acc_ref)

        @pl.when(step == grid[0] * grid[1] * grid[2] - 1)
        def _():
            _exchange_wait(copies)

    hbm = pl.BlockSpec(memory_space=pltpu.HBM)
    return pl.pallas_call(
        body_with_exchange, name=name, grid=grid, in_specs=[a_spec, b_spec, hbm], out_specs=[o_spec, hbm],
        out_shape=[o_shape, jax.ShapeDtypeStruct((N_DEV,) + tuple(blk), x.dtype)],
        scratch_shapes=[pltpu.VMEM((tm, tn), f32)] + list(_EXCHANGE_SCRATCH),
        compiler_params=pltpu.CompilerParams(dimension_semantics=("arbitrary", "arbitrary", "arbitrary")),
    )(a, b, x)


def _row_spec(item, tr):
    a, bc, off = item
    if bc is None:
        return pl.BlockSpec((tr, a.shape[1]), lambda i, j: (i, 0))
    return pl.BlockSpec((tr, bc), lambda i, j, off=off: (i, j + off))


def _param_spec(p):
    return pl.BlockSpec(p.shape, lambda i, j: (0, 0))


def _row_tile(lp, items):
    widest = max(a.shape[1] if bc is None else bc for (a, bc, _) in items)
    return ROW_TILE if widest >= 1024 else _pick(lp, (768, 512, 256))


def _head_cols(tiles, h, heads):
    return [x[:, h * (x.shape[1] // heads):(h + 1) * (x.shape[1] // heads)] for x in tiles]


def _rowwise(fn, rows, nodiff, params, outs, *, ncol=1, heads=1, name):
    lp = rows[0][0].shape[0]
    tr = _row_tile(lp, rows)
    nr, nd = len(rows), len(nodiff)

    def body(*refs):
        rv = [r[...].astype(f32) for r in refs[:nr]]
        nv = [r[...] for r in refs[nr:nr + nd]]
        pv = [r[...] for r in refs[nr + nd:nr + nd + len(params)]]
        per_head = [fn(_head_cols(rv, h, heads), nv, pv) for h in range(heads)]
        res = [jnp.concatenate(list(vals), axis=1) if heads > 1 else vals[0] for vals in zip(*per_head)]
        for ref, val in zip(refs[nr + nd + len(params):], res):
            ref[...] = val.astype(ref.dtype)

    out_specs = [pl.BlockSpec((tr, c if bc is None else bc), (lambda i, j: (i, 0)) if bc is None else (lambda i, j: (i, j)))
                 for (c, _, bc) in outs]
    return pl.pallas_call(
        body, name=name, grid=(lp // tr, ncol),
        in_specs=[_row_spec(it, tr) for it in rows + nodiff] + [_param_spec(p) for p in params],
        out_specs=out_specs,
        out_shape=[jax.ShapeDtypeStruct((lp, c), dt) for (c, dt, _) in outs],
    )(*[it[0] for it in rows + nodiff], *params)


def _rowwise_vjp(fn, rows, nodiff, params, cts, *, ncol=1, heads=1, name, ct_pre=None, extra=None, grad_dtypes=None):
    lp = rows[0][0].shape[0]
    tr = _row_tile(lp, rows)
    nr, nd, npar, nct = len(rows), len(nodiff), len(params), len(cts)
    extra = extra or [None] * nr
    grad_dtypes = grad_dtypes or [f32] * nr
    ex_items = [(e, rows[k][1], 0) for k, e in enumerate(extra) if e is not None]
    ex_pos = [k for k, e in enumerate(extra) if e is not None]
    for (a, bc, _) in rows:
        assert bc is not None or ncol == 1

    def body(*refs):
        pos = 0
        rv = [r[...].astype(f32) for r in refs[pos:pos + nr]]; pos += nr
        nv = [r[...] for r in refs[pos:pos + nd]]; pos += nd
        pv = [r[...] for r in refs[pos:pos + npar]]; pos += npar
        cv = [r[...].astype(f32) for r in refs[pos:pos + nct]]; pos += nct
        ev = [r[...].astype(f32) for r in refs[pos:pos + len(ex_items)]]; pos += len(ex_items)
        drow_refs = refs[pos:pos + nr]; pos += nr
        dpar_refs = refs[pos:pos + npar]
        ctv = ct_pre(cv) if ct_pre is not None else cv
        drow_h, dpar = [], None
        for h in range(heads):
            outs, vjp_fn = jax.vjp(lambda rr, pp: fn(rr, nv, pp), _head_cols(rv, h, heads), pv)
            dr, dp = vjp_fn([c.astype(o.dtype) for c, o in zip(_head_cols(ctv, h, heads), outs)])
            drow_h.append(dr)
            dpar = dp if dpar is None else [a + b for a, b in zip(dpar, dp)]
        drow = [jnp.concatenate(list(vals), axis=1) if heads > 1 else vals[0] for vals in zip(*drow_h)]
        for k, e in zip(ex_pos, ev):
            drow[k] = drow[k] + e
        for ref, val in zip(drow_refs, drow):
            ref[...] = val.astype(ref.dtype)
        first = jnp.logical_and(pl.program_id(0) == 0, pl.program_id(1) == 0)

        @pl.when(first)
        def _():
            for ref, val in zip(dpar_refs, dpar):
                ref[...] = val

        @pl.when(jnp.logical_not(first))
        def _():
            for ref, val in zip(dpar_refs, dpar):
                ref[...] += val

    drow_shapes, drow_specs = [], []
    for (a, bc, _), dt in zip(rows, grad_dtypes):
        if bc is None:
            drow_shapes.append(jax.ShapeDtypeStruct((lp, a.shape[1]), dt))
            drow_specs.append(pl.BlockSpec((tr, a.shape[1]), lambda i, j: (i, 0)))
        else:
            drow_shapes.append(jax.ShapeDtypeStruct((lp, ncol * bc), dt))
            drow_specs.append(pl.BlockSpec((tr, bc), lambda i, j: (i, j)))
    res = pl.pallas_call(
        body, name=name, grid=(lp // tr, ncol),
        in_specs=[_row_spec(it, tr) for it in rows + nodiff] + [_param_spec(p) for p in params]
        + [_row_spec(it, tr) for it in cts + ex_items],
        out_specs=drow_specs + [_param_spec(p) for p in params],
        out_shape=drow_shapes + [jax.ShapeDtypeStruct(p.shape, f32) for p in params],
        compiler_params=pltpu.CompilerParams(dimension_semantics=("arbitrary", "arbitrary")),
    )(*[it[0] for it in rows + nodiff], *params, *[it[0] for it in cts + ex_items])
    return res[:nr], res[nr:]


def _rms(x, g):
    return x * lax.rsqrt(jnp.mean(x * x, axis=-1, keepdims=True) + NORM_EPS) * g


def _l2n(x):
    return x * lax.rsqrt(jnp.sum(x * x, axis=-1, keepdims=True) + NORM_EPS)


def _sigmoid(x):
    return 1.0 / (1.0 + jnp.exp(-x))


def _silu(x):
    return x * _sigmoid(x)


def _softplus(x):
    return jnp.maximum(x, 0.0) + jnp.log(1.0 + jnp.exp(-jnp.abs(x)))


def _row_ids(shape):
    return pl.program_id(0) * shape[0] + lax.broadcasted_iota(jnp.int32, shape, 0)


def _st_prenorm(r, n, p):
    return [_rms(r[0], p[0])]


def _st_gdn_q(r, n, p):
    return [_l2n(_silu(r[0])) * (GDN_D ** -0.5)]


def _st_gdn_k(r, n, p):
    return [_l2n(_silu(r[0]))]


def _st_gdn_v(r, n, p):
    return [_silu(r[0])]


def _st_gdn_gate(r, n, p):
    real = _row_ids(r[0].shape) >= PAD_FRONT
    beta = jnp.where(real, _sigmoid(r[0]), 0.0)
    g = jnp.where(real, -jnp.exp(p[0]) * _softplus(r[1] + p[1]), 0.0)
    return [beta, g]


def _st_gdn_out(r, n, p):
    return [_rms(r[0], p[0]) * _silu(r[1])]


def _st_mid(r, n, p):
    h1 = r[0] + _rms(r[1], p[0])
    return [h1, _rms(h1, p[1]), _rms(h1, p[2])]


def _st_latent(r, n, p):
    ckr, cq = r
    c_kv = _rms(ckr[:, :MLA_KV_RANK], p[0])
    k_rope = ckr[:, 128:256] * n[0] + ckr[:, 384:512] * n[1]
    return [c_kv, k_rope, _rms(cq, p[1])]


Q_GROUP = 8


def _st_q_rope(r, n, p):
    half = Q_GROUP * MLA_QKP
    out = []
    for h in range(Q_GROUP):
        cols = slice(h * MLA_QKP, (h + 1) * MLA_QKP)
        out.append((r[0][:, :half][:, cols] * n[0] + r[0][:, half:][:, cols] * n[1]) * (MLA_QK ** -0.5))
    return [jnp.concatenate(out, axis=1)]


def _st_q_rope_t(r, n, p):
    plain, swapped = [], []
    for h in range(Q_GROUP):
        ct = r[0][:, h * MLA_QKP:(h + 1) * MLA_QKP] * (MLA_QK ** -0.5)
        plain.append(ct * n[0])
        swapped.append(ct * n[1])
    return [jnp.concatenate(plain + swapped, axis=1)]


def _make_st_loss(n_tokens):
    def st(r, n, p):
        h2 = r[0] + _rms(r[1], p[0])
        rows = _row_ids((r[0].shape[0], 1))
        real = jnp.logical_and(rows >= ROW0, rows < ROW0 + n_tokens)
        err = h2 - n[0]
        return [jnp.where(real, 0.5 * jnp.mean(err * err, axis=-1, keepdims=True), 0.0)]
    return st


CONV_BC = 1024


def _conv_fwd(x, w, *, col_blocks, name, tr=ROW_TILE):
    lp = x.shape[0]

    def body(x_ref, xp_ref, w_ref, o_ref):
        i = pl.program_id(0)
        prev = jnp.where(i > 0, xp_ref[...], 0.0)
        xc = jnp.concatenate([prev, x_ref[...]], axis=0)
        wv = w_ref[...]
        acc = wv[3:4, :] * x_ref[...]
        for j in range(3):
            acc = acc + wv[j:j + 1, :] * pltpu.roll(xc, 3 - j, 0)[8:, :]
        o_ref[...] = acc

    return pl.pallas_call(
        body, name=name, grid=(lp // tr, col_blocks),
        in_specs=[pl.BlockSpec((tr, CONV_BC), lambda i, j: (i, j)),
                  pl.BlockSpec((8, CONV_BC), lambda i, j: (jnp.maximum(i * (tr // 8) - 1, 0), j)),
                  pl.BlockSpec((4, CONV_BC), lambda i, j: (0, j))],
        out_specs=pl.BlockSpec((tr, CONV_BC), lambda i, j: (i, j)),
        out_shape=jax.ShapeDtypeStruct((lp, col_blocks * CONV_BC), f32),
    )(x, x, w)


def _conv_bwd(dc, x, w, *, x_off, w_off, name, tr=ROW_TILE):
    lp, width = dc.shape
    ncb, nrow = width // CONV_BC, lp // tr

    def body(dc_ref, dcn_ref, x_ref, xp_ref, w_ref, dx_ref, dw_ref):
        i = pl.program_id(1)
        nxt = jnp.where(i < nrow - 1, dcn_ref[...], 0.0)
        dcv = dc_ref[...]
        dcc = jnp.concatenate([dcv, nxt], axis=0)
        prev = jnp.where(i > 0, xp_ref[...], 0.0)
        xc = jnp.concatenate([prev, x_ref[...]], axis=0)
        wv = w_ref[...]
        dx = wv[3:4, :] * dcv
        dws = [None] * 4
        dws[3] = jnp.sum(dcv * x_ref[...], axis=0, keepdims=True)
        for j in range(3):
            dx = dx + wv[j:j + 1, :] * pltpu.roll(dcc, tr + 8 - (3 - j), 0)[:tr, :]
            dws[j] = jnp.sum(dcv * pltpu.roll(xc, 3 - j, 0)[8:, :], axis=0, keepdims=True)
        dx_ref[...] = dx.astype(dx_ref.dtype)

        @pl.when(i == 0)
        def _():
            for j in range(4):
                dw_ref[j:j + 1, :] = dws[j]

        @pl.when(i > 0)
        def _():
            for j in range(4):
                dw_ref[j:j + 1, :] += dws[j]

    last8 = lp // 8 - 1
    return pl.pallas_call(
        body, name=name, grid=(ncb, nrow),
        in_specs=[pl.BlockSpec((tr, CONV_BC), lambda j, i: (i, j)),
                  pl.BlockSpec((8, CONV_BC), lambda j, i: (jnp.minimum((i + 1) * (tr // 8), last8), j)),
                  pl.BlockSpec((tr, CONV_BC), lambda j, i: (i, j + x_off)),
                  pl.BlockSpec((8, CONV_BC), lambda j, i: (jnp.maximum(i * (tr // 8) - 1, 0), j + x_off)),
                  pl.BlockSpec((4, CONV_BC), lambda j, i: (0, j + w_off))],
        out_specs=[pl.BlockSpec((tr, CONV_BC), lambda j, i: (i, j)),
                   pl.BlockSpec((4, CONV_BC), lambda j, i: (0, j))],
        out_shape=[jax.ShapeDtypeStruct((lp, width), bf16), jax.ShapeDtypeStruct((4, width), f32)],
        compiler_params=pltpu.CompilerParams(dimension_semantics=("arbitrary", "arbitrary")),
    )(dc, dc, x, x, w)


GDN_PACK = 4
GDN_FWD_INTERLEAVE, GDN_BWD_INTERLEAVE = 4, 4


def _bd(x, cb):
    r = x.shape[0]
    tall = jnp.concatenate([x] * GDN_PACK, axis=0)
    rows = lax.broadcasted_iota(jnp.int32, tall.shape, 0) // r
    cols = lax.broadcasted_iota(jnp.int32, tall.shape, 1) // cb
    return jnp.where(rows == cols, tall, jnp.zeros_like(tall))


def _diag(full, r, cb):
    cols = lax.broadcasted_iota(jnp.int32, (r, full.shape[1]), 1) // cb
    out = jnp.where(cols == 0, full[0:r, :], 0.0)
    for a in range(1, GDN_PACK):
        out = out + jnp.where(cols == a, full[a * r:(a + 1) * r, :], 0.0)
    return out


def _stack(x, cb):
    return jnp.concatenate([x[:, a * cb:(a + 1) * cb] for a in range(GDN_PACK)], axis=0)


def _make_packed(dot):
    @jax.custom_vjp
    def pmm(x, y):
        return dot(x, _bd(y, y.shape[1] // GDN_PACK), NN)

    @jax.custom_vjp
    def pnt(x, y):
        k = x.shape[1] // GDN_PACK
        return _diag(dot(_stack(x, k), _stack(y, k), NT), x.shape[0], y.shape[0])

    @jax.custom_vjp
    def ptn(x, y):
        return _diag(dot(x, y, TN), x.shape[1] // GDN_PACK, y.shape[1] // GDN_PACK)

    def pmm_bwd(res, ct):
        x, y = res
        cb = y.shape[1] // GDN_PACK
        return dot(ct, _bd(y, cb), NT), _diag(dot(x, ct, TN), y.shape[0], cb)

    pmm.defvjp(lambda x, y: (pmm(x, y), (x, y)), pmm_bwd)
    pnt.defvjp(lambda x, y: (pnt(x, y), (x, y)), lambda res, ct: (pmm(ct, res[1]), ptn(ct, res[0])))
    ptn.defvjp(lambda x, y: (ptn(x, y), (x, y)), lambda res, ct: (pnt(res[1], ct), pmm(res[0], ct)))
    return pmm, pnt, ptn


_pmm, _pnt, _ptn = _make_packed(_dlo)


@jax.custom_vjp
def _inv_packed(ms):
    c = ms[0].shape[0]
    ii = lax.broadcasted_iota(jnp.int32, ms[0].shape, 0)
    jj = lax.broadcasted_iota(jnp.int32, ms[0].shape, 1) % c
    ts = [jnp.where(ii == jj, 1.0, 0.0) - m for m in ms]
    ps = [(-m).astype(bf16) for m in ms]
    for _ in range(int(math.log2(c)) - 1):
        ps = [_dlo(p, _bd(p, c), NN).astype(bf16) for p in ps]
        ts = [t + _dlo(t, _bd(p, c), NN) for t, p in zip(ts, ps)]
    return tuple(ts)


def _inv_packed_fwd(ms):
    ts = _inv_packed(ms)
    return ts, ts


def _inv_packed_bwd(ts, cts):
    c = ts[0].shape[0]
    ys = [_diag(_dlo(t, ct, TN), c, c) for t, ct in zip(ts, cts)]
    return (tuple(-_dlo(y, _bd(t.astype(bf16), c), NT) for y, t in zip(ys, ts)),)


_inv_packed.defvjp(_inv_packed_fwd, _inv_packed_bwd)


def _gdn_prep(q2, k2, v4, bcols, gcols, grows):
    c, d = v4.shape[0], GDN_D
    q4 = jnp.concatenate([q2[:, :d], q2[:, :d], q2[:, d:], q2[:, d:]], axis=1)
    k4 = jnp.concatenate([k2[:, :d], k2[:, :d], k2[:, d:], k2[:, d:]], axis=1)
    beta4 = jnp.concatenate([jnp.broadcast_to(b, (c, d)) for b in bcols], axis=1)
    gc4 = jnp.concatenate([jnp.broadcast_to(g, (c, d)) for g in gcols], axis=1)
    low = lax.broadcasted_iota(jnp.int32, (c, 128), 1) < c
    gi = jnp.concatenate([jnp.where(low, gcols[0], gcols[1]), jnp.where(low, gcols[2], gcols[3])], axis=1)
    gj = jnp.concatenate([jnp.where(low, grows[0], grows[1]), jnp.where(low, grows[2], grows[3])], axis=1)
    ii = lax.broadcasted_iota(jnp.int32, gi.shape, 0)
    jj = lax.broadcasted_iota(jnp.int32, gi.shape, 1) % c
    dec = jnp.exp(jnp.where(ii >= jj, gi - gj, NEG))
    rid = lax.broadcasted_iota(jnp.int32, gc4.shape, 0)
    glast = jnp.sum(jnp.where(rid == c - 1, gc4, 0.0), axis=0, keepdims=True)
    eg = jnp.exp(gc4)
    kb = k4 * beta4
    return dict(q=q4, k=k4, kb=kb, vb=v4 * beta4, kbe=kb * eg, qe=q4 * eg, dec=dec, dec_strict=jnp.where(ii > jj, dec, 0.0),
                sdecay=jnp.exp(glast), kd=k4 * jnp.exp(glast - gc4))


def _gdn_groups(groups):
    c = groups[0][3].shape[0]
    ss = [g[0] for g in groups]
    e = [_gdn_prep(*g[1:]) for g in groups]
    ts = _inv_packed(tuple(_pnt(x["kb"], x["k"]) * x["dec_strict"] for x in e))
    us = [_pmm(t, x["vb"]) for t, x in zip(ts, e)]
    ws = [_pmm(t, x["kbe"]) for t, x in zip(ts, e)]
    attns = [_pnt(x["q"], x["k"]) * x["dec"] for x in e]
    ws_qs = [_pmm(jnp.concatenate([w, x["qe"]], axis=0), s) for w, x, s in zip(ws, e, ss)]
    v_news = [u - y[:c] for u, y in zip(us, ws_qs)]
    os = [y[c:] + _pmm(a, vn) for y, a, vn in zip(ws_qs, attns, v_news)]
    s_news = [s * x["sdecay"] + _ptn(x["kd"], vn) for s, x, vn in zip(ss, e, v_news)]
    return list(zip(os, s_news))


def _lane_pick(x, h):
    lane = lax.broadcasted_iota(jnp.int32, x.shape, 1)
    return jnp.sum(jnp.where(lane == h, x, 0.0), axis=1, keepdims=True)


def _cum_log_decay(g):
    c = g.shape[0]
    lower = (lax.broadcasted_iota(jnp.int32, (c, c), 0) >= lax.broadcasted_iota(jnp.int32, (c, c), 1)).astype(f32)
    upper2 = (lax.broadcasted_iota(jnp.int32, (c, 128), 0) <= lax.broadcasted_iota(jnp.int32, (c, 128), 1) % c).astype(f32)
    return _dhi(lower, g, NN), _dhi(g, upper2, TN)


def _group_operands(gi, s_ref, q_ref, k_ref, v_ref, bv, gcv, gct_s):
    heads = [gi * GDN_PACK + u for u in range(GDN_PACK)]
    qk_off = pl.multiple_of(gi * 2 * GDN_D, 2 * GDN_D)
    v_off = pl.multiple_of(gi * GDN_PACK * GDN_D, GDN_PACK * GDN_D)
    return (s_ref[gi], q_ref[:, pl.ds(qk_off, 2 * GDN_D)], k_ref[:, pl.ds(qk_off, 2 * GDN_D)],
            v_ref[:, pl.ds(v_off, GDN_PACK * GDN_D)],
            [_lane_pick(bv, h) for h in heads], [_lane_pick(gcv, h) for h in heads],
            [gct_s[pl.ds(h, 1), :] for h in heads]), heads, qk_off, v_off


def _gdn_fwd(qn, kn, v, beta, g, *, n_real_chunks, name):
    lp = qn.shape[0]
    nchunk = lp // GDN_CHUNK
    C, D = GDN_CHUNK, GDN_D
    NG, SW = GDN_V_HEADS // GDN_PACK, GDN_PACK * GDN_D

    def body(q_ref, k_ref, v_ref, b_ref, g_ref, o_ref, st_ref, s_s, gc_s, gct_s):
        ci = pl.program_id(0)

        @pl.when(ci == 0)
        def _():
            s_s[...] = jnp.zeros_like(s_s)

        @pl.when(ci >= n_real_chunks)
        def _():
            o_ref[...] = jnp.zeros_like(o_ref)
            st_ref[...] = jnp.zeros_like(st_ref)

        @pl.when(ci < n_real_chunks)
        def _():
            gc, gct = _cum_log_decay(g_ref[...])
            gc_s[...] = gc
            gct_s[...] = gct

            def some_groups(it, carry):
                ids = [it * GDN_FWD_INTERLEAVE + u for u in range(GDN_FWD_INTERLEAVE)]
                ops = [_group_operands(gi, s_s, q_ref, k_ref, v_ref, b_ref[...], gc_s[...], gct_s) for gi in ids]
                res = _gdn_groups([op[0] for op in ops])
                for gi, op, (o, s_new) in zip(ids, ops, res):
                    st_ref[gi] = op[0][0]
                    s_s[gi] = s_new
                    o_ref[:, pl.ds(op[3], SW)] = o
                return carry

            lax.fori_loop(0, NG // GDN_FWD_INTERLEAVE, some_groups, 0)

    return pl.pallas_call(
        body, name=name, grid=(nchunk,),
        in_specs=[pl.BlockSpec((C, GDN_QK_W), lambda c: (c, 0)), pl.BlockSpec((C, GDN_QK_W), lambda c: (c, 0)),
                  pl.BlockSpec((C, GDN_V_W), lambda c: (c, 0)), pl.BlockSpec((C, 128), lambda c: (c, 0)),
                  pl.BlockSpec((C, 128), lambda c: (c, 0))],
        out_specs=[pl.BlockSpec((C, GDN_V_W), lambda c: (c, 0)),
                   pl.BlockSpec((None, NG, D, SW), lambda c: (c, 0, 0, 0))],
        out_shape=[jax.ShapeDtypeStruct((lp, GDN_V_W), f32), jax.ShapeDtypeStruct((nchunk, NG, D, SW), f32)],
        scratch_shapes=[pltpu.VMEM((NG, D, SW), f32), pltpu.VMEM((C, 128), f32), pltpu.VMEM((128, 128), f32)],
        compiler_params=pltpu.CompilerParams(dimension_semantics=("arbitrary",)),
    )(qn, kn, v, beta, g)


def _gdn_bwd(qn, kn, v, beta, g, states, do, *, n_real_chunks, name):
    lp = qn.shape[0]
    nchunk = lp // GDN_CHUNK
    C, D = GDN_CHUNK, GDN_D
    NG, SW = GDN_V_HEADS // GDN_PACK, GDN_PACK * GDN_D
    rev = lambda i: (nchunk - 1 - i, 0)

    def body(q_ref, k_ref, v_ref, b_ref, g_ref, st_ref, do_ref,
             dq_ref, dk_ref, dv_ref, db_ref, dg_ref, ds_s, gc_s, gct_s, dgc_s, dgct_s, dbeta_s):
        step = pl.program_id(0)
        ci = nchunk - 1 - step

        @pl.when(step == 0)
        def _():
            ds_s[...] = jnp.zeros_like(ds_s)

        @pl.when(ci >= n_real_chunks)
        def _():
            for r in (dq_ref, dk_ref, dv_ref, db_ref, dg_ref):
                r[...] = jnp.zeros_like(r)

        @pl.when(ci < n_real_chunks)
        def _():
            gc, gct = _cum_log_decay(g_ref[...])
            gc_s[...] = gc
            gct_s[...] = gct
            dgc_s[...] = jnp.zeros_like(dgc_s)
            dgct_s[...] = jnp.zeros_like(dgct_s)
            dbeta_s[...] = jnp.zeros_like(dbeta_s)

            def some_groups(it, carry):
                ids = [it * GDN_BWD_INTERLEAVE + u for u in range(GDN_BWD_INTERLEAVE)]
                ops = [_group_operands(gi, st_ref, q_ref, k_ref, v_ref, b_ref[...], gc_s[...], gct_s) for gi in ids]
                cts = [(do_ref[:, pl.ds(op[3], SW)], ds_s[gi]) for gi, op in zip(ids, ops)]
                _, vjp_fn = jax.vjp(_gdn_groups, [op[0] for op in ops])
                (grads,) = vjp_fn(cts)
                lane = lax.broadcasted_iota(jnp.int32, (C, 128), 1)
                dbeta_acc, dgc_acc = dbeta_s[...], dgc_s[...]
                for gi, (_, heads, qk_off, v_off), (dsp, dq2, dk2, dv4, dbcols, dgcols, dgrows) in zip(ids, ops, grads):
                    ds_s[gi] = dsp
                    dq_ref[:, pl.ds(qk_off, 2 * D)] = dq2
                    dk_ref[:, pl.ds(qk_off, 2 * D)] = dk2
                    dv_ref[:, pl.ds(v_off, SW)] = dv4
                    for h, dbcol, dgcol, dgrow in zip(heads, dbcols, dgcols, dgrows):
                        dbeta_acc = dbeta_acc + jnp.where(lane == h, dbcol, 0.0)
                        dgc_acc = dgc_acc + jnp.where(lane == h, dgcol, 0.0)
                        dgct_s[pl.ds(h, 1), :] = dgrow
                dbeta_s[...] = dbeta_acc
                dgc_s[...] = dgc_acc
                return carry

            lax.fori_loop(0, NG // GDN_BWD_INTERLEAVE, some_groups, 0)
            fold = (lax.broadcasted_iota(jnp.int32, (128, C), 0) % C == lax.broadcasted_iota(jnp.int32, (128, C), 1)).astype(f32)
            eye = (lax.broadcasted_iota(jnp.int32, (128, 128), 0) == lax.broadcasted_iota(jnp.int32, (128, 128), 1)).astype(f32)
            dgc = dgc_s[...] + _dhi(_dhi(dgct_s[...], fold, NN), eye, TN)
            upper = (lax.broadcasted_iota(jnp.int32, (C, C), 0) <= lax.broadcasted_iota(jnp.int32, (C, C), 1)).astype(f32)
            dg_ref[...] = _dhi(upper, dgc, NN)
            db_ref[...] = dbeta_s[...]

    return pl.pallas_call(
        body, name=name, grid=(nchunk,),
        in_specs=[pl.BlockSpec((C, GDN_QK_W), rev), pl.BlockSpec((C, GDN_QK_W), rev), pl.BlockSpec((C, GDN_V_W), rev),
                  pl.BlockSpec((C, 128), rev), pl.BlockSpec((C, 128), rev),
                  pl.BlockSpec((None, NG, D, SW), lambda i: (nchunk - 1 - i, 0, 0, 0)), pl.BlockSpec((C, GDN_V_W), rev)],
        out_specs=[pl.BlockSpec((C, GDN_QK_W), rev), pl.BlockSpec((C, GDN_QK_W), rev), pl.BlockSpec((C, GDN_V_W), rev),
                   pl.BlockSpec((C, 128), rev), pl.BlockSpec((C, 128), rev)],
        out_shape=[jax.ShapeDtypeStruct((lp, GDN_QK_W), f32)] * 2 + [jax.ShapeDtypeStruct((lp, GDN_V_W), f32)]
        + [jax.ShapeDtypeStruct((lp, 128), f32)] * 2,
        scratch_shapes=[pltpu.VMEM((NG, D, SW), f32), pltpu.VMEM((C, 128), f32), pltpu.VMEM((128, 128), f32),
                        pltpu.VMEM((C, 128), f32), pltpu.VMEM((128, 128), f32), pltpu.VMEM((C, 128), f32)],
        compiler_params=pltpu.CompilerParams(dimension_semantics=("arbitrary",)),
    )(qn, kn, v, beta, g, states, do)


def _att_mask_t(k0, q0, tk, tq):
    kcol = k0 + lax.broadcasted_iota(jnp.int32, (tk, tq), 0)
    qrow = q0 + lax.broadcasted_iota(jnp.int32, (tk, tq), 1)
    return jnp.logical_and(qrow >= kcol, kcol >= PAD_FRONT)


def _att_tiles(lp):
    tq = _pick(lp, (768, 512, 256))
    return tq, _pick(tq, ATT_KEY_TILES)


def _attention_fwd(q, kvu, kr, proj2, *, name):
    lp = q.shape[0]
    H = MLA_HEADS
    tq, tk = _att_tiles(lp)
    r = tq // tk

    def body(q_ref, kn_ref, kr_ref, v_ref, z_ref, o_ref, lse_ref, og_ref, m_s, l_s, acc_s, sa_s, sb_s):
        qi = pl.program_id(1)
        m_s[...] = jnp.full_like(m_s, NEG)
        l_s[...] = jnp.zeros_like(l_s)
        acc_s[...] = jnp.zeros_like(acc_s)

        def scores(ki):
            k0 = pl.multiple_of(ki * tk, tk)
            k = jnp.concatenate([kn_ref[pl.ds(k0, tk), :], kr_ref[pl.ds(k0, tk), :]], axis=1)
            return lax.dot_general(k, q_ref[...], (NT, ((), ())), preferred_element_type=f32)

        def consume(st, ki, masked):
            k0 = pl.multiple_of(ki * tk, tk)
            if masked:
                st = jnp.where(_att_mask_t(k0, qi * tq, tk, tq), st, NEG)
            m_prev = m_s[...]
            m_new = jnp.maximum(m_prev, jnp.max(st, axis=0, keepdims=True))
            alpha = jnp.exp(m_prev - m_new)
            p = jnp.exp(st - m_new)
            l_s[...] = alpha * l_s[...] + jnp.sum(p, axis=0, keepdims=True)
            acc_s[...] = alpha * acc_s[...] + lax.dot_general(v_ref[pl.ds(k0, tk), :], p.astype(bf16), (TN, ((), ())),
                                                              preferred_element_type=f32)
            m_s[...] = m_new

        n_full = qi * r

        def chain(blocks):
            bufs = (sa_s, sb_s)
            for j, (ki, masked) in enumerate(blocks):
                if j + 1 < len(blocks):
                    bufs[(j + 1) % 2][...] = scores(blocks[j + 1][0])
                consume(bufs[j % 2][...], ki, masked)

        diagonal = [(n_full + d, True) for d in range(r)]

        @pl.when(qi == 0)
        def _():
            sa_s[...] = scores(0)
            chain(diagonal)

        @pl.when(qi > 0)
        def _():
            sb_s[...] = scores(0)
            sa_s[...] = scores(1)
            consume(sb_s[...], 0, True)
            n_pairs = (n_full - 1) // 2

            def two(pi, carry):
                ki = 1 + 2 * pi
                sb_s[...] = scores(ki + 1)
                consume(sa_s[...], ki, False)
                sa_s[...] = scores(ki + 2)
                consume(sb_s[...], ki + 1, False)
                return carry

            lax.fori_loop(0, n_pairs, two, 0)
            nxt = 1 + 2 * n_pairs

            @pl.when(nxt < n_full)
            def _():
                chain([(nxt, False)] + diagonal)

            @pl.when(nxt == n_full)
            def _():
                chain(diagonal)
        o = jnp.transpose(acc_s[...] / l_s[...])
        o_ref[...] = o
        og_ref[...] = (o * _silu(z_ref[...])).astype(og_ref.dtype)
        lse_ref[...] = m_s[...] + jnp.log(l_s[...])

    return pl.pallas_call(
        body, name=name, grid=(H, lp // tq),
        in_specs=[pl.BlockSpec((tq, MLA_QKP), lambda h, qi: (qi, h)),
                  pl.BlockSpec((lp, 128), lambda h, qi: (0, h)),
                  pl.BlockSpec((lp, 128), lambda h, qi: (0, 0)),
                  pl.BlockSpec((lp, 128), lambda h, qi: (0, H + h)),
                  pl.BlockSpec((tq, 128), lambda h, qi: (qi, MLA_Q_RANK // 128 + h))],
        out_specs=[pl.BlockSpec((tq, 128), lambda h, qi: (qi, h)),
                   pl.BlockSpec((None, 1, tq), lambda h, qi: (h, 0, qi)),
                   pl.BlockSpec((tq, 128), lambda h, qi: (qi, h))],
        out_shape=[jax.ShapeDtypeStruct((lp, MLA_V_W), f32), jax.ShapeDtypeStruct((H, 1, lp), f32),
                   jax.ShapeDtypeStruct((lp, MLA_V_W), bf16)],
        scratch_shapes=[pltpu.VMEM((1, tq), f32), pltpu.VMEM((1, tq), f32), pltpu.VMEM((128, tq), f32),
                        pltpu.VMEM((tk, tq), f32), pltpu.VMEM((tk, tq), f32)],
        compiler_params=pltpu.CompilerParams(dimension_semantics=("arbitrary", "arbitrary")),
    )(q, kvu, kr, kvu, proj2)


def _gate_bwd(o, proj2, dgated, *, name):
    lp = o.shape[0]
    H, w = MLA_HEADS, 2 * MLA_V
    tq = _pick(lp, (768, 512, 256))

    def body(o_ref, z_ref, g_ref, do_ref, dz_ref, dl_ref):
        ov, z, g = o_ref[...], z_ref[...], g_ref[...]
        s = _sigmoid(z)
        do = (g * (z * s)).astype(bf16)
        do_ref[...] = do
        dz_ref[...] = (g * ov * (s * (1.0 + z * (1.0 - s)))).astype(dz_ref.dtype)
        prod = ov * do.astype(f32)
        for u in range(2):
            dl_ref[u] = jnp.sum(jnp.transpose(prod[:, u * MLA_V:(u + 1) * MLA_V]), axis=0, keepdims=True)

    blk = pl.BlockSpec((tq, w), lambda j, qi: (qi, j))
    return pl.pallas_call(
        body, name=name, grid=(H // 2, lp // tq),
        in_specs=[blk, pl.BlockSpec((tq, w), lambda j, qi: (qi, j + MLA_Q_RANK // w)), blk],
        out_specs=[blk, blk, pl.BlockSpec((2, 1, tq), lambda j, qi: (j, 0, qi))],
        out_shape=[jax.ShapeDtypeStruct((lp, MLA_V_W), bf16), jax.ShapeDtypeStruct((lp, MLA_V_W), bf16),
                   jax.ShapeDtypeStruct((H, 1, lp), f32)],
    )(o, proj2, dgated)


def _attention_bwd(q, kvu, kr, lse, delta, do, *, name):
    lp = q.shape[0]
    tq, t = _att_tiles(lp)
    H, nb = MLA_HEADS, lp // t
    r, nq = tq // t, lp // tq

    def body(q_ref, kn_ref, kr_ref, v_ref, lse_ref, dl_ref, do_ref, dq_ref, dkn_ref, dv_ref, dkr_ref, dk_s, dv_s,
             sa_s, da_s, sb_s, db_s):
        ki = pl.program_id(1)
        k0 = ki * t
        k = jnp.concatenate([kn_ref[...], kr_ref[...]], axis=1)
        vv = v_ref[...]
        dk_s[...] = jnp.zeros_like(dk_s)
        dv_s[...] = jnp.zeros_like(dv_s)

        def products(qi, s_ref, d_ref):
            q0 = pl.multiple_of(qi * tq, tq)
            s_ref[...] = lax.dot_general(k, q_ref[pl.ds(q0, tq), :], (NT, ((), ())), preferred_element_type=f32)
            d_ref[...] = lax.dot_general(vv, do_ref[pl.ds(q0, tq), :], (NT, ((), ())), preferred_element_type=f32)

        def accumulate(s_ref, d_ref, qi, masked, first):
            q0 = pl.multiple_of(qi * tq, tq)
            qv = q_ref[pl.ds(q0, tq), :]
            dob = do_ref[pl.ds(q0, tq), :]
            st = s_ref[...]
            if masked:
                st = jnp.where(_att_mask_t(k0, q0, t, tq), st, NEG)
            p = jnp.exp(st - lse_ref[:, pl.ds(q0, tq)])
            dv_s[...] += jnp.dot(p.astype(bf16), dob, preferred_element_type=f32)
            ds = (p * (d_ref[...] - dl_ref[:, pl.ds(q0, tq)])).astype(bf16)
            dk_s[...] += jnp.dot(ds, qv, preferred_element_type=f32)
            dq = lax.dot_general(ds, k, (TN, ((), ())), preferred_element_type=f32)
            if first:
                dq_ref[pl.ds(q0, tq), :] = dq
            else:
                dq_ref[pl.ds(q0, tq), :] += dq

        def sweep(qd, mask_all, first):
            last = nq - 1
            products(qd, sa_s, da_s)
            products(jnp.minimum(qd + 1, last), sb_s, db_s)
            accumulate(sa_s, da_s, qd, True, first)
            n = last - qd

            def two(pi, carry):
                i = qd + 1 + 2 * pi
                products(i + 1, sa_s, da_s)
                accumulate(sb_s, db_s, i, mask_all, first)
                products(jnp.minimum(i + 2, last), sb_s, db_s)
                accumulate(sa_s, da_s, i + 1, mask_all, first)
                return carry

            lax.fori_loop(0, n // 2, two, 0)

            @pl.when(n % 2 == 1)
            def _():
                accumulate(sb_s, db_s, last, mask_all, first)

        @pl.when(ki == 0)
        def _():
            sweep(0, True, True)

        @pl.when(ki > 0)
        def _():
            sweep(ki // r, False, False)

        dkn_ref[...] = dk_s[:, :128].astype(dkn_ref.dtype)
        dkr_ref[...] = dk_s[:, 128:]
        dv_ref[...] = dv_s[...].astype(dv_ref.dtype)

    return pl.pallas_call(
        body, name=name, grid=(H, nb),
        in_specs=[pl.BlockSpec((lp, MLA_QKP), lambda h, ki: (0, h)),
                  pl.BlockSpec((t, 128), lambda h, ki: (ki, h)),
                  pl.BlockSpec((t, 128), lambda h, ki: (ki, 0)),
                  pl.BlockSpec((t, 128), lambda h, ki: (ki, H + h)),
                  pl.BlockSpec((None, 1, lp), lambda h, ki: (h, 0, 0)),
                  pl.BlockSpec((None, 1, lp), lambda h, ki: (h, 0, 0)),
                  pl.BlockSpec((lp, 128), lambda h, ki: (0, h))],
        out_specs=[pl.BlockSpec((lp, MLA_QKP), lambda h, ki: (0, h)),
                   pl.BlockSpec((t, 128), lambda h, ki: (ki, h)),
                   pl.BlockSpec((t, 128), lambda h, ki: (ki, h)),
                   pl.BlockSpec((t, 128), lambda h, ki: (ki, h))],
        out_shape=[jax.ShapeDtypeStruct((lp, H * MLA_QKP), f32), jax.ShapeDtypeStruct((lp, MLA_V_W), bf16),
                   jax.ShapeDtypeStruct((lp, MLA_V_W), bf16), jax.ShapeDtypeStruct((lp, MLA_V_W), f32)],
        scratch_shapes=[pltpu.VMEM((t, MLA_QKP), f32), pltpu.VMEM((t, 128), f32)] + [pltpu.VMEM((t, tq), f32)] * 4,
        compiler_params=pltpu.CompilerParams(dimension_semantics=("arbitrary", "arbitrary")),
    )(q, kvu, kr, kvu, lse, delta, do)


def _exchange_copies(x_ref, o_ref, send_sems, recv_sems, local_sem, gather):
    mx, my, mc = lax.axis_index("x"), lax.axis_index("y"), lax.axis_index("c")
    me = 4 * mx + 2 * my + mc
    own = pltpu.make_async_copy(x_ref if gather else x_ref.at[me], o_ref.at[me], local_sem)
    sends, arrivals = [], []
    for k in range(1, N_DEV):
        px = 1 - mx if k & 4 else mx
        py = 1 - my if k & 2 else my
        pc = 1 - mc if k & 1 else mc
        peer = 4 * px + 2 * py + pc
        sends.append(pltpu.make_async_remote_copy(
            src_ref=x_ref if gather else x_ref.at[peer], dst_ref=o_ref.at[me],
            send_sem=send_sems.at[k - 1], recv_sem=recv_sems.at[k - 1],
            device_id=(px, py, pc), device_id_type=MESH_ID))
        arrivals.append(pltpu.make_async_remote_copy(
            src_ref=o_ref.at[peer], dst_ref=o_ref.at[peer],
            send_sem=send_sems.at[k - 1], recv_sem=recv_sems.at[k - 1],
            device_id=(mx, my, mc), device_id_type=MESH_ID))
    return own, sends, arrivals


def _exchange_start(copies):
    own, sends, _ = copies
    own.start()
    for cp in sends:
        cp.start()


def _exchange_wait(copies):
    own, sends, arrivals = copies
    for cp in arrivals:
        cp.wait_recv()
    for cp in sends:
        cp.wait_send()
    own.wait()


_EXCHANGE_SCRATCH = [pltpu.SemaphoreType.DMA((N_DEV - 1,)), pltpu.SemaphoreType.DMA((N_DEV - 1,)), pltpu.SemaphoreType.DMA]


def _exchange(x, *, gather, name):
    blk = x.shape if gather else x.shape[1:]

    def body(x_ref, o_ref, send_sems, recv_sems, local_sem):
        copies = _exchange_copies(x_ref, o_ref, send_sems, recv_sems, local_sem, gather)
        _exchange_start(copies)
        _exchange_wait(copies)

    return pl.pallas_call(
        body, name=name,
        in_specs=[pl.BlockSpec(memory_space=pltpu.HBM)], out_specs=pl.BlockSpec(memory_space=pltpu.HBM),
        out_shape=jax.ShapeDtypeStruct((N_DEV,) + tuple(blk), x.dtype),
        scratch_shapes=list(_EXCHANGE_SCRATCH),
    )(x)


def _reduce_adamw(parts, w, m, v, *, name):
    r = w.shape[0]
    tr = max(d for d in range(16, 3201, 16) if r % d == 0)

    def body(p_ref, w_ref, m_ref, v_ref, g_ref, d_ref, nm_ref, nv_ref):
        g = p_ref[0].astype(f32)
        for s in range(1, N_DEV):
            g = g + p_ref[s].astype(f32)
        mm = ADAM_B1 * m_ref[...] + (1.0 - ADAM_B1) * g
        vv = ADAM_B2 * v_ref[...] + (1.0 - ADAM_B2) * (g * g)
        m_hat = mm / (1.0 - ADAM_B1 ** ADAM_STEP)
        v_hat = vv / (1.0 - ADAM_B2 ** ADAM_STEP)
        g_ref[...] = g
        d_ref[...] = -ADAM_LR * (m_hat / (jnp.sqrt(v_hat) + ADAM_EPS) + ADAM_WD * w_ref[...])
        nm_ref[...] = mm
        nv_ref[...] = vv

    spec = pl.BlockSpec((tr, 128), lambda i: (i, 0))
    return pl.pallas_call(
        body, name=name, grid=(r // tr,),
        in_specs=[pl.BlockSpec((N_DEV, tr, 128), lambda i: (0, i, 0)), spec, spec, spec],
        out_specs=[spec] * 4, out_shape=[jax.ShapeDtypeStruct((r, 128), f32)] * 4,
    )(parts, w, m, v)


_SHARDED = ("gdn_w_in", "gdn_w_out", "kv_w_down", "kv_w_up", "mla_w_in", "mla_w_q_up", "mla_w_out", "meta_tokens", "gdn_conv_w")
_COL_SHARDED = {"gdn_w_in", "kv_w_up", "mla_w_in", "mla_w_q_up", "meta_tokens", "gdn_conv_w"}
_GATHER_FIRST = ("gdn_w_in",)
_GATHER_F32 = ("meta_tokens", "gdn_conv_w")
_GATHER_REST = ("gdn_w_out", "kv_w_down", "kv_w_up", "mla_w_in", "mla_w_q_up", "mla_w_out")
_SCATTER_EARLY = ("gdn_w_out", "kv_w_down", "kv_w_up", "mla_w_in", "mla_w_q_up", "mla_w_out")
_SCATTER_LATE = ("gdn_w_in", "meta_tokens", "gdn_conv_w")
_REPLICATED = ("pre_norm", "post_norm", "gdn_a_log", "gdn_dt_bias", "gdn_out_norm", "kv_norm", "kv_latent_norm",
               "mla_q_latent_norm")


def _rows128(a):
    flat = a.reshape(-1)
    pad = (-flat.shape[0]) % 128
    if pad:
        flat = jnp.pad(flat, (0, pad))
    return flat.reshape(-1, 128)


def _pack(arrs, row_multiple):
    parts = [_rows128(a) for a in arrs]
    buf = jnp.concatenate(parts, axis=0)
    pad = (-buf.shape[0]) % row_multiple
    if pad:
        buf = jnp.pad(buf, ((0, pad), (0, 0)))
    return buf


def _unpack(buf, shapes):
    out, r = [], 0
    for shp in shapes:
        n = math.prod(shp)
        rows = -(-n // 128)
        out.append(buf[r:r + rows].reshape(-1)[:n].reshape(shp))
        r += rows
    return out


def _unshard(g, full_shape, col):
    if col:
        return jnp.transpose(g, (1, 0, 2)).reshape(full_shape)
    return g.reshape(full_shape)


def _to_shards(a, col):
    r, c = a.shape
    if col:
        return jnp.transpose(a.reshape(r, N_DEV, c // N_DEV), (1, 0, 2))
    return a.reshape(N_DEV, r // N_DEV, c)


def _pad_cols(a, width):
    return jnp.pad(a, ((0, 0), (0, width - a.shape[1])))


def _rope_tables(lp):
    inv = ROPE_THETA ** (-jnp.arange(0, MLA_ROPE, 2, dtype=f32) / MLA_ROPE)
    pos = (jnp.arange(lp, dtype=jnp.int32) - PAD_FRONT).astype(f32)
    ang = pos[:, None] * inv[None, :]
    cos, sin = jnp.cos(ang), jnp.sin(ang)
    z = jnp.zeros((lp, 64), f32)
    return jnp.concatenate([cos, cos, z], axis=1), jnp.concatenate([-sin, sin, z], axis=1)


def kernel(x, meta_tokens, pre_norm, post_norm, gdn_w_in, gdn_conv_w, gdn_a_log, gdn_dt_bias, gdn_out_norm, gdn_w_out, kv_norm, kv_w_down, kv_latent_norm, kv_w_up, mla_w_in, mla_q_latent_norm, mla_w_q_up, mla_w_out, loss_target, m_meta_tokens, m_pre_norm, m_post_norm, m_gdn_w_in, m_gdn_conv_w, m_gdn_a_log, m_gdn_dt_bias, m_gdn_out_norm, m_gdn_w_out, m_kv_norm, m_kv_w_down, m_kv_latent_norm, m_kv_w_up, m_mla_w_in, m_mla_q_latent_norm, m_mla_w_q_up, m_mla_w_out, v_meta_tokens, v_pre_norm, v_post_norm, v_gdn_w_in, v_gdn_conv_w, v_gdn_a_log, v_gdn_dt_bias, v_gdn_out_norm, v_gdn_w_out, v_kv_norm, v_kv_w_down, v_kv_latent_norm, v_kv_w_up, v_mla_w_in, v_mla_q_latent_norm, v_mla_w_q_up, v_mla_w_out):
    W = dict(meta_tokens=meta_tokens, pre_norm=pre_norm, post_norm=post_norm, gdn_w_in=gdn_w_in, gdn_conv_w=gdn_conv_w,
             gdn_a_log=gdn_a_log, gdn_dt_bias=gdn_dt_bias, gdn_out_norm=gdn_out_norm, gdn_w_out=gdn_w_out, kv_norm=kv_norm,
             kv_w_down=kv_w_down, kv_latent_norm=kv_latent_norm, kv_w_up=kv_w_up, mla_w_in=mla_w_in,
             mla_q_latent_norm=mla_q_latent_norm, mla_w_q_up=mla_w_q_up, mla_w_out=mla_w_out)
    M = dict(meta_tokens=m_meta_tokens, pre_norm=m_pre_norm, post_norm=m_post_norm, gdn_w_in=m_gdn_w_in, gdn_conv_w=m_gdn_conv_w,
             gdn_a_log=m_gdn_a_log, gdn_dt_bias=m_gdn_dt_bias, gdn_out_norm=m_gdn_out_norm, gdn_w_out=m_gdn_w_out, kv_norm=m_kv_norm,
             kv_w_down=m_kv_w_down, kv_latent_norm=m_kv_latent_norm, kv_w_up=m_kv_w_up, mla_w_in=m_mla_w_in,
             mla_q_latent_norm=m_mla_q_latent_norm, mla_w_q_up=m_mla_w_q_up, mla_w_out=m_mla_w_out)
    V = dict(meta_tokens=v_meta_tokens, pre_norm=v_pre_norm, post_norm=v_post_norm, gdn_w_in=v_gdn_w_in, gdn_conv_w=v_gdn_conv_w,
             gdn_a_log=v_gdn_a_log, gdn_dt_bias=v_gdn_dt_bias, gdn_out_norm=v_gdn_out_norm, gdn_w_out=v_gdn_w_out, kv_norm=v_kv_norm,
             kv_w_down=v_kv_w_down, kv_latent_norm=v_kv_latent_norm, kv_w_up=v_kv_w_up, mla_w_in=v_mla_w_in,
             mla_q_latent_norm=v_mla_q_latent_norm, mla_w_q_up=v_mla_w_q_up, mla_w_out=v_mla_w_out)
    order = list(W)

    n_tok = x.shape[1]
    assert n_tok % GDN_CHUNK == 0
    n_real = ROW0 + n_tok
    lp = -(-n_real // ROW_TILE) * ROW_TILE
    n_real_chunks = n_real // GDN_CHUNK

    shard2d = {n: W[n].reshape(W[n].shape[-2:]) for n in _SHARDED}
    full_shape = {n: ((s.shape[0], s.shape[1] * N_DEV) if n in _COL_SHARDED else (s.shape[0] * N_DEV, s.shape[1]))
                  for n, s in shard2d.items()}
    full = {}

    def unpack_gathered(names, buf):
        r = 0
        for n in names:
            shp = shard2d[n].shape
            rows = math.prod(shp) // 128
            blocks = buf[:, r:r + rows].reshape((N_DEV,) + shp)
            full[n] = _unshard(blocks, full_shape[n], n in _COL_SHARDED)
            r += rows

    unpack_gathered(_GATHER_FIRST, _exchange(_pack([shard2d[n].astype(bf16) for n in _GATHER_FIRST], 16), gather=True,
                                             name="gather_w_in"))
    unpack_gathered(_GATHER_F32, _exchange(_pack([shard2d[n] for n in _GATHER_F32], 8), gather=True, name="gather_meta_conv"))
    rest_shards = _pack([shard2d[n].astype(bf16) for n in _GATHER_REST], 16)

    h0 = jnp.concatenate([jnp.zeros((PAD_FRONT, D_MODEL), f32), full["meta_tokens"], x[0],
                          jnp.zeros((lp - n_real, D_MODEL), f32)], axis=0)
    tgt = jnp.concatenate([jnp.zeros((ROW0, D_MODEL), f32), loss_target[0], jnp.zeros((lp - n_real, D_MODEL), f32)], axis=0)
    cos_k, sin_k = _rope_tables(lp)
    one = jnp.ones((lp, 128), f32)
    cos_q = jnp.concatenate([one, cos_k], axis=1)
    sin_q = jnp.concatenate([jnp.zeros((lp, 128), f32), sin_k], axis=1)

    w_in = full["gdn_w_in"]
    s1 = GDN_CONV_W + GDN_V_W
    w_in_p = jnp.concatenate([w_in[:, :s1], _pad_cols(w_in[:, s1:s1 + 16], 128), _pad_cols(w_in[:, s1 + 16:], 128)], axis=1)
    pre0, pre1 = pre_norm[0:1], pre_norm[1:2]
    post0, post1 = post_norm[0:1], post_norm[1:2]
    (hn0,) = _rowwise(_st_prenorm, [(h0, None, 0)], [], [pre0], [(D_MODEL, bf16, None)], name="f_prenorm0")
    proj, g_rest = _matmul(hn0, w_in_p, name="f_gdn_in", side=(rest_shards, True))
    unpack_gathered(_GATHER_REST, g_rest)

    wd = full["kv_w_down"]
    zc = jnp.zeros((D_MODEL, 64), bf16)
    wd2 = jnp.concatenate([wd, zc, jnp.zeros((D_MODEL, 128), bf16), wd[:, 160:192], wd[:, 128:160], zc], axis=1)
    wup_p = jnp.transpose(full["kv_w_up"].reshape(MLA_KV_RANK, MLA_HEADS, 2, 128), (0, 2, 1, 3)).reshape(MLA_KV_RANK, 2 * MLA_V_W)
    wq = full["mla_w_q_up"].reshape(MLA_Q_RANK, MLA_HEADS, MLA_QK)
    zq64 = jnp.zeros((MLA_Q_RANK, MLA_HEADS, 64), bf16)
    wq_plain = jnp.concatenate([wq, zq64], axis=2).reshape(MLA_Q_RANK, MLA_HEADS * MLA_QKP)
    wq_swap = jnp.concatenate([jnp.zeros((MLA_Q_RANK, MLA_HEADS, 128), bf16), wq[:, :, 160:192], wq[:, :, 128:160], zq64],
                              axis=2).reshape(MLA_Q_RANK, MLA_HEADS * MLA_QKP)
    q_half = Q_GROUP * MLA_QKP
    wq2 = jnp.concatenate([wq_plain[:, :q_half], wq_swap[:, :q_half], wq_plain[:, q_half:], wq_swap[:, q_half:]], axis=1)
    w_mla_in, w_gdn_out, w_mla_out = full["mla_w_in"], full["gdn_w_out"], full["mla_w_out"]
    conv_w = full["gdn_conv_w"]
    alog_p, dtb_p = _pad_cols(gdn_a_log, 128), _pad_cols(gdn_dt_bias, 128)
    kvn, kvln = kv_norm.reshape(1, -1), kv_latent_norm.reshape(1, -1)

    conv = _conv_fwd(proj, conv_w, col_blocks=GDN_CONV_W // CONV_BC, name="f_conv")
    (qn,) = _rowwise(_st_gdn_q, [(conv, GDN_QK_W, 0)], [], [], [(GDN_QK_W, f32, GDN_QK_W)], heads=GDN_QK_HEADS, name="f_gdn_q")
    (kn,) = _rowwise(_st_gdn_k, [(conv, GDN_QK_W, 1)], [], [], [(GDN_QK_W, f32, GDN_QK_W)], heads=GDN_QK_HEADS, name="f_gdn_k")
    (vv,) = _rowwise(_st_gdn_v, [(conv, GDN_V_W, 1)], [], [], [(GDN_V_W, f32, GDN_V_W)], name="f_gdn_v")
    gate_rows = [(proj, 128, s1 // 128), (proj, 128, s1 // 128 + 1)]
    beta, gdec = _rowwise(_st_gdn_gate, gate_rows, [], [alog_p, dtb_p], [(128, f32, None)] * 2, name="f_gdn_gate")
    o_gdn, states = _gdn_fwd(qn, kn, vv, beta, gdec, n_real_chunks=n_real_chunks, name="f_gdn")
    out_rows = [(o_gdn, GDN_V_W, 0), (proj, GDN_V_W, GDN_CONV_W // GDN_V_W)]
    (og,) = _rowwise(_st_gdn_out, out_rows, [], [gdn_out_norm], [(GDN_V_W, bf16, GDN_V_W)], heads=GDN_V_HEADS, name="f_gdn_out")
    y0 = _matmul(og, w_gdn_out, name="f_gdn_wout")
    mid_rows = [(h0, None, 0), (y0, None, 0)]
    h1, hn1, hkv = _rowwise(_st_mid, mid_rows, [], [post0, pre1, kvn],
                            [(D_MODEL, f32, None), (D_MODEL, bf16, None), (D_MODEL, bf16, None)], name="f_mid")
    ckr = _matmul(hkv, wd2, name="f_kv_down")
    proj2 = _matmul(hn1, w_mla_in, name="f_mla_in")
    lat_rows = [(ckr, None, 0), (proj2, MLA_Q_RANK, 0)]
    lat_nd = [(cos_k, None, 0), (sin_k, None, 0)]
    c_kv, k_rope, c_q = _rowwise(_st_latent, lat_rows, lat_nd, [kvln, mla_q_latent_norm],
                                 [(128, bf16, None), (128, bf16, None), (MLA_Q_RANK, bf16, None)], name="f_latent")
    kvu = _matmul(c_kv, wup_p, out_dtype=bf16, name="f_kv_up")
    qq = _matmul(c_q, wq2, out_dtype=bf16, name="f_q_up")
    q_nd = [(cos_q, None, 0), (sin_q, None, 0)]
    (q_att,) = _rowwise(_st_q_rope, [(qq, 2 * q_half, 0)], q_nd, [], [(2 * q_half, bf16, q_half)], ncol=2, name="f_q_rope")
    o_att, lse, og2 = _attention_fwd(q_att, kvu, k_rope, proj2, name="f_attention")
    y1 = _matmul(og2, w_mla_out, name="f_mla_wout")
    st_loss = _make_st_loss(n_tok)
    loss_rows_in = [(h1, None, 0), (y1, None, 0)]
    (loss_rows,) = _rowwise(st_loss, loss_rows_in, [(tgt, None, 0)], [post1], [(1, f32, None)], name="f_loss")
    loss = lax.psum(jnp.sum(loss_rows), ("x", "y", "c"))

    ones_ct = jnp.ones((lp, 1), f32)
    (dh1_a, dy1), (dpost1,) = _rowwise_vjp(st_loss, loss_rows_in, [(tgt, None, 0)], [post1], [(ones_ct, None, 0)], name="b_loss")
    dog2 = _matmul(dy1, w_mla_out, tb=True, name="b_mla_wout_x")
    dw_mla_out = _matmul(og2, dy1, ta=True, name="b_mla_wout_w")
    do_att, dz2, delta = _gate_bwd(o_att, proj2, dog2, name="b_mla_gate")
    dq_att, dkn, dvv, dkr_h = _attention_bwd(q_att, kvu, k_rope, lse, delta, do_att, name="b_attention")
    (dqq,) = _rowwise(_st_q_rope_t, [(dq_att, q_half, 0)], q_nd, [], [(4 * q_half, bf16, 2 * q_half)], ncol=2, name="b_q_rope")
    dc_q = _matmul(dqq, wq2, tb=True, name="b_q_up_x")
    dwq2 = _matmul(c_q, dqq, ta=True, name="b_q_up_w")
    dkvu = jnp.concatenate([dkn, dvv], axis=1)
    dc_kv = _matmul(dkvu, wup_p, tb=True, name="b_kv_up_x")
    dwup_p = _matmul(c_kv, dkvu, ta=True, name="b_kv_up_w")

    def lat_ct(cv):
        dkr = cv[1][:, 0:128]
        for h in range(1, MLA_HEADS):
            dkr = dkr + cv[1][:, h * 128:(h + 1) * 128]
        return [cv[0], dkr, cv[2]]

    (dckr, dcq_pre), (dkvln, dqln) = _rowwise_vjp(
        _st_latent, lat_rows, lat_nd, [kvln, mla_q_latent_norm],
        [(dc_kv, None, 0), (dkr_h, None, 0), (dc_q, None, 0)], ct_pre=lat_ct, name="b_latent", grad_dtypes=[bf16, bf16])
    dproj2 = jnp.concatenate([dcq_pre, dz2], axis=1)
    dhn1 = _matmul(dproj2, w_mla_in, tb=True, name="b_mla_in_x")
    dw_mla_in = _matmul(hn1, dproj2, ta=True, name="b_mla_in_w")
    dhkv = _matmul(dckr, wd2, tb=True, name="b_kv_down_x")
    dwd2 = _matmul(hkv, dckr, ta=True, name="b_kv_down_w")
    (dh0_a, dy0), (dpost0, dpre1, dkvn) = _rowwise_vjp(
        _st_mid, mid_rows, [], [post0, pre1, kvn], [(dh1_a, None, 0), (dhn1, None, 0), (dhkv, None, 0)], name="b_mid")
    dog = _matmul(dy0, w_gdn_out, tb=True, name="b_gdn_wout_x")
    dw_gdn_out = _matmul(og, dy0, ta=True, name="b_gdn_wout_w")
    (do_gdn, dz), (dout_norm,) = _rowwise_vjp(_st_gdn_out, out_rows, [], [gdn_out_norm], [(dog, GDN_V_W, 0)], heads=GDN_V_HEADS,
                                              name="b_gdn_out", grad_dtypes=[f32, bf16])
    dq_g, dk_g, dv_g, dbeta, dgdec = _gdn_bwd(qn, kn, vv, beta, gdec, states, do_gdn, n_real_chunks=n_real_chunks, name="b_gdn")
    (db_col, da_col), (dalog_p, ddtb_p) = _rowwise_vjp(
        _st_gdn_gate, gate_rows, [], [alog_p, dtb_p], [(dbeta, None, 0), (dgdec, None, 0)], name="b_gdn_gate",
        grad_dtypes=[bf16, bf16])
    (dconv_q,), _ = _rowwise_vjp(_st_gdn_q, [(conv, GDN_QK_W, 0)], [], [], [(dq_g, GDN_QK_W, 0)], heads=GDN_QK_HEADS, name="b_gdn_q")
    (dconv_k,), _ = _rowwise_vjp(_st_gdn_k, [(conv, GDN_QK_W, 1)], [], [], [(dk_g, GDN_QK_W, 0)], heads=GDN_QK_HEADS, name="b_gdn_k")
    (dconv_v,), _ = _rowwise_vjp(_st_gdn_v, [(conv, GDN_V_W, 1)], [], [], [(dv_g, GDN_V_W, 0)], name="b_gdn_v")
    nq_b = GDN_QK_W // CONV_BC
    dpre_q, dcw_q = _conv_bwd(dconv_q, proj, conv_w, x_off=0, w_off=0, name="b_conv_q")
    dpre_k, dcw_k = _conv_bwd(dconv_k, proj, conv_w, x_off=nq_b, w_off=nq_b, name="b_conv_k")
    dpre_v, dcw_v = _conv_bwd(dconv_v, proj, conv_w, x_off=2 * nq_b, w_off=2 * nq_b, name="b_conv_v")
    dproj = jnp.concatenate([dpre_q, dpre_k, dpre_v, dz, db_col, da_col], axis=1)
    G = {}
    G["kv_w_down"] = jnp.concatenate([dwd2[:, :128], dwd2[:, 128:160] + dwd2[:, 416:448], dwd2[:, 160:192] + dwd2[:, 384:416]], axis=1)
    G["kv_w_up"] = jnp.transpose(dwup_p.reshape(MLA_KV_RANK, 2, MLA_HEADS, 128), (0, 2, 1, 3)).reshape(MLA_KV_RANK, 2 * MLA_V_W)
    G["mla_w_in"] = dw_mla_in
    dq4 = dwq2.reshape(MLA_Q_RANK, 2, 2, Q_GROUP, MLA_QKP)
    dqp = dq4[:, :, 0].reshape(MLA_Q_RANK, MLA_HEADS, MLA_QKP)
    dqs = dq4[:, :, 1].reshape(MLA_Q_RANK, MLA_HEADS, MLA_QKP)
    G["mla_w_q_up"] = jnp.concatenate([dqp[:, :, :128], dqp[:, :, 128:160] + dqs[:, :, 160:192],
                                       dqp[:, :, 160:192] + dqs[:, :, 128:160]], axis=2).reshape(MLA_Q_RANK, MLA_HEADS * MLA_QK)
    G["mla_w_out"] = dw_mla_out
    G["gdn_w_out"] = dw_gdn_out

    def shards_to_send(names):
        return jnp.concatenate([_to_shards(G[n], n in _COL_SHARDED).reshape(N_DEV, -1, 128).astype(bf16) for n in names], axis=1)

    dhn0, parts_early = _matmul(dproj, w_in_p, tb=True, name="b_gdn_in_x", side=(shards_to_send(_SCATTER_EARLY), False))
    dw_in_p = _matmul(hn0, dproj, ta=True, name="b_gdn_in_w")
    (dh0,), (dpre0,) = _rowwise_vjp(_st_prenorm, [(h0, None, 0)], [], [pre0], [(dhn0, None, 0)], extra=[dh0_a], name="b_prenorm0")

    grad_x = dh0[ROW0:n_real][None]
    G["meta_tokens"] = dh0[PAD_FRONT:ROW0]
    G["gdn_w_in"] = jnp.concatenate([dw_in_p[:, :s1 + 16], dw_in_p[:, s1 + 128:s1 + 144]], axis=1)
    G["gdn_conv_w"] = jnp.concatenate([dcw_q, dcw_k, dcw_v], axis=1)
    G["pre_norm"] = jnp.concatenate([dpre0, dpre1], axis=0)
    G["post_norm"] = jnp.concatenate([dpost0, dpost1], axis=0)
    G["gdn_a_log"] = dalog_p[:, :GDN_V_HEADS]
    G["gdn_dt_bias"] = ddtb_p[:, :GDN_V_HEADS]
    G["gdn_out_norm"] = dout_norm
    G["kv_norm"] = dkvn.reshape(-1)
    G["kv_latent_norm"] = dkvln.reshape(-1)
    G["mla_q_latent_norm"] = dqln

    parts_late = _exchange(shards_to_send(_SCATTER_LATE), gather=False, name="scatter_grads")
    parts_r = _exchange(_pack([G[n] for n in _REPLICATED], 8), gather=True, name="gather_small_grads")
    outs = {}
    for names, parts, tag in ((_SCATTER_EARLY, parts_early, "early"), (_SCATTER_LATE, parts_late, "late"),
                              (_REPLICATED, parts_r, "replicated")):
        w_p, m_p, v_p = (_pack([d[n] for n in names], 8) for d in (W, M, V))
        res = _reduce_adamw(parts, w_p, m_p, v_p, name="adamw_" + tag)
        for kind, buf in zip(("grad", "delta", "new_m", "new_v"), res):
            for n, a in zip(names, _unpack(buf, [W[n].shape for n in names])):
                outs[kind, n] = a
    return (loss, grad_x, *[outs[k, n] for k in ("grad", "delta", "new_m", "new_v") for n in order])
```

```python
import functools
import math

import jax
import jax.numpy as jnp
from jax import lax
from jax.experimental import pallas as pl
from jax.experimental.pallas import tpu as pltpu

f32, bf16 = jnp.float32, jnp.bfloat16
HIGHEST = lax.Precision.HIGHEST
MESH_ID = pl.DeviceIdType.MESH

N_DEV = 8
D_MODEL = 1024
N_META = 16
NORM_EPS = 1e-6
PAD_FRONT = 48
ROW0 = PAD_FRONT + N_META
GDN_QK_HEADS, GDN_V_HEADS, GDN_D = 8, 16, 128
GDN_CHUNK = 64
GDN_QK_W, GDN_V_W = GDN_QK_HEADS * GDN_D, GDN_V_HEADS * GDN_D
GDN_CONV_W = 2 * GDN_QK_W + GDN_V_W
GDN_IN_W = GDN_CONV_W + GDN_V_W + 2 * GDN_V_HEADS
GDN_IN_WP = GDN_CONV_W + GDN_V_W + 2 * 128
MLA_HEADS, MLA_NOPE, MLA_ROPE, MLA_V = 16, 128, 64, 128
MLA_Q_RANK, MLA_KV_RANK = 256, 128
MLA_QK = MLA_NOPE + MLA_ROPE
MLA_QKP = 256
MLA_V_W = MLA_HEADS * MLA_V
ROPE_THETA = 10000.0
NEG = -1e30
ROW_TILE = 256
ATT_KEY_TILES = (384, 256)

ADAM_LR, ADAM_B1, ADAM_B2, ADAM_EPS, ADAM_WD, ADAM_STEP = 0.001, 0.9, 0.999, 1e-08, 0.01, 10

NN = ((1,), (0,))
NT = ((1,), (1,))
TN = ((0,), (0,))


def _pick(dim, prefs):
    for p in prefs:
        if dim % p == 0:
            return p
    return dim


def _dlo(a, b, dims):
    return lax.dot_general(a.astype(bf16), b.astype(bf16), (dims, ((), ())), preferred_element_type=f32)


def _dhi(a, b, dims):
    return lax.dot_general(a, b, (dims, ((), ())), precision=HIGHEST, preferred_element_type=f32)


def _matmul(a, b, *, ta=False, tb=False, out_dtype=f32, name, side=None):
    assert not (ta and tb)
    if ta:
        kdim, m = a.shape
    else:
        m, kdim = a.shape
    n = b.shape[0] if tb else b.shape[1]
    assert (b.shape[1] if tb else b.shape[0]) == kdim
    tm = _pick(m, (1024, 768, 512, 384, 256, 128))
    tn = _pick(n, (1024, 768, 640, 512, 256, 128))
    tk = _pick(kdim, (1024, 768, 640, 512, 256, 128))
    nk = kdim // tk
    dims = TN if ta else (NT if tb else NN)

    grid = (m // tm, n // tn, nk)

    def product(a_ref, b_ref, o_ref, acc_ref):
        k = pl.program_id(2)

        @pl.when(k == 0)
        def _():
            acc_ref[...] = jnp.zeros_like(acc_ref)

        acc_ref[...] += lax.dot_general(a_ref[...].astype(bf16), b_ref[...].astype(bf16), (dims, ((), ())),
                                        preferred_element_type=f32)

        @pl.when(k == nk - 1)
        def _():
            o_ref[...] = acc_ref[...].astype(o_ref.dtype)

    a_spec = pl.BlockSpec((tk, tm), lambda i, j, k: (k, i)) if ta else pl.BlockSpec((tm, tk), lambda i, j, k: (i, k))
    b_spec = pl.BlockSpec((tn, tk), lambda i, j, k: (j, k)) if tb else pl.BlockSpec((tk, tn), lambda i, j, k: (k, j))
    o_spec = pl.BlockSpec((tm, tn), lambda i, j, k: (i, j))
    o_shape = jax.ShapeDtypeStruct((m, n), out_dtype)
    if side is None:
        def body(a_ref, b_ref, o_ref, acc_ref):
            product(a_ref, b_ref, o_ref, acc_ref)

        return pl.pallas_call(
            body, name=name, grid=grid, in_specs=[a_spec, b_spec], out_specs=o_spec, out_shape=o_shape,
            scratch_shapes=[pltpu.VMEM((tm, tn), f32)],
            compiler_params=pltpu.CompilerParams(dimension_semantics=("parallel", "parallel", "arbitrary")),
        )(a, b)

    x, gather = side
    blk = x.shape if gather else x.shape[1:]

    def body_with_exchange(a_ref, b_ref, x_ref, o_ref, xo_ref, acc_ref, send_sems, recv_sems, local_sem):
        step = (pl.program_id(0) * grid[1] + pl.program_id(1)) * grid[2] + pl.program_id(2)
        copies = _exchange_copies(x_ref, xo_ref, send_sems, recv_sems, local_sem, gather)

        @pl.when(step == 0)
        def _():
            _exchange_start(copies)

        product(a_ref, b_ref, o_ref, acc_ref)

        @pl.when(step == grid[0] * grid[1] * grid[2] - 1)
        def _():
            _exchange_wait(copies)

    hbm = pl.BlockSpec(memory_space=pltpu.HBM)
    return pl.pallas_call(
        body_with_exchange, name=name, grid=grid, in_specs=[a_spec, b_spec, hbm], out_specs=[o_spec, hbm],
        out_shape=[o_shape, jax.ShapeDtypeStruct((N_DEV,) + tuple(blk), x.dtype)],
        scratch_shapes=[pltpu.VMEM((tm, tn), f32)] + list(_EXCHANGE_SCRATCH),
        compiler_params=pltpu.CompilerParams(dimension_semantics=("arbitrary", "arbitrary", "arbitrary")),
    )(a, b, x)


def _row_spec(item, tr):
    a, bc, off = item
    if bc is None:
        return pl.BlockSpec((tr, a.shape[1]), lambda i, j: (i, 0))
    return pl.BlockSpec((tr, bc), lambda i, j, off=off: (i, j + off))


def _param_spec(p):
    return pl.BlockSpec(p.shape, lambda i, j: (0, 0))


def _row_tile(lp, items):
    widest = max(a.shape[1] if bc is None else bc for (a, bc, _) in items)
    return ROW_TILE if widest >= 1024 else _pick(lp, (768, 512, 256))


def _head_cols(tiles, h, heads):
    return [x[:, h * (x.shape[1] // heads):(h + 1) * (x.shape[1] // heads)] for x in tiles]


def _rowwise(fn, rows, nodiff, params, outs, *, ncol=1, heads=1, name):
    lp = rows[0][0].shape[0]
    tr = _row_tile(lp, rows)
    nr, nd = len(rows), len(nodiff)

    def body(*refs):
        rv = [r[...].astype(f32) for r in refs[:nr]]
        nv = [r[...] for r in refs[nr:nr + nd]]
        pv = [r[...] for r in refs[nr + nd:nr + nd + len(params)]]
        per_head = [fn(_head_cols(rv, h, heads), nv, pv) for h in range(heads)]
        res = [jnp.concatenate(list(vals), axis=1) if heads > 1 else vals[0] for vals in zip(*per_head)]
        for ref, val in zip(refs[nr + nd + len(params):], res):
            ref[...] = val.astype(ref.dtype)

    out_specs = [pl.BlockSpec((tr, c if bc is None else bc), (lambda i, j: (i, 0)) if bc is None else (lambda i, j: (i, j)))
                 for (c, _, bc) in outs]
    return pl.pallas_call(
        body, name=name, grid=(lp // tr, ncol),
        in_specs=[_row_spec(it, tr) for it in rows + nodiff] + [_param_spec(p) for p in params],
        out_specs=out_specs,
        out_shape=[jax.ShapeDtypeStruct((lp, c), dt) for (c, dt, _) in outs],
    )(*[it[0] for it in rows + nodiff], *params)


def _rowwise_vjp(fn, rows, nodiff, params, cts, *, ncol=1, heads=1, name, ct_pre=None, extra=None, grad_dtypes=None):
    lp = rows[0][0].shape[0]
    tr = _row_tile(lp, rows)
    nr, nd, npar, nct = len(rows), len(nodiff), len(params), len(cts)
    extra = extra or [None] * nr
    grad_dtypes = grad_dtypes or [f32] * nr
    ex_items = [(e, rows[k][1], 0) for k, e in enumerate(extra) if e is not None]
    ex_pos = [k for k, e in enumerate(extra) if e is not None]
    for (a, bc, _) in rows:
        assert bc is not None or ncol == 1

    def body(*refs):
        pos = 0
        rv = [r[...].astype(f32) for r in refs[pos:pos + nr]]; pos += nr
        nv = [r[...] for r in refs[pos:pos + nd]]; pos += nd
        pv = [r[...] for r in refs[pos:pos + npar]]; pos += npar
        cv = [r[...].astype(f32) for r in refs[pos:pos + nct]]; pos += nct
        ev = [r[...].astype(f32) for r in refs[pos:pos + len(ex_items)]]; pos += len(ex_items)
        drow_refs = refs[pos:pos + nr]; pos += nr
        dpar_refs = refs[pos:pos + npar]
        ctv = ct_pre(cv) if ct_pre is not None else cv
        drow_h, dpar = [], None
        for h in range(heads):
            outs, vjp_fn = jax.vjp(lambda rr, pp: fn(rr, nv, pp), _head_cols(rv, h, heads), pv)
            dr, dp = vjp_fn([c.astype(o.dtype) for c, o in zip(_head_cols(ctv, h, heads), outs)])
            drow_h.append(dr)
            dpar = dp if dpar is None else [a + b for a, b in zip(dpar, dp)]
        drow = [jnp.concatenate(list(vals), axis=1) if heads > 1 else vals[0] for vals in zip(*drow_h)]
        for k, e in zip(ex_pos, ev):
            drow[k] = drow[k] + e
        for ref, val in zip(drow_refs, drow):
            ref[...] = val.astype(ref.dtype)
        first = jnp.logical_and(pl.program_id(0) == 0, pl.program_id(1) == 0)

        @pl.when(first)
        def _():
            for ref, val in zip(dpar_refs, dpar):
                ref[...] = val

        @pl.when(jnp.logical_not(first))
        def _():
            for ref, val in zip(dpar_refs, dpar):
                ref[...] += val

    drow_shapes, drow_specs = [], []
    for (a, bc, _), dt in zip(rows, grad_dtypes):
        if bc is None:
            drow_shapes.append(jax.ShapeDtypeStruct((lp, a.shape[1]), dt))
            drow_specs.append(pl.BlockSpec((tr, a.shape[1]), lambda i, j: (i, 0)))
        else:
            drow_shapes.append(jax.ShapeDtypeStruct((lp, ncol * bc), dt))
            drow_specs.append(pl.BlockSpec((tr, bc), lambda i, j: (i, j)))
    res = pl.pallas_call(
        body, name=name, grid=(lp // tr, ncol),
        in_specs=[_row_spec(it, tr) for it in rows + nodiff] + [_param_spec(p) for p in params]
        + [_row_spec(it, tr) for it in cts + ex_items],
        out_specs=drow_specs + [_param_spec(p) for p in params],
        out_shape=drow_shapes + [jax.ShapeDtypeStruct(p.shape, f32) for p in params],
        compiler_params=pltpu.CompilerParams(dimension_semantics=("arbitrary", "arbitrary")),
    )(*[it[0] for it in rows + nodiff], *params, *[it[0] for it in cts + ex_items])
    return res[:nr], res[nr:]


def _rms(x, g):
    return x * lax.rsqrt(jnp.mean(x * x, axis=-1, keepdims=True) + NORM_EPS) * g


def _l2n(x):
    return x * lax.rsqrt(jnp.sum(x * x, axis=-1, keepdims=True) + NORM_EPS)


def _sigmoid(x):
    return 1.0 / (1.0 + jnp.exp(-x))


def _silu(x):
    return x * _sigmoid(x)


def _softplus(x):
    return jnp.maximum(x, 0.0) + jnp.log(1.0 + jnp.exp(-jnp.abs(x)))


def _row_ids(shape):
    return pl.program_id(0) * shape[0] + lax.broadcasted_iota(jnp.int32, shape, 0)


def _st_prenorm(r, n, p):
    return [_rms(r[0], p[0])]


def _st_gdn_q(r, n, p):
    return [_l2n(_silu(r[0])) * (GDN_D ** -0.5)]


def _st_gdn_k(r, n, p):
    return [_l2n(_silu(r[0]))]


def _st_gdn_v(r, n, p):
    return [_silu(r[0])]


def _st_gdn_gate(r, n, p):
    real = _row_ids(r[0].shape) >= PAD_FRONT
    beta = jnp.where(real, _sigmoid(r[0]), 0.0)
    g = jnp.where(real, -jnp.exp(p[0]) * _softplus(r[1] + p[1]), 0.0)
    return [beta, g]


def _st_gdn_out(r, n, p):
    return [_rms(r[0], p[0]) * _silu(r[1])]


def _st_mid(r, n, p):
    h1 = r[0] + _rms(r[1], p[0])
    return [h1, _rms(h1, p[1]), _rms(h1, p[2])]


def _st_latent(r, n, p):
    ckr, cq = r
    c_kv = _rms(ckr[:, :MLA_KV_RANK], p[0])
    k_rope = ckr[:, 128:256] * n[0] + ckr[:, 384:512] * n[1]
    return [c_kv, k_rope, _rms(cq, p[1])]


Q_GROUP = 8


def _st_q_rope(r, n, p):
    half = Q_GROUP * MLA_QKP
    out = []
    for h in range(Q_GROUP):
        cols = slice(h * MLA_QKP, (h + 1) * MLA_QKP)
        out.append((r[0][:, :half][:, cols] * n[0] + r[0][:, half:][:, cols] * n[1]) * (MLA_QK ** -0.5))
    return [jnp.concatenate(out, axis=1)]


def _st_q_rope_t(r, n, p):
    plain, swapped = [], []
    for h in range(Q_GROUP):
        ct = r[0][:, h * MLA_QKP:(h + 1) * MLA_QKP] * (MLA_QK ** -0.5)
        plain.append(ct * n[0])
        swapped.append(ct * n[1])
    return [jnp.concatenate(plain + swapped, axis=1)]


def _make_st_loss(n_tokens):
    def st(r, n, p):
        h2 = r[0] + _rms(r[1], p[0])
        rows = _row_ids((r[0].shape[0], 1))
        real = jnp.logical_and(rows >= ROW0, rows < ROW0 + n_tokens)
        err = h2 - n[0]
        return [jnp.where(real, 0.5 * jnp.mean(err * err, axis=-1, keepdims=True), 0.0)]
    return st


CONV_BC = 1024


def _conv_fwd(x, w, *, col_blocks, name, tr=ROW_TILE):
    lp = x.shape[0]

    def body(x_ref, xp_ref, w_ref, o_ref):
        i = pl.program_id(0)
        prev = jnp.where(i > 0, xp_ref[...], 0.0)
        xc = jnp.concatenate([prev, x_ref[...]], axis=0)
        wv = w_ref[...]
        acc = wv[3:4, :] * x_ref[...]
        for j in range(3):
            acc = acc + wv[j:j + 1, :] * pltpu.roll(xc, 3 - j, 0)[8:, :]
        o_ref[...] = acc

    return pl.pallas_call(
        body, name=name, grid=(lp // tr, col_blocks),
        in_specs=[pl.BlockSpec((tr, CONV_BC), lambda i, j: (i, j)),
                  pl.BlockSpec((8, CONV_BC), lambda i, j: (jnp.maximum(i * (tr // 8) - 1, 0), j)),
                  pl.BlockSpec((4, CONV_BC), lambda i, j: (0, j))],
        out_specs=pl.BlockSpec((tr, CONV_BC), lambda i, j: (i, j)),
        out_shape=jax.ShapeDtypeStruct((lp, col_blocks * CONV_BC), f32),
    )(x, x, w)


def _conv_bwd(dc, x, w, *, x_off, w_off, name, tr=ROW_TILE):
    lp, width = dc.shape
    ncb, nrow = width // CONV_BC, lp // tr

    def body(dc_ref, dcn_ref, x_ref, xp_ref, w_ref, dx_ref, dw_ref):
        i = pl.program_id(1)
        nxt = jnp.where(i < nrow - 1, dcn_ref[...], 0.0)
        dcv = dc_ref[...]
        dcc = jnp.concatenate([dcv, nxt], axis=0)
        prev = jnp.where(i > 0, xp_ref[...], 0.0)
        xc = jnp.concatenate([prev, x_ref[...]], axis=0)
        wv = w_ref[...]
        dx = wv[3:4, :] * dcv
        dws = [None] * 4
        dws[3] = jnp.sum(dcv * x_ref[...], axis=0, keepdims=True)
        for j in range(3):
            dx = dx + wv[j:j + 1, :] * pltpu.roll(dcc, tr + 8 - (3 - j), 0)[:tr, :]
            dws[j] = jnp.sum(dcv * pltpu.roll(xc, 3 - j, 0)[8:, :], axis=0, keepdims=True)
        dx_ref[...] = dx.astype(dx_ref.dtype)

        @pl.when(i == 0)
        def _():
            for j in range(4):
                dw_ref[j:j + 1, :] = dws[j]

        @pl.when(i > 0)
        def _():
            for j in range(4):
                dw_ref[j:j + 1, :] += dws[j]

    last8 = lp // 8 - 1
    return pl.pallas_call(
        body, name=name, grid=(ncb, nrow),
        in_specs=[pl.BlockSpec((tr, CONV_BC), lambda j, i: (i, j)),
                  pl.BlockSpec((8, CONV_BC), lambda j, i: (jnp.minimum((i + 1) * (tr // 8), last8), j)),
                  pl.BlockSpec((tr, CONV_BC), lambda j, i: (i, j + x_off)),
                  pl.BlockSpec((8, CONV_BC), lambda j, i: (jnp.maximum(i * (tr // 8) - 1, 0), j + x_off)),
                  pl.BlockSpec((4, CONV_BC), lambda j, i: (0, j + w_off))],
        out_specs=[pl.BlockSpec((tr, CONV_BC), lambda j, i: (i, j)),
                   pl.BlockSpec((4, CONV_BC), lambda j, i: (0, j))],
        out_shape=[jax.ShapeDtypeStruct((lp, width), bf16), jax.ShapeDtypeStruct((4, width), f32)],
        compiler_params=pltpu.CompilerParams(dimension_semantics=("arbitrary", "arbitrary")),
    )(dc, dc, x, x, w)


GDN_PACK = 4
GDN_FWD_INTERLEAVE, GDN_BWD_INTERLEAVE = 4, 4


def _bd(x, cb):
    r = x.shape[0]
    tall = jnp.concatenate([x] * GDN_PACK, axis=0)
    rows = lax.broadcasted_iota(jnp.int32, tall.shape, 0) // r
    cols = lax.broadcasted_iota(jnp.int32, tall.shape, 1) // cb
    return jnp.where(rows == cols, tall, jnp.zeros_like(tall))


def _diag(full, r, cb):
    cols = lax.broadcasted_iota(jnp.int32, (r, full.shape[1]), 1) // cb
    out = jnp.where(cols == 0, full[0:r, :], 0.0)
    for a in range(1, GDN_PACK):
        out = out + jnp.where(cols == a, full[a * r:(a + 1) * r, :], 0.0)
    return out


def _stack(x, cb):
    return jnp.concatenate([x[:, a * cb:(a + 1) * cb] for a in range(GDN_PACK)], axis=0)


def _make_packed(dot):
    @jax.custom_vjp
    def pmm(x, y):
        return dot(x, _bd(y, y.shape[1] // GDN_PACK), NN)

    @jax.custom_vjp
    def pnt(x, y):
        k = x.shape[1] // GDN_PACK
        return _diag(dot(_stack(x, k), _stack(y, k), NT), x.shape[0], y.shape[0])

    @jax.custom_vjp
    def ptn(x, y):
        return _diag(dot(x, y, TN), x.shape[1] // GDN_PACK, y.shape[1] // GDN_PACK)

    def pmm_bwd(res, ct):
        x, y = res
        cb = y.shape[1] // GDN_PACK
        return dot(ct, _bd(y, cb), NT), _diag(dot(x, ct, TN), y.shape[0], cb)

    pmm.defvjp(lambda x, y: (pmm(x, y), (x, y)), pmm_bwd)
    pnt.defvjp(lambda x, y: (pnt(x, y), (x, y)), lambda res, ct: (pmm(ct, res[1]), ptn(ct, res[0])))
    ptn.defvjp(lambda x, y: (ptn(x, y), (x, y)), lambda res, ct: (pnt(res[1], ct), pmm(res[0], ct)))
    return pmm, pnt, ptn


_pmm, _pnt, _ptn = _make_packed(_dlo)


@jax.custom_vjp
def _inv_packed(ms):
    c = ms[0].shape[0]
    ii = lax.broadcasted_iota(jnp.int32, ms[0].shape, 0)
    jj = lax.broadcasted_iota(jnp.int32, ms[0].shape, 1) % c
    ts = [jnp.where(ii == jj, 1.0, 0.0) - m for m in ms]
    ps = [(-m).astype(bf16) for m in ms]
    for _ in range(int(math.log2(c)) - 1):
        ps = [_dlo(p, _bd(p, c), NN).astype(bf16) for p in ps]
        ts = [t + _dlo(t, _bd(p, c), NN) for t, p in zip(ts, ps)]
    return tuple(ts)


def _inv_packed_fwd(ms):
    ts = _inv_packed(ms)
    return ts, ts


def _inv_packed_bwd(ts, cts):
    c = ts[0].shape[0]
    ys = [_diag(_dlo(t, ct, TN), c, c) for t, ct in zip(ts, cts)]
    return (tuple(-_dlo(y, _bd(t.astype(bf16), c), NT) for y, t in zip(ys, ts)),)


_inv_packed.defvjp(_inv_packed_fwd, _inv_packed_bwd)


def _gdn_prep(q2, k2, v4, bcols, gcols, grows):
    c, d = v4.shape[0], GDN_D
    q4 = jnp.concatenate([q2[:, :d], q2[:, :d], q2[:, d:], q2[:, d:]], axis=1)
    k4 = jnp.concatenate([k2[:, :d], k2[:, :d], k2[:, d:], k2[:, d:]], axis=1)
    beta4 = jnp.concatenate([jnp.broadcast_to(b, (c, d)) for b in bcols], axis=1)
    gc4 = jnp.concatenate([jnp.broadcast_to(g, (c, d)) for g in gcols], axis=1)
    low = lax.broadcasted_iota(jnp.int32, (c, 128), 1) < c
    gi = jnp.concatenate([jnp.where(low, gcols[0], gcols[1]), jnp.where(low, gcols[2], gcols[3])], axis=1)
    gj = jnp.concatenate([jnp.where(low, grows[0], grows[1]), jnp.where(low, grows[2], grows[3])], axis=1)
    ii = lax.broadcasted_iota(jnp.int32, gi.shape, 0)
    jj = lax.broadcasted_iota(jnp.int32, gi.shape, 1) % c
    dec = jnp.exp(jnp.where(ii >= jj, gi - gj, NEG))
    rid = lax.broadcasted_iota(jnp.int32, gc4.shape, 0)
    glast = jnp.sum(jnp.where(rid == c - 1, gc4, 0.0), axis=0, keepdims=True)
    eg = jnp.exp(gc4)
    kb = k4 * beta4
    return dict(q=q4, k=k4, kb=kb, vb=v4 * beta4, kbe=kb * eg, qe=q4 * eg, dec=dec, dec_strict=jnp.where(ii > jj, dec, 0.0),
                sdecay=jnp.exp(glast), kd=k4 * jnp.exp(glast - gc4))


@jax.custom_vjp
def _inv_packed_known(ms, ts):
    return ts


_inv_packed_known.defvjp(lambda ms, ts: (ts, ts),
                         lambda ts, cts: (_inv_packed_bwd(ts, cts)[0], tuple(jnp.zeros_like(t) for t in ts)))


def _gdn_groups(groups, known_inverses=None, with_inverses=False):
    c = groups[0][3].shape[0]
    ss = [g[0] for g in groups]
    e = [_gdn_prep(*g[1:]) for g in groups]
    ms = tuple(_pnt(x["kb"], x["k"]) * x["dec_strict"] for x in e)
    ts = _inv_packed(ms) if known_inverses is None else _inv_packed_known(ms, tuple(known_inverses))
    us = [_pmm(t, x["vb"]) for t, x in zip(ts, e)]
    ws = [_pmm(t, x["kbe"]) for t, x in zip(ts, e)]
    attns = [_pnt(x["q"], x["k"]) * x["dec"] for x in e]
    ws_qs = [_pmm(jnp.concatenate([w, x["qe"]], axis=0), s) for w, x, s in zip(ws, e, ss)]
    v_news = [u - y[:c] for u, y in zip(us, ws_qs)]
    os = [y[c:] + _pmm(a, vn) for y, a, vn in zip(ws_qs, attns, v_news)]
    s_news = [s * x["sdecay"] + _ptn(x["kd"], vn) for s, x, vn in zip(ss, e, v_news)]
    if with_inverses:
        return list(zip(os, s_news)), list(ts)
    return list(zip(os, s_news))


def _lane_pick(x, h):
    lane = lax.broadcasted_iota(jnp.int32, x.shape, 1)
    return jnp.sum(jnp.where(lane == h, x, 0.0), axis=1, keepdims=True)


def _cum_log_decay(g):
    c = g.shape[0]
    lower = (lax.broadcasted_iota(jnp.int32, (c, c), 0) >= lax.broadcasted_iota(jnp.int32, (c, c), 1)).astype(f32)
    upper2 = (lax.broadcasted_iota(jnp.int32, (c, 128), 0) <= lax.broadcasted_iota(jnp.int32, (c, 128), 1) % c).astype(f32)
    return _dhi(lower, g, NN), _dhi(g, upper2, TN)


def _group_operands(gi, s_ref, q_ref, k_ref, v_ref, bv, gcv, gct_s):
    heads = [gi * GDN_PACK + u for u in range(GDN_PACK)]
    qk_off = pl.multiple_of(gi * 2 * GDN_D, 2 * GDN_D)
    v_off = pl.multiple_of(gi * GDN_PACK * GDN_D, GDN_PACK * GDN_D)
    return (s_ref[gi], q_ref[:, pl.ds(qk_off, 2 * GDN_D)], k_ref[:, pl.ds(qk_off, 2 * GDN_D)],
            v_ref[:, pl.ds(v_off, GDN_PACK * GDN_D)],
            [_lane_pick(bv, h) for h in heads], [_lane_pick(gcv, h) for h in heads],
            [gct_s[pl.ds(h, 1), :] for h in heads]), heads, qk_off, v_off


def _gdn_fwd(qn, kn, v, beta, g, *, n_real_chunks, name):
    lp = qn.shape[0]
    nchunk = lp // GDN_CHUNK
    C, D = GDN_CHUNK, GDN_D
    NG, SW = GDN_V_HEADS // GDN_PACK, GDN_PACK * GDN_D

    def body(q_ref, k_ref, v_ref, b_ref, g_ref, o_ref, st_ref, inv_ref, s_s, gc_s, gct_s):
        ci = pl.program_id(0)

        @pl.when(ci == 0)
        def _():
            s_s[...] = jnp.zeros_like(s_s)

        @pl.when(ci >= n_real_chunks)
        def _():
            o_ref[...] = jnp.zeros_like(o_ref)
            st_ref[...] = jnp.zeros_like(st_ref)
            inv_ref[...] = jnp.zeros_like(inv_ref)

        @pl.when(ci < n_real_chunks)
        def _():
            gc, gct = _cum_log_decay(g_ref[...])
            gc_s[...] = gc
            gct_s[...] = gct

            def some_groups(it, carry):
                ids = [it * GDN_FWD_INTERLEAVE + u for u in range(GDN_FWD_INTERLEAVE)]
                ops = [_group_operands(gi, s_s, q_ref, k_ref, v_ref, b_ref[...], gc_s[...], gct_s) for gi in ids]
                res, inverses = _gdn_groups([op[0] for op in ops], with_inverses=True)
                for gi, op, (o, s_new), t in zip(ids, ops, res, inverses):
                    st_ref[gi] = op[0][0]
                    inv_ref[gi] = t
                    s_s[gi] = s_new
                    o_ref[:, pl.ds(op[3], SW)] = o
                return carry

            lax.fori_loop(0, NG // GDN_FWD_INTERLEAVE, some_groups, 0)

    return pl.pallas_call(
        body, name=name, grid=(nchunk,),
        in_specs=[pl.BlockSpec((C, GDN_QK_W), lambda c: (c, 0)), pl.BlockSpec((C, GDN_QK_W), lambda c: (c, 0)),
                  pl.BlockSpec((C, GDN_V_W), lambda c: (c, 0)), pl.BlockSpec((C, 128), lambda c: (c, 0)),
                  pl.BlockSpec((C, 128), lambda c: (c, 0))],
        out_specs=[pl.BlockSpec((C, GDN_V_W), lambda c: (c, 0)),
                   pl.BlockSpec((None, NG, D, SW), lambda c: (c, 0, 0, 0)),
                   pl.BlockSpec((None, NG, C, GDN_PACK * C), lambda c: (c, 0, 0, 0))],
        out_shape=[jax.ShapeDtypeStruct((lp, GDN_V_W), f32), jax.ShapeDtypeStruct((nchunk, NG, D, SW), f32),
                   jax.ShapeDtypeStruct((nchunk, NG, C, GDN_PACK * C), f32)],
        scratch_shapes=[pltpu.VMEM((NG, D, SW), f32), pltpu.VMEM((C, 128), f32), pltpu.VMEM((128, 128), f32)],
        compiler_params=pltpu.CompilerParams(dimension_semantics=("arbitrary",)),
    )(qn, kn, v, beta, g)


def _gdn_bwd(qn, kn, v, beta, g, states, inverses, do, *, n_real_chunks, name):
    lp = qn.shape[0]
    nchunk = lp // GDN_CHUNK
    C, D = GDN_CHUNK, GDN_D
    NG, SW = GDN_V_HEADS // GDN_PACK, GDN_PACK * GDN_D
    rev = lambda i: (nchunk - 1 - i, 0)

    def body(q_ref, k_ref, v_ref, b_ref, g_ref, st_ref, inv_ref, do_ref,
             dq_ref, dk_ref, dv_ref, db_ref, dg_ref, ds_s, gc_s, gct_s, dgc_s, dgct_s, dbeta_s):
        step = pl.program_id(0)
        ci = nchunk - 1 - step

        @pl.when(step == 0)
        def _():
            ds_s[...] = jnp.zeros_like(ds_s)

        @pl.when(ci >= n_real_chunks)
        def _():
            for r in (dq_ref, dk_ref, dv_ref, db_ref, dg_ref):
                r[...] = jnp.zeros_like(r)

        @pl.when(ci < n_real_chunks)
        def _():
            gc, gct = _cum_log_decay(g_ref[...])
            gc_s[...] = gc
            gct_s[...] = gct
            dgc_s[...] = jnp.zeros_like(dgc_s)
            dgct_s[...] = jnp.zeros_like(dgct_s)
            dbeta_s[...] = jnp.zeros_like(dbeta_s)

            def some_groups(it, carry):
                ids = [it * GDN_BWD_INTERLEAVE + u for u in range(GDN_BWD_INTERLEAVE)]
                ops = [_group_operands(gi, st_ref, q_ref, k_ref, v_ref, b_ref[...], gc_s[...], gct_s) for gi in ids]
                cts = [(do_ref[:, pl.ds(op[3], SW)], ds_s[gi]) for gi, op in zip(ids, ops)]
                known = [inv_ref[gi] for gi in ids]
                _, vjp_fn = jax.vjp(lambda gs: _gdn_groups(gs, known_inverses=known), [op[0] for op in ops])
                (grads,) = vjp_fn(cts)
                lane = lax.broadcasted_iota(jnp.int32, (C, 128), 1)
                dbeta_acc, dgc_acc = dbeta_s[...], dgc_s[...]
                for gi, (_, heads, qk_off, v_off), (dsp, dq2, dk2, dv4, dbcols, dgcols, dgrows) in zip(ids, ops, grads):
                    ds_s[gi] = dsp
                    dq_ref[:, pl.ds(qk_off, 2 * D)] = dq2
                    dk_ref[:, pl.ds(qk_off, 2 * D)] = dk2
                    dv_ref[:, pl.ds(v_off, SW)] = dv4
                    for h, dbcol, dgcol, dgrow in zip(heads, dbcols, dgcols, dgrows):
                        dbeta_acc = dbeta_acc + jnp.where(lane == h, dbcol, 0.0)
                        dgc_acc = dgc_acc + jnp.where(lane == h, dgcol, 0.0)
                        dgct_s[pl.ds(h, 1), :] = dgrow
                dbeta_s[...] = dbeta_acc
                dgc_s[...] = dgc_acc
                return carry

            lax.fori_loop(0, NG // GDN_BWD_INTERLEAVE, some_groups, 0)
            fold = (lax.broadcasted_iota(jnp.int32, (128, C), 0) % C == lax.broadcasted_iota(jnp.int32, (128, C), 1)).astype(f32)
            eye = (lax.broadcasted_iota(jnp.int32, (128, 128), 0) == lax.broadcasted_iota(jnp.int32, (128, 128), 1)).astype(f32)
            dgc = dgc_s[...] + _dhi(_dhi(dgct_s[...], fold, NN), eye, TN)
            upper = (lax.broadcasted_iota(jnp.int32, (C, C), 0) <= lax.broadcasted_iota(jnp.int32, (C, C), 1)).astype(f32)
            dg_ref[...] = _dhi(upper, dgc, NN)
            db_ref[...] = dbeta_s[...]

    return pl.pallas_call(
        body, name=name, grid=(nchunk,),
        in_specs=[pl.BlockSpec((C, GDN_QK_W), rev), pl.BlockSpec((C, GDN_QK_W), rev), pl.BlockSpec((C, GDN_V_W), rev),
                  pl.BlockSpec((C, 128), rev), pl.BlockSpec((C, 128), rev),
                  pl.BlockSpec((None, NG, D, SW), lambda i: (nchunk - 1 - i, 0, 0, 0)),
                  pl.BlockSpec((None, NG, C, GDN_PACK * C), lambda i: (nchunk - 1 - i, 0, 0, 0)), pl.BlockSpec((C, GDN_V_W), rev)],
        out_specs=[pl.BlockSpec((C, GDN_QK_W), rev), pl.BlockSpec((C, GDN_QK_W), rev), pl.BlockSpec((C, GDN_V_W), rev),
                   pl.BlockSpec((C, 128), rev), pl.BlockSpec((C, 128), rev)],
        out_shape=[jax.ShapeDtypeStruct((lp, GDN_QK_W), f32)] * 2 + [jax.ShapeDtypeStruct((lp, GDN_V_W), f32)]
        + [jax.ShapeDtypeStruct((lp, 128), f32)] * 2,
        scratch_shapes=[pltpu.VMEM((NG, D, SW), f32), pltpu.VMEM((C, 128), f32), pltpu.VMEM((128, 128), f32),
                        pltpu.VMEM((C, 128), f32), pltpu.VMEM((128, 128), f32), pltpu.VMEM((C, 128), f32)],
        compiler_params=pltpu.CompilerParams(dimension_semantics=("arbitrary",)),
    )(qn, kn, v, beta, g, states, inverses, do)


def _att_mask_t(k0, q0, tk, tq):
    kcol = k0 + lax.broadcasted_iota(jnp.int32, (tk, tq), 0)
    qrow = q0 + lax.broadcasted_iota(jnp.int32, (tk, tq), 1)
    return jnp.logical_and(qrow >= kcol, kcol >= PAD_FRONT)


def _att_tiles(lp):
    tq = _pick(lp, (768, 512, 256))
    return tq, _pick(tq, ATT_KEY_TILES)


def _attention_fwd(q, kvu, kr, proj2, *, name):
    lp = q.shape[0]
    H = MLA_HEADS
    tq, tk = _att_tiles(lp)
    r = tq // tk

    def body(q_ref, kn_ref, kr_ref, v_ref, z_ref, o_ref, lse_ref, og_ref, m_s, l_s, acc_s, sa_s, sb_s):
        qi = pl.program_id(1)
        m_s[...] = jnp.full_like(m_s, NEG)
        l_s[...] = jnp.zeros_like(l_s)
        acc_s[...] = jnp.zeros_like(acc_s)

        def scores(ki):
            k0 = pl.multiple_of(ki * tk, tk)
            k = jnp.concatenate([kn_ref[pl.ds(k0, tk), :], kr_ref[pl.ds(k0, tk), :]], axis=1)
            return lax.dot_general(k, q_ref[...], (NT, ((), ())), preferred_element_type=f32)

        def consume(st, ki, masked):
            k0 = pl.multiple_of(ki * tk, tk)
            if masked:
                st = jnp.where(_att_mask_t(k0, qi * tq, tk, tq), st, NEG)
            m_prev = m_s[...]
            m_new = jnp.maximum(m_prev, jnp.max(st, axis=0, keepdims=True))
            alpha = jnp.exp(m_prev - m_new)
            p = jnp.exp(st - m_new)
            l_s[...] = alpha * l_s[...] + jnp.sum(p, axis=0, keepdims=True)
            acc_s[...] = alpha * acc_s[...] + lax.dot_general(v_ref[pl.ds(k0, tk), :], p.astype(bf16), (TN, ((), ())),
                                                              preferred_element_type=f32)
            m_s[...] = m_new

        n_full = qi * r

        def chain(blocks):
            bufs = (sa_s, sb_s)
            for j, (ki, masked) in enumerate(blocks):
                if j + 1 < len(blocks):
                    bufs[(j + 1) % 2][...] = scores(blocks[j + 1][0])
                consume(bufs[j % 2][...], ki, masked)

        diagonal = [(n_full + d, True) for d in range(r)]

        @pl.when(qi == 0)
        def _():
            sa_s[...] = scores(0)
            chain(diagonal)

        @pl.when(qi > 0)
        def _():
            sb_s[...] = scores(0)
            sa_s[...] = scores(1)
            consume(sb_s[...], 0, True)
            n_pairs = (n_full - 1) // 2

            def two(pi, carry):
                ki = 1 + 2 * pi
                sb_s[...] = scores(ki + 1)
                consume(sa_s[...], ki, False)
                sa_s[...] = scores(ki + 2)
                consume(sb_s[...], ki + 1, False)
                return carry

            lax.fori_loop(0, n_pairs, two, 0)
            nxt = 1 + 2 * n_pairs

            @pl.when(nxt < n_full)
            def _():
                chain([(nxt, False)] + diagonal)

            @pl.when(nxt == n_full)
            def _():
                chain(diagonal)
        o = jnp.transpose(acc_s[...] / l_s[...])
        o_ref[...] = o
        og_ref[...] = (o * _silu(z_ref[...])).astype(og_ref.dtype)
        lse_ref[...] = m_s[...] + jnp.log(l_s[...])

    return pl.pallas_call(
        body, name=name, grid=(H, lp // tq),
        in_specs=[pl.BlockSpec((tq, MLA_QKP), lambda h, qi: (qi, h)),
                  pl.BlockSpec((lp, 128), lambda h, qi: (0, h)),
                  pl.BlockSpec((lp, 128), lambda h, qi: (0, 0)),
                  pl.BlockSpec((lp, 128), lambda h, qi: (0, H + h)),
                  pl.BlockSpec((tq, 128), lambda h, qi: (qi, MLA_Q_RANK // 128 + h))],
        out_specs=[pl.BlockSpec((tq, 128), lambda h, qi: (qi, h)),
                   pl.BlockSpec((None, 1, tq), lambda h, qi: (h, 0, qi)),
                   pl.BlockSpec((tq, 128), lambda h, qi: (qi, h))],
        out_shape=[jax.ShapeDtypeStruct((lp, MLA_V_W), f32), jax.ShapeDtypeStruct((H, 1, lp), f32),
                   jax.ShapeDtypeStruct((lp, MLA_V_W), bf16)],
        scratch_shapes=[pltpu.VMEM((1, tq), f32), pltpu.VMEM((1, tq), f32), pltpu.VMEM((128, tq), f32),
                        pltpu.VMEM((tk, tq), f32), pltpu.VMEM((tk, tq), f32)],
        compiler_params=pltpu.CompilerParams(dimension_semantics=("arbitrary", "arbitrary")),
    )(q, kvu, kr, kvu, proj2)


def _gate_bwd(o, proj2, dgated, *, name):
    lp = o.shape[0]
    H, w = MLA_HEADS, 2 * MLA_V
    tq = _pick(lp, (768, 512, 256))

    def body(o_ref, z_ref, g_ref, do_ref, dz_ref, dl_ref):
        ov, z, g = o_ref[...], z_ref[...], g_ref[...]
        s = _sigmoid(z)
        do = (g * (z * s)).astype(bf16)
        do_ref[...] = do
        dz_ref[...] = (g * ov * (s * (1.0 + z * (1.0 - s)))).astype(dz_ref.dtype)
        prod = ov * do.astype(f32)
        for u in range(2):
            dl_ref[u] = jnp.sum(jnp.transpose(prod[:, u * MLA_V:(u + 1) * MLA_V]), axis=0, keepdims=True)

    blk = pl.BlockSpec((tq, w), lambda j, qi: (qi, j))
    return pl.pallas_call(
        body, name=name, grid=(H // 2, lp // tq),
        in_specs=[blk, pl.BlockSpec((tq, w), lambda j, qi: (qi, j + MLA_Q_RANK // w)), blk],
        out_specs=[blk, blk, pl.BlockSpec((2, 1, tq), lambda j, qi: (j, 0, qi))],
        out_shape=[jax.ShapeDtypeStruct((lp, MLA_V_W), bf16), jax.ShapeDtypeStruct((lp, MLA_V_W), bf16),
                   jax.ShapeDtypeStruct((H, 1, lp), f32)],
    )(o, proj2, dgated)


def _attention_bwd(q, kvu, kr, lse, delta, do, *, name):
    lp = q.shape[0]
    tq, t = _att_tiles(lp)
    H, nb = MLA_HEADS, lp // t
    r, nq = tq // t, lp // tq

    def body(q_ref, kn_ref, kr_ref, v_ref, lse_ref, dl_ref, do_ref, dq_ref, dkn_ref, dv_ref, dkr_ref, dk_s, dv_s,
             sa_s, da_s, sb_s, db_s):
        ki = pl.program_id(1)
        k0 = ki * t
        k = jnp.concatenate([kn_ref[...], kr_ref[...]], axis=1)
        vv = v_ref[...]
        dk_s[...] = jnp.zeros_like(dk_s)
        dv_s[...] = jnp.zeros_like(dv_s)

        def products(qi, s_ref, d_ref):
            q0 = pl.multiple_of(qi * tq, tq)
            s_ref[...] = lax.dot_general(k, q_ref[pl.ds(q0, tq), :], (NT, ((), ())), preferred_element_type=f32)
            d_ref[...] = lax.dot_general(vv, do_ref[pl.ds(q0, tq), :], (NT, ((), ())), preferred_element_type=f32)

        def accumulate(s_ref, d_ref, qi, masked, first):
            q0 = pl.multiple_of(qi * tq, tq)
            qv = q_ref[pl.ds(q0, tq), :]
            dob = do_ref[pl.ds(q0, tq), :]
            st = s_ref[...]
            if masked:
                st = jnp.where(_att_mask_t(k0, q0, t, tq), st, NEG)
            p = jnp.exp(st - lse_ref[:, pl.ds(q0, tq)])
            dv_s[...] += jnp.dot(p.astype(bf16), dob, preferred_element_type=f32)
            ds = (p * (d_ref[...] - dl_ref[:, pl.ds(q0, tq)])).astype(bf16)
            dk_s[...] += jnp.dot(ds, qv, preferred_element_type=f32)
            dq = lax.dot_general(ds, k, (TN, ((), ())), preferred_element_type=f32)
            if first:
                dq_ref[pl.ds(q0, tq), :] = dq
            else:
                dq_ref[pl.ds(q0, tq), :] += dq

        def sweep(qd, mask_all, first):
            last = nq - 1
            products(qd, sa_s, da_s)
            products(jnp.minimum(qd + 1, last), sb_s, db_s)
            accumulate(sa_s, da_s, qd, True, first)
            n = last - qd

            def two(pi, carry):
                i = qd + 1 + 2 * pi
                products(i + 1, sa_s, da_s)
                accumulate(sb_s, db_s, i, mask_all, first)
                products(jnp.minimum(i + 2, last), sb_s, db_s)
                accumulate(sa_s, da_s, i + 1, mask_all, first)
                return carry

            lax.fori_loop(0, n // 2, two, 0)

            @pl.when(n % 2 == 1)
            def _():
                accumulate(sb_s, db_s, last, mask_all, first)

        @pl.when(ki == 0)
        def _():
            sweep(0, True, True)

        @pl.when(ki > 0)
        def _():
            sweep(ki // r, False, False)

        dkn_ref[...] = dk_s[:, :128].astype(dkn_ref.dtype)
        dkr_ref[...] = dk_s[:, 128:]
        dv_ref[...] = dv_s[...].astype(dv_ref.dtype)

    return pl.pallas_call(
        body, name=name, grid=(H, nb),
        in_specs=[pl.BlockSpec((lp, MLA_QKP), lambda h, ki: (0, h)),
                  pl.BlockSpec((t, 128), lambda h, ki: (ki, h)),
                  pl.BlockSpec((t, 128), lambda h, ki: (ki, 0)),
                  pl.BlockSpec((t, 128), lambda h, ki: (ki, H + h)),
                  pl.BlockSpec((None, 1, lp), lambda h, ki: (h, 0, 0)),
                  pl.BlockSpec((None, 1, lp), lambda h, ki: (h, 0, 0)),
                  pl.BlockSpec((lp, 128), lambda h, ki: (0, h))],
        out_specs=[pl.BlockSpec((lp, MLA_QKP), lambda h, ki: (0, h)),
                   pl.BlockSpec((t, 128), lambda h, ki: (ki, h)),
                   pl.BlockSpec((t, 128), lambda h, ki: (ki, h)),
                   pl.BlockSpec((t, 128), lambda h, ki: (ki, h))],
        out_shape=[jax.ShapeDtypeStruct((lp, H * MLA_QKP), f32), jax.ShapeDtypeStruct((lp, MLA_V_W), bf16),
                   jax.ShapeDtypeStruct((lp, MLA_V_W), bf16), jax.ShapeDtypeStruct((lp, MLA_V_W), f32)],
        scratch_shapes=[pltpu.VMEM((t, MLA_QKP), f32), pltpu.VMEM((t, 128), f32)] + [pltpu.VMEM((t, tq), f32)] * 4,
        compiler_params=pltpu.CompilerParams(dimension_semantics=("arbitrary", "arbitrary")),
    )(q, kvu, kr, kvu, lse, delta, do)


def _q_proj_fwd(c_q, wq2, cos_q, sin_q, *, name, tm=ROW_TILE):
    lp, rank = c_q.shape
    gw = Q_GROUP * MLA_QKP
    ng = wq2.shape[1] // (2 * gw)

    def body(c_ref, w_ref, cos_ref, sin_ref, o_ref):
        qq = jnp.dot(c_ref[...], w_ref[...], preferred_element_type=f32)
        o_ref[...] = _st_q_rope([qq], [cos_ref[...], sin_ref[...]], [])[0].astype(o_ref.dtype)

    tab = pl.BlockSpec((tm, MLA_QKP), lambda j, i: (i, 0))
    return pl.pallas_call(
        body, name=name, grid=(ng, lp // tm),
        in_specs=[pl.BlockSpec((tm, rank), lambda j, i: (i, 0)), pl.BlockSpec((rank, 2 * gw), lambda j, i: (0, j)), tab, tab],
        out_specs=pl.BlockSpec((tm, gw), lambda j, i: (i, j)),
        out_shape=jax.ShapeDtypeStruct((lp, ng * gw), bf16),
    )(c_q, wq2, cos_q, sin_q)


def _q_proj_bwd(dq, c_q, wq2, cos_q, sin_q, *, name, tm=ROW_TILE):
    lp, rank = c_q.shape
    gw = Q_GROUP * MLA_QKP
    ng = wq2.shape[1] // (2 * gw)

    def body(dq_ref, c_ref, w_ref, cos_ref, sin_ref, dc_ref, dw_ref):
        dqq = _st_q_rope_t([dq_ref[...]], [cos_ref[...], sin_ref[...]], [])[0].astype(bf16)
        dc_ref[...] = lax.dot_general(dqq, w_ref[...], (NT, ((), ())), preferred_element_type=f32)
        dw = lax.dot_general(c_ref[...], dqq, (TN, ((), ())), preferred_element_type=f32)

        @pl.when(pl.program_id(1) == 0)
        def _():
            dw_ref[...] = dw

        @pl.when(pl.program_id(1) > 0)
        def _():
            dw_ref[...] += dw

    tab = pl.BlockSpec((tm, MLA_QKP), lambda j, i: (i, 0))
    return pl.pallas_call(
        body, name=name, grid=(ng, lp // tm),
        in_specs=[pl.BlockSpec((tm, gw), lambda j, i: (i, j)), pl.BlockSpec((tm, rank), lambda j, i: (i, 0)),
                  pl.BlockSpec((rank, 2 * gw), lambda j, i: (0, j)), tab, tab],
        out_specs=[pl.BlockSpec((tm, rank), lambda j, i: (i, j)), pl.BlockSpec((rank, 2 * gw), lambda j, i: (0, j))],
        out_shape=[jax.ShapeDtypeStruct((lp, ng * rank), f32), jax.ShapeDtypeStruct(wq2.shape, f32)],
        compiler_params=pltpu.CompilerParams(dimension_semantics=("arbitrary", "arbitrary")),
    )(dq, c_q, wq2, cos_q, sin_q)


def _exchange_copies(x_ref, o_ref, send_sems, recv_sems, local_sem, gather):
    mx, my, mc = lax.axis_index("x"), lax.axis_index("y"), lax.axis_index("c")
    me = 4 * mx + 2 * my + mc
    own = pltpu.make_async_copy(x_ref if gather else x_ref.at[me], o_ref.at[me], local_sem)
    sends, arrivals = [], []
    for k in range(1, N_DEV):
        px = 1 - mx if k & 4 else mx
        py = 1 - my if k & 2 else my
        pc = 1 - mc if k & 1 else mc
        peer = 4 * px + 2 * py + pc
        sends.append(pltpu.make_async_remote_copy(
            src_ref=x_ref if gather else x_ref.at[peer], dst_ref=o_ref.at[me],
            send_sem=send_sems.at[k - 1], recv_sem=recv_sems.at[k - 1],
            device_id=(px, py, pc), device_id_type=MESH_ID))
        arrivals.append(pltpu.make_async_remote_copy(
            src_ref=o_ref.at[peer], dst_ref=o_ref.at[peer],
            send_sem=send_sems.at[k - 1], recv_sem=recv_sems.at[k - 1],
            device_id=(mx, my, mc), device_id_type=MESH_ID))
    return own, sends, arrivals


def _exchange_start(copies):
    own, sends, _ = copies
    own.start()
    for cp in sends:
        cp.start()


def _exchange_wait(copies):
    own, sends, arrivals = copies
    for cp in arrivals:
        cp.wait_recv()
    for cp in sends:
        cp.wait_send()
    own.wait()


_EXCHANGE_SCRATCH = [pltpu.SemaphoreType.DMA((N_DEV - 1,)), pltpu.SemaphoreType.DMA((N_DEV - 1,)), pltpu.SemaphoreType.DMA]


def _exchange(x, *, gather, name):
    blk = x.shape if gather else x.shape[1:]

    def body(x_ref, o_ref, send_sems, recv_sems, local_sem):
        copies = _exchange_copies(x_ref, o_ref, send_sems, recv_sems, local_sem, gather)
        _exchange_start(copies)
        _exchange_wait(copies)

    return pl.pallas_call(
        body, name=name,
        in_specs=[pl.BlockSpec(memory_space=pltpu.HBM)], out_specs=pl.BlockSpec(memory_space=pltpu.HBM),
        out_shape=jax.ShapeDtypeStruct((N_DEV,) + tuple(blk), x.dtype),
        scratch_shapes=list(_EXCHANGE_SCRATCH),
    )(x)


def _reduce_adamw(parts, w, m, v, *, name):
    r = w.shape[0]
    tr = max(d for d in range(16, 3201, 16) if r % d == 0)

    def body(p_ref, w_ref, m_ref, v_ref, g_ref, d_ref, nm_ref, nv_ref):
        g = p_ref[0].astype(f32)
        for s in range(1, N_DEV):
            g = g + p_ref[s].astype(f32)
        mm = ADAM_B1 * m_ref[...] + (1.0 - ADAM_B1) * g
        vv = ADAM_B2 * v_ref[...] + (1.0 - ADAM_B2) * (g * g)
        m_hat = mm / (1.0 - ADAM_B1 ** ADAM_STEP)
        v_hat = vv / (1.0 - ADAM_B2 ** ADAM_STEP)
        g_ref[...] = g
        d_ref[...] = -ADAM_LR * (m_hat / (jnp.sqrt(v_hat) + ADAM_EPS) + ADAM_WD * w_ref[...])
        nm_ref[...] = mm
        nv_ref[...] = vv

    spec = pl.BlockSpec((tr, 128), lambda i: (i, 0))
    return pl.pallas_call(
        body, name=name, grid=(r // tr,),
        in_specs=[pl.BlockSpec((N_DEV, tr, 128), lambda i: (0, i, 0)), spec, spec, spec],
        out_specs=[spec] * 4, out_shape=[jax.ShapeDtypeStruct((r, 128), f32)] * 4,
    )(parts, w, m, v)


_SHARDED = ("gdn_w_in", "gdn_w_out", "kv_w_down", "kv_w_up", "mla_w_in", "mla_w_q_up", "mla_w_out", "meta_tokens", "gdn_conv_w")
_COL_SHARDED = {"gdn_w_in", "kv_w_up", "mla_w_in", "mla_w_q_up", "meta_tokens", "gdn_conv_w"}
_GATHER_FIRST = ("gdn_w_in",)
_GATHER_F32 = ("meta_tokens", "gdn_conv_w")
_GATHER_REST = ("gdn_w_out", "kv_w_down", "kv_w_up", "mla_w_in", "mla_w_q_up", "mla_w_out")
_SCATTER_EARLY = ("gdn_w_out", "kv_w_down", "kv_w_up", "mla_w_in", "mla_w_q_up", "mla_w_out")
_SCATTER_LATE = ("gdn_w_in", "meta_tokens", "gdn_conv_w")
_REPLICATED = ("pre_norm", "post_norm", "gdn_a_log", "gdn_dt_bias", "gdn_out_norm", "kv_norm", "kv_latent_norm",
               "mla_q_latent_norm")


def _rows128(a):
    flat = a.reshape(-1)
    pad = (-flat.shape[0]) % 128
    if pad:
        flat = jnp.pad(flat, (0, pad))
    return flat.reshape(-1, 128)


def _pack(arrs, row_multiple):
    parts = [_rows128(a) for a in arrs]
    buf = jnp.concatenate(parts, axis=0)
    pad = (-buf.shape[0]) % row_multiple
    if pad:
        buf = jnp.pad(buf, ((0, pad), (0, 0)))
    return buf


def _unpack(buf, shapes):
    out, r = [], 0
    for shp in shapes:
        n = math.prod(shp)
        rows = -(-n // 128)
        out.append(buf[r:r + rows].reshape(-1)[:n].reshape(shp))
        r += rows
    return out


def _unshard(g, full_shape, col):
    if col:
        return jnp.transpose(g, (1, 0, 2)).reshape(full_shape)
    return g.reshape(full_shape)


def _to_shards(a, col):
    r, c = a.shape
    if col:
        return jnp.transpose(a.reshape(r, N_DEV, c // N_DEV), (1, 0, 2))
    return a.reshape(N_DEV, r // N_DEV, c)


def _pad_cols(a, width):
    return jnp.pad(a, ((0, 0), (0, width - a.shape[1])))


def _rope_tables(lp):
    inv = ROPE_THETA ** (-jnp.arange(0, MLA_ROPE, 2, dtype=f32) / MLA_ROPE)
    pos = (jnp.arange(lp, dtype=jnp.int32) - PAD_FRONT).astype(f32)
    ang = pos[:, None] * inv[None, :]
    cos, sin = jnp.cos(ang), jnp.sin(ang)
    z = jnp.zeros((lp, 64), f32)
    return jnp.concatenate([cos, cos, z], axis=1), jnp.concatenate([-sin, sin, z], axis=1)


def kernel(x, meta_tokens, pre_norm, post_norm, gdn_w_in, gdn_conv_w, gdn_a_log, gdn_dt_bias, gdn_out_norm, gdn_w_out, kv_norm, kv_w_down, kv_latent_norm, kv_w_up, mla_w_in, mla_q_latent_norm, mla_w_q_up, mla_w_out, loss_target, m_meta_tokens, m_pre_norm, m_post_norm, m_gdn_w_in, m_gdn_conv_w, m_gdn_a_log, m_gdn_dt_bias, m_gdn_out_norm, m_gdn_w_out, m_kv_norm, m_kv_w_down, m_kv_latent_norm, m_kv_w_up, m_mla_w_in, m_mla_q_latent_norm, m_mla_w_q_up, m_mla_w_out, v_meta_tokens, v_pre_norm, v_post_norm, v_gdn_w_in, v_gdn_conv_w, v_gdn_a_log, v_gdn_dt_bias, v_gdn_out_norm, v_gdn_w_out, v_kv_norm, v_kv_w_down, v_kv_latent_norm, v_kv_w_up, v_mla_w_in, v_mla_q_latent_norm, v_mla_w_q_up, v_mla_w_out):
    W = dict(meta_tokens=meta_tokens, pre_norm=pre_norm, post_norm=post_norm, gdn_w_in=gdn_w_in, gdn_conv_w=gdn_conv_w,
             gdn_a_log=gdn_a_log, gdn_dt_bias=gdn_dt_bias, gdn_out_norm=gdn_out_norm, gdn_w_out=gdn_w_out, kv_norm=kv_norm,
             kv_w_down=kv_w_down, kv_latent_norm=kv_latent_norm, kv_w_up=kv_w_up, mla_w_in=mla_w_in,
             mla_q_latent_norm=mla_q_latent_norm, mla_w_q_up=mla_w_q_up, mla_w_out=mla_w_out)
    M = dict(meta_tokens=m_meta_tokens, pre_norm=m_pre_norm, post_norm=m_post_norm, gdn_w_in=m_gdn_w_in, gdn_conv_w=m_gdn_conv_w,
             gdn_a_log=m_gdn_a_log, gdn_dt_bias=m_gdn_dt_bias, gdn_out_norm=m_gdn_out_norm, gdn_w_out=m_gdn_w_out, kv_norm=m_kv_norm,
             kv_w_down=m_kv_w_down, kv_latent_norm=m_kv_latent_norm, kv_w_up=m_kv_w_up, mla_w_in=m_mla_w_in,
             mla_q_latent_norm=m_mla_q_latent_norm, mla_w_q_up=m_mla_w_q_up, mla_w_out=m_mla_w_out)
    V = dict(meta_tokens=v_meta_tokens, pre_norm=v_pre_norm, post_norm=v_post_norm, gdn_w_in=v_gdn_w_in, gdn_conv_w=v_gdn_conv_w,
             gdn_a_log=v_gdn_a_log, gdn_dt_bias=v_gdn_dt_bias, gdn_out_norm=v_gdn_out_norm, gdn_w_out=v_gdn_w_out, kv_norm=v_kv_norm,
             kv_w_down=v_kv_w_down, kv_latent_norm=v_kv_latent_norm, kv_w_up=v_kv_w_up, mla_w_in=v_mla_w_in,
             mla_q_latent_norm=v_mla_q_latent_norm, mla_w_q_up=v_mla_w_q_up, mla_w_out=v_mla_w_out)
    order = list(W)

    n_tok = x.shape[1]
    assert n_tok % GDN_CHUNK == 0
    n_real = ROW0 + n_tok
    lp = -(-n_real // ROW_TILE) * ROW_TILE
    n_real_chunks = n_real // GDN_CHUNK

    shard2d = {n: W[n].reshape(W[n].shape[-2:]) for n in _SHARDED}
    full_shape = {n: ((s.shape[0], s.shape[1] * N_DEV) if n in _COL_SHARDED else (s.shape[0] * N_DEV, s.shape[1]))
                  for n, s in shard2d.items()}
    full = {}

    def unpack_gathered(names, buf):
        r = 0
        for n in names:
            shp = shard2d[n].shape
            rows = math.prod(shp) // 128
            blocks = buf[:, r:r + rows].reshape((N_DEV,) + shp)
            full[n] = _unshard(blocks, full_shape[n], n in _COL_SHARDED)
            r += rows

    unpack_gathered(_GATHER_FIRST, _exchange(_pack([shard2d[n].astype(bf16) for n in _GATHER_FIRST], 16), gather=True,
                                             name="gather_w_in"))
    unpack_gathered(_GATHER_F32, _exchange(_pack([shard2d[n] for n in _GATHER_F32], 8), gather=True, name="gather_meta_conv"))
    rest_shards = _pack([shard2d[n].astype(bf16) for n in _GATHER_REST], 16)

    h0 = jnp.concatenate([jnp.zeros((PAD_FRONT, D_MODEL), f32), full["meta_tokens"], x[0],
                          jnp.zeros((lp - n_real, D_MODEL), f32)], axis=0)
    tgt = jnp.concatenate([jnp.zeros((ROW0, D_MODEL), f32), loss_target[0], jnp.zeros((lp - n_real, D_MODEL), f32)], axis=0)
    cos_k, sin_k = _rope_tables(lp)
    one = jnp.ones((lp, 128), f32)
    cos_q = jnp.concatenate([one, cos_k], axis=1)
    sin_q = jnp.concatenate([jnp.zeros((lp, 128), f32), sin_k], axis=1)

    w_in = full["gdn_w_in"]
    s1 = GDN_CONV_W + GDN_V_W
    w_in_p = jnp.concatenate([w_in[:, :s1], _pad_cols(w_in[:, s1:s1 + 16], 128), _pad_cols(w_in[:, s1 + 16:], 128)], axis=1)
    pre0, pre1 = pre_norm[0:1], pre_norm[1:2]
    post0, post1 = post_norm[0:1], post_norm[1:2]
    (hn0,) = _rowwise(_st_prenorm, [(h0, None, 0)], [], [pre0], [(D_MODEL, bf16, None)], name="f_prenorm0")
    proj, g_rest = _matmul(hn0, w_in_p, name="f_gdn_in", side=(rest_shards, True))
    unpack_gathered(_GATHER_REST, g_rest)

    wd = full["kv_w_down"]
    zc = jnp.zeros((D_MODEL, 64), bf16)
    wd2 = jnp.concatenate([wd, zc, jnp.zeros((D_MODEL, 128), bf16), wd[:, 160:192], wd[:, 128:160], zc], axis=1)
    wup_p = jnp.transpose(full["kv_w_up"].reshape(MLA_KV_RANK, MLA_HEADS, 2, 128), (0, 2, 1, 3)).reshape(MLA_KV_RANK, 2 * MLA_V_W)
    wq = full["mla_w_q_up"].reshape(MLA_Q_RANK, MLA_HEADS, MLA_QK)
    zq64 = jnp.zeros((MLA_Q_RANK, MLA_HEADS, 64), bf16)
    wq_plain = jnp.concatenate([wq, zq64], axis=2).reshape(MLA_Q_RANK, MLA_HEADS * MLA_QKP)
    wq_swap = jnp.concatenate([jnp.zeros((MLA_Q_RANK, MLA_HEADS, 128), bf16), wq[:, :, 160:192], wq[:, :, 128:160], zq64],
                              axis=2).reshape(MLA_Q_RANK, MLA_HEADS * MLA_QKP)
    q_half = Q_GROUP * MLA_QKP
    wq2 = jnp.concatenate([wq_plain[:, :q_half], wq_swap[:, :q_half], wq_plain[:, q_half:], wq_swap[:, q_half:]], axis=1)
    w_mla_in, w_gdn_out, w_mla_out = full["mla_w_in"], full["gdn_w_out"], full["mla_w_out"]
    conv_w = full["gdn_conv_w"]
    alog_p, dtb_p = _pad_cols(gdn_a_log, 128), _pad_cols(gdn_dt_bias, 128)
    kvn, kvln = kv_norm.reshape(1, -1), kv_latent_norm.reshape(1, -1)

    conv = _conv_fwd(proj, conv_w, col_blocks=GDN_CONV_W // CONV_BC, name="f_conv")
    (qn,) = _rowwise(_st_gdn_q, [(conv, GDN_QK_W, 0)], [], [], [(GDN_QK_W, f32, GDN_QK_W)], heads=GDN_QK_HEADS, name="f_gdn_q")
    (kn,) = _rowwise(_st_gdn_k, [(conv, GDN_QK_W, 1)], [], [], [(GDN_QK_W, f32, GDN_QK_W)], heads=GDN_QK_HEADS, name="f_gdn_k")
    (vv,) = _rowwise(_st_gdn_v, [(conv, GDN_V_W, 1)], [], [], [(GDN_V_W, f32, GDN_V_W)], name="f_gdn_v")
    gate_rows = [(proj, 128, s1 // 128), (proj, 128, s1 // 128 + 1)]
    beta, gdec = _rowwise(_st_gdn_gate, gate_rows, [], [alog_p, dtb_p], [(128, f32, None)] * 2, name="f_gdn_gate")
    o_gdn, states, inverses = _gdn_fwd(qn, kn, vv, beta, gdec, n_real_chunks=n_real_chunks, name="f_gdn")
    out_rows = [(o_gdn, GDN_V_W, 0), (proj, GDN_V_W, GDN_CONV_W // GDN_V_W)]
    (og,) = _rowwise(_st_gdn_out, out_rows, [], [gdn_out_norm], [(GDN_V_W, bf16, GDN_V_W)], heads=GDN_V_HEADS, name="f_gdn_out")
    y0 = _matmul(og, w_gdn_out, name="f_gdn_wout")
    mid_rows = [(h0, None, 0), (y0, None, 0)]
    h1, hn1, hkv = _rowwise(_st_mid, mid_rows, [], [post0, pre1, kvn],
                            [(D_MODEL, f32, None), (D_MODEL, bf16, None), (D_MODEL, bf16, None)], name="f_mid")
    ckr = _matmul(hkv, wd2, name="f_kv_down")
    proj2 = _matmul(hn1, w_mla_in, name="f_mla_in")
    lat_rows = [(ckr, None, 0), (proj2, MLA_Q_RANK, 0)]
    lat_nd = [(cos_k, None, 0), (sin_k, None, 0)]
    c_kv, k_rope, c_q = _rowwise(_st_latent, lat_rows, lat_nd, [kvln, mla_q_latent_norm],
                                 [(128, bf16, None), (128, bf16, None), (MLA_Q_RANK, bf16, None)], name="f_latent")
    kvu = _matmul(c_kv, wup_p, out_dtype=bf16, name="f_kv_up")
    q_att = _q_proj_fwd(c_q, wq2, cos_q, sin_q, name="f_q_proj")
    o_att, lse, og2 = _attention_fwd(q_att, kvu, k_rope, proj2, name="f_attention")
    y1 = _matmul(og2, w_mla_out, name="f_mla_wout")
    st_loss = _make_st_loss(n_tok)
    loss_rows_in = [(h1, None, 0), (y1, None, 0)]
    (loss_rows,) = _rowwise(st_loss, loss_rows_in, [(tgt, None, 0)], [post1], [(1, f32, None)], name="f_loss")
    loss = lax.psum(jnp.sum(loss_rows), ("x", "y", "c"))

    ones_ct = jnp.ones((lp, 1), f32)
    (dh1_a, dy1), (dpost1,) = _rowwise_vjp(st_loss, loss_rows_in, [(tgt, None, 0)], [post1], [(ones_ct, None, 0)], name="b_loss")
    dog2 = _matmul(dy1, w_mla_out, tb=True, name="b_mla_wout_x")
    dw_mla_out = _matmul(og2, dy1, ta=True, name="b_mla_wout_w")
    do_att, dz2, delta = _gate_bwd(o_att, proj2, dog2, name="b_mla_gate")
    dq_att, dkn, dvv, dkr_h = _attention_bwd(q_att, kvu, k_rope, lse, delta, do_att, name="b_attention")
    dc_q, dwq2 = _q_proj_bwd(dq_att, c_q, wq2, cos_q, sin_q, name="b_q_proj")
    dkvu = jnp.concatenate([dkn, dvv], axis=1)
    dc_kv = _matmul(dkvu, wup_p, tb=True, name="b_kv_up_x")
    dwup_p = _matmul(c_kv, dkvu, ta=True, name="b_kv_up_w")

    def lat_ct(cv):
        dkr = cv[1][:, 0:128]
        for h in range(1, MLA_HEADS):
            dkr = dkr + cv[1][:, h * 128:(h + 1) * 128]
        return [cv[0], dkr, cv[2][:, :MLA_Q_RANK] + cv[2][:, MLA_Q_RANK:]]

    (dckr, dcq_pre), (dkvln, dqln) = _rowwise_vjp(
        _st_latent, lat_rows, lat_nd, [kvln, mla_q_latent_norm],
        [(dc_kv, None, 0), (dkr_h, None, 0), (dc_q, None, 0)], ct_pre=lat_ct, name="b_latent", grad_dtypes=[bf16, bf16])
    dproj2 = jnp.concatenate([dcq_pre, dz2], axis=1)
    dhn1 = _matmul(dproj2, w_mla_in, tb=True, name="b_mla_in_x")
    dw_mla_in = _matmul(hn1, dproj2, ta=True, name="b_mla_in_w")
    dhkv = _matmul(dckr, wd2, tb=True, name="b_kv_down_x")
    dwd2 = _matmul(hkv, dckr, ta=True, name="b_kv_down_w")
    (dh0_a, dy0), (dpost0, dpre1, dkvn) = _rowwise_vjp(
        _st_mid, mid_rows, [], [post0, pre1, kvn], [(dh1_a, None, 0), (dhn1, None, 0), (dhkv, None, 0)], name="b_mid")
    dog = _matmul(dy0, w_gdn_out, tb=True, name="b_gdn_wout_x")
    dw_gdn_out = _matmul(og, dy0, ta=True, name="b_gdn_wout_w")
    (do_gdn, dz), (dout_norm,) = _rowwise_vjp(_st_gdn_out, out_rows, [], [gdn_out_norm], [(dog, GDN_V_W, 0)], heads=GDN_V_HEADS,
                                              name="b_gdn_out", grad_dtypes=[f32, bf16])
    dq_g, dk_g, dv_g, dbeta, dgdec = _gdn_bwd(qn, kn, vv, beta, gdec, states, inverses, do_gdn, n_real_chunks=n_real_chunks,
                                              name="b_gdn")
    (db_col, da_col), (dalog_p, ddtb_p) = _rowwise_vjp(
        _st_gdn_gate, gate_rows, [], [alog_p, dtb_p], [(dbeta, None, 0), (dgdec, None, 0)], name="b_gdn_gate",
        grad_dtypes=[bf16, bf16])
    (dconv_q,), _ = _rowwise_vjp(_st_gdn_q, [(conv, GDN_QK_W, 0)], [], [], [(dq_g, GDN_QK_W, 0)], heads=GDN_QK_HEADS, name="b_gdn_q")
    (dconv_k,), _ = _rowwise_vjp(_st_gdn_k, [(conv, GDN_QK_W, 1)], [], [], [(dk_g, GDN_QK_W, 0)], heads=GDN_QK_HEADS, name="b_gdn_k")
    (dconv_v,), _ = _rowwise_vjp(_st_gdn_v, [(conv, GDN_V_W, 1)], [], [], [(dv_g, GDN_V_W, 0)], name="b_gdn_v")
    nq_b = GDN_QK_W // CONV_BC
    dpre_q, dcw_q = _conv_bwd(dconv_q, proj, conv_w, x_off=0, w_off=0, name="b_conv_q")
    dpre_k, dcw_k = _conv_bwd(dconv_k, proj, conv_w, x_off=nq_b, w_off=nq_b, name="b_conv_k")
    dpre_v, dcw_v = _conv_bwd(dconv_v, proj, conv_w, x_off=2 * nq_b, w_off=2 * nq_b, name="b_conv_v")
    dproj = jnp.concatenate([dpre_q, dpre_k, dpre_v, dz, db_col, da_col], axis=1)
    G = {}
    G["kv_w_down"] = jnp.concatenate([dwd2[:, :128], dwd2[:, 128:160] + dwd2[:, 416:448], dwd2[:, 160:192] + dwd2[:, 384:416]], axis=1)
    G["kv_w_up"] = jnp.transpose(dwup_p.reshape(MLA_KV_RANK, 2, MLA_HEADS, 128), (0, 2, 1, 3)).reshape(MLA_KV_RANK, 2 * MLA_V_W)
    G["mla_w_in"] = dw_mla_in
    dq4 = dwq2.reshape(MLA_Q_RANK, 2, 2, Q_GROUP, MLA_QKP)
    dqp = dq4[:, :, 0].reshape(MLA_Q_RANK, MLA_HEADS, MLA_QKP)
    dqs = dq4[:, :, 1].reshape(MLA_Q_RANK, MLA_HEADS, MLA_QKP)
    G["mla_w_q_up"] = jnp.concatenate([dqp[:, :, :128], dqp[:, :, 128:160] + dqs[:, :, 160:192],
                                       dqp[:, :, 160:192] + dqs[:, :, 128:160]], axis=2).reshape(MLA_Q_RANK, MLA_HEADS * MLA_QK)
    G["mla_w_out"] = dw_mla_out
    G["gdn_w_out"] = dw_gdn_out

    def shards_to_send(names):
        return jnp.concatenate([_to_shards(G[n], n in _COL_SHARDED).reshape(N_DEV, -1, 128).astype(bf16) for n in names], axis=1)

    dhn0, parts_early = _matmul(dproj, w_in_p, tb=True, name="b_gdn_in_x", side=(shards_to_send(_SCATTER_EARLY), False))
    dw_in_p = _matmul(hn0, dproj, ta=True, name="b_gdn_in_w")
    (dh0,), (dpre0,) = _rowwise_vjp(_st_prenorm, [(h0, None, 0)], [], [pre0], [(dhn0, None, 0)], extra=[dh0_a], name="b_prenorm0")

    grad_x = dh0[ROW0:n_real][None]
    G["meta_tokens"] = dh0[PAD_FRONT:ROW0]
    G["gdn_w_in"] = jnp.concatenate([dw_in_p[:, :s1 + 16], dw_in_p[:, s1 + 128:s1 + 144]], axis=1)
    G["gdn_conv_w"] = jnp.concatenate([dcw_q, dcw_k, dcw_v], axis=1)
    G["pre_norm"] = jnp.concatenate([dpre0, dpre1], axis=0)
    G["post_norm"] = jnp.concatenate([dpost0, dpost1], axis=0)
    G["gdn_a_log"] = dalog_p[:, :GDN_V_HEADS]
    G["gdn_dt_bias"] = ddtb_p[:, :GDN_V_HEADS]
    G["gdn_out_norm"] = dout_norm
    G["kv_norm"] = dkvn.reshape(-1)
    G["kv_latent_norm"] = dkvln.reshape(-1)
    G["mla_q_latent_norm"] = dqln

    parts_late = _exchange(shards_to_send(_SCATTER_LATE), gather=False, name="scatter_grads")
    parts_r = _exchange(_pack([G[n] for n in _REPLICATED], 8), gather=True, name="gather_small_grads")
    outs = {}
    for names, parts, tag in ((_SCATTER_EARLY, parts_early, "early"), (_SCATTER_LATE, parts_late, "late"),
                              (_REPLICATED, parts_r, "replicated")):
        w_p, m_p, v_p = (_pack([d[n] for n in names], 8) for d in (W, M, V))
        res = _reduce_adamw(parts, w_p, m_p, v_p, name="adamw_" + tag)
        for kind, buf in zip(("grad", "delta", "new_m", "new_v"), res):
            for n, a in zip(names, _unpack(buf, [W[n].shape for n in names])):
                outs[kind, n] = a
    return (loss, grad_x, *[outs[k, n] for k in ("grad", "delta", "new_m", "new_v") for n in order])
```

```python
import functools
import math

import jax
import jax.numpy as jnp
from jax import lax
from jax.experimental import pallas as pl
from jax.experimental.pallas import tpu as pltpu

f32, bf16 = jnp.float32, jnp.bfloat16
HIGHEST = lax.Precision.HIGHEST
MESH_ID = pl.DeviceIdType.MESH

N_DEV = 8
D_MODEL = 1024
N_META = 16
NORM_EPS = 1e-6
PAD_FRONT = 48
ROW0 = PAD_FRONT + N_META
GDN_QK_HEADS, GDN_V_HEADS, GDN_D = 8, 16, 128
GDN_CHUNK = 64
GDN_QK_W, GDN_V_W = GDN_QK_HEADS * GDN_D, GDN_V_HEADS * GDN_D
GDN_CONV_W = 2 * GDN_QK_W + GDN_V_W
GDN_IN_W = GDN_CONV_W + GDN_V_W + 2 * GDN_V_HEADS
GDN_IN_WP = GDN_CONV_W + GDN_V_W + 2 * 128
MLA_HEADS, MLA_NOPE, MLA_ROPE, MLA_V = 16, 128, 64, 128
MLA_Q_RANK, MLA_KV_RANK = 256, 128
MLA_QK = MLA_NOPE + MLA_ROPE
MLA_QKP = 256
MLA_V_W = MLA_HEADS * MLA_V
ROPE_THETA = 10000.0
NEG = -1e30
ROW_TILE = 256
ATT_KEY_TILES = (384, 256)

ADAM_LR, ADAM_B1, ADAM_B2, ADAM_EPS, ADAM_WD, ADAM_STEP = 0.001, 0.9, 0.999, 1e-08, 0.01, 10

NN = ((1,), (0,))
NT = ((1,), (1,))
TN = ((0,), (0,))


def _pick(dim, prefs):
    for p in prefs:
        if dim % p == 0:
            return p
    return dim


def _dlo(a, b, dims):
    return lax.dot_general(a.astype(bf16), b.astype(bf16), (dims, ((), ())), preferred_element_type=f32)


def _dhi(a, b, dims):
    return lax.dot_general(a, b, (dims, ((), ())), precision=HIGHEST, preferred_element_type=f32)


def _matmul(a, b, *, ta=False, tb=False, out_dtype=f32, name, side=None):
    assert not (ta and tb)
    if ta:
        kdim, m = a.shape
    else:
        m, kdim = a.shape
    n = b.shape[0] if tb else b.shape[1]
    assert (b.shape[1] if tb else b.shape[0]) == kdim
    tm = _pick(m, (1024, 768, 512, 384, 256, 128))
    tn = _pick(n, (1024, 768, 640, 512, 256, 128))
    tk = _pick(kdim, (1024, 768, 640, 512, 256, 128))
    nk = kdim // tk
    dims = TN if ta else (NT if tb else NN)

    grid = (m // tm, n // tn, nk)

    def product(a_ref, b_ref, o_ref, acc_ref):
        k = pl.program_id(2)

        @pl.when(k == 0)
        def _():
            acc_ref[...] = jnp.zeros_like(acc_ref)

        acc_ref[...] += lax.dot_general(a_ref[...].astype(bf16), b_ref[...].astype(bf16), (dims, ((), ())),
                                        preferred_element_type=f32)

        @pl.when(k == nk - 1)
        def _():
            o_ref[...] = acc_ref[...].astype(o_ref.dtype)

    a_spec = pl.BlockSpec((tk, tm), lambda i, j, k: (k, i)) if ta else pl.BlockSpec((tm, tk), lambda i, j, k: (i, k))
    b_spec = pl.BlockSpec((tn, tk), lambda i, j, k: (j, k)) if tb else pl.BlockSpec((tk, tn), lambda i, j, k: (k, j))
    o_spec = pl.BlockSpec((tm, tn), lambda i, j, k: (i, j))
    o_shape = jax.ShapeDtypeStruct((m, n), out_dtype)
    if side is None:
        def body(a_ref, b_ref, o_ref, acc_ref):
            product(a_ref, b_ref, o_ref, acc_ref)

        return pl.pallas_call(
            body, name=name, grid=grid, in_specs=[a_spec, b_spec], out_specs=o_spec, out_shape=o_shape,
            scratch_shapes=[pltpu.VMEM((tm, tn), f32)],
            compiler_params=pltpu.CompilerParams(dimension_semantics=("parallel", "parallel", "arbitrary")),
        )(a, b)

    x, gather = side
    blk = x.shape if gather else x.shape[1:]

    def body_with_exchange(a_ref, b_ref, x_ref, o_ref, xo_ref, acc_ref, send_sems, recv_sems, local_sem):
        step = (pl.program_id(0) * grid[1] + pl.program_id(1)) * grid[2] + pl.program_id(2)
        copies = _exchange_copies(x_ref, xo_ref, send_sems, recv_sems, local_sem, gather)

        @pl.when(step == 0)
        def _():
            _exchange_start(copies)

        product(a_ref, b_ref, o_ref, acc_ref)

        @pl.when(step == grid[0] * grid[1] * grid[2] - 1)
        def _():
            _exchange_wait(copies)

    hbm = pl.BlockSpec(memory_space=pltpu.HBM)
    return pl.pallas_call(
        body_with_exchange, name=name, grid=grid, in_specs=[a_spec, b_spec, hbm], out_specs=[o_spec, hbm],
        out_shape=[o_shape, jax.ShapeDtypeStruct((N_DEV,) + tuple(blk), x.dtype)],
        scratch_shapes=[pltpu.VMEM((tm, tn), f32)] + list(_EXCHANGE_SCRATCH),
        compiler_params=pltpu.CompilerParams(dimension_semantics=("arbitrary", "arbitrary", "arbitrary")),
    )(a, b, x)


def _row_spec(item, tr):
    a, bc, off = item
    if bc is None:
        return pl.BlockSpec((tr, a.shape[1]), lambda i, j: (i, 0))
    return pl.BlockSpec((tr, bc), lambda i, j, off=off: (i, j + off))


def _param_spec(p):
    return pl.BlockSpec(p.shape, lambda i, j: (0, 0))


def _row_tile(lp, items):
    widest = max(a.shape[1] if bc is None else bc for (a, bc, _) in items)
    return ROW_TILE if widest >= 1024 else _pick(lp, (768, 512, 256))


def _head_cols(tiles, h, heads):
    return [x[:, h * (x.shape[1] // heads):(h + 1) * (x.shape[1] // heads)] for x in tiles]


def _rowwise(fn, rows, nodiff, params, outs, *, ncol=1, heads=1, name):
    lp = rows[0][0].shape[0]
    tr = _row_tile(lp, rows)
    nr, nd = len(rows), len(nodiff)

    def body(*refs):
        rv = [r[...].astype(f32) for r in refs[:nr]]
        nv = [r[...] for r in refs[nr:nr + nd]]
        pv = [r[...] for r in refs[nr + nd:nr + nd + len(params)]]
        per_head = [fn(_head_cols(rv, h, heads), nv, pv) for h in range(heads)]
        res = [jnp.concatenate(list(vals), axis=1) if heads > 1 else vals[0] for vals in zip(*per_head)]
        for ref, val in zip(refs[nr + nd + len(params):], res):
            ref[...] = val.astype(ref.dtype)

    out_specs = [pl.BlockSpec((tr, c if bc is None else bc), (lambda i, j: (i, 0)) if bc is None else (lambda i, j: (i, j)))
                 for (c, _, bc) in outs]
    return pl.pallas_call(
        body, name=name, grid=(lp // tr, ncol),
        in_specs=[_row_spec(it, tr) for it in rows + nodiff] + [_param_spec(p) for p in params],
        out_specs=out_specs,
        out_shape=[jax.ShapeDtypeStruct((lp, c), dt) for (c, dt, _) in outs],
    )(*[it[0] for it in rows + nodiff], *params)


def _rowwise_vjp(fn, rows, nodiff, params, cts, *, ncol=1, heads=1, name, ct_pre=None, extra=None, grad_dtypes=None):
    lp = rows[0][0].shape[0]
    tr = _row_tile(lp, rows)
    nr, nd, npar, nct = len(rows), len(nodiff), len(params), len(cts)
    extra = extra or [None] * nr
    grad_dtypes = grad_dtypes or [f32] * nr
    ex_items = [(e, rows[k][1], 0) for k, e in enumerate(extra) if e is not None]
    ex_pos = [k for k, e in enumerate(extra) if e is not None]
    for (a, bc, _) in rows:
        assert bc is not None or ncol == 1

    def body(*refs):
        pos = 0
        rv = [r[...].astype(f32) for r in refs[pos:pos + nr]]; pos += nr
        nv = [r[...] for r in refs[pos:pos + nd]]; pos += nd
        pv = [r[...] for r in refs[pos:pos + npar]]; pos += npar
        cv = [r[...].astype(f32) for r in refs[pos:pos + nct]]; pos += nct
        ev = [r[...].astype(f32) for r in refs[pos:pos + len(ex_items)]]; pos += len(ex_items)
        drow_refs = refs[pos:pos + nr]; pos += nr
        dpar_refs = refs[pos:pos + npar]
        ctv = ct_pre(cv) if ct_pre is not None else cv
        drow_h, dpar = [], None
        for h in range(heads):
            outs, vjp_fn = jax.vjp(lambda rr, pp: fn(rr, nv, pp), _head_cols(rv, h, heads), pv)
            dr, dp = vjp_fn([c.astype(o.dtype) for c, o in zip(_head_cols(ctv, h, heads), outs)])
            drow_h.append(dr)
            dpar = dp if dpar is None else [a + b for a, b in zip(dpar, dp)]
        drow = [jnp.concatenate(list(vals), axis=1) if heads > 1 else vals[0] for vals in zip(*drow_h)]
        for k, e in zip(ex_pos, ev):
            drow[k] = drow[k] + e
        for ref, val in zip(drow_refs, drow):
            ref[...] = val.astype(ref.dtype)
        first = jnp.logical_and(pl.program_id(0) == 0, pl.program_id(1) == 0)

        @pl.when(first)
        def _():
            for ref, val in zip(dpar_refs, dpar):
                ref[...] = val

        @pl.when(jnp.logical_not(first))
        def _():
            for ref, val in zip(dpar_refs, dpar):
                ref[...] += val

    drow_shapes, drow_specs = [], []
    for (a, bc, _), dt in zip(rows, grad_dtypes):
        if bc is None:
            drow_shapes.append(jax.ShapeDtypeStruct((lp, a.shape[1]), dt))
            drow_specs.append(pl.BlockSpec((tr, a.shape[1]), lambda i, j: (i, 0)))
        else:
            drow_shapes.append(jax.ShapeDtypeStruct((lp, ncol * bc), dt))
            drow_specs.append(pl.BlockSpec((tr, bc), lambda i, j: (i, j)))
    res = pl.pallas_call(
        body, name=name, grid=(lp // tr, ncol),
        in_specs=[_row_spec(it, tr) for it in rows + nodiff] + [_param_spec(p) for p in params]
        + [_row_spec(it, tr) for it in cts + ex_items],
        out_specs=drow_specs + [_param_spec(p) for p in params],
        out_shape=drow_shapes + [jax.ShapeDtypeStruct(p.shape, f32) for p in params],
        compiler_params=pltpu.CompilerParams(dimension_semantics=("arbitrary", "arbitrary")),
    )(*[it[0] for it in rows + nodiff], *params, *[it[0] for it in cts + ex_items])
    return res[:nr], res[nr:]


def _rms(x, g):
    return x * lax.rsqrt(jnp.mean(x * x, axis=-1, keepdims=True) + NORM_EPS) * g


def _l2n(x):
    return x * lax.rsqrt(jnp.sum(x * x, axis=-1, keepdims=True) + NORM_EPS)


def _sigmoid(x):
    return 1.0 / (1.0 + jnp.exp(-x))


def _silu(x):
    return x * _sigmoid(x)


def _softplus(x):
    return jnp.maximum(x, 0.0) + jnp.log(1.0 + jnp.exp(-jnp.abs(x)))


def _row_ids(shape):
    return pl.program_id(0) * shape[0] + lax.broadcasted_iota(jnp.int32, shape, 0)


def _st_prenorm(r, n, p):
    return [_rms(r[0], p[0])]


def _st_gdn_q(r, n, p):
    return [_l2n(_silu(r[0])) * (GDN_D ** -0.5)]


def _st_gdn_k(r, n, p):
    return [_l2n(_silu(r[0]))]


def _st_gdn_v(r, n, p):
    return [_silu(r[0])]


def _st_gdn_gate(r, n, p):
    real = _row_ids(r[0].shape) >= PAD_FRONT
    beta = jnp.where(real, _sigmoid(r[0]), 0.0)
    g = jnp.where(real, -jnp.exp(p[0]) * _softplus(r[1] + p[1]), 0.0)
    return [beta, g]


def _st_gdn_out(r, n, p):
    return [_rms(r[0], p[0]) * _silu(r[1])]


def _st_mid(r, n, p):
    h1 = r[0] + _rms(r[1], p[0])
    return [h1, _rms(h1, p[1]), _rms(h1, p[2])]


def _st_latent(r, n, p):
    ckr, cq = r
    c_kv = _rms(ckr[:, :MLA_KV_RANK], p[0])
    k_rope = ckr[:, 128:256] * n[0] + ckr[:, 384:512] * n[1]
    return [c_kv, k_rope, _rms(cq, p[1])]


Q_GROUP = 8


def _st_q_rope(r, n, p):
    half = Q_GROUP * MLA_QKP
    out = []
    for h in range(Q_GROUP):
        cols = slice(h * MLA_QKP, (h + 1) * MLA_QKP)
        out.append((r[0][:, :half][:, cols] * n[0] + r[0][:, half:][:, cols] * n[1]) * (MLA_QK ** -0.5))
    return [jnp.concatenate(out, axis=1)]


def _st_q_rope_t(r, n, p):
    plain, swapped = [], []
    for h in range(Q_GROUP):
        ct = r[0][:, h * MLA_QKP:(h + 1) * MLA_QKP] * (MLA_QK ** -0.5)
        plain.append(ct * n[0])
        swapped.append(ct * n[1])
    return [jnp.concatenate(plain + swapped, axis=1)]


def _make_st_loss(n_tokens):
    def st(r, n, p):
        h2 = r[0] + _rms(r[1], p[0])
        rows = _row_ids((r[0].shape[0], 1))
        real = jnp.logical_and(rows >= ROW0, rows < ROW0 + n_tokens)
        err = h2 - n[0]
        return [jnp.where(real, 0.5 * jnp.mean(err * err, axis=-1, keepdims=True), 0.0)]
    return st


CONV_BC = 1024
HALO = 16


def _conv_fwd(x, w, *, col_blocks, name, tr=ROW_TILE):
    lp = x.shape[0]

    def body(x_ref, xp_ref, w_ref, o_ref):
        i = pl.program_id(0)
        xv = x_ref[...].astype(f32)
        prev = jnp.where(i > 0, xp_ref[...].astype(f32), 0.0)
        xc = jnp.concatenate([prev, xv], axis=0)
        wv = w_ref[...]
        acc = wv[3:4, :] * xv
        for j in range(3):
            acc = acc + wv[j:j + 1, :] * pltpu.roll(xc, 3 - j, 0)[HALO:, :]
        o_ref[...] = acc.astype(o_ref.dtype)

    return pl.pallas_call(
        body, name=name, grid=(lp // tr, col_blocks),
        in_specs=[pl.BlockSpec((tr, CONV_BC), lambda i, j: (i, j)),
                  pl.BlockSpec((HALO, CONV_BC), lambda i, j: (jnp.maximum(i * (tr // HALO) - 1, 0), j)),
                  pl.BlockSpec((4, CONV_BC), lambda i, j: (0, j))],
        out_specs=pl.BlockSpec((tr, CONV_BC), lambda i, j: (i, j)),
        out_shape=jax.ShapeDtypeStruct((lp, col_blocks * CONV_BC), bf16),
    )(x, x, w)


def _conv_bwd(dc, x, w, *, x_off, w_off, name, tr=ROW_TILE):
    lp, width = dc.shape
    ncb, nrow = width // CONV_BC, lp // tr

    def body(dc_ref, dcn_ref, x_ref, xp_ref, w_ref, dx_ref, dw_ref):
        i = pl.program_id(1)
        nxt = jnp.where(i < nrow - 1, dcn_ref[...], 0.0)
        dcv = dc_ref[...]
        dcc = jnp.concatenate([dcv, nxt], axis=0)
        xv = x_ref[...].astype(f32)
        prev = jnp.where(i > 0, xp_ref[...].astype(f32), 0.0)
        xc = jnp.concatenate([prev, xv], axis=0)
        wv = w_ref[...]
        dx = wv[3:4, :] * dcv
        dws = [None] * 4
        dws[3] = jnp.sum(dcv * xv, axis=0, keepdims=True)
        for j in range(3):
            dx = dx + wv[j:j + 1, :] * pltpu.roll(dcc, tr + 8 - (3 - j), 0)[:tr, :]
            dws[j] = jnp.sum(dcv * pltpu.roll(xc, 3 - j, 0)[HALO:, :], axis=0, keepdims=True)
        dx_ref[...] = dx.astype(dx_ref.dtype)

        @pl.when(i == 0)
        def _():
            for j in range(4):
                dw_ref[j:j + 1, :] = dws[j]

        @pl.when(i > 0)
        def _():
            for j in range(4):
                dw_ref[j:j + 1, :] += dws[j]

    last8 = lp // 8 - 1
    return pl.pallas_call(
        body, name=name, grid=(ncb, nrow),
        in_specs=[pl.BlockSpec((tr, CONV_BC), lambda j, i: (i, j)),
                  pl.BlockSpec((8, CONV_BC), lambda j, i: (jnp.minimum((i + 1) * (tr // 8), last8), j)),
                  pl.BlockSpec((tr, CONV_BC), lambda j, i: (i, j + x_off)),
                  pl.BlockSpec((HALO, CONV_BC), lambda j, i: (jnp.maximum(i * (tr // HALO) - 1, 0), j + x_off)),
                  pl.BlockSpec((4, CONV_BC), lambda j, i: (0, j + w_off))],
        out_specs=[pl.BlockSpec((tr, CONV_BC), lambda j, i: (i, j)),
                   pl.BlockSpec((4, CONV_BC), lambda j, i: (0, j))],
        out_shape=[jax.ShapeDtypeStruct((lp, width), bf16), jax.ShapeDtypeStruct((4, width), f32)],
        compiler_params=pltpu.CompilerParams(dimension_semantics=("arbitrary", "arbitrary")),
    )(dc, dc, x, x, w)


GDN_PACK = 4
GDN_FWD_INTERLEAVE, GDN_BWD_INTERLEAVE = 4, 4


def _bd(x, cb):
    r = x.shape[0]
    tall = jnp.concatenate([x] * GDN_PACK, axis=0)
    rows = lax.broadcasted_iota(jnp.int32, tall.shape, 0) // r
    cols = lax.broadcasted_iota(jnp.int32, tall.shape, 1) // cb
    return jnp.where(rows == cols, tall, jnp.zeros_like(tall))


def _diag(full, r, cb):
    cols = lax.broadcasted_iota(jnp.int32, (r, full.shape[1]), 1) // cb
    out = jnp.where(cols == 0, full[0:r, :], 0.0)
    for a in range(1, GDN_PACK):
        out = out + jnp.where(cols == a, full[a * r:(a + 1) * r, :], 0.0)
    return out


def _stack(x, cb):
    return jnp.concatenate([x[:, a * cb:(a + 1) * cb] for a in range(GDN_PACK)], axis=0)


def _make_packed(dot):
    @jax.custom_vjp
    def pmm(x, y):
        return dot(x, _bd(y, y.shape[1] // GDN_PACK), NN)

    @jax.custom_vjp
    def pnt(x, y):
        k = x.shape[1] // GDN_PACK
        return _diag(dot(_stack(x, k), _stack(y, k), NT), x.shape[0], y.shape[0])

    @jax.custom_vjp
    def ptn(x, y):
        return _diag(dot(x, y, TN), x.shape[1] // GDN_PACK, y.shape[1] // GDN_PACK)

    def pmm_bwd(res, ct):
        x, y = res
        cb = y.shape[1] // GDN_PACK
        return dot(ct, _bd(y, cb), NT), _diag(dot(x, ct, TN), y.shape[0], cb)

    pmm.defvjp(lambda x, y: (pmm(x, y), (x, y)), pmm_bwd)
    pnt.defvjp(lambda x, y: (pnt(x, y), (x, y)), lambda res, ct: (pmm(ct, res[1]), ptn(ct, res[0])))
    ptn.defvjp(lambda x, y: (ptn(x, y), (x, y)), lambda res, ct: (pnt(res[1], ct), pmm(res[0], ct)))
    return pmm, pnt, ptn


_pmm, _pnt, _ptn = _make_packed(_dlo)


@jax.custom_vjp
def _inv_packed(ms):
    c = ms[0].shape[0]
    ii = lax.broadcasted_iota(jnp.int32, ms[0].shape, 0)
    jj = lax.broadcasted_iota(jnp.int32, ms[0].shape, 1) % c
    ts = [jnp.where(ii == jj, 1.0, 0.0) - m for m in ms]
    ps = [(-m).astype(bf16) for m in ms]
    for _ in range(int(math.log2(c)) - 1):
        ps = [_dlo(p, _bd(p, c), NN).astype(bf16) for p in ps]
        ts = [t + _dlo(t, _bd(p, c), NN) for t, p in zip(ts, ps)]
    return tuple(ts)


def _inv_packed_fwd(ms):
    ts = _inv_packed(ms)
    return ts, ts


def _inv_packed_bwd(ts, cts):
    c = ts[0].shape[0]
    ys = [_diag(_dlo(t, ct, TN), c, c) for t, ct in zip(ts, cts)]
    return (tuple(-_dlo(y, _bd(t.astype(bf16), c), NT) for y, t in zip(ys, ts)),)


_inv_packed.defvjp(_inv_packed_fwd, _inv_packed_bwd)


def _gdn_prep(q2, k2, v4, bcols, gcols, grows):
    c, d = v4.shape[0], GDN_D
    q4 = jnp.concatenate([q2[:, :d], q2[:, :d], q2[:, d:], q2[:, d:]], axis=1)
    k4 = jnp.concatenate([k2[:, :d], k2[:, :d], k2[:, d:], k2[:, d:]], axis=1)
    beta4 = jnp.concatenate([jnp.broadcast_to(b, (c, d)) for b in bcols], axis=1)
    gc4 = jnp.concatenate([jnp.broadcast_to(g, (c, d)) for g in gcols], axis=1)
    low = lax.broadcasted_iota(jnp.int32, (c, 128), 1) < c
    gi = jnp.concatenate([jnp.where(low, gcols[0], gcols[1]), jnp.where(low, gcols[2], gcols[3])], axis=1)
    gj = jnp.concatenate([jnp.where(low, grows[0], grows[1]), jnp.where(low, grows[2], grows[3])], axis=1)
    ii = lax.broadcasted_iota(jnp.int32, gi.shape, 0)
    jj = lax.broadcasted_iota(jnp.int32, gi.shape, 1) % c
    dec = jnp.exp(jnp.where(ii >= jj, gi - gj, NEG))
    rid = lax.broadcasted_iota(jnp.int32, gc4.shape, 0)
    glast = jnp.sum(jnp.where(rid == c - 1, gc4, 0.0), axis=0, keepdims=True)
    eg = jnp.exp(gc4)
    kb = k4 * beta4
    return dict(q=q4, k=k4, kb=kb, vb=v4 * beta4, kbe=kb * eg, qe=q4 * eg, dec=dec, dec_strict=jnp.where(ii > jj, dec, 0.0),
                sdecay=jnp.exp(glast), kd=k4 * jnp.exp(glast - gc4))


@jax.custom_vjp
def _inv_packed_known(ms, ts):
    return ts


_inv_packed_known.defvjp(lambda ms, ts: (ts, ts),
                         lambda ts, cts: (_inv_packed_bwd(ts, cts)[0], tuple(jnp.zeros_like(t) for t in ts)))


def _gdn_groups(groups, known_inverses=None, with_inverses=False):
    c = groups[0][3].shape[0]
    ss = [g[0] for g in groups]
    e = [_gdn_prep(*g[1:]) for g in groups]
    ms = tuple(_pnt(x["kb"], x["k"]) * x["dec_strict"] for x in e)
    ts = _inv_packed(ms) if known_inverses is None else _inv_packed_known(ms, tuple(known_inverses))
    us = [_pmm(t, x["vb"]) for t, x in zip(ts, e)]
    ws = [_pmm(t, x["kbe"]) for t, x in zip(ts, e)]
    attns = [_pnt(x["q"], x["k"]) * x["dec"] for x in e]
    ws_qs = [_pmm(jnp.concatenate([w, x["qe"]], axis=0), s) for w, x, s in zip(ws, e, ss)]
    v_news = [u - y[:c] for u, y in zip(us, ws_qs)]
    os = [y[c:] + _pmm(a, vn) for y, a, vn in zip(ws_qs, attns, v_news)]
    s_news = [s * x["sdecay"] + _ptn(x["kd"], vn) for s, x, vn in zip(ss, e, v_news)]
    if with_inverses:
        return list(zip(os, s_news)), list(ts)
    return list(zip(os, s_news))


def _lane_pick(x, h):
    lane = lax.broadcasted_iota(jnp.int32, x.shape, 1)
    return jnp.sum(jnp.where(lane == h, x, 0.0), axis=1, keepdims=True)


def _cum_log_decay(g):
    c = g.shape[0]
    lower = (lax.broadcasted_iota(jnp.int32, (c, c), 0) >= lax.broadcasted_iota(jnp.int32, (c, c), 1)).astype(f32)
    upper2 = (lax.broadcasted_iota(jnp.int32, (c, 128), 0) <= lax.broadcasted_iota(jnp.int32, (c, 128), 1) % c).astype(f32)
    return _dhi(lower, g, NN), _dhi(g, upper2, TN)


def _group_operands(gi, s_ref, q_ref, k_ref, v_ref, bv, gcv, gct_s):
    heads = [gi * GDN_PACK + u for u in range(GDN_PACK)]
    qk_off = pl.multiple_of(gi * 2 * GDN_D, 2 * GDN_D)
    v_off = pl.multiple_of(gi * GDN_PACK * GDN_D, GDN_PACK * GDN_D)
    return (s_ref[gi], q_ref[:, pl.ds(qk_off, 2 * GDN_D)].astype(f32), k_ref[:, pl.ds(qk_off, 2 * GDN_D)].astype(f32),
            v_ref[:, pl.ds(v_off, GDN_PACK * GDN_D)].astype(f32),
            [_lane_pick(bv, h) for h in heads], [_lane_pick(gcv, h) for h in heads],
            [gct_s[pl.ds(h, 1), :] for h in heads]), heads, qk_off, v_off


def _gdn_fwd(qn, kn, v, beta, g, *, n_real_chunks, name):
    lp = qn.shape[0]
    nchunk = lp // GDN_CHUNK
    C, D = GDN_CHUNK, GDN_D
    NG, SW = GDN_V_HEADS // GDN_PACK, GDN_PACK * GDN_D

    def body(q_ref, k_ref, v_ref, b_ref, g_ref, o_ref, st_ref, inv_ref, s_s, gc_s, gct_s):
        ci = pl.program_id(0)

        @pl.when(ci == 0)
        def _():
            s_s[...] = jnp.zeros_like(s_s)

        @pl.when(ci >= n_real_chunks)
        def _():
            o_ref[...] = jnp.zeros_like(o_ref)
            st_ref[...] = jnp.zeros_like(st_ref)
            inv_ref[...] = jnp.zeros_like(inv_ref)

        @pl.when(ci < n_real_chunks)
        def _():
            gc, gct = _cum_log_decay(g_ref[...])
            gc_s[...] = gc
            gct_s[...] = gct

            def some_groups(it, carry):
                ids = [it * GDN_FWD_INTERLEAVE + u for u in range(GDN_FWD_INTERLEAVE)]
                ops = [_group_operands(gi, s_s, q_ref, k_ref, v_ref, b_ref[...], gc_s[...], gct_s) for gi in ids]
                res, inverses = _gdn_groups([op[0] for op in ops], with_inverses=True)
                for gi, op, (o, s_new), t in zip(ids, ops, res, inverses):
                    st_ref[gi] = op[0][0]
                    inv_ref[gi] = t
                    s_s[gi] = s_new
                    o_ref[:, pl.ds(op[3], SW)] = o
                return carry

            lax.fori_loop(0, NG // GDN_FWD_INTERLEAVE, some_groups, 0)

    return pl.pallas_call(
        body, name=name, grid=(nchunk,),
        in_specs=[pl.BlockSpec((C, GDN_QK_W), lambda c: (c, 0)), pl.BlockSpec((C, GDN_QK_W), lambda c: (c, 0)),
                  pl.BlockSpec((C, GDN_V_W), lambda c: (c, 0)), pl.BlockSpec((C, 128), lambda c: (c, 0)),
                  pl.BlockSpec((C, 128), lambda c: (c, 0))],
        out_specs=[pl.BlockSpec((C, GDN_V_W), lambda c: (c, 0)),
                   pl.BlockSpec((None, NG, D, SW), lambda c: (c, 0, 0, 0)),
                   pl.BlockSpec((None, NG, C, GDN_PACK * C), lambda c: (c, 0, 0, 0))],
        out_shape=[jax.ShapeDtypeStruct((lp, GDN_V_W), f32), jax.ShapeDtypeStruct((nchunk, NG, D, SW), f32),
                   jax.ShapeDtypeStruct((nchunk, NG, C, GDN_PACK * C), f32)],
        scratch_shapes=[pltpu.VMEM((NG, D, SW), f32), pltpu.VMEM((C, 128), f32), pltpu.VMEM((128, 128), f32)],
        compiler_params=pltpu.CompilerParams(dimension_semantics=("arbitrary",)),
    )(qn, kn, v, beta, g)


def _gdn_bwd(qn, kn, v, beta, g, states, inverses, do, *, n_real_chunks, name):
    lp = qn.shape[0]
    nchunk = lp // GDN_CHUNK
    C, D = GDN_CHUNK, GDN_D
    NG, SW = GDN_V_HEADS // GDN_PACK, GDN_PACK * GDN_D
    rev = lambda i: (nchunk - 1 - i, 0)

    def body(q_ref, k_ref, v_ref, b_ref, g_ref, st_ref, inv_ref, do_ref,
             dq_ref, dk_ref, dv_ref, db_ref, dg_ref, ds_s, gc_s, gct_s, dgc_s, dgct_s, dbeta_s):
        step = pl.program_id(0)
        ci = nchunk - 1 - step

        @pl.when(step == 0)
        def _():
            ds_s[...] = jnp.zeros_like(ds_s)

        @pl.when(ci >= n_real_chunks)
        def _():
            for r in (dq_ref, dk_ref, dv_ref, db_ref, dg_ref):
                r[...] = jnp.zeros_like(r)

        @pl.when(ci < n_real_chunks)
        def _():
            gc, gct = _cum_log_decay(g_ref[...])
            gc_s[...] = gc
            gct_s[...] = gct
            dgc_s[...] = jnp.zeros_like(dgc_s)
            dgct_s[...] = jnp.zeros_like(dgct_s)
            dbeta_s[...] = jnp.zeros_like(dbeta_s)

            def some_groups(it, carry):
                ids = [it * GDN_BWD_INTERLEAVE + u for u in range(GDN_BWD_INTERLEAVE)]
                ops = [_group_operands(gi, st_ref, q_ref, k_ref, v_ref, b_ref[...], gc_s[...], gct_s) for gi in ids]
                cts = [(do_ref[:, pl.ds(op[3], SW)], ds_s[gi]) for gi, op in zip(ids, ops)]
                known = [inv_ref[gi] for gi in ids]
                _, vjp_fn = jax.vjp(lambda gs: _gdn_groups(gs, known_inverses=known), [op[0] for op in ops])
                (grads,) = vjp_fn(cts)
                lane = lax.broadcasted_iota(jnp.int32, (C, 128), 1)
                dbeta_acc, dgc_acc = dbeta_s[...], dgc_s[...]
                for gi, (_, heads, qk_off, v_off), (dsp, dq2, dk2, dv4, dbcols, dgcols, dgrows) in zip(ids, ops, grads):
                    ds_s[gi] = dsp
                    dq_ref[:, pl.ds(qk_off, 2 * D)] = dq2
                    dk_ref[:, pl.ds(qk_off, 2 * D)] = dk2
                    dv_ref[:, pl.ds(v_off, SW)] = dv4
                    for h, dbcol, dgcol, dgrow in zip(heads, dbcols, dgcols, dgrows):
                        dbeta_acc = dbeta_acc + jnp.where(lane == h, dbcol, 0.0)
                        dgc_acc = dgc_acc + jnp.where(lane == h, dgcol, 0.0)
                        dgct_s[pl.ds(h, 1), :] = dgrow
                dbeta_s[...] = dbeta_acc
                dgc_s[...] = dgc_acc
                return carry

            lax.fori_loop(0, NG // GDN_BWD_INTERLEAVE, some_groups, 0)
            fold = (lax.broadcasted_iota(jnp.int32, (128, C), 0) % C == lax.broadcasted_iota(jnp.int32, (128, C), 1)).astype(f32)
            eye = (lax.broadcasted_iota(jnp.int32, (128, 128), 0) == lax.broadcasted_iota(jnp.int32, (128, 128), 1)).astype(f32)
            dgc = dgc_s[...] + _dhi(_dhi(dgct_s[...], fold, NN), eye, TN)
            upper = (lax.broadcasted_iota(jnp.int32, (C, C), 0) <= lax.broadcasted_iota(jnp.int32, (C, C), 1)).astype(f32)
            dg_ref[...] = _dhi(upper, dgc, NN)
            db_ref[...] = dbeta_s[...]

    return pl.pallas_call(
        body, name=name, grid=(nchunk,),
        in_specs=[pl.BlockSpec((C, GDN_QK_W), rev), pl.BlockSpec((C, GDN_QK_W), rev), pl.BlockSpec((C, GDN_V_W), rev),
                  pl.BlockSpec((C, 128), rev), pl.BlockSpec((C, 128), rev),
                  pl.BlockSpec((None, NG, D, SW), lambda i: (nchunk - 1 - i, 0, 0, 0)),
                  pl.BlockSpec((None, NG, C, GDN_PACK * C), lambda i: (nchunk - 1 - i, 0, 0, 0)), pl.BlockSpec((C, GDN_V_W), rev)],
        out_specs=[pl.BlockSpec((C, GDN_QK_W), rev), pl.BlockSpec((C, GDN_QK_W), rev), pl.BlockSpec((C, GDN_V_W), rev),
                   pl.BlockSpec((C, 128), rev), pl.BlockSpec((C, 128), rev)],
        out_shape=[jax.ShapeDtypeStruct((lp, GDN_QK_W), f32)] * 2 + [jax.ShapeDtypeStruct((lp, GDN_V_W), f32)]
        + [jax.ShapeDtypeStruct((lp, 128), f32)] * 2,
        scratch_shapes=[pltpu.VMEM((NG, D, SW), f32), pltpu.VMEM((C, 128), f32), pltpu.VMEM((128, 128), f32),
                        pltpu.VMEM((C, 128), f32), pltpu.VMEM((128, 128), f32), pltpu.VMEM((C, 128), f32)],
        compiler_params=pltpu.CompilerParams(dimension_semantics=("arbitrary",)),
    )(qn, kn, v, beta, g, states, inverses, do)


def _att_mask_t(k0, q0, tk, tq):
    kcol = k0 + lax.broadcasted_iota(jnp.int32, (tk, tq), 0)
    qrow = q0 + lax.broadcasted_iota(jnp.int32, (tk, tq), 1)
    return jnp.logical_and(qrow >= kcol, kcol >= PAD_FRONT)


def _att_tiles(lp):
    tq = _pick(lp, (768, 512, 256))
    return tq, _pick(tq, ATT_KEY_TILES)


def _attention_fwd(q, kvu, kr, proj2, *, name):
    lp = q.shape[0]
    H = MLA_HEADS
    tq, tk = _att_tiles(lp)
    r = tq // tk

    def body(q_ref, kn_ref, kr_ref, v_ref, z_ref, o_ref, lse_ref, og_ref, m_s, l_s, acc_s, sa_s, sb_s):
        qi = pl.program_id(1)
        m_s[...] = jnp.full_like(m_s, NEG)
        l_s[...] = jnp.zeros_like(l_s)
        acc_s[...] = jnp.zeros_like(acc_s)

        def scores(ki):
            k0 = pl.multiple_of(ki * tk, tk)
            k = jnp.concatenate([kn_ref[pl.ds(k0, tk), :], kr_ref[pl.ds(k0, tk), :]], axis=1)
            return lax.dot_general(k, q_ref[...], (NT, ((), ())), preferred_element_type=f32)

        def consume(st, ki, masked):
            k0 = pl.multiple_of(ki * tk, tk)
            if masked:
                st = jnp.where(_att_mask_t(k0, qi * tq, tk, tq), st, NEG)
            m_prev = m_s[...]
            m_new = jnp.maximum(m_prev, jnp.max(st, axis=0, keepdims=True))
            alpha = jnp.exp(m_prev - m_new)
            p = jnp.exp(st - m_new)
            l_s[...] = alpha * l_s[...] + jnp.sum(p, axis=0, keepdims=True)
            acc_s[...] = alpha * acc_s[...] + lax.dot_general(v_ref[pl.ds(k0, tk), :], p.astype(bf16), (TN, ((), ())),
                                                              preferred_element_type=f32)
            m_s[...] = m_new

        n_full = qi * r

        def chain(blocks):
            bufs = (sa_s, sb_s)
            for j, (ki, masked) in enumerate(blocks):
                if j + 1 < len(blocks):
                    bufs[(j + 1) % 2][...] = scores(blocks[j + 1][0])
                consume(bufs[j % 2][...], ki, masked)

        diagonal = [(n_full + d, True) for d in range(r)]

        @pl.when(qi == 0)
        def _():
            sa_s[...] = scores(0)
            chain(diagonal)

        @pl.when(qi > 0)
        def _():
            sb_s[...] = scores(0)
            sa_s[...] = scores(1)
            consume(sb_s[...], 0, True)
            n_pairs = (n_full - 1) // 2

            def two(pi, carry):
                ki = 1 + 2 * pi
                sb_s[...] = scores(ki + 1)
                consume(sa_s[...], ki, False)
                sa_s[...] = scores(ki + 2)
                consume(sb_s[...], ki + 1, False)
                return carry

            lax.fori_loop(0, n_pairs, two, 0)
            nxt = 1 + 2 * n_pairs

            @pl.when(nxt < n_full)
            def _():
                chain([(nxt, False)] + diagonal)

            @pl.when(nxt == n_full)
            def _():
                chain(diagonal)
        o = jnp.transpose(acc_s[...] / l_s[...])
        o_ref[...] = o
        og_ref[...] = (o * _silu(z_ref[...].astype(f32))).astype(og_ref.dtype)
        lse_ref[...] = m_s[...] + jnp.log(l_s[...])

    return pl.pallas_call(
        body, name=name, grid=(H, lp // tq),
        in_specs=[pl.BlockSpec((tq, MLA_QKP), lambda h, qi: (qi, h)),
                  pl.BlockSpec((lp, 128), lambda h, qi: (0, h)),
                  pl.BlockSpec((lp, 128), lambda h, qi: (0, 0)),
                  pl.BlockSpec((lp, 128), lambda h, qi: (0, H + h)),
                  pl.BlockSpec((tq, 128), lambda h, qi: (qi, MLA_Q_RANK // 128 + h))],
        out_specs=[pl.BlockSpec((tq, 128), lambda h, qi: (qi, h)),
                   pl.BlockSpec((None, 1, tq), lambda h, qi: (h, 0, qi)),
                   pl.BlockSpec((tq, 128), lambda h, qi: (qi, h))],
        out_shape=[jax.ShapeDtypeStruct((lp, MLA_V_W), f32), jax.ShapeDtypeStruct((H, 1, lp), f32),
                   jax.ShapeDtypeStruct((lp, MLA_V_W), bf16)],
        scratch_shapes=[pltpu.VMEM((1, tq), f32), pltpu.VMEM((1, tq), f32), pltpu.VMEM((128, tq), f32),
                        pltpu.VMEM((tk, tq), f32), pltpu.VMEM((tk, tq), f32)],
        compiler_params=pltpu.CompilerParams(dimension_semantics=("arbitrary", "arbitrary")),
    )(q, kvu, kr, kvu, proj2)


def _gate_bwd(o, proj2, dgated, *, name):
    lp = o.shape[0]
    H, w = MLA_HEADS, 2 * MLA_V
    tq = _pick(lp, (768, 512, 256))

    def body(o_ref, z_ref, g_ref, do_ref, dz_ref, dl_ref):
        ov, z, g = o_ref[...], z_ref[...].astype(f32), g_ref[...]
        s = _sigmoid(z)
        do = (g * (z * s)).astype(bf16)
        do_ref[...] = do
        dz_ref[...] = (g * ov * (s * (1.0 + z * (1.0 - s)))).astype(dz_ref.dtype)
        prod = ov * do.astype(f32)
        for u in range(2):
            dl_ref[u] = jnp.sum(jnp.transpose(prod[:, u * MLA_V:(u + 1) * MLA_V]), axis=0, keepdims=True)

    blk = pl.BlockSpec((tq, w), lambda j, qi: (qi, j))
    return pl.pallas_call(
        body, name=name, grid=(H // 2, lp // tq),
        in_specs=[blk, pl.BlockSpec((tq, w), lambda j, qi: (qi, j + MLA_Q_RANK // w)), blk],
        out_specs=[blk, blk, pl.BlockSpec((2, 1, tq), lambda j, qi: (j, 0, qi))],
        out_shape=[jax.ShapeDtypeStruct((lp, MLA_V_W), bf16), jax.ShapeDtypeStruct((lp, MLA_V_W), bf16),
                   jax.ShapeDtypeStruct((H, 1, lp), f32)],
    )(o, proj2, dgated)


def _attention_bwd(q, kvu, kr, lse, delta, do, *, name):
    lp = q.shape[0]
    tq, t = _att_tiles(lp)
    H, nb = MLA_HEADS, lp // t
    r, nq = tq // t, lp // tq

    def body(q_ref, kn_ref, kr_ref, v_ref, lse_ref, dl_ref, do_ref, dq_ref, dkn_ref, dv_ref, dkr_ref, dk_s, dv_s,
             sa_s, da_s, sb_s, db_s):
        ki = pl.program_id(1)
        k0 = ki * t
        k = jnp.concatenate([kn_ref[...], kr_ref[...]], axis=1)
        vv = v_ref[...]
        dk_s[...] = jnp.zeros_like(dk_s)
        dv_s[...] = jnp.zeros_like(dv_s)

        def products(qi, s_ref, d_ref):
            q0 = pl.multiple_of(qi * tq, tq)
            s_ref[...] = lax.dot_general(k, q_ref[pl.ds(q0, tq), :], (NT, ((), ())), preferred_element_type=f32)
            d_ref[...] = lax.dot_general(vv, do_ref[pl.ds(q0, tq), :], (NT, ((), ())), preferred_element_type=f32)

        def accumulate(s_ref, d_ref, qi, masked, first):
            q0 = pl.multiple_of(qi * tq, tq)
            qv = q_ref[pl.ds(q0, tq), :]
            dob = do_ref[pl.ds(q0, tq), :]
            st = s_ref[...]
            if masked:
                st = jnp.where(_att_mask_t(k0, q0, t, tq), st, NEG)
            p = jnp.exp(st - lse_ref[:, pl.ds(q0, tq)])
            dv_s[...] += jnp.dot(p.astype(bf16), dob, preferred_element_type=f32)
            ds = (p * (d_ref[...] - dl_ref[:, pl.ds(q0, tq)])).astype(bf16)
            dk_s[...] += jnp.dot(ds, qv, preferred_element_type=f32)
            dq = lax.dot_general(ds, k, (TN, ((), ())), preferred_element_type=f32)
            if first:
                dq_ref[pl.ds(q0, tq), :] = dq
            else:
                dq_ref[pl.ds(q0, tq), :] += dq

        def sweep(qd, mask_all, first):
            last = nq - 1
            products(qd, sa_s, da_s)
            products(jnp.minimum(qd + 1, last), sb_s, db_s)
            accumulate(sa_s, da_s, qd, True, first)
            n = last - qd

            def two(pi, carry):
                i = qd + 1 + 2 * pi
                products(i + 1, sa_s, da_s)
                accumulate(sb_s, db_s, i, mask_all, first)
                products(jnp.minimum(i + 2, last), sb_s, db_s)
                accumulate(sa_s, da_s, i + 1, mask_all, first)
                return carry

            lax.fori_loop(0, n // 2, two, 0)

            @pl.when(n % 2 == 1)
            def _():
                accumulate(sb_s, db_s, last, mask_all, first)

        @pl.when(ki == 0)
        def _():
            sweep(0, True, True)

        @pl.when(ki > 0)
        def _():
            sweep(ki // r, False, False)

        dkn_ref[...] = dk_s[:, :128].astype(dkn_ref.dtype)
        dkr_ref[...] = dk_s[:, 128:]
        dv_ref[...] = dv_s[...].astype(dv_ref.dtype)

    return pl.pallas_call(
        body, name=name, grid=(H, nb),
        in_specs=[pl.BlockSpec((lp, MLA_QKP), lambda h, ki: (0, h)),
                  pl.BlockSpec((t, 128), lambda h, ki: (ki, h)),
                  pl.BlockSpec((t, 128), lambda h, ki: (ki, 0)),
                  pl.BlockSpec((t, 128), lambda h, ki: (ki, H + h)),
                  pl.BlockSpec((None, 1, lp), lambda h, ki: (h, 0, 0)),
                  pl.BlockSpec((None, 1, lp), lambda h, ki: (h, 0, 0)),
                  pl.BlockSpec((lp, 128), lambda h, ki: (0, h))],
        out_specs=[pl.BlockSpec((lp, MLA_QKP), lambda h, ki: (0, h)),
                   pl.BlockSpec((t, 128), lambda h, ki: (ki, h)),
                   pl.BlockSpec((t, 128), lambda h, ki: (ki, h)),
                   pl.BlockSpec((t, 128), lambda h, ki: (ki, h))],
        out_shape=[jax.ShapeDtypeStruct((lp, H * MLA_QKP), f32), jax.ShapeDtypeStruct((lp, MLA_V_W), bf16),
                   jax.ShapeDtypeStruct((lp, MLA_V_W), bf16), jax.ShapeDtypeStruct((lp, MLA_V_W), f32)],
        scratch_shapes=[pltpu.VMEM((t, MLA_QKP), f32), pltpu.VMEM((t, 128), f32)] + [pltpu.VMEM((t, tq), f32)] * 4,
        compiler_params=pltpu.CompilerParams(dimension_semantics=("arbitrary", "arbitrary")),
    )(q, kvu, kr, kvu, lse, delta, do)


def _q_proj_fwd(c_q, wq2, cos_q, sin_q, *, name, tm=ROW_TILE):
    lp, rank = c_q.shape
    gw = Q_GROUP * MLA_QKP
    ng = wq2.shape[1] // (2 * gw)

    def body(c_ref, w_ref, cos_ref, sin_ref, o_ref):
        qq = jnp.dot(c_ref[...], w_ref[...], preferred_element_type=f32)
        o_ref[...] = _st_q_rope([qq], [cos_ref[...], sin_ref[...]], [])[0].astype(o_ref.dtype)

    tab = pl.BlockSpec((tm, MLA_QKP), lambda j, i: (i, 0))
    return pl.pallas_call(
        body, name=name, grid=(ng, lp // tm),
        in_specs=[pl.BlockSpec((tm, rank), lambda j, i: (i, 0)), pl.BlockSpec((rank, 2 * gw), lambda j, i: (0, j)), tab, tab],
        out_specs=pl.BlockSpec((tm, gw), lambda j, i: (i, j)),
        out_shape=jax.ShapeDtypeStruct((lp, ng * gw), bf16),
    )(c_q, wq2, cos_q, sin_q)


def _q_proj_bwd(dq, c_q, wq2, cos_q, sin_q, *, name, tm=ROW_TILE):
    lp, rank = c_q.shape
    gw = Q_GROUP * MLA_QKP
    ng = wq2.shape[1] // (2 * gw)

    def body(dq_ref, c_ref, w_ref, cos_ref, sin_ref, dc_ref, dw_ref):
        dqq = _st_q_rope_t([dq_ref[...]], [cos_ref[...], sin_ref[...]], [])[0].astype(bf16)
        dc_ref[...] = lax.dot_general(dqq, w_ref[...], (NT, ((), ())), preferred_element_type=f32)
        dw = lax.dot_general(c_ref[...], dqq, (TN, ((), ())), preferred_element_type=f32)

        @pl.when(pl.program_id(1) == 0)
        def _():
            dw_ref[...] = dw

        @pl.when(pl.program_id(1) > 0)
        def _():
            dw_ref[...] += dw

    tab = pl.BlockSpec((tm, MLA_QKP), lambda j, i: (i, 0))
    return pl.pallas_call(
        body, name=name, grid=(ng, lp // tm),
        in_specs=[pl.BlockSpec((tm, gw), lambda j, i: (i, j)), pl.BlockSpec((tm, rank), lambda j, i: (i, 0)),
                  pl.BlockSpec((rank, 2 * gw), lambda j, i: (0, j)), tab, tab],
        out_specs=[pl.BlockSpec((tm, rank), lambda j, i: (i, j)), pl.BlockSpec((rank, 2 * gw), lambda j, i: (0, j))],
        out_shape=[jax.ShapeDtypeStruct((lp, ng * rank), f32), jax.ShapeDtypeStruct(wq2.shape, f32)],
        compiler_params=pltpu.CompilerParams(dimension_semantics=("arbitrary", "arbitrary")),
    )(dq, c_q, wq2, cos_q, sin_q)


def _exchange_copies(x_ref, o_ref, send_sems, recv_sems, local_sem, gather):
    mx, my, mc = lax.axis_index("x"), lax.axis_index("y"), lax.axis_index("c")
    me = 4 * mx + 2 * my + mc
    own = pltpu.make_async_copy(x_ref if gather else x_ref.at[me], o_ref.at[me], local_sem)
    sends, arrivals = [], []
    for k in range(1, N_DEV):
        px = 1 - mx if k & 4 else mx
        py = 1 - my if k & 2 else my
        pc = 1 - mc if k & 1 else mc
        peer = 4 * px + 2 * py + pc
        sends.append(pltpu.make_async_remote_copy(
            src_ref=x_ref if gather else x_ref.at[peer], dst_ref=o_ref.at[me],
            send_sem=send_sems.at[k - 1], recv_sem=recv_sems.at[k - 1],
            device_id=(px, py, pc), device_id_type=MESH_ID))
        arrivals.append(pltpu.make_async_remote_copy(
            src_ref=o_ref.at[peer], dst_ref=o_ref.at[peer],
            send_sem=send_sems.at[k - 1], recv_sem=recv_sems.at[k - 1],
            device_id=(mx, my, mc), device_id_type=MESH_ID))
    return own, sends, arrivals


def _exchange_start(copies):
    own, sends, _ = copies
    own.start()
    for cp in sends:
        cp.start()


def _exchange_wait(copies):
    own, sends, arrivals = copies
    for cp in arrivals:
        cp.wait_recv()
    for cp in sends:
        cp.wait_send()
    own.wait()


_EXCHANGE_SCRATCH = [pltpu.SemaphoreType.DMA((N_DEV - 1,)), pltpu.SemaphoreType.DMA((N_DEV - 1,)), pltpu.SemaphoreType.DMA]


def _exchange(x, *, gather, name):
    blk = x.shape if gather else x.shape[1:]

    def body(x_ref, o_ref, send_sems, recv_sems, local_sem):
        copies = _exchange_copies(x_ref, o_ref, send_sems, recv_sems, local_sem, gather)
        _exchange_start(copies)
        _exchange_wait(copies)

    return pl.pallas_call(
        body, name=name,
        in_specs=[pl.BlockSpec(memory_space=pltpu.HBM)], out_specs=pl.BlockSpec(memory_space=pltpu.HBM),
        out_shape=jax.ShapeDtypeStruct((N_DEV,) + tuple(blk), x.dtype),
        scratch_shapes=list(_EXCHANGE_SCRATCH),
    )(x)


def _reduce_adamw(parts, w, m, v, *, name):
    r = w.shape[0]
    tr = max(d for d in range(16, 3201, 16) if r % d == 0)

    def body(p_ref, w_ref, m_ref, v_ref, g_ref, d_ref, nm_ref, nv_ref):
        g = p_ref[0].astype(f32)
        for s in range(1, N_DEV):
            g = g + p_ref[s].astype(f32)
        mm = ADAM_B1 * m_ref[...] + (1.0 - ADAM_B1) * g
        vv = ADAM_B2 * v_ref[...] + (1.0 - ADAM_B2) * (g * g)
        m_hat = mm / (1.0 - ADAM_B1 ** ADAM_STEP)
        v_hat = vv / (1.0 - ADAM_B2 ** ADAM_STEP)
        g_ref[...] = g
        d_ref[...] = -ADAM_LR * (m_hat / (jnp.sqrt(v_hat) + ADAM_EPS) + ADAM_WD * w_ref[...])
        nm_ref[...] = mm
        nv_ref[...] = vv

    spec = pl.BlockSpec((tr, 128), lambda i: (i, 0))
    return pl.pallas_call(
        body, name=name, grid=(r // tr,),
        in_specs=[pl.BlockSpec((N_DEV, tr, 128), lambda i: (0, i, 0)), spec, spec, spec],
        out_specs=[spec] * 4, out_shape=[jax.ShapeDtypeStruct((r, 128), f32)] * 4,
    )(parts, w, m, v)


_SHARDED = ("gdn_w_in", "gdn_w_out", "kv_w_down", "kv_w_up", "mla_w_in", "mla_w_q_up", "mla_w_out", "meta_tokens", "gdn_conv_w")
_COL_SHARDED = {"gdn_w_in", "kv_w_up", "mla_w_in", "mla_w_q_up", "meta_tokens", "gdn_conv_w"}
_GATHER_FIRST = ("gdn_w_in",)
_GATHER_F32 = ("meta_tokens", "gdn_conv_w")
_GATHER_REST = ("gdn_w_out", "kv_w_down", "kv_w_up", "mla_w_in", "mla_w_q_up", "mla_w_out")
_SCATTER_EARLY = ("gdn_w_out", "kv_w_down", "kv_w_up", "mla_w_in", "mla_w_q_up", "mla_w_out")
_SCATTER_LATE = ("gdn_w_in", "meta_tokens", "gdn_conv_w")
_REPLICATED = ("pre_norm", "post_norm", "gdn_a_log", "gdn_dt_bias", "gdn_out_norm", "kv_norm", "kv_latent_norm",
               "mla_q_latent_norm")


def _rows128(a):
    flat = a.reshape(-1)
    pad = (-flat.shape[0]) % 128
    if pad:
        flat = jnp.pad(flat, (0, pad))
    return flat.reshape(-1, 128)


def _pack(arrs, row_multiple):
    parts = [_rows128(a) for a in arrs]
    buf = jnp.concatenate(parts, axis=0)
    pad = (-buf.shape[0]) % row_multiple
    if pad:
        buf = jnp.pad(buf, ((0, pad), (0, 0)))
    return buf


def _unpack(buf, shapes):
    out, r = [], 0
    for shp in shapes:
        n = math.prod(shp)
        rows = -(-n // 128)
        out.append(buf[r:r + rows].reshape(-1)[:n].reshape(shp))
        r += rows
    return out


def _unshard(g, full_shape, col):
    if col:
        return jnp.transpose(g, (1, 0, 2)).reshape(full_shape)
    return g.reshape(full_shape)


def _to_shards(a, col):
    r, c = a.shape
    if col:
        return jnp.transpose(a.reshape(r, N_DEV, c // N_DEV), (1, 0, 2))
    return a.reshape(N_DEV, r // N_DEV, c)


def _pad_cols(a, width):
    return jnp.pad(a, ((0, 0), (0, width - a.shape[1])))


def _rope_tables(lp):
    inv = ROPE_THETA ** (-jnp.arange(0, MLA_ROPE, 2, dtype=f32) / MLA_ROPE)
    pos = (jnp.arange(lp, dtype=jnp.int32) - PAD_FRONT).astype(f32)
    ang = pos[:, None] * inv[None, :]
    cos, sin = jnp.cos(ang), jnp.sin(ang)
    z = jnp.zeros((lp, 64), f32)
    return jnp.concatenate([cos, cos, z], axis=1), jnp.concatenate([-sin, sin, z], axis=1)


def kernel(x, meta_tokens, pre_norm, post_norm, gdn_w_in, gdn_conv_w, gdn_a_log, gdn_dt_bias, gdn_out_norm, gdn_w_out, kv_norm, kv_w_down, kv_latent_norm, kv_w_up, mla_w_in, mla_q_latent_norm, mla_w_q_up, mla_w_out, loss_target, m_meta_tokens, m_pre_norm, m_post_norm, m_gdn_w_in, m_gdn_conv_w, m_gdn_a_log, m_gdn_dt_bias, m_gdn_out_norm, m_gdn_w_out, m_kv_norm, m_kv_w_down, m_kv_latent_norm, m_kv_w_up, m_mla_w_in, m_mla_q_latent_norm, m_mla_w_q_up, m_mla_w_out, v_meta_tokens, v_pre_norm, v_post_norm, v_gdn_w_in, v_gdn_conv_w, v_gdn_a_log, v_gdn_dt_bias, v_gdn_out_norm, v_gdn_w_out, v_kv_norm, v_kv_w_down, v_kv_latent_norm, v_kv_w_up, v_mla_w_in, v_mla_q_latent_norm, v_mla_w_q_up, v_mla_w_out):
    W = dict(meta_tokens=meta_tokens, pre_norm=pre_norm, post_norm=post_norm, gdn_w_in=gdn_w_in, gdn_conv_w=gdn_conv_w,
             gdn_a_log=gdn_a_log, gdn_dt_bias=gdn_dt_bias, gdn_out_norm=gdn_out_norm, gdn_w_out=gdn_w_out, kv_norm=kv_norm,
             kv_w_down=kv_w_down, kv_latent_norm=kv_latent_norm, kv_w_up=kv_w_up, mla_w_in=mla_w_in,
             mla_q_latent_norm=mla_q_latent_norm, mla_w_q_up=mla_w_q_up, mla_w_out=mla_w_out)
    M = dict(meta_tokens=m_meta_tokens, pre_norm=m_pre_norm, post_norm=m_post_norm, gdn_w_in=m_gdn_w_in, gdn_conv_w=m_gdn_conv_w,
             gdn_a_log=m_gdn_a_log, gdn_dt_bias=m_gdn_dt_bias, gdn_out_norm=m_gdn_out_norm, gdn_w_out=m_gdn_w_out, kv_norm=m_kv_norm,
             kv_w_down=m_kv_w_down, kv_latent_norm=m_kv_latent_norm, kv_w_up=m_kv_w_up, mla_w_in=m_mla_w_in,
             mla_q_latent_norm=m_mla_q_latent_norm, mla_w_q_up=m_mla_w_q_up, mla_w_out=m_mla_w_out)
    V = dict(meta_tokens=v_meta_tokens, pre_norm=v_pre_norm, post_norm=v_post_norm, gdn_w_in=v_gdn_w_in, gdn_conv_w=v_gdn_conv_w,
             gdn_a_log=v_gdn_a_log, gdn_dt_bias=v_gdn_dt_bias, gdn_out_norm=v_gdn_out_norm, gdn_w_out=v_gdn_w_out, kv_norm=v_kv_norm,
             kv_w_down=v_kv_w_down, kv_latent_norm=v_kv_latent_norm, kv_w_up=v_kv_w_up, mla_w_in=v_mla_w_in,
             mla_q_latent_norm=v_mla_q_latent_norm, mla_w_q_up=v_mla_w_q_up, mla_w_out=v_mla_w_out)
    order = list(W)

    n_tok = x.shape[1]
    assert n_tok % GDN_CHUNK == 0
    n_real = ROW0 + n_tok
    lp = -(-n_real // ROW_TILE) * ROW_TILE
    n_real_chunks = n_real // GDN_CHUNK

    shard2d = {n: W[n].reshape(W[n].shape[-2:]) for n in _SHARDED}
    full_shape = {n: ((s.shape[0], s.shape[1] * N_DEV) if n in _COL_SHARDED else (s.shape[0] * N_DEV, s.shape[1]))
                  for n, s in shard2d.items()}
    full = {}

    def unpack_gathered(names, buf):
        r = 0
        for n in names:
            shp = shard2d[n].shape
            rows = math.prod(shp) // 128
            blocks = buf[:, r:r + rows].reshape((N_DEV,) + shp)
            full[n] = _unshard(blocks, full_shape[n], n in _COL_SHARDED)
            r += rows

    unpack_gathered(_GATHER_FIRST, _exchange(_pack([shard2d[n].astype(bf16) for n in _GATHER_FIRST], 16), gather=True,
                                             name="gather_w_in"))
    unpack_gathered(_GATHER_F32, _exchange(_pack([shard2d[n] for n in _GATHER_F32], 8), gather=True, name="gather_meta_conv"))
    rest_shards = _pack([shard2d[n].astype(bf16) for n in _GATHER_REST], 16)

    h0 = jnp.concatenate([jnp.zeros((PAD_FRONT, D_MODEL), f32), full["meta_tokens"], x[0],
                          jnp.zeros((lp - n_real, D_MODEL), f32)], axis=0)
    tgt = jnp.concatenate([jnp.zeros((ROW0, D_MODEL), f32), loss_target[0], jnp.zeros((lp - n_real, D_MODEL), f32)], axis=0)
    cos_k, sin_k = _rope_tables(lp)
    one = jnp.ones((lp, 128), f32)
    cos_q = jnp.concatenate([one, cos_k], axis=1)
    sin_q = jnp.concatenate([jnp.zeros((lp, 128), f32), sin_k], axis=1)

    w_in = full["gdn_w_in"]
    s1 = GDN_CONV_W + GDN_V_W
    w_in_p = jnp.concatenate([w_in[:, :s1], _pad_cols(w_in[:, s1:s1 + 16], 128), _pad_cols(w_in[:, s1 + 16:], 128)], axis=1)
    pre0, pre1 = pre_norm[0:1], pre_norm[1:2]
    post0, post1 = post_norm[0:1], post_norm[1:2]
    (hn0,) = _rowwise(_st_prenorm, [(h0, None, 0)], [], [pre0], [(D_MODEL, bf16, None)], name="f_prenorm0")
    proj, g_rest = _matmul(hn0, w_in_p, out_dtype=bf16, name="f_gdn_in", side=(rest_shards, True))
    unpack_gathered(_GATHER_REST, g_rest)

    wd = full["kv_w_down"]
    zc = jnp.zeros((D_MODEL, 64), bf16)
    wd2 = jnp.concatenate([wd, zc, jnp.zeros((D_MODEL, 128), bf16), wd[:, 160:192], wd[:, 128:160], zc], axis=1)
    wup_p = jnp.transpose(full["kv_w_up"].reshape(MLA_KV_RANK, MLA_HEADS, 2, 128), (0, 2, 1, 3)).reshape(MLA_KV_RANK, 2 * MLA_V_W)
    wq = full["mla_w_q_up"].reshape(MLA_Q_RANK, MLA_HEADS, MLA_QK)
    zq64 = jnp.zeros((MLA_Q_RANK, MLA_HEADS, 64), bf16)
    wq_plain = jnp.concatenate([wq, zq64], axis=2).reshape(MLA_Q_RANK, MLA_HEADS * MLA_QKP)
    wq_swap = jnp.concatenate([jnp.zeros((MLA_Q_RANK, MLA_HEADS, 128), bf16), wq[:, :, 160:192], wq[:, :, 128:160], zq64],
                              axis=2).reshape(MLA_Q_RANK, MLA_HEADS * MLA_QKP)
    q_half = Q_GROUP * MLA_QKP
    wq2 = jnp.concatenate([wq_plain[:, :q_half], wq_swap[:, :q_half], wq_plain[:, q_half:], wq_swap[:, q_half:]], axis=1)
    w_mla_in, w_gdn_out, w_mla_out = full["mla_w_in"], full["gdn_w_out"], full["mla_w_out"]
    conv_w = full["gdn_conv_w"]
    alog_p, dtb_p = _pad_cols(gdn_a_log, 128), _pad_cols(gdn_dt_bias, 128)
    kvn, kvln = kv_norm.reshape(1, -1), kv_latent_norm.reshape(1, -1)

    conv = _conv_fwd(proj, conv_w, col_blocks=GDN_CONV_W // CONV_BC, name="f_conv")
    (qn,) = _rowwise(_st_gdn_q, [(conv, GDN_QK_W, 0)], [], [], [(GDN_QK_W, bf16, GDN_QK_W)], heads=GDN_QK_HEADS, name="f_gdn_q")
    (kn,) = _rowwise(_st_gdn_k, [(conv, GDN_QK_W, 1)], [], [], [(GDN_QK_W, bf16, GDN_QK_W)], heads=GDN_QK_HEADS, name="f_gdn_k")
    (vv,) = _rowwise(_st_gdn_v, [(conv, GDN_V_W, 1)], [], [], [(GDN_V_W, bf16, GDN_V_W)], name="f_gdn_v")
    gate_rows = [(proj, 128, s1 // 128), (proj, 128, s1 // 128 + 1)]
    beta, gdec = _rowwise(_st_gdn_gate, gate_rows, [], [alog_p, dtb_p], [(128, f32, None)] * 2, name="f_gdn_gate")
    o_gdn, states, inverses = _gdn_fwd(qn, kn, vv, beta, gdec, n_real_chunks=n_real_chunks, name="f_gdn")
    out_rows = [(o_gdn, GDN_V_W, 0), (proj, GDN_V_W, GDN_CONV_W // GDN_V_W)]
    (og,) = _rowwise(_st_gdn_out, out_rows, [], [gdn_out_norm], [(GDN_V_W, bf16, GDN_V_W)], heads=GDN_V_HEADS, name="f_gdn_out")
    y0 = _matmul(og, w_gdn_out, name="f_gdn_wout")
    mid_rows = [(h0, None, 0), (y0, None, 0)]
    h1, hn1, hkv = _rowwise(_st_mid, mid_rows, [], [post0, pre1, kvn],
                            [(D_MODEL, f32, None), (D_MODEL, bf16, None), (D_MODEL, bf16, None)], name="f_mid")
    ckr = _matmul(hkv, wd2, name="f_kv_down")
    proj2 = _matmul(hn1, w_mla_in, out_dtype=bf16, name="f_mla_in")
    lat_rows = [(ckr, None, 0), (proj2, MLA_Q_RANK, 0)]
    lat_nd = [(cos_k, None, 0), (sin_k, None, 0)]
    c_kv, k_rope, c_q = _rowwise(_st_latent, lat_rows, lat_nd, [kvln, mla_q_latent_norm],
                                 [(128, bf16, None), (128, bf16, None), (MLA_Q_RANK, bf16, None)], name="f_latent")
    kvu = _matmul(c_kv, wup_p, out_dtype=bf16, name="f_kv_up")
    q_att = _q_proj_fwd(c_q, wq2, cos_q, sin_q, name="f_q_proj")
    o_att, lse, og2 = _attention_fwd(q_att, kvu, k_rope, proj2, name="f_attention")
    y1 = _matmul(og2, w_mla_out, name="f_mla_wout")
    st_loss = _make_st_loss(n_tok)
    loss_rows_in = [(h1, None, 0), (y1, None, 0)]
    (loss_rows,) = _rowwise(st_loss, loss_rows_in, [(tgt, None, 0)], [post1], [(1, f32, None)], name="f_loss")
    loss = lax.psum(jnp.sum(loss_rows), ("x", "y", "c"))

    ones_ct = jnp.ones((lp, 1), f32)
    (dh1_a, dy1), (dpost1,) = _rowwise_vjp(st_loss, loss_rows_in, [(tgt, None, 0)], [post1], [(ones_ct, None, 0)], name="b_loss")
    dog2 = _matmul(dy1, w_mla_out, tb=True, name="b_mla_wout_x")
    dw_mla_out = _matmul(og2, dy1, ta=True, name="b_mla_wout_w")
    do_att, dz2, delta = _gate_bwd(o_att, proj2, dog2, name="b_mla_gate")
    dq_att, dkn, dvv, dkr_h = _attention_bwd(q_att, kvu, k_rope, lse, delta, do_att, name="b_attention")
    dc_q, dwq2 = _q_proj_bwd(dq_att, c_q, wq2, cos_q, sin_q, name="b_q_proj")
    dkvu = jnp.concatenate([dkn, dvv], axis=1)
    dc_kv = _matmul(dkvu, wup_p, tb=True, name="b_kv_up_x")
    dwup_p = _matmul(c_kv, dkvu, ta=True, name="b_kv_up_w")

    def lat_ct(cv):
        dkr = cv[1][:, 0:128]
        for h in range(1, MLA_HEADS):
            dkr = dkr + cv[1][:, h * 128:(h + 1) * 128]
        return [cv[0], dkr, cv[2][:, :MLA_Q_RANK] + cv[2][:, MLA_Q_RANK:]]

    (dckr, dcq_pre), (dkvln, dqln) = _rowwise_vjp(
        _st_latent, lat_rows, lat_nd, [kvln, mla_q_latent_norm],
        [(dc_kv, None, 0), (dkr_h, None, 0), (dc_q, None, 0)], ct_pre=lat_ct, name="b_latent", grad_dtypes=[bf16, bf16])
    dproj2 = jnp.concatenate([dcq_pre, dz2], axis=1)
    dhn1 = _matmul(dproj2, w_mla_in, tb=True, name="b_mla_in_x")
    dw_mla_in = _matmul(hn1, dproj2, ta=True, name="b_mla_in_w")
    dhkv = _matmul(dckr, wd2, tb=True, name="b_kv_down_x")
    dwd2 = _matmul(hkv, dckr, ta=True, name="b_kv_down_w")
    (dh0_a, dy0), (dpost0, dpre1, dkvn) = _rowwise_vjp(
        _st_mid, mid_rows, [], [post0, pre1, kvn], [(dh1_a, None, 0), (dhn1, None, 0), (dhkv, None, 0)], name="b_mid")
    dog = _matmul(dy0, w_gdn_out, tb=True, name="b_gdn_wout_x")
    dw_gdn_out = _matmul(og, dy0, ta=True, name="b_gdn_wout_w")
    (do_gdn, dz), (dout_norm,) = _rowwise_vjp(_st_gdn_out, out_rows, [], [gdn_out_norm], [(dog, GDN_V_W, 0)], heads=GDN_V_HEADS,
                                              name="b_gdn_out", grad_dtypes=[f32, bf16])
    dq_g, dk_g, dv_g, dbeta, dgdec = _gdn_bwd(qn, kn, vv, beta, gdec, states, inverses, do_gdn, n_real_chunks=n_real_chunks,
                                              name="b_gdn")
    (db_col, da_col), (dalog_p, ddtb_p) = _rowwise_vjp(
        _st_gdn_gate, gate_rows, [], [alog_p, dtb_p], [(dbeta, None, 0), (dgdec, None, 0)], name="b_gdn_gate",
        grad_dtypes=[bf16, bf16])
    (dconv_q,), _ = _rowwise_vjp(_st_gdn_q, [(conv, GDN_QK_W, 0)], [], [], [(dq_g, GDN_QK_W, 0)], heads=GDN_QK_HEADS, name="b_gdn_q")
    (dconv_k,), _ = _rowwise_vjp(_st_gdn_k, [(conv, GDN_QK_W, 1)], [], [], [(dk_g, GDN_QK_W, 0)], heads=GDN_QK_HEADS, name="b_gdn_k")
    (dconv_v,), _ = _rowwise_vjp(_st_gdn_v, [(conv, GDN_V_W, 1)], [], [], [(dv_g, GDN_V_W, 0)], name="b_gdn_v")
    nq_b = GDN_QK_W // CONV_BC
    dpre_q, dcw_q = _conv_bwd(dconv_q, proj, conv_w, x_off=0, w_off=0, name="b_conv_q")
    dpre_k, dcw_k = _conv_bwd(dconv_k, proj, conv_w, x_off=nq_b, w_off=nq_b, name="b_conv_k")
    dpre_v, dcw_v = _conv_bwd(dconv_v, proj, conv_w, x_off=2 * nq_b, w_off=2 * nq_b, name="b_conv_v")
    dproj = jnp.concatenate([dpre_q, dpre_k, dpre_v, dz, db_col, da_col], axis=1)
    G = {}
    G["kv_w_down"] = jnp.concatenate([dwd2[:, :128], dwd2[:, 128:160] + dwd2[:, 416:448], dwd2[:, 160:192] + dwd2[:, 384:416]], axis=1)
    G["kv_w_up"] = jnp.transpose(dwup_p.reshape(MLA_KV_RANK, 2, MLA_HEADS, 128), (0, 2, 1, 3)).reshape(MLA_KV_RANK, 2 * MLA_V_W)
    G["mla_w_in"] = dw_mla_in
    dq4 = dwq2.reshape(MLA_Q_RANK, 2, 2, Q_GROUP, MLA_QKP)
    dqp = dq4[:, :, 0].reshape(MLA_Q_RANK, MLA_HEADS, MLA_QKP)
    dqs = dq4[:, :, 1].reshape(MLA_Q_RANK, MLA_HEADS, MLA_QKP)
    G["mla_w_q_up"] = jnp.concatenate([dqp[:, :, :128], dqp[:, :, 128:160] + dqs[:, :, 160:192],
                                       dqp[:, :, 160:192] + dqs[:, :, 128:160]], axis=2).reshape(MLA_Q_RANK, MLA_HEADS * MLA_QK)
    G["mla_w_out"] = dw_mla_out
    G["gdn_w_out"] = dw_gdn_out

    def shards_to_send(names):
        return jnp.concatenate([_to_shards(G[n], n in _COL_SHARDED).reshape(N_DEV, -1, 128).astype(bf16) for n in names], axis=1)

    dhn0, parts_early = _matmul(dproj, w_in_p, tb=True, name="b_gdn_in_x", side=(shards_to_send(_SCATTER_EARLY), False))
    dw_in_p = _matmul(hn0, dproj, ta=True, name="b_gdn_in_w")
    (dh0,), (dpre0,) = _rowwise_vjp(_st_prenorm, [(h0, None, 0)], [], [pre0], [(dhn0, None, 0)], extra=[dh0_a], name="b_prenorm0")

    grad_x = dh0[ROW0:n_real][None]
    G["meta_tokens"] = dh0[PAD_FRONT:ROW0]
    G["gdn_w_in"] = jnp.concatenate([dw_in_p[:, :s1 + 16], dw_in_p[:, s1 + 128:s1 + 144]], axis=1)
    G["gdn_conv_w"] = jnp.concatenate([dcw_q, dcw_k, dcw_v], axis=1)
    G["pre_norm"] = jnp.concatenate([dpre0, dpre1], axis=0)
    G["post_norm"] = jnp.concatenate([dpost0, dpost1], axis=0)
    G["gdn_a_log"] = dalog_p[:, :GDN_V_HEADS]
    G["gdn_dt_bias"] = ddtb_p[:, :GDN_V_HEADS]
    G["gdn_out_norm"] = dout_norm
    G["kv_norm"] = dkvn.reshape(-1)
    G["kv_latent_norm"] = dkvln.reshape(-1)
    G["mla_q_latent_norm"] = dqln

    parts_late = _exchange(shards_to_send(_SCATTER_LATE), gather=False, name="scatter_grads")
    parts_r = _exchange(_pack([G[n] for n in _REPLICATED], 8), gather=True, name="gather_small_grads")
    outs = {}
    for names, parts, tag in ((_SCATTER_EARLY, parts_early, "early"), (_SCATTER_LATE, parts_late, "late"),
                              (_REPLICATED, parts_r, "replicated")):
        w_p, m_p, v_p = (_pack([d[n] for n in names], 8) for d in (W, M, V))
        res = _reduce_adamw(parts, w_p, m_p, v_p, name="adamw_" + tag)
        for kind, buf in zip(("grad", "delta", "new_m", "new_v"), res):
            for n, a in zip(names, _unpack(buf, [W[n].shape for n in names])):
                outs[kind, n] = a
    return (loss, grad_x, *[outs[k, n] for k in ("grad", "delta", "new_m", "new_v") for n in order])
```

```python
import functools
import math

import jax
import jax.numpy as jnp
from jax import lax
from jax.experimental import pallas as pl
from jax.experimental.pallas import tpu as pltpu

f32, bf16 = jnp.float32, jnp.bfloat16
MESH_ID = pl.DeviceIdType.MESH

N_DEV = 8
D_MODEL = 1024
N_META = 16
NORM_EPS = 1e-6
PAD_FRONT = 48
ROW0 = PAD_FRONT + N_META
GDN_QK_HEADS, GDN_V_HEADS, GDN_D = 8, 16, 128
GDN_CHUNK = 64
GDN_QK_W, GDN_V_W = GDN_QK_HEADS * GDN_D, GDN_V_HEADS * GDN_D
GDN_CONV_W = 2 * GDN_QK_W + GDN_V_W
GDN_IN_W = GDN_CONV_W + GDN_V_W + 2 * GDN_V_HEADS
GDN_IN_WP = GDN_CONV_W + GDN_V_W + 2 * 128
MLA_HEADS, MLA_NOPE, MLA_ROPE, MLA_V = 16, 128, 64, 128
MLA_Q_RANK, MLA_KV_RANK = 256, 128
MLA_QK = MLA_NOPE + MLA_ROPE
MLA_QKP = 256
MLA_V_W = MLA_HEADS * MLA_V
ROPE_THETA = 10000.0
NEG = -1e30
ROW_TILE = 256
ATT_KEY_TILES = (384, 256)

ADAM_LR, ADAM_B1, ADAM_B2, ADAM_EPS, ADAM_WD, ADAM_STEP = 0.001, 0.9, 0.999, 1e-08, 0.01, 10

NN = ((1,), (0,))
NT = ((1,), (1,))
TN = ((0,), (0,))


def _pick(dim, prefs):
    for p in prefs:
        if dim % p == 0:
            return p
    return dim


def _dlo(a, b, dims):
    return lax.dot_general(a.astype(bf16), b.astype(bf16), (dims, ((), ())), preferred_element_type=f32)


def _dsel(a, b, dims, selector):
    x = b if selector == 0 else a
    hi = x.astype(bf16)
    rest = x - hi.astype(f32)
    mid = rest.astype(bf16)
    low = (rest - mid.astype(f32)).astype(bf16)
    sel = (a if selector == 0 else b).astype(bf16)
    d = (lambda p: lax.dot_general(sel, p, (dims, ((), ())), preferred_element_type=f32)) if selector == 0 else \
        (lambda p: lax.dot_general(p, sel, (dims, ((), ())), preferred_element_type=f32))
    return d(hi) + (d(mid) + d(low))


def _matmul(a, b, *, ta=False, tb=False, out_dtype=f32, name, side=None):
    assert not (ta and tb)
    if ta:
        kdim, m = a.shape
    else:
        m, kdim = a.shape
    n = b.shape[0] if tb else b.shape[1]
    assert (b.shape[1] if tb else b.shape[0]) == kdim
    tm = _pick(m, (1024, 768, 512, 384, 256, 128))
    tn = _pick(n, (1024, 768, 640, 512, 256, 128))
    tk = _pick(kdim, (1024, 768, 640, 512, 256, 128))
    nk = kdim // tk
    dims = TN if ta else (NT if tb else NN)

    grid = (m // tm, n // tn, nk)

    def product(a_ref, b_ref, o_ref, acc_ref):
        k = pl.program_id(2)

        @pl.when(k == 0)
        def _():
            acc_ref[...] = jnp.zeros_like(acc_ref)

        acc_ref[...] += lax.dot_general(a_ref[...].astype(bf16), b_ref[...].astype(bf16), (dims, ((), ())),
                                        preferred_element_type=f32)

        @pl.when(k == nk - 1)
        def _():
            o_ref[...] = acc_ref[...].astype(o_ref.dtype)

    a_spec = pl.BlockSpec((tk, tm), lambda i, j, k: (k, i)) if ta else pl.BlockSpec((tm, tk), lambda i, j, k: (i, k))
    b_spec = pl.BlockSpec((tn, tk), lambda i, j, k: (j, k)) if tb else pl.BlockSpec((tk, tn), lambda i, j, k: (k, j))
    o_spec = pl.BlockSpec((tm, tn), lambda i, j, k: (i, j))
    o_shape = jax.ShapeDtypeStruct((m, n), out_dtype)
    if side is None:
        def body(a_ref, b_ref, o_ref, acc_ref):
            product(a_ref, b_ref, o_ref, acc_ref)

        return pl.pallas_call(
            body, name=name, grid=grid, in_specs=[a_spec, b_spec], out_specs=o_spec, out_shape=o_shape,
            scratch_shapes=[pltpu.VMEM((tm, tn), f32)],
            compiler_params=pltpu.CompilerParams(dimension_semantics=("parallel", "parallel", "arbitrary")),
        )(a, b)

    x, gather = side
    blk = x.shape if gather else x.shape[1:]

    def body_with_exchange(a_ref, b_ref, x_ref, o_ref, xo_ref, acc_ref, send_sems, recv_sems, local_sem):
        step = (pl.program_id(0) * grid[1] + pl.program_id(1)) * grid[2] + pl.program_id(2)
        copies = _exchange_copies(x_ref, xo_ref, send_sems, recv_sems, local_sem, gather)

        @pl.when(step == 0)
        def _():
            _exchange_start(copies)

        product(a_ref, b_ref, o_ref, acc_ref)

        @pl.when(step == grid[0] * grid[1] * grid[2] - 1)
        def _():
            _exchange_wait(copies)

    hbm = pl.BlockSpec(memory_space=pltpu.HBM)
    return pl.pallas_call(
        body_with_exchange, name=name, grid=grid, in_specs=[a_spec, b_spec, hbm], out_specs=[o_spec, hbm],
        out_shape=[o_shape, jax.ShapeDtypeStruct((N_DEV,) + tuple(blk), x.dtype)],
        scratch_shapes=[pltpu.VMEM((tm, tn), f32)] + list(_EXCHANGE_SCRATCH),
        compiler_params=pltpu.CompilerParams(dimension_semantics=("arbitrary", "arbitrary", "arbitrary")),
    )(a, b, x)


def _row_spec(item, tr):
    a, bc, off = item
    if bc is None:
        return pl.BlockSpec((tr, a.shape[1]), lambda i, j: (i, 0))
    return pl.BlockSpec((tr, bc), lambda i, j, off=off: (i, j + off))


def _param_spec(p):
    return pl.BlockSpec(p.shape, lambda i, j: (0, 0))


def _row_tile(lp, items):
    widest = max(a.shape[1] if bc is None else bc for (a, bc, _) in items)
    return ROW_TILE if widest >= 1024 else _pick(lp, (768, 512, 256))


def _head_cols(tiles, h, heads):
    return [x[:, h * (x.shape[1] // heads):(h + 1) * (x.shape[1] // heads)] for x in tiles]


def _rowwise(fn, rows, nodiff, params, outs, *, ncol=1, heads=1, name):
    lp = rows[0][0].shape[0]
    tr = _row_tile(lp, rows)
    nr, nd = len(rows), len(nodiff)

    def body(*refs):
        rv = [r[...].astype(f32) for r in refs[:nr]]
        nv = [r[...] for r in refs[nr:nr + nd]]
        pv = [r[...] for r in refs[nr + nd:nr + nd + len(params)]]
        per_head = [fn(_head_cols(rv, h, heads), nv, pv) for h in range(heads)]
        res = [jnp.concatenate(list(vals), axis=1) if heads > 1 else vals[0] for vals in zip(*per_head)]
        for ref, val in zip(refs[nr + nd + len(params):], res):
            ref[...] = val.astype(ref.dtype)

    out_specs = [pl.BlockSpec((tr, c if bc is None else bc), (lambda i, j: (i, 0)) if bc is None else (lambda i, j: (i, j)))
                 for (c, _, bc) in outs]
    return pl.pallas_call(
        body, name=name, grid=(lp // tr, ncol),
        in_specs=[_row_spec(it, tr) for it in rows + nodiff] + [_param_spec(p) for p in params],
        out_specs=out_specs,
        out_shape=[jax.ShapeDtypeStruct((lp, c), dt) for (c, dt, _) in outs],
    )(*[it[0] for it in rows + nodiff], *params)


def _rowwise_vjp(fn, rows, nodiff, params, cts, *, ncol=1, heads=1, name, ct_pre=None, extra=None, grad_dtypes=None):
    lp = rows[0][0].shape[0]
    tr = _row_tile(lp, rows)
    nr, nd, npar, nct = len(rows), len(nodiff), len(params), len(cts)
    extra = extra or [None] * nr
    grad_dtypes = grad_dtypes or [f32] * nr
    ex_items = [(e, rows[k][1], 0) for k, e in enumerate(extra) if e is not None]
    ex_pos = [k for k, e in enumerate(extra) if e is not None]
    for (a, bc, _) in rows:
        assert bc is not None or ncol == 1

    def body(*refs):
        pos = 0
        rv = [r[...].astype(f32) for r in refs[pos:pos + nr]]; pos += nr
        nv = [r[...] for r in refs[pos:pos + nd]]; pos += nd
        pv = [r[...] for r in refs[pos:pos + npar]]; pos += npar
        cv = [r[...].astype(f32) for r in refs[pos:pos + nct]]; pos += nct
        ev = [r[...].astype(f32) for r in refs[pos:pos + len(ex_items)]]; pos += len(ex_items)
        drow_refs = refs[pos:pos + nr]; pos += nr
        dpar_refs = refs[pos:pos + npar]
        ctv = ct_pre(cv) if ct_pre is not None else cv
        drow_h, dpar = [], None
        for h in range(heads):
            outs, vjp_fn = jax.vjp(lambda rr, pp: fn(rr, nv, pp), _head_cols(rv, h, heads), pv)
            dr, dp = vjp_fn([c.astype(o.dtype) for c, o in zip(_head_cols(ctv, h, heads), outs)])
            drow_h.append(dr)
            dpar = dp if dpar is None else [a + b for a, b in zip(dpar, dp)]
        drow = [jnp.concatenate(list(vals), axis=1) if heads > 1 else vals[0] for vals in zip(*drow_h)]
        for k, e in zip(ex_pos, ev):
            drow[k] = drow[k] + e
        for ref, val in zip(drow_refs, drow):
            ref[...] = val.astype(ref.dtype)
        first = jnp.logical_and(pl.program_id(0) == 0, pl.program_id(1) == 0)

        @pl.when(first)
        def _():
            for ref, val in zip(dpar_refs, dpar):
                ref[...] = val

        @pl.when(jnp.logical_not(first))
        def _():
            for ref, val in zip(dpar_refs, dpar):
                ref[...] += val

    drow_shapes, drow_specs = [], []
    for (a, bc, _), dt in zip(rows, grad_dtypes):
        if bc is None:
            drow_shapes.append(jax.ShapeDtypeStruct((lp, a.shape[1]), dt))
            drow_specs.append(pl.BlockSpec((tr, a.shape[1]), lambda i, j: (i, 0)))
        else:
            drow_shapes.append(jax.ShapeDtypeStruct((lp, ncol * bc), dt))
            drow_specs.append(pl.BlockSpec((tr, bc), lambda i, j: (i, j)))
    res = pl.pallas_call(
        body, name=name, grid=(lp // tr, ncol),
        in_specs=[_row_spec(it, tr) for it in rows + nodiff] + [_param_spec(p) for p in params]
        + [_row_spec(it, tr) for it in cts + ex_items],
        out_specs=drow_specs + [_param_spec(p) for p in params],
        out_shape=drow_shapes + [jax.ShapeDtypeStruct(p.shape, f32) for p in params],
        compiler_params=pltpu.CompilerParams(dimension_semantics=("arbitrary", "arbitrary")),
    )(*[it[0] for it in rows + nodiff], *params, *[it[0] for it in cts + ex_items])
    return res[:nr], res[nr:]


def _rms(x, g):
    return x * lax.rsqrt(jnp.mean(x * x, axis=-1, keepdims=True) + NORM_EPS) * g


def _l2n(x):
    return x * lax.rsqrt(jnp.sum(x * x, axis=-1, keepdims=True) + NORM_EPS)


def _sigmoid(x):
    return 1.0 / (1.0 + jnp.exp(-x))


def _silu(x):
    return x * _sigmoid(x)


def _softplus(x):
    return jnp.maximum(x, 0.0) + jnp.log(1.0 + jnp.exp(-jnp.abs(x)))


def _row_ids(shape):
    return pl.program_id(0) * shape[0] + lax.broadcasted_iota(jnp.int32, shape, 0)


def _st_prenorm(r, n, p):
    return [_rms(r[0], p[0])]


def _st_gdn_q(r, n, p):
    return [_l2n(_silu(r[0])) * (GDN_D ** -0.5)]


def _st_gdn_k(r, n, p):
    return [_l2n(_silu(r[0]))]


def _st_gdn_v(r, n, p):
    return [_silu(r[0])]


def _st_gdn_gate(r, n, p):
    real = _row_ids(r[0].shape) >= PAD_FRONT
    beta = jnp.where(real, _sigmoid(r[0]), 0.0)
    g = jnp.where(real, -jnp.exp(p[0]) * _softplus(r[1] + p[1]), 0.0)
    return [beta, g]


def _st_gdn_out(r, n, p):
    return [_rms(r[0], p[0]) * _silu(r[1])]


def _st_mid(r, n, p):
    h1 = r[0] + _rms(r[1], p[0])
    return [h1, _rms(h1, p[1]), _rms(h1, p[2])]


def _st_latent(r, n, p):
    ckr, cq = r
    c_kv = _rms(ckr[:, :MLA_KV_RANK], p[0])
    k_rope = ckr[:, 128:256] * n[0] + ckr[:, 384:512] * n[1]
    return [c_kv, k_rope, _rms(cq, p[1])]


Q_GROUP = 8


def _st_q_rope(r, n, p):
    half = Q_GROUP * MLA_QKP
    out = []
    for h in range(Q_GROUP):
        cols = slice(h * MLA_QKP, (h + 1) * MLA_QKP)
        out.append((r[0][:, :half][:, cols] * n[0] + r[0][:, half:][:, cols] * n[1]) * (MLA_QK ** -0.5))
    return [jnp.concatenate(out, axis=1)]


def _st_q_rope_t(r, n, p):
    plain, swapped = [], []
    for h in range(Q_GROUP):
        ct = r[0][:, h * MLA_QKP:(h + 1) * MLA_QKP] * (MLA_QK ** -0.5)
        plain.append(ct * n[0])
        swapped.append(ct * n[1])
    return [jnp.concatenate(plain + swapped, axis=1)]


def _make_st_loss(n_tokens):
    def st(r, n, p):
        h2 = r[0] + _rms(r[1], p[0])
        rows = _row_ids((r[0].shape[0], 1))
        real = jnp.logical_and(rows >= ROW0, rows < ROW0 + n_tokens)
        err = h2 - n[0]
        return [jnp.where(real, 0.5 * jnp.mean(err * err, axis=-1, keepdims=True), 0.0)]
    return st


CONV_BC = 1024
HALO = 16


def _conv_fwd(x, w, *, col_blocks, name, tr=ROW_TILE):
    lp = x.shape[0]

    def body(x_ref, xp_ref, w_ref, o_ref):
        i = pl.program_id(0)
        xv = x_ref[...].astype(f32)
        prev = jnp.where(i > 0, xp_ref[...].astype(f32), 0.0)
        xc = jnp.concatenate([prev, xv], axis=0)
        wv = w_ref[...]
        acc = wv[3:4, :] * xv
        for j in range(3):
            acc = acc + wv[j:j + 1, :] * pltpu.roll(xc, 3 - j, 0)[HALO:, :]
        o_ref[...] = acc.astype(o_ref.dtype)

    return pl.pallas_call(
        body, name=name, grid=(lp // tr, col_blocks),
        in_specs=[pl.BlockSpec((tr, CONV_BC), lambda i, j: (i, j)),
                  pl.BlockSpec((HALO, CONV_BC), lambda i, j: (jnp.maximum(i * (tr // HALO) - 1, 0), j)),
                  pl.BlockSpec((4, CONV_BC), lambda i, j: (0, j))],
        out_specs=pl.BlockSpec((tr, CONV_BC), lambda i, j: (i, j)),
        out_shape=jax.ShapeDtypeStruct((lp, col_blocks * CONV_BC), bf16),
    )(x, x, w)


def _conv_bwd(dc, x, w, *, x_off, w_off, name, tr=ROW_TILE):
    lp, width = dc.shape
    ncb, nrow = width // CONV_BC, lp // tr

    def body(dc_ref, dcn_ref, x_ref, xp_ref, w_ref, dx_ref, dw_ref):
        i = pl.program_id(1)
        nxt = jnp.where(i < nrow - 1, dcn_ref[...], 0.0)
        dcv = dc_ref[...]
        dcc = jnp.concatenate([dcv, nxt], axis=0)
        xv = x_ref[...].astype(f32)
        prev = jnp.where(i > 0, xp_ref[...].astype(f32), 0.0)
        xc = jnp.concatenate([prev, xv], axis=0)
        wv = w_ref[...]
        dx = wv[3:4, :] * dcv
        dws = [None] * 4
        dws[3] = jnp.sum(dcv * xv, axis=0, keepdims=True)
        for j in range(3):
            dx = dx + wv[j:j + 1, :] * pltpu.roll(dcc, tr + 8 - (3 - j), 0)[:tr, :]
            dws[j] = jnp.sum(dcv * pltpu.roll(xc, 3 - j, 0)[HALO:, :], axis=0, keepdims=True)
        dx_ref[...] = dx.astype(dx_ref.dtype)

        @pl.when(i == 0)
        def _():
            for j in range(4):
                dw_ref[j:j + 1, :] = dws[j]

        @pl.when(i > 0)
        def _():
            for j in range(4):
                dw_ref[j:j + 1, :] += dws[j]

    last8 = lp // 8 - 1
    return pl.pallas_call(
        body, name=name, grid=(ncb, nrow),
        in_specs=[pl.BlockSpec((tr, CONV_BC), lambda j, i: (i, j)),
                  pl.BlockSpec((8, CONV_BC), lambda j, i: (jnp.minimum((i + 1) * (tr // 8), last8), j)),
                  pl.BlockSpec((tr, CONV_BC), lambda j, i: (i, j + x_off)),
                  pl.BlockSpec((HALO, CONV_BC), lambda j, i: (jnp.maximum(i * (tr // HALO) - 1, 0), j + x_off)),
                  pl.BlockSpec((4, CONV_BC), lambda j, i: (0, j + w_off))],
        out_specs=[pl.BlockSpec((tr, CONV_BC), lambda j, i: (i, j)),
                   pl.BlockSpec((4, CONV_BC), lambda j, i: (0, j))],
        out_shape=[jax.ShapeDtypeStruct((lp, width), bf16), jax.ShapeDtypeStruct((4, width), f32)],
        compiler_params=pltpu.CompilerParams(dimension_semantics=("arbitrary", "arbitrary")),
    )(dc, dc, x, x, w)


GDN_PACK = 4
GDN_FWD_INTERLEAVE, GDN_BWD_INTERLEAVE = 4, 4


def _bd(x, cb):
    r = x.shape[0]
    tall = jnp.concatenate([x] * GDN_PACK, axis=0)
    rows = lax.broadcasted_iota(jnp.int32, tall.shape, 0) // r
    cols = lax.broadcasted_iota(jnp.int32, tall.shape, 1) // cb
    return jnp.where(rows == cols, tall, jnp.zeros_like(tall))


def _diag(full, r, cb):
    cols = lax.broadcasted_iota(jnp.int32, (r, full.shape[1]), 1) // cb
    out = jnp.where(cols == 0, full[0:r, :], 0.0)
    for a in range(1, GDN_PACK):
        out = out + jnp.where(cols == a, full[a * r:(a + 1) * r, :], 0.0)
    return out


def _stack(x, cb):
    return jnp.concatenate([x[:, a * cb:(a + 1) * cb] for a in range(GDN_PACK)], axis=0)


def _make_packed(dot):
    @jax.custom_vjp
    def pmm(x, y):
        return dot(x, _bd(y, y.shape[1] // GDN_PACK), NN)

    @jax.custom_vjp
    def pnt(x, y):
        k = x.shape[1] // GDN_PACK
        return _diag(dot(_stack(x, k), _stack(y, k), NT), x.shape[0], y.shape[0])

    @jax.custom_vjp
    def ptn(x, y):
        return _diag(dot(x, y, TN), x.shape[1] // GDN_PACK, y.shape[1] // GDN_PACK)

    def pmm_bwd(res, ct):
        x, y = res
        cb = y.shape[1] // GDN_PACK
        return dot(ct, _bd(y, cb), NT), _diag(dot(x, ct, TN), y.shape[0], cb)

    pmm.defvjp(lambda x, y: (pmm(x, y), (x, y)), pmm_bwd)
    pnt.defvjp(lambda x, y: (pnt(x, y), (x, y)), lambda res, ct: (pmm(ct, res[1]), ptn(ct, res[0])))
    ptn.defvjp(lambda x, y: (ptn(x, y), (x, y)), lambda res, ct: (pnt(res[1], ct), pmm(res[0], ct)))
    return pmm, pnt, ptn


_pmm, _pnt, _ptn = _make_packed(_dlo)


@jax.custom_vjp
def _inv_packed(ms):
    c = ms[0].shape[0]
    ii = lax.broadcasted_iota(jnp.int32, ms[0].shape, 0)
    jj = lax.broadcasted_iota(jnp.int32, ms[0].shape, 1) % c
    ts = [jnp.where(ii == jj, 1.0, 0.0) - m for m in ms]
    ps = [(-m).astype(bf16) for m in ms]
    for _ in range(int(math.log2(c)) - 1):
        ps = [_dlo(p, _bd(p, c), NN).astype(bf16) for p in ps]
        ts = [t + _dlo(t, _bd(p, c), NN) for t, p in zip(ts, ps)]
    return tuple(ts)


def _inv_packed_fwd(ms):
    ts = _inv_packed(ms)
    return ts, ts


def _inv_packed_bwd(ts, cts):
    c = ts[0].shape[0]
    ys = [_diag(_dlo(t, ct, TN), c, c) for t, ct in zip(ts, cts)]
    return (tuple(-_dlo(y, _bd(t.astype(bf16), c), NT) for y, t in zip(ys, ts)),)


_inv_packed.defvjp(_inv_packed_fwd, _inv_packed_bwd)


def _gdn_prep(q2, k2, v4, bcols, gcols, grows):
    c, d = v4.shape[0], GDN_D
    q4 = jnp.concatenate([q2[:, :d], q2[:, :d], q2[:, d:], q2[:, d:]], axis=1)
    k4 = jnp.concatenate([k2[:, :d], k2[:, :d], k2[:, d:], k2[:, d:]], axis=1)
    beta4 = jnp.concatenate([jnp.broadcast_to(b, (c, d)) for b in bcols], axis=1)
    gc4 = jnp.concatenate([jnp.broadcast_to(g, (c, d)) for g in gcols], axis=1)
    low = lax.broadcasted_iota(jnp.int32, (c, 128), 1) < c
    gi = jnp.concatenate([jnp.where(low, gcols[0], gcols[1]), jnp.where(low, gcols[2], gcols[3])], axis=1)
    gj = jnp.concatenate([jnp.where(low, grows[0], grows[1]), jnp.where(low, grows[2], grows[3])], axis=1)
    ii = lax.broadcasted_iota(jnp.int32, gi.shape, 0)
    jj = lax.broadcasted_iota(jnp.int32, gi.shape, 1) % c
    dec = jnp.exp(jnp.where(ii >= jj, gi - gj, NEG))
    rid = lax.broadcasted_iota(jnp.int32, gc4.shape, 0)
    glast = jnp.sum(jnp.where(rid == c - 1, gc4, 0.0), axis=0, keepdims=True)
    eg = jnp.exp(gc4)
    kb = k4 * beta4
    return dict(q=q4, k=k4, kb=kb, vb=v4 * beta4, kbe=kb * eg, qe=q4 * eg, dec=dec, dec_strict=jnp.where(ii > jj, dec, 0.0),
                sdecay=jnp.exp(glast), kd=k4 * jnp.exp(glast - gc4))


@jax.custom_vjp
def _inv_packed_known(ms, ts):
    return ts


_inv_packed_known.defvjp(lambda ms, ts: (ts, ts),
                         lambda ts, cts: (_inv_packed_bwd(ts, cts)[0], tuple(jnp.zeros_like(t) for t in ts)))


def _gdn_groups(groups, known_inverses=None, with_inverses=False):
    c = groups[0][3].shape[0]
    ss = [g[0] for g in groups]
    e = [_gdn_prep(*g[1:]) for g in groups]
    ms = tuple(_pnt(x["kb"], x["k"]) * x["dec_strict"] for x in e)
    ts = _inv_packed(ms) if known_inverses is None else _inv_packed_known(ms, tuple(known_inverses))
    us = [_pmm(t, x["vb"]) for t, x in zip(ts, e)]
    ws = [_pmm(t, x["kbe"]) for t, x in zip(ts, e)]
    attns = [_pnt(x["q"], x["k"]) * x["dec"] for x in e]
    ws_qs = [_pmm(jnp.concatenate([w, x["qe"]], axis=0), s) for w, x, s in zip(ws, e, ss)]
    v_news = [u - y[:c] for u, y in zip(us, ws_qs)]
    os = [y[c:] + _pmm(a, vn) for y, a, vn in zip(ws_qs, attns, v_news)]
    s_news = [s * x["sdecay"] + _ptn(x["kd"], vn) for s, x, vn in zip(ss, e, v_news)]
    if with_inverses:
        return list(zip(os, s_news)), list(ts)
    return list(zip(os, s_news))


def _lane_pick(x, h):
    lane = lax.broadcasted_iota(jnp.int32, x.shape, 1)
    return jnp.sum(jnp.where(lane == h, x, 0.0), axis=1, keepdims=True)


def _cum_log_decay(g):
    c = g.shape[0]
    lower = (lax.broadcasted_iota(jnp.int32, (c, c), 0) >= lax.broadcasted_iota(jnp.int32, (c, c), 1)).astype(f32)
    upper2 = (lax.broadcasted_iota(jnp.int32, (c, 128), 0) <= lax.broadcasted_iota(jnp.int32, (c, 128), 1) % c).astype(f32)
    return _dsel(lower, g, NN, 0), _dsel(g, upper2, TN, 1)


def _group_operands(gi, s_ref, q_ref, k_ref, v_ref, bv, gcv, gct_s):
    heads = [gi * GDN_PACK + u for u in range(GDN_PACK)]
    qk_off = pl.multiple_of(gi * 2 * GDN_D, 2 * GDN_D)
    v_off = pl.multiple_of(gi * GDN_PACK * GDN_D, GDN_PACK * GDN_D)
    return (s_ref[gi], q_ref[:, pl.ds(qk_off, 2 * GDN_D)].astype(f32), k_ref[:, pl.ds(qk_off, 2 * GDN_D)].astype(f32),
            v_ref[:, pl.ds(v_off, GDN_PACK * GDN_D)].astype(f32),
            [_lane_pick(bv, h) for h in heads], [_lane_pick(gcv, h) for h in heads],
            [gct_s[pl.ds(h, 1), :] for h in heads]), heads, qk_off, v_off


def _gdn_fwd(qn, kn, v, beta, g, *, n_real_chunks, name):
    lp = qn.shape[0]
    nchunk = lp // GDN_CHUNK
    C, D = GDN_CHUNK, GDN_D
    NG, SW = GDN_V_HEADS // GDN_PACK, GDN_PACK * GDN_D

    def body(q_ref, k_ref, v_ref, b_ref, g_ref, o_ref, st_ref, inv_ref, s_s, gc_s, gct_s):
        ci = pl.program_id(0)

        @pl.when(ci == 0)
        def _():
            s_s[...] = jnp.zeros_like(s_s)

        @pl.when(ci >= n_real_chunks)
        def _():
            o_ref[...] = jnp.zeros_like(o_ref)
            st_ref[...] = jnp.zeros_like(st_ref)
            inv_ref[...] = jnp.zeros_like(inv_ref)

        @pl.when(ci < n_real_chunks)
        def _():
            gc, gct = _cum_log_decay(g_ref[...])
            gc_s[...] = gc
            gct_s[...] = gct

            def some_groups(it, carry):
                ids = [it * GDN_FWD_INTERLEAVE + u for u in range(GDN_FWD_INTERLEAVE)]
                ops = [_group_operands(gi, s_s, q_ref, k_ref, v_ref, b_ref[...], gc_s[...], gct_s) for gi in ids]
                res, inverses = _gdn_groups([op[0] for op in ops], with_inverses=True)
                for gi, op, (o, s_new), t in zip(ids, ops, res, inverses):
                    st_ref[gi] = op[0][0]
                    inv_ref[gi] = t
                    s_s[gi] = s_new
                    o_ref[:, pl.ds(op[3], SW)] = o
                return carry

            lax.fori_loop(0, NG // GDN_FWD_INTERLEAVE, some_groups, 0)

    return pl.pallas_call(
        body, name=name, grid=(nchunk,),
        in_specs=[pl.BlockSpec((C, GDN_QK_W), lambda c: (c, 0)), pl.BlockSpec((C, GDN_QK_W), lambda c: (c, 0)),
                  pl.BlockSpec((C, GDN_V_W), lambda c: (c, 0)), pl.BlockSpec((C, 128), lambda c: (c, 0)),
                  pl.BlockSpec((C, 128), lambda c: (c, 0))],
        out_specs=[pl.BlockSpec((C, GDN_V_W), lambda c: (c, 0)),
                   pl.BlockSpec((None, NG, D, SW), lambda c: (c, 0, 0, 0)),
                   pl.BlockSpec((None, NG, C, GDN_PACK * C), lambda c: (c, 0, 0, 0))],
        out_shape=[jax.ShapeDtypeStruct((lp, GDN_V_W), f32), jax.ShapeDtypeStruct((nchunk, NG, D, SW), f32),
                   jax.ShapeDtypeStruct((nchunk, NG, C, GDN_PACK * C), f32)],
        scratch_shapes=[pltpu.VMEM((NG, D, SW), f32), pltpu.VMEM((C, 128), f32), pltpu.VMEM((128, 128), f32)],
        compiler_params=pltpu.CompilerParams(dimension_semantics=("arbitrary",)),
    )(qn, kn, v, beta, g)


def _gdn_bwd(qn, kn, v, beta, g, states, inverses, do, *, n_real_chunks, name):
    lp = qn.shape[0]
    nchunk = lp // GDN_CHUNK
    C, D = GDN_CHUNK, GDN_D
    NG, SW = GDN_V_HEADS // GDN_PACK, GDN_PACK * GDN_D
    rev = lambda i: (nchunk - 1 - i, 0)

    def body(q_ref, k_ref, v_ref, b_ref, g_ref, st_ref, inv_ref, do_ref,
             dq_ref, dk_ref, dv_ref, db_ref, dg_ref, ds_s, gc_s, gct_s, dgc_s, dgct_s, dbeta_s):
        step = pl.program_id(0)
        ci = nchunk - 1 - step

        @pl.when(step == 0)
        def _():
            ds_s[...] = jnp.zeros_like(ds_s)

        @pl.when(ci >= n_real_chunks)
        def _():
            for r in (dq_ref, dk_ref, dv_ref, db_ref, dg_ref):
                r[...] = jnp.zeros_like(r)

        @pl.when(ci < n_real_chunks)
        def _():
            gc, gct = _cum_log_decay(g_ref[...])
            gc_s[...] = gc
            gct_s[...] = gct
            dgc_s[...] = jnp.zeros_like(dgc_s)
            dgct_s[...] = jnp.zeros_like(dgct_s)
            dbeta_s[...] = jnp.zeros_like(dbeta_s)

            def some_groups(it, carry):
                ids = [it * GDN_BWD_INTERLEAVE + u for u in range(GDN_BWD_INTERLEAVE)]
                ops = [_group_operands(gi, st_ref, q_ref, k_ref, v_ref, b_ref[...], gc_s[...], gct_s) for gi in ids]
                cts = [(do_ref[:, pl.ds(op[3], SW)], ds_s[gi]) for gi, op in zip(ids, ops)]
                known = [inv_ref[gi] for gi in ids]
                _, vjp_fn = jax.vjp(lambda gs: _gdn_groups(gs, known_inverses=known), [op[0] for op in ops])
                (grads,) = vjp_fn(cts)
                lane = lax.broadcasted_iota(jnp.int32, (C, 128), 1)
                dbeta_acc, dgc_acc = dbeta_s[...], dgc_s[...]
                for gi, (_, heads, qk_off, v_off), (dsp, dq2, dk2, dv4, dbcols, dgcols, dgrows) in zip(ids, ops, grads):
                    ds_s[gi] = dsp
                    dq_ref[:, pl.ds(qk_off, 2 * D)] = dq2
                    dk_ref[:, pl.ds(qk_off, 2 * D)] = dk2
                    dv_ref[:, pl.ds(v_off, SW)] = dv4
                    for h, dbcol, dgcol, dgrow in zip(heads, dbcols, dgcols, dgrows):
                        dbeta_acc = dbeta_acc + jnp.where(lane == h, dbcol, 0.0)
                        dgc_acc = dgc_acc + jnp.where(lane == h, dgcol, 0.0)
                        dgct_s[pl.ds(h, 1), :] = dgrow
                dbeta_s[...] = dbeta_acc
                dgc_s[...] = dgc_acc
                return carry

            lax.fori_loop(0, NG // GDN_BWD_INTERLEAVE, some_groups, 0)
            fold = (lax.broadcasted_iota(jnp.int32, (128, C), 0) % C == lax.broadcasted_iota(jnp.int32, (128, C), 1)).astype(f32)
            eye = (lax.broadcasted_iota(jnp.int32, (128, 128), 0) == lax.broadcasted_iota(jnp.int32, (128, 128), 1)).astype(f32)
            dgc = dgc_s[...] + _dsel(_dsel(dgct_s[...], fold, NN, 1), eye, TN, 1)
            upper = (lax.broadcasted_iota(jnp.int32, (C, C), 0) <= lax.broadcasted_iota(jnp.int32, (C, C), 1)).astype(f32)
            dg_ref[...] = _dsel(upper, dgc, NN, 0)
            db_ref[...] = dbeta_s[...]

    return pl.pallas_call(
        body, name=name, grid=(nchunk,),
        in_specs=[pl.BlockSpec((C, GDN_QK_W), rev), pl.BlockSpec((C, GDN_QK_W), rev), pl.BlockSpec((C, GDN_V_W), rev),
                  pl.BlockSpec((C, 128), rev), pl.BlockSpec((C, 128), rev),
                  pl.BlockSpec((None, NG, D, SW), lambda i: (nchunk - 1 - i, 0, 0, 0)),
                  pl.BlockSpec((None, NG, C, GDN_PACK * C), lambda i: (nchunk - 1 - i, 0, 0, 0)), pl.BlockSpec((C, GDN_V_W), rev)],
        out_specs=[pl.BlockSpec((C, GDN_QK_W), rev), pl.BlockSpec((C, GDN_QK_W), rev), pl.BlockSpec((C, GDN_V_W), rev),
                   pl.BlockSpec((C, 128), rev), pl.BlockSpec((C, 128), rev)],
        out_shape=[jax.ShapeDtypeStruct((lp, GDN_QK_W), f32)] * 2 + [jax.ShapeDtypeStruct((lp, GDN_V_W), f32)]
        + [jax.ShapeDtypeStruct((lp, 128), f32)] * 2,
        scratch_shapes=[pltpu.VMEM((NG, D, SW), f32), pltpu.VMEM((C, 128), f32), pltpu.VMEM((128, 128), f32),
                        pltpu.VMEM((C, 128), f32), pltpu.VMEM((128, 128), f32), pltpu.VMEM((C, 128), f32)],
        compiler_params=pltpu.CompilerParams(dimension_semantics=("arbitrary",)),
    )(qn, kn, v, beta, g, states, inverses, do)


def _attention_bias(tq, tk):
    kk = lax.broadcasted_iota(jnp.int32, (tk, tq), 0)
    qq = lax.broadcasted_iota(jnp.int32, (tk, tq), 1)
    pad = jnp.where(kk < PAD_FRONT, NEG, 0.0)
    diag = [jnp.where(qq >= kk + d * tk, 0.0, NEG) for d in range(tq // tk)]
    return jnp.stack([pad] + diag + [jnp.minimum(pad, diag[0])]).astype(f32)


def _att_tiles(lp):
    tq = _pick(lp, (768, 512, 256))
    return tq, _pick(tq, ATT_KEY_TILES)


def _attention_fwd(q, kvu, kr, proj2, *, name):
    lp = q.shape[0]
    H = MLA_HEADS
    tq, tk = _att_tiles(lp)
    r = tq // tk

    def body(q_ref, kn_ref, kr_ref, v_ref, z_ref, bias_ref, o_ref, lse_ref, og_ref, m_s, l_s, acc_s, sa_s, sb_s):
        qi = pl.program_id(1)
        m_s[...] = jnp.full_like(m_s, NEG)
        l_s[...] = jnp.zeros_like(l_s)
        acc_s[...] = jnp.zeros_like(acc_s)

        def scores(ki):
            k0 = pl.multiple_of(ki * tk, tk)
            k = jnp.concatenate([kn_ref[pl.ds(k0, tk), :], kr_ref[pl.ds(k0, tk), :]], axis=1)
            return lax.dot_general(k, q_ref[...], (NT, ((), ())), preferred_element_type=f32)

        def consume(st, ki, mask):
            k0 = pl.multiple_of(ki * tk, tk)
            if mask is not None:
                st = st + bias_ref[mask]
            m_prev = m_s[...]
            m_new = jnp.maximum(m_prev, jnp.max(st, axis=0, keepdims=True))
            alpha = jnp.exp(m_prev - m_new)
            p = jnp.exp(st - m_new)
            l_s[...] = alpha * l_s[...] + jnp.sum(p, axis=0, keepdims=True)
            acc_s[...] = alpha * acc_s[...] + lax.dot_general(v_ref[pl.ds(k0, tk), :], p.astype(bf16), (TN, ((), ())),
                                                              preferred_element_type=f32)
            m_s[...] = m_new

        n_full = qi * r

        def chain(blocks):
            bufs = (sa_s, sb_s)
            for j, (ki, masked) in enumerate(blocks):
                if j + 1 < len(blocks):
                    bufs[(j + 1) % 2][...] = scores(blocks[j + 1][0])
                consume(bufs[j % 2][...], ki, masked)

        diagonal = [(n_full + d, 1 + d) for d in range(r)]

        @pl.when(qi == 0)
        def _():
            sa_s[...] = scores(0)
            chain([(0, r + 1)] + diagonal[1:])

        @pl.when(qi > 0)
        def _():
            sb_s[...] = scores(0)
            sa_s[...] = scores(1)
            consume(sb_s[...], 0, 0)
            n_pairs = (n_full - 1) // 2

            def two(pi, carry):
                ki = 1 + 2 * pi
                sb_s[...] = scores(ki + 1)
                consume(sa_s[...], ki, None)
                sa_s[...] = scores(ki + 2)
                consume(sb_s[...], ki + 1, None)
                return carry

            lax.fori_loop(0, n_pairs, two, 0)
            nxt = 1 + 2 * n_pairs

            @pl.when(nxt < n_full)
            def _():
                chain([(nxt, None)] + diagonal)

            @pl.when(nxt == n_full)
            def _():
                chain(diagonal)
        o = jnp.transpose(acc_s[...] / l_s[...])
        o_ref[...] = o
        og_ref[...] = (o * _silu(z_ref[...].astype(f32))).astype(og_ref.dtype)
        lse_ref[...] = m_s[...] + jnp.log(l_s[...])

    return pl.pallas_call(
        body, name=name, grid=(H, lp // tq),
        in_specs=[pl.BlockSpec((tq, MLA_QKP), lambda h, qi: (qi, h)),
                  pl.BlockSpec((lp, 128), lambda h, qi: (0, h)),
                  pl.BlockSpec((lp, 128), lambda h, qi: (0, 0)),
                  pl.BlockSpec((lp, 128), lambda h, qi: (0, H + h)),
                  pl.BlockSpec((tq, 128), lambda h, qi: (qi, MLA_Q_RANK // 128 + h)),
                  pl.BlockSpec((r + 2, tk, tq), lambda h, qi: (0, 0, 0))],
        out_specs=[pl.BlockSpec((tq, 128), lambda h, qi: (qi, h)),
                   pl.BlockSpec((None, 1, tq), lambda h, qi: (h, 0, qi)),
                   pl.BlockSpec((tq, 128), lambda h, qi: (qi, h))],
        out_shape=[jax.ShapeDtypeStruct((lp, MLA_V_W), f32), jax.ShapeDtypeStruct((H, 1, lp), f32),
                   jax.ShapeDtypeStruct((lp, MLA_V_W), bf16)],
        scratch_shapes=[pltpu.VMEM((1, tq), f32), pltpu.VMEM((1, tq), f32), pltpu.VMEM((128, tq), f32),
                        pltpu.VMEM((tk, tq), f32), pltpu.VMEM((tk, tq), f32)],
        compiler_params=pltpu.CompilerParams(dimension_semantics=("arbitrary", "arbitrary")),
    )(q, kvu, kr, kvu, proj2, _attention_bias(tq, tk))


def _gate_bwd(o, proj2, dgated, *, name):
    lp = o.shape[0]
    H, w = MLA_HEADS, 2 * MLA_V
    tq = _pick(lp, (768, 512, 256))

    def body(o_ref, z_ref, g_ref, do_ref, dz_ref, dl_ref):
        ov, z, g = o_ref[...], z_ref[...].astype(f32), g_ref[...]
        s = _sigmoid(z)
        do = (g * (z * s)).astype(bf16)
        do_ref[...] = do
        dz_ref[...] = (g * ov * (s * (1.0 + z * (1.0 - s)))).astype(dz_ref.dtype)
        prod = ov * do.astype(f32)
        for u in range(2):
            dl_ref[u] = jnp.sum(jnp.transpose(prod[:, u * MLA_V:(u + 1) * MLA_V]), axis=0, keepdims=True)

    blk = pl.BlockSpec((tq, w), lambda j, qi: (qi, j))
    return pl.pallas_call(
        body, name=name, grid=(H // 2, lp // tq),
        in_specs=[blk, pl.BlockSpec((tq, w), lambda j, qi: (qi, j + MLA_Q_RANK // w)), blk],
        out_specs=[blk, blk, pl.BlockSpec((2, 1, tq), lambda j, qi: (j, 0, qi))],
        out_shape=[jax.ShapeDtypeStruct((lp, MLA_V_W), bf16), jax.ShapeDtypeStruct((lp, MLA_V_W), bf16),
                   jax.ShapeDtypeStruct((H, 1, lp), f32)],
    )(o, proj2, dgated)


def _attention_bwd(q, kvu, kr, lse, delta, do, *, name):
    lp = q.shape[0]
    tq, t = _att_tiles(lp)
    H, nb = MLA_HEADS, lp // t
    r, nq = tq // t, lp // tq

    def body(q_ref, kn_ref, kr_ref, v_ref, lse_ref, dl_ref, do_ref, bias_ref, dq_ref, dkn_ref, dv_ref, dkr_ref, dk_s, dv_s,
             sa_s, da_s, sb_s, db_s):
        ki = pl.program_id(1)
        k = jnp.concatenate([kn_ref[...], kr_ref[...]], axis=1)
        vv = v_ref[...]
        dk_s[...] = jnp.zeros_like(dk_s)
        dv_s[...] = jnp.zeros_like(dv_s)

        def products(qi, s_ref, d_ref):
            q0 = pl.multiple_of(qi * tq, tq)
            s_ref[...] = lax.dot_general(k, q_ref[pl.ds(q0, tq), :], (NT, ((), ())), preferred_element_type=f32)
            d_ref[...] = lax.dot_general(vv, do_ref[pl.ds(q0, tq), :], (NT, ((), ())), preferred_element_type=f32)

        def accumulate(s_ref, d_ref, qi, mask, first):
            q0 = pl.multiple_of(qi * tq, tq)
            qv = q_ref[pl.ds(q0, tq), :]
            dob = do_ref[pl.ds(q0, tq), :]
            st = s_ref[...]
            if mask is not None:
                st = st + bias_ref[mask]
            p = jnp.exp(st - lse_ref[:, pl.ds(q0, tq)])
            dv_s[...] += jnp.dot(p.astype(bf16), dob, preferred_element_type=f32)
            ds = (p * (d_ref[...] - dl_ref[:, pl.ds(q0, tq)])).astype(bf16)
            dk_s[...] += jnp.dot(ds, qv, preferred_element_type=f32)
            dq = lax.dot_general(ds, k, (TN, ((), ())), preferred_element_type=f32)
            if first:
                dq_ref[pl.ds(q0, tq), :] = dq
            else:
                dq_ref[pl.ds(q0, tq), :] += dq

        def sweep(qd, first_mask, other_mask, first):
            last = nq - 1
            products(qd, sa_s, da_s)
            products(jnp.minimum(qd + 1, last), sb_s, db_s)
            accumulate(sa_s, da_s, qd, first_mask, first)
            n = last - qd

            def two(pi, carry):
                i = qd + 1 + 2 * pi
                products(i + 1, sa_s, da_s)
                accumulate(sb_s, db_s, i, other_mask, first)
                products(jnp.minimum(i + 2, last), sb_s, db_s)
                accumulate(sa_s, da_s, i + 1, other_mask, first)
                return carry

            lax.fori_loop(0, n // 2, two, 0)

            @pl.when(n % 2 == 1)
            def _():
                accumulate(sb_s, db_s, last, other_mask, first)

        @pl.when(ki == 0)
        def _():
            sweep(0, r + 1, 0, True)

        @pl.when(ki > 0)
        def _():
            sweep(ki // r, 1 + ki % r, None, False)

        dkn_ref[...] = dk_s[:, :128].astype(dkn_ref.dtype)
        dkr_ref[...] = dk_s[:, 128:]
        dv_ref[...] = dv_s[...].astype(dv_ref.dtype)

    return pl.pallas_call(
        body, name=name, grid=(H, nb),
        in_specs=[pl.BlockSpec((lp, MLA_QKP), lambda h, ki: (0, h)),
                  pl.BlockSpec((t, 128), lambda h, ki: (ki, h)),
                  pl.BlockSpec((t, 128), lambda h, ki: (ki, 0)),
                  pl.BlockSpec((t, 128), lambda h, ki: (ki, H + h)),
                  pl.BlockSpec((None, 1, lp), lambda h, ki: (h, 0, 0)),
                  pl.BlockSpec((None, 1, lp), lambda h, ki: (h, 0, 0)),
                  pl.BlockSpec((lp, 128), lambda h, ki: (0, h)),
                  pl.BlockSpec((r + 2, t, tq), lambda h, ki: (0, 0, 0))],
        out_specs=[pl.BlockSpec((lp, MLA_QKP), lambda h, ki: (0, h)),
                   pl.BlockSpec((t, 128), lambda h, ki: (ki, h)),
                   pl.BlockSpec((t, 128), lambda h, ki: (ki, h)),
                   pl.BlockSpec((t, 128), lambda h, ki: (ki, h))],
        out_shape=[jax.ShapeDtypeStruct((lp, H * MLA_QKP), f32), jax.ShapeDtypeStruct((lp, MLA_V_W), bf16),
                   jax.ShapeDtypeStruct((lp, MLA_V_W), bf16), jax.ShapeDtypeStruct((lp, MLA_V_W), f32)],
        scratch_shapes=[pltpu.VMEM((t, MLA_QKP), f32), pltpu.VMEM((t, 128), f32)] + [pltpu.VMEM((t, tq), f32)] * 4,
        compiler_params=pltpu.CompilerParams(dimension_semantics=("arbitrary", "arbitrary")),
    )(q, kvu, kr, kvu, lse, delta, do, _attention_bias(tq, t))


def _q_proj_fwd(c_q, wq2, cos_q, sin_q, *, name, tm=ROW_TILE):
    lp, rank = c_q.shape
    gw = Q_GROUP * MLA_QKP
    ng = wq2.shape[1] // (2 * gw)

    def body(c_ref, w_ref, cos_ref, sin_ref, o_ref):
        qq = jnp.dot(c_ref[...], w_ref[...], preferred_element_type=f32)
        o_ref[...] = _st_q_rope([qq], [cos_ref[...], sin_ref[...]], [])[0].astype(o_ref.dtype)

    tab = pl.BlockSpec((tm, MLA_QKP), lambda j, i: (i, 0))
    return pl.pallas_call(
        body, name=name, grid=(ng, lp // tm),
        in_specs=[pl.BlockSpec((tm, rank), lambda j, i: (i, 0)), pl.BlockSpec((rank, 2 * gw), lambda j, i: (0, j)), tab, tab],
        out_specs=pl.BlockSpec((tm, gw), lambda j, i: (i, j)),
        out_shape=jax.ShapeDtypeStruct((lp, ng * gw), bf16),
    )(c_q, wq2, cos_q, sin_q)


def _q_proj_bwd(dq, c_q, wq2, cos_q, sin_q, *, name, tm=ROW_TILE):
    lp, rank = c_q.shape
    gw = Q_GROUP * MLA_QKP
    ng = wq2.shape[1] // (2 * gw)

    def body(dq_ref, c_ref, w_ref, cos_ref, sin_ref, dc_ref, dw_ref):
        dqq = _st_q_rope_t([dq_ref[...]], [cos_ref[...], sin_ref[...]], [])[0].astype(bf16)
        dc_ref[...] = lax.dot_general(dqq, w_ref[...], (NT, ((), ())), preferred_element_type=f32)
        dw = lax.dot_general(c_ref[...], dqq, (TN, ((), ())), preferred_element_type=f32)

        @pl.when(pl.program_id(1) == 0)
        def _():
            dw_ref[...] = dw

        @pl.when(pl.program_id(1) > 0)
        def _():
            dw_ref[...] += dw

    tab = pl.BlockSpec((tm, MLA_QKP), lambda j, i: (i, 0))
    return pl.pallas_call(
        body, name=name, grid=(ng, lp // tm),
        in_specs=[pl.BlockSpec((tm, gw), lambda j, i: (i, j)), pl.BlockSpec((tm, rank), lambda j, i: (i, 0)),
                  pl.BlockSpec((rank, 2 * gw), lambda j, i: (0, j)), tab, tab],
        out_specs=[pl.BlockSpec((tm, rank), lambda j, i: (i, j)), pl.BlockSpec((rank, 2 * gw), lambda j, i: (0, j))],
        out_shape=[jax.ShapeDtypeStruct((lp, ng * rank), f32), jax.ShapeDtypeStruct(wq2.shape, f32)],
        compiler_params=pltpu.CompilerParams(dimension_semantics=("arbitrary", "arbitrary")),
    )(dq, c_q, wq2, cos_q, sin_q)


def _exchange_copies(x_ref, o_ref, send_sems, recv_sems, local_sem, gather):
    mx, my, mc = lax.axis_index("x"), lax.axis_index("y"), lax.axis_index("c")
    me = 4 * mx + 2 * my + mc
    own = pltpu.make_async_copy(x_ref if gather else x_ref.at[me], o_ref.at[me], local_sem)
    sends, arrivals = [], []
    for k in range(1, N_DEV):
        px = 1 - mx if k & 4 else mx
        py = 1 - my if k & 2 else my
        pc = 1 - mc if k & 1 else mc
        peer = 4 * px + 2 * py + pc
        sends.append(pltpu.make_async_remote_copy(
            src_ref=x_ref if gather else x_ref.at[peer], dst_ref=o_ref.at[me],
            send_sem=send_sems.at[k - 1], recv_sem=recv_sems.at[k - 1],
            device_id=(px, py, pc), device_id_type=MESH_ID))
        arrivals.append(pltpu.make_async_remote_copy(
            src_ref=o_ref.at[peer], dst_ref=o_ref.at[peer],
            send_sem=send_sems.at[k - 1], recv_sem=recv_sems.at[k - 1],
            device_id=(mx, my, mc), device_id_type=MESH_ID))
    return own, sends, arrivals


def _exchange_start(copies):
    own, sends, _ = copies
    own.start()
    for cp in sends:
        cp.start()


def _exchange_wait(copies):
    own, sends, arrivals = copies
    for cp in arrivals:
        cp.wait_recv()
    for cp in sends:
        cp.wait_send()
    own.wait()


_EXCHANGE_SCRATCH = [pltpu.SemaphoreType.DMA((N_DEV - 1,)), pltpu.SemaphoreType.DMA((N_DEV - 1,)), pltpu.SemaphoreType.DMA]


def _exchange(x, *, gather, name):
    blk = x.shape if gather else x.shape[1:]

    def body(x_ref, o_ref, send_sems, recv_sems, local_sem):
        copies = _exchange_copies(x_ref, o_ref, send_sems, recv_sems, local_sem, gather)
        _exchange_start(copies)
        _exchange_wait(copies)

    return pl.pallas_call(
        body, name=name,
        in_specs=[pl.BlockSpec(memory_space=pltpu.HBM)], out_specs=pl.BlockSpec(memory_space=pltpu.HBM),
        out_shape=jax.ShapeDtypeStruct((N_DEV,) + tuple(blk), x.dtype),
        scratch_shapes=list(_EXCHANGE_SCRATCH),
    )(x)


def _reduce_adamw(parts, w, m, v, *, name):
    r = w.shape[0]
    tr = max(d for d in range(16, 3201, 16) if r % d == 0)

    def body(p_ref, w_ref, m_ref, v_ref, g_ref, d_ref, nm_ref, nv_ref):
        g = p_ref[0].astype(f32)
        for s in range(1, N_DEV):
            g = g + p_ref[s].astype(f32)
        mm = ADAM_B1 * m_ref[...] + (1.0 - ADAM_B1) * g
        vv = ADAM_B2 * v_ref[...] + (1.0 - ADAM_B2) * (g * g)
        m_hat = mm / (1.0 - ADAM_B1 ** ADAM_STEP)
        v_hat = vv / (1.0 - ADAM_B2 ** ADAM_STEP)
        g_ref[...] = g
        d_ref[...] = -ADAM_LR * (m_hat / (jnp.sqrt(v_hat) + ADAM_EPS) + ADAM_WD * w_ref[...])
        nm_ref[...] = mm
        nv_ref[...] = vv

    spec = pl.BlockSpec((tr, 128), lambda i: (i, 0))
    return pl.pallas_call(
        body, name=name, grid=(r // tr,),
        in_specs=[pl.BlockSpec((N_DEV, tr, 128), lambda i: (0, i, 0)), spec, spec, spec],
        out_specs=[spec] * 4, out_shape=[jax.ShapeDtypeStruct((r, 128), f32)] * 4,
    )(parts, w, m, v)


_SHARDED = ("gdn_w_in", "gdn_w_out", "kv_w_down", "kv_w_up", "mla_w_in", "mla_w_q_up", "mla_w_out", "meta_tokens", "gdn_conv_w")
_COL_SHARDED = {"gdn_w_in", "kv_w_up", "mla_w_in", "mla_w_q_up", "meta_tokens", "gdn_conv_w"}
_GATHER_FIRST = ("gdn_w_in",)
_GATHER_F32 = ("meta_tokens", "gdn_conv_w")
_GATHER_REST = ("gdn_w_out", "kv_w_down", "kv_w_up", "mla_w_in", "mla_w_q_up", "mla_w_out")
_SCATTER_EARLY = ("gdn_w_out", "kv_w_down", "kv_w_up", "mla_w_in", "mla_w_q_up", "mla_w_out")
_SCATTER_LATE = ("gdn_w_in", "meta_tokens", "gdn_conv_w")
_REPLICATED = ("pre_norm", "post_norm", "gdn_a_log", "gdn_dt_bias", "gdn_out_norm", "kv_norm", "kv_latent_norm",
               "mla_q_latent_norm")


def _rows128(a):
    flat = a.reshape(-1)
    pad = (-flat.shape[0]) % 128
    if pad:
        flat = jnp.pad(flat, (0, pad))
    return flat.reshape(-1, 128)


def _pack(arrs, row_multiple):
    parts = [_rows128(a) for a in arrs]
    buf = jnp.concatenate(parts, axis=0)
    pad = (-buf.shape[0]) % row_multiple
    if pad:
        buf = jnp.pad(buf, ((0, pad), (0, 0)))
    return buf


def _unpack(buf, shapes):
    out, r = [], 0
    for shp in shapes:
        n = math.prod(shp)
        rows = -(-n // 128)
        out.append(buf[r:r + rows].reshape(-1)[:n].reshape(shp))
        r += rows
    return out


def _unshard(g, full_shape, col):
    if col:
        return jnp.transpose(g, (1, 0, 2)).reshape(full_shape)
    return g.reshape(full_shape)


def _to_shards(a, col):
    r, c = a.shape
    if col:
        return jnp.transpose(a.reshape(r, N_DEV, c // N_DEV), (1, 0, 2))
    return a.reshape(N_DEV, r // N_DEV, c)


def _pad_cols(a, width):
    return jnp.pad(a, ((0, 0), (0, width - a.shape[1])))


def _rope_tables(lp):
    inv = ROPE_THETA ** (-jnp.arange(0, MLA_ROPE, 2, dtype=f32) / MLA_ROPE)
    pos = (jnp.arange(lp, dtype=jnp.int32) - PAD_FRONT).astype(f32)
    ang = pos[:, None] * inv[None, :]
    cos, sin = jnp.cos(ang), jnp.sin(ang)
    z = jnp.zeros((lp, 64), f32)
    return jnp.concatenate([cos, cos, z], axis=1), jnp.concatenate([-sin, sin, z], axis=1)


def kernel(x, meta_tokens, pre_norm, post_norm, gdn_w_in, gdn_conv_w, gdn_a_log, gdn_dt_bias, gdn_out_norm, gdn_w_out, kv_norm, kv_w_down, kv_latent_norm, kv_w_up, mla_w_in, mla_q_latent_norm, mla_w_q_up, mla_w_out, loss_target, m_meta_tokens, m_pre_norm, m_post_norm, m_gdn_w_in, m_gdn_conv_w, m_gdn_a_log, m_gdn_dt_bias, m_gdn_out_norm, m_gdn_w_out, m_kv_norm, m_kv_w_down, m_kv_latent_norm, m_kv_w_up, m_mla_w_in, m_mla_q_latent_norm, m_mla_w_q_up, m_mla_w_out, v_meta_tokens, v_pre_norm, v_post_norm, v_gdn_w_in, v_gdn_conv_w, v_gdn_a_log, v_gdn_dt_bias, v_gdn_out_norm, v_gdn_w_out, v_kv_norm, v_kv_w_down, v_kv_latent_norm, v_kv_w_up, v_mla_w_in, v_mla_q_latent_norm, v_mla_w_q_up, v_mla_w_out):
    W = dict(meta_tokens=meta_tokens, pre_norm=pre_norm, post_norm=post_norm, gdn_w_in=gdn_w_in, gdn_conv_w=gdn_conv_w,
             gdn_a_log=gdn_a_log, gdn_dt_bias=gdn_dt_bias, gdn_out_norm=gdn_out_norm, gdn_w_out=gdn_w_out, kv_norm=kv_norm,
             kv_w_down=kv_w_down, kv_latent_norm=kv_latent_norm, kv_w_up=kv_w_up, mla_w_in=mla_w_in,
             mla_q_latent_norm=mla_q_latent_norm, mla_w_q_up=mla_w_q_up, mla_w_out=mla_w_out)
    M = dict(meta_tokens=m_meta_tokens, pre_norm=m_pre_norm, post_norm=m_post_norm, gdn_w_in=m_gdn_w_in, gdn_conv_w=m_gdn_conv_w,
             gdn_a_log=m_gdn_a_log, gdn_dt_bias=m_gdn_dt_bias, gdn_out_norm=m_gdn_out_norm, gdn_w_out=m_gdn_w_out, kv_norm=m_kv_norm,
             kv_w_down=m_kv_w_down, kv_latent_norm=m_kv_latent_norm, kv_w_up=m_kv_w_up, mla_w_in=m_mla_w_in,
             mla_q_latent_norm=m_mla_q_latent_norm, mla_w_q_up=m_mla_w_q_up, mla_w_out=m_mla_w_out)
    V = dict(meta_tokens=v_meta_tokens, pre_norm=v_pre_norm, post_norm=v_post_norm, gdn_w_in=v_gdn_w_in, gdn_conv_w=v_gdn_conv_w,
             gdn_a_log=v_gdn_a_log, gdn_dt_bias=v_gdn_dt_bias, gdn_out_norm=v_gdn_out_norm, gdn_w_out=v_gdn_w_out, kv_norm=v_kv_norm,
             kv_w_down=v_kv_w_down, kv_latent_norm=v_kv_latent_norm, kv_w_up=v_kv_w_up, mla_w_in=v_mla_w_in,
             mla_q_latent_norm=v_mla_q_latent_norm, mla_w_q_up=v_mla_w_q_up, mla_w_out=v_mla_w_out)
    order = list(W)

    n_tok = x.shape[1]
    assert n_tok % GDN_CHUNK == 0
    n_real = ROW0 + n_tok
    lp = -(-n_real // ROW_TILE) * ROW_TILE
    n_real_chunks = n_real // GDN_CHUNK

    shard2d = {n: W[n].reshape(W[n].shape[-2:]) for n in _SHARDED}
    full_shape = {n: ((s.shape[0], s.shape[1] * N_DEV) if n in _COL_SHARDED else (s.shape[0] * N_DEV, s.shape[1]))
                  for n, s in shard2d.items()}
    full = {}

    def unpack_gathered(names, buf):
        r = 0
        for n in names:
            shp = shard2d[n].shape
            rows = math.prod(shp) // 128
            blocks = buf[:, r:r + rows].reshape((N_DEV,) + shp)
            full[n] = _unshard(blocks, full_shape[n], n in _COL_SHARDED)
            r += rows

    unpack_gathered(_GATHER_FIRST, _exchange(_pack([shard2d[n].astype(bf16) for n in _GATHER_FIRST], 16), gather=True,
                                             name="gather_w_in"))
    unpack_gathered(_GATHER_F32, _exchange(_pack([shard2d[n] for n in _GATHER_F32], 8), gather=True, name="gather_meta_conv"))
    rest_shards = _pack([shard2d[n].astype(bf16) for n in _GATHER_REST], 16)

    h0 = jnp.concatenate([jnp.zeros((PAD_FRONT, D_MODEL), f32), full["meta_tokens"], x[0],
                          jnp.zeros((lp - n_real, D_MODEL), f32)], axis=0)
    tgt = jnp.concatenate([jnp.zeros((ROW0, D_MODEL), f32), loss_target[0], jnp.zeros((lp - n_real, D_MODEL), f32)], axis=0)
    cos_k, sin_k = _rope_tables(lp)
    one = jnp.ones((lp, 128), f32)
    cos_q = jnp.concatenate([one, cos_k], axis=1)
    sin_q = jnp.concatenate([jnp.zeros((lp, 128), f32), sin_k], axis=1)

    w_in = full["gdn_w_in"]
    s1 = GDN_CONV_W + GDN_V_W
    w_in_p = jnp.concatenate([w_in[:, :s1], _pad_cols(w_in[:, s1:s1 + 16], 128), _pad_cols(w_in[:, s1 + 16:], 128)], axis=1)
    pre0, pre1 = pre_norm[0:1], pre_norm[1:2]
    post0, post1 = post_norm[0:1], post_norm[1:2]
    (hn0,) = _rowwise(_st_prenorm, [(h0, None, 0)], [], [pre0], [(D_MODEL, bf16, None)], name="f_prenorm0")
    proj, g_rest = _matmul(hn0, w_in_p, out_dtype=bf16, name="f_gdn_in", side=(rest_shards, True))
    unpack_gathered(_GATHER_REST, g_rest)

    wd = full["kv_w_down"]
    zc = jnp.zeros((D_MODEL, 64), bf16)
    wd2 = jnp.concatenate([wd, zc, jnp.zeros((D_MODEL, 128), bf16), wd[:, 160:192], wd[:, 128:160], zc], axis=1)
    wup_p = jnp.transpose(full["kv_w_up"].reshape(MLA_KV_RANK, MLA_HEADS, 2, 128), (0, 2, 1, 3)).reshape(MLA_KV_RANK, 2 * MLA_V_W)
    wq = full["mla_w_q_up"].reshape(MLA_Q_RANK, MLA_HEADS, MLA_QK)
    zq64 = jnp.zeros((MLA_Q_RANK, MLA_HEADS, 64), bf16)
    wq_plain = jnp.concatenate([wq, zq64], axis=2).reshape(MLA_Q_RANK, MLA_HEADS * MLA_QKP)
    wq_swap = jnp.concatenate([jnp.zeros((MLA_Q_RANK, MLA_HEADS, 128), bf16), wq[:, :, 160:192], wq[:, :, 128:160], zq64],
                              axis=2).reshape(MLA_Q_RANK, MLA_HEADS * MLA_QKP)
    q_half = Q_GROUP * MLA_QKP
    wq2 = jnp.concatenate([wq_plain[:, :q_half], wq_swap[:, :q_half], wq_plain[:, q_half:], wq_swap[:, q_half:]], axis=1)
    w_mla_in, w_gdn_out, w_mla_out = full["mla_w_in"], full["gdn_w_out"], full["mla_w_out"]
    conv_w = full["gdn_conv_w"]
    alog_p, dtb_p = _pad_cols(gdn_a_log, 128), _pad_cols(gdn_dt_bias, 128)
    kvn, kvln = kv_norm.reshape(1, -1), kv_latent_norm.reshape(1, -1)

    conv = _conv_fwd(proj, conv_w, col_blocks=GDN_CONV_W // CONV_BC, name="f_conv")
    (qn,) = _rowwise(_st_gdn_q, [(conv, GDN_QK_W, 0)], [], [], [(GDN_QK_W, bf16, GDN_QK_W)], heads=GDN_QK_HEADS, name="f_gdn_q")
    (kn,) = _rowwise(_st_gdn_k, [(conv, GDN_QK_W, 1)], [], [], [(GDN_QK_W, bf16, GDN_QK_W)], heads=GDN_QK_HEADS, name="f_gdn_k")
    (vv,) = _rowwise(_st_gdn_v, [(conv, GDN_V_W, 1)], [], [], [(GDN_V_W, bf16, GDN_V_W)], name="f_gdn_v")
    gate_rows = [(proj, 128, s1 // 128), (proj, 128, s1 // 128 + 1)]
    beta, gdec = _rowwise(_st_gdn_gate, gate_rows, [], [alog_p, dtb_p], [(128, f32, None)] * 2, name="f_gdn_gate")
    o_gdn, states, inverses = _gdn_fwd(qn, kn, vv, beta, gdec, n_real_chunks=n_real_chunks, name="f_gdn")
    out_rows = [(o_gdn, GDN_V_W, 0), (proj, GDN_V_W, GDN_CONV_W // GDN_V_W)]
    (og,) = _rowwise(_st_gdn_out, out_rows, [], [gdn_out_norm], [(GDN_V_W, bf16, GDN_V_W)], heads=GDN_V_HEADS, name="f_gdn_out")
    y0 = _matmul(og, w_gdn_out, name="f_gdn_wout")
    mid_rows = [(h0, None, 0), (y0, None, 0)]
    h1, hn1, hkv = _rowwise(_st_mid, mid_rows, [], [post0, pre1, kvn],
                            [(D_MODEL, f32, None), (D_MODEL, bf16, None), (D_MODEL, bf16, None)], name="f_mid")
    ckr = _matmul(hkv, wd2, name="f_kv_down")
    proj2 = _matmul(hn1, w_mla_in, out_dtype=bf16, name="f_mla_in")
    lat_rows = [(ckr, None, 0), (proj2, MLA_Q_RANK, 0)]
    lat_nd = [(cos_k, None, 0), (sin_k, None, 0)]
    c_kv, k_rope, c_q = _rowwise(_st_latent, lat_rows, lat_nd, [kvln, mla_q_latent_norm],
                                 [(128, bf16, None), (128, bf16, None), (MLA_Q_RANK, bf16, None)], name="f_latent")
    kvu = _matmul(c_kv, wup_p, out_dtype=bf16, name="f_kv_up")
    q_att = _q_proj_fwd(c_q, wq2, cos_q, sin_q, name="f_q_proj")
    o_att, lse, og2 = _attention_fwd(q_att, kvu, k_rope, proj2, name="f_attention")
    y1 = _matmul(og2, w_mla_out, name="f_mla_wout")
    st_loss = _make_st_loss(n_tok)
    loss_rows_in = [(h1, None, 0), (y1, None, 0)]
    (loss_rows,) = _rowwise(st_loss, loss_rows_in, [(tgt, None, 0)], [post1], [(1, f32, None)], name="f_loss")
    loss_here = jnp.sum(loss_rows)

    ones_ct = jnp.ones((lp, 1), f32)
    (dh1_a, dy1), (dpost1,) = _rowwise_vjp(st_loss, loss_rows_in, [(tgt, None, 0)], [post1], [(ones_ct, None, 0)], name="b_loss")
    dog2 = _matmul(dy1, w_mla_out, tb=True, name="b_mla_wout_x")
    dw_mla_out = _matmul(og2, dy1, ta=True, name="b_mla_wout_w")
    do_att, dz2, delta = _gate_bwd(o_att, proj2, dog2, name="b_mla_gate")
    dq_att, dkn, dvv, dkr_h = _attention_bwd(q_att, kvu, k_rope, lse, delta, do_att, name="b_attention")
    dc_q, dwq2 = _q_proj_bwd(dq_att, c_q, wq2, cos_q, sin_q, name="b_q_proj")
    dkvu = jnp.concatenate([dkn, dvv], axis=1)
    dc_kv = _matmul(dkvu, wup_p, tb=True, name="b_kv_up_x")
    dwup_p = _matmul(c_kv, dkvu, ta=True, name="b_kv_up_w")

    def lat_ct(cv):
        dkr = cv[1][:, 0:128]
        for h in range(1, MLA_HEADS):
            dkr = dkr + cv[1][:, h * 128:(h + 1) * 128]
        return [cv[0], dkr, cv[2][:, :MLA_Q_RANK] + cv[2][:, MLA_Q_RANK:]]

    (dckr, dcq_pre), (dkvln, dqln) = _rowwise_vjp(
        _st_latent, lat_rows, lat_nd, [kvln, mla_q_latent_norm],
        [(dc_kv, None, 0), (dkr_h, None, 0), (dc_q, None, 0)], ct_pre=lat_ct, name="b_latent", grad_dtypes=[bf16, bf16])
    dproj2 = jnp.concatenate([dcq_pre, dz2], axis=1)
    dhn1 = _matmul(dproj2, w_mla_in, tb=True, name="b_mla_in_x")
    dw_mla_in = _matmul(hn1, dproj2, ta=True, name="b_mla_in_w")
    dhkv = _matmul(dckr, wd2, tb=True, name="b_kv_down_x")
    dwd2 = _matmul(hkv, dckr, ta=True, name="b_kv_down_w")
    (dh0_a, dy0), (dpost0, dpre1, dkvn) = _rowwise_vjp(
        _st_mid, mid_rows, [], [post0, pre1, kvn], [(dh1_a, None, 0), (dhn1, None, 0), (dhkv, None, 0)], name="b_mid")
    dog = _matmul(dy0, w_gdn_out, tb=True, name="b_gdn_wout_x")
    dw_gdn_out = _matmul(og, dy0, ta=True, name="b_gdn_wout_w")
    (do_gdn, dz), (dout_norm,) = _rowwise_vjp(_st_gdn_out, out_rows, [], [gdn_out_norm], [(dog, GDN_V_W, 0)], heads=GDN_V_HEADS,
                                              name="b_gdn_out", grad_dtypes=[f32, bf16])
    dq_g, dk_g, dv_g, dbeta, dgdec = _gdn_bwd(qn, kn, vv, beta, gdec, states, inverses, do_gdn, n_real_chunks=n_real_chunks,
                                              name="b_gdn")
    (db_col, da_col), (dalog_p, ddtb_p) = _rowwise_vjp(
        _st_gdn_gate, gate_rows, [], [alog_p, dtb_p], [(dbeta, None, 0), (dgdec, None, 0)], name="b_gdn_gate",
        grad_dtypes=[bf16, bf16])
    (dconv_q,), _ = _rowwise_vjp(_st_gdn_q, [(conv, GDN_QK_W, 0)], [], [], [(dq_g, GDN_QK_W, 0)], heads=GDN_QK_HEADS, name="b_gdn_q")
    (dconv_k,), _ = _rowwise_vjp(_st_gdn_k, [(conv, GDN_QK_W, 1)], [], [], [(dk_g, GDN_QK_W, 0)], heads=GDN_QK_HEADS, name="b_gdn_k")
    (dconv_v,), _ = _rowwise_vjp(_st_gdn_v, [(conv, GDN_V_W, 1)], [], [], [(dv_g, GDN_V_W, 0)], name="b_gdn_v")
    nq_b = GDN_QK_W // CONV_BC
    dpre_q, dcw_q = _conv_bwd(dconv_q, proj, conv_w, x_off=0, w_off=0, name="b_conv_q")
    dpre_k, dcw_k = _conv_bwd(dconv_k, proj, conv_w, x_off=nq_b, w_off=nq_b, name="b_conv_k")
    dpre_v, dcw_v = _conv_bwd(dconv_v, proj, conv_w, x_off=2 * nq_b, w_off=2 * nq_b, name="b_conv_v")
    dproj = jnp.concatenate([dpre_q, dpre_k, dpre_v, dz, db_col, da_col], axis=1)
    G = {}
    G["kv_w_down"] = jnp.concatenate([dwd2[:, :128], dwd2[:, 128:160] + dwd2[:, 416:448], dwd2[:, 160:192] + dwd2[:, 384:416]], axis=1)
    G["kv_w_up"] = jnp.transpose(dwup_p.reshape(MLA_KV_RANK, 2, MLA_HEADS, 128), (0, 2, 1, 3)).reshape(MLA_KV_RANK, 2 * MLA_V_W)
    G["mla_w_in"] = dw_mla_in
    dq4 = dwq2.reshape(MLA_Q_RANK, 2, 2, Q_GROUP, MLA_QKP)
    dqp = dq4[:, :, 0].reshape(MLA_Q_RANK, MLA_HEADS, MLA_QKP)
    dqs = dq4[:, :, 1].reshape(MLA_Q_RANK, MLA_HEADS, MLA_QKP)
    G["mla_w_q_up"] = jnp.concatenate([dqp[:, :, :128], dqp[:, :, 128:160] + dqs[:, :, 160:192],
                                       dqp[:, :, 160:192] + dqs[:, :, 128:160]], axis=2).reshape(MLA_Q_RANK, MLA_HEADS * MLA_QK)
    G["mla_w_out"] = dw_mla_out
    G["gdn_w_out"] = dw_gdn_out

    def shards_to_send(names):
        return jnp.concatenate([_to_shards(G[n], n in _COL_SHARDED).reshape(N_DEV, -1, 128).astype(bf16) for n in names], axis=1)

    dhn0, parts_early = _matmul(dproj, w_in_p, tb=True, name="b_gdn_in_x", side=(shards_to_send(_SCATTER_EARLY), False))
    dw_in_p = _matmul(hn0, dproj, ta=True, name="b_gdn_in_w")
    (dh0,), (dpre0,) = _rowwise_vjp(_st_prenorm, [(h0, None, 0)], [], [pre0], [(dhn0, None, 0)], extra=[dh0_a], name="b_prenorm0")

    grad_x = dh0[ROW0:n_real][None]
    G["meta_tokens"] = dh0[PAD_FRONT:ROW0]
    G["gdn_w_in"] = jnp.concatenate([dw_in_p[:, :s1 + 16], dw_in_p[:, s1 + 128:s1 + 144]], axis=1)
    G["gdn_conv_w"] = jnp.concatenate([dcw_q, dcw_k, dcw_v], axis=1)
    G["pre_norm"] = jnp.concatenate([dpre0, dpre1], axis=0)
    G["post_norm"] = jnp.concatenate([dpost0, dpost1], axis=0)
    G["gdn_a_log"] = dalog_p[:, :GDN_V_HEADS]
    G["gdn_dt_bias"] = ddtb_p[:, :GDN_V_HEADS]
    G["gdn_out_norm"] = dout_norm
    G["kv_norm"] = dkvn.reshape(-1)
    G["kv_latent_norm"] = dkvln.reshape(-1)
    G["mla_q_latent_norm"] = dqln

    parts_late = _exchange(shards_to_send(_SCATTER_LATE), gather=False, name="scatter_grads")
    G["loss"] = loss_here.reshape(1, 1)
    for d in (W, M, V):
        d["loss"] = jnp.zeros((1, 1), f32)
    replicated = _REPLICATED + ("loss",)
    parts_r = _exchange(_pack([G[n] for n in replicated], 8), gather=True, name="gather_small_grads")
    outs = {}
    for names, parts, tag in ((_SCATTER_EARLY, parts_early, "early"), (_SCATTER_LATE, parts_late, "late"),
                              (replicated, parts_r, "replicated")):
        w_p, m_p, v_p = (_pack([d[n] for n in names], 8) for d in (W, M, V))
        res = _reduce_adamw(parts, w_p, m_p, v_p, name="adamw_" + tag)
        for kind, buf in zip(("grad", "delta", "new_m", "new_v"), res):
            for n, a in zip(names, _unpack(buf, [W[n].shape for n in names])):
                outs[kind, n] = a
    loss = outs["grad", "loss"].reshape(())
    return (loss, grad_x, *[outs[k, n] for k in ("grad", "delta", "new_m", "new_v") for n in order])
```

```python
import functools
import math

import jax
import jax.numpy as jnp
from jax import lax
from jax.experimental import pallas as pl
from jax.experimental.pallas import tpu as pltpu

f32, bf16 = jnp.float32, jnp.bfloat16
MESH_ID = pl.DeviceIdType.MESH

N_DEV = 8
D_MODEL = 1024
N_META = 16
NORM_EPS = 1e-6
PAD_FRONT = 48
ROW0 = PAD_FRONT + N_META
GDN_QK_HEADS, GDN_V_HEADS, GDN_D = 8, 16, 128
GDN_CHUNK = 64
GDN_QK_W, GDN_V_W = GDN_QK_HEADS * GDN_D, GDN_V_HEADS * GDN_D
GDN_CONV_W = 2 * GDN_QK_W + GDN_V_W
GDN_IN_W = GDN_CONV_W + GDN_V_W + 2 * GDN_V_HEADS
GDN_IN_WP = GDN_CONV_W + GDN_V_W + 2 * 128
MLA_HEADS, MLA_NOPE, MLA_ROPE, MLA_V = 16, 128, 64, 128
MLA_Q_RANK, MLA_KV_RANK = 256, 128
MLA_QK = MLA_NOPE + MLA_ROPE
MLA_QKP = 256
MLA_V_W = MLA_HEADS * MLA_V
ROPE_THETA = 10000.0
NEG = -1e30
ROW_TILE = 256
ATT_KEY_TILES = (384, 256)

ADAM_LR, ADAM_B1, ADAM_B2, ADAM_EPS, ADAM_WD, ADAM_STEP = 0.001, 0.9, 0.999, 1e-08, 0.01, 10

NN = ((1,), (0,))
NT = ((1,), (1,))
TN = ((0,), (0,))


def _pick(dim, prefs):
    for p in prefs:
        if dim % p == 0:
            return p
    return dim


def _dlo(a, b, dims):
    return lax.dot_general(a.astype(bf16), b.astype(bf16), (dims, ((), ())), preferred_element_type=f32)


def _dsel(a, b, dims, selector):
    x = b if selector == 0 else a
    hi = x.astype(bf16)
    rest = x - hi.astype(f32)
    mid = rest.astype(bf16)
    low = (rest - mid.astype(f32)).astype(bf16)
    sel = (a if selector == 0 else b).astype(bf16)
    d = (lambda p: lax.dot_general(sel, p, (dims, ((), ())), preferred_element_type=f32)) if selector == 0 else \
        (lambda p: lax.dot_general(p, sel, (dims, ((), ())), preferred_element_type=f32))
    return d(hi) + (d(mid) + d(low))


def _matmul(a, b, *, ta=False, tb=False, out_dtype=f32, name, side=None):
    assert not (ta and tb)
    if ta:
        kdim, m = a.shape
    else:
        m, kdim = a.shape
    n = b.shape[0] if tb else b.shape[1]
    assert (b.shape[1] if tb else b.shape[0]) == kdim
    tm = _pick(m, (1024, 768, 512, 384, 256, 128))
    tn = _pick(n, (1024, 768, 640, 512, 256, 128))
    tk = _pick(kdim, (1024, 768, 640, 512, 256, 128))
    nk = kdim // tk
    dims = TN if ta else (NT if tb else NN)

    grid = (m // tm, n // tn, nk)

    def product(a_ref, b_ref, o_ref, acc_ref):
        k = pl.program_id(2)

        @pl.when(k == 0)
        def _():
            acc_ref[...] = jnp.zeros_like(acc_ref)

        acc_ref[...] += lax.dot_general(a_ref[...].astype(bf16), b_ref[...].astype(bf16), (dims, ((), ())),
                                        preferred_element_type=f32)

        @pl.when(k == nk - 1)
        def _():
            o_ref[...] = acc_ref[...].astype(o_ref.dtype)

    a_spec = pl.BlockSpec((tk, tm), lambda i, j, k: (k, i)) if ta else pl.BlockSpec((tm, tk), lambda i, j, k: (i, k))
    b_spec = pl.BlockSpec((tn, tk), lambda i, j, k: (j, k)) if tb else pl.BlockSpec((tk, tn), lambda i, j, k: (k, j))
    o_spec = pl.BlockSpec((tm, tn), lambda i, j, k: (i, j))
    o_shape = jax.ShapeDtypeStruct((m, n), out_dtype)
    if side is None:
        def body(a_ref, b_ref, o_ref, acc_ref):
            product(a_ref, b_ref, o_ref, acc_ref)

        return pl.pallas_call(
            body, name=name, grid=grid, in_specs=[a_spec, b_spec], out_specs=o_spec, out_shape=o_shape,
            scratch_shapes=[pltpu.VMEM((tm, tn), f32)],
            compiler_params=pltpu.CompilerParams(dimension_semantics=("parallel", "parallel", "arbitrary")),
        )(a, b)

    x, gather = side
    blk = x.shape if gather else x.shape[1:]

    def body_with_exchange(a_ref, b_ref, x_ref, o_ref, xo_ref, acc_ref, send_sems, recv_sems, local_sem):
        step = (pl.program_id(0) * grid[1] + pl.program_id(1)) * grid[2] + pl.program_id(2)
        copies = _exchange_copies(x_ref, xo_ref, send_sems, recv_sems, local_sem, gather)

        @pl.when(step == 0)
        def _():
            _exchange_start(copies)

        product(a_ref, b_ref, o_ref, acc_ref)

        @pl.when(step == grid[0] * grid[1] * grid[2] - 1)
        def _():
            _exchange_wait(copies)

    hbm = pl.BlockSpec(memory_space=pltpu.HBM)
    return pl.pallas_call(
        body_with_exchange, name=name, grid=grid, in_specs=[a_spec, b_spec, hbm], out_specs=[o_spec, hbm],
        out_shape=[o_shape, jax.ShapeDtypeStruct((N_DEV,) + tuple(blk), x.dtype)],
        scratch_shapes=[pltpu.VMEM((tm, tn), f32)] + list(_EXCHANGE_SCRATCH),
        compiler_params=pltpu.CompilerParams(dimension_semantics=("arbitrary", "arbitrary", "arbitrary")),
    )(a, b, x)


def _row_spec(item, tr):
    a, bc, off = item
    if bc is None:
        return pl.BlockSpec((tr, a.shape[1]), lambda i, j: (i, 0))
    return pl.BlockSpec((tr, bc), lambda i, j, off=off: (i, j + off))


def _param_spec(p):
    return pl.BlockSpec(p.shape, lambda i, j: (0, 0))


def _row_tile(lp, items):
    widest = max(a.shape[1] if bc is None else bc for (a, bc, _) in items)
    return ROW_TILE if widest >= 1024 else _pick(lp, (768, 512, 256))


def _head_cols(tiles, h, heads):
    return [x[:, h * (x.shape[1] // heads):(h + 1) * (x.shape[1] // heads)] for x in tiles]


def _rowwise(fn, rows, nodiff, params, outs, *, ncol=1, heads=1, name):
    lp = rows[0][0].shape[0]
    tr = _row_tile(lp, rows)
    nr, nd = len(rows), len(nodiff)

    def body(*refs):
        rv = [r[...].astype(f32) for r in refs[:nr]]
        nv = [r[...] for r in refs[nr:nr + nd]]
        pv = [r[...] for r in refs[nr + nd:nr + nd + len(params)]]
        per_head = [fn(_head_cols(rv, h, heads), nv, pv) for h in range(heads)]
        res = [jnp.concatenate(list(vals), axis=1) if heads > 1 else vals[0] for vals in zip(*per_head)]
        for ref, val in zip(refs[nr + nd + len(params):], res):
            ref[...] = val.astype(ref.dtype)

    out_specs = [pl.BlockSpec((tr, c if bc is None else bc), (lambda i, j: (i, 0)) if bc is None else (lambda i, j: (i, j)))
                 for (c, _, bc) in outs]
    return pl.pallas_call(
        body, name=name, grid=(lp // tr, ncol),
        in_specs=[_row_spec(it, tr) for it in rows + nodiff] + [_param_spec(p) for p in params],
        out_specs=out_specs,
        out_shape=[jax.ShapeDtypeStruct((lp, c), dt) for (c, dt, _) in outs],
    )(*[it[0] for it in rows + nodiff], *params)


def _rowwise_vjp(fn, rows, nodiff, params, cts, *, ncol=1, heads=1, name, ct_pre=None, extra=None, grad_dtypes=None):
    lp = rows[0][0].shape[0]
    tr = _row_tile(lp, rows)
    nr, nd, npar, nct = len(rows), len(nodiff), len(params), len(cts)
    extra = extra or [None] * nr
    grad_dtypes = grad_dtypes or [f32] * nr
    ex_items = [(e, rows[k][1], 0) for k, e in enumerate(extra) if e is not None]
    ex_pos = [k for k, e in enumerate(extra) if e is not None]
    for (a, bc, _) in rows:
        assert bc is not None or ncol == 1

    def body(*refs):
        pos = 0
        rv = [r[...].astype(f32) for r in refs[pos:pos + nr]]; pos += nr
        nv = [r[...] for r in refs[pos:pos + nd]]; pos += nd
        pv = [r[...] for r in refs[pos:pos + npar]]; pos += npar
        cv = [r[...].astype(f32) for r in refs[pos:pos + nct]]; pos += nct
        ev = [r[...].astype(f32) for r in refs[pos:pos + len(ex_items)]]; pos += len(ex_items)
        drow_refs = refs[pos:pos + nr]; pos += nr
        dpar_refs = refs[pos:pos + npar]
        ctv = ct_pre(cv) if ct_pre is not None else cv
        drow_h, dpar = [], None
        for h in range(heads):
            outs, vjp_fn = jax.vjp(lambda rr, pp: fn(rr, nv, pp), _head_cols(rv, h, heads), pv)
            dr, dp = vjp_fn([c.astype(o.dtype) for c, o in zip(_head_cols(ctv, h, heads), outs)])
            drow_h.append(dr)
            dpar = dp if dpar is None else [a + b for a, b in zip(dpar, dp)]
        drow = [jnp.concatenate(list(vals), axis=1) if heads > 1 else vals[0] for vals in zip(*drow_h)]
        for k, e in zip(ex_pos, ev):
            drow[k] = drow[k] + e
        for ref, val in zip(drow_refs, drow):
            ref[...] = val.astype(ref.dtype)
        first = jnp.logical_and(pl.program_id(0) == 0, pl.program_id(1) == 0)

        @pl.when(first)
        def _():
            for ref, val in zip(dpar_refs, dpar):
                ref[...] = val

        @pl.when(jnp.logical_not(first))
        def _():
            for ref, val in zip(dpar_refs, dpar):
                ref[...] += val

    drow_shapes, drow_specs = [], []
    for (a, bc, _), dt in zip(rows, grad_dtypes):
        if bc is None:
            drow_shapes.append(jax.ShapeDtypeStruct((lp, a.shape[1]), dt))
            drow_specs.append(pl.BlockSpec((tr, a.shape[1]), lambda i, j: (i, 0)))
        else:
            drow_shapes.append(jax.ShapeDtypeStruct((lp, ncol * bc), dt))
            drow_specs.append(pl.BlockSpec((tr, bc), lambda i, j: (i, j)))
    res = pl.pallas_call(
        body, name=name, grid=(lp // tr, ncol),
        in_specs=[_row_spec(it, tr) for it in rows + nodiff] + [_param_spec(p) for p in params]
        + [_row_spec(it, tr) for it in cts + ex_items],
        out_specs=drow_specs + [_param_spec(p) for p in params],
        out_shape=drow_shapes + [jax.ShapeDtypeStruct(p.shape, f32) for p in params],
        compiler_params=pltpu.CompilerParams(dimension_semantics=("arbitrary", "arbitrary")),
    )(*[it[0] for it in rows + nodiff], *params, *[it[0] for it in cts + ex_items])
    return res[:nr], res[nr:]


def _rms(x, g):
    return x * lax.rsqrt(jnp.mean(x * x, axis=-1, keepdims=True) + NORM_EPS) * g


def _l2n(x):
    return x * lax.rsqrt(jnp.sum(x * x, axis=-1, keepdims=True) + NORM_EPS)


def _sigmoid(x):
    return 1.0 / (1.0 + jnp.exp(-x))


def _silu(x):
    return x * _sigmoid(x)


def _softplus(x):
    return jnp.maximum(x, 0.0) + jnp.log(1.0 + jnp.exp(-jnp.abs(x)))


def _row_ids(shape):
    return pl.program_id(0) * shape[0] + lax.broadcasted_iota(jnp.int32, shape, 0)


def _st_prenorm(r, n, p):
    return [_rms(r[0], p[0])]


def _st_gdn_q(r, n, p):
    return [_l2n(_silu(r[0])) * (GDN_D ** -0.5)]


def _st_gdn_k(r, n, p):
    return [_l2n(_silu(r[0]))]


def _st_gdn_v(r, n, p):
    return [_silu(r[0])]


def _st_gdn_gate(r, n, p):
    real = _row_ids(r[0].shape) >= PAD_FRONT
    beta = jnp.where(real, _sigmoid(r[0]), 0.0)
    g = jnp.where(real, -jnp.exp(p[0]) * _softplus(r[1] + p[1]), 0.0)
    return [beta, g]


def _st_gdn_out(r, n, p):
    return [_rms(r[0], p[0]) * _silu(r[1])]


def _st_mid(r, n, p):
    h1 = r[0] + _rms(r[1], p[0])
    return [h1, _rms(h1, p[1]), _rms(h1, p[2])]


def _st_latent(r, n, p):
    ckr, cq = r
    c_kv = _rms(ckr[:, :MLA_KV_RANK], p[0])
    k_rope = ckr[:, 128:256] * n[0] + ckr[:, 384:512] * n[1]
    return [c_kv, k_rope, _rms(cq, p[1])]


Q_GROUP = 8


def _st_q_rope(r, n, p):
    half = Q_GROUP * MLA_QKP
    out = []
    for h in range(Q_GROUP):
        cols = slice(h * MLA_QKP, (h + 1) * MLA_QKP)
        out.append((r[0][:, :half][:, cols] * n[0] + r[0][:, half:][:, cols] * n[1]) * (MLA_QK ** -0.5))
    return [jnp.concatenate(out, axis=1)]


def _st_q_rope_t(r, n, p):
    plain, swapped = [], []
    for h in range(Q_GROUP):
        ct = r[0][:, h * MLA_QKP:(h + 1) * MLA_QKP] * (MLA_QK ** -0.5)
        plain.append(ct * n[0])
        swapped.append(ct * n[1])
    return [jnp.concatenate(plain + swapped, axis=1)]


def _make_st_loss(n_tokens):
    def st(r, n, p):
        h2 = r[0] + _rms(r[1], p[0])
        rows = _row_ids((r[0].shape[0], 1))
        real = jnp.logical_and(rows >= ROW0, rows < ROW0 + n_tokens)
        err = h2 - n[0]
        return [jnp.where(real, 0.5 * jnp.mean(err * err, axis=-1, keepdims=True), 0.0)]
    return st


CONV_BC = 1024
HALO = 16


def _conv_fwd(x, w, *, col_blocks, name, tr=ROW_TILE):
    lp = x.shape[0]

    def body(x_ref, xp_ref, w_ref, o_ref):
        i = pl.program_id(0)
        xv = x_ref[...].astype(f32)
        prev = jnp.where(i > 0, xp_ref[...].astype(f32), 0.0)
        xc = jnp.concatenate([prev, xv], axis=0)
        wv = w_ref[...]
        acc = wv[3:4, :] * xv
        for j in range(3):
            acc = acc + wv[j:j + 1, :] * pltpu.roll(xc, 3 - j, 0)[HALO:, :]
        o_ref[...] = acc.astype(o_ref.dtype)

    return pl.pallas_call(
        body, name=name, grid=(lp // tr, col_blocks),
        in_specs=[pl.BlockSpec((tr, CONV_BC), lambda i, j: (i, j)),
                  pl.BlockSpec((HALO, CONV_BC), lambda i, j: (jnp.maximum(i * (tr // HALO) - 1, 0), j)),
                  pl.BlockSpec((4, CONV_BC), lambda i, j: (0, j))],
        out_specs=pl.BlockSpec((tr, CONV_BC), lambda i, j: (i, j)),
        out_shape=jax.ShapeDtypeStruct((lp, col_blocks * CONV_BC), bf16),
    )(x, x, w)


def _conv_bwd(dc, x, w, *, x_off, w_off, name, tr=ROW_TILE):
    lp, width = dc.shape
    ncb, nrow = width // CONV_BC, lp // tr

    def body(dc_ref, dcn_ref, x_ref, xp_ref, w_ref, dx_ref, dw_ref):
        i = pl.program_id(1)
        nxt = jnp.where(i < nrow - 1, dcn_ref[...], 0.0)
        dcv = dc_ref[...]
        dcc = jnp.concatenate([dcv, nxt], axis=0)
        xv = x_ref[...].astype(f32)
        prev = jnp.where(i > 0, xp_ref[...].astype(f32), 0.0)
        xc = jnp.concatenate([prev, xv], axis=0)
        wv = w_ref[...]
        dx = wv[3:4, :] * dcv
        dws = [None] * 4
        dws[3] = jnp.sum(dcv * xv, axis=0, keepdims=True)
        for j in range(3):
            dx = dx + wv[j:j + 1, :] * pltpu.roll(dcc, tr + 8 - (3 - j), 0)[:tr, :]
            dws[j] = jnp.sum(dcv * pltpu.roll(xc, 3 - j, 0)[HALO:, :], axis=0, keepdims=True)
        dx_ref[...] = dx.astype(dx_ref.dtype)

        @pl.when(i == 0)
        def _():
            for j in range(4):
                dw_ref[j:j + 1, :] = dws[j]

        @pl.when(i > 0)
        def _():
            for j in range(4):
                dw_ref[j:j + 1, :] += dws[j]

    last8 = lp // 8 - 1
    return pl.pallas_call(
        body, name=name, grid=(ncb, nrow),
        in_specs=[pl.BlockSpec((tr, CONV_BC), lambda j, i: (i, j)),
                  pl.BlockSpec((8, CONV_BC), lambda j, i: (jnp.minimum((i + 1) * (tr // 8), last8), j)),
                  pl.BlockSpec((tr, CONV_BC), lambda j, i: (i, j + x_off)),
                  pl.BlockSpec((HALO, CONV_BC), lambda j, i: (jnp.maximum(i * (tr // HALO) - 1, 0), j + x_off)),
                  pl.BlockSpec((4, CONV_BC), lambda j, i: (0, j + w_off))],
        out_specs=[pl.BlockSpec((tr, CONV_BC), lambda j, i: (i, j)),
                   pl.BlockSpec((4, CONV_BC), lambda j, i: (0, j))],
        out_shape=[jax.ShapeDtypeStruct((lp, width), bf16), jax.ShapeDtypeStruct((4, width), f32)],
        compiler_params=pltpu.CompilerParams(dimension_semantics=("arbitrary", "arbitrary")),
    )(dc, dc, x, x, w)


GDN_PACK = 4
GDN_FWD_INTERLEAVE, GDN_BWD_INTERLEAVE = 4, 4


MXU_TILE = 256


def _bd(x, cb, g=GDN_PACK):
    r = x.shape[0]
    tall = jnp.concatenate([x] * g, axis=0)
    rows = lax.broadcasted_iota(jnp.int32, tall.shape, 0) // r
    cols = lax.broadcasted_iota(jnp.int32, tall.shape, 1) // cb
    return jnp.where(rows == cols, tall, jnp.zeros_like(tall))


def _diag(full, r, cb, g=GDN_PACK):
    cols = lax.broadcasted_iota(jnp.int32, (r, full.shape[1]), 1) // cb
    out = jnp.where(cols == 0, full[0:r, :], 0.0)
    for a in range(1, g):
        out = out + jnp.where(cols == a, full[a * r:(a + 1) * r, :], 0.0)
    return out


def _stack(x, cb):
    return jnp.concatenate([x[:, a * cb:(a + 1) * cb] for a in range(GDN_PACK)], axis=0)


def _lane_halves(a):
    return a[:, :a.shape[1] // 2], a[:, a.shape[1] // 2:]


def _make_packed(dot):
    G = GDN_PACK

    def two_by_two(k, cb):
        return k * G > MXU_TILE and cb * G > MXU_TILE

    def times_bd(x, y, dims):
        k, cb = y.shape[0], y.shape[1] // G
        if two_by_two(k, cb):
            return jnp.concatenate([dot(xh, _bd(yh, cb, G // 2), dims)
                                    for xh, yh in zip(_lane_halves(x), _lane_halves(y))], axis=1)
        return dot(x, _bd(y, cb), dims)

    def tn_diag(x, y):
        k, cb = x.shape[1] // G, y.shape[1] // G
        if two_by_two(k, cb):
            return jnp.concatenate([_diag(dot(xh, yh, TN), k, cb, G // 2)
                                    for xh, yh in zip(_lane_halves(x), _lane_halves(y))], axis=1)
        return _diag(dot(x, y, TN), k, cb)

    @jax.custom_vjp
    def pmm(x, y):
        return times_bd(x, y, NN)

    @jax.custom_vjp
    def pnt(x, y):
        k = x.shape[1] // G
        return _diag(dot(_stack(x, k), _stack(y, k), NT), x.shape[0], y.shape[0])

    @jax.custom_vjp
    def ptn(x, y):
        return tn_diag(x, y)

    def pmm_bwd(res, ct):
        x, y = res
        return times_bd(ct, y, NT), tn_diag(x, ct)

    pmm.defvjp(lambda x, y: (pmm(x, y), (x, y)), pmm_bwd)
    pnt.defvjp(lambda x, y: (pnt(x, y), (x, y)), lambda res, ct: (pmm(ct, res[1]), ptn(ct, res[0])))
    ptn.defvjp(lambda x, y: (ptn(x, y), (x, y)), lambda res, ct: (pnt(res[1], ct), pmm(res[0], ct)))
    return pmm, pnt, ptn


_pmm, _pnt, _ptn = _make_packed(_dlo)


@jax.custom_vjp
def _inv_packed(ms):
    c = ms[0].shape[0]
    ii = lax.broadcasted_iota(jnp.int32, ms[0].shape, 0)
    jj = lax.broadcasted_iota(jnp.int32, ms[0].shape, 1) % c
    ts = [jnp.where(ii == jj, 1.0, 0.0) - m for m in ms]
    ps = [(-m).astype(bf16) for m in ms]
    for _ in range(int(math.log2(c)) - 1):
        ps = [_dlo(p, _bd(p, c), NN).astype(bf16) for p in ps]
        ts = [t + _dlo(t, _bd(p, c), NN) for t, p in zip(ts, ps)]
    return tuple(ts)


def _inv_packed_fwd(ms):
    ts = _inv_packed(ms)
    return ts, ts


def _inv_packed_bwd(ts, cts):
    c = ts[0].shape[0]
    ys = [_diag(_dlo(t, ct, TN), c, c) for t, ct in zip(ts, cts)]
    return (tuple(-_dlo(y, _bd(t.astype(bf16), c), NT) for y, t in zip(ys, ts)),)


_inv_packed.defvjp(_inv_packed_fwd, _inv_packed_bwd)


def _gdn_prep(q2, k2, v4, bcols, gcols, grows):
    c, d = v4.shape[0], GDN_D
    q4 = jnp.concatenate([q2[:, :d], q2[:, :d], q2[:, d:], q2[:, d:]], axis=1)
    k4 = jnp.concatenate([k2[:, :d], k2[:, :d], k2[:, d:], k2[:, d:]], axis=1)
    beta4 = jnp.concatenate([jnp.broadcast_to(b, (c, d)) for b in bcols], axis=1)
    gc4 = jnp.concatenate([jnp.broadcast_to(g, (c, d)) for g in gcols], axis=1)
    low = lax.broadcasted_iota(jnp.int32, (c, 128), 1) < c
    gi = jnp.concatenate([jnp.where(low, gcols[0], gcols[1]), jnp.where(low, gcols[2], gcols[3])], axis=1)
    gj = jnp.concatenate([jnp.where(low, grows[0], grows[1]), jnp.where(low, grows[2], grows[3])], axis=1)
    ii = lax.broadcasted_iota(jnp.int32, gi.shape, 0)
    jj = lax.broadcasted_iota(jnp.int32, gi.shape, 1) % c
    dec = jnp.exp(jnp.where(ii >= jj, gi - gj, NEG))
    rid = lax.broadcasted_iota(jnp.int32, gc4.shape, 0)
    glast = jnp.sum(jnp.where(rid == c - 1, gc4, 0.0), axis=0, keepdims=True)
    eg = jnp.exp(gc4)
    kb = k4 * beta4
    return dict(q=q4, k=k4, kb=kb, vb=v4 * beta4, kbe=kb * eg, qe=q4 * eg, dec=dec, dec_strict=jnp.where(ii > jj, dec, 0.0),
                sdecay=jnp.exp(glast), kd=k4 * jnp.exp(glast - gc4))


@jax.custom_vjp
def _inv_packed_known(ms, ts):
    return ts


_inv_packed_known.defvjp(lambda ms, ts: (ts, ts),
                         lambda ts, cts: (_inv_packed_bwd(ts, cts)[0], tuple(jnp.zeros_like(t) for t in ts)))


def _gdn_groups(groups, known_inverses=None, with_inverses=False):
    c = groups[0][3].shape[0]
    ss = [g[0] for g in groups]
    e = [_gdn_prep(*g[1:]) for g in groups]
    ms = tuple(_pnt(x["kb"], x["k"]) * x["dec_strict"] for x in e)
    ts = _inv_packed(ms) if known_inverses is None else _inv_packed_known(ms, tuple(known_inverses))
    us = [_pmm(t, x["vb"]) for t, x in zip(ts, e)]
    ws = [_pmm(t, x["kbe"]) for t, x in zip(ts, e)]
    attns = [_pnt(x["q"], x["k"]) * x["dec"] for x in e]
    ws_qs = [_pmm(jnp.concatenate([w, x["qe"]], axis=0), s) for w, x, s in zip(ws, e, ss)]
    v_news = [u - y[:c] for u, y in zip(us, ws_qs)]
    os = [y[c:] + _pmm(a, vn) for y, a, vn in zip(ws_qs, attns, v_news)]
    s_news = [s * x["sdecay"] + _ptn(x["kd"], vn) for s, x, vn in zip(ss, e, v_news)]
    if with_inverses:
        return list(zip(os, s_news)), list(ts)
    return list(zip(os, s_news))


def _lane_pick(x, h):
    lane = lax.broadcasted_iota(jnp.int32, x.shape, 1)
    return jnp.sum(jnp.where(lane == h, x, 0.0), axis=1, keepdims=True)


def _cum_log_decay(g):
    c = g.shape[0]
    lower = (lax.broadcasted_iota(jnp.int32, (c, c), 0) >= lax.broadcasted_iota(jnp.int32, (c, c), 1)).astype(f32)
    upper2 = (lax.broadcasted_iota(jnp.int32, (c, 128), 0) <= lax.broadcasted_iota(jnp.int32, (c, 128), 1) % c).astype(f32)
    return _dsel(lower, g, NN, 0), _dsel(g, upper2, TN, 1)


def _group_operands(gi, s_ref, q_ref, k_ref, v_ref, bv, gcv, gct_s):
    heads = [gi * GDN_PACK + u for u in range(GDN_PACK)]
    qk_off = pl.multiple_of(gi * 2 * GDN_D, 2 * GDN_D)
    v_off = pl.multiple_of(gi * GDN_PACK * GDN_D, GDN_PACK * GDN_D)
    return (s_ref[gi], q_ref[:, pl.ds(qk_off, 2 * GDN_D)].astype(f32), k_ref[:, pl.ds(qk_off, 2 * GDN_D)].astype(f32),
            v_ref[:, pl.ds(v_off, GDN_PACK * GDN_D)].astype(f32),
            [_lane_pick(bv, h) for h in heads], [_lane_pick(gcv, h) for h in heads],
            [gct_s[pl.ds(h, 1), :] for h in heads]), heads, qk_off, v_off


def _gdn_fwd(qn, kn, v, beta, g, *, n_real_chunks, name):
    lp = qn.shape[0]
    nchunk = lp // GDN_CHUNK
    C, D = GDN_CHUNK, GDN_D
    NG, SW = GDN_V_HEADS // GDN_PACK, GDN_PACK * GDN_D

    def body(q_ref, k_ref, v_ref, b_ref, g_ref, o_ref, st_ref, inv_ref, s_s, gc_s, gct_s):
        ci = pl.program_id(0)

        @pl.when(ci == 0)
        def _():
            s_s[...] = jnp.zeros_like(s_s)

        @pl.when(ci >= n_real_chunks)
        def _():
            o_ref[...] = jnp.zeros_like(o_ref)
            st_ref[...] = jnp.zeros_like(st_ref)
            inv_ref[...] = jnp.zeros_like(inv_ref)

        @pl.when(ci < n_real_chunks)
        def _():
            gc, gct = _cum_log_decay(g_ref[...])
            gc_s[...] = gc
            gct_s[...] = gct

            def some_groups(it, carry):
                ids = [it * GDN_FWD_INTERLEAVE + u for u in range(GDN_FWD_INTERLEAVE)]
                ops = [_group_operands(gi, s_s, q_ref, k_ref, v_ref, b_ref[...], gc_s[...], gct_s) for gi in ids]
                res, inverses = _gdn_groups([op[0] for op in ops], with_inverses=True)
                for gi, op, (o, s_new), t in zip(ids, ops, res, inverses):
                    st_ref[gi] = op[0][0]
                    inv_ref[gi] = t
                    s_s[gi] = s_new
                    o_ref[:, pl.ds(op[3], SW)] = o
                return carry

            lax.fori_loop(0, NG // GDN_FWD_INTERLEAVE, some_groups, 0)

    return pl.pallas_call(
        body, name=name, grid=(nchunk,),
        in_specs=[pl.BlockSpec((C, GDN_QK_W), lambda c: (c, 0)), pl.BlockSpec((C, GDN_QK_W), lambda c: (c, 0)),
                  pl.BlockSpec((C, GDN_V_W), lambda c: (c, 0)), pl.BlockSpec((C, 128), lambda c: (c, 0)),
                  pl.BlockSpec((C, 128), lambda c: (c, 0))],
        out_specs=[pl.BlockSpec((C, GDN_V_W), lambda c: (c, 0)),
                   pl.BlockSpec((None, NG, D, SW), lambda c: (c, 0, 0, 0)),
                   pl.BlockSpec((None, NG, C, GDN_PACK * C), lambda c: (c, 0, 0, 0))],
        out_shape=[jax.ShapeDtypeStruct((lp, GDN_V_W), f32), jax.ShapeDtypeStruct((nchunk, NG, D, SW), f32),
                   jax.ShapeDtypeStruct((nchunk, NG, C, GDN_PACK * C), f32)],
        scratch_shapes=[pltpu.VMEM((NG, D, SW), f32), pltpu.VMEM((C, 128), f32), pltpu.VMEM((128, 128), f32)],
        compiler_params=pltpu.CompilerParams(dimension_semantics=("arbitrary",)),
    )(qn, kn, v, beta, g)


def _gdn_bwd(qn, kn, v, beta, g, states, inverses, do, *, n_real_chunks, name):
    lp = qn.shape[0]
    nchunk = lp // GDN_CHUNK
    C, D = GDN_CHUNK, GDN_D
    NG, SW = GDN_V_HEADS // GDN_PACK, GDN_PACK * GDN_D
    rev = lambda i: (nchunk - 1 - i, 0)

    def body(q_ref, k_ref, v_ref, b_ref, g_ref, st_ref, inv_ref, do_ref,
             dq_ref, dk_ref, dv_ref, db_ref, dg_ref, ds_s, gc_s, gct_s, dgc_s, dgct_s, dbeta_s):
        step = pl.program_id(0)
        ci = nchunk - 1 - step

        @pl.when(step == 0)
        def _():
            ds_s[...] = jnp.zeros_like(ds_s)

        @pl.when(ci >= n_real_chunks)
        def _():
            for r in (dq_ref, dk_ref, dv_ref, db_ref, dg_ref):
                r[...] = jnp.zeros_like(r)

        @pl.when(ci < n_real_chunks)
        def _():
            gc, gct = _cum_log_decay(g_ref[...])
            gc_s[...] = gc
            gct_s[...] = gct
            dgc_s[...] = jnp.zeros_like(dgc_s)
            dgct_s[...] = jnp.zeros_like(dgct_s)
            dbeta_s[...] = jnp.zeros_like(dbeta_s)

            def some_groups(it, carry):
                ids = [it * GDN_BWD_INTERLEAVE + u for u in range(GDN_BWD_INTERLEAVE)]
                ops = [_group_operands(gi, st_ref, q_ref, k_ref, v_ref, b_ref[...], gc_s[...], gct_s) for gi in ids]
                cts = [(do_ref[:, pl.ds(op[3], SW)], ds_s[gi]) for gi, op in zip(ids, ops)]
                known = [inv_ref[gi] for gi in ids]
                _, vjp_fn = jax.vjp(lambda gs: _gdn_groups(gs, known_inverses=known), [op[0] for op in ops])
                (grads,) = vjp_fn(cts)
                lane = lax.broadcasted_iota(jnp.int32, (C, 128), 1)
                dbeta_acc, dgc_acc = dbeta_s[...], dgc_s[...]
                for gi, (_, heads, qk_off, v_off), (dsp, dq2, dk2, dv4, dbcols, dgcols, dgrows) in zip(ids, ops, grads):
                    ds_s[gi] = dsp
                    dq_ref[:, pl.ds(qk_off, 2 * D)] = dq2
                    dk_ref[:, pl.ds(qk_off, 2 * D)] = dk2
                    dv_ref[:, pl.ds(v_off, SW)] = dv4
                    for h, dbcol, dgcol, dgrow in zip(heads, dbcols, dgcols, dgrows):
                        dbeta_acc = dbeta_acc + jnp.where(lane == h, dbcol, 0.0)
                        dgc_acc = dgc_acc + jnp.where(lane == h, dgcol, 0.0)
                        dgct_s[pl.ds(h, 1), :] = dgrow
                dbeta_s[...] = dbeta_acc
                dgc_s[...] = dgc_acc
                return carry

            lax.fori_loop(0, NG // GDN_BWD_INTERLEAVE, some_groups, 0)
            fold = (lax.broadcasted_iota(jnp.int32, (128, C), 0) % C == lax.broadcasted_iota(jnp.int32, (128, C), 1)).astype(f32)
            eye = (lax.broadcasted_iota(jnp.int32, (128, 128), 0) == lax.broadcasted_iota(jnp.int32, (128, 128), 1)).astype(f32)
            dgc = dgc_s[...] + _dsel(_dsel(dgct_s[...], fold, NN, 1), eye, TN, 1)
            upper = (lax.broadcasted_iota(jnp.int32, (C, C), 0) <= lax.broadcasted_iota(jnp.int32, (C, C), 1)).astype(f32)
            dg_ref[...] = _dsel(upper, dgc, NN, 0)
            db_ref[...] = dbeta_s[...]

    return pl.pallas_call(
        body, name=name, grid=(nchunk,),
        in_specs=[pl.BlockSpec((C, GDN_QK_W), rev), pl.BlockSpec((C, GDN_QK_W), rev), pl.BlockSpec((C, GDN_V_W), rev),
                  pl.BlockSpec((C, 128), rev), pl.BlockSpec((C, 128), rev),
                  pl.BlockSpec((None, NG, D, SW), lambda i: (nchunk - 1 - i, 0, 0, 0)),
                  pl.BlockSpec((None, NG, C, GDN_PACK * C), lambda i: (nchunk - 1 - i, 0, 0, 0)), pl.BlockSpec((C, GDN_V_W), rev)],
        out_specs=[pl.BlockSpec((C, GDN_QK_W), rev), pl.BlockSpec((C, GDN_QK_W), rev), pl.BlockSpec((C, GDN_V_W), rev),
                   pl.BlockSpec((C, 128), rev), pl.BlockSpec((C, 128), rev)],
        out_shape=[jax.ShapeDtypeStruct((lp, GDN_QK_W), f32)] * 2 + [jax.ShapeDtypeStruct((lp, GDN_V_W), f32)]
        + [jax.ShapeDtypeStruct((lp, 128), f32)] * 2,
        scratch_shapes=[pltpu.VMEM((NG, D, SW), f32), pltpu.VMEM((C, 128), f32), pltpu.VMEM((128, 128), f32),
                        pltpu.VMEM((C, 128), f32), pltpu.VMEM((128, 128), f32), pltpu.VMEM((C, 128), f32)],
        compiler_params=pltpu.CompilerParams(dimension_semantics=("arbitrary",)),
    )(qn, kn, v, beta, g, states, inverses, do)


def _attention_bias(tq, tk):
    kk = lax.broadcasted_iota(jnp.int32, (tk, tq), 0)
    qq = lax.broadcasted_iota(jnp.int32, (tk, tq), 1)
    pad = jnp.where(kk < PAD_FRONT, NEG, 0.0)
    diag = [jnp.where(qq >= kk + d * tk, 0.0, NEG) for d in range(tq // tk)]
    return jnp.stack([pad] + diag + [jnp.minimum(pad, diag[0])]).astype(f32)


def _att_tiles(lp):
    tq = _pick(lp, (768, 512, 256))
    return tq, _pick(tq, ATT_KEY_TILES)


def _attention_fwd(q, kvu, kr, proj2, *, name):
    lp = q.shape[0]
    H = MLA_HEADS
    tq, tk = _att_tiles(lp)
    r = tq // tk

    def body(q_ref, kn_ref, kr_ref, v_ref, z_ref, bias_ref, o_ref, lse_ref, og_ref, m_s, l_s, acc_s, sa_s, sb_s):
        qi = pl.program_id(1)
        m_s[...] = jnp.full_like(m_s, NEG)
        l_s[...] = jnp.zeros_like(l_s)
        acc_s[...] = jnp.zeros_like(acc_s)

        def scores(ki):
            k0 = pl.multiple_of(ki * tk, tk)
            k = jnp.concatenate([kn_ref[pl.ds(k0, tk), :], kr_ref[pl.ds(k0, tk), :]], axis=1)
            return lax.dot_general(k, q_ref[...], (NT, ((), ())), preferred_element_type=f32)

        def consume(st, ki, mask):
            k0 = pl.multiple_of(ki * tk, tk)
            if mask is not None:
                st = st + bias_ref[mask]
            m_prev = m_s[...]
            m_new = jnp.maximum(m_prev, jnp.max(st, axis=0, keepdims=True))
            alpha = jnp.exp(m_prev - m_new)
            p = jnp.exp(st - m_new)
            l_s[...] = alpha * l_s[...] + jnp.sum(p, axis=0, keepdims=True)
            acc_s[...] = alpha * acc_s[...] + lax.dot_general(v_ref[pl.ds(k0, tk), :], p.astype(bf16), (TN, ((), ())),
                                                              preferred_element_type=f32)
            m_s[...] = m_new

        n_full = qi * r

        def chain(blocks):
            bufs = (sa_s, sb_s)
            for j, (ki, masked) in enumerate(blocks):
                if j + 1 < len(blocks):
                    bufs[(j + 1) % 2][...] = scores(blocks[j + 1][0])
                consume(bufs[j % 2][...], ki, masked)

        diagonal = [(n_full + d, 1 + d) for d in range(r)]

        @pl.when(qi == 0)
        def _():
            sa_s[...] = scores(0)
            chain([(0, r + 1)] + diagonal[1:])

        @pl.when(qi > 0)
        def _():
            sb_s[...] = scores(0)
            sa_s[...] = scores(1)
            consume(sb_s[...], 0, 0)
            n_pairs = (n_full - 1) // 2

            def two(pi, carry):
                ki = 1 + 2 * pi
                sb_s[...] = scores(ki + 1)
                consume(sa_s[...], ki, None)
                sa_s[...] = scores(ki + 2)
                consume(sb_s[...], ki + 1, None)
                return carry

            lax.fori_loop(0, n_pairs, two, 0)
            nxt = 1 + 2 * n_pairs

            @pl.when(nxt < n_full)
            def _():
                chain([(nxt, None)] + diagonal)

            @pl.when(nxt == n_full)
            def _():
                chain(diagonal)
        o = jnp.transpose(acc_s[...] / l_s[...])
        o_ref[...] = o
        og_ref[...] = (o * _silu(z_ref[...].astype(f32))).astype(og_ref.dtype)
        lse_ref[...] = m_s[...] + jnp.log(l_s[...])

    return pl.pallas_call(
        body, name=name, grid=(H, lp // tq),
        in_specs=[pl.BlockSpec((tq, MLA_QKP), lambda h, qi: (qi, h)),
                  pl.BlockSpec((lp, 128), lambda h, qi: (0, h)),
                  pl.BlockSpec((lp, 128), lambda h, qi: (0, 0)),
                  pl.BlockSpec((lp, 128), lambda h, qi: (0, H + h)),
                  pl.BlockSpec((tq, 128), lambda h, qi: (qi, MLA_Q_RANK // 128 + h)),
                  pl.BlockSpec((r + 2, tk, tq), lambda h, qi: (0, 0, 0))],
        out_specs=[pl.BlockSpec((tq, 128), lambda h, qi: (qi, h)),
                   pl.BlockSpec((None, 1, tq), lambda h, qi: (h, 0, qi)),
                   pl.BlockSpec((tq, 128), lambda h, qi: (qi, h))],
        out_shape=[jax.ShapeDtypeStruct((lp, MLA_V_W), f32), jax.ShapeDtypeStruct((H, 1, lp), f32),
                   jax.ShapeDtypeStruct((lp, MLA_V_W), bf16)],
        scratch_shapes=[pltpu.VMEM((1, tq), f32), pltpu.VMEM((1, tq), f32), pltpu.VMEM((128, tq), f32),
                        pltpu.VMEM((tk, tq), f32), pltpu.VMEM((tk, tq), f32)],
        compiler_params=pltpu.CompilerParams(dimension_semantics=("arbitrary", "arbitrary")),
    )(q, kvu, kr, kvu, proj2, _attention_bias(tq, tk))


def _gate_bwd(o, proj2, dgated, *, name):
    lp = o.shape[0]
    H, w = MLA_HEADS, 2 * MLA_V
    tq = _pick(lp, (768, 512, 256))

    def body(o_ref, z_ref, g_ref, do_ref, dz_ref, dl_ref):
        ov, z, g = o_ref[...], z_ref[...].astype(f32), g_ref[...]
        s = _sigmoid(z)
        do = (g * (z * s)).astype(bf16)
        do_ref[...] = do
        dz_ref[...] = (g * ov * (s * (1.0 + z * (1.0 - s)))).astype(dz_ref.dtype)
        prod = ov * do.astype(f32)
        for u in range(2):
            dl_ref[u] = jnp.sum(jnp.transpose(prod[:, u * MLA_V:(u + 1) * MLA_V]), axis=0, keepdims=True)

    blk = pl.BlockSpec((tq, w), lambda j, qi: (qi, j))
    return pl.pallas_call(
        body, name=name, grid=(H // 2, lp // tq),
        in_specs=[blk, pl.BlockSpec((tq, w), lambda j, qi: (qi, j + MLA_Q_RANK // w)), blk],
        out_specs=[blk, blk, pl.BlockSpec((2, 1, tq), lambda j, qi: (j, 0, qi))],
        out_shape=[jax.ShapeDtypeStruct((lp, MLA_V_W), bf16), jax.ShapeDtypeStruct((lp, MLA_V_W), bf16),
                   jax.ShapeDtypeStruct((H, 1, lp), f32)],
    )(o, proj2, dgated)


def _attention_bwd(q, kvu, kr, lse, delta, do, *, name):
    lp = q.shape[0]
    tq, t = _att_tiles(lp)
    H, nb = MLA_HEADS, lp // t
    r, nq = tq // t, lp // tq

    def body(q_ref, kn_ref, kr_ref, v_ref, lse_ref, dl_ref, do_ref, bias_ref, dq_ref, dkn_ref, dv_ref, dkr_ref, dk_s, dv_s,
             sa_s, da_s, sb_s, db_s):
        ki = pl.program_id(1)
        k = jnp.concatenate([kn_ref[...], kr_ref[...]], axis=1)
        vv = v_ref[...]
        dk_s[...] = jnp.zeros_like(dk_s)
        dv_s[...] = jnp.zeros_like(dv_s)

        def products(qi, s_ref, d_ref):
            q0 = pl.multiple_of(qi * tq, tq)
            s_ref[...] = lax.dot_general(k, q_ref[pl.ds(q0, tq), :], (NT, ((), ())), preferred_element_type=f32)
            d_ref[...] = lax.dot_general(vv, do_ref[pl.ds(q0, tq), :], (NT, ((), ())), preferred_element_type=f32)

        def accumulate(s_ref, d_ref, qi, mask, first):
            q0 = pl.multiple_of(qi * tq, tq)
            qv = q_ref[pl.ds(q0, tq), :]
            dob = do_ref[pl.ds(q0, tq), :]
            st = s_ref[...]
            if mask is not None:
                st = st + bias_ref[mask]
            p = jnp.exp(st - lse_ref[:, pl.ds(q0, tq)])
            dv_s[...] += jnp.dot(p.astype(bf16), dob, preferred_element_type=f32)
            ds = (p * (d_ref[...] - dl_ref[:, pl.ds(q0, tq)])).astype(bf16)
            dk_s[...] += jnp.dot(ds, qv, preferred_element_type=f32)
            dq = lax.dot_general(ds, k, (TN, ((), ())), preferred_element_type=f32)
            if first:
                dq_ref[pl.ds(q0, tq), :] = dq
            else:
                dq_ref[pl.ds(q0, tq), :] += dq

        def sweep(qd, first_mask, other_mask, first):
            last = nq - 1
            products(qd, sa_s, da_s)
            products(jnp.minimum(qd + 1, last), sb_s, db_s)
            accumulate(sa_s, da_s, qd, first_mask, first)
            n = last - qd

            def two(pi, carry):
                i = qd + 1 + 2 * pi
                products(i + 1, sa_s, da_s)
                accumulate(sb_s, db_s, i, other_mask, first)
                products(jnp.minimum(i + 2, last), sb_s, db_s)
                accumulate(sa_s, da_s, i + 1, other_mask, first)
                return carry

            lax.fori_loop(0, n // 2, two, 0)

            @pl.when(n % 2 == 1)
            def _():
                accumulate(sb_s, db_s, last, other_mask, first)

        @pl.when(ki == 0)
        def _():
            sweep(0, r + 1, 0, True)

        @pl.when(ki > 0)
        def _():
            sweep(ki // r, 1 + ki % r, None, False)

        dkn_ref[...] = dk_s[:, :128].astype(dkn_ref.dtype)
        dkr_ref[...] = dk_s[:, 128:]
        dv_ref[...] = dv_s[...].astype(dv_ref.dtype)

    return pl.pallas_call(
        body, name=name, grid=(H, nb),
        in_specs=[pl.BlockSpec((lp, MLA_QKP), lambda h, ki: (0, h)),
                  pl.BlockSpec((t, 128), lambda h, ki: (ki, h)),
                  pl.BlockSpec((t, 128), lambda h, ki: (ki, 0)),
                  pl.BlockSpec((t, 128), lambda h, ki: (ki, H + h)),
                  pl.BlockSpec((None, 1, lp), lambda h, ki: (h, 0, 0)),
                  pl.BlockSpec((None, 1, lp), lambda h, ki: (h, 0, 0)),
                  pl.BlockSpec((lp, 128), lambda h, ki: (0, h)),
                  pl.BlockSpec((r + 2, t, tq), lambda h, ki: (0, 0, 0))],
        out_specs=[pl.BlockSpec((lp, MLA_QKP), lambda h, ki: (0, h)),
                   pl.BlockSpec((t, 128), lambda h, ki: (ki, h)),
                   pl.BlockSpec((t, 128), lambda h, ki: (ki, h)),
                   pl.BlockSpec((t, 128), lambda h, ki: (ki, h))],
        out_shape=[jax.ShapeDtypeStruct((lp, H * MLA_QKP), f32), jax.ShapeDtypeStruct((lp, MLA_V_W), bf16),
                   jax.ShapeDtypeStruct((lp, MLA_V_W), bf16), jax.ShapeDtypeStruct((lp, MLA_V_W), f32)],
        scratch_shapes=[pltpu.VMEM((t, MLA_QKP), f32), pltpu.VMEM((t, 128), f32)] + [pltpu.VMEM((t, tq), f32)] * 4,
        compiler_params=pltpu.CompilerParams(dimension_semantics=("arbitrary", "arbitrary")),
    )(q, kvu, kr, kvu, lse, delta, do, _attention_bias(tq, t))


def _q_proj_fwd(c_q, wq2, cos_q, sin_q, *, name, tm=ROW_TILE):
    lp, rank = c_q.shape
    gw = Q_GROUP * MLA_QKP
    ng = wq2.shape[1] // (2 * gw)

    def body(c_ref, w_ref, cos_ref, sin_ref, o_ref):
        qq = jnp.dot(c_ref[...], w_ref[...], preferred_element_type=f32)
        o_ref[...] = _st_q_rope([qq], [cos_ref[...], sin_ref[...]], [])[0].astype(o_ref.dtype)

    tab = pl.BlockSpec((tm, MLA_QKP), lambda j, i: (i, 0))
    return pl.pallas_call(
        body, name=name, grid=(ng, lp // tm),
        in_specs=[pl.BlockSpec((tm, rank), lambda j, i: (i, 0)), pl.BlockSpec((rank, 2 * gw), lambda j, i: (0, j)), tab, tab],
        out_specs=pl.BlockSpec((tm, gw), lambda j, i: (i, j)),
        out_shape=jax.ShapeDtypeStruct((lp, ng * gw), bf16),
    )(c_q, wq2, cos_q, sin_q)


def _q_proj_bwd(dq, c_q, wq2, cos_q, sin_q, *, name, tm=ROW_TILE):
    lp, rank = c_q.shape
    gw = Q_GROUP * MLA_QKP
    ng = wq2.shape[1] // (2 * gw)

    def body(dq_ref, c_ref, w_ref, cos_ref, sin_ref, dc_ref, dw_ref):
        dqq = _st_q_rope_t([dq_ref[...]], [cos_ref[...], sin_ref[...]], [])[0].astype(bf16)
        dc_ref[...] = lax.dot_general(dqq, w_ref[...], (NT, ((), ())), preferred_element_type=f32)
        dw = lax.dot_general(c_ref[...], dqq, (TN, ((), ())), preferred_element_type=f32)

        @pl.when(pl.program_id(1) == 0)
        def _():
            dw_ref[...] = dw

        @pl.when(pl.program_id(1) > 0)
        def _():
            dw_ref[...] += dw

    tab = pl.BlockSpec((tm, MLA_QKP), lambda j, i: (i, 0))
    return pl.pallas_call(
        body, name=name, grid=(ng, lp // tm),
        in_specs=[pl.BlockSpec((tm, gw), lambda j, i: (i, j)), pl.BlockSpec((tm, rank), lambda j, i: (i, 0)),
                  pl.BlockSpec((rank, 2 * gw), lambda j, i: (0, j)), tab, tab],
        out_specs=[pl.BlockSpec((tm, rank), lambda j, i: (i, j)), pl.BlockSpec((rank, 2 * gw), lambda j, i: (0, j))],
        out_shape=[jax.ShapeDtypeStruct((lp, ng * rank), f32), jax.ShapeDtypeStruct(wq2.shape, f32)],
        compiler_params=pltpu.CompilerParams(dimension_semantics=("arbitrary", "arbitrary")),
    )(dq, c_q, wq2, cos_q, sin_q)


def _exchange_copies(x_ref, o_ref, send_sems, recv_sems, local_sem, gather):
    mx, my, mc = lax.axis_index("x"), lax.axis_index("y"), lax.axis_index("c")
    me = 4 * mx + 2 * my + mc
    own = pltpu.make_async_copy(x_ref if gather else x_ref.at[me], o_ref.at[me], local_sem)
    sends, arrivals = [], []
    for k in range(1, N_DEV):
        px = 1 - mx if k & 4 else mx
        py = 1 - my if k & 2 else my
        pc = 1 - mc if k & 1 else mc
        peer = 4 * px + 2 * py + pc
        sends.append(pltpu.make_async_remote_copy(
            src_ref=x_ref if gather else x_ref.at[peer], dst_ref=o_ref.at[me],
            send_sem=send_sems.at[k - 1], recv_sem=recv_sems.at[k - 1],
            device_id=(px, py, pc), device_id_type=MESH_ID))
        arrivals.append(pltpu.make_async_remote_copy(
            src_ref=o_ref.at[peer], dst_ref=o_ref.at[peer],
            send_sem=send_sems.at[k - 1], recv_sem=recv_sems.at[k - 1],
            device_id=(mx, my, mc), device_id_type=MESH_ID))
    return own, sends, arrivals


def _exchange_start(copies):
    own, sends, _ = copies
    own.start()
    for cp in sends:
        cp.start()


def _exchange_wait(copies):
    own, sends, arrivals = copies
    for cp in arrivals:
        cp.wait_recv()
    for cp in sends:
        cp.wait_send()
    own.wait()


_EXCHANGE_SCRATCH = [pltpu.SemaphoreType.DMA((N_DEV - 1,)), pltpu.SemaphoreType.DMA((N_DEV - 1,)), pltpu.SemaphoreType.DMA]


def _exchange(x, *, gather, name):
    blk = x.shape if gather else x.shape[1:]

    def body(x_ref, o_ref, send_sems, recv_sems, local_sem):
        copies = _exchange_copies(x_ref, o_ref, send_sems, recv_sems, local_sem, gather)
        _exchange_start(copies)
        _exchange_wait(copies)

    return pl.pallas_call(
        body, name=name,
        in_specs=[pl.BlockSpec(memory_space=pltpu.HBM)], out_specs=pl.BlockSpec(memory_space=pltpu.HBM),
        out_shape=jax.ShapeDtypeStruct((N_DEV,) + tuple(blk), x.dtype),
        scratch_shapes=list(_EXCHANGE_SCRATCH),
    )(x)


def _reduce_adamw(parts, w, m, v, *, name):
    r = w.shape[0]
    tr = max(d for d in range(16, 3201, 16) if r % d == 0)

    def body(p_ref, w_ref, m_ref, v_ref, g_ref, d_ref, nm_ref, nv_ref):
        g = p_ref[0].astype(f32)
        for s in range(1, N_DEV):
            g = g + p_ref[s].astype(f32)
        mm = ADAM_B1 * m_ref[...] + (1.0 - ADAM_B1) * g
        vv = ADAM_B2 * v_ref[...] + (1.0 - ADAM_B2) * (g * g)
        m_hat = mm / (1.0 - ADAM_B1 ** ADAM_STEP)
        v_hat = vv / (1.0 - ADAM_B2 ** ADAM_STEP)
        g_ref[...] = g
        d_ref[...] = -ADAM_LR * (m_hat / (jnp.sqrt(v_hat) + ADAM_EPS) + ADAM_WD * w_ref[...])
        nm_ref[...] = mm
        nv_ref[...] = vv

    spec = pl.BlockSpec((tr, 128), lambda i: (i, 0))
    return pl.pallas_call(
        body, name=name, grid=(r // tr,),
        in_specs=[pl.BlockSpec((N_DEV, tr, 128), lambda i: (0, i, 0)), spec, spec, spec],
        out_specs=[spec] * 4, out_shape=[jax.ShapeDtypeStruct((r, 128), f32)] * 4,
    )(parts, w, m, v)


_SHARDED = ("gdn_w_in", "gdn_w_out", "kv_w_down", "kv_w_up", "mla_w_in", "mla_w_q_up", "mla_w_out", "meta_tokens", "gdn_conv_w")
_COL_SHARDED = {"gdn_w_in", "kv_w_up", "mla_w_in", "mla_w_q_up", "meta_tokens", "gdn_conv_w"}
_GATHER_FIRST = ("gdn_w_in",)
_GATHER_F32 = ("meta_tokens", "gdn_conv_w")
_GATHER_REST = ("gdn_w_out", "kv_w_down", "kv_w_up", "mla_w_in", "mla_w_q_up", "mla_w_out")
_SCATTER_EARLY = ("gdn_w_out", "kv_w_down", "kv_w_up", "mla_w_in", "mla_w_q_up", "mla_w_out")
_SCATTER_LATE = ("gdn_w_in", "meta_tokens", "gdn_conv_w")
_REPLICATED = ("pre_norm", "post_norm", "gdn_a_log", "gdn_dt_bias", "gdn_out_norm", "kv_norm", "kv_latent_norm",
               "mla_q_latent_norm")


def _rows128(a):
    flat = a.reshape(-1)
    pad = (-flat.shape[0]) % 128
    if pad:
        flat = jnp.pad(flat, (0, pad))
    return flat.reshape(-1, 128)


def _pack(arrs, row_multiple):
    parts = [_rows128(a) for a in arrs]
    buf = jnp.concatenate(parts, axis=0)
    pad = (-buf.shape[0]) % row_multiple
    if pad:
        buf = jnp.pad(buf, ((0, pad), (0, 0)))
    return buf


def _unpack(buf, shapes):
    out, r = [], 0
    for shp in shapes:
        n = math.prod(shp)
        rows = -(-n // 128)
        out.append(buf[r:r + rows].reshape(-1)[:n].reshape(shp))
        r += rows
    return out


def _unshard(g, full_shape, col):
    if col:
        return jnp.transpose(g, (1, 0, 2)).reshape(full_shape)
    return g.reshape(full_shape)


def _to_shards(a, col):
    r, c = a.shape
    if col:
        return jnp.transpose(a.reshape(r, N_DEV, c // N_DEV), (1, 0, 2))
    return a.reshape(N_DEV, r // N_DEV, c)


def _pad_cols(a, width):
    return jnp.pad(a, ((0, 0), (0, width - a.shape[1])))


def _rope_tables(lp):
    inv = ROPE_THETA ** (-jnp.arange(0, MLA_ROPE, 2, dtype=f32) / MLA_ROPE)
    pos = (jnp.arange(lp, dtype=jnp.int32) - PAD_FRONT).astype(f32)
    ang = pos[:, None] * inv[None, :]
    cos, sin = jnp.cos(ang), jnp.sin(ang)
    z = jnp.zeros((lp, 64), f32)
    return jnp.concatenate([cos, cos, z], axis=1), jnp.concatenate([-sin, sin, z], axis=1)


def kernel(x, meta_tokens, pre_norm, post_norm, gdn_w_in, gdn_conv_w, gdn_a_log, gdn_dt_bias, gdn_out_norm, gdn_w_out, kv_norm, kv_w_down, kv_latent_norm, kv_w_up, mla_w_in, mla_q_latent_norm, mla_w_q_up, mla_w_out, loss_target, m_meta_tokens, m_pre_norm, m_post_norm, m_gdn_w_in, m_gdn_conv_w, m_gdn_a_log, m_gdn_dt_bias, m_gdn_out_norm, m_gdn_w_out, m_kv_norm, m_kv_w_down, m_kv_latent_norm, m_kv_w_up, m_mla_w_in, m_mla_q_latent_norm, m_mla_w_q_up, m_mla_w_out, v_meta_tokens, v_pre_norm, v_post_norm, v_gdn_w_in, v_gdn_conv_w, v_gdn_a_log, v_gdn_dt_bias, v_gdn_out_norm, v_gdn_w_out, v_kv_norm, v_kv_w_down, v_kv_latent_norm, v_kv_w_up, v_mla_w_in, v_mla_q_latent_norm, v_mla_w_q_up, v_mla_w_out):
    W = dict(meta_tokens=meta_tokens, pre_norm=pre_norm, post_norm=post_norm, gdn_w_in=gdn_w_in, gdn_conv_w=gdn_conv_w,
             gdn_a_log=gdn_a_log, gdn_dt_bias=gdn_dt_bias, gdn_out_norm=gdn_out_norm, gdn_w_out=gdn_w_out, kv_norm=kv_norm,
             kv_w_down=kv_w_down, kv_latent_norm=kv_latent_norm, kv_w_up=kv_w_up, mla_w_in=mla_w_in,
             mla_q_latent_norm=mla_q_latent_norm, mla_w_q_up=mla_w_q_up, mla_w_out=mla_w_out)
    M = dict(meta_tokens=m_meta_tokens, pre_norm=m_pre_norm, post_norm=m_post_norm, gdn_w_in=m_gdn_w_in, gdn_conv_w=m_gdn_conv_w,
             gdn_a_log=m_gdn_a_log, gdn_dt_bias=m_gdn_dt_bias, gdn_out_norm=m_gdn_out_norm, gdn_w_out=m_gdn_w_out, kv_norm=m_kv_norm,
             kv_w_down=m_kv_w_down, kv_latent_norm=m_kv_latent_norm, kv_w_up=m_kv_w_up, mla_w_in=m_mla_w_in,
             mla_q_latent_norm=m_mla_q_latent_norm, mla_w_q_up=m_mla_w_q_up, mla_w_out=m_mla_w_out)
    V = dict(meta_tokens=v_meta_tokens, pre_norm=v_pre_norm, post_norm=v_post_norm, gdn_w_in=v_gdn_w_in, gdn_conv_w=v_gdn_conv_w,
             gdn_a_log=v_gdn_a_log, gdn_dt_bias=v_gdn_dt_bias, gdn_out_norm=v_gdn_out_norm, gdn_w_out=v_gdn_w_out, kv_norm=v_kv_norm,
             kv_w_down=v_kv_w_down, kv_latent_norm=v_kv_latent_norm, kv_w_up=v_kv_w_up, mla_w_in=v_mla_w_in,
             mla_q_latent_norm=v_mla_q_latent_norm, mla_w_q_up=v_mla_w_q_up, mla_w_out=v_mla_w_out)
    order = list(W)

    n_tok = x.shape[1]
    assert n_tok % GDN_CHUNK == 0
    n_real = ROW0 + n_tok
    lp = -(-n_real // ROW_TILE) * ROW_TILE
    n_real_chunks = n_real // GDN_CHUNK

    shard2d = {n: W[n].reshape(W[n].shape[-2:]) for n in _SHARDED}
    full_shape = {n: ((s.shape[0], s.shape[1] * N_DEV) if n in _COL_SHARDED else (s.shape[0] * N_DEV, s.shape[1]))
                  for n, s in shard2d.items()}
    full = {}

    def unpack_gathered(names, buf):
        r = 0
        for n in names:
            shp = shard2d[n].shape
            rows = math.prod(shp) // 128
            blocks = buf[:, r:r + rows].reshape((N_DEV,) + shp)
            full[n] = _unshard(blocks, full_shape[n], n in _COL_SHARDED)
            r += rows

    unpack_gathered(_GATHER_FIRST, _exchange(_pack([shard2d[n].astype(bf16) for n in _GATHER_FIRST], 16), gather=True,
                                             name="gather_w_in"))
    unpack_gathered(_GATHER_F32, _exchange(_pack([shard2d[n] for n in _GATHER_F32], 8), gather=True, name="gather_meta_conv"))
    rest_shards = _pack([shard2d[n].astype(bf16) for n in _GATHER_REST], 16)

    h0 = jnp.concatenate([jnp.zeros((PAD_FRONT, D_MODEL), f32), full["meta_tokens"], x[0],
                          jnp.zeros((lp - n_real, D_MODEL), f32)], axis=0)
    tgt = jnp.concatenate([jnp.zeros((ROW0, D_MODEL), f32), loss_target[0], jnp.zeros((lp - n_real, D_MODEL), f32)], axis=0)
    cos_k, sin_k = _rope_tables(lp)
    one = jnp.ones((lp, 128), f32)
    cos_q = jnp.concatenate([one, cos_k], axis=1)
    sin_q = jnp.concatenate([jnp.zeros((lp, 128), f32), sin_k], axis=1)

    w_in = full["gdn_w_in"]
    s1 = GDN_CONV_W + GDN_V_W
    w_in_p = jnp.concatenate([w_in[:, :s1], _pad_cols(w_in[:, s1:s1 + 16], 128), _pad_cols(w_in[:, s1 + 16:], 128)], axis=1)
    pre0, pre1 = pre_norm[0:1], pre_norm[1:2]
    post0, post1 = post_norm[0:1], post_norm[1:2]
    (hn0,) = _rowwise(_st_prenorm, [(h0, None, 0)], [], [pre0], [(D_MODEL, bf16, None)], name="f_prenorm0")
    proj, g_rest = _matmul(hn0, w_in_p, out_dtype=bf16, name="f_gdn_in", side=(rest_shards, True))
    unpack_gathered(_GATHER_REST, g_rest)

    wd = full["kv_w_down"]
    zc = jnp.zeros((D_MODEL, 64), bf16)
    wd2 = jnp.concatenate([wd, zc, jnp.zeros((D_MODEL, 128), bf16), wd[:, 160:192], wd[:, 128:160], zc], axis=1)
    wup_p = jnp.transpose(full["kv_w_up"].reshape(MLA_KV_RANK, MLA_HEADS, 2, 128), (0, 2, 1, 3)).reshape(MLA_KV_RANK, 2 * MLA_V_W)
    wq = full["mla_w_q_up"].reshape(MLA_Q_RANK, MLA_HEADS, MLA_QK)
    zq64 = jnp.zeros((MLA_Q_RANK, MLA_HEADS, 64), bf16)
    wq_plain = jnp.concatenate([wq, zq64], axis=2).reshape(MLA_Q_RANK, MLA_HEADS * MLA_QKP)
    wq_swap = jnp.concatenate([jnp.zeros((MLA_Q_RANK, MLA_HEADS, 128), bf16), wq[:, :, 160:192], wq[:, :, 128:160], zq64],
                              axis=2).reshape(MLA_Q_RANK, MLA_HEADS * MLA_QKP)
    q_half = Q_GROUP * MLA_QKP
    wq2 = jnp.concatenate([wq_plain[:, :q_half], wq_swap[:, :q_half], wq_plain[:, q_half:], wq_swap[:, q_half:]], axis=1)
    w_mla_in, w_gdn_out, w_mla_out = full["mla_w_in"], full["gdn_w_out"], full["mla_w_out"]
    conv_w = full["gdn_conv_w"]
    alog_p, dtb_p = _pad_cols(gdn_a_log, 128), _pad_cols(gdn_dt_bias, 128)
    kvn, kvln = kv_norm.reshape(1, -1), kv_latent_norm.reshape(1, -1)

    conv = _conv_fwd(proj, conv_w, col_blocks=GDN_CONV_W // CONV_BC, name="f_conv")
    (qn,) = _rowwise(_st_gdn_q, [(conv, GDN_QK_W, 0)], [], [], [(GDN_QK_W, bf16, GDN_QK_W)], heads=GDN_QK_HEADS, name="f_gdn_q")
    (kn,) = _rowwise(_st_gdn_k, [(conv, GDN_QK_W, 1)], [], [], [(GDN_QK_W, bf16, GDN_QK_W)], heads=GDN_QK_HEADS, name="f_gdn_k")
    (vv,) = _rowwise(_st_gdn_v, [(conv, GDN_V_W, 1)], [], [], [(GDN_V_W, bf16, GDN_V_W)], name="f_gdn_v")
    gate_rows = [(proj, 128, s1 // 128), (proj, 128, s1 // 128 + 1)]
    beta, gdec = _rowwise(_st_gdn_gate, gate_rows, [], [alog_p, dtb_p], [(128, f32, None)] * 2, name="f_gdn_gate")
    o_gdn, states, inverses = _gdn_fwd(qn, kn, vv, beta, gdec, n_real_chunks=n_real_chunks, name="f_gdn")
    out_rows = [(o_gdn, GDN_V_W, 0), (proj, GDN_V_W, GDN_CONV_W // GDN_V_W)]
    (og,) = _rowwise(_st_gdn_out, out_rows, [], [gdn_out_norm], [(GDN_V_W, bf16, GDN_V_W)], heads=GDN_V_HEADS, name="f_gdn_out")
    y0 = _matmul(og, w_gdn_out, name="f_gdn_wout")
    mid_rows = [(h0, None, 0), (y0, None, 0)]
    h1, hn1, hkv = _rowwise(_st_mid, mid_rows, [], [post0, pre1, kvn],
                            [(D_MODEL, f32, None), (D_MODEL, bf16, None), (D_MODEL, bf16, None)], name="f_mid")
    ckr = _matmul(hkv, wd2, name="f_kv_down")
    proj2 = _matmul(hn1, w_mla_in, out_dtype=bf16, name="f_mla_in")
    lat_rows = [(ckr, None, 0), (proj2, MLA_Q_RANK, 0)]
    lat_nd = [(cos_k, None, 0), (sin_k, None, 0)]
    c_kv, k_rope, c_q = _rowwise(_st_latent, lat_rows, lat_nd, [kvln, mla_q_latent_norm],
                                 [(128, bf16, None), (128, bf16, None), (MLA_Q_RANK, bf16, None)], name="f_latent")
    kvu = _matmul(c_kv, wup_p, out_dtype=bf16, name="f_kv_up")
    q_att = _q_proj_fwd(c_q, wq2, cos_q, sin_q, name="f_q_proj")
    o_att, lse, og2 = _attention_fwd(q_att, kvu, k_rope, proj2, name="f_attention")
    y1 = _matmul(og2, w_mla_out, name="f_mla_wout")
    st_loss = _make_st_loss(n_tok)
    loss_rows_in = [(h1, None, 0), (y1, None, 0)]
    (loss_rows,) = _rowwise(st_loss, loss_rows_in, [(tgt, None, 0)], [post1], [(1, f32, None)], name="f_loss")
    loss_here = jnp.sum(loss_rows)

    ones_ct = jnp.ones((lp, 1), f32)
    (dh1_a, dy1), (dpost1,) = _rowwise_vjp(st_loss, loss_rows_in, [(tgt, None, 0)], [post1], [(ones_ct, None, 0)], name="b_loss")
    dog2 = _matmul(dy1, w_mla_out, tb=True, name="b_mla_wout_x")
    dw_mla_out = _matmul(og2, dy1, ta=True, name="b_mla_wout_w")
    do_att, dz2, delta = _gate_bwd(o_att, proj2, dog2, name="b_mla_gate")
    dq_att, dkn, dvv, dkr_h = _attention_bwd(q_att, kvu, k_rope, lse, delta, do_att, name="b_attention")
    dc_q, dwq2 = _q_proj_bwd(dq_att, c_q, wq2, cos_q, sin_q, name="b_q_proj")
    dkvu = jnp.concatenate([dkn, dvv], axis=1)
    dc_kv = _matmul(dkvu, wup_p, tb=True, name="b_kv_up_x")
    dwup_p = _matmul(c_kv, dkvu, ta=True, name="b_kv_up_w")

    def lat_ct(cv):
        dkr = cv[1][:, 0:128]
        for h in range(1, MLA_HEADS):
            dkr = dkr + cv[1][:, h * 128:(h + 1) * 128]
        return [cv[0], dkr, cv[2][:, :MLA_Q_RANK] + cv[2][:, MLA_Q_RANK:]]

    (dckr, dcq_pre), (dkvln, dqln) = _rowwise_vjp(
        _st_latent, lat_rows, lat_nd, [kvln, mla_q_latent_norm],
        [(dc_kv, None, 0), (dkr_h, None, 0), (dc_q, None, 0)], ct_pre=lat_ct, name="b_latent", grad_dtypes=[bf16, bf16])
    dproj2 = jnp.concatenate([dcq_pre, dz2], axis=1)
    dhn1 = _matmul(dproj2, w_mla_in, tb=True, name="b_mla_in_x")
    dw_mla_in = _matmul(hn1, dproj2, ta=True, name="b_mla_in_w")
    dhkv = _matmul(dckr, wd2, tb=True, name="b_kv_down_x")
    dwd2 = _matmul(hkv, dckr, ta=True, name="b_kv_down_w")
    (dh0_a, dy0), (dpost0, dpre1, dkvn) = _rowwise_vjp(
        _st_mid, mid_rows, [], [post0, pre1, kvn], [(dh1_a, None, 0), (dhn1, None, 0), (dhkv, None, 0)], name="b_mid")
    dog = _matmul(dy0, w_gdn_out, tb=True, name="b_gdn_wout_x")
    dw_gdn_out = _matmul(og, dy0, ta=True, name="b_gdn_wout_w")
    (do_gdn, dz), (dout_norm,) = _rowwise_vjp(_st_gdn_out, out_rows, [], [gdn_out_norm], [(dog, GDN_V_W, 0)], heads=GDN_V_HEADS,
                                              name="b_gdn_out", grad_dtypes=[f32, bf16])
    dq_g, dk_g, dv_g, dbeta, dgdec = _gdn_bwd(qn, kn, vv, beta, gdec, states, inverses, do_gdn, n_real_chunks=n_real_chunks,
                                              name="b_gdn")
    (db_col, da_col), (dalog_p, ddtb_p) = _rowwise_vjp(
        _st_gdn_gate, gate_rows, [], [alog_p, dtb_p], [(dbeta, None, 0), (dgdec, None, 0)], name="b_gdn_gate",
        grad_dtypes=[bf16, bf16])
    (dconv_q,), _ = _rowwise_vjp(_st_gdn_q, [(conv, GDN_QK_W, 0)], [], [], [(dq_g, GDN_QK_W, 0)], heads=GDN_QK_HEADS, name="b_gdn_q")
    (dconv_k,), _ = _rowwise_vjp(_st_gdn_k, [(conv, GDN_QK_W, 1)], [], [], [(dk_g, GDN_QK_W, 0)], heads=GDN_QK_HEADS, name="b_gdn_k")
    (dconv_v,), _ = _rowwise_vjp(_st_gdn_v, [(conv, GDN_V_W, 1)], [], [], [(dv_g, GDN_V_W, 0)], name="b_gdn_v")
    nq_b = GDN_QK_W // CONV_BC
    dpre_q, dcw_q = _conv_bwd(dconv_q, proj, conv_w, x_off=0, w_off=0, name="b_conv_q")
    dpre_k, dcw_k = _conv_bwd(dconv_k, proj, conv_w, x_off=nq_b, w_off=nq_b, name="b_conv_k")
    dpre_v, dcw_v = _conv_bwd(dconv_v, proj, conv_w, x_off=2 * nq_b, w_off=2 * nq_b, name="b_conv_v")
    dproj = jnp.concatenate([dpre_q, dpre_k, dpre_v, dz, db_col, da_col], axis=1)
    G = {}
    G["kv_w_down"] = jnp.concatenate([dwd2[:, :128], dwd2[:, 128:160] + dwd2[:, 416:448], dwd2[:, 160:192] + dwd2[:, 384:416]], axis=1)
    G["kv_w_up"] = jnp.transpose(dwup_p.reshape(MLA_KV_RANK, 2, MLA_HEADS, 128), (0, 2, 1, 3)).reshape(MLA_KV_RANK, 2 * MLA_V_W)
    G["mla_w_in"] = dw_mla_in
    dq4 = dwq2.reshape(MLA_Q_RANK, 2, 2, Q_GROUP, MLA_QKP)
    dqp = dq4[:, :, 0].reshape(MLA_Q_RANK, MLA_HEADS, MLA_QKP)
    dqs = dq4[:, :, 1].reshape(MLA_Q_RANK, MLA_HEADS, MLA_QKP)
    G["mla_w_q_up"] = jnp.concatenate([dqp[:, :, :128], dqp[:, :, 128:160] + dqs[:, :, 160:192],
                                       dqp[:, :, 160:192] + dqs[:, :, 128:160]], axis=2).reshape(MLA_Q_RANK, MLA_HEADS * MLA_QK)
    G["mla_w_out"] = dw_mla_out
    G["gdn_w_out"] = dw_gdn_out

    def shards_to_send(names):
        return jnp.concatenate([_to_shards(G[n], n in _COL_SHARDED).reshape(N_DEV, -1, 128).astype(bf16) for n in names], axis=1)

    dhn0, parts_early = _matmul(dproj, w_in_p, tb=True, name="b_gdn_in_x", side=(shards_to_send(_SCATTER_EARLY), False))
    dw_in_p = _matmul(hn0, dproj, ta=True, name="b_gdn_in_w")
    (dh0,), (dpre0,) = _rowwise_vjp(_st_prenorm, [(h0, None, 0)], [], [pre0], [(dhn0, None, 0)], extra=[dh0_a], name="b_prenorm0")

    grad_x = dh0[ROW0:n_real][None]
    G["meta_tokens"] = dh0[PAD_FRONT:ROW0]
    G["gdn_w_in"] = jnp.concatenate([dw_in_p[:, :s1 + 16], dw_in_p[:, s1 + 128:s1 + 144]], axis=1)
    G["gdn_conv_w"] = jnp.concatenate([dcw_q, dcw_k, dcw_v], axis=1)
    G["pre_norm"] = jnp.concatenate([dpre0, dpre1], axis=0)
    G["post_norm"] = jnp.concatenate([dpost0, dpost1], axis=0)
    G["gdn_a_log"] = dalog_p[:, :GDN_V_HEADS]
    G["gdn_dt_bias"] = ddtb_p[:, :GDN_V_HEADS]
    G["gdn_out_norm"] = dout_norm
    G["kv_norm"] = dkvn.reshape(-1)
    G["kv_latent_norm"] = dkvln.reshape(-1)
    G["mla_q_latent_norm"] = dqln

    parts_late = _exchange(shards_to_send(_SCATTER_LATE), gather=False, name="scatter_grads")
    G["loss"] = loss_here.reshape(1, 1)
    for d in (W, M, V):
        d["loss"] = jnp.zeros((1, 1), f32)
    replicated = _REPLICATED + ("loss",)
    parts_r = _exchange(_pack([G[n] for n in replicated], 8), gather=True, name="gather_small_grads")
    outs = {}
    for names, parts, tag in ((_SCATTER_EARLY, parts_early, "early"), (_SCATTER_LATE, parts_late, "late"),
                              (replicated, parts_r, "replicated")):
        w_p, m_p, v_p = (_pack([d[n] for n in names], 8) for d in (W, M, V))
        res = _reduce_adamw(parts, w_p, m_p, v_p, name="adamw_" + tag)
        for kind, buf in zip(("grad", "delta", "new_m", "new_v"), res):
            for n, a in zip(names, _unpack(buf, [W[n].shape for n in names])):
                outs[kind, n] = a
    loss = outs["grad", "loss"].reshape(())
    return (loss, grad_x, *[outs[k, n] for k in ("grad", "delta", "new_m", "new_v") for n in order])
```

```python
import functools
import math

import jax
import jax.numpy as jnp
from jax import lax
from jax.experimental import pallas as pl
from jax.experimental.pallas import tpu as pltpu

f32, bf16 = jnp.float32, jnp.bfloat16
MESH_ID = pl.DeviceIdType.MESH

N_DEV = 8
D_MODEL = 1024
N_META = 16
NORM_EPS = 1e-6
PAD_FRONT = 48
ROW0 = PAD_FRONT + N_META
GDN_QK_HEADS, GDN_V_HEADS, GDN_D = 8, 16, 128
GDN_CHUNK = 64
GDN_QK_W, GDN_V_W = GDN_QK_HEADS * GDN_D, GDN_V_HEADS * GDN_D
GDN_CONV_W = 2 * GDN_QK_W + GDN_V_W
GDN_IN_W = GDN_CONV_W + GDN_V_W + 2 * GDN_V_HEADS
GDN_IN_WP = GDN_CONV_W + GDN_V_W + 2 * 128
MLA_HEADS, MLA_NOPE, MLA_ROPE, MLA_V = 16, 128, 64, 128
MLA_Q_RANK, MLA_KV_RANK = 256, 128
MLA_QK = MLA_NOPE + MLA_ROPE
MLA_QKP = 256
MLA_V_W = MLA_HEADS * MLA_V
ROPE_THETA = 10000.0
NEG = -1e30
ROW_TILE = 256
ATT_KEY_TILES = (384, 256)

ADAM_LR, ADAM_B1, ADAM_B2, ADAM_EPS, ADAM_WD, ADAM_STEP = 0.001, 0.9, 0.999, 1e-08, 0.01, 10

NN = ((1,), (0,))
NT = ((1,), (1,))
TN = ((0,), (0,))


def _pick(dim, prefs):
    for p in prefs:
        if dim % p == 0:
            return p
    return dim


def _dlo(a, b, dims):
    return lax.dot_general(a.astype(bf16), b.astype(bf16), (dims, ((), ())), preferred_element_type=f32)


def _dsel(a, b, dims, selector):
    x = b if selector == 0 else a
    hi = x.astype(bf16)
    rest = x - hi.astype(f32)
    mid = rest.astype(bf16)
    low = (rest - mid.astype(f32)).astype(bf16)
    sel = (a if selector == 0 else b).astype(bf16)
    d = (lambda p: lax.dot_general(sel, p, (dims, ((), ())), preferred_element_type=f32)) if selector == 0 else \
        (lambda p: lax.dot_general(p, sel, (dims, ((), ())), preferred_element_type=f32))
    return d(hi) + (d(mid) + d(low))


def _matmul(a, b, *, ta=False, tb=False, out_dtype=f32, name, side=None):
    assert not (ta and tb)
    if ta:
        kdim, m = a.shape
    else:
        m, kdim = a.shape
    n = b.shape[0] if tb else b.shape[1]
    assert (b.shape[1] if tb else b.shape[0]) == kdim
    tm = _pick(m, (1024, 768, 512, 384, 256, 128))
    tn = _pick(n, (1024, 768, 640, 512, 256, 128))
    tk = _pick(kdim, (1024, 768, 640, 512, 256, 128))
    nk = kdim // tk
    dims = TN if ta else (NT if tb else NN)

    grid = (m // tm, n // tn, nk)

    def product(a_ref, b_ref, o_ref, acc_ref):
        k = pl.program_id(2)

        @pl.when(k == 0)
        def _():
            acc_ref[...] = jnp.zeros_like(acc_ref)

        acc_ref[...] += lax.dot_general(a_ref[...].astype(bf16), b_ref[...].astype(bf16), (dims, ((), ())),
                                        preferred_element_type=f32)

        @pl.when(k == nk - 1)
        def _():
            o_ref[...] = acc_ref[...].astype(o_ref.dtype)

    a_spec = pl.BlockSpec((tk, tm), lambda i, j, k: (k, i)) if ta else pl.BlockSpec((tm, tk), lambda i, j, k: (i, k))
    b_spec = pl.BlockSpec((tn, tk), lambda i, j, k: (j, k)) if tb else pl.BlockSpec((tk, tn), lambda i, j, k: (k, j))
    o_spec = pl.BlockSpec((tm, tn), lambda i, j, k: (i, j))
    o_shape = jax.ShapeDtypeStruct((m, n), out_dtype)
    if side is None:
        def body(a_ref, b_ref, o_ref, acc_ref):
            product(a_ref, b_ref, o_ref, acc_ref)

        return pl.pallas_call(
            body, name=name, grid=grid, in_specs=[a_spec, b_spec], out_specs=o_spec, out_shape=o_shape,
            scratch_shapes=[pltpu.VMEM((tm, tn), f32)],
            compiler_params=pltpu.CompilerParams(dimension_semantics=("parallel", "parallel", "arbitrary")),
        )(a, b)

    x, gather = side
    blk = x.shape if gather else x.shape[1:]

    def body_with_exchange(a_ref, b_ref, x_ref, o_ref, xo_ref, acc_ref, send_sems, recv_sems, local_sem):
        step = (pl.program_id(0) * grid[1] + pl.program_id(1)) * grid[2] + pl.program_id(2)
        copies = _exchange_copies(x_ref, xo_ref, send_sems, recv_sems, local_sem, gather)

        @pl.when(step == 0)
        def _():
            _exchange_start(copies)

        product(a_ref, b_ref, o_ref, acc_ref)

        @pl.when(step == grid[0] * grid[1] * grid[2] - 1)
        def _():
            _exchange_wait(copies)

    hbm = pl.BlockSpec(memory_space=pltpu.HBM)
    return pl.pallas_call(
        body_with_exchange, name=name, grid=grid, in_specs=[a_spec, b_spec, hbm], out_specs=[o_spec, hbm],
        out_shape=[o_shape, jax.ShapeDtypeStruct((N_DEV,) + tuple(blk), x.dtype)],
        scratch_shapes=[pltpu.VMEM((tm, tn), f32)] + list(_EXCHANGE_SCRATCH),
        compiler_params=pltpu.CompilerParams(dimension_semantics=("arbitrary", "arbitrary", "arbitrary")),
    )(a, b, x)


def _row_spec(item, tr):
    a, bc, off = item
    if bc is None:
        return pl.BlockSpec((tr, a.shape[1]), lambda i, j: (i, 0))
    return pl.BlockSpec((tr, bc), lambda i, j, off=off: (i, j + off))


def _param_spec(p):
    return pl.BlockSpec(p.shape, lambda i, j: (0, 0))


def _row_tile(lp, items):
    widest = max(a.shape[1] if bc is None else bc for (a, bc, _) in items)
    return ROW_TILE if widest >= 1024 else _pick(lp, (768, 512, 256))


def _head_cols(tiles, h, heads):
    return [x[:, h * (x.shape[1] // heads):(h + 1) * (x.shape[1] // heads)] for x in tiles]


def _rowwise(fn, rows, nodiff, params, outs, *, ncol=1, heads=1, name):
    lp = rows[0][0].shape[0]
    tr = _row_tile(lp, rows)
    nr, nd = len(rows), len(nodiff)

    def body(*refs):
        rv = [r[...].astype(f32) for r in refs[:nr]]
        nv = [r[...] for r in refs[nr:nr + nd]]
        pv = [r[...] for r in refs[nr + nd:nr + nd + len(params)]]
        per_head = [fn(_head_cols(rv, h, heads), nv, pv) for h in range(heads)]
        res = [jnp.concatenate(list(vals), axis=1) if heads > 1 else vals[0] for vals in zip(*per_head)]
        for ref, val in zip(refs[nr + nd + len(params):], res):
            ref[...] = val.astype(ref.dtype)

    out_specs = [pl.BlockSpec((tr, c if bc is None else bc), (lambda i, j: (i, 0)) if bc is None else (lambda i, j: (i, j)))
                 for (c, _, bc) in outs]
    return pl.pallas_call(
        body, name=name, grid=(lp // tr, ncol),
        in_specs=[_row_spec(it, tr) for it in rows + nodiff] + [_param_spec(p) for p in params],
        out_specs=out_specs,
        out_shape=[jax.ShapeDtypeStruct((lp, c), dt) for (c, dt, _) in outs],
    )(*[it[0] for it in rows + nodiff], *params)


def _rowwise_vjp(fn, rows, nodiff, params, cts, *, ncol=1, heads=1, name, ct_pre=None, extra=None, grad_dtypes=None):
    lp = rows[0][0].shape[0]
    tr = _row_tile(lp, rows)
    nr, nd, npar, nct = len(rows), len(nodiff), len(params), len(cts)
    extra = extra or [None] * nr
    grad_dtypes = grad_dtypes or [f32] * nr
    ex_items = [(e, rows[k][1], 0) for k, e in enumerate(extra) if e is not None]
    ex_pos = [k for k, e in enumerate(extra) if e is not None]
    for (a, bc, _) in rows:
        assert bc is not None or ncol == 1

    def body(*refs):
        pos = 0
        rv = [r[...].astype(f32) for r in refs[pos:pos + nr]]; pos += nr
        nv = [r[...] for r in refs[pos:pos + nd]]; pos += nd
        pv = [r[...] for r in refs[pos:pos + npar]]; pos += npar
        cv = [r[...].astype(f32) for r in refs[pos:pos + nct]]; pos += nct
        ev = [r[...].astype(f32) for r in refs[pos:pos + len(ex_items)]]; pos += len(ex_items)
        drow_refs = refs[pos:pos + nr]; pos += nr
        dpar_refs = refs[pos:pos + npar]
        ctv = ct_pre(cv) if ct_pre is not None else cv
        drow_h, dpar = [], None
        for h in range(heads):
            outs, vjp_fn = jax.vjp(lambda rr, pp: fn(rr, nv, pp), _head_cols(rv, h, heads), pv)
            dr, dp = vjp_fn([c.astype(o.dtype) for c, o in zip(_head_cols(ctv, h, heads), outs)])
            drow_h.append(dr)
            dpar = dp if dpar is None else [a + b for a, b in zip(dpar, dp)]
        drow = [jnp.concatenate(list(vals), axis=1) if heads > 1 else vals[0] for vals in zip(*drow_h)]
        for k, e in zip(ex_pos, ev):
            drow[k] = drow[k] + e
        for ref, val in zip(drow_refs, drow):
            ref[...] = val.astype(ref.dtype)
        first = jnp.logical_and(pl.program_id(0) == 0, pl.program_id(1) == 0)

        @pl.when(first)
        def _():
            for ref, val in zip(dpar_refs, dpar):
                ref[...] = val

        @pl.when(jnp.logical_not(first))
        def _():
            for ref, val in zip(dpar_refs, dpar):
                ref[...] += val

    drow_shapes, drow_specs = [], []
    for (a, bc, _), dt in zip(rows, grad_dtypes):
        if bc is None:
            drow_shapes.append(jax.ShapeDtypeStruct((lp, a.shape[1]), dt))
            drow_specs.append(pl.BlockSpec((tr, a.shape[1]), lambda i, j: (i, 0)))
        else:
            drow_shapes.append(jax.ShapeDtypeStruct((lp, ncol * bc), dt))
            drow_specs.append(pl.BlockSpec((tr, bc), lambda i, j: (i, j)))
    res = pl.pallas_call(
        body, name=name, grid=(lp // tr, ncol),
        in_specs=[_row_spec(it, tr) for it in rows + nodiff] + [_param_spec(p) for p in params]
        + [_row_spec(it, tr) for it in cts + ex_items],
        out_specs=drow_specs + [_param_spec(p) for p in params],
        out_shape=drow_shapes + [jax.ShapeDtypeStruct(p.shape, f32) for p in params],
        compiler_params=pltpu.CompilerParams(dimension_semantics=("arbitrary", "arbitrary")),
    )(*[it[0] for it in rows + nodiff], *params, *[it[0] for it in cts + ex_items])
    return res[:nr], res[nr:]


def _rms(x, g):
    return x * lax.rsqrt(jnp.mean(x * x, axis=-1, keepdims=True) + NORM_EPS) * g


def _l2n(x):
    return x * lax.rsqrt(jnp.sum(x * x, axis=-1, keepdims=True) + NORM_EPS)


def _sigmoid(x):
    return 1.0 / (1.0 + jnp.exp(-x))


def _silu(x):
    return x * _sigmoid(x)


def _softplus(x):
    return jnp.maximum(x, 0.0) + jnp.log(1.0 + jnp.exp(-jnp.abs(x)))


def _row_ids(shape):
    return pl.program_id(0) * shape[0] + lax.broadcasted_iota(jnp.int32, shape, 0)


def _st_prenorm(r, n, p):
    return [_rms(r[0], p[0])]


def _st_gdn_q(r, n, p):
    return [_l2n(_silu(r[0])) * (GDN_D ** -0.5)]


def _st_gdn_k(r, n, p):
    return [_l2n(_silu(r[0]))]


def _st_gdn_v(r, n, p):
    return [_silu(r[0])]


def _st_gdn_gate(r, n, p):
    real = _row_ids(r[0].shape) >= PAD_FRONT
    beta = jnp.where(real, _sigmoid(r[0]), 0.0)
    g = jnp.where(real, -jnp.exp(p[0]) * _softplus(r[1] + p[1]), 0.0)
    return [beta, g]


def _st_gdn_out(r, n, p):
    return [_rms(r[0], p[0]) * _silu(r[1])]


def _st_mid(r, n, p):
    h1 = r[0] + _rms(r[1], p[0])
    return [h1, _rms(h1, p[1]), _rms(h1, p[2])]


def _st_latent(r, n, p):
    ckr, cq = r
    c_kv = _rms(ckr[:, :MLA_KV_RANK], p[0])
    k_rope = ckr[:, 128:256] * n[0] + ckr[:, 384:512] * n[1]
    return [c_kv, k_rope, _rms(cq, p[1])]


Q_GROUP = 8


def _st_q_rope(r, n, p):
    half = Q_GROUP * MLA_QKP
    out = []
    for h in range(Q_GROUP):
        cols = slice(h * MLA_QKP, (h + 1) * MLA_QKP)
        out.append((r[0][:, :half][:, cols] * n[0] + r[0][:, half:][:, cols] * n[1]) * (MLA_QK ** -0.5))
    return [jnp.concatenate(out, axis=1)]


def _st_q_rope_t(r, n, p):
    plain, swapped = [], []
    for h in range(Q_GROUP):
        ct = r[0][:, h * MLA_QKP:(h + 1) * MLA_QKP] * (MLA_QK ** -0.5)
        plain.append(ct * n[0])
        swapped.append(ct * n[1])
    return [jnp.concatenate(plain + swapped, axis=1)]


def _make_st_loss(n_tokens):
    def st(r, n, p):
        h2 = r[0] + _rms(r[1], p[0])
        rows = _row_ids((r[0].shape[0], 1))
        real = jnp.logical_and(rows >= ROW0, rows < ROW0 + n_tokens)
        err = h2 - n[0]
        return [jnp.where(real, 0.5 * jnp.mean(err * err, axis=-1, keepdims=True), 0.0)]
    return st


CONV_BC = 1024
HALO = 16


def _conv_fwd(x, w, *, col_blocks, name):
    lp = x.shape[0]
    tr = _pick(lp, (768, 512, 256))

    def body(x_ref, xp_ref, w_ref, o_ref):
        i = pl.program_id(0)
        xv = x_ref[...].astype(f32)
        prev = jnp.where(i > 0, xp_ref[...].astype(f32), 0.0)
        xc = jnp.concatenate([prev, xv], axis=0)
        wv = w_ref[...]
        acc = wv[3:4, :] * xv
        for j in range(3):
            acc = acc + wv[j:j + 1, :] * pltpu.roll(xc, 3 - j, 0)[HALO:, :]
        o_ref[...] = acc.astype(o_ref.dtype)

    return pl.pallas_call(
        body, name=name, grid=(lp // tr, col_blocks),
        in_specs=[pl.BlockSpec((tr, CONV_BC), lambda i, j: (i, j)),
                  pl.BlockSpec((HALO, CONV_BC), lambda i, j: (jnp.maximum(i * (tr // HALO) - 1, 0), j)),
                  pl.BlockSpec((4, CONV_BC), lambda i, j: (0, j))],
        out_specs=pl.BlockSpec((tr, CONV_BC), lambda i, j: (i, j)),
        out_shape=jax.ShapeDtypeStruct((lp, col_blocks * CONV_BC), bf16),
    )(x, x, w)


def _conv_bwd(dc, x, w, *, x_off, w_off, name):
    lp, width = dc.shape
    tr = _pick(lp, (768, 512, 256))
    ncb, nrow = width // CONV_BC, lp // tr

    def body(dc_ref, dcn_ref, x_ref, xp_ref, w_ref, dx_ref, dw_ref):
        i = pl.program_id(1)
        nxt = jnp.where(i < nrow - 1, dcn_ref[...], 0.0)
        dcv = dc_ref[...]
        dcc = jnp.concatenate([dcv, nxt], axis=0)
        xv = x_ref[...].astype(f32)
        prev = jnp.where(i > 0, xp_ref[...].astype(f32), 0.0)
        xc = jnp.concatenate([prev, xv], axis=0)
        wv = w_ref[...]
        dx = wv[3:4, :] * dcv
        dws = [None] * 4
        dws[3] = jnp.sum(dcv * xv, axis=0, keepdims=True)
        for j in range(3):
            dx = dx + wv[j:j + 1, :] * pltpu.roll(dcc, tr + 8 - (3 - j), 0)[:tr, :]
            dws[j] = jnp.sum(dcv * pltpu.roll(xc, 3 - j, 0)[HALO:, :], axis=0, keepdims=True)
        dx_ref[...] = dx.astype(dx_ref.dtype)

        @pl.when(i == 0)
        def _():
            for j in range(4):
                dw_ref[j:j + 1, :] = dws[j]

        @pl.when(i > 0)
        def _():
            for j in range(4):
                dw_ref[j:j + 1, :] += dws[j]

    last8 = lp // 8 - 1
    return pl.pallas_call(
        body, name=name, grid=(ncb, nrow),
        in_specs=[pl.BlockSpec((tr, CONV_BC), lambda j, i: (i, j)),
                  pl.BlockSpec((8, CONV_BC), lambda j, i: (jnp.minimum((i + 1) * (tr // 8), last8), j)),
                  pl.BlockSpec((tr, CONV_BC), lambda j, i: (i, j + x_off)),
                  pl.BlockSpec((HALO, CONV_BC), lambda j, i: (jnp.maximum(i * (tr // HALO) - 1, 0), j + x_off)),
                  pl.BlockSpec((4, CONV_BC), lambda j, i: (0, j + w_off))],
        out_specs=[pl.BlockSpec((tr, CONV_BC), lambda j, i: (i, j)),
                   pl.BlockSpec((4, CONV_BC), lambda j, i: (0, j))],
        out_shape=[jax.ShapeDtypeStruct((lp, width), bf16), jax.ShapeDtypeStruct((4, width), f32)],
        compiler_params=pltpu.CompilerParams(dimension_semantics=("arbitrary", "arbitrary")),
    )(dc, dc, x, x, w)


GDN_PACK = 4
GDN_FWD_INTERLEAVE, GDN_BWD_INTERLEAVE = 4, 4


MXU_TILE = 256


def _bd(x, cb, g=GDN_PACK):
    r = x.shape[0]
    tall = jnp.concatenate([x] * g, axis=0)
    rows = lax.broadcasted_iota(jnp.int32, tall.shape, 0) // r
    cols = lax.broadcasted_iota(jnp.int32, tall.shape, 1) // cb
    return jnp.where(rows == cols, tall, jnp.zeros_like(tall))


def _diag(full, r, cb, g=GDN_PACK):
    cols = lax.broadcasted_iota(jnp.int32, (r, full.shape[1]), 1) // cb
    out = jnp.where(cols == 0, full[0:r, :], 0.0)
    for a in range(1, g):
        out = out + jnp.where(cols == a, full[a * r:(a + 1) * r, :], 0.0)
    return out


def _stack(x, cb):
    return jnp.concatenate([x[:, a * cb:(a + 1) * cb] for a in range(GDN_PACK)], axis=0)


def _lane_halves(a):
    return a[:, :a.shape[1] // 2], a[:, a.shape[1] // 2:]


def _make_packed(dot):
    G = GDN_PACK

    def two_by_two(k, cb):
        return k * G > MXU_TILE and cb * G > MXU_TILE

    def times_bd(x, y, dims):
        k, cb = y.shape[0], y.shape[1] // G
        if two_by_two(k, cb):
            return jnp.concatenate([dot(xh, _bd(yh, cb, G // 2), dims)
                                    for xh, yh in zip(_lane_halves(x), _lane_halves(y))], axis=1)
        return dot(x, _bd(y, cb), dims)

    def tn_diag(x, y):
        k, cb = x.shape[1] // G, y.shape[1] // G
        if two_by_two(k, cb):
            return jnp.concatenate([_diag(dot(xh, yh, TN), k, cb, G // 2)
                                    for xh, yh in zip(_lane_halves(x), _lane_halves(y))], axis=1)
        return _diag(dot(x, y, TN), k, cb)

    @jax.custom_vjp
    def pmm(x, y):
        return times_bd(x, y, NN)

    @jax.custom_vjp
    def pnt(x, y):
        k = x.shape[1] // G
        return _diag(dot(_stack(x, k), _stack(y, k), NT), x.shape[0], y.shape[0])

    @jax.custom_vjp
    def ptn(x, y):
        return tn_diag(x, y)

    def pmm_bwd(res, ct):
        x, y = res
        return times_bd(ct, y, NT), tn_diag(x, ct)

    pmm.defvjp(lambda x, y: (pmm(x, y), (x, y)), pmm_bwd)
    pnt.defvjp(lambda x, y: (pnt(x, y), (x, y)), lambda res, ct: (pmm(ct, res[1]), ptn(ct, res[0])))
    ptn.defvjp(lambda x, y: (ptn(x, y), (x, y)), lambda res, ct: (pnt(res[1], ct), pmm(res[0], ct)))
    return pmm, pnt, ptn


_pmm, _pnt, _ptn = _make_packed(_dlo)


@jax.custom_vjp
def _inv_packed(ms):
    c = ms[0].shape[0]
    ii = lax.broadcasted_iota(jnp.int32, ms[0].shape, 0)
    jj = lax.broadcasted_iota(jnp.int32, ms[0].shape, 1) % c
    ts = [jnp.where(ii == jj, 1.0, 0.0) - m for m in ms]
    ps = [(-m).astype(bf16) for m in ms]
    for _ in range(int(math.log2(c)) - 1):
        ps = [_dlo(p, _bd(p, c), NN).astype(bf16) for p in ps]
        ts = [t + _dlo(t, _bd(p, c), NN) for t, p in zip(ts, ps)]
    return tuple(ts)


def _inv_packed_fwd(ms):
    ts = _inv_packed(ms)
    return ts, ts


def _inv_packed_bwd(ts, cts):
    c = ts[0].shape[0]
    ys = [_diag(_dlo(t, ct, TN), c, c) for t, ct in zip(ts, cts)]
    return (tuple(-_dlo(y, _bd(t.astype(bf16), c), NT) for y, t in zip(ys, ts)),)


_inv_packed.defvjp(_inv_packed_fwd, _inv_packed_bwd)


def _gdn_prep(q2, k2, v4, bcols, gcols, grows):
    c, d = v4.shape[0], GDN_D
    q4 = jnp.concatenate([q2[:, :d], q2[:, :d], q2[:, d:], q2[:, d:]], axis=1)
    k4 = jnp.concatenate([k2[:, :d], k2[:, :d], k2[:, d:], k2[:, d:]], axis=1)
    beta4 = jnp.concatenate([jnp.broadcast_to(b, (c, d)) for b in bcols], axis=1)
    gc4 = jnp.concatenate([jnp.broadcast_to(g, (c, d)) for g in gcols], axis=1)
    low = lax.broadcasted_iota(jnp.int32, (c, 128), 1) < c
    gi = jnp.concatenate([jnp.where(low, gcols[0], gcols[1]), jnp.where(low, gcols[2], gcols[3])], axis=1)
    gj = jnp.concatenate([jnp.where(low, grows[0], grows[1]), jnp.where(low, grows[2], grows[3])], axis=1)
    ii = lax.broadcasted_iota(jnp.int32, gi.shape, 0)
    jj = lax.broadcasted_iota(jnp.int32, gi.shape, 1) % c
    dec = jnp.exp(jnp.where(ii >= jj, gi - gj, NEG))
    rid = lax.broadcasted_iota(jnp.int32, gc4.shape, 0)
    glast = jnp.sum(jnp.where(rid == c - 1, gc4, 0.0), axis=0, keepdims=True)
    eg = jnp.exp(gc4)
    kb = k4 * beta4
    return dict(q=q4, k=k4, kb=kb, vb=v4 * beta4, kbe=kb * eg, qe=q4 * eg, dec=dec, dec_strict=jnp.where(ii > jj, dec, 0.0),
                sdecay=jnp.exp(glast), kd=k4 * jnp.exp(glast - gc4))


@jax.custom_vjp
def _inv_packed_known(ms, ts):
    return ts


_inv_packed_known.defvjp(lambda ms, ts: (ts, ts),
                         lambda ts, cts: (_inv_packed_bwd(ts, cts)[0], tuple(jnp.zeros_like(t) for t in ts)))


def _gdn_groups(groups, known_inverses=None, with_inverses=False):
    c = groups[0][3].shape[0]
    ss = [g[0] for g in groups]
    e = [_gdn_prep(*g[1:]) for g in groups]
    ms = tuple(_pnt(x["kb"], x["k"]) * x["dec_strict"] for x in e)
    ts = _inv_packed(ms) if known_inverses is None else _inv_packed_known(ms, tuple(known_inverses))
    us = [_pmm(t, x["vb"]) for t, x in zip(ts, e)]
    ws = [_pmm(t, x["kbe"]) for t, x in zip(ts, e)]
    attns = [_pnt(x["q"], x["k"]) * x["dec"] for x in e]
    ws_qs = [_pmm(jnp.concatenate([w, x["qe"]], axis=0), s) for w, x, s in zip(ws, e, ss)]
    v_news = [u - y[:c] for u, y in zip(us, ws_qs)]
    os = [y[c:] + _pmm(a, vn) for y, a, vn in zip(ws_qs, attns, v_news)]
    s_news = [s * x["sdecay"] + _ptn(x["kd"], vn) for s, x, vn in zip(ss, e, v_news)]
    if with_inverses:
        return list(zip(os, s_news)), list(ts)
    return list(zip(os, s_news))


def _lane_pick(x, h):
    lane = lax.broadcasted_iota(jnp.int32, x.shape, 1)
    return jnp.sum(jnp.where(lane == h, x, 0.0), axis=1, keepdims=True)


def _cum_log_decay(g):
    c = g.shape[0]
    lower = (lax.broadcasted_iota(jnp.int32, (c, c), 0) >= lax.broadcasted_iota(jnp.int32, (c, c), 1)).astype(f32)
    upper2 = (lax.broadcasted_iota(jnp.int32, (c, 128), 0) <= lax.broadcasted_iota(jnp.int32, (c, 128), 1) % c).astype(f32)
    return _dsel(lower, g, NN, 0), _dsel(g, upper2, TN, 1)


def _group_operands(gi, s_ref, q_ref, k_ref, v_ref, bv, gcv, gct_s):
    heads = [gi * GDN_PACK + u for u in range(GDN_PACK)]
    qk_off = pl.multiple_of(gi * 2 * GDN_D, 2 * GDN_D)
    v_off = pl.multiple_of(gi * GDN_PACK * GDN_D, GDN_PACK * GDN_D)
    return (s_ref[gi], q_ref[:, pl.ds(qk_off, 2 * GDN_D)].astype(f32), k_ref[:, pl.ds(qk_off, 2 * GDN_D)].astype(f32),
            v_ref[:, pl.ds(v_off, GDN_PACK * GDN_D)].astype(f32),
            [_lane_pick(bv, h) for h in heads], [_lane_pick(gcv, h) for h in heads],
            [gct_s[pl.ds(h, 1), :] for h in heads]), heads, qk_off, v_off


def _gdn_fwd(qn, kn, v, beta, g, *, n_real_chunks, name):
    lp = qn.shape[0]
    nchunk = lp // GDN_CHUNK
    C, D = GDN_CHUNK, GDN_D
    NG, SW = GDN_V_HEADS // GDN_PACK, GDN_PACK * GDN_D

    def body(q_ref, k_ref, v_ref, b_ref, g_ref, o_ref, st_ref, inv_ref, s_s, gc_s, gct_s):
        ci = pl.program_id(0)

        @pl.when(ci == 0)
        def _():
            s_s[...] = jnp.zeros_like(s_s)

        @pl.when(ci >= n_real_chunks)
        def _():
            o_ref[...] = jnp.zeros_like(o_ref)
            st_ref[...] = jnp.zeros_like(st_ref)
            inv_ref[...] = jnp.zeros_like(inv_ref)

        @pl.when(ci < n_real_chunks)
        def _():
            gc, gct = _cum_log_decay(g_ref[...])
            gc_s[...] = gc
            gct_s[...] = gct

            def some_groups(it, carry):
                ids = [it * GDN_FWD_INTERLEAVE + u for u in range(GDN_FWD_INTERLEAVE)]
                ops = [_group_operands(gi, s_s, q_ref, k_ref, v_ref, b_ref[...], gc_s[...], gct_s) for gi in ids]
                res, inverses = _gdn_groups([op[0] for op in ops], with_inverses=True)
                for gi, op, (o, s_new), t in zip(ids, ops, res, inverses):
                    st_ref[gi] = op[0][0]
                    inv_ref[gi] = t
                    s_s[gi] = s_new
                    o_ref[:, pl.ds(op[3], SW)] = o
                return carry

            lax.fori_loop(0, NG // GDN_FWD_INTERLEAVE, some_groups, 0)

    return pl.pallas_call(
        body, name=name, grid=(nchunk,),
        in_specs=[pl.BlockSpec((C, GDN_QK_W), lambda c: (c, 0)), pl.BlockSpec((C, GDN_QK_W), lambda c: (c, 0)),
                  pl.BlockSpec((C, GDN_V_W), lambda c: (c, 0)), pl.BlockSpec((C, 128), lambda c: (c, 0)),
                  pl.BlockSpec((C, 128), lambda c: (c, 0))],
        out_specs=[pl.BlockSpec((C, GDN_V_W), lambda c: (c, 0)),
                   pl.BlockSpec((None, NG, D, SW), lambda c: (c, 0, 0, 0)),
                   pl.BlockSpec((None, NG, C, GDN_PACK * C), lambda c: (c, 0, 0, 0))],
        out_shape=[jax.ShapeDtypeStruct((lp, GDN_V_W), f32), jax.ShapeDtypeStruct((nchunk, NG, D, SW), f32),
                   jax.ShapeDtypeStruct((nchunk, NG, C, GDN_PACK * C), f32)],
        scratch_shapes=[pltpu.VMEM((NG, D, SW), f32), pltpu.VMEM((C, 128), f32), pltpu.VMEM((128, 128), f32)],
        compiler_params=pltpu.CompilerParams(dimension_semantics=("arbitrary",)),
    )(qn, kn, v, beta, g)


def _gdn_bwd(qn, kn, v, beta, g, states, inverses, do, *, n_real_chunks, name):
    lp = qn.shape[0]
    nchunk = lp // GDN_CHUNK
    C, D = GDN_CHUNK, GDN_D
    NG, SW = GDN_V_HEADS // GDN_PACK, GDN_PACK * GDN_D
    rev = lambda i: (nchunk - 1 - i, 0)

    def body(q_ref, k_ref, v_ref, b_ref, g_ref, st_ref, inv_ref, do_ref,
             dq_ref, dk_ref, dv_ref, db_ref, dg_ref, ds_s, gc_s, gct_s, dgc_s, dgct_s, dbeta_s):
        step = pl.program_id(0)
        ci = nchunk - 1 - step

        @pl.when(step == 0)
        def _():
            ds_s[...] = jnp.zeros_like(ds_s)

        @pl.when(ci >= n_real_chunks)
        def _():
            for r in (dq_ref, dk_ref, dv_ref, db_ref, dg_ref):
                r[...] = jnp.zeros_like(r)

        @pl.when(ci < n_real_chunks)
        def _():
            gc, gct = _cum_log_decay(g_ref[...])
            gc_s[...] = gc
            gct_s[...] = gct
            dgc_s[...] = jnp.zeros_like(dgc_s)
            dgct_s[...] = jnp.zeros_like(dgct_s)
            dbeta_s[...] = jnp.zeros_like(dbeta_s)

            def some_groups(it, carry):
                ids = [it * GDN_BWD_INTERLEAVE + u for u in range(GDN_BWD_INTERLEAVE)]
                ops = [_group_operands(gi, st_ref, q_ref, k_ref, v_ref, b_ref[...], gc_s[...], gct_s) for gi in ids]
                cts = [(do_ref[:, pl.ds(op[3], SW)], ds_s[gi]) for gi, op in zip(ids, ops)]
                known = [inv_ref[gi] for gi in ids]
                _, vjp_fn = jax.vjp(lambda gs: _gdn_groups(gs, known_inverses=known), [op[0] for op in ops])
                (grads,) = vjp_fn(cts)
                lane = lax.broadcasted_iota(jnp.int32, (C, 128), 1)
                dbeta_acc, dgc_acc = dbeta_s[...], dgc_s[...]
                for gi, (_, heads, qk_off, v_off), (dsp, dq2, dk2, dv4, dbcols, dgcols, dgrows) in zip(ids, ops, grads):
                    ds_s[gi] = dsp
                    dq_ref[:, pl.ds(qk_off, 2 * D)] = dq2
                    dk_ref[:, pl.ds(qk_off, 2 * D)] = dk2
                    dv_ref[:, pl.ds(v_off, SW)] = dv4
                    for h, dbcol, dgcol, dgrow in zip(heads, dbcols, dgcols, dgrows):
                        dbeta_acc = dbeta_acc + jnp.where(lane == h, dbcol, 0.0)
                        dgc_acc = dgc_acc + jnp.where(lane == h, dgcol, 0.0)
                        dgct_s[pl.ds(h, 1), :] = dgrow
                dbeta_s[...] = dbeta_acc
                dgc_s[...] = dgc_acc
                return carry

            lax.fori_loop(0, NG // GDN_BWD_INTERLEAVE, some_groups, 0)
            fold = (lax.broadcasted_iota(jnp.int32, (128, C), 0) % C == lax.broadcasted_iota(jnp.int32, (128, C), 1)).astype(f32)
            eye = (lax.broadcasted_iota(jnp.int32, (128, 128), 0) == lax.broadcasted_iota(jnp.int32, (128, 128), 1)).astype(f32)
            dgc = dgc_s[...] + _dsel(_dsel(dgct_s[...], fold, NN, 1), eye, TN, 1)
            upper = (lax.broadcasted_iota(jnp.int32, (C, C), 0) <= lax.broadcasted_iota(jnp.int32, (C, C), 1)).astype(f32)
            dg_ref[...] = _dsel(upper, dgc, NN, 0)
            db_ref[...] = dbeta_s[...]

    return pl.pallas_call(
        body, name=name, grid=(nchunk,),
        in_specs=[pl.BlockSpec((C, GDN_QK_W), rev), pl.BlockSpec((C, GDN_QK_W), rev), pl.BlockSpec((C, GDN_V_W), rev),
                  pl.BlockSpec((C, 128), rev), pl.BlockSpec((C, 128), rev),
                  pl.BlockSpec((None, NG, D, SW), lambda i: (nchunk - 1 - i, 0, 0, 0)),
                  pl.BlockSpec((None, NG, C, GDN_PACK * C), lambda i: (nchunk - 1 - i, 0, 0, 0)), pl.BlockSpec((C, GDN_V_W), rev)],
        out_specs=[pl.BlockSpec((C, GDN_QK_W), rev), pl.BlockSpec((C, GDN_QK_W), rev), pl.BlockSpec((C, GDN_V_W), rev),
                   pl.BlockSpec((C, 128), rev), pl.BlockSpec((C, 128), rev)],
        out_shape=[jax.ShapeDtypeStruct((lp, GDN_QK_W), f32)] * 2 + [jax.ShapeDtypeStruct((lp, GDN_V_W), f32)]
        + [jax.ShapeDtypeStruct((lp, 128), f32)] * 2,
        scratch_shapes=[pltpu.VMEM((NG, D, SW), f32), pltpu.VMEM((C, 128), f32), pltpu.VMEM((128, 128), f32),
                        pltpu.VMEM((C, 128), f32), pltpu.VMEM((128, 128), f32), pltpu.VMEM((C, 128), f32)],
        compiler_params=pltpu.CompilerParams(dimension_semantics=("arbitrary",)),
    )(qn, kn, v, beta, g, states, inverses, do)


def _attention_bias(tq, tk):
    kk = lax.broadcasted_iota(jnp.int32, (tk, tq), 0)
    qq = lax.broadcasted_iota(jnp.int32, (tk, tq), 1)
    pad = jnp.where(kk < PAD_FRONT, NEG, 0.0)
    diag = [jnp.where(qq >= kk + d * tk, 0.0, NEG) for d in range(tq // tk)]
    return jnp.stack([pad] + diag + [jnp.minimum(pad, diag[0])]).astype(f32)


def _att_tiles(lp):
    tq = _pick(lp, (768, 512, 256))
    return tq, _pick(tq, ATT_KEY_TILES)


def _attention_fwd(q, kvu, kr, proj2, *, name):
    lp = q.shape[0]
    H = MLA_HEADS
    tq, tk = _att_tiles(lp)
    r = tq // tk

    def body(q_ref, kn_ref, kr_ref, v_ref, z_ref, bias_ref, o_ref, lse_ref, og_ref, m_s, l_s, acc_s, sa_s, sb_s):
        qi = pl.program_id(1)
        m_s[...] = jnp.full_like(m_s, NEG)
        l_s[...] = jnp.zeros_like(l_s)
        acc_s[...] = jnp.zeros_like(acc_s)

        def scores(ki):
            k0 = pl.multiple_of(ki * tk, tk)
            k = jnp.concatenate([kn_ref[pl.ds(k0, tk), :], kr_ref[pl.ds(k0, tk), :]], axis=1)
            return lax.dot_general(k, q_ref[...], (NT, ((), ())), preferred_element_type=f32)

        def consume(st, ki, mask):
            k0 = pl.multiple_of(ki * tk, tk)
            if mask is not None:
                st = st + bias_ref[mask]
            m_prev = m_s[...]
            m_new = jnp.maximum(m_prev, jnp.max(st, axis=0, keepdims=True))
            alpha = jnp.exp(m_prev - m_new)
            p = jnp.exp(st - m_new)
            l_s[...] = alpha * l_s[...] + jnp.sum(p, axis=0, keepdims=True)
            acc_s[...] = alpha * acc_s[...] + lax.dot_general(v_ref[pl.ds(k0, tk), :], p.astype(bf16), (TN, ((), ())),
                                                              preferred_element_type=f32)
            m_s[...] = m_new

        n_full = qi * r

        def chain(blocks):
            bufs = (sa_s, sb_s)
            for j, (ki, masked) in enumerate(blocks):
                if j + 1 < len(blocks):
                    bufs[(j + 1) % 2][...] = scores(blocks[j + 1][0])
                consume(bufs[j % 2][...], ki, masked)

        diagonal = [(n_full + d, 1 + d) for d in range(r)]

        @pl.when(qi == 0)
        def _():
            sa_s[...] = scores(0)
            chain([(0, r + 1)] + diagonal[1:])

        @pl.when(qi > 0)
        def _():
            sb_s[...] = scores(0)
            sa_s[...] = scores(1)
            consume(sb_s[...], 0, 0)
            n_pairs = (n_full - 1) // 2

            def two(pi, carry):
                ki = 1 + 2 * pi
                sb_s[...] = scores(ki + 1)
                consume(sa_s[...], ki, None)
                sa_s[...] = scores(ki + 2)
                consume(sb_s[...], ki + 1, None)
                return carry

            lax.fori_loop(0, n_pairs, two, 0)
            nxt = 1 + 2 * n_pairs

            @pl.when(nxt < n_full)
            def _():
                chain([(nxt, None)] + diagonal)

            @pl.when(nxt == n_full)
            def _():
                chain(diagonal)
        o = jnp.transpose(acc_s[...] / l_s[...])
        o_ref[...] = o
        og_ref[...] = (o * _silu(z_ref[...].astype(f32))).astype(og_ref.dtype)
        lse_ref[...] = m_s[...] + jnp.log(l_s[...])

    return pl.pallas_call(
        body, name=name, grid=(H, lp // tq),
        in_specs=[pl.BlockSpec((tq, MLA_QKP), lambda h, qi: (qi, h)),
                  pl.BlockSpec((lp, 128), lambda h, qi: (0, h)),
                  pl.BlockSpec((lp, 128), lambda h, qi: (0, 0)),
                  pl.BlockSpec((lp, 128), lambda h, qi: (0, H + h)),
                  pl.BlockSpec((tq, 128), lambda h, qi: (qi, MLA_Q_RANK // 128 + h)),
                  pl.BlockSpec((r + 2, tk, tq), lambda h, qi: (0, 0, 0))],
        out_specs=[pl.BlockSpec((tq, 128), lambda h, qi: (qi, h)),
                   pl.BlockSpec((None, 1, tq), lambda h, qi: (h, 0, qi)),
                   pl.BlockSpec((tq, 128), lambda h, qi: (qi, h))],
        out_shape=[jax.ShapeDtypeStruct((lp, MLA_V_W), f32), jax.ShapeDtypeStruct((H, 1, lp), f32),
                   jax.ShapeDtypeStruct((lp, MLA_V_W), bf16)],
        scratch_shapes=[pltpu.VMEM((1, tq), f32), pltpu.VMEM((1, tq), f32), pltpu.VMEM((128, tq), f32),
                        pltpu.VMEM((tk, tq), f32), pltpu.VMEM((tk, tq), f32)],
        compiler_params=pltpu.CompilerParams(dimension_semantics=("arbitrary", "arbitrary")),
    )(q, kvu, kr, kvu, proj2, _attention_bias(tq, tk))


def _gate_bwd(o, proj2, dgated, *, name):
    lp = o.shape[0]
    H, w = MLA_HEADS, 2 * MLA_V
    tq = _pick(lp, (768, 512, 256))

    def body(o_ref, z_ref, g_ref, do_ref, dz_ref, dl_ref):
        ov, z, g = o_ref[...], z_ref[...].astype(f32), g_ref[...]
        s = _sigmoid(z)
        do = (g * (z * s)).astype(bf16)
        do_ref[...] = do
        dz_ref[...] = (g * ov * (s * (1.0 + z * (1.0 - s)))).astype(dz_ref.dtype)
        prod = ov * do.astype(f32)
        for u in range(2):
            dl_ref[u] = jnp.sum(jnp.transpose(prod[:, u * MLA_V:(u + 1) * MLA_V]), axis=0, keepdims=True)

    blk = pl.BlockSpec((tq, w), lambda j, qi: (qi, j))
    return pl.pallas_call(
        body, name=name, grid=(H // 2, lp // tq),
        in_specs=[blk, pl.BlockSpec((tq, w), lambda j, qi: (qi, j + MLA_Q_RANK // w)), blk],
        out_specs=[blk, blk, pl.BlockSpec((2, 1, tq), lambda j, qi: (j, 0, qi))],
        out_shape=[jax.ShapeDtypeStruct((lp, MLA_V_W), bf16), jax.ShapeDtypeStruct((lp, MLA_V_W), bf16),
                   jax.ShapeDtypeStruct((H, 1, lp), f32)],
    )(o, proj2, dgated)


def _attention_bwd(q, kvu, kr, lse, delta, do, *, name):
    lp = q.shape[0]
    tq, t = _att_tiles(lp)
    H, nb = MLA_HEADS, lp // t
    r, nq = tq // t, lp // tq

    def body(q_ref, kn_ref, kr_ref, v_ref, lse_ref, dl_ref, do_ref, bias_ref, dq_ref, dkn_ref, dv_ref, dkr_ref, dk_s, dv_s,
             sa_s, da_s, sb_s, db_s):
        ki = pl.program_id(1)
        k = jnp.concatenate([kn_ref[...], kr_ref[...]], axis=1)
        vv = v_ref[...]
        dk_s[...] = jnp.zeros_like(dk_s)
        dv_s[...] = jnp.zeros_like(dv_s)

        def products(qi, s_ref, d_ref):
            q0 = pl.multiple_of(qi * tq, tq)
            s_ref[...] = lax.dot_general(k, q_ref[pl.ds(q0, tq), :], (NT, ((), ())), preferred_element_type=f32)
            d_ref[...] = lax.dot_general(vv, do_ref[pl.ds(q0, tq), :], (NT, ((), ())), preferred_element_type=f32)

        def accumulate(s_ref, d_ref, qi, mask, first):
            q0 = pl.multiple_of(qi * tq, tq)
            qv = q_ref[pl.ds(q0, tq), :]
            dob = do_ref[pl.ds(q0, tq), :]
            st = s_ref[...]
            if mask is not None:
                st = st + bias_ref[mask]
            p = jnp.exp(st - lse_ref[:, pl.ds(q0, tq)])
            dv_s[...] += jnp.dot(p.astype(bf16), dob, preferred_element_type=f32)
            ds = (p * (d_ref[...] - dl_ref[:, pl.ds(q0, tq)])).astype(bf16)
            dk_s[...] += jnp.dot(ds, qv, preferred_element_type=f32)
            dq = lax.dot_general(ds, k, (TN, ((), ())), preferred_element_type=f32)
            if first:
                dq_ref[pl.ds(q0, tq), :] = dq
            else:
                dq_ref[pl.ds(q0, tq), :] += dq

        def sweep(qd, first_mask, other_mask, first):
            last = nq - 1
            products(qd, sa_s, da_s)
            products(jnp.minimum(qd + 1, last), sb_s, db_s)
            accumulate(sa_s, da_s, qd, first_mask, first)
            n = last - qd

            def two(pi, carry):
                i = qd + 1 + 2 * pi
                products(i + 1, sa_s, da_s)
                accumulate(sb_s, db_s, i, other_mask, first)
                products(jnp.minimum(i + 2, last), sb_s, db_s)
                accumulate(sa_s, da_s, i + 1, other_mask, first)
                return carry

            lax.fori_loop(0, n // 2, two, 0)

            @pl.when(n % 2 == 1)
            def _():
                accumulate(sb_s, db_s, last, other_mask, first)

        @pl.when(ki == 0)
        def _():
            sweep(0, r + 1, 0, True)

        @pl.when(ki > 0)
        def _():
            sweep(ki // r, 1 + ki % r, None, False)

        dkn_ref[...] = dk_s[:, :128].astype(dkn_ref.dtype)
        dkr_ref[...] = dk_s[:, 128:]
        dv_ref[...] = dv_s[...].astype(dv_ref.dtype)

    return pl.pallas_call(
        body, name=name, grid=(H, nb),
        in_specs=[pl.BlockSpec((lp, MLA_QKP), lambda h, ki: (0, h)),
                  pl.BlockSpec((t, 128), lambda h, ki: (ki, h)),
                  pl.BlockSpec((t, 128), lambda h, ki: (ki, 0)),
                  pl.BlockSpec((t, 128), lambda h, ki: (ki, H + h)),
                  pl.BlockSpec((None, 1, lp), lambda h, ki: (h, 0, 0)),
                  pl.BlockSpec((None, 1, lp), lambda h, ki: (h, 0, 0)),
                  pl.BlockSpec((lp, 128), lambda h, ki: (0, h)),
                  pl.BlockSpec((r + 2, t, tq), lambda h, ki: (0, 0, 0))],
        out_specs=[pl.BlockSpec((lp, MLA_QKP), lambda h, ki: (0, h)),
                   pl.BlockSpec((t, 128), lambda h, ki: (ki, h)),
                   pl.BlockSpec((t, 128), lambda h, ki: (ki, h)),
                   pl.BlockSpec((t, 128), lambda h, ki: (ki, h))],
        out_shape=[jax.ShapeDtypeStruct((lp, H * MLA_QKP), f32), jax.ShapeDtypeStruct((lp, MLA_V_W), bf16),
                   jax.ShapeDtypeStruct((lp, MLA_V_W), bf16), jax.ShapeDtypeStruct((lp, MLA_V_W), f32)],
        scratch_shapes=[pltpu.VMEM((t, MLA_QKP), f32), pltpu.VMEM((t, 128), f32)] + [pltpu.VMEM((t, tq), f32)] * 4,
        compiler_params=pltpu.CompilerParams(dimension_semantics=("arbitrary", "arbitrary")),
    )(q, kvu, kr, kvu, lse, delta, do, _attention_bias(tq, t))


def _q_proj_fwd(c_q, wq2, cos_q, sin_q, *, name, tm=ROW_TILE):
    lp, rank = c_q.shape
    gw = Q_GROUP * MLA_QKP
    ng = wq2.shape[1] // (2 * gw)

    def body(c_ref, w_ref, cos_ref, sin_ref, o_ref):
        qq = jnp.dot(c_ref[...], w_ref[...], preferred_element_type=f32)
        o_ref[...] = _st_q_rope([qq], [cos_ref[...], sin_ref[...]], [])[0].astype(o_ref.dtype)

    tab = pl.BlockSpec((tm, MLA_QKP), lambda j, i: (i, 0))
    return pl.pallas_call(
        body, name=name, grid=(ng, lp // tm),
        in_specs=[pl.BlockSpec((tm, rank), lambda j, i: (i, 0)), pl.BlockSpec((rank, 2 * gw), lambda j, i: (0, j)), tab, tab],
        out_specs=pl.BlockSpec((tm, gw), lambda j, i: (i, j)),
        out_shape=jax.ShapeDtypeStruct((lp, ng * gw), bf16),
    )(c_q, wq2, cos_q, sin_q)


def _q_proj_bwd(dq, c_q, wq2, cos_q, sin_q, *, name, tm=ROW_TILE):
    lp, rank = c_q.shape
    gw = Q_GROUP * MLA_QKP
    ng = wq2.shape[1] // (2 * gw)

    def body(dq_ref, c_ref, w_ref, cos_ref, sin_ref, dc_ref, dw_ref):
        dqq = _st_q_rope_t([dq_ref[...]], [cos_ref[...], sin_ref[...]], [])[0].astype(bf16)
        dc_ref[...] = lax.dot_general(dqq, w_ref[...], (NT, ((), ())), preferred_element_type=f32)
        dw = lax.dot_general(c_ref[...], dqq, (TN, ((), ())), preferred_element_type=f32)

        @pl.when(pl.program_id(1) == 0)
        def _():
            dw_ref[...] = dw

        @pl.when(pl.program_id(1) > 0)
        def _():
            dw_ref[...] += dw

    tab = pl.BlockSpec((tm, MLA_QKP), lambda j, i: (i, 0))
    return pl.pallas_call(
        body, name=name, grid=(ng, lp // tm),
        in_specs=[pl.BlockSpec((tm, gw), lambda j, i: (i, j)), pl.BlockSpec((tm, rank), lambda j, i: (i, 0)),
                  pl.BlockSpec((rank, 2 * gw), lambda j, i: (0, j)), tab, tab],
        out_specs=[pl.BlockSpec((tm, rank), lambda j, i: (i, j)), pl.BlockSpec((rank, 2 * gw), lambda j, i: (0, j))],
        out_shape=[jax.ShapeDtypeStruct((lp, ng * rank), f32), jax.ShapeDtypeStruct(wq2.shape, f32)],
        compiler_params=pltpu.CompilerParams(dimension_semantics=("arbitrary", "arbitrary")),
    )(dq, c_q, wq2, cos_q, sin_q)


def _exchange_copies(x_ref, o_ref, send_sems, recv_sems, local_sem, gather):
    mx, my, mc = lax.axis_index("x"), lax.axis_index("y"), lax.axis_index("c")
    me = 4 * mx + 2 * my + mc
    own = pltpu.make_async_copy(x_ref if gather else x_ref.at[me], o_ref.at[me], local_sem)
    sends, arrivals = [], []
    for k in range(1, N_DEV):
        px = 1 - mx if k & 4 else mx
        py = 1 - my if k & 2 else my
        pc = 1 - mc if k & 1 else mc
        peer = 4 * px + 2 * py + pc
        sends.append(pltpu.make_async_remote_copy(
            src_ref=x_ref if gather else x_ref.at[peer], dst_ref=o_ref.at[me],
            send_sem=send_sems.at[k - 1], recv_sem=recv_sems.at[k - 1],
            device_id=(px, py, pc), device_id_type=MESH_ID))
        arrivals.append(pltpu.make_async_remote_copy(
            src_ref=o_ref.at[peer], dst_ref=o_ref.at[peer],
            send_sem=send_sems.at[k - 1], recv_sem=recv_sems.at[k - 1],
            device_id=(mx, my, mc), device_id_type=MESH_ID))
    return own, sends, arrivals


def _exchange_start(copies):
    own, sends, _ = copies
    own.start()
    for cp in sends:
        cp.start()


def _exchange_wait(copies):
    own, sends, arrivals = copies
    for cp in arrivals:
        cp.wait_recv()
    for cp in sends:
        cp.wait_send()
    own.wait()


_EXCHANGE_SCRATCH = [pltpu.SemaphoreType.DMA((N_DEV - 1,)), pltpu.SemaphoreType.DMA((N_DEV - 1,)), pltpu.SemaphoreType.DMA]


def _exchange(x, *, gather, name):
    blk = x.shape if gather else x.shape[1:]

    def body(x_ref, o_ref, send_sems, recv_sems, local_sem):
        copies = _exchange_copies(x_ref, o_ref, send_sems, recv_sems, local_sem, gather)
        _exchange_start(copies)
        _exchange_wait(copies)

    return pl.pallas_call(
        body, name=name,
        in_specs=[pl.BlockSpec(memory_space=pltpu.HBM)], out_specs=pl.BlockSpec(memory_space=pltpu.HBM),
        out_shape=jax.ShapeDtypeStruct((N_DEV,) + tuple(blk), x.dtype),
        scratch_shapes=list(_EXCHANGE_SCRATCH),
    )(x)


def _reduce_adamw(parts, w, m, v, *, name):
    r = w.shape[0]
    tr = max(d for d in range(16, 3201, 16) if r % d == 0)

    def body(p_ref, w_ref, m_ref, v_ref, g_ref, d_ref, nm_ref, nv_ref):
        g = p_ref[0].astype(f32)
        for s in range(1, N_DEV):
            g = g + p_ref[s].astype(f32)
        mm = ADAM_B1 * m_ref[...] + (1.0 - ADAM_B1) * g
        vv = ADAM_B2 * v_ref[...] + (1.0 - ADAM_B2) * (g * g)
        m_hat = mm / (1.0 - ADAM_B1 ** ADAM_STEP)
        v_hat = vv / (1.0 - ADAM_B2 ** ADAM_STEP)
        g_ref[...] = g
        d_ref[...] = -ADAM_LR * (m_hat / (jnp.sqrt(v_hat) + ADAM_EPS) + ADAM_WD * w_ref[...])
        nm_ref[...] = mm
        nv_ref[...] = vv

    spec = pl.BlockSpec((tr, 128), lambda i: (i, 0))
    return pl.pallas_call(
        body, name=name, grid=(r // tr,),
        in_specs=[pl.BlockSpec((N_DEV, tr, 128), lambda i: (0, i, 0)), spec, spec, spec],
        out_specs=[spec] * 4, out_shape=[jax.ShapeDtypeStruct((r, 128), f32)] * 4,
    )(parts, w, m, v)


_SHARDED = ("gdn_w_in", "gdn_w_out", "kv_w_down", "kv_w_up", "mla_w_in", "mla_w_q_up", "mla_w_out", "meta_tokens", "gdn_conv_w")
_COL_SHARDED = {"gdn_w_in", "kv_w_up", "mla_w_in", "mla_w_q_up", "meta_tokens", "gdn_conv_w"}
_GATHER_FIRST = ("gdn_w_in",)
_GATHER_F32 = ("meta_tokens", "gdn_conv_w")
_GATHER_REST = ("gdn_w_out", "kv_w_down", "kv_w_up", "mla_w_in", "mla_w_q_up", "mla_w_out")
_SCATTER_EARLY = ("gdn_w_out", "kv_w_down", "kv_w_up", "mla_w_in", "mla_w_q_up", "mla_w_out")
_SCATTER_LATE = ("gdn_w_in", "meta_tokens", "gdn_conv_w")
_REPLICATED = ("pre_norm", "post_norm", "gdn_a_log", "gdn_dt_bias", "gdn_out_norm", "kv_norm", "kv_latent_norm",
               "mla_q_latent_norm")


def _rows128(a):
    flat = a.reshape(-1)
    pad = (-flat.shape[0]) % 128
    if pad:
        flat = jnp.pad(flat, (0, pad))
    return flat.reshape(-1, 128)


def _pack(arrs, row_multiple):
    parts = [_rows128(a) for a in arrs]
    buf = jnp.concatenate(parts, axis=0)
    pad = (-buf.shape[0]) % row_multiple
    if pad:
        buf = jnp.pad(buf, ((0, pad), (0, 0)))
    return buf


def _unpack(buf, shapes):
    out, r = [], 0
    for shp in shapes:
        n = math.prod(shp)
        rows = -(-n // 128)
        out.append(buf[r:r + rows].reshape(-1)[:n].reshape(shp))
        r += rows
    return out


def _unshard(g, full_shape, col):
    if col:
        return jnp.transpose(g, (1, 0, 2)).reshape(full_shape)
    return g.reshape(full_shape)


def _to_shards(a, col):
    r, c = a.shape
    if col:
        return jnp.transpose(a.reshape(r, N_DEV, c // N_DEV), (1, 0, 2))
    return a.reshape(N_DEV, r // N_DEV, c)


def _pad_cols(a, width):
    return jnp.pad(a, ((0, 0), (0, width - a.shape[1])))


def _rope_tables(lp):
    inv = ROPE_THETA ** (-jnp.arange(0, MLA_ROPE, 2, dtype=f32) / MLA_ROPE)
    pos = (jnp.arange(lp, dtype=jnp.int32) - PAD_FRONT).astype(f32)
    ang = pos[:, None] * inv[None, :]
    cos, sin = jnp.cos(ang), jnp.sin(ang)
    z = jnp.zeros((lp, 64), f32)
    return jnp.concatenate([cos, cos, z], axis=1), jnp.concatenate([-sin, sin, z], axis=1)


def kernel(x, meta_tokens, pre_norm, post_norm, gdn_w_in, gdn_conv_w, gdn_a_log, gdn_dt_bias, gdn_out_norm, gdn_w_out, kv_norm, kv_w_down, kv_latent_norm, kv_w_up, mla_w_in, mla_q_latent_norm, mla_w_q_up, mla_w_out, loss_target, m_meta_tokens, m_pre_norm, m_post_norm, m_gdn_w_in, m_gdn_conv_w, m_gdn_a_log, m_gdn_dt_bias, m_gdn_out_norm, m_gdn_w_out, m_kv_norm, m_kv_w_down, m_kv_latent_norm, m_kv_w_up, m_mla_w_in, m_mla_q_latent_norm, m_mla_w_q_up, m_mla_w_out, v_meta_tokens, v_pre_norm, v_post_norm, v_gdn_w_in, v_gdn_conv_w, v_gdn_a_log, v_gdn_dt_bias, v_gdn_out_norm, v_gdn_w_out, v_kv_norm, v_kv_w_down, v_kv_latent_norm, v_kv_w_up, v_mla_w_in, v_mla_q_latent_norm, v_mla_w_q_up, v_mla_w_out):
    W = dict(meta_tokens=meta_tokens, pre_norm=pre_norm, post_norm=post_norm, gdn_w_in=gdn_w_in, gdn_conv_w=gdn_conv_w,
             gdn_a_log=gdn_a_log, gdn_dt_bias=gdn_dt_bias, gdn_out_norm=gdn_out_norm, gdn_w_out=gdn_w_out, kv_norm=kv_norm,
             kv_w_down=kv_w_down, kv_latent_norm=kv_latent_norm, kv_w_up=kv_w_up, mla_w_in=mla_w_in,
             mla_q_latent_norm=mla_q_latent_norm, mla_w_q_up=mla_w_q_up, mla_w_out=mla_w_out)
    M = dict(meta_tokens=m_meta_tokens, pre_norm=m_pre_norm, post_norm=m_post_norm, gdn_w_in=m_gdn_w_in, gdn_conv_w=m_gdn_conv_w,
             gdn_a_log=m_gdn_a_log, gdn_dt_bias=m_gdn_dt_bias, gdn_out_norm=m_gdn_out_norm, gdn_w_out=m_gdn_w_out, kv_norm=m_kv_norm,
             kv_w_down=m_kv_w_down, kv_latent_norm=m_kv_latent_norm, kv_w_up=m_kv_w_up, mla_w_in=m_mla_w_in,
             mla_q_latent_norm=m_mla_q_latent_norm, mla_w_q_up=m_mla_w_q_up, mla_w_out=m_mla_w_out)
    V = dict(meta_tokens=v_meta_tokens, pre_norm=v_pre_norm, post_norm=v_post_norm, gdn_w_in=v_gdn_w_in, gdn_conv_w=v_gdn_conv_w,
             gdn_a_log=v_gdn_a_log, gdn_dt_bias=v_gdn_dt_bias, gdn_out_norm=v_gdn_out_norm, gdn_w_out=v_gdn_w_out, kv_norm=v_kv_norm,
             kv_w_down=v_kv_w_down, kv_latent_norm=v_kv_latent_norm, kv_w_up=v_kv_w_up, mla_w_in=v_mla_w_in,
             mla_q_latent_norm=v_mla_q_latent_norm, mla_w_q_up=v_mla_w_q_up, mla_w_out=v_mla_w_out)
    order = list(W)

    n_tok = x.shape[1]
    assert n_tok % GDN_CHUNK == 0
    n_real = ROW0 + n_tok
    lp = -(-n_real // ROW_TILE) * ROW_TILE
    n_real_chunks = n_real // GDN_CHUNK

    shard2d = {n: W[n].reshape(W[n].shape[-2:]) for n in _SHARDED}
    full_shape = {n: ((s.shape[0], s.shape[1] * N_DEV) if n in _COL_SHARDED else (s.shape[0] * N_DEV, s.shape[1]))
                  for n, s in shard2d.items()}
    full = {}

    def unpack_gathered(names, buf):
        r = 0
        for n in names:
            shp = shard2d[n].shape
            rows = math.prod(shp) // 128
            blocks = buf[:, r:r + rows].reshape((N_DEV,) + shp)
            full[n] = _unshard(blocks, full_shape[n], n in _COL_SHARDED)
            r += rows

    unpack_gathered(_GATHER_FIRST, _exchange(_pack([shard2d[n].astype(bf16) for n in _GATHER_FIRST], 16), gather=True,
                                             name="gather_w_in"))
    unpack_gathered(_GATHER_F32, _exchange(_pack([shard2d[n] for n in _GATHER_F32], 8), gather=True, name="gather_meta_conv"))
    rest_shards = _pack([shard2d[n].astype(bf16) for n in _GATHER_REST], 16)

    h0 = jnp.concatenate([jnp.zeros((PAD_FRONT, D_MODEL), f32), full["meta_tokens"], x[0],
                          jnp.zeros((lp - n_real, D_MODEL), f32)], axis=0)
    tgt = jnp.concatenate([jnp.zeros((ROW0, D_MODEL), f32), loss_target[0], jnp.zeros((lp - n_real, D_MODEL), f32)], axis=0)
    cos_k, sin_k = _rope_tables(lp)
    one = jnp.ones((lp, 128), f32)
    cos_q = jnp.concatenate([one, cos_k], axis=1)
    sin_q = jnp.concatenate([jnp.zeros((lp, 128), f32), sin_k], axis=1)

    w_in = full["gdn_w_in"]
    s1 = GDN_CONV_W + GDN_V_W
    w_in_p = jnp.concatenate([w_in[:, :s1], _pad_cols(w_in[:, s1:s1 + 16], 128), _pad_cols(w_in[:, s1 + 16:], 128)], axis=1)
    pre0, pre1 = pre_norm[0:1], pre_norm[1:2]
    post0, post1 = post_norm[0:1], post_norm[1:2]
    (hn0,) = _rowwise(_st_prenorm, [(h0, None, 0)], [], [pre0], [(D_MODEL, bf16, None)], name="f_prenorm0")
    proj, g_rest = _matmul(hn0, w_in_p, out_dtype=bf16, name="f_gdn_in", side=(rest_shards, True))
    unpack_gathered(_GATHER_REST, g_rest)

    wd = full["kv_w_down"]
    zc = jnp.zeros((D_MODEL, 64), bf16)
    wd2 = jnp.concatenate([wd, zc, jnp.zeros((D_MODEL, 128), bf16), wd[:, 160:192], wd[:, 128:160], zc], axis=1)
    wup_p = jnp.transpose(full["kv_w_up"].reshape(MLA_KV_RANK, MLA_HEADS, 2, 128), (0, 2, 1, 3)).reshape(MLA_KV_RANK, 2 * MLA_V_W)
    wq = full["mla_w_q_up"].reshape(MLA_Q_RANK, MLA_HEADS, MLA_QK)
    zq64 = jnp.zeros((MLA_Q_RANK, MLA_HEADS, 64), bf16)
    wq_plain = jnp.concatenate([wq, zq64], axis=2).reshape(MLA_Q_RANK, MLA_HEADS * MLA_QKP)
    wq_swap = jnp.concatenate([jnp.zeros((MLA_Q_RANK, MLA_HEADS, 128), bf16), wq[:, :, 160:192], wq[:, :, 128:160], zq64],
                              axis=2).reshape(MLA_Q_RANK, MLA_HEADS * MLA_QKP)
    q_half = Q_GROUP * MLA_QKP
    wq2 = jnp.concatenate([wq_plain[:, :q_half], wq_swap[:, :q_half], wq_plain[:, q_half:], wq_swap[:, q_half:]], axis=1)
    w_mla_in, w_gdn_out, w_mla_out = full["mla_w_in"], full["gdn_w_out"], full["mla_w_out"]
    conv_w = full["gdn_conv_w"]
    alog_p, dtb_p = _pad_cols(gdn_a_log, 128), _pad_cols(gdn_dt_bias, 128)
    kvn, kvln = kv_norm.reshape(1, -1), kv_latent_norm.reshape(1, -1)

    conv = _conv_fwd(proj, conv_w, col_blocks=GDN_CONV_W // CONV_BC, name="f_conv")
    (qn,) = _rowwise(_st_gdn_q, [(conv, GDN_QK_W, 0)], [], [], [(GDN_QK_W, bf16, GDN_QK_W)], heads=GDN_QK_HEADS, name="f_gdn_q")
    (kn,) = _rowwise(_st_gdn_k, [(conv, GDN_QK_W, 1)], [], [], [(GDN_QK_W, bf16, GDN_QK_W)], heads=GDN_QK_HEADS, name="f_gdn_k")
    (vv,) = _rowwise(_st_gdn_v, [(conv, GDN_V_W, 1)], [], [], [(GDN_V_W, bf16, GDN_V_W)], name="f_gdn_v")
    gate_rows = [(proj, 128, s1 // 128), (proj, 128, s1 // 128 + 1)]
    beta, gdec = _rowwise(_st_gdn_gate, gate_rows, [], [alog_p, dtb_p], [(128, f32, None)] * 2, name="f_gdn_gate")
    o_gdn, states, inverses = _gdn_fwd(qn, kn, vv, beta, gdec, n_real_chunks=n_real_chunks, name="f_gdn")
    out_rows = [(o_gdn, GDN_V_W, 0), (proj, GDN_V_W, GDN_CONV_W // GDN_V_W)]
    (og,) = _rowwise(_st_gdn_out, out_rows, [], [gdn_out_norm], [(GDN_V_W, bf16, GDN_V_W)], heads=GDN_V_HEADS, name="f_gdn_out")
    y0 = _matmul(og, w_gdn_out, name="f_gdn_wout")
    mid_rows = [(h0, None, 0), (y0, None, 0)]
    h1, hn1, hkv = _rowwise(_st_mid, mid_rows, [], [post0, pre1, kvn],
                            [(D_MODEL, f32, None), (D_MODEL, bf16, None), (D_MODEL, bf16, None)], name="f_mid")
    ckr = _matmul(hkv, wd2, name="f_kv_down")
    proj2 = _matmul(hn1, w_mla_in, out_dtype=bf16, name="f_mla_in")
    lat_rows = [(ckr, None, 0), (proj2, MLA_Q_RANK, 0)]
    lat_nd = [(cos_k, None, 0), (sin_k, None, 0)]
    c_kv, k_rope, c_q = _rowwise(_st_latent, lat_rows, lat_nd, [kvln, mla_q_latent_norm],
                                 [(128, bf16, None), (128, bf16, None), (MLA_Q_RANK, bf16, None)], name="f_latent")
    kvu = _matmul(c_kv, wup_p, out_dtype=bf16, name="f_kv_up")
    q_att = _q_proj_fwd(c_q, wq2, cos_q, sin_q, name="f_q_proj")
    o_att, lse, og2 = _attention_fwd(q_att, kvu, k_rope, proj2, name="f_attention")
    y1 = _matmul(og2, w_mla_out, name="f_mla_wout")
    st_loss = _make_st_loss(n_tok)
    loss_rows_in = [(h1, None, 0), (y1, None, 0)]
    (loss_rows,) = _rowwise(st_loss, loss_rows_in, [(tgt, None, 0)], [post1], [(1, f32, None)], name="f_loss")
    loss_here = jnp.sum(loss_rows)

    ones_ct = jnp.ones((lp, 1), f32)
    (dh1_a, dy1), (dpost1,) = _rowwise_vjp(st_loss, loss_rows_in, [(tgt, None, 0)], [post1], [(ones_ct, None, 0)], name="b_loss")
    dog2 = _matmul(dy1, w_mla_out, tb=True, name="b_mla_wout_x")
    dw_mla_out = _matmul(og2, dy1, ta=True, name="b_mla_wout_w")
    do_att, dz2, delta = _gate_bwd(o_att, proj2, dog2, name="b_mla_gate")
    dq_att, dkn, dvv, dkr_h = _attention_bwd(q_att, kvu, k_rope, lse, delta, do_att, name="b_attention")
    dc_q, dwq2 = _q_proj_bwd(dq_att, c_q, wq2, cos_q, sin_q, name="b_q_proj")
    dkvu = jnp.concatenate([dkn, dvv], axis=1)
    dc_kv = _matmul(dkvu, wup_p, tb=True, name="b_kv_up_x")
    dwup_p = _matmul(c_kv, dkvu, ta=True, name="b_kv_up_w")

    def lat_ct(cv):
        dkr = cv[1][:, 0:128]
        for h in range(1, MLA_HEADS):
            dkr = dkr + cv[1][:, h * 128:(h + 1) * 128]
        return [cv[0], dkr, cv[2][:, :MLA_Q_RANK] + cv[2][:, MLA_Q_RANK:]]

    (dckr, dcq_pre), (dkvln, dqln) = _rowwise_vjp(
        _st_latent, lat_rows, lat_nd, [kvln, mla_q_latent_norm],
        [(dc_kv, None, 0), (dkr_h, None, 0), (dc_q, None, 0)], ct_pre=lat_ct, name="b_latent", grad_dtypes=[bf16, bf16])
    dproj2 = jnp.concatenate([dcq_pre, dz2], axis=1)
    dhn1 = _matmul(dproj2, w_mla_in, tb=True, name="b_mla_in_x")
    dw_mla_in = _matmul(hn1, dproj2, ta=True, name="b_mla_in_w")
    dhkv = _matmul(dckr, wd2, tb=True, name="b_kv_down_x")
    dwd2 = _matmul(hkv, dckr, ta=True, name="b_kv_down_w")
    (dh0_a, dy0), (dpost0, dpre1, dkvn) = _rowwise_vjp(
        _st_mid, mid_rows, [], [post0, pre1, kvn], [(dh1_a, None, 0), (dhn1, None, 0), (dhkv, None, 0)], name="b_mid")
    dog = _matmul(dy0, w_gdn_out, tb=True, name="b_gdn_wout_x")
    dw_gdn_out = _matmul(og, dy0, ta=True, name="b_gdn_wout_w")
    (do_gdn, dz), (dout_norm,) = _rowwise_vjp(_st_gdn_out, out_rows, [], [gdn_out_norm], [(dog, GDN_V_W, 0)], heads=GDN_V_HEADS,
                                              name="b_gdn_out", grad_dtypes=[f32, bf16])
    dq_g, dk_g, dv_g, dbeta, dgdec = _gdn_bwd(qn, kn, vv, beta, gdec, states, inverses, do_gdn, n_real_chunks=n_real_chunks,
                                              name="b_gdn")
    (db_col, da_col), (dalog_p, ddtb_p) = _rowwise_vjp(
        _st_gdn_gate, gate_rows, [], [alog_p, dtb_p], [(dbeta, None, 0), (dgdec, None, 0)], name="b_gdn_gate",
        grad_dtypes=[bf16, bf16])
    (dconv_q,), _ = _rowwise_vjp(_st_gdn_q, [(conv, GDN_QK_W, 0)], [], [], [(dq_g, GDN_QK_W, 0)], heads=GDN_QK_HEADS, name="b_gdn_q")
    (dconv_k,), _ = _rowwise_vjp(_st_gdn_k, [(conv, GDN_QK_W, 1)], [], [], [(dk_g, GDN_QK_W, 0)], heads=GDN_QK_HEADS, name="b_gdn_k")
    (dconv_v,), _ = _rowwise_vjp(_st_gdn_v, [(conv, GDN_V_W, 1)], [], [], [(dv_g, GDN_V_W, 0)], name="b_gdn_v")
    nq_b = GDN_QK_W // CONV_BC
    dpre_q, dcw_q = _conv_bwd(dconv_q, proj, conv_w, x_off=0, w_off=0, name="b_conv_q")
    dpre_k, dcw_k = _conv_bwd(dconv_k, proj, conv_w, x_off=nq_b, w_off=nq_b, name="b_conv_k")
    dpre_v, dcw_v = _conv_bwd(dconv_v, proj, conv_w, x_off=2 * nq_b, w_off=2 * nq_b, name="b_conv_v")
    dproj = jnp.concatenate([dpre_q, dpre_k, dpre_v, dz, db_col, da_col], axis=1)
    G = {}
    G["kv_w_down"] = jnp.concatenate([dwd2[:, :128], dwd2[:, 128:160] + dwd2[:, 416:448], dwd2[:, 160:192] + dwd2[:, 384:416]], axis=1)
    G["kv_w_up"] = jnp.transpose(dwup_p.reshape(MLA_KV_RANK, 2, MLA_HEADS, 128), (0, 2, 1, 3)).reshape(MLA_KV_RANK, 2 * MLA_V_W)
    G["mla_w_in"] = dw_mla_in
    dq4 = dwq2.reshape(MLA_Q_RANK, 2, 2, Q_GROUP, MLA_QKP)
    dqp = dq4[:, :, 0].reshape(MLA_Q_RANK, MLA_HEADS, MLA_QKP)
    dqs = dq4[:, :, 1].reshape(MLA_Q_RANK, MLA_HEADS, MLA_QKP)
    G["mla_w_q_up"] = jnp.concatenate([dqp[:, :, :128], dqp[:, :, 128:160] + dqs[:, :, 160:192],
                                       dqp[:, :, 160:192] + dqs[:, :, 128:160]], axis=2).reshape(MLA_Q_RANK, MLA_HEADS * MLA_QK)
    G["mla_w_out"] = dw_mla_out
    G["gdn_w_out"] = dw_gdn_out

    def shards_to_send(names):
        return jnp.concatenate([_to_shards(G[n], n in _COL_SHARDED).reshape(N_DEV, -1, 128).astype(bf16) for n in names], axis=1)

    dhn0, parts_early = _matmul(dproj, w_in_p, tb=True, name="b_gdn_in_x", side=(shards_to_send(_SCATTER_EARLY), False))
    dw_in_p = _matmul(hn0, dproj, ta=True, name="b_gdn_in_w")
    (dh0,), (dpre0,) = _rowwise_vjp(_st_prenorm, [(h0, None, 0)], [], [pre0], [(dhn0, None, 0)], extra=[dh0_a], name="b_prenorm0")

    grad_x = dh0[ROW0:n_real][None]
    G["meta_tokens"] = dh0[PAD_FRONT:ROW0]
    G["gdn_w_in"] = jnp.concatenate([dw_in_p[:, :s1 + 16], dw_in_p[:, s1 + 128:s1 + 144]], axis=1)
    G["gdn_conv_w"] = jnp.concatenate([dcw_q, dcw_k, dcw_v], axis=1)
    G["pre_norm"] = jnp.concatenate([dpre0, dpre1], axis=0)
    G["post_norm"] = jnp.concatenate([dpost0, dpost1], axis=0)
    G["gdn_a_log"] = dalog_p[:, :GDN_V_HEADS]
    G["gdn_dt_bias"] = ddtb_p[:, :GDN_V_HEADS]
    G["gdn_out_norm"] = dout_norm
    G["kv_norm"] = dkvn.reshape(-1)
    G["kv_latent_norm"] = dkvln.reshape(-1)
    G["mla_q_latent_norm"] = dqln

    parts_late = _exchange(shards_to_send(_SCATTER_LATE), gather=False, name="scatter_grads")
    G["loss"] = loss_here.reshape(1, 1)
    for d in (W, M, V):
        d["loss"] = jnp.zeros((1, 1), f32)
    replicated = _REPLICATED + ("loss",)
    parts_r = _exchange(_pack([G[n] for n in replicated], 8), gather=True, name="gather_small_grads")
    outs = {}
    for names, parts, tag in ((_SCATTER_EARLY, parts_early, "early"), (_SCATTER_LATE, parts_late, "late"),
                              (replicated, parts_r, "replicated")):
        w_p, m_p, v_p = (_pack([d[n] for n in names], 8) for d in (W, M, V))
        res = _reduce_adamw(parts, w_p, m_p, v_p, name="adamw_" + tag)
        for kind, buf in zip(("grad", "delta", "new_m", "new_v"), res):
            for n, a in zip(names, _unpack(buf, [W[n].shape for n in names])):
                outs[kind, n] = a
    loss = outs["grad", "loss"].reshape(())
    return (loss, grad_x, *[outs[k, n] for k in ("grad", "delta", "new_m", "new_v") for n in order])
```

```python
import functools
import math

import jax
import jax.numpy as jnp
from jax import lax
from jax.experimental import pallas as pl
from jax.experimental.pallas import tpu as pltpu

f32, bf16 = jnp.float32, jnp.bfloat16
MESH_ID = pl.DeviceIdType.MESH

N_DEV = 8
D_MODEL = 1024
N_META = 16
NORM_EPS = 1e-6
PAD_FRONT = 48
ROW0 = PAD_FRONT + N_META
GDN_QK_HEADS, GDN_V_HEADS, GDN_D = 8, 16, 128
GDN_CHUNK = 64
GDN_QK_W, GDN_V_W = GDN_QK_HEADS * GDN_D, GDN_V_HEADS * GDN_D
GDN_CONV_W = 2 * GDN_QK_W + GDN_V_W
GDN_IN_W = GDN_CONV_W + GDN_V_W + 2 * GDN_V_HEADS
GDN_IN_WP = GDN_CONV_W + GDN_V_W + 2 * 128
MLA_HEADS, MLA_NOPE, MLA_ROPE, MLA_V = 16, 128, 64, 128
MLA_Q_RANK, MLA_KV_RANK = 256, 128
MLA_QK = MLA_NOPE + MLA_ROPE
MLA_QKP = 256
MLA_V_W = MLA_HEADS * MLA_V
ROPE_THETA = 10000.0
NEG = -1e30
ROW_TILE = 256
ATT_KEY_TILES = (384, 256)

ADAM_LR, ADAM_B1, ADAM_B2, ADAM_EPS, ADAM_WD, ADAM_STEP = 0.001, 0.9, 0.999, 1e-08, 0.01, 10

NN = ((1,), (0,))
NT = ((1,), (1,))
TN = ((0,), (0,))


def _pick(dim, prefs):
    for p in prefs:
        if dim % p == 0:
            return p
    return dim


def _dlo(a, b, dims):
    return lax.dot_general(a.astype(bf16), b.astype(bf16), (dims, ((), ())), preferred_element_type=f32)


def _dsel(a, b, dims, selector):
    x = b if selector == 0 else a
    hi = x.astype(bf16)
    rest = x - hi.astype(f32)
    mid = rest.astype(bf16)
    low = (rest - mid.astype(f32)).astype(bf16)
    sel = (a if selector == 0 else b).astype(bf16)
    d = (lambda p: lax.dot_general(sel, p, (dims, ((), ())), preferred_element_type=f32)) if selector == 0 else \
        (lambda p: lax.dot_general(p, sel, (dims, ((), ())), preferred_element_type=f32))
    return d(hi) + (d(mid) + d(low))


def _matmul(a, b, *, ta=False, tb=False, out_dtype=f32, name, side=None):
    assert not (ta and tb)
    if ta:
        kdim, m = a.shape
    else:
        m, kdim = a.shape
    n = b.shape[0] if tb else b.shape[1]
    assert (b.shape[1] if tb else b.shape[0]) == kdim
    tm = _pick(m, (1024, 768, 512, 384, 256, 128))
    tn = _pick(n, (1024, 768, 640, 512, 256, 128))
    tk = _pick(kdim, (1024, 768, 640, 512, 256, 128))
    nk = kdim // tk
    dims = TN if ta else (NT if tb else NN)

    grid = (m // tm, n // tn, nk)

    def product(a_ref, b_ref, o_ref, acc_ref):
        k = pl.program_id(2)

        @pl.when(k == 0)
        def _():
            acc_ref[...] = jnp.zeros_like(acc_ref)

        acc_ref[...] += lax.dot_general(a_ref[...].astype(bf16), b_ref[...].astype(bf16), (dims, ((), ())),
                                        preferred_element_type=f32)

        @pl.when(k == nk - 1)
        def _():
            o_ref[...] = acc_ref[...].astype(o_ref.dtype)

    a_spec = pl.BlockSpec((tk, tm), lambda i, j, k: (k, i)) if ta else pl.BlockSpec((tm, tk), lambda i, j, k: (i, k))
    b_spec = pl.BlockSpec((tn, tk), lambda i, j, k: (j, k)) if tb else pl.BlockSpec((tk, tn), lambda i, j, k: (k, j))
    o_spec = pl.BlockSpec((tm, tn), lambda i, j, k: (i, j))
    o_shape = jax.ShapeDtypeStruct((m, n), out_dtype)
    if side is None:
        def body(a_ref, b_ref, o_ref, acc_ref):
            product(a_ref, b_ref, o_ref, acc_ref)

        return pl.pallas_call(
            body, name=name, grid=grid, in_specs=[a_spec, b_spec], out_specs=o_spec, out_shape=o_shape,
            scratch_shapes=[pltpu.VMEM((tm, tn), f32)],
            compiler_params=pltpu.CompilerParams(dimension_semantics=("parallel", "parallel", "arbitrary")),
        )(a, b)

    x, gather = side
    blk = x.shape if gather else x.shape[1:]

    def body_with_exchange(a_ref, b_ref, x_ref, o_ref, xo_ref, acc_ref, send_sems, recv_sems, local_sem):
        step = (pl.program_id(0) * grid[1] + pl.program_id(1)) * grid[2] + pl.program_id(2)
        copies = _exchange_copies(x_ref, xo_ref, send_sems, recv_sems, local_sem, gather)

        @pl.when(step == 0)
        def _():
            _exchange_start(copies)

        product(a_ref, b_ref, o_ref, acc_ref)

        @pl.when(step == grid[0] * grid[1] * grid[2] - 1)
        def _():
            _exchange_wait(copies)

    hbm = pl.BlockSpec(memory_space=pltpu.HBM)
    return pl.pallas_call(
        body_with_exchange, name=name, grid=grid, in_specs=[a_spec, b_spec, hbm], out_specs=[o_spec, hbm],
        out_shape=[o_shape, jax.ShapeDtypeStruct((N_DEV,) + tuple(blk), x.dtype)],
        scratch_shapes=[pltpu.VMEM((tm, tn), f32)] + list(_EXCHANGE_SCRATCH),
        compiler_params=pltpu.CompilerParams(dimension_semantics=("arbitrary", "arbitrary", "arbitrary")),
    )(a, b, x)


def _row_spec(item, tr):
    a, bc, off = item
    if bc is None:
        return pl.BlockSpec((tr, a.shape[1]), lambda i, j: (i, 0))
    return pl.BlockSpec((tr, bc), lambda i, j, off=off: (i, j + off))


def _param_spec(p):
    return pl.BlockSpec(p.shape, lambda i, j: (0, 0))


def _row_tile(lp, items):
    widest = max(a.shape[1] if bc is None else bc for (a, bc, _) in items)
    return ROW_TILE if widest >= 1024 else _pick(lp, (768, 512, 256))


def _head_cols(tiles, h, heads):
    return [x[:, h * (x.shape[1] // heads):(h + 1) * (x.shape[1] // heads)] for x in tiles]


def _rowwise(fn, rows, nodiff, params, outs, *, ncol=1, heads=1, name):
    lp = rows[0][0].shape[0]
    tr = _row_tile(lp, rows)
    nr, nd = len(rows), len(nodiff)

    def body(*refs):
        rv = [r[...].astype(f32) for r in refs[:nr]]
        nv = [r[...] for r in refs[nr:nr + nd]]
        pv = [r[...] for r in refs[nr + nd:nr + nd + len(params)]]
        per_head = [fn(_head_cols(rv, h, heads), nv, pv) for h in range(heads)]
        res = [jnp.concatenate(list(vals), axis=1) if heads > 1 else vals[0] for vals in zip(*per_head)]
        for ref, val in zip(refs[nr + nd + len(params):], res):
            ref[...] = val.astype(ref.dtype)

    out_specs = [pl.BlockSpec((tr, c if bc is None else bc), (lambda i, j: (i, 0)) if bc is None else (lambda i, j: (i, j)))
                 for (c, _, bc) in outs]
    return pl.pallas_call(
        body, name=name, grid=(lp // tr, ncol),
        in_specs=[_row_spec(it, tr) for it in rows + nodiff] + [_param_spec(p) for p in params],
        out_specs=out_specs,
        out_shape=[jax.ShapeDtypeStruct((lp, c), dt) for (c, dt, _) in outs],
    )(*[it[0] for it in rows + nodiff], *params)


def _rowwise_vjp(fn, rows, nodiff, params, cts, *, ncol=1, heads=1, name, ct_pre=None, extra=None, grad_dtypes=None):
    lp = rows[0][0].shape[0]
    tr = _row_tile(lp, rows)
    nr, nd, npar, nct = len(rows), len(nodiff), len(params), len(cts)
    extra = extra or [None] * nr
    grad_dtypes = grad_dtypes or [f32] * nr
    ex_items = [(e, rows[k][1], 0) for k, e in enumerate(extra) if e is not None]
    ex_pos = [k for k, e in enumerate(extra) if e is not None]
    for (a, bc, _) in rows:
        assert bc is not None or ncol == 1

    def body(*refs):
        pos = 0
        rv = [r[...].astype(f32) for r in refs[pos:pos + nr]]; pos += nr
        nv = [r[...] for r in refs[pos:pos + nd]]; pos += nd
        pv = [r[...] for r in refs[pos:pos + npar]]; pos += npar
        cv = [r[...].astype(f32) for r in refs[pos:pos + nct]]; pos += nct
        ev = [r[...].astype(f32) for r in refs[pos:pos + len(ex_items)]]; pos += len(ex_items)
        drow_refs = refs[pos:pos + nr]; pos += nr
        dpar_refs = refs[pos:pos + npar]
        ctv = ct_pre(cv) if ct_pre is not None else cv
        drow_h, dpar = [], None
        for h in range(heads):
            outs, vjp_fn = jax.vjp(lambda rr, pp: fn(rr, nv, pp), _head_cols(rv, h, heads), pv)
            dr, dp = vjp_fn([c.astype(o.dtype) for c, o in zip(_head_cols(ctv, h, heads), outs)])
            drow_h.append(dr)
            dpar = dp if dpar is None else [a + b for a, b in zip(dpar, dp)]
        drow = [jnp.concatenate(list(vals), axis=1) if heads > 1 else vals[0] for vals in zip(*drow_h)]
        for k, e in zip(ex_pos, ev):
            drow[k] = drow[k] + e
        for ref, val in zip(drow_refs, drow):
            ref[...] = val.astype(ref.dtype)
        first = jnp.logical_and(pl.program_id(0) == 0, pl.program_id(1) == 0)

        @pl.when(first)
        def _():
            for ref, val in zip(dpar_refs, dpar):
                ref[...] = val

        @pl.when(jnp.logical_not(first))
        def _():
            for ref, val in zip(dpar_refs, dpar):
                ref[...] += val

    drow_shapes, drow_specs = [], []
    for (a, bc, _), dt in zip(rows, grad_dtypes):
        if bc is None:
            drow_shapes.append(jax.ShapeDtypeStruct((lp, a.shape[1]), dt))
            drow_specs.append(pl.BlockSpec((tr, a.shape[1]), lambda i, j: (i, 0)))
        else:
            drow_shapes.append(jax.ShapeDtypeStruct((lp, ncol * bc), dt))
            drow_specs.append(pl.BlockSpec((tr, bc), lambda i, j: (i, j)))
    res = pl.pallas_call(
        body, name=name, grid=(lp // tr, ncol),
        in_specs=[_row_spec(it, tr) for it in rows + nodiff] + [_param_spec(p) for p in params]
        + [_row_spec(it, tr) for it in cts + ex_items],
        out_specs=drow_specs + [_param_spec(p) for p in params],
        out_shape=drow_shapes + [jax.ShapeDtypeStruct(p.shape, f32) for p in params],
        compiler_params=pltpu.CompilerParams(dimension_semantics=("arbitrary", "arbitrary")),
    )(*[it[0] for it in rows + nodiff], *params, *[it[0] for it in cts + ex_items])
    return res[:nr], res[nr:]


def _rms(x, g):
    return x * lax.rsqrt(jnp.mean(x * x, axis=-1, keepdims=True) + NORM_EPS) * g


def _l2n(x):
    return x * lax.rsqrt(jnp.sum(x * x, axis=-1, keepdims=True) + NORM_EPS)


def _sigmoid(x):
    return 1.0 / (1.0 + jnp.exp(-x))


def _silu(x):
    return x * _sigmoid(x)


def _softplus(x):
    return jnp.maximum(x, 0.0) + jnp.log(1.0 + jnp.exp(-jnp.abs(x)))


def _row_ids(shape):
    return pl.program_id(0) * shape[0] + lax.broadcasted_iota(jnp.int32, shape, 0)


def _st_prenorm(r, n, p):
    return [_rms(r[0], p[0])]


def _st_gdn_q(r, n, p):
    return [_l2n(_silu(r[0])) * (GDN_D ** -0.5)]


def _st_gdn_k(r, n, p):
    return [_l2n(_silu(r[0]))]


def _st_gdn_v(r, n, p):
    return [_silu(r[0])]


def _st_gdn_gate(r, n, p):
    real = _row_ids(r[0].shape) >= PAD_FRONT
    beta = jnp.where(real, _sigmoid(r[0]), 0.0)
    g = jnp.where(real, -jnp.exp(p[0]) * _softplus(r[1] + p[1]), 0.0)
    return [beta, g]


def _st_gdn_out(r, n, p):
    return [_rms(r[0], p[0]) * _silu(r[1])]


def _st_mid(r, n, p):
    h1 = r[0] + _rms(r[1], p[0])
    return [h1, _rms(h1, p[1]), _rms(h1, p[2])]


def _st_latent(r, n, p):
    ckr, cq = r
    c_kv = _rms(ckr[:, :MLA_KV_RANK], p[0])
    k_rope = ckr[:, 128:256] * n[0] + ckr[:, 384:512] * n[1]
    return [c_kv, k_rope, _rms(cq, p[1])]


Q_GROUP = 8


def _st_q_rope(r, n, p):
    half = Q_GROUP * MLA_QKP
    out = []
    for h in range(Q_GROUP):
        cols = slice(h * MLA_QKP, (h + 1) * MLA_QKP)
        out.append((r[0][:, :half][:, cols] * n[0] + r[0][:, half:][:, cols] * n[1]) * (MLA_QK ** -0.5))
    return [jnp.concatenate(out, axis=1)]


def _st_q_rope_t(r, n, p):
    plain, swapped = [], []
    for h in range(Q_GROUP):
        ct = r[0][:, h * MLA_QKP:(h + 1) * MLA_QKP] * (MLA_QK ** -0.5)
        plain.append(ct * n[0])
        swapped.append(ct * n[1])
    return [jnp.concatenate(plain + swapped, axis=1)]


def _make_st_loss(n_tokens):
    def st(r, n, p):
        h2 = r[0] + _rms(r[1], p[0])
        rows = _row_ids((r[0].shape[0], 1))
        real = jnp.logical_and(rows >= ROW0, rows < ROW0 + n_tokens)
        err = h2 - n[0]
        return [jnp.where(real, 0.5 * jnp.mean(err * err, axis=-1, keepdims=True), 0.0)]
    return st


CONV_BC = 1024
HALO = 16


def _conv_fwd(x, w, *, col_blocks, name):
    lp = x.shape[0]
    tr = _pick(lp, (768, 512, 256))

    def body(x_ref, xp_ref, w_ref, o_ref):
        i = pl.program_id(0)
        xv = x_ref[...].astype(f32)
        prev = jnp.where(i > 0, xp_ref[...].astype(f32), 0.0)
        xc = jnp.concatenate([prev, xv], axis=0)
        wv = w_ref[...]
        acc = wv[3:4, :] * xv
        for j in range(3):
            acc = acc + wv[j:j + 1, :] * pltpu.roll(xc, 3 - j, 0)[HALO:, :]
        o_ref[...] = acc.astype(o_ref.dtype)

    return pl.pallas_call(
        body, name=name, grid=(lp // tr, col_blocks),
        in_specs=[pl.BlockSpec((tr, CONV_BC), lambda i, j: (i, j)),
                  pl.BlockSpec((HALO, CONV_BC), lambda i, j: (jnp.maximum(i * (tr // HALO) - 1, 0), j)),
                  pl.BlockSpec((4, CONV_BC), lambda i, j: (0, j))],
        out_specs=pl.BlockSpec((tr, CONV_BC), lambda i, j: (i, j)),
        out_shape=jax.ShapeDtypeStruct((lp, col_blocks * CONV_BC), bf16),
    )(x, x, w)


def _conv_bwd(dc, x, w, *, x_off, w_off, name):
    lp, width = dc.shape
    tr = _pick(lp, (768, 512, 256))
    ncb, nrow = width // CONV_BC, lp // tr

    def body(dc_ref, dcn_ref, x_ref, xp_ref, w_ref, dx_ref, dw_ref):
        i = pl.program_id(1)
        nxt = jnp.where(i < nrow - 1, dcn_ref[...], 0.0)
        dcv = dc_ref[...]
        dcc = jnp.concatenate([dcv, nxt], axis=0)
        xv = x_ref[...].astype(f32)
        prev = jnp.where(i > 0, xp_ref[...].astype(f32), 0.0)
        xc = jnp.concatenate([prev, xv], axis=0)
        wv = w_ref[...]
        dx = wv[3:4, :] * dcv
        dws = [None] * 4
        dws[3] = jnp.sum(dcv * xv, axis=0, keepdims=True)
        for j in range(3):
            dx = dx + wv[j:j + 1, :] * pltpu.roll(dcc, tr + 8 - (3 - j), 0)[:tr, :]
            dws[j] = jnp.sum(dcv * pltpu.roll(xc, 3 - j, 0)[HALO:, :], axis=0, keepdims=True)
        dx_ref[...] = dx.astype(dx_ref.dtype)

        @pl.when(i == 0)
        def _():
            for j in range(4):
                dw_ref[j:j + 1, :] = dws[j]

        @pl.when(i > 0)
        def _():
            for j in range(4):
                dw_ref[j:j + 1, :] += dws[j]

    last8 = lp // 8 - 1
    return pl.pallas_call(
        body, name=name, grid=(ncb, nrow),
        in_specs=[pl.BlockSpec((tr, CONV_BC), lambda j, i: (i, j)),
                  pl.BlockSpec((8, CONV_BC), lambda j, i: (jnp.minimum((i + 1) * (tr // 8), last8), j)),
                  pl.BlockSpec((tr, CONV_BC), lambda j, i: (i, j + x_off)),
                  pl.BlockSpec((HALO, CONV_BC), lambda j, i: (jnp.maximum(i * (tr // HALO) - 1, 0), j + x_off)),
                  pl.BlockSpec((4, CONV_BC), lambda j, i: (0, j + w_off))],
        out_specs=[pl.BlockSpec((tr, CONV_BC), lambda j, i: (i, j)),
                   pl.BlockSpec((4, CONV_BC), lambda j, i: (0, j))],
        out_shape=[jax.ShapeDtypeStruct((lp, width), bf16), jax.ShapeDtypeStruct((4, width), f32)],
        compiler_params=pltpu.CompilerParams(dimension_semantics=("arbitrary", "arbitrary")),
    )(dc, dc, x, x, w)


GDN_PACK = 4
GDN_FWD_INTERLEAVE, GDN_BWD_INTERLEAVE = 4, 4


MXU_TILE = 256


def _bd(x, cb, g=GDN_PACK):
    r = x.shape[0]
    tall = jnp.concatenate([x] * g, axis=0)
    rows = lax.broadcasted_iota(jnp.int32, tall.shape, 0) // r
    cols = lax.broadcasted_iota(jnp.int32, tall.shape, 1) // cb
    return jnp.where(rows == cols, tall, jnp.zeros_like(tall))


def _diag(full, r, cb, g=GDN_PACK):
    cols = lax.broadcasted_iota(jnp.int32, (r, full.shape[1]), 1) // cb
    out = jnp.where(cols == 0, full[0:r, :], 0.0)
    for a in range(1, g):
        out = out + jnp.where(cols == a, full[a * r:(a + 1) * r, :], 0.0)
    return out


def _stack(x, cb):
    return jnp.concatenate([x[:, a * cb:(a + 1) * cb] for a in range(GDN_PACK)], axis=0)


def _lane_halves(a):
    return a[:, :a.shape[1] // 2], a[:, a.shape[1] // 2:]


def _make_packed(dot):
    G = GDN_PACK

    def two_by_two(k, cb):
        return k * G > MXU_TILE and cb * G > MXU_TILE

    def times_bd(x, y, dims):
        k, cb = y.shape[0], y.shape[1] // G
        if two_by_two(k, cb):
            return jnp.concatenate([dot(xh, _bd(yh, cb, G // 2), dims)
                                    for xh, yh in zip(_lane_halves(x), _lane_halves(y))], axis=1)
        return dot(x, _bd(y, cb), dims)

    def tn_diag(x, y):
        k, cb = x.shape[1] // G, y.shape[1] // G
        if two_by_two(k, cb):
            return jnp.concatenate([_diag(dot(xh, yh, TN), k, cb, G // 2)
                                    for xh, yh in zip(_lane_halves(x), _lane_halves(y))], axis=1)
        return _diag(dot(x, y, TN), k, cb)

    @jax.custom_vjp
    def pmm(x, y):
        return times_bd(x, y, NN)

    @jax.custom_vjp
    def pnt(x, y):
        k = x.shape[1] // G
        return _diag(dot(_stack(x, k), _stack(y, k), NT), x.shape[0], y.shape[0])

    @jax.custom_vjp
    def ptn(x, y):
        return tn_diag(x, y)

    def pmm_bwd(res, ct):
        x, y = res
        return times_bd(ct, y, NT), tn_diag(x, ct)

    pmm.defvjp(lambda x, y: (pmm(x, y), (x, y)), pmm_bwd)
    pnt.defvjp(lambda x, y: (pnt(x, y), (x, y)), lambda res, ct: (pmm(ct, res[1]), ptn(ct, res[0])))
    ptn.defvjp(lambda x, y: (ptn(x, y), (x, y)), lambda res, ct: (pnt(res[1], ct), pmm(res[0], ct)))
    return pmm, pnt, ptn


_pmm, _pnt, _ptn = _make_packed(_dlo)


@jax.custom_vjp
def _inv_packed(ms):
    c = ms[0].shape[0]
    ii = lax.broadcasted_iota(jnp.int32, ms[0].shape, 0)
    jj = lax.broadcasted_iota(jnp.int32, ms[0].shape, 1) % c
    ts = [jnp.where(ii == jj, 1.0, 0.0) - m for m in ms]
    ps = [(-m).astype(bf16) for m in ms]
    for _ in range(int(math.log2(c)) - 1):
        ps = [_dlo(p, _bd(p, c), NN).astype(bf16) for p in ps]
        ts = [t + _dlo(t, _bd(p, c), NN) for t, p in zip(ts, ps)]
    return tuple(ts)


def _inv_packed_fwd(ms):
    ts = _inv_packed(ms)
    return ts, ts


def _inv_packed_bwd(ts, cts):
    c = ts[0].shape[0]
    ys = [_diag(_dlo(t, ct, TN), c, c) for t, ct in zip(ts, cts)]
    return (tuple(-_dlo(y, _bd(t.astype(bf16), c), NT) for y, t in zip(ys, ts)),)


_inv_packed.defvjp(_inv_packed_fwd, _inv_packed_bwd)


def _gdn_prep(q2, k2, v4, bcols, gcols, grows):
    c, d = v4.shape[0], GDN_D
    q4 = jnp.concatenate([q2[:, :d], q2[:, :d], q2[:, d:], q2[:, d:]], axis=1)
    k4 = jnp.concatenate([k2[:, :d], k2[:, :d], k2[:, d:], k2[:, d:]], axis=1)
    beta4 = jnp.concatenate([jnp.broadcast_to(b, (c, d)) for b in bcols], axis=1)
    gc4 = jnp.concatenate([jnp.broadcast_to(g, (c, d)) for g in gcols], axis=1)
    low = lax.broadcasted_iota(jnp.int32, (c, 128), 1) < c
    gi = jnp.concatenate([jnp.where(low, gcols[0], gcols[1]), jnp.where(low, gcols[2], gcols[3])], axis=1)
    gj = jnp.concatenate([jnp.where(low, grows[0], grows[1]), jnp.where(low, grows[2], grows[3])], axis=1)
    ii = lax.broadcasted_iota(jnp.int32, gi.shape, 0)
    jj = lax.broadcasted_iota(jnp.int32, gi.shape, 1) % c
    dec = jnp.exp(jnp.where(ii >= jj, gi - gj, NEG))
    rid = lax.broadcasted_iota(jnp.int32, gc4.shape, 0)
    glast = jnp.sum(jnp.where(rid == c - 1, gc4, 0.0), axis=0, keepdims=True)
    eg = jnp.exp(gc4)
    kb = k4 * beta4
    return dict(q=q4, k=k4, kb=kb, vb=v4 * beta4, kbe=kb * eg, qe=q4 * eg, dec=dec, dec_strict=jnp.where(ii > jj, dec, 0.0),
                sdecay=jnp.exp(glast), kd=k4 * jnp.exp(glast - gc4))


@jax.custom_vjp
def _inv_packed_known(ms, ts):
    return ts


_inv_packed_known.defvjp(lambda ms, ts: (ts, ts),
                         lambda ts, cts: (_inv_packed_bwd(ts, cts)[0], tuple(jnp.zeros_like(t) for t in ts)))


def _gdn_groups(groups, known_inverses=None, with_inverses=False):
    c = groups[0][3].shape[0]
    ss = [g[0] for g in groups]
    e = [_gdn_prep(*g[1:]) for g in groups]
    ms = tuple(_pnt(x["kb"], x["k"]) * x["dec_strict"] for x in e)
    ts = _inv_packed(ms) if known_inverses is None else _inv_packed_known(ms, tuple(known_inverses))
    us = [_pmm(t, x["vb"]) for t, x in zip(ts, e)]
    ws = [_pmm(t, x["kbe"]) for t, x in zip(ts, e)]
    attns = [_pnt(x["q"], x["k"]) * x["dec"] for x in e]
    ws_qs = [_pmm(jnp.concatenate([w, x["qe"]], axis=0), s) for w, x, s in zip(ws, e, ss)]
    v_news = [u - y[:c] for u, y in zip(us, ws_qs)]
    os = [y[c:] + _pmm(a, vn) for y, a, vn in zip(ws_qs, attns, v_news)]
    s_news = [s * x["sdecay"] + _ptn(x["kd"], vn) for s, x, vn in zip(ss, e, v_news)]
    if with_inverses:
        return list(zip(os, s_news)), list(ts)
    return list(zip(os, s_news))


def _lane_pick(x, h):
    lane = lax.broadcasted_iota(jnp.int32, x.shape, 1)
    return jnp.sum(jnp.where(lane == h, x, 0.0), axis=1, keepdims=True)


def _cum_log_decay(g):
    c = g.shape[0]
    lower = (lax.broadcasted_iota(jnp.int32, (c, c), 0) >= lax.broadcasted_iota(jnp.int32, (c, c), 1)).astype(f32)
    upper2 = (lax.broadcasted_iota(jnp.int32, (c, 128), 0) <= lax.broadcasted_iota(jnp.int32, (c, 128), 1) % c).astype(f32)
    return _dsel(lower, g, NN, 0), _dsel(g, upper2, TN, 1)


def _group_operands(gi, s_ref, q_ref, k_ref, v_ref, bv, gcv, gct_s):
    heads = [gi * GDN_PACK + u for u in range(GDN_PACK)]
    qk_off = pl.multiple_of(gi * 2 * GDN_D, 2 * GDN_D)
    v_off = pl.multiple_of(gi * GDN_PACK * GDN_D, GDN_PACK * GDN_D)
    return (s_ref[gi], q_ref[:, pl.ds(qk_off, 2 * GDN_D)].astype(f32), k_ref[:, pl.ds(qk_off, 2 * GDN_D)].astype(f32),
            v_ref[:, pl.ds(v_off, GDN_PACK * GDN_D)].astype(f32),
            [_lane_pick(bv, h) for h in heads], [_lane_pick(gcv, h) for h in heads],
            [gct_s[pl.ds(h, 1), :] for h in heads]), heads, qk_off, v_off


def _gdn_fwd(qn, kn, v, beta, g, *, n_real_chunks, name):
    lp = qn.shape[0]
    nchunk = lp // GDN_CHUNK
    C, D = GDN_CHUNK, GDN_D
    NG, SW = GDN_V_HEADS // GDN_PACK, GDN_PACK * GDN_D

    def body(q_ref, k_ref, v_ref, b_ref, g_ref, o_ref, st_ref, inv_ref, s_s, gc_s, gct_s):
        ci = pl.program_id(0)

        @pl.when(ci == 0)
        def _():
            s_s[...] = jnp.zeros_like(s_s)

        @pl.when(ci >= n_real_chunks)
        def _():
            o_ref[...] = jnp.zeros_like(o_ref)
            st_ref[...] = jnp.zeros_like(st_ref)
            inv_ref[...] = jnp.zeros_like(inv_ref)

        @pl.when(ci < n_real_chunks)
        def _():
            gc, gct = _cum_log_decay(g_ref[...])
            gc_s[...] = gc
            gct_s[...] = gct

            def some_groups(it, carry):
                ids = [it * GDN_FWD_INTERLEAVE + u for u in range(GDN_FWD_INTERLEAVE)]
                ops = [_group_operands(gi, s_s, q_ref, k_ref, v_ref, b_ref[...], gc_s[...], gct_s) for gi in ids]
                res, inverses = _gdn_groups([op[0] for op in ops], with_inverses=True)
                for gi, op, (o, s_new), t in zip(ids, ops, res, inverses):
                    st_ref[gi] = op[0][0]
                    inv_ref[gi] = t
                    s_s[gi] = s_new
                    o_ref[:, pl.ds(op[3], SW)] = o
                return carry

            lax.fori_loop(0, NG // GDN_FWD_INTERLEAVE, some_groups, 0)

    return pl.pallas_call(
        body, name=name, grid=(nchunk,),
        in_specs=[pl.BlockSpec((C, GDN_QK_W), lambda c: (c, 0)), pl.BlockSpec((C, GDN_QK_W), lambda c: (c, 0)),
                  pl.BlockSpec((C, GDN_V_W), lambda c: (c, 0)), pl.BlockSpec((C, 128), lambda c: (c, 0)),
                  pl.BlockSpec((C, 128), lambda c: (c, 0))],
        out_specs=[pl.BlockSpec((C, GDN_V_W), lambda c: (c, 0)),
                   pl.BlockSpec((None, NG, D, SW), lambda c: (c, 0, 0, 0)),
                   pl.BlockSpec((None, NG, C, GDN_PACK * C), lambda c: (c, 0, 0, 0))],
        out_shape=[jax.ShapeDtypeStruct((lp, GDN_V_W), f32), jax.ShapeDtypeStruct((nchunk, NG, D, SW), f32),
                   jax.ShapeDtypeStruct((nchunk, NG, C, GDN_PACK * C), f32)],
        scratch_shapes=[pltpu.VMEM((NG, D, SW), f32), pltpu.VMEM((C, 128), f32), pltpu.VMEM((128, 128), f32)],
        compiler_params=pltpu.CompilerParams(dimension_semantics=("arbitrary",)),
    )(qn, kn, v, beta, g)


def _gdn_bwd(qn, kn, v, beta, g, states, inverses, do, *, n_real_chunks, name):
    lp = qn.shape[0]
    nchunk = lp // GDN_CHUNK
    C, D = GDN_CHUNK, GDN_D
    NG, SW = GDN_V_HEADS // GDN_PACK, GDN_PACK * GDN_D
    rev = lambda i: (nchunk - 1 - i, 0)

    def body(q_ref, k_ref, v_ref, b_ref, g_ref, st_ref, inv_ref, do_ref,
             dq_ref, dk_ref, dv_ref, db_ref, dg_ref, ds_s, gc_s, gct_s, dgc_s, dgct_s, dbeta_s):
        step = pl.program_id(0)
        ci = nchunk - 1 - step

        @pl.when(step == 0)
        def _():
            ds_s[...] = jnp.zeros_like(ds_s)

        @pl.when(ci >= n_real_chunks)
        def _():
            for r in (dq_ref, dk_ref, dv_ref, db_ref, dg_ref):
                r[...] = jnp.zeros_like(r)

        @pl.when(ci < n_real_chunks)
        def _():
            gc, gct = _cum_log_decay(g_ref[...])
            gc_s[...] = gc
            gct_s[...] = gct
            dgc_s[...] = jnp.zeros_like(dgc_s)
            dgct_s[...] = jnp.zeros_like(dgct_s)
            dbeta_s[...] = jnp.zeros_like(dbeta_s)

            def some_groups(it, carry):
                ids = [it * GDN_BWD_INTERLEAVE + u for u in range(GDN_BWD_INTERLEAVE)]
                ops = [_group_operands(gi, st_ref, q_ref, k_ref, v_ref, b_ref[...], gc_s[...], gct_s) for gi in ids]
                cts = [(do_ref[:, pl.ds(op[3], SW)], ds_s[gi]) for gi, op in zip(ids, ops)]
                known = [inv_ref[gi] for gi in ids]
                _, vjp_fn = jax.vjp(lambda gs: _gdn_groups(gs, known_inverses=known), [op[0] for op in ops])
                (grads,) = vjp_fn(cts)
                lane = lax.broadcasted_iota(jnp.int32, (C, 128), 1)
                dbeta_acc, dgc_acc = dbeta_s[...], dgc_s[...]
                for gi, (_, heads, qk_off, v_off), (dsp, dq2, dk2, dv4, dbcols, dgcols, dgrows) in zip(ids, ops, grads):
                    ds_s[gi] = dsp
                    dq_ref[:, pl.ds(qk_off, 2 * D)] = dq2
                    dk_ref[:, pl.ds(qk_off, 2 * D)] = dk2
                    dv_ref[:, pl.ds(v_off, SW)] = dv4
                    for h, dbcol, dgcol, dgrow in zip(heads, dbcols, dgcols, dgrows):
                        dbeta_acc = dbeta_acc + jnp.where(lane == h, dbcol, 0.0)
                        dgc_acc = dgc_acc + jnp.where(lane == h, dgcol, 0.0)
                        dgct_s[pl.ds(h, 1), :] = dgrow
                dbeta_s[...] = dbeta_acc
                dgc_s[...] = dgc_acc
                return carry

            lax.fori_loop(0, NG // GDN_BWD_INTERLEAVE, some_groups, 0)
            fold = (lax.broadcasted_iota(jnp.int32, (128, C), 0) % C == lax.broadcasted_iota(jnp.int32, (128, C), 1)).astype(f32)
            eye = (lax.broadcasted_iota(jnp.int32, (128, 128), 0) == lax.broadcasted_iota(jnp.int32, (128, 128), 1)).astype(f32)
            dgc = dgc_s[...] + _dsel(_dsel(dgct_s[...], fold, NN, 1), eye, TN, 1)
            upper = (lax.broadcasted_iota(jnp.int32, (C, C), 0) <= lax.broadcasted_iota(jnp.int32, (C, C), 1)).astype(f32)
            dg_ref[...] = _dsel(upper, dgc, NN, 0)
            db_ref[...] = dbeta_s[...]

    return pl.pallas_call(
        body, name=name, grid=(nchunk,),
        in_specs=[pl.BlockSpec((C, GDN_QK_W), rev), pl.BlockSpec((C, GDN_QK_W), rev), pl.BlockSpec((C, GDN_V_W), rev),
                  pl.BlockSpec((C, 128), rev), pl.BlockSpec((C, 128), rev),
                  pl.BlockSpec((None, NG, D, SW), lambda i: (nchunk - 1 - i, 0, 0, 0)),
                  pl.BlockSpec((None, NG, C, GDN_PACK * C), lambda i: (nchunk - 1 - i, 0, 0, 0)), pl.BlockSpec((C, GDN_V_W), rev)],
        out_specs=[pl.BlockSpec((C, GDN_QK_W), rev), pl.BlockSpec((C, GDN_QK_W), rev), pl.BlockSpec((C, GDN_V_W), rev),
                   pl.BlockSpec((C, 128), rev), pl.BlockSpec((C, 128), rev)],
        out_shape=[jax.ShapeDtypeStruct((lp, GDN_QK_W), f32)] * 2 + [jax.ShapeDtypeStruct((lp, GDN_V_W), f32)]
        + [jax.ShapeDtypeStruct((lp, 128), f32)] * 2,
        scratch_shapes=[pltpu.VMEM((NG, D, SW), f32), pltpu.VMEM((C, 128), f32), pltpu.VMEM((128, 128), f32),
                        pltpu.VMEM((C, 128), f32), pltpu.VMEM((128, 128), f32), pltpu.VMEM((C, 128), f32)],
        compiler_params=pltpu.CompilerParams(dimension_semantics=("arbitrary",)),
    )(qn, kn, v, beta, g, states, inverses, do)


def _attention_bias(tq, tk):
    kk = lax.broadcasted_iota(jnp.int32, (tk, tq), 0)
    qq = lax.broadcasted_iota(jnp.int32, (tk, tq), 1)
    pad = jnp.where(kk < PAD_FRONT, NEG, 0.0)
    diag = [jnp.where(qq >= kk + d * tk, 0.0, NEG) for d in range(tq // tk)]
    return jnp.stack([pad] + diag + [jnp.minimum(pad, diag[0])]).astype(f32)


def _att_tiles(lp):
    tq = _pick(lp, (768, 512, 256))
    return tq, _pick(tq, ATT_KEY_TILES)


def _attention_fwd(q, kvu, kr, proj2, *, name):
    lp = q.shape[0]
    H = MLA_HEADS
    tq, tk = _att_tiles(lp)
    r = tq // tk

    def body(q_ref, kn_ref, kr_ref, v_ref, z_ref, bias_ref, o_ref, lse_ref, og_ref, m_s, l_s, acc_s, sa_s, sb_s):
        qi = pl.program_id(1)
        m_s[...] = jnp.full_like(m_s, NEG)
        l_s[...] = jnp.zeros_like(l_s)
        acc_s[...] = jnp.zeros_like(acc_s)

        def scores(ki):
            k0 = pl.multiple_of(ki * tk, tk)
            k = jnp.concatenate([kn_ref[pl.ds(k0, tk), :], kr_ref[pl.ds(k0, tk), :]], axis=1)
            return lax.dot_general(k, q_ref[...], (NT, ((), ())), preferred_element_type=f32)

        def consume(st, ki, mask):
            k0 = pl.multiple_of(ki * tk, tk)
            if mask is not None:
                st = st + bias_ref[mask]
            m_prev = m_s[...]
            m_new = jnp.maximum(m_prev, jnp.max(st, axis=0, keepdims=True))
            alpha = jnp.exp(m_prev - m_new)
            p = jnp.exp(st - m_new)
            l_s[...] = alpha * l_s[...] + jnp.sum(p, axis=0, keepdims=True)
            acc_s[...] = alpha * acc_s[...] + lax.dot_general(v_ref[pl.ds(k0, tk), :], p.astype(bf16), (TN, ((), ())),
                                                              preferred_element_type=f32)
            m_s[...] = m_new

        n_full = qi * r

        def chain(blocks):
            bufs = (sa_s, sb_s)
            for j, (ki, masked) in enumerate(blocks):
                if j + 1 < len(blocks):
                    bufs[(j + 1) % 2][...] = scores(blocks[j + 1][0])
                consume(bufs[j % 2][...], ki, masked)

        diagonal = [(n_full + d, 1 + d) for d in range(r)]

        @pl.when(qi == 0)
        def _():
            sa_s[...] = scores(0)
            chain([(0, r + 1)] + diagonal[1:])

        @pl.when(qi > 0)
        def _():
            sb_s[...] = scores(0)
            sa_s[...] = scores(1)
            consume(sb_s[...], 0, 0)
            n_pairs = (n_full - 1) // 2

            def two(pi, carry):
                ki = 1 + 2 * pi
                sb_s[...] = scores(ki + 1)
                consume(sa_s[...], ki, None)
                sa_s[...] = scores(ki + 2)
                consume(sb_s[...], ki + 1, None)
                return carry

            lax.fori_loop(0, n_pairs, two, 0)
            nxt = 1 + 2 * n_pairs

            @pl.when(nxt < n_full)
            def _():
                chain([(nxt, None)] + diagonal)

            @pl.when(nxt == n_full)
            def _():
                chain(diagonal)
        o = jnp.transpose(acc_s[...] / l_s[...])
        o_ref[...] = o
        og_ref[...] = (o * _silu(z_ref[...].astype(f32))).astype(og_ref.dtype)
        lse_ref[...] = m_s[...] + jnp.log(l_s[...])

    return pl.pallas_call(
        body, name=name, grid=(H, lp // tq),
        in_specs=[pl.BlockSpec((tq, MLA_QKP), lambda h, qi: (qi, h)),
                  pl.BlockSpec((lp, 128), lambda h, qi: (0, h)),
                  pl.BlockSpec((lp, 128), lambda h, qi: (0, 0)),
                  pl.BlockSpec((lp, 128), lambda h, qi: (0, H + h)),
                  pl.BlockSpec((tq, 128), lambda h, qi: (qi, MLA_Q_RANK // 128 + h)),
                  pl.BlockSpec((r + 2, tk, tq), lambda h, qi: (0, 0, 0))],
        out_specs=[pl.BlockSpec((tq, 128), lambda h, qi: (qi, h)),
                   pl.BlockSpec((None, 1, tq), lambda h, qi: (h, 0, qi)),
                   pl.BlockSpec((tq, 128), lambda h, qi: (qi, h))],
        out_shape=[jax.ShapeDtypeStruct((lp, MLA_V_W), f32), jax.ShapeDtypeStruct((H, 1, lp), f32),
                   jax.ShapeDtypeStruct((lp, MLA_V_W), bf16)],
        scratch_shapes=[pltpu.VMEM((1, tq), f32), pltpu.VMEM((1, tq), f32), pltpu.VMEM((128, tq), f32),
                        pltpu.VMEM((tk, tq), f32), pltpu.VMEM((tk, tq), f32)],
        compiler_params=pltpu.CompilerParams(dimension_semantics=("arbitrary", "arbitrary")),
    )(q, kvu, kr, kvu, proj2, _attention_bias(tq, tk))


def _gate_bwd(o, proj2, dgated, *, name):
    lp = o.shape[0]
    H, w = MLA_HEADS, 2 * MLA_V
    tq = _pick(lp, (768, 512, 256))

    def body(o_ref, z_ref, g_ref, do_ref, dz_ref, dl_ref):
        ov, z, g = o_ref[...], z_ref[...].astype(f32), g_ref[...]
        s = _sigmoid(z)
        do = (g * (z * s)).astype(bf16)
        do_ref[...] = do
        dz_ref[...] = (g * ov * (s * (1.0 + z * (1.0 - s)))).astype(dz_ref.dtype)
        prod = ov * do.astype(f32)
        for u in range(2):
            dl_ref[u] = jnp.sum(jnp.transpose(prod[:, u * MLA_V:(u + 1) * MLA_V]), axis=0, keepdims=True)

    blk = pl.BlockSpec((tq, w), lambda j, qi: (qi, j))
    return pl.pallas_call(
        body, name=name, grid=(H // 2, lp // tq),
        in_specs=[blk, pl.BlockSpec((tq, w), lambda j, qi: (qi, j + MLA_Q_RANK // w)), blk],
        out_specs=[blk, blk, pl.BlockSpec((2, 1, tq), lambda j, qi: (j, 0, qi))],
        out_shape=[jax.ShapeDtypeStruct((lp, MLA_V_W), bf16), jax.ShapeDtypeStruct((lp, MLA_V_W), bf16),
                   jax.ShapeDtypeStruct((H, 1, lp), f32)],
    )(o, proj2, dgated)


def _attention_bwd(q, kvu, kr, lse, delta, do, *, name):
    lp = q.shape[0]
    tq, t = _att_tiles(lp)
    H, nb = MLA_HEADS, lp // t
    r, nq = tq // t, lp // tq

    def body(q_ref, kn_ref, kr_ref, v_ref, lse_ref, dl_ref, do_ref, bias_ref, dq_ref, dkn_ref, dv_ref, dkr_ref, dk_s, dv_s,
             sa_s, da_s, sb_s, db_s):
        ki = pl.program_id(1)
        k = jnp.concatenate([kn_ref[...], kr_ref[...]], axis=1)
        vv = v_ref[...]
        dk_s[...] = jnp.zeros_like(dk_s)
        dv_s[...] = jnp.zeros_like(dv_s)

        def products(qi, s_ref, d_ref):
            q0 = pl.multiple_of(qi * tq, tq)
            s_ref[...] = lax.dot_general(k, q_ref[pl.ds(q0, tq), :], (NT, ((), ())), preferred_element_type=f32)
            d_ref[...] = lax.dot_general(vv, do_ref[pl.ds(q0, tq), :], (NT, ((), ())), preferred_element_type=f32)

        def accumulate(s_ref, d_ref, qi, mask, first):
            q0 = pl.multiple_of(qi * tq, tq)
            qv = q_ref[pl.ds(q0, tq), :]
            dob = do_ref[pl.ds(q0, tq), :]
            st = s_ref[...]
            if mask is not None:
                st = st + bias_ref[mask]
            p = jnp.exp(st - lse_ref[:, pl.ds(q0, tq)])
            dv_s[...] += jnp.dot(p.astype(bf16), dob, preferred_element_type=f32)
            ds = (p * (d_ref[...] - dl_ref[:, pl.ds(q0, tq)])).astype(bf16)
            dk_s[...] += jnp.dot(ds, qv, preferred_element_type=f32)
            dq = lax.dot_general(ds, k, (TN, ((), ())), preferred_element_type=f32)
            if first:
                dq_ref[pl.ds(q0, tq), :] = dq
            else:
                dq_ref[pl.ds(q0, tq), :] += dq

        def sweep(qd, first_mask, other_mask, first):
            last = nq - 1
            products(qd, sa_s, da_s)
            products(jnp.minimum(qd + 1, last), sb_s, db_s)
            accumulate(sa_s, da_s, qd, first_mask, first)
            n = last - qd

            def two(pi, carry):
                i = qd + 1 + 2 * pi
                products(i + 1, sa_s, da_s)
                accumulate(sb_s, db_s, i, other_mask, first)
                products(jnp.minimum(i + 2, last), sb_s, db_s)
                accumulate(sa_s, da_s, i + 1, other_mask, first)
                return carry

            lax.fori_loop(0, n // 2, two, 0)

            @pl.when(n % 2 == 1)
            def _():
                accumulate(sb_s, db_s, last, other_mask, first)

        @pl.when(ki == 0)
        def _():
            sweep(0, r + 1, 0, True)

        @pl.when(ki > 0)
        def _():
            sweep(ki // r, 1 + ki % r, None, False)

        dkn_ref[...] = dk_s[:, :128].astype(dkn_ref.dtype)
        dkr_ref[...] = dk_s[:, 128:]
        dv_ref[...] = dv_s[...].astype(dv_ref.dtype)

    return pl.pallas_call(
        body, name=name, grid=(H, nb),
        in_specs=[pl.BlockSpec((lp, MLA_QKP), lambda h, ki: (0, h)),
                  pl.BlockSpec((t, 128), lambda h, ki: (ki, h)),
                  pl.BlockSpec((t, 128), lambda h, ki: (ki, 0)),
                  pl.BlockSpec((t, 128), lambda h, ki: (ki, H + h)),
                  pl.BlockSpec((None, 1, lp), lambda h, ki: (h, 0, 0)),
                  pl.BlockSpec((None, 1, lp), lambda h, ki: (h, 0, 0)),
                  pl.BlockSpec((lp, 128), lambda h, ki: (0, h)),
                  pl.BlockSpec((r + 2, t, tq), lambda h, ki: (0, 0, 0))],
        out_specs=[pl.BlockSpec((lp, MLA_QKP), lambda h, ki: (0, h)),
                   pl.BlockSpec((t, 128), lambda h, ki: (ki, h)),
                   pl.BlockSpec((t, 128), lambda h, ki: (ki, h)),
                   pl.BlockSpec((t, 128), lambda h, ki: (ki, h))],
        out_shape=[jax.ShapeDtypeStruct((lp, H * MLA_QKP), f32), jax.ShapeDtypeStruct((lp, MLA_V_W), bf16),
                   jax.ShapeDtypeStruct((lp, MLA_V_W), bf16), jax.ShapeDtypeStruct((lp, MLA_V_W), f32)],
        scratch_shapes=[pltpu.VMEM((t, MLA_QKP), f32), pltpu.VMEM((t, 128), f32)] + [pltpu.VMEM((t, tq), f32)] * 4,
        compiler_params=pltpu.CompilerParams(dimension_semantics=("arbitrary", "arbitrary")),
    )(q, kvu, kr, kvu, lse, delta, do, _attention_bias(tq, t))


def _q_proj_fwd(c_q, wq2, cos_q, sin_q, *, name, tm=ROW_TILE):
    lp, rank = c_q.shape
    gw = Q_GROUP * MLA_QKP
    ng = wq2.shape[1] // (2 * gw)

    def body(c_ref, w_ref, cos_ref, sin_ref, o_ref):
        qq = jnp.dot(c_ref[...], w_ref[...], preferred_element_type=f32)
        o_ref[...] = _st_q_rope([qq], [cos_ref[...], sin_ref[...]], [])[0].astype(o_ref.dtype)

    tab = pl.BlockSpec((tm, MLA_QKP), lambda j, i: (i, 0))
    return pl.pallas_call(
        body, name=name, grid=(ng, lp // tm),
        in_specs=[pl.BlockSpec((tm, rank), lambda j, i: (i, 0)), pl.BlockSpec((rank, 2 * gw), lambda j, i: (0, j)), tab, tab],
        out_specs=pl.BlockSpec((tm, gw), lambda j, i: (i, j)),
        out_shape=jax.ShapeDtypeStruct((lp, ng * gw), bf16),
    )(c_q, wq2, cos_q, sin_q)


def _q_proj_bwd(dq, c_q, wq2, cos_q, sin_q, *, name, tm=ROW_TILE):
    lp, rank = c_q.shape
    gw = Q_GROUP * MLA_QKP
    ng = wq2.shape[1] // (2 * gw)

    def body(dq_ref, c_ref, w_ref, cos_ref, sin_ref, dc_ref, dw_ref):
        dqq = _st_q_rope_t([dq_ref[...]], [cos_ref[...], sin_ref[...]], [])[0].astype(bf16)
        dc_ref[...] = lax.dot_general(dqq, w_ref[...], (NT, ((), ())), preferred_element_type=f32)
        dw = lax.dot_general(c_ref[...], dqq, (TN, ((), ())), preferred_element_type=f32)

        @pl.when(pl.program_id(1) == 0)
        def _():
            dw_ref[...] = dw

        @pl.when(pl.program_id(1) > 0)
        def _():
            dw_ref[...] += dw

    tab = pl.BlockSpec((tm, MLA_QKP), lambda j, i: (i, 0))
    return pl.pallas_call(
        body, name=name, grid=(ng, lp // tm),
        in_specs=[pl.BlockSpec((tm, gw), lambda j, i: (i, j)), pl.BlockSpec((tm, rank), lambda j, i: (i, 0)),
                  pl.BlockSpec((rank, 2 * gw), lambda j, i: (0, j)), tab, tab],
        out_specs=[pl.BlockSpec((tm, rank), lambda j, i: (i, j)), pl.BlockSpec((rank, 2 * gw), lambda j, i: (0, j))],
        out_shape=[jax.ShapeDtypeStruct((lp, ng * rank), f32), jax.ShapeDtypeStruct(wq2.shape, f32)],
        compiler_params=pltpu.CompilerParams(dimension_semantics=("arbitrary", "arbitrary")),
    )(dq, c_q, wq2, cos_q, sin_q)


def _exchange_copies(x_ref, o_ref, send_sems, recv_sems, local_sem, gather):
    mx, my, mc = lax.axis_index("x"), lax.axis_index("y"), lax.axis_index("c")
    me = 4 * mx + 2 * my + mc
    own = pltpu.make_async_copy(x_ref if gather else x_ref.at[me], o_ref.at[me], local_sem)
    sends, arrivals = [], []
    for k in range(1, N_DEV):
        px = 1 - mx if k & 4 else mx
        py = 1 - my if k & 2 else my
        pc = 1 - mc if k & 1 else mc
        peer = 4 * px + 2 * py + pc
        sends.append(pltpu.make_async_remote_copy(
            src_ref=x_ref if gather else x_ref.at[peer], dst_ref=o_ref.at[me],
            send_sem=send_sems.at[k - 1], recv_sem=recv_sems.at[k - 1],
            device_id=(px, py, pc), device_id_type=MESH_ID))
        arrivals.append(pltpu.make_async_remote_copy(
            src_ref=o_ref.at[peer], dst_ref=o_ref.at[peer],
            send_sem=send_sems.at[k - 1], recv_sem=recv_sems.at[k - 1],
            device_id=(mx, my, mc), device_id_type=MESH_ID))
    return own, sends, arrivals


def _exchange_start(copies):
    own, sends, _ = copies
    own.start()
    for cp in sends:
        cp.start()


def _exchange_wait(copies):
    own, sends, arrivals = copies
    for cp in arrivals:
        cp.wait_recv()
    for cp in sends:
        cp.wait_send()
    own.wait()


_EXCHANGE_SCRATCH = [pltpu.SemaphoreType.DMA((N_DEV - 1,)), pltpu.SemaphoreType.DMA((N_DEV - 1,)), pltpu.SemaphoreType.DMA]


def _exchange(x, *, gather, name):
    blk = x.shape if gather else x.shape[1:]

    def body(x_ref, o_ref, send_sems, recv_sems, local_sem):
        copies = _exchange_copies(x_ref, o_ref, send_sems, recv_sems, local_sem, gather)
        _exchange_start(copies)
        _exchange_wait(copies)

    return pl.pallas_call(
        body, name=name,
        in_specs=[pl.BlockSpec(memory_space=pltpu.HBM)], out_specs=pl.BlockSpec(memory_space=pltpu.HBM),
        out_shape=jax.ShapeDtypeStruct((N_DEV,) + tuple(blk), x.dtype),
        scratch_shapes=list(_EXCHANGE_SCRATCH),
    )(x)


def _reduce_adamw(parts, w, m, v, *, name):
    r = w.shape[0]
    tr = max(d for d in range(16, 3201, 16) if r % d == 0)

    def body(p_ref, w_ref, m_ref, v_ref, g_ref, d_ref, nm_ref, nv_ref):
        g = p_ref[0].astype(f32)
        for s in range(1, N_DEV):
            g = g + p_ref[s].astype(f32)
        mm = ADAM_B1 * m_ref[...] + (1.0 - ADAM_B1) * g
        vv = ADAM_B2 * v_ref[...] + (1.0 - ADAM_B2) * (g * g)
        m_hat = mm / (1.0 - ADAM_B1 ** ADAM_STEP)
        v_hat = vv / (1.0 - ADAM_B2 ** ADAM_STEP)
        g_ref[...] = g
        d_ref[...] = -ADAM_LR * (m_hat / (jnp.sqrt(v_hat) + ADAM_EPS) + ADAM_WD * w_ref[...])
        nm_ref[...] = mm
        nv_ref[...] = vv

    spec = pl.BlockSpec((tr, 128), lambda i: (i, 0))
    return pl.pallas_call(
        body, name=name, grid=(r // tr,),
        in_specs=[pl.BlockSpec((N_DEV, tr, 128), lambda i: (0, i, 0)), spec, spec, spec],
        out_specs=[spec] * 4, out_shape=[jax.ShapeDtypeStruct((r, 128), f32)] * 4,
    )(parts, w, m, v)


_SHARDED = ("gdn_w_in", "gdn_w_out", "kv_w_down", "kv_w_up", "mla_w_in", "mla_w_q_up", "mla_w_out", "meta_tokens", "gdn_conv_w")
_COL_SHARDED = {"gdn_w_in", "kv_w_up", "mla_w_in", "mla_w_q_up", "meta_tokens", "gdn_conv_w"}
_GATHER_FIRST = ("gdn_w_in",)
_GATHER_F32 = ("meta_tokens", "gdn_conv_w")
_GATHER_REST = ("gdn_w_out", "kv_w_down", "kv_w_up", "mla_w_in", "mla_w_q_up", "mla_w_out")
_SCATTER_EARLY = ("gdn_w_out", "kv_w_down", "kv_w_up", "mla_w_in", "mla_w_q_up", "mla_w_out")
_SCATTER_LATE = ("gdn_w_in", "gdn_conv_w")
_SCATTER_LAST = ("meta_tokens",)
_REPLICATED = ("pre_norm", "post_norm", "gdn_a_log", "gdn_dt_bias", "gdn_out_norm", "kv_norm", "kv_latent_norm",
               "mla_q_latent_norm")


def _rows128(a):
    flat = a.reshape(-1)
    pad = (-flat.shape[0]) % 128
    if pad:
        flat = jnp.pad(flat, (0, pad))
    return flat.reshape(-1, 128)


def _pack(arrs, row_multiple):
    parts = [_rows128(a) for a in arrs]
    buf = jnp.concatenate(parts, axis=0)
    pad = (-buf.shape[0]) % row_multiple
    if pad:
        buf = jnp.pad(buf, ((0, pad), (0, 0)))
    return buf


def _unpack(buf, shapes):
    out, r = [], 0
    for shp in shapes:
        n = math.prod(shp)
        rows = -(-n // 128)
        out.append(buf[r:r + rows].reshape(-1)[:n].reshape(shp))
        r += rows
    return out


def _unshard(g, full_shape, col):
    if col:
        return jnp.transpose(g, (1, 0, 2)).reshape(full_shape)
    return g.reshape(full_shape)


def _to_shards(a, col):
    r, c = a.shape
    if col:
        return jnp.transpose(a.reshape(r, N_DEV, c // N_DEV), (1, 0, 2))
    return a.reshape(N_DEV, r // N_DEV, c)


def _pad_cols(a, width):
    return jnp.pad(a, ((0, 0), (0, width - a.shape[1])))


def _rope_tables(lp):
    inv = ROPE_THETA ** (-jnp.arange(0, MLA_ROPE, 2, dtype=f32) / MLA_ROPE)
    pos = (jnp.arange(lp, dtype=jnp.int32) - PAD_FRONT).astype(f32)
    ang = pos[:, None] * inv[None, :]
    cos, sin = jnp.cos(ang), jnp.sin(ang)
    z = jnp.zeros((lp, 64), f32)
    return jnp.concatenate([cos, cos, z], axis=1), jnp.concatenate([-sin, sin, z], axis=1)


def kernel(x, meta_tokens, pre_norm, post_norm, gdn_w_in, gdn_conv_w, gdn_a_log, gdn_dt_bias, gdn_out_norm, gdn_w_out, kv_norm, kv_w_down, kv_latent_norm, kv_w_up, mla_w_in, mla_q_latent_norm, mla_w_q_up, mla_w_out, loss_target, m_meta_tokens, m_pre_norm, m_post_norm, m_gdn_w_in, m_gdn_conv_w, m_gdn_a_log, m_gdn_dt_bias, m_gdn_out_norm, m_gdn_w_out, m_kv_norm, m_kv_w_down, m_kv_latent_norm, m_kv_w_up, m_mla_w_in, m_mla_q_latent_norm, m_mla_w_q_up, m_mla_w_out, v_meta_tokens, v_pre_norm, v_post_norm, v_gdn_w_in, v_gdn_conv_w, v_gdn_a_log, v_gdn_dt_bias, v_gdn_out_norm, v_gdn_w_out, v_kv_norm, v_kv_w_down, v_kv_latent_norm, v_kv_w_up, v_mla_w_in, v_mla_q_latent_norm, v_mla_w_q_up, v_mla_w_out):
    W = dict(meta_tokens=meta_tokens, pre_norm=pre_norm, post_norm=post_norm, gdn_w_in=gdn_w_in, gdn_conv_w=gdn_conv_w,
             gdn_a_log=gdn_a_log, gdn_dt_bias=gdn_dt_bias, gdn_out_norm=gdn_out_norm, gdn_w_out=gdn_w_out, kv_norm=kv_norm,
             kv_w_down=kv_w_down, kv_latent_norm=kv_latent_norm, kv_w_up=kv_w_up, mla_w_in=mla_w_in,
             mla_q_latent_norm=mla_q_latent_norm, mla_w_q_up=mla_w_q_up, mla_w_out=mla_w_out)
    M = dict(meta_tokens=m_meta_tokens, pre_norm=m_pre_norm, post_norm=m_post_norm, gdn_w_in=m_gdn_w_in, gdn_conv_w=m_gdn_conv_w,
             gdn_a_log=m_gdn_a_log, gdn_dt_bias=m_gdn_dt_bias, gdn_out_norm=m_gdn_out_norm, gdn_w_out=m_gdn_w_out, kv_norm=m_kv_norm,
             kv_w_down=m_kv_w_down, kv_latent_norm=m_kv_latent_norm, kv_w_up=m_kv_w_up, mla_w_in=m_mla_w_in,
             mla_q_latent_norm=m_mla_q_latent_norm, mla_w_q_up=m_mla_w_q_up, mla_w_out=m_mla_w_out)
    V = dict(meta_tokens=v_meta_tokens, pre_norm=v_pre_norm, post_norm=v_post_norm, gdn_w_in=v_gdn_w_in, gdn_conv_w=v_gdn_conv_w,
             gdn_a_log=v_gdn_a_log, gdn_dt_bias=v_gdn_dt_bias, gdn_out_norm=v_gdn_out_norm, gdn_w_out=v_gdn_w_out, kv_norm=v_kv_norm,
             kv_w_down=v_kv_w_down, kv_latent_norm=v_kv_latent_norm, kv_w_up=v_kv_w_up, mla_w_in=v_mla_w_in,
             mla_q_latent_norm=v_mla_q_latent_norm, mla_w_q_up=v_mla_w_q_up, mla_w_out=v_mla_w_out)
    order = list(W)

    n_tok = x.shape[1]
    assert n_tok % GDN_CHUNK == 0
    n_real = ROW0 + n_tok
    lp = -(-n_real // ROW_TILE) * ROW_TILE
    n_real_chunks = n_real // GDN_CHUNK

    shard2d = {n: W[n].reshape(W[n].shape[-2:]) for n in _SHARDED}
    full_shape = {n: ((s.shape[0], s.shape[1] * N_DEV) if n in _COL_SHARDED else (s.shape[0] * N_DEV, s.shape[1]))
                  for n, s in shard2d.items()}
    full = {}

    def unpack_gathered(names, buf):
        r = 0
        for n in names:
            shp = shard2d[n].shape
            rows = math.prod(shp) // 128
            blocks = buf[:, r:r + rows].reshape((N_DEV,) + shp)
            full[n] = _unshard(blocks, full_shape[n], n in _COL_SHARDED)
            r += rows

    unpack_gathered(_GATHER_FIRST, _exchange(_pack([shard2d[n].astype(bf16) for n in _GATHER_FIRST], 16), gather=True,
                                             name="gather_w_in"))
    unpack_gathered(_GATHER_F32, _exchange(_pack([shard2d[n] for n in _GATHER_F32], 8), gather=True, name="gather_meta_conv"))
    rest_shards = _pack([shard2d[n].astype(bf16) for n in _GATHER_REST], 16)

    h0 = jnp.concatenate([jnp.zeros((PAD_FRONT, D_MODEL), f32), full["meta_tokens"], x[0],
                          jnp.zeros((lp - n_real, D_MODEL), f32)], axis=0)
    tgt = jnp.concatenate([jnp.zeros((ROW0, D_MODEL), f32), loss_target[0], jnp.zeros((lp - n_real, D_MODEL), f32)], axis=0)
    cos_k, sin_k = _rope_tables(lp)
    one = jnp.ones((lp, 128), f32)
    cos_q = jnp.concatenate([one, cos_k], axis=1)
    sin_q = jnp.concatenate([jnp.zeros((lp, 128), f32), sin_k], axis=1)

    w_in = full["gdn_w_in"]
    s1 = GDN_CONV_W + GDN_V_W
    w_in_p = jnp.concatenate([w_in[:, :s1], _pad_cols(w_in[:, s1:s1 + 16], 128), _pad_cols(w_in[:, s1 + 16:], 128)], axis=1)
    pre0, pre1 = pre_norm[0:1], pre_norm[1:2]
    post0, post1 = post_norm[0:1], post_norm[1:2]
    (hn0,) = _rowwise(_st_prenorm, [(h0, None, 0)], [], [pre0], [(D_MODEL, bf16, None)], name="f_prenorm0")
    proj, g_rest = _matmul(hn0, w_in_p, out_dtype=bf16, name="f_gdn_in", side=(rest_shards, True))
    unpack_gathered(_GATHER_REST, g_rest)

    wd = full["kv_w_down"]
    zc = jnp.zeros((D_MODEL, 64), bf16)
    wd2 = jnp.concatenate([wd, zc, jnp.zeros((D_MODEL, 128), bf16), wd[:, 160:192], wd[:, 128:160], zc], axis=1)
    wup_p = jnp.transpose(full["kv_w_up"].reshape(MLA_KV_RANK, MLA_HEADS, 2, 128), (0, 2, 1, 3)).reshape(MLA_KV_RANK, 2 * MLA_V_W)
    wq = full["mla_w_q_up"].reshape(MLA_Q_RANK, MLA_HEADS, MLA_QK)
    zq64 = jnp.zeros((MLA_Q_RANK, MLA_HEADS, 64), bf16)
    wq_plain = jnp.concatenate([wq, zq64], axis=2).reshape(MLA_Q_RANK, MLA_HEADS * MLA_QKP)
    wq_swap = jnp.concatenate([jnp.zeros((MLA_Q_RANK, MLA_HEADS, 128), bf16), wq[:, :, 160:192], wq[:, :, 128:160], zq64],
                              axis=2).reshape(MLA_Q_RANK, MLA_HEADS * MLA_QKP)
    q_half = Q_GROUP * MLA_QKP
    wq2 = jnp.concatenate([wq_plain[:, :q_half], wq_swap[:, :q_half], wq_plain[:, q_half:], wq_swap[:, q_half:]], axis=1)
    w_mla_in, w_gdn_out, w_mla_out = full["mla_w_in"], full["gdn_w_out"], full["mla_w_out"]
    conv_w = full["gdn_conv_w"]
    alog_p, dtb_p = _pad_cols(gdn_a_log, 128), _pad_cols(gdn_dt_bias, 128)
    kvn, kvln = kv_norm.reshape(1, -1), kv_latent_norm.reshape(1, -1)

    conv = _conv_fwd(proj, conv_w, col_blocks=GDN_CONV_W // CONV_BC, name="f_conv")
    (qn,) = _rowwise(_st_gdn_q, [(conv, GDN_QK_W, 0)], [], [], [(GDN_QK_W, bf16, GDN_QK_W)], heads=GDN_QK_HEADS, name="f_gdn_q")
    (kn,) = _rowwise(_st_gdn_k, [(conv, GDN_QK_W, 1)], [], [], [(GDN_QK_W, bf16, GDN_QK_W)], heads=GDN_QK_HEADS, name="f_gdn_k")
    (vv,) = _rowwise(_st_gdn_v, [(conv, GDN_V_W, 1)], [], [], [(GDN_V_W, bf16, GDN_V_W)], name="f_gdn_v")
    gate_rows = [(proj, 128, s1 // 128), (proj, 128, s1 // 128 + 1)]
    beta, gdec = _rowwise(_st_gdn_gate, gate_rows, [], [alog_p, dtb_p], [(128, f32, None)] * 2, name="f_gdn_gate")
    o_gdn, states, inverses = _gdn_fwd(qn, kn, vv, beta, gdec, n_real_chunks=n_real_chunks, name="f_gdn")
    out_rows = [(o_gdn, GDN_V_W, 0), (proj, GDN_V_W, GDN_CONV_W // GDN_V_W)]
    (og,) = _rowwise(_st_gdn_out, out_rows, [], [gdn_out_norm], [(GDN_V_W, bf16, GDN_V_W)], heads=GDN_V_HEADS, name="f_gdn_out")
    y0 = _matmul(og, w_gdn_out, name="f_gdn_wout")
    mid_rows = [(h0, None, 0), (y0, None, 0)]
    h1, hn1, hkv = _rowwise(_st_mid, mid_rows, [], [post0, pre1, kvn],
                            [(D_MODEL, f32, None), (D_MODEL, bf16, None), (D_MODEL, bf16, None)], name="f_mid")
    ckr = _matmul(hkv, wd2, name="f_kv_down")
    proj2 = _matmul(hn1, w_mla_in, out_dtype=bf16, name="f_mla_in")
    lat_rows = [(ckr, None, 0), (proj2, MLA_Q_RANK, 0)]
    lat_nd = [(cos_k, None, 0), (sin_k, None, 0)]
    c_kv, k_rope, c_q = _rowwise(_st_latent, lat_rows, lat_nd, [kvln, mla_q_latent_norm],
                                 [(128, bf16, None), (128, bf16, None), (MLA_Q_RANK, bf16, None)], name="f_latent")
    kvu = _matmul(c_kv, wup_p, out_dtype=bf16, name="f_kv_up")
    q_att = _q_proj_fwd(c_q, wq2, cos_q, sin_q, name="f_q_proj")
    o_att, lse, og2 = _attention_fwd(q_att, kvu, k_rope, proj2, name="f_attention")
    y1 = _matmul(og2, w_mla_out, name="f_mla_wout")
    st_loss = _make_st_loss(n_tok)
    loss_rows_in = [(h1, None, 0), (y1, None, 0)]
    (loss_rows,) = _rowwise(st_loss, loss_rows_in, [(tgt, None, 0)], [post1], [(1, f32, None)], name="f_loss")
    loss_here = jnp.sum(loss_rows)

    ones_ct = jnp.ones((lp, 1), f32)
    (dh1_a, dy1), (dpost1,) = _rowwise_vjp(st_loss, loss_rows_in, [(tgt, None, 0)], [post1], [(ones_ct, None, 0)], name="b_loss")
    dog2 = _matmul(dy1, w_mla_out, tb=True, name="b_mla_wout_x")
    dw_mla_out = _matmul(og2, dy1, ta=True, name="b_mla_wout_w")
    do_att, dz2, delta = _gate_bwd(o_att, proj2, dog2, name="b_mla_gate")
    dq_att, dkn, dvv, dkr_h = _attention_bwd(q_att, kvu, k_rope, lse, delta, do_att, name="b_attention")
    dc_q, dwq2 = _q_proj_bwd(dq_att, c_q, wq2, cos_q, sin_q, name="b_q_proj")
    dkvu = jnp.concatenate([dkn, dvv], axis=1)
    dc_kv = _matmul(dkvu, wup_p, tb=True, name="b_kv_up_x")
    dwup_p = _matmul(c_kv, dkvu, ta=True, name="b_kv_up_w")

    def lat_ct(cv):
        dkr = cv[1][:, 0:128]
        for h in range(1, MLA_HEADS):
            dkr = dkr + cv[1][:, h * 128:(h + 1) * 128]
        return [cv[0], dkr, cv[2][:, :MLA_Q_RANK] + cv[2][:, MLA_Q_RANK:]]

    (dckr, dcq_pre), (dkvln, dqln) = _rowwise_vjp(
        _st_latent, lat_rows, lat_nd, [kvln, mla_q_latent_norm],
        [(dc_kv, None, 0), (dkr_h, None, 0), (dc_q, None, 0)], ct_pre=lat_ct, name="b_latent", grad_dtypes=[bf16, bf16])
    dproj2 = jnp.concatenate([dcq_pre, dz2], axis=1)
    dhn1 = _matmul(dproj2, w_mla_in, tb=True, name="b_mla_in_x")
    dw_mla_in = _matmul(hn1, dproj2, ta=True, name="b_mla_in_w")
    dhkv = _matmul(dckr, wd2, tb=True, name="b_kv_down_x")
    dwd2 = _matmul(hkv, dckr, ta=True, name="b_kv_down_w")
    (dh0_a, dy0), (dpost0, dpre1, dkvn) = _rowwise_vjp(
        _st_mid, mid_rows, [], [post0, pre1, kvn], [(dh1_a, None, 0), (dhn1, None, 0), (dhkv, None, 0)], name="b_mid")
    dog = _matmul(dy0, w_gdn_out, tb=True, name="b_gdn_wout_x")
    dw_gdn_out = _matmul(og, dy0, ta=True, name="b_gdn_wout_w")
    (do_gdn, dz), (dout_norm,) = _rowwise_vjp(_st_gdn_out, out_rows, [], [gdn_out_norm], [(dog, GDN_V_W, 0)], heads=GDN_V_HEADS,
                                              name="b_gdn_out", grad_dtypes=[f32, bf16])
    dq_g, dk_g, dv_g, dbeta, dgdec = _gdn_bwd(qn, kn, vv, beta, gdec, states, inverses, do_gdn, n_real_chunks=n_real_chunks,
                                              name="b_gdn")
    (db_col, da_col), (dalog_p, ddtb_p) = _rowwise_vjp(
        _st_gdn_gate, gate_rows, [], [alog_p, dtb_p], [(dbeta, None, 0), (dgdec, None, 0)], name="b_gdn_gate",
        grad_dtypes=[bf16, bf16])
    (dconv_q,), _ = _rowwise_vjp(_st_gdn_q, [(conv, GDN_QK_W, 0)], [], [], [(dq_g, GDN_QK_W, 0)], heads=GDN_QK_HEADS, name="b_gdn_q")
    (dconv_k,), _ = _rowwise_vjp(_st_gdn_k, [(conv, GDN_QK_W, 1)], [], [], [(dk_g, GDN_QK_W, 0)], heads=GDN_QK_HEADS, name="b_gdn_k")
    (dconv_v,), _ = _rowwise_vjp(_st_gdn_v, [(conv, GDN_V_W, 1)], [], [], [(dv_g, GDN_V_W, 0)], name="b_gdn_v")
    nq_b = GDN_QK_W // CONV_BC
    dpre_q, dcw_q = _conv_bwd(dconv_q, proj, conv_w, x_off=0, w_off=0, name="b_conv_q")
    dpre_k, dcw_k = _conv_bwd(dconv_k, proj, conv_w, x_off=nq_b, w_off=nq_b, name="b_conv_k")
    dpre_v, dcw_v = _conv_bwd(dconv_v, proj, conv_w, x_off=2 * nq_b, w_off=2 * nq_b, name="b_conv_v")
    dproj = jnp.concatenate([dpre_q, dpre_k, dpre_v, dz, db_col, da_col], axis=1)
    G = {}
    G["kv_w_down"] = jnp.concatenate([dwd2[:, :128], dwd2[:, 128:160] + dwd2[:, 416:448], dwd2[:, 160:192] + dwd2[:, 384:416]], axis=1)
    G["kv_w_up"] = jnp.transpose(dwup_p.reshape(MLA_KV_RANK, 2, MLA_HEADS, 128), (0, 2, 1, 3)).reshape(MLA_KV_RANK, 2 * MLA_V_W)
    G["mla_w_in"] = dw_mla_in
    dq4 = dwq2.reshape(MLA_Q_RANK, 2, 2, Q_GROUP, MLA_QKP)
    dqp = dq4[:, :, 0].reshape(MLA_Q_RANK, MLA_HEADS, MLA_QKP)
    dqs = dq4[:, :, 1].reshape(MLA_Q_RANK, MLA_HEADS, MLA_QKP)
    G["mla_w_q_up"] = jnp.concatenate([dqp[:, :, :128], dqp[:, :, 128:160] + dqs[:, :, 160:192],
                                       dqp[:, :, 160:192] + dqs[:, :, 128:160]], axis=2).reshape(MLA_Q_RANK, MLA_HEADS * MLA_QK)
    G["mla_w_out"] = dw_mla_out
    G["gdn_w_out"] = dw_gdn_out

    def shards_to_send(names):
        return jnp.concatenate([_to_shards(G[n], n in _COL_SHARDED).reshape(N_DEV, -1, 128).astype(bf16) for n in names], axis=1)

    dw_in_p, parts_early = _matmul(hn0, dproj, ta=True, name="b_gdn_in_w", side=(shards_to_send(_SCATTER_EARLY), False))
    G["gdn_w_in"] = jnp.concatenate([dw_in_p[:, :s1 + 16], dw_in_p[:, s1 + 128:s1 + 144]], axis=1)
    G["gdn_conv_w"] = jnp.concatenate([dcw_q, dcw_k, dcw_v], axis=1)
    dhn0, parts_late = _matmul(dproj, w_in_p, tb=True, name="b_gdn_in_x", side=(shards_to_send(_SCATTER_LATE), False))
    (dh0,), (dpre0,) = _rowwise_vjp(_st_prenorm, [(h0, None, 0)], [], [pre0], [(dhn0, None, 0)], extra=[dh0_a], name="b_prenorm0")

    grad_x = dh0[ROW0:n_real][None]
    G["meta_tokens"] = dh0[PAD_FRONT:ROW0]
    G["pre_norm"] = jnp.concatenate([dpre0, dpre1], axis=0)
    G["post_norm"] = jnp.concatenate([dpost0, dpost1], axis=0)
    G["gdn_a_log"] = dalog_p[:, :GDN_V_HEADS]
    G["gdn_dt_bias"] = ddtb_p[:, :GDN_V_HEADS]
    G["gdn_out_norm"] = dout_norm
    G["kv_norm"] = dkvn.reshape(-1)
    G["kv_latent_norm"] = dkvln.reshape(-1)
    G["mla_q_latent_norm"] = dqln

    parts_last = _exchange(shards_to_send(_SCATTER_LAST), gather=False, name="scatter_meta_grads")
    G["loss"] = loss_here.reshape(1, 1)
    for d in (W, M, V):
        d["loss"] = jnp.zeros((1, 1), f32)
    replicated = _REPLICATED + ("loss",)
    parts_r = _exchange(_pack([G[n] for n in replicated], 8), gather=True, name="gather_small_grads")
    outs = {}
    for names, parts, tag in ((_SCATTER_EARLY, parts_early, "early"), (_SCATTER_LATE, parts_late, "late"),
                              (_SCATTER_LAST, parts_last, "last"), (replicated, parts_r, "replicated")):
        w_p, m_p, v_p = (_pack([d[n] for n in names], 8) for d in (W, M, V))
        res = _reduce_adamw(parts, w_p, m_p, v_p, name="adamw_" + tag)
        for kind, buf in zip(("grad", "delta", "new_m", "new_v"), res):
            for n, a in zip(names, _unpack(buf, [W[n].shape for n in names])):
                outs[kind, n] = a
    loss = outs["grad", "loss"].reshape(())
    return (loss, grad_x, *[outs[k, n] for k in ("grad", "delta", "new_m", "new_v") for n in order])
```

```python
import functools
import math

import jax
import jax.numpy as jnp
from jax import lax
from jax.experimental import pallas as pl
from jax.experimental.pallas import tpu as pltpu

f32, bf16 = jnp.float32, jnp.bfloat16
MESH_ID = pl.DeviceIdType.MESH

N_DEV = 8
D_MODEL = 1024
N_META = 16
NORM_EPS = 1e-6
PAD_FRONT = 48
ROW0 = PAD_FRONT + N_META
GDN_QK_HEADS, GDN_V_HEADS, GDN_D = 8, 16, 128
GDN_CHUNK = 64
GDN_QK_W, GDN_V_W = GDN_QK_HEADS * GDN_D, GDN_V_HEADS * GDN_D
GDN_CONV_W = 2 * GDN_QK_W + GDN_V_W
GDN_IN_W = GDN_CONV_W + GDN_V_W + 2 * GDN_V_HEADS
GDN_IN_WP = GDN_CONV_W + GDN_V_W + 2 * 128
MLA_HEADS, MLA_NOPE, MLA_ROPE, MLA_V = 16, 128, 64, 128
MLA_Q_RANK, MLA_KV_RANK = 256, 128
MLA_QK = MLA_NOPE + MLA_ROPE
MLA_QKP = 256
MLA_V_W = MLA_HEADS * MLA_V
ROPE_THETA = 10000.0
NEG = -1e30
ROW_TILE = 256
ATT_KEY_TILES = (384, 256)

ADAM_LR, ADAM_B1, ADAM_B2, ADAM_EPS, ADAM_WD, ADAM_STEP = 0.001, 0.9, 0.999, 1e-08, 0.01, 10

NN = ((1,), (0,))
NT = ((1,), (1,))
TN = ((0,), (0,))


def _pick(dim, prefs):
    for p in prefs:
        if dim % p == 0:
            return p
    return dim


def _dlo(a, b, dims):
    return lax.dot_general(a.astype(bf16), b.astype(bf16), (dims, ((), ())), preferred_element_type=f32)


def _dsel(a, b, dims, selector):
    x = b if selector == 0 else a
    hi = x.astype(bf16)
    rest = x - hi.astype(f32)
    mid = rest.astype(bf16)
    low = (rest - mid.astype(f32)).astype(bf16)
    sel = (a if selector == 0 else b).astype(bf16)
    d = (lambda p: lax.dot_general(sel, p, (dims, ((), ())), preferred_element_type=f32)) if selector == 0 else \
        (lambda p: lax.dot_general(p, sel, (dims, ((), ())), preferred_element_type=f32))
    return d(hi) + (d(mid) + d(low))


def _matmul(a, b, *, ta=False, tb=False, out_dtype=f32, name, side=None):
    assert not (ta and tb)
    if ta:
        kdim, m = a.shape
    else:
        m, kdim = a.shape
    n = b.shape[0] if tb else b.shape[1]
    assert (b.shape[1] if tb else b.shape[0]) == kdim
    tm = _pick(m, (1024, 768, 512, 384, 256, 128))
    tn = _pick(n, (1024, 768, 640, 512, 256, 128))
    tk = _pick(kdim, (1024, 768, 640, 512, 256, 128))
    nk = kdim // tk
    dims = TN if ta else (NT if tb else NN)

    grid = (m // tm, n // tn, nk)

    def product(a_ref, b_ref, o_ref, acc_ref):
        k = pl.program_id(2)

        @pl.when(k == 0)
        def _():
            acc_ref[...] = jnp.zeros_like(acc_ref)

        acc_ref[...] += lax.dot_general(a_ref[...].astype(bf16), b_ref[...].astype(bf16), (dims, ((), ())),
                                        preferred_element_type=f32)

        @pl.when(k == nk - 1)
        def _():
            o_ref[...] = acc_ref[...].astype(o_ref.dtype)

    a_spec = pl.BlockSpec((tk, tm), lambda i, j, k: (k, i)) if ta else pl.BlockSpec((tm, tk), lambda i, j, k: (i, k))
    b_spec = pl.BlockSpec((tn, tk), lambda i, j, k: (j, k)) if tb else pl.BlockSpec((tk, tn), lambda i, j, k: (k, j))
    o_spec = pl.BlockSpec((tm, tn), lambda i, j, k: (i, j))
    o_shape = jax.ShapeDtypeStruct((m, n), out_dtype)
    if side is None:
        def body(a_ref, b_ref, o_ref, acc_ref):
            product(a_ref, b_ref, o_ref, acc_ref)

        return pl.pallas_call(
            body, name=name, grid=grid, in_specs=[a_spec, b_spec], out_specs=o_spec, out_shape=o_shape,
            scratch_shapes=[pltpu.VMEM((tm, tn), f32)],
            compiler_params=pltpu.CompilerParams(dimension_semantics=("parallel", "parallel", "arbitrary")),
        )(a, b)

    x, gather = side
    blk = x.shape if gather else x.shape[1:]

    def body_with_exchange(a_ref, b_ref, x_ref, o_ref, xo_ref, acc_ref, send_sems, recv_sems, local_sem):
        step = (pl.program_id(0) * grid[1] + pl.program_id(1)) * grid[2] + pl.program_id(2)
        copies = _exchange_copies(x_ref, xo_ref, send_sems, recv_sems, local_sem, gather)

        @pl.when(step == 0)
        def _():
            _exchange_start(copies)

        product(a_ref, b_ref, o_ref, acc_ref)

        @pl.when(step == grid[0] * grid[1] * grid[2] - 1)
        def _():
            _exchange_wait(copies)

    hbm = pl.BlockSpec(memory_space=pltpu.HBM)
    return pl.pallas_call(
        body_with_exchange, name=name, grid=grid, in_specs=[a_spec, b_spec, hbm], out_specs=[o_spec, hbm],
        out_shape=[o_shape, jax.ShapeDtypeStruct((N_DEV,) + tuple(blk), x.dtype)],
        scratch_shapes=[pltpu.VMEM((tm, tn), f32)] + list(_EXCHANGE_SCRATCH),
        compiler_params=pltpu.CompilerParams(dimension_semantics=("arbitrary", "arbitrary", "arbitrary")),
    )(a, b, x)


def _row_spec(item, tr):
    a, bc, off = item
    if bc is None:
        return pl.BlockSpec((tr, a.shape[1]), lambda i, j: (i, 0))
    return pl.BlockSpec((tr, bc), lambda i, j, off=off: (i, j + off))


def _param_spec(p):
    return pl.BlockSpec(p.shape, lambda i, j: (0, 0))


def _row_tile(lp, items):
    widest = max(a.shape[1] if bc is None else bc for (a, bc, _) in items)
    return ROW_TILE if widest >= 1024 else _pick(lp, (768, 512, 256))


def _head_cols(tiles, h, heads):
    return [x[:, h * (x.shape[1] // heads):(h + 1) * (x.shape[1] // heads)] for x in tiles]


def _rowwise(fn, rows, nodiff, params, outs, *, ncol=1, heads=1, name):
    lp = rows[0][0].shape[0]
    tr = _row_tile(lp, rows)
    nr, nd = len(rows), len(nodiff)

    def body(*refs):
        rv = [r[...].astype(f32) for r in refs[:nr]]
        nv = [r[...] for r in refs[nr:nr + nd]]
        pv = [r[...] for r in refs[nr + nd:nr + nd + len(params)]]
        per_head = [fn(_head_cols(rv, h, heads), nv, pv) for h in range(heads)]
        res = [jnp.concatenate(list(vals), axis=1) if heads > 1 else vals[0] for vals in zip(*per_head)]
        for ref, val in zip(refs[nr + nd + len(params):], res):
            ref[...] = val.astype(ref.dtype)

    out_specs = [pl.BlockSpec((tr, c if bc is None else bc), (lambda i, j: (i, 0)) if bc is None else (lambda i, j: (i, j)))
                 for (c, _, bc) in outs]
    return pl.pallas_call(
        body, name=name, grid=(lp // tr, ncol),
        in_specs=[_row_spec(it, tr) for it in rows + nodiff] + [_param_spec(p) for p in params],
        out_specs=out_specs,
        out_shape=[jax.ShapeDtypeStruct((lp, c), dt) for (c, dt, _) in outs],
    )(*[it[0] for it in rows + nodiff], *params)


def _rowwise_vjp(fn, rows, nodiff, params, cts, *, ncol=1, heads=1, name, ct_pre=None, extra=None, grad_dtypes=None):
    lp = rows[0][0].shape[0]
    tr = _row_tile(lp, rows)
    nr, nd, npar, nct = len(rows), len(nodiff), len(params), len(cts)
    extra = extra or [None] * nr
    grad_dtypes = grad_dtypes or [f32] * nr
    ex_items = [(e, rows[k][1], 0) for k, e in enumerate(extra) if e is not None]
    ex_pos = [k for k, e in enumerate(extra) if e is not None]
    for (a, bc, _) in rows:
        assert bc is not None or ncol == 1

    def body(*refs):
        pos = 0
        rv = [r[...].astype(f32) for r in refs[pos:pos + nr]]; pos += nr
        nv = [r[...] for r in refs[pos:pos + nd]]; pos += nd
        pv = [r[...] for r in refs[pos:pos + npar]]; pos += npar
        cv = [r[...].astype(f32) for r in refs[pos:pos + nct]]; pos += nct
        ev = [r[...].astype(f32) for r in refs[pos:pos + len(ex_items)]]; pos += len(ex_items)
        drow_refs = refs[pos:pos + nr]; pos += nr
        dpar_refs = refs[pos:pos + npar]
        ctv = ct_pre(cv) if ct_pre is not None else cv
        drow_h, dpar = [], None
        for h in range(heads):
            outs, vjp_fn = jax.vjp(lambda rr, pp: fn(rr, nv, pp), _head_cols(rv, h, heads), pv)
            dr, dp = vjp_fn([c.astype(o.dtype) for c, o in zip(_head_cols(ctv, h, heads), outs)])
            drow_h.append(dr)
            dpar = dp if dpar is None else [a + b for a, b in zip(dpar, dp)]
        drow = [jnp.concatenate(list(vals), axis=1) if heads > 1 else vals[0] for vals in zip(*drow_h)]
        for k, e in zip(ex_pos, ev):
            drow[k] = drow[k] + e
        for ref, val in zip(drow_refs, drow):
            ref[...] = val.astype(ref.dtype)
        first = jnp.logical_and(pl.program_id(0) == 0, pl.program_id(1) == 0)

        @pl.when(first)
        def _():
            for ref, val in zip(dpar_refs, dpar):
                ref[...] = val

        @pl.when(jnp.logical_not(first))
        def _():
            for ref, val in zip(dpar_refs, dpar):
                ref[...] += val

    drow_shapes, drow_specs = [], []
    for (a, bc, _), dt in zip(rows, grad_dtypes):
        if bc is None:
            drow_shapes.append(jax.ShapeDtypeStruct((lp, a.shape[1]), dt))
            drow_specs.append(pl.BlockSpec((tr, a.shape[1]), lambda i, j: (i, 0)))
        else:
            drow_shapes.append(jax.ShapeDtypeStruct((lp, ncol * bc), dt))
            drow_specs.append(pl.BlockSpec((tr, bc), lambda i, j: (i, j)))
    res = pl.pallas_call(
        body, name=name, grid=(lp // tr, ncol),
        in_specs=[_row_spec(it, tr) for it in rows + nodiff] + [_param_spec(p) for p in params]
        + [_row_spec(it, tr) for it in cts + ex_items],
        out_specs=drow_specs + [_param_spec(p) for p in params],
        out_shape=drow_shapes + [jax.ShapeDtypeStruct(p.shape, f32) for p in params],
        compiler_params=pltpu.CompilerParams(dimension_semantics=("arbitrary", "arbitrary")),
    )(*[it[0] for it in rows + nodiff], *params, *[it[0] for it in cts + ex_items])
    return res[:nr], res[nr:]


def _rms(x, g):
    return x * lax.rsqrt(jnp.mean(x * x, axis=-1, keepdims=True) + NORM_EPS) * g


def _l2n(x):
    return x * lax.rsqrt(jnp.sum(x * x, axis=-1, keepdims=True) + NORM_EPS)


def _sigmoid(x):
    return 1.0 / (1.0 + jnp.exp(-x))


def _silu(x):
    return x * _sigmoid(x)


def _softplus(x):
    return jnp.maximum(x, 0.0) + jnp.log(1.0 + jnp.exp(-jnp.abs(x)))


def _row_ids(shape):
    return pl.program_id(0) * shape[0] + lax.broadcasted_iota(jnp.int32, shape, 0)


def _st_prenorm(r, n, p):
    return [_rms(r[0], p[0])]


def _st_gdn_q(r, n, p):
    return [_l2n(_silu(r[0])) * (GDN_D ** -0.5)]


def _st_gdn_k(r, n, p):
    return [_l2n(_silu(r[0]))]


def _st_gdn_v(r, n, p):
    return [_silu(r[0])]


def _st_gdn_gate(r, n, p):
    real = _row_ids(r[0].shape) >= PAD_FRONT
    beta = jnp.where(real, _sigmoid(r[0]), 0.0)
    g = jnp.where(real, -jnp.exp(p[0]) * _softplus(r[1] + p[1]), 0.0)
    return [beta, g]


def _st_gdn_out(r, n, p):
    return [_rms(r[0], p[0]) * _silu(r[1])]


def _st_mid(r, n, p):
    h1 = r[0] + _rms(r[1], p[0])
    return [h1, _rms(h1, p[1]), _rms(h1, p[2])]


def _st_latent(r, n, p):
    ckr, cq = r
    c_kv = _rms(ckr[:, :MLA_KV_RANK], p[0])
    k_rope = ckr[:, 128:256] * n[0] + ckr[:, 384:512] * n[1]
    return [c_kv, k_rope, _rms(cq, p[1])]


Q_GROUP = 8


def _st_q_rope(r, n, p):
    half = Q_GROUP * MLA_QKP
    out = []
    for h in range(Q_GROUP):
        cols = slice(h * MLA_QKP, (h + 1) * MLA_QKP)
        out.append((r[0][:, :half][:, cols] * n[0] + r[0][:, half:][:, cols] * n[1]) * (MLA_QK ** -0.5))
    return [jnp.concatenate(out, axis=1)]


def _st_q_rope_t(r, n, p):
    plain, swapped = [], []
    for h in range(Q_GROUP):
        ct = r[0][:, h * MLA_QKP:(h + 1) * MLA_QKP] * (MLA_QK ** -0.5)
        plain.append(ct * n[0])
        swapped.append(ct * n[1])
    return [jnp.concatenate(plain + swapped, axis=1)]


def _make_st_loss(n_tokens):
    def st(r, n, p):
        h2 = r[0] + _rms(r[1], p[0])
        rows = _row_ids((r[0].shape[0], 1))
        real = jnp.logical_and(rows >= ROW0, rows < ROW0 + n_tokens)
        err = h2 - n[0]
        return [jnp.where(real, 0.5 * jnp.mean(err * err, axis=-1, keepdims=True), 0.0)]
    return st


CONV_BC = 1024
HALO = 16


def _conv_fwd(x, w, *, col_blocks, name):
    lp = x.shape[0]
    tr = _pick(lp, (768, 512, 256))

    def body(x_ref, xp_ref, w_ref, o_ref):
        i = pl.program_id(0)
        xv = x_ref[...].astype(f32)
        prev = jnp.where(i > 0, xp_ref[...].astype(f32), 0.0)
        xc = jnp.concatenate([prev, xv], axis=0)
        wv = w_ref[...]
        acc = wv[3:4, :] * xv
        for j in range(3):
            acc = acc + wv[j:j + 1, :] * pltpu.roll(xc, 3 - j, 0)[HALO:, :]
        o_ref[...] = acc.astype(o_ref.dtype)

    return pl.pallas_call(
        body, name=name, grid=(lp // tr, col_blocks),
        in_specs=[pl.BlockSpec((tr, CONV_BC), lambda i, j: (i, j)),
                  pl.BlockSpec((HALO, CONV_BC), lambda i, j: (jnp.maximum(i * (tr // HALO) - 1, 0), j)),
                  pl.BlockSpec((4, CONV_BC), lambda i, j: (0, j))],
        out_specs=pl.BlockSpec((tr, CONV_BC), lambda i, j: (i, j)),
        out_shape=jax.ShapeDtypeStruct((lp, col_blocks * CONV_BC), bf16),
    )(x, x, w)


def _conv_bwd(dc, x, w, *, x_off, w_off, name):
    lp, width = dc.shape
    tr = _pick(lp, (768, 512, 256))
    ncb, nrow = width // CONV_BC, lp // tr

    def body(dc_ref, dcn_ref, x_ref, xp_ref, w_ref, dx_ref, dw_ref):
        i = pl.program_id(1)
        nxt = jnp.where(i < nrow - 1, dcn_ref[...], 0.0)
        dcv = dc_ref[...]
        dcc = jnp.concatenate([dcv, nxt], axis=0)
        xv = x_ref[...].astype(f32)
        prev = jnp.where(i > 0, xp_ref[...].astype(f32), 0.0)
        xc = jnp.concatenate([prev, xv], axis=0)
        wv = w_ref[...]
        dx = wv[3:4, :] * dcv
        dws = [None] * 4
        dws[3] = jnp.sum(dcv * xv, axis=0, keepdims=True)
        for j in range(3):
            dx = dx + wv[j:j + 1, :] * pltpu.roll(dcc, tr + 8 - (3 - j), 0)[:tr, :]
            dws[j] = jnp.sum(dcv * pltpu.roll(xc, 3 - j, 0)[HALO:, :], axis=0, keepdims=True)
        dx_ref[...] = dx.astype(dx_ref.dtype)

        @pl.when(i == 0)
        def _():
            for j in range(4):
                dw_ref[j:j + 1, :] = dws[j]

        @pl.when(i > 0)
        def _():
            for j in range(4):
                dw_ref[j:j + 1, :] += dws[j]

    last8 = lp // 8 - 1
    return pl.pallas_call(
        body, name=name, grid=(ncb, nrow),
        in_specs=[pl.BlockSpec((tr, CONV_BC), lambda j, i: (i, j)),
                  pl.BlockSpec((8, CONV_BC), lambda j, i: (jnp.minimum((i + 1) * (tr // 8), last8), j)),
                  pl.BlockSpec((tr, CONV_BC), lambda j, i: (i, j + x_off)),
                  pl.BlockSpec((HALO, CONV_BC), lambda j, i: (jnp.maximum(i * (tr // HALO) - 1, 0), j + x_off)),
                  pl.BlockSpec((4, CONV_BC), lambda j, i: (0, j + w_off))],
        out_specs=[pl.BlockSpec((tr, CONV_BC), lambda j, i: (i, j)),
                   pl.BlockSpec((4, CONV_BC), lambda j, i: (0, j))],
        out_shape=[jax.ShapeDtypeStruct((lp, width), bf16), jax.ShapeDtypeStruct((4, width), f32)],
        compiler_params=pltpu.CompilerParams(dimension_semantics=("arbitrary", "arbitrary")),
    )(dc, dc, x, x, w)


GDN_PACK = 4
GDN_FWD_INTERLEAVE, GDN_BWD_INTERLEAVE = 4, 4


MXU_TILE = 256


def _bd(x, cb, g=GDN_PACK):
    r = x.shape[0]
    tall = jnp.concatenate([x] * g, axis=0)
    rows = lax.broadcasted_iota(jnp.int32, tall.shape, 0) // r
    cols = lax.broadcasted_iota(jnp.int32, tall.shape, 1) // cb
    return jnp.where(rows == cols, tall, jnp.zeros_like(tall))


def _diag(full, r, cb, g=GDN_PACK):
    cols = lax.broadcasted_iota(jnp.int32, (r, full.shape[1]), 1) // cb
    out = jnp.where(cols == 0, full[0:r, :], 0.0)
    for a in range(1, g):
        out = out + jnp.where(cols == a, full[a * r:(a + 1) * r, :], 0.0)
    return out


def _stack(x, cb):
    return jnp.concatenate([x[:, a * cb:(a + 1) * cb] for a in range(GDN_PACK)], axis=0)


def _lane_halves(a):
    return a[:, :a.shape[1] // 2], a[:, a.shape[1] // 2:]


def _make_packed(dot):
    G = GDN_PACK

    def two_by_two(k, cb):
        return k * G > MXU_TILE and cb * G > MXU_TILE

    def times_bd(x, y, dims):
        k, cb = y.shape[0], y.shape[1] // G
        if two_by_two(k, cb):
            return jnp.concatenate([dot(xh, _bd(yh, cb, G // 2), dims)
                                    for xh, yh in zip(_lane_halves(x), _lane_halves(y))], axis=1)
        return dot(x, _bd(y, cb), dims)

    def tn_diag(x, y):
        k, cb = x.shape[1] // G, y.shape[1] // G
        if two_by_two(k, cb):
            return jnp.concatenate([_diag(dot(xh, yh, TN), k, cb, G // 2)
                                    for xh, yh in zip(_lane_halves(x), _lane_halves(y))], axis=1)
        return _diag(dot(x, y, TN), k, cb)

    @jax.custom_vjp
    def pmm(x, y):
        return times_bd(x, y, NN)

    @jax.custom_vjp
    def pnt(x, y):
        k = x.shape[1] // G
        return _diag(dot(_stack(x, k), _stack(y, k), NT), x.shape[0], y.shape[0])

    @jax.custom_vjp
    def ptn(x, y):
        return tn_diag(x, y)

    def pmm_bwd(res, ct):
        x, y = res
        return times_bd(ct, y, NT), tn_diag(x, ct)

    pmm.defvjp(lambda x, y: (pmm(x, y), (x, y)), pmm_bwd)
    pnt.defvjp(lambda x, y: (pnt(x, y), (x, y)), lambda res, ct: (pmm(ct, res[1]), ptn(ct, res[0])))
    ptn.defvjp(lambda x, y: (ptn(x, y), (x, y)), lambda res, ct: (pnt(res[1], ct), pmm(res[0], ct)))
    return pmm, pnt, ptn


_pmm, _pnt, _ptn = _make_packed(_dlo)


@jax.custom_vjp
def _inv_packed(ms):
    c = ms[0].shape[0]
    ii = lax.broadcasted_iota(jnp.int32, ms[0].shape, 0)
    jj = lax.broadcasted_iota(jnp.int32, ms[0].shape, 1) % c
    ts = [jnp.where(ii == jj, 1.0, 0.0) - m for m in ms]
    ps = [(-m).astype(bf16) for m in ms]
    for _ in range(int(math.log2(c)) - 1):
        ps = [_dlo(p, _bd(p, c), NN).astype(bf16) for p in ps]
        ts = [t + _dlo(t, _bd(p, c), NN) for t, p in zip(ts, ps)]
    return tuple(ts)


def _inv_packed_fwd(ms):
    ts = _inv_packed(ms)
    return ts, ts


def _inv_packed_bwd(ts, cts):
    c = ts[0].shape[0]
    ys = [_diag(_dlo(t, ct, TN), c, c) for t, ct in zip(ts, cts)]
    return (tuple(-_dlo(y, _bd(t.astype(bf16), c), NT) for y, t in zip(ys, ts)),)


_inv_packed.defvjp(_inv_packed_fwd, _inv_packed_bwd)


def _gdn_prep(q2, k2, v4, bcols, gcols, grows):
    c, d = v4.shape[0], GDN_D
    q4 = jnp.concatenate([q2[:, :d], q2[:, :d], q2[:, d:], q2[:, d:]], axis=1)
    k4 = jnp.concatenate([k2[:, :d], k2[:, :d], k2[:, d:], k2[:, d:]], axis=1)
    beta4 = jnp.concatenate([jnp.broadcast_to(b, (c, d)) for b in bcols], axis=1)
    gc4 = jnp.concatenate([jnp.broadcast_to(g, (c, d)) for g in gcols], axis=1)
    low = lax.broadcasted_iota(jnp.int32, (c, 128), 1) < c
    gi = jnp.concatenate([jnp.where(low, gcols[0], gcols[1]), jnp.where(low, gcols[2], gcols[3])], axis=1)
    gj = jnp.concatenate([jnp.where(low, grows[0], grows[1]), jnp.where(low, grows[2], grows[3])], axis=1)
    ii = lax.broadcasted_iota(jnp.int32, gi.shape, 0)
    jj = lax.broadcasted_iota(jnp.int32, gi.shape, 1) % c
    dec = jnp.exp(jnp.where(ii >= jj, gi - gj, NEG))
    rid = lax.broadcasted_iota(jnp.int32, gc4.shape, 0)
    glast = jnp.sum(jnp.where(rid == c - 1, gc4, 0.0), axis=0, keepdims=True)
    eg = jnp.exp(gc4)
    kb = k4 * beta4
    return dict(q=q4, k=k4, kb=kb, vb=v4 * beta4, kbe=kb * eg, qe=q4 * eg, dec=dec, dec_strict=jnp.where(ii > jj, dec, 0.0),
                sdecay=jnp.exp(glast), kd=k4 * jnp.exp(glast - gc4))


@jax.custom_vjp
def _inv_packed_known(ms, ts):
    return ts


_inv_packed_known.defvjp(lambda ms, ts: (ts, ts),
                         lambda ts, cts: (_inv_packed_bwd(ts, cts)[0], tuple(jnp.zeros_like(t) for t in ts)))


def _gdn_groups(groups, known_inverses=None, with_inverses=False):
    c = groups[0][3].shape[0]
    ss = [g[0] for g in groups]
    e = [_gdn_prep(*g[1:]) for g in groups]
    ms = tuple(_pnt(x["kb"], x["k"]) * x["dec_strict"] for x in e)
    ts = _inv_packed(ms) if known_inverses is None else _inv_packed_known(ms, tuple(known_inverses))
    us = [_pmm(t, x["vb"]) for t, x in zip(ts, e)]
    ws = [_pmm(t, x["kbe"]) for t, x in zip(ts, e)]
    attns = [_pnt(x["q"], x["k"]) * x["dec"] for x in e]
    ws_qs = [_pmm(jnp.concatenate([w, x["qe"]], axis=0), s) for w, x, s in zip(ws, e, ss)]
    v_news = [u - y[:c] for u, y in zip(us, ws_qs)]
    os = [y[c:] + _pmm(a, vn) for y, a, vn in zip(ws_qs, attns, v_news)]
    s_news = [s * x["sdecay"] + _ptn(x["kd"], vn) for s, x, vn in zip(ss, e, v_news)]
    if with_inverses:
        return list(zip(os, s_news)), list(ts)
    return list(zip(os, s_news))


def _lane_pick(x, h):
    lane = lax.broadcasted_iota(jnp.int32, x.shape, 1)
    return jnp.sum(jnp.where(lane == h, x, 0.0), axis=1, keepdims=True)


def _cum_log_decay(g):
    c = g.shape[0]
    lower = (lax.broadcasted_iota(jnp.int32, (c, c), 0) >= lax.broadcasted_iota(jnp.int32, (c, c), 1)).astype(f32)
    upper2 = (lax.broadcasted_iota(jnp.int32, (c, 128), 0) <= lax.broadcasted_iota(jnp.int32, (c, 128), 1) % c).astype(f32)
    return _dsel(lower, g, NN, 0), _dsel(g, upper2, TN, 1)


def _group_operands(gi, s_ref, q_ref, k_ref, v_ref, bv, gcv, gct_s):
    heads = [gi * GDN_PACK + u for u in range(GDN_PACK)]
    qk_off = pl.multiple_of(gi * 2 * GDN_D, 2 * GDN_D)
    v_off = pl.multiple_of(gi * GDN_PACK * GDN_D, GDN_PACK * GDN_D)
    return (s_ref[gi], q_ref[:, pl.ds(qk_off, 2 * GDN_D)].astype(f32), k_ref[:, pl.ds(qk_off, 2 * GDN_D)].astype(f32),
            v_ref[:, pl.ds(v_off, GDN_PACK * GDN_D)].astype(f32),
            [_lane_pick(bv, h) for h in heads], [_lane_pick(gcv, h) for h in heads],
            [gct_s[pl.ds(h, 1), :] for h in heads]), heads, qk_off, v_off


def _gdn_fwd(qn, kn, v, beta, g, *, n_real_chunks, name):
    lp = qn.shape[0]
    nchunk = lp // GDN_CHUNK
    C, D = GDN_CHUNK, GDN_D
    NG, SW = GDN_V_HEADS // GDN_PACK, GDN_PACK * GDN_D

    def body(q_ref, k_ref, v_ref, b_ref, g_ref, o_ref, st_ref, inv_ref, s_s, gc_s, gct_s):
        ci = pl.program_id(0)

        @pl.when(ci == 0)
        def _():
            s_s[...] = jnp.zeros_like(s_s)

        @pl.when(ci >= n_real_chunks)
        def _():
            o_ref[...] = jnp.zeros_like(o_ref)
            st_ref[...] = jnp.zeros_like(st_ref)
            inv_ref[...] = jnp.zeros_like(inv_ref)

        @pl.when(ci < n_real_chunks)
        def _():
            gc, gct = _cum_log_decay(g_ref[...])
            gc_s[...] = gc
            gct_s[...] = gct

            def some_groups(it, carry):
                ids = [it * GDN_FWD_INTERLEAVE + u for u in range(GDN_FWD_INTERLEAVE)]
                ops = [_group_operands(gi, s_s, q_ref, k_ref, v_ref, b_ref[...], gc_s[...], gct_s) for gi in ids]
                res, inverses = _gdn_groups([op[0] for op in ops], with_inverses=True)
                for gi, op, (o, s_new), t in zip(ids, ops, res, inverses):
                    st_ref[gi] = op[0][0]
                    inv_ref[gi] = t
                    s_s[gi] = s_new
                    o_ref[:, pl.ds(op[3], SW)] = o
                return carry

            lax.fori_loop(0, NG // GDN_FWD_INTERLEAVE, some_groups, 0)

    return pl.pallas_call(
        body, name=name, grid=(nchunk,),
        in_specs=[pl.BlockSpec((C, GDN_QK_W), lambda c: (c, 0)), pl.BlockSpec((C, GDN_QK_W), lambda c: (c, 0)),
                  pl.BlockSpec((C, GDN_V_W), lambda c: (c, 0)), pl.BlockSpec((C, 128), lambda c: (c, 0)),
                  pl.BlockSpec((C, 128), lambda c: (c, 0))],
        out_specs=[pl.BlockSpec((C, GDN_V_W), lambda c: (c, 0)),
                   pl.BlockSpec((None, NG, D, SW), lambda c: (c, 0, 0, 0)),
                   pl.BlockSpec((None, NG, C, GDN_PACK * C), lambda c: (c, 0, 0, 0))],
        out_shape=[jax.ShapeDtypeStruct((lp, GDN_V_W), f32), jax.ShapeDtypeStruct((nchunk, NG, D, SW), f32),
                   jax.ShapeDtypeStruct((nchunk, NG, C, GDN_PACK * C), f32)],
        scratch_shapes=[pltpu.VMEM((NG, D, SW), f32), pltpu.VMEM((C, 128), f32), pltpu.VMEM((128, 128), f32)],
        compiler_params=pltpu.CompilerParams(dimension_semantics=("arbitrary",)),
    )(qn, kn, v, beta, g)


def _gdn_bwd(qn, kn, v, beta, g, states, inverses, do, *, n_real_chunks, name):
    lp = qn.shape[0]
    nchunk = lp // GDN_CHUNK
    C, D = GDN_CHUNK, GDN_D
    NG, SW = GDN_V_HEADS // GDN_PACK, GDN_PACK * GDN_D
    rev = lambda i: (nchunk - 1 - i, 0)

    def body(q_ref, k_ref, v_ref, b_ref, g_ref, st_ref, inv_ref, do_ref,
             dq_ref, dk_ref, dv_ref, db_ref, dg_ref, ds_s, gc_s, gct_s, dgc_s, dgct_s, dbeta_s):
        step = pl.program_id(0)
        ci = nchunk - 1 - step

        @pl.when(step == 0)
        def _():
            ds_s[...] = jnp.zeros_like(ds_s)

        @pl.when(ci >= n_real_chunks)
        def _():
            for r in (dq_ref, dk_ref, dv_ref, db_ref, dg_ref):
                r[...] = jnp.zeros_like(r)

        @pl.when(ci < n_real_chunks)
        def _():
            gc, gct = _cum_log_decay(g_ref[...])
            gc_s[...] = gc
            gct_s[...] = gct
            dgc_s[...] = jnp.zeros_like(dgc_s)
            dgct_s[...] = jnp.zeros_like(dgct_s)
            dbeta_s[...] = jnp.zeros_like(dbeta_s)

            def some_groups(it, carry):
                ids = [it * GDN_BWD_INTERLEAVE + u for u in range(GDN_BWD_INTERLEAVE)]
                ops = [_group_operands(gi, st_ref, q_ref, k_ref, v_ref, b_ref[...], gc_s[...], gct_s) for gi in ids]
                cts = [(do_ref[:, pl.ds(op[3], SW)], ds_s[gi]) for gi, op in zip(ids, ops)]
                known = [inv_ref[gi] for gi in ids]
                _, vjp_fn = jax.vjp(lambda gs: _gdn_groups(gs, known_inverses=known), [op[0] for op in ops])
                (grads,) = vjp_fn(cts)
                lane = lax.broadcasted_iota(jnp.int32, (C, 128), 1)
                dbeta_acc, dgc_acc = dbeta_s[...], dgc_s[...]
                for gi, (_, heads, qk_off, v_off), (dsp, dq2, dk2, dv4, dbcols, dgcols, dgrows) in zip(ids, ops, grads):
                    ds_s[gi] = dsp
                    dq_ref[:, pl.ds(qk_off, 2 * D)] = dq2
                    dk_ref[:, pl.ds(qk_off, 2 * D)] = dk2
                    dv_ref[:, pl.ds(v_off, SW)] = dv4
                    for h, dbcol, dgcol, dgrow in zip(heads, dbcols, dgcols, dgrows):
                        dbeta_acc = dbeta_acc + jnp.where(lane == h, dbcol, 0.0)
                        dgc_acc = dgc_acc + jnp.where(lane == h, dgcol, 0.0)
                        dgct_s[pl.ds(h, 1), :] = dgrow
                dbeta_s[...] = dbeta_acc
                dgc_s[...] = dgc_acc
                return carry

            lax.fori_loop(0, NG // GDN_BWD_INTERLEAVE, some_groups, 0)
            fold = (lax.broadcasted_iota(jnp.int32, (128, C), 0) % C == lax.broadcasted_iota(jnp.int32, (128, C), 1)).astype(f32)
            eye = (lax.broadcasted_iota(jnp.int32, (128, 128), 0) == lax.broadcasted_iota(jnp.int32, (128, 128), 1)).astype(f32)
            dgc = dgc_s[...] + _dsel(_dsel(dgct_s[...], fold, NN, 1), eye, TN, 1)
            upper = (lax.broadcasted_iota(jnp.int32, (C, C), 0) <= lax.broadcasted_iota(jnp.int32, (C, C), 1)).astype(f32)
            dg_ref[...] = _dsel(upper, dgc, NN, 0)
            db_ref[...] = dbeta_s[...]

    return pl.pallas_call(
        body, name=name, grid=(nchunk,),
        in_specs=[pl.BlockSpec((C, GDN_QK_W), rev), pl.BlockSpec((C, GDN_QK_W), rev), pl.BlockSpec((C, GDN_V_W), rev),
                  pl.BlockSpec((C, 128), rev), pl.BlockSpec((C, 128), rev),
                  pl.BlockSpec((None, NG, D, SW), lambda i: (nchunk - 1 - i, 0, 0, 0)),
                  pl.BlockSpec((None, NG, C, GDN_PACK * C), lambda i: (nchunk - 1 - i, 0, 0, 0)), pl.BlockSpec((C, GDN_V_W), rev)],
        out_specs=[pl.BlockSpec((C, GDN_QK_W), rev), pl.BlockSpec((C, GDN_QK_W), rev), pl.BlockSpec((C, GDN_V_W), rev),
                   pl.BlockSpec((C, 128), rev), pl.BlockSpec((C, 128), rev)],
        out_shape=[jax.ShapeDtypeStruct((lp, GDN_QK_W), f32)] * 2 + [jax.ShapeDtypeStruct((lp, GDN_V_W), f32)]
        + [jax.ShapeDtypeStruct((lp, 128), f32)] * 2,
        scratch_shapes=[pltpu.VMEM((NG, D, SW), f32), pltpu.VMEM((C, 128), f32), pltpu.VMEM((128, 128), f32),
                        pltpu.VMEM((C, 128), f32), pltpu.VMEM((128, 128), f32), pltpu.VMEM((C, 128), f32)],
        compiler_params=pltpu.CompilerParams(dimension_semantics=("arbitrary",)),
    )(qn, kn, v, beta, g, states, inverses, do)


def _attention_bias(tq, tk):
    kk = lax.broadcasted_iota(jnp.int32, (tk, tq), 0)
    qq = lax.broadcasted_iota(jnp.int32, (tk, tq), 1)
    pad = jnp.where(kk < PAD_FRONT, NEG, 0.0)
    diag = [jnp.where(qq >= kk + d * tk, 0.0, NEG) for d in range(tq // tk)]
    return jnp.stack([pad] + diag + [jnp.minimum(pad, diag[0])]).astype(f32)


def _att_tiles(lp):
    tq = _pick(lp, (768, 512, 256))
    return tq, _pick(tq, ATT_KEY_TILES)


def _attention_fwd(q, kvu, kr, proj2, *, name):
    lp = q.shape[0]
    H = MLA_HEADS
    tq, tk = _att_tiles(lp)
    r = tq // tk

    def body(q_ref, kn_ref, kr_ref, v_ref, z_ref, bias_ref, o_ref, lse_ref, og_ref, m_s, l_s, acc_s, sa_s, sb_s):
        qi = pl.program_id(1)
        m_s[...] = jnp.full_like(m_s, NEG)
        l_s[...] = jnp.zeros_like(l_s)
        acc_s[...] = jnp.zeros_like(acc_s)

        def scores(ki):
            k0 = pl.multiple_of(ki * tk, tk)
            k = jnp.concatenate([kn_ref[pl.ds(k0, tk), :], kr_ref[pl.ds(k0, tk), :]], axis=1)
            return lax.dot_general(k, q_ref[...], (NT, ((), ())), preferred_element_type=f32)

        def consume(st, ki, mask):
            k0 = pl.multiple_of(ki * tk, tk)
            if mask is not None:
                st = st + bias_ref[mask]
            m_prev = m_s[...]
            m_new = jnp.maximum(m_prev, jnp.max(st, axis=0, keepdims=True))
            alpha = jnp.exp(m_prev - m_new)
            p = jnp.exp(st - m_new)
            l_s[...] = alpha * l_s[...] + jnp.sum(p, axis=0, keepdims=True)
            acc_s[...] = alpha * acc_s[...] + lax.dot_general(v_ref[pl.ds(k0, tk), :], p.astype(bf16), (TN, ((), ())),
                                                              preferred_element_type=f32)
            m_s[...] = m_new

        n_full = qi * r

        def chain(blocks):
            bufs = (sa_s, sb_s)
            for j, (ki, masked) in enumerate(blocks):
                if j + 1 < len(blocks):
                    bufs[(j + 1) % 2][...] = scores(blocks[j + 1][0])
                consume(bufs[j % 2][...], ki, masked)

        diagonal = [(n_full + d, 1 + d) for d in range(r)]

        @pl.when(qi == 0)
        def _():
            sa_s[...] = scores(0)
            chain([(0, r + 1)] + diagonal[1:])

        @pl.when(qi > 0)
        def _():
            sb_s[...] = scores(0)
            sa_s[...] = scores(1)
            consume(sb_s[...], 0, 0)
            n_pairs = (n_full - 1) // 2

            def two(pi, carry):
                ki = 1 + 2 * pi
                sb_s[...] = scores(ki + 1)
                consume(sa_s[...], ki, None)
                sa_s[...] = scores(ki + 2)
                consume(sb_s[...], ki + 1, None)
                return carry

            lax.fori_loop(0, n_pairs, two, 0)
            nxt = 1 + 2 * n_pairs

            @pl.when(nxt < n_full)
            def _():
                chain([(nxt, None)] + diagonal)

            @pl.when(nxt == n_full)
            def _():
                chain(diagonal)
        o = jnp.transpose(acc_s[...] / l_s[...])
        o_ref[...] = o
        og_ref[...] = (o * _silu(z_ref[...].astype(f32))).astype(og_ref.dtype)
        lse_ref[...] = m_s[...] + jnp.log(l_s[...])

    return pl.pallas_call(
        body, name=name, grid=(H, lp // tq),
        in_specs=[pl.BlockSpec((tq, MLA_QKP), lambda h, qi: (qi, h)),
                  pl.BlockSpec((lp, 128), lambda h, qi: (0, h)),
                  pl.BlockSpec((lp, 128), lambda h, qi: (0, 0)),
                  pl.BlockSpec((lp, 128), lambda h, qi: (0, H + h)),
                  pl.BlockSpec((tq, 128), lambda h, qi: (qi, MLA_Q_RANK // 128 + h)),
                  pl.BlockSpec((r + 2, tk, tq), lambda h, qi: (0, 0, 0))],
        out_specs=[pl.BlockSpec((tq, 128), lambda h, qi: (qi, h)),
                   pl.BlockSpec((None, 1, tq), lambda h, qi: (h, 0, qi)),
                   pl.BlockSpec((tq, 128), lambda h, qi: (qi, h))],
        out_shape=[jax.ShapeDtypeStruct((lp, MLA_V_W), f32), jax.ShapeDtypeStruct((H, 1, lp), f32),
                   jax.ShapeDtypeStruct((lp, MLA_V_W), bf16)],
        scratch_shapes=[pltpu.VMEM((1, tq), f32), pltpu.VMEM((1, tq), f32), pltpu.VMEM((128, tq), f32),
                        pltpu.VMEM((tk, tq), f32), pltpu.VMEM((tk, tq), f32)],
        compiler_params=pltpu.CompilerParams(dimension_semantics=("arbitrary", "arbitrary")),
    )(q, kvu, kr, kvu, proj2, _attention_bias(tq, tk))


def _gate_bwd(o, proj2, dgated, *, name):
    lp = o.shape[0]
    H, w = MLA_HEADS, 2 * MLA_V
    tq = _pick(lp, (768, 512, 256))

    def body(o_ref, z_ref, g_ref, do_ref, dz_ref, dl_ref):
        ov, z, g = o_ref[...], z_ref[...].astype(f32), g_ref[...]
        s = _sigmoid(z)
        do = (g * (z * s)).astype(bf16)
        do_ref[...] = do
        dz_ref[...] = (g * ov * (s * (1.0 + z * (1.0 - s)))).astype(dz_ref.dtype)
        prod = ov * do.astype(f32)
        for u in range(2):
            dl_ref[u] = jnp.sum(jnp.transpose(prod[:, u * MLA_V:(u + 1) * MLA_V]), axis=0, keepdims=True)

    blk = pl.BlockSpec((tq, w), lambda j, qi: (qi, j))
    return pl.pallas_call(
        body, name=name, grid=(H // 2, lp // tq),
        in_specs=[blk, pl.BlockSpec((tq, w), lambda j, qi: (qi, j + MLA_Q_RANK // w)), blk],
        out_specs=[blk, blk, pl.BlockSpec((2, 1, tq), lambda j, qi: (j, 0, qi))],
        out_shape=[jax.ShapeDtypeStruct((lp, MLA_V_W), bf16), jax.ShapeDtypeStruct((lp, MLA_V_W), bf16),
                   jax.ShapeDtypeStruct((H, 1, lp), f32)],
    )(o, proj2, dgated)


def _attention_bwd(q, kvu, kr, lse, delta, do, *, name):
    lp = q.shape[0]
    tq, t = _att_tiles(lp)
    H, nb = MLA_HEADS, lp // t
    r, nq = tq // t, lp // tq

    def body(q_ref, kn_ref, kr_ref, v_ref, lse_ref, dl_ref, do_ref, bias_ref, dq_ref, dkn_ref, dv_ref, dkr_ref, dk_s, dv_s,
             sa_s, da_s, sb_s, db_s):
        ki = pl.program_id(1)
        k = jnp.concatenate([kn_ref[...], kr_ref[...]], axis=1)
        vv = v_ref[...]
        dk_s[...] = jnp.zeros_like(dk_s)
        dv_s[...] = jnp.zeros_like(dv_s)

        def products(qi, s_ref, d_ref):
            q0 = pl.multiple_of(qi * tq, tq)
            s_ref[...] = lax.dot_general(k, q_ref[pl.ds(q0, tq), :], (NT, ((), ())), preferred_element_type=f32)
            d_ref[...] = lax.dot_general(vv, do_ref[pl.ds(q0, tq), :], (NT, ((), ())), preferred_element_type=f32)

        def accumulate(s_ref, d_ref, qi, mask, first):
            q0 = pl.multiple_of(qi * tq, tq)
            qv = q_ref[pl.ds(q0, tq), :]
            dob = do_ref[pl.ds(q0, tq), :]
            st = s_ref[...]
            if mask is not None:
                st = st + bias_ref[mask]
            p = jnp.exp(st - lse_ref[:, pl.ds(q0, tq)])
            dv_s[...] += jnp.dot(p.astype(bf16), dob, preferred_element_type=f32)
            ds = (p * (d_ref[...] - dl_ref[:, pl.ds(q0, tq)])).astype(bf16)
            dk_s[...] += jnp.dot(ds, qv, preferred_element_type=f32)
            dq = lax.dot_general(ds, k, (TN, ((), ())), preferred_element_type=f32)
            if first:
                dq_ref[pl.ds(q0, tq), :] = dq
            else:
                dq_ref[pl.ds(q0, tq), :] += dq

        def sweep(qd, first_mask, other_mask, first):
            last = nq - 1
            products(qd, sa_s, da_s)
            products(jnp.minimum(qd + 1, last), sb_s, db_s)
            accumulate(sa_s, da_s, qd, first_mask, first)
            n = last - qd

            def two(pi, carry):
                i = qd + 1 + 2 * pi
                products(i + 1, sa_s, da_s)
                accumulate(sb_s, db_s, i, other_mask, first)
                products(jnp.minimum(i + 2, last), sb_s, db_s)
                accumulate(sa_s, da_s, i + 1, other_mask, first)
                return carry

            lax.fori_loop(0, n // 2, two, 0)

            @pl.when(n % 2 == 1)
            def _():
                accumulate(sb_s, db_s, last, other_mask, first)

        @pl.when(ki == 0)
        def _():
            sweep(0, r + 1, 0, True)

        @pl.when(ki > 0)
        def _():
            sweep(ki // r, 1 + ki % r, None, False)

        dkn_ref[...] = dk_s[:, :128].astype(dkn_ref.dtype)
        dkr_ref[...] = dk_s[:, 128:]
        dv_ref[...] = dv_s[...].astype(dv_ref.dtype)

    return pl.pallas_call(
        body, name=name, grid=(H, nb),
        in_specs=[pl.BlockSpec((lp, MLA_QKP), lambda h, ki: (0, h)),
                  pl.BlockSpec((t, 128), lambda h, ki: (ki, h)),
                  pl.BlockSpec((t, 128), lambda h, ki: (ki, 0)),
                  pl.BlockSpec((t, 128), lambda h, ki: (ki, H + h)),
                  pl.BlockSpec((None, 1, lp), lambda h, ki: (h, 0, 0)),
                  pl.BlockSpec((None, 1, lp), lambda h, ki: (h, 0, 0)),
                  pl.BlockSpec((lp, 128), lambda h, ki: (0, h)),
                  pl.BlockSpec((r + 2, t, tq), lambda h, ki: (0, 0, 0))],
        out_specs=[pl.BlockSpec((lp, MLA_QKP), lambda h, ki: (0, h)),
                   pl.BlockSpec((t, 128), lambda h, ki: (ki, h)),
                   pl.BlockSpec((t, 128), lambda h, ki: (ki, h)),
                   pl.BlockSpec((t, 128), lambda h, ki: (ki, h))],
        out_shape=[jax.ShapeDtypeStruct((lp, H * MLA_QKP), f32), jax.ShapeDtypeStruct((lp, MLA_V_W), bf16),
                   jax.ShapeDtypeStruct((lp, MLA_V_W), bf16), jax.ShapeDtypeStruct((lp, MLA_V_W), f32)],
        scratch_shapes=[pltpu.VMEM((t, MLA_QKP), f32), pltpu.VMEM((t, 128), f32)] + [pltpu.VMEM((t, tq), f32)] * 4,
        compiler_params=pltpu.CompilerParams(dimension_semantics=("arbitrary", "arbitrary")),
    )(q, kvu, kr, kvu, lse, delta, do, _attention_bias(tq, t))


def _q_proj_fwd(c_q, wq2, cos_q, sin_q, *, name, tm=ROW_TILE):
    lp, rank = c_q.shape
    gw = Q_GROUP * MLA_QKP
    ng = wq2.shape[1] // (2 * gw)

    def body(c_ref, w_ref, cos_ref, sin_ref, o_ref):
        qq = jnp.dot(c_ref[...], w_ref[...], preferred_element_type=f32)
        o_ref[...] = _st_q_rope([qq], [cos_ref[...], sin_ref[...]], [])[0].astype(o_ref.dtype)

    tab = pl.BlockSpec((tm, MLA_QKP), lambda j, i: (i, 0))
    return pl.pallas_call(
        body, name=name, grid=(ng, lp // tm),
        in_specs=[pl.BlockSpec((tm, rank), lambda j, i: (i, 0)), pl.BlockSpec((rank, 2 * gw), lambda j, i: (0, j)), tab, tab],
        out_specs=pl.BlockSpec((tm, gw), lambda j, i: (i, j)),
        out_shape=jax.ShapeDtypeStruct((lp, ng * gw), bf16),
    )(c_q, wq2, cos_q, sin_q)


def _q_proj_bwd(dq, c_q, wq2, cos_q, sin_q, *, name, tm=ROW_TILE):
    lp, rank = c_q.shape
    gw = Q_GROUP * MLA_QKP
    ng = wq2.shape[1] // (2 * gw)

    def body(dq_ref, c_ref, w_ref, cos_ref, sin_ref, dc_ref, dw_ref):
        dqq = _st_q_rope_t([dq_ref[...]], [cos_ref[...], sin_ref[...]], [])[0].astype(bf16)
        dc_ref[...] = lax.dot_general(dqq, w_ref[...], (NT, ((), ())), preferred_element_type=f32)
        dw = lax.dot_general(c_ref[...], dqq, (TN, ((), ())), preferred_element_type=f32)

        @pl.when(pl.program_id(1) == 0)
        def _():
            dw_ref[...] = dw

        @pl.when(pl.program_id(1) > 0)
        def _():
            dw_ref[...] += dw

    tab = pl.BlockSpec((tm, MLA_QKP), lambda j, i: (i, 0))
    return pl.pallas_call(
        body, name=name, grid=(ng, lp // tm),
        in_specs=[pl.BlockSpec((tm, gw), lambda j, i: (i, j)), pl.BlockSpec((tm, rank), lambda j, i: (i, 0)),
                  pl.BlockSpec((rank, 2 * gw), lambda j, i: (0, j)), tab, tab],
        out_specs=[pl.BlockSpec((tm, rank), lambda j, i: (i, j)), pl.BlockSpec((rank, 2 * gw), lambda j, i: (0, j))],
        out_shape=[jax.ShapeDtypeStruct((lp, ng * rank), f32), jax.ShapeDtypeStruct(wq2.shape, f32)],
        compiler_params=pltpu.CompilerParams(dimension_semantics=("arbitrary", "arbitrary")),
    )(dq, c_q, wq2, cos_q, sin_q)


def _exchange_copies(x_ref, o_ref, send_sems, recv_sems, local_sem, gather):
    mx, my, mc = lax.axis_index("x"), lax.axis_index("y"), lax.axis_index("c")
    me = 4 * mx + 2 * my + mc
    own = pltpu.make_async_copy(x_ref if gather else x_ref.at[me], o_ref.at[me], local_sem)
    sends, arrivals = [], []
    for k in range(1, N_DEV):
        px = 1 - mx if k & 4 else mx
        py = 1 - my if k & 2 else my
        pc = 1 - mc if k & 1 else mc
        peer = 4 * px + 2 * py + pc
        sends.append(pltpu.make_async_remote_copy(
            src_ref=x_ref if gather else x_ref.at[peer], dst_ref=o_ref.at[me],
            send_sem=send_sems.at[k - 1], recv_sem=recv_sems.at[k - 1],
            device_id=(px, py, pc), device_id_type=MESH_ID))
        arrivals.append(pltpu.make_async_remote_copy(
            src_ref=o_ref.at[peer], dst_ref=o_ref.at[peer],
            send_sem=send_sems.at[k - 1], recv_sem=recv_sems.at[k - 1],
            device_id=(mx, my, mc), device_id_type=MESH_ID))
    return own, sends, arrivals


def _exchange_start(copies):
    own, sends, _ = copies
    own.start()
    for cp in sends:
        cp.start()


def _exchange_wait(copies):
    own, sends, arrivals = copies
    for cp in arrivals:
        cp.wait_recv()
    for cp in sends:
        cp.wait_send()
    own.wait()


_EXCHANGE_SCRATCH = [pltpu.SemaphoreType.DMA((N_DEV - 1,)), pltpu.SemaphoreType.DMA((N_DEV - 1,)), pltpu.SemaphoreType.DMA]


def _exchange(x, *, gather, name):
    blk = x.shape if gather else x.shape[1:]

    def body(x_ref, o_ref, send_sems, recv_sems, local_sem):
        copies = _exchange_copies(x_ref, o_ref, send_sems, recv_sems, local_sem, gather)
        _exchange_start(copies)
        _exchange_wait(copies)

    return pl.pallas_call(
        body, name=name,
        in_specs=[pl.BlockSpec(memory_space=pltpu.HBM)], out_specs=pl.BlockSpec(memory_space=pltpu.HBM),
        out_shape=jax.ShapeDtypeStruct((N_DEV,) + tuple(blk), x.dtype),
        scratch_shapes=list(_EXCHANGE_SCRATCH),
    )(x)


def _reduce_adamw(parts, w, m, v, *, name):
    r = w.shape[0]
    tr = max(d for d in range(16, 3201, 16) if r % d == 0)

    def body(p_ref, w_ref, m_ref, v_ref, g_ref, d_ref, nm_ref, nv_ref):
        g = p_ref[0].astype(f32)
        for s in range(1, N_DEV):
            g = g + p_ref[s].astype(f32)
        mm = ADAM_B1 * m_ref[...] + (1.0 - ADAM_B1) * g
        vv = ADAM_B2 * v_ref[...] + (1.0 - ADAM_B2) * (g * g)
        m_hat = mm / (1.0 - ADAM_B1 ** ADAM_STEP)
        v_hat = vv / (1.0 - ADAM_B2 ** ADAM_STEP)
        g_ref[...] = g
        d_ref[...] = -ADAM_LR * (m_hat / (jnp.sqrt(v_hat) + ADAM_EPS) + ADAM_WD * w_ref[...])
        nm_ref[...] = mm
        nv_ref[...] = vv

    spec = pl.BlockSpec((tr, 128), lambda i: (i, 0))
    return pl.pallas_call(
        body, name=name, grid=(r // tr,),
        in_specs=[pl.BlockSpec((N_DEV, tr, 128), lambda i: (0, i, 0)), spec, spec, spec],
        out_specs=[spec] * 4, out_shape=[jax.ShapeDtypeStruct((r, 128), f32)] * 4,
    )(parts, w, m, v)


_SHARDED = ("gdn_w_in", "gdn_w_out", "kv_w_down", "kv_w_up", "mla_w_in", "mla_w_q_up", "mla_w_out", "meta_tokens", "gdn_conv_w")
_COL_SHARDED = {"gdn_w_in", "kv_w_up", "mla_w_in", "mla_w_q_up", "meta_tokens", "gdn_conv_w"}
_GATHER_FIRST = ("gdn_w_in",)
_GATHER_F32 = ("meta_tokens", "gdn_conv_w")
_GATHER_REST = ("gdn_w_out", "kv_w_down", "kv_w_up", "mla_w_in", "mla_w_q_up", "mla_w_out")
_SCATTER_EARLY = ("kv_w_down", "kv_w_up", "mla_w_in", "mla_w_q_up", "mla_w_out")
_SCATTER_LATE = ("gdn_w_in", "gdn_conv_w", "gdn_w_out")
_SCATTER_LAST = ("meta_tokens",)
_REPLICATED = ("pre_norm", "post_norm", "gdn_a_log", "gdn_dt_bias", "gdn_out_norm", "kv_norm", "kv_latent_norm",
               "mla_q_latent_norm")


def _rows128(a):
    flat = a.reshape(-1)
    pad = (-flat.shape[0]) % 128
    if pad:
        flat = jnp.pad(flat, (0, pad))
    return flat.reshape(-1, 128)


def _pack(arrs, row_multiple):
    parts = [_rows128(a) for a in arrs]
    buf = jnp.concatenate(parts, axis=0)
    pad = (-buf.shape[0]) % row_multiple
    if pad:
        buf = jnp.pad(buf, ((0, pad), (0, 0)))
    return buf


def _unpack(buf, shapes):
    out, r = [], 0
    for shp in shapes:
        n = math.prod(shp)
        rows = -(-n // 128)
        out.append(buf[r:r + rows].reshape(-1)[:n].reshape(shp))
        r += rows
    return out


def _unshard(g, full_shape, col):
    if col:
        return jnp.transpose(g, (1, 0, 2)).reshape(full_shape)
    return g.reshape(full_shape)


def _to_shards(a, col):
    r, c = a.shape
    if col:
        return jnp.transpose(a.reshape(r, N_DEV, c // N_DEV), (1, 0, 2))
    return a.reshape(N_DEV, r // N_DEV, c)


def _pad_cols(a, width):
    return jnp.pad(a, ((0, 0), (0, width - a.shape[1])))


def _rope_tables(lp):
    inv = ROPE_THETA ** (-jnp.arange(0, MLA_ROPE, 2, dtype=f32) / MLA_ROPE)
    pos = (jnp.arange(lp, dtype=jnp.int32) - PAD_FRONT).astype(f32)
    ang = pos[:, None] * inv[None, :]
    cos, sin = jnp.cos(ang), jnp.sin(ang)
    z = jnp.zeros((lp, 64), f32)
    return jnp.concatenate([cos, cos, z], axis=1), jnp.concatenate([-sin, sin, z], axis=1)


def kernel(x, meta_tokens, pre_norm, post_norm, gdn_w_in, gdn_conv_w, gdn_a_log, gdn_dt_bias, gdn_out_norm, gdn_w_out, kv_norm, kv_w_down, kv_latent_norm, kv_w_up, mla_w_in, mla_q_latent_norm, mla_w_q_up, mla_w_out, loss_target, m_meta_tokens, m_pre_norm, m_post_norm, m_gdn_w_in, m_gdn_conv_w, m_gdn_a_log, m_gdn_dt_bias, m_gdn_out_norm, m_gdn_w_out, m_kv_norm, m_kv_w_down, m_kv_latent_norm, m_kv_w_up, m_mla_w_in, m_mla_q_latent_norm, m_mla_w_q_up, m_mla_w_out, v_meta_tokens, v_pre_norm, v_post_norm, v_gdn_w_in, v_gdn_conv_w, v_gdn_a_log, v_gdn_dt_bias, v_gdn_out_norm, v_gdn_w_out, v_kv_norm, v_kv_w_down, v_kv_latent_norm, v_kv_w_up, v_mla_w_in, v_mla_q_latent_norm, v_mla_w_q_up, v_mla_w_out):
    W = dict(meta_tokens=meta_tokens, pre_norm=pre_norm, post_norm=post_norm, gdn_w_in=gdn_w_in, gdn_conv_w=gdn_conv_w,
             gdn_a_log=gdn_a_log, gdn_dt_bias=gdn_dt_bias, gdn_out_norm=gdn_out_norm, gdn_w_out=gdn_w_out, kv_norm=kv_norm,
             kv_w_down=kv_w_down, kv_latent_norm=kv_latent_norm, kv_w_up=kv_w_up, mla_w_in=mla_w_in,
             mla_q_latent_norm=mla_q_latent_norm, mla_w_q_up=mla_w_q_up, mla_w_out=mla_w_out)
    M = dict(meta_tokens=m_meta_tokens, pre_norm=m_pre_norm, post_norm=m_post_norm, gdn_w_in=m_gdn_w_in, gdn_conv_w=m_gdn_conv_w,
             gdn_a_log=m_gdn_a_log, gdn_dt_bias=m_gdn_dt_bias, gdn_out_norm=m_gdn_out_norm, gdn_w_out=m_gdn_w_out, kv_norm=m_kv_norm,
             kv_w_down=m_kv_w_down, kv_latent_norm=m_kv_latent_norm, kv_w_up=m_kv_w_up, mla_w_in=m_mla_w_in,
             mla_q_latent_norm=m_mla_q_latent_norm, mla_w_q_up=m_mla_w_q_up, mla_w_out=m_mla_w_out)
    V = dict(meta_tokens=v_meta_tokens, pre_norm=v_pre_norm, post_norm=v_post_norm, gdn_w_in=v_gdn_w_in, gdn_conv_w=v_gdn_conv_w,
             gdn_a_log=v_gdn_a_log, gdn_dt_bias=v_gdn_dt_bias, gdn_out_norm=v_gdn_out_norm, gdn_w_out=v_gdn_w_out, kv_norm=v_kv_norm,
             kv_w_down=v_kv_w_down, kv_latent_norm=v_kv_latent_norm, kv_w_up=v_kv_w_up, mla_w_in=v_mla_w_in,
             mla_q_latent_norm=v_mla_q_latent_norm, mla_w_q_up=v_mla_w_q_up, mla_w_out=v_mla_w_out)
    order = list(W)

    n_tok = x.shape[1]
    assert n_tok % GDN_CHUNK == 0
    n_real = ROW0 + n_tok
    lp = -(-n_real // ROW_TILE) * ROW_TILE
    n_real_chunks = n_real // GDN_CHUNK

    shard2d = {n: W[n].reshape(W[n].shape[-2:]) for n in _SHARDED}
    full_shape = {n: ((s.shape[0], s.shape[1] * N_DEV) if n in _COL_SHARDED else (s.shape[0] * N_DEV, s.shape[1]))
                  for n, s in shard2d.items()}
    full = {}

    def unpack_gathered(names, buf):
        r = 0
        for n in names:
            shp = shard2d[n].shape
            rows = math.prod(shp) // 128
            blocks = buf[:, r:r + rows].reshape((N_DEV,) + shp)
            full[n] = _unshard(blocks, full_shape[n], n in _COL_SHARDED)
            r += rows

    unpack_gathered(_GATHER_FIRST, _exchange(_pack([shard2d[n].astype(bf16) for n in _GATHER_FIRST], 16), gather=True,
                                             name="gather_w_in"))
    unpack_gathered(_GATHER_F32, _exchange(_pack([shard2d[n] for n in _GATHER_F32], 8), gather=True, name="gather_meta_conv"))
    rest_shards = _pack([shard2d[n].astype(bf16) for n in _GATHER_REST], 16)

    h0 = jnp.concatenate([jnp.zeros((PAD_FRONT, D_MODEL), f32), full["meta_tokens"], x[0],
                          jnp.zeros((lp - n_real, D_MODEL), f32)], axis=0)
    tgt = jnp.concatenate([jnp.zeros((ROW0, D_MODEL), f32), loss_target[0], jnp.zeros((lp - n_real, D_MODEL), f32)], axis=0)
    cos_k, sin_k = _rope_tables(lp)
    one = jnp.ones((lp, 128), f32)
    cos_q = jnp.concatenate([one, cos_k], axis=1)
    sin_q = jnp.concatenate([jnp.zeros((lp, 128), f32), sin_k], axis=1)

    w_in = full["gdn_w_in"]
    s1 = GDN_CONV_W + GDN_V_W
    w_in_p = jnp.concatenate([w_in[:, :s1], _pad_cols(w_in[:, s1:s1 + 16], 128), _pad_cols(w_in[:, s1 + 16:], 128)], axis=1)
    pre0, pre1 = pre_norm[0:1], pre_norm[1:2]
    post0, post1 = post_norm[0:1], post_norm[1:2]
    (hn0,) = _rowwise(_st_prenorm, [(h0, None, 0)], [], [pre0], [(D_MODEL, bf16, None)], name="f_prenorm0")
    proj, g_rest = _matmul(hn0, w_in_p, out_dtype=bf16, name="f_gdn_in", side=(rest_shards, True))
    unpack_gathered(_GATHER_REST, g_rest)

    wd = full["kv_w_down"]
    zc = jnp.zeros((D_MODEL, 64), bf16)
    wd2 = jnp.concatenate([wd, zc, jnp.zeros((D_MODEL, 128), bf16), wd[:, 160:192], wd[:, 128:160], zc], axis=1)
    wup_p = jnp.transpose(full["kv_w_up"].reshape(MLA_KV_RANK, MLA_HEADS, 2, 128), (0, 2, 1, 3)).reshape(MLA_KV_RANK, 2 * MLA_V_W)
    wq = full["mla_w_q_up"].reshape(MLA_Q_RANK, MLA_HEADS, MLA_QK)
    zq64 = jnp.zeros((MLA_Q_RANK, MLA_HEADS, 64), bf16)
    wq_plain = jnp.concatenate([wq, zq64], axis=2).reshape(MLA_Q_RANK, MLA_HEADS * MLA_QKP)
    wq_swap = jnp.concatenate([jnp.zeros((MLA_Q_RANK, MLA_HEADS, 128), bf16), wq[:, :, 160:192], wq[:, :, 128:160], zq64],
                              axis=2).reshape(MLA_Q_RANK, MLA_HEADS * MLA_QKP)
    q_half = Q_GROUP * MLA_QKP
    wq2 = jnp.concatenate([wq_plain[:, :q_half], wq_swap[:, :q_half], wq_plain[:, q_half:], wq_swap[:, q_half:]], axis=1)
    w_mla_in, w_gdn_out, w_mla_out = full["mla_w_in"], full["gdn_w_out"], full["mla_w_out"]
    conv_w = full["gdn_conv_w"]
    alog_p, dtb_p = _pad_cols(gdn_a_log, 128), _pad_cols(gdn_dt_bias, 128)
    kvn, kvln = kv_norm.reshape(1, -1), kv_latent_norm.reshape(1, -1)

    conv = _conv_fwd(proj, conv_w, col_blocks=GDN_CONV_W // CONV_BC, name="f_conv")
    (qn,) = _rowwise(_st_gdn_q, [(conv, GDN_QK_W, 0)], [], [], [(GDN_QK_W, bf16, GDN_QK_W)], heads=GDN_QK_HEADS, name="f_gdn_q")
    (kn,) = _rowwise(_st_gdn_k, [(conv, GDN_QK_W, 1)], [], [], [(GDN_QK_W, bf16, GDN_QK_W)], heads=GDN_QK_HEADS, name="f_gdn_k")
    (vv,) = _rowwise(_st_gdn_v, [(conv, GDN_V_W, 1)], [], [], [(GDN_V_W, bf16, GDN_V_W)], name="f_gdn_v")
    gate_rows = [(proj, 128, s1 // 128), (proj, 128, s1 // 128 + 1)]
    beta, gdec = _rowwise(_st_gdn_gate, gate_rows, [], [alog_p, dtb_p], [(128, f32, None)] * 2, name="f_gdn_gate")
    o_gdn, states, inverses = _gdn_fwd(qn, kn, vv, beta, gdec, n_real_chunks=n_real_chunks, name="f_gdn")
    out_rows = [(o_gdn, GDN_V_W, 0), (proj, GDN_V_W, GDN_CONV_W // GDN_V_W)]
    (og,) = _rowwise(_st_gdn_out, out_rows, [], [gdn_out_norm], [(GDN_V_W, bf16, GDN_V_W)], heads=GDN_V_HEADS, name="f_gdn_out")
    y0 = _matmul(og, w_gdn_out, name="f_gdn_wout")
    mid_rows = [(h0, None, 0), (y0, None, 0)]
    h1, hn1, hkv = _rowwise(_st_mid, mid_rows, [], [post0, pre1, kvn],
                            [(D_MODEL, f32, None), (D_MODEL, bf16, None), (D_MODEL, bf16, None)], name="f_mid")
    ckr = _matmul(hkv, wd2, name="f_kv_down")
    proj2 = _matmul(hn1, w_mla_in, out_dtype=bf16, name="f_mla_in")
    lat_rows = [(ckr, None, 0), (proj2, MLA_Q_RANK, 0)]
    lat_nd = [(cos_k, None, 0), (sin_k, None, 0)]
    c_kv, k_rope, c_q = _rowwise(_st_latent, lat_rows, lat_nd, [kvln, mla_q_latent_norm],
                                 [(128, bf16, None), (128, bf16, None), (MLA_Q_RANK, bf16, None)], name="f_latent")
    kvu = _matmul(c_kv, wup_p, out_dtype=bf16, name="f_kv_up")
    q_att = _q_proj_fwd(c_q, wq2, cos_q, sin_q, name="f_q_proj")
    o_att, lse, og2 = _attention_fwd(q_att, kvu, k_rope, proj2, name="f_attention")
    y1 = _matmul(og2, w_mla_out, name="f_mla_wout")
    st_loss = _make_st_loss(n_tok)
    loss_rows_in = [(h1, None, 0), (y1, None, 0)]
    (loss_rows,) = _rowwise(st_loss, loss_rows_in, [(tgt, None, 0)], [post1], [(1, f32, None)], name="f_loss")
    loss_here = jnp.sum(loss_rows)

    ones_ct = jnp.ones((lp, 1), f32)
    (dh1_a, dy1), (dpost1,) = _rowwise_vjp(st_loss, loss_rows_in, [(tgt, None, 0)], [post1], [(ones_ct, None, 0)], name="b_loss")
    dog2 = _matmul(dy1, w_mla_out, tb=True, name="b_mla_wout_x")
    dw_mla_out = _matmul(og2, dy1, ta=True, name="b_mla_wout_w")
    do_att, dz2, delta = _gate_bwd(o_att, proj2, dog2, name="b_mla_gate")
    dq_att, dkn, dvv, dkr_h = _attention_bwd(q_att, kvu, k_rope, lse, delta, do_att, name="b_attention")
    dc_q, dwq2 = _q_proj_bwd(dq_att, c_q, wq2, cos_q, sin_q, name="b_q_proj")
    dkvu = jnp.concatenate([dkn, dvv], axis=1)
    dc_kv = _matmul(dkvu, wup_p, tb=True, name="b_kv_up_x")
    dwup_p = _matmul(c_kv, dkvu, ta=True, name="b_kv_up_w")

    def lat_ct(cv):
        dkr = cv[1][:, 0:128]
        for h in range(1, MLA_HEADS):
            dkr = dkr + cv[1][:, h * 128:(h + 1) * 128]
        return [cv[0], dkr, cv[2][:, :MLA_Q_RANK] + cv[2][:, MLA_Q_RANK:]]

    (dckr, dcq_pre), (dkvln, dqln) = _rowwise_vjp(
        _st_latent, lat_rows, lat_nd, [kvln, mla_q_latent_norm],
        [(dc_kv, None, 0), (dkr_h, None, 0), (dc_q, None, 0)], ct_pre=lat_ct, name="b_latent", grad_dtypes=[bf16, bf16])
    dproj2 = jnp.concatenate([dcq_pre, dz2], axis=1)
    dhn1 = _matmul(dproj2, w_mla_in, tb=True, name="b_mla_in_x")
    dw_mla_in = _matmul(hn1, dproj2, ta=True, name="b_mla_in_w")
    dhkv = _matmul(dckr, wd2, tb=True, name="b_kv_down_x")
    dwd2 = _matmul(hkv, dckr, ta=True, name="b_kv_down_w")
    (dh0_a, dy0), (dpost0, dpre1, dkvn) = _rowwise_vjp(
        _st_mid, mid_rows, [], [post0, pre1, kvn], [(dh1_a, None, 0), (dhn1, None, 0), (dhkv, None, 0)], name="b_mid")
    dog = _matmul(dy0, w_gdn_out, tb=True, name="b_gdn_wout_x")
    dw_gdn_out = _matmul(og, dy0, ta=True, name="b_gdn_wout_w")
    (do_gdn, dz), (dout_norm,) = _rowwise_vjp(_st_gdn_out, out_rows, [], [gdn_out_norm], [(dog, GDN_V_W, 0)], heads=GDN_V_HEADS,
                                              name="b_gdn_out", grad_dtypes=[f32, bf16])
    dq_g, dk_g, dv_g, dbeta, dgdec = _gdn_bwd(qn, kn, vv, beta, gdec, states, inverses, do_gdn, n_real_chunks=n_real_chunks,
                                              name="b_gdn")
    (db_col, da_col), (dalog_p, ddtb_p) = _rowwise_vjp(
        _st_gdn_gate, gate_rows, [], [alog_p, dtb_p], [(dbeta, None, 0), (dgdec, None, 0)], name="b_gdn_gate",
        grad_dtypes=[bf16, bf16])
    (dconv_q,), _ = _rowwise_vjp(_st_gdn_q, [(conv, GDN_QK_W, 0)], [], [], [(dq_g, GDN_QK_W, 0)], heads=GDN_QK_HEADS, name="b_gdn_q")
    (dconv_k,), _ = _rowwise_vjp(_st_gdn_k, [(conv, GDN_QK_W, 1)], [], [], [(dk_g, GDN_QK_W, 0)], heads=GDN_QK_HEADS, name="b_gdn_k")
    (dconv_v,), _ = _rowwise_vjp(_st_gdn_v, [(conv, GDN_V_W, 1)], [], [], [(dv_g, GDN_V_W, 0)], name="b_gdn_v")
    nq_b = GDN_QK_W // CONV_BC
    dpre_q, dcw_q = _conv_bwd(dconv_q, proj, conv_w, x_off=0, w_off=0, name="b_conv_q")
    dpre_k, dcw_k = _conv_bwd(dconv_k, proj, conv_w, x_off=nq_b, w_off=nq_b, name="b_conv_k")
    dpre_v, dcw_v = _conv_bwd(dconv_v, proj, conv_w, x_off=2 * nq_b, w_off=2 * nq_b, name="b_conv_v")
    dproj = jnp.concatenate([dpre_q, dpre_k, dpre_v, dz, db_col, da_col], axis=1)
    G = {}
    G["kv_w_down"] = jnp.concatenate([dwd2[:, :128], dwd2[:, 128:160] + dwd2[:, 416:448], dwd2[:, 160:192] + dwd2[:, 384:416]], axis=1)
    G["kv_w_up"] = jnp.transpose(dwup_p.reshape(MLA_KV_RANK, 2, MLA_HEADS, 128), (0, 2, 1, 3)).reshape(MLA_KV_RANK, 2 * MLA_V_W)
    G["mla_w_in"] = dw_mla_in
    dq4 = dwq2.reshape(MLA_Q_RANK, 2, 2, Q_GROUP, MLA_QKP)
    dqp = dq4[:, :, 0].reshape(MLA_Q_RANK, MLA_HEADS, MLA_QKP)
    dqs = dq4[:, :, 1].reshape(MLA_Q_RANK, MLA_HEADS, MLA_QKP)
    G["mla_w_q_up"] = jnp.concatenate([dqp[:, :, :128], dqp[:, :, 128:160] + dqs[:, :, 160:192],
                                       dqp[:, :, 160:192] + dqs[:, :, 128:160]], axis=2).reshape(MLA_Q_RANK, MLA_HEADS * MLA_QK)
    G["mla_w_out"] = dw_mla_out
    G["gdn_w_out"] = dw_gdn_out

    def shards_to_send(names):
        return jnp.concatenate([_to_shards(G[n], n in _COL_SHARDED).reshape(N_DEV, -1, 128).astype(bf16) for n in names], axis=1)

    dw_in_p, parts_early = _matmul(hn0, dproj, ta=True, name="b_gdn_in_w", side=(shards_to_send(_SCATTER_EARLY), False))
    G["gdn_w_in"] = jnp.concatenate([dw_in_p[:, :s1 + 16], dw_in_p[:, s1 + 128:s1 + 144]], axis=1)
    G["gdn_conv_w"] = jnp.concatenate([dcw_q, dcw_k, dcw_v], axis=1)
    dhn0, parts_late = _matmul(dproj, w_in_p, tb=True, name="b_gdn_in_x", side=(shards_to_send(_SCATTER_LATE), False))
    (dh0,), (dpre0,) = _rowwise_vjp(_st_prenorm, [(h0, None, 0)], [], [pre0], [(dhn0, None, 0)], extra=[dh0_a], name="b_prenorm0")

    grad_x = dh0[ROW0:n_real][None]
    G["meta_tokens"] = dh0[PAD_FRONT:ROW0]
    G["pre_norm"] = jnp.concatenate([dpre0, dpre1], axis=0)
    G["post_norm"] = jnp.concatenate([dpost0, dpost1], axis=0)
    G["gdn_a_log"] = dalog_p[:, :GDN_V_HEADS]
    G["gdn_dt_bias"] = ddtb_p[:, :GDN_V_HEADS]
    G["gdn_out_norm"] = dout_norm
    G["kv_norm"] = dkvn.reshape(-1)
    G["kv_latent_norm"] = dkvln.reshape(-1)
    G["mla_q_latent_norm"] = dqln

    parts_last = _exchange(shards_to_send(_SCATTER_LAST), gather=False, name="scatter_meta_grads")
    G["loss"] = loss_here.reshape(1, 1)
    for d in (W, M, V):
        d["loss"] = jnp.zeros((1, 1), f32)
    replicated = _REPLICATED + ("loss",)
    parts_r = _exchange(_pack([G[n] for n in replicated], 8), gather=True, name="gather_small_grads")
    outs = {}
    for names, parts, tag in ((_SCATTER_EARLY, parts_early, "early"), (_SCATTER_LATE, parts_late, "late"),
                              (_SCATTER_LAST, parts_last, "last"), (replicated, parts_r, "replicated")):
        w_p, m_p, v_p = (_pack([d[n] for n in names], 8) for d in (W, M, V))
        res = _reduce_adamw(parts, w_p, m_p, v_p, name="adamw_" + tag)
        for kind, buf in zip(("grad", "delta", "new_m", "new_v"), res):
            for n, a in zip(names, _unpack(buf, [W[n].shape for n in names])):
                outs[kind, n] = a
    loss = outs["grad", "loss"].reshape(())
    return (loss, grad_x, *[outs[k, n] for k in ("grad", "delta", "new_m", "new_v") for n in order])
```

```python
import functools
import math

import jax
import jax.numpy as jnp
from jax import lax
from jax.experimental import pallas as pl
from jax.experimental.pallas import tpu as pltpu

f32, bf16 = jnp.float32, jnp.bfloat16
MESH_ID = pl.DeviceIdType.MESH

N_DEV = 8
D_MODEL = 1024
N_META = 16
NORM_EPS = 1e-6
PAD_FRONT = 48
ROW0 = PAD_FRONT + N_META
GDN_QK_HEADS, GDN_V_HEADS, GDN_D = 8, 16, 128
GDN_CHUNK = 64
GDN_QK_W, GDN_V_W = GDN_QK_HEADS * GDN_D, GDN_V_HEADS * GDN_D
GDN_CONV_W = 2 * GDN_QK_W + GDN_V_W
GDN_IN_W = GDN_CONV_W + GDN_V_W + 2 * GDN_V_HEADS
GDN_IN_WP = GDN_CONV_W + GDN_V_W + 2 * 128
MLA_HEADS, MLA_NOPE, MLA_ROPE, MLA_V = 16, 128, 64, 128
MLA_Q_RANK, MLA_KV_RANK = 256, 128
MLA_QK = MLA_NOPE + MLA_ROPE
MLA_QKP = 256
MLA_V_W = MLA_HEADS * MLA_V
ROPE_THETA = 10000.0
NEG = -1e30
ROW_TILE = 256
ATT_KEY_TILES = (384, 256)

ADAM_LR, ADAM_B1, ADAM_B2, ADAM_EPS, ADAM_WD, ADAM_STEP = 0.001, 0.9, 0.999, 1e-08, 0.01, 10

NN = ((1,), (0,))
NT = ((1,), (1,))
TN = ((0,), (0,))


def _pick(dim, prefs):
    for p in prefs:
        if dim % p == 0:
            return p
    return dim


def _dlo(a, b, dims):
    return lax.dot_general(a.astype(bf16), b.astype(bf16), (dims, ((), ())), preferred_element_type=f32)


def _dsel(a, b, dims, selector):
    x = b if selector == 0 else a
    hi = x.astype(bf16)
    rest = x - hi.astype(f32)
    mid = rest.astype(bf16)
    low = (rest - mid.astype(f32)).astype(bf16)
    sel = (a if selector == 0 else b).astype(bf16)
    d = (lambda p: lax.dot_general(sel, p, (dims, ((), ())), preferred_element_type=f32)) if selector == 0 else \
        (lambda p: lax.dot_general(p, sel, (dims, ((), ())), preferred_element_type=f32))
    return d(hi) + (d(mid) + d(low))


def _matmul(a, b, *, ta=False, tb=False, out_dtype=f32, name, side=None):
    assert not (ta and tb)
    if ta:
        kdim, m = a.shape
    else:
        m, kdim = a.shape
    n = b.shape[0] if tb else b.shape[1]
    assert (b.shape[1] if tb else b.shape[0]) == kdim
    tm = _pick(m, (1024, 768, 512, 384, 256, 128))
    tn = _pick(n, (1024, 768, 640, 512, 256, 128))
    tk = _pick(kdim, (1024, 768, 640, 512, 256, 128))
    nk = kdim // tk
    dims = TN if ta else (NT if tb else NN)

    grid = (m // tm, n // tn, nk)

    def product(a_ref, b_ref, o_ref, acc_ref):
        k = pl.program_id(2)

        @pl.when(k == 0)
        def _():
            acc_ref[...] = jnp.zeros_like(acc_ref)

        acc_ref[...] += lax.dot_general(a_ref[...].astype(bf16), b_ref[...].astype(bf16), (dims, ((), ())),
                                        preferred_element_type=f32)

        @pl.when(k == nk - 1)
        def _():
            o_ref[...] = acc_ref[...].astype(o_ref.dtype)

    a_spec = pl.BlockSpec((tk, tm), lambda i, j, k: (k, i)) if ta else pl.BlockSpec((tm, tk), lambda i, j, k: (i, k))
    b_spec = pl.BlockSpec((tn, tk), lambda i, j, k: (j, k)) if tb else pl.BlockSpec((tk, tn), lambda i, j, k: (k, j))
    o_spec = pl.BlockSpec((tm, tn), lambda i, j, k: (i, j))
    o_shape = jax.ShapeDtypeStruct((m, n), out_dtype)
    if side is None:
        def body(a_ref, b_ref, o_ref, acc_ref):
            product(a_ref, b_ref, o_ref, acc_ref)

        return pl.pallas_call(
            body, name=name, grid=grid, in_specs=[a_spec, b_spec], out_specs=o_spec, out_shape=o_shape,
            scratch_shapes=[pltpu.VMEM((tm, tn), f32)],
            compiler_params=pltpu.CompilerParams(dimension_semantics=("parallel", "parallel", "arbitrary")),
        )(a, b)

    x, gather = side
    blk = x.shape if gather else x.shape[1:]

    def body_with_exchange(a_ref, b_ref, x_ref, o_ref, xo_ref, acc_ref, send_sems, recv_sems, local_sem):
        step = (pl.program_id(0) * grid[1] + pl.program_id(1)) * grid[2] + pl.program_id(2)
        copies = _exchange_copies(x_ref, xo_ref, send_sems, recv_sems, local_sem, gather)

        @pl.when(step == 0)
        def _():
            _exchange_start(copies)

        product(a_ref, b_ref, o_ref, acc_ref)

        @pl.when(step == grid[0] * grid[1] * grid[2] - 1)
        def _():
            _exchange_wait(copies)

    hbm = pl.BlockSpec(memory_space=pltpu.HBM)
    return pl.pallas_call(
        body_with_exchange, name=name, grid=grid, in_specs=[a_spec, b_spec, hbm], out_specs=[o_spec, hbm],
        out_shape=[o_shape, jax.ShapeDtypeStruct((N_DEV,) + tuple(blk), x.dtype)],
        scratch_shapes=[pltpu.VMEM((tm, tn), f32)] + list(_EXCHANGE_SCRATCH),
        compiler_params=pltpu.CompilerParams(dimension_semantics=("arbitrary", "arbitrary", "arbitrary")),
    )(a, b, x)


def _row_spec(item, tr):
    a, bc, off = item
    if bc is None:
        return pl.BlockSpec((tr, a.shape[1]), lambda i, j: (i, 0))
    return pl.BlockSpec((tr, bc), lambda i, j, off=off: (i, j + off))


def _param_spec(p):
    return pl.BlockSpec(p.shape, lambda i, j: (0, 0))


def _row_tile(lp, items):
    widest = max(a.shape[1] if bc is None else bc for (a, bc, _) in items)
    return _pick(lp, (384, ROW_TILE)) if widest >= 1024 else _pick(lp, (768, 512, 256))


def _head_cols(tiles, h, heads):
    return [x[:, h * (x.shape[1] // heads):(h + 1) * (x.shape[1] // heads)] for x in tiles]


def _rowwise(fn, rows, nodiff, params, outs, *, ncol=1, heads=1, name):
    lp = rows[0][0].shape[0]
    tr = _row_tile(lp, rows)
    nr, nd = len(rows), len(nodiff)

    def body(*refs):
        rv = [r[...].astype(f32) for r in refs[:nr]]
        nv = [r[...] for r in refs[nr:nr + nd]]
        pv = [r[...] for r in refs[nr + nd:nr + nd + len(params)]]
        per_head = [fn(_head_cols(rv, h, heads), nv, pv) for h in range(heads)]
        res = [jnp.concatenate(list(vals), axis=1) if heads > 1 else vals[0] for vals in zip(*per_head)]
        for ref, val in zip(refs[nr + nd + len(params):], res):
            ref[...] = val.astype(ref.dtype)

    out_specs = [pl.BlockSpec((tr, c if bc is None else bc), (lambda i, j: (i, 0)) if bc is None else (lambda i, j: (i, j)))
                 for (c, _, bc) in outs]
    return pl.pallas_call(
        body, name=name, grid=(lp // tr, ncol),
        in_specs=[_row_spec(it, tr) for it in rows + nodiff] + [_param_spec(p) for p in params],
        out_specs=out_specs,
        out_shape=[jax.ShapeDtypeStruct((lp, c), dt) for (c, dt, _) in outs],
    )(*[it[0] for it in rows + nodiff], *params)


def _rowwise_vjp(fn, rows, nodiff, params, cts, *, ncol=1, heads=1, name, ct_pre=None, extra=None, grad_dtypes=None):
    lp = rows[0][0].shape[0]
    tr = _row_tile(lp, rows)
    nr, nd, npar, nct = len(rows), len(nodiff), len(params), len(cts)
    extra = extra or [None] * nr
    grad_dtypes = grad_dtypes or [f32] * nr
    ex_items = [(e, rows[k][1], 0) for k, e in enumerate(extra) if e is not None]
    ex_pos = [k for k, e in enumerate(extra) if e is not None]
    for (a, bc, _) in rows:
        assert bc is not None or ncol == 1

    def body(*refs):
        pos = 0
        rv = [r[...].astype(f32) for r in refs[pos:pos + nr]]; pos += nr
        nv = [r[...] for r in refs[pos:pos + nd]]; pos += nd
        pv = [r[...] for r in refs[pos:pos + npar]]; pos += npar
        cv = [r[...].astype(f32) for r in refs[pos:pos + nct]]; pos += nct
        ev = [r[...].astype(f32) for r in refs[pos:pos + len(ex_items)]]; pos += len(ex_items)
        drow_refs = refs[pos:pos + nr]; pos += nr
        dpar_refs = refs[pos:pos + npar]
        ctv = ct_pre(cv) if ct_pre is not None else cv
        drow_h, dpar = [], None
        for h in range(heads):
            outs, vjp_fn = jax.vjp(lambda rr, pp: fn(rr, nv, pp), _head_cols(rv, h, heads), pv)
            dr, dp = vjp_fn([c.astype(o.dtype) for c, o in zip(_head_cols(ctv, h, heads), outs)])
            drow_h.append(dr)
            dpar = dp if dpar is None else [a + b for a, b in zip(dpar, dp)]
        drow = [jnp.concatenate(list(vals), axis=1) if heads > 1 else vals[0] for vals in zip(*drow_h)]
        for k, e in zip(ex_pos, ev):
            drow[k] = drow[k] + e
        for ref, val in zip(drow_refs, drow):
            ref[...] = val.astype(ref.dtype)
        first = jnp.logical_and(pl.program_id(0) == 0, pl.program_id(1) == 0)

        @pl.when(first)
        def _():
            for ref, val in zip(dpar_refs, dpar):
                ref[...] = val

        @pl.when(jnp.logical_not(first))
        def _():
            for ref, val in zip(dpar_refs, dpar):
                ref[...] += val

    drow_shapes, drow_specs = [], []
    for (a, bc, _), dt in zip(rows, grad_dtypes):
        if bc is None:
            drow_shapes.append(jax.ShapeDtypeStruct((lp, a.shape[1]), dt))
            drow_specs.append(pl.BlockSpec((tr, a.shape[1]), lambda i, j: (i, 0)))
        else:
            drow_shapes.append(jax.ShapeDtypeStruct((lp, ncol * bc), dt))
            drow_specs.append(pl.BlockSpec((tr, bc), lambda i, j: (i, j)))
    res = pl.pallas_call(
        body, name=name, grid=(lp // tr, ncol),
        in_specs=[_row_spec(it, tr) for it in rows + nodiff] + [_param_spec(p) for p in params]
        + [_row_spec(it, tr) for it in cts + ex_items],
        out_specs=drow_specs + [_param_spec(p) for p in params],
        out_shape=drow_shapes + [jax.ShapeDtypeStruct(p.shape, f32) for p in params],
        compiler_params=pltpu.CompilerParams(dimension_semantics=("arbitrary", "arbitrary")),
    )(*[it[0] for it in rows + nodiff], *params, *[it[0] for it in cts + ex_items])
    return res[:nr], res[nr:]


def _rms(x, g):
    return x * lax.rsqrt(jnp.mean(x * x, axis=-1, keepdims=True) + NORM_EPS) * g


def _l2n(x):
    return x * lax.rsqrt(jnp.sum(x * x, axis=-1, keepdims=True) + NORM_EPS)


def _sigmoid(x):
    return 1.0 / (1.0 + jnp.exp(-x))


def _silu(x):
    return x * _sigmoid(x)


def _softplus(x):
    return jnp.maximum(x, 0.0) + jnp.log(1.0 + jnp.exp(-jnp.abs(x)))


def _row_ids(shape):
    return pl.program_id(0) * shape[0] + lax.broadcasted_iota(jnp.int32, shape, 0)


def _st_prenorm(r, n, p):
    return [_rms(r[0], p[0])]


def _st_gdn_q(r, n, p):
    return [_l2n(_silu(r[0])) * (GDN_D ** -0.5)]


def _st_gdn_k(r, n, p):
    return [_l2n(_silu(r[0]))]


def _st_gdn_v(r, n, p):
    return [_silu(r[0])]


def _st_gdn_gate(r, n, p):
    real = _row_ids(r[0].shape) >= PAD_FRONT
    beta = jnp.where(real, _sigmoid(r[0]), 0.0)
    g = jnp.where(real, -jnp.exp(p[0]) * _softplus(r[1] + p[1]), 0.0)
    return [beta, g]


def _st_gdn_out(r, n, p):
    return [_rms(r[0], p[0]) * _silu(r[1])]


def _st_mid(r, n, p):
    h1 = r[0] + _rms(r[1], p[0])
    return [h1, _rms(h1, p[1]), _rms(h1, p[2])]


def _st_latent(r, n, p):
    ckr, cq = r
    c_kv = _rms(ckr[:, :MLA_KV_RANK], p[0])
    k_rope = ckr[:, 128:256] * n[0] + ckr[:, 384:512] * n[1]
    return [c_kv, k_rope, _rms(cq, p[1])]


Q_GROUP = 8


def _st_q_rope(r, n, p):
    half = Q_GROUP * MLA_QKP
    out = []
    for h in range(Q_GROUP):
        cols = slice(h * MLA_QKP, (h + 1) * MLA_QKP)
        out.append((r[0][:, :half][:, cols] * n[0] + r[0][:, half:][:, cols] * n[1]) * (MLA_QK ** -0.5))
    return [jnp.concatenate(out, axis=1)]


def _st_q_rope_t(r, n, p):
    plain, swapped = [], []
    for h in range(Q_GROUP):
        ct = r[0][:, h * MLA_QKP:(h + 1) * MLA_QKP] * (MLA_QK ** -0.5)
        plain.append(ct * n[0])
        swapped.append(ct * n[1])
    return [jnp.concatenate(plain + swapped, axis=1)]


def _make_st_loss(n_tokens):
    def st(r, n, p):
        h2 = r[0] + _rms(r[1], p[0])
        rows = _row_ids((r[0].shape[0], 1))
        real = jnp.logical_and(rows >= ROW0, rows < ROW0 + n_tokens)
        err = h2 - n[0]
        return [jnp.where(real, 0.5 * jnp.mean(err * err, axis=-1, keepdims=True), 0.0)]
    return st


CONV_BC = 1024
HALO = 16


def _conv_fwd(x, w, *, col_blocks, name):
    lp = x.shape[0]
    tr = _pick(lp, (768, 512, 256))

    def body(x_ref, xp_ref, w_ref, o_ref):
        i = pl.program_id(0)
        xv = x_ref[...].astype(f32)
        prev = jnp.where(i > 0, xp_ref[...].astype(f32), 0.0)
        xc = jnp.concatenate([prev, xv], axis=0)
        wv = w_ref[...]
        acc = wv[3:4, :] * xv
        for j in range(3):
            acc = acc + wv[j:j + 1, :] * pltpu.roll(xc, 3 - j, 0)[HALO:, :]
        o_ref[...] = acc.astype(o_ref.dtype)

    return pl.pallas_call(
        body, name=name, grid=(lp // tr, col_blocks),
        in_specs=[pl.BlockSpec((tr, CONV_BC), lambda i, j: (i, j)),
                  pl.BlockSpec((HALO, CONV_BC), lambda i, j: (jnp.maximum(i * (tr // HALO) - 1, 0), j)),
                  pl.BlockSpec((4, CONV_BC), lambda i, j: (0, j))],
        out_specs=pl.BlockSpec((tr, CONV_BC), lambda i, j: (i, j)),
        out_shape=jax.ShapeDtypeStruct((lp, col_blocks * CONV_BC), bf16),
    )(x, x, w)


def _conv_bwd(dc, x, w, *, x_off, w_off, name):
    lp, width = dc.shape
    tr = _pick(lp, (768, 512, 256))
    ncb, nrow = width // CONV_BC, lp // tr

    def body(dc_ref, dcn_ref, x_ref, xp_ref, w_ref, dx_ref, dw_ref):
        i = pl.program_id(1)
        nxt = jnp.where(i < nrow - 1, dcn_ref[...], 0.0)
        dcv = dc_ref[...]
        dcc = jnp.concatenate([dcv, nxt], axis=0)
        xv = x_ref[...].astype(f32)
        prev = jnp.where(i > 0, xp_ref[...].astype(f32), 0.0)
        xc = jnp.concatenate([prev, xv], axis=0)
        wv = w_ref[...]
        dx = wv[3:4, :] * dcv
        dws = [None] * 4
        dws[3] = jnp.sum(dcv * xv, axis=0, keepdims=True)
        for j in range(3):
            dx = dx + wv[j:j + 1, :] * pltpu.roll(dcc, tr + 8 - (3 - j), 0)[:tr, :]
            dws[j] = jnp.sum(dcv * pltpu.roll(xc, 3 - j, 0)[HALO:, :], axis=0, keepdims=True)
        dx_ref[...] = dx.astype(dx_ref.dtype)

        @pl.when(i == 0)
        def _():
            for j in range(4):
                dw_ref[j:j + 1, :] = dws[j]

        @pl.when(i > 0)
        def _():
            for j in range(4):
                dw_ref[j:j + 1, :] += dws[j]

    last8 = lp // 8 - 1
    return pl.pallas_call(
        body, name=name, grid=(ncb, nrow),
        in_specs=[pl.BlockSpec((tr, CONV_BC), lambda j, i: (i, j)),
                  pl.BlockSpec((8, CONV_BC), lambda j, i: (jnp.minimum((i + 1) * (tr // 8), last8), j)),
                  pl.BlockSpec((tr, CONV_BC), lambda j, i: (i, j + x_off)),
                  pl.BlockSpec((HALO, CONV_BC), lambda j, i: (jnp.maximum(i * (tr // HALO) - 1, 0), j + x_off)),
                  pl.BlockSpec((4, CONV_BC), lambda j, i: (0, j + w_off))],
        out_specs=[pl.BlockSpec((tr, CONV_BC), lambda j, i: (i, j)),
                   pl.BlockSpec((4, CONV_BC), lambda j, i: (0, j))],
        out_shape=[jax.ShapeDtypeStruct((lp, width), bf16), jax.ShapeDtypeStruct((4, width), f32)],
        compiler_params=pltpu.CompilerParams(dimension_semantics=("arbitrary", "arbitrary")),
    )(dc, dc, x, x, w)


GDN_PACK = 4
GDN_FWD_INTERLEAVE, GDN_BWD_INTERLEAVE = 4, 4


MXU_TILE = 256


def _bd(x, cb, g=GDN_PACK):
    r = x.shape[0]
    tall = jnp.concatenate([x] * g, axis=0)
    rows = lax.broadcasted_iota(jnp.int32, tall.shape, 0) // r
    cols = lax.broadcasted_iota(jnp.int32, tall.shape, 1) // cb
    return jnp.where(rows == cols, tall, jnp.zeros_like(tall))


def _diag(full, r, cb, g=GDN_PACK):
    cols = lax.broadcasted_iota(jnp.int32, (r, full.shape[1]), 1) // cb
    out = jnp.where(cols == 0, full[0:r, :], 0.0)
    for a in range(1, g):
        out = out + jnp.where(cols == a, full[a * r:(a + 1) * r, :], 0.0)
    return out


def _stack(x, cb):
    return jnp.concatenate([x[:, a * cb:(a + 1) * cb] for a in range(GDN_PACK)], axis=0)


def _lane_halves(a):
    return a[:, :a.shape[1] // 2], a[:, a.shape[1] // 2:]


def _make_packed(dot):
    G = GDN_PACK

    def two_by_two(k, cb):
        return k * G > MXU_TILE and cb * G > MXU_TILE

    def times_bd(x, y, dims):
        k, cb = y.shape[0], y.shape[1] // G
        if two_by_two(k, cb):
            return jnp.concatenate([dot(xh, _bd(yh, cb, G // 2), dims)
                                    for xh, yh in zip(_lane_halves(x), _lane_halves(y))], axis=1)
        return dot(x, _bd(y, cb), dims)

    def tn_diag(x, y):
        k, cb = x.shape[1] // G, y.shape[1] // G
        if two_by_two(k, cb):
            return jnp.concatenate([_diag(dot(xh, yh, TN), k, cb, G // 2)
                                    for xh, yh in zip(_lane_halves(x), _lane_halves(y))], axis=1)
        return _diag(dot(x, y, TN), k, cb)

    @jax.custom_vjp
    def pmm(x, y):
        return times_bd(x, y, NN)

    @jax.custom_vjp
    def pnt(x, y):
        k = x.shape[1] // G
        return _diag(dot(_stack(x, k), _stack(y, k), NT), x.shape[0], y.shape[0])

    @jax.custom_vjp
    def ptn(x, y):
        return tn_diag(x, y)

    def pmm_bwd(res, ct):
        x, y = res
        return times_bd(ct, y, NT), tn_diag(x, ct)

    pmm.defvjp(lambda x, y: (pmm(x, y), (x, y)), pmm_bwd)
    pnt.defvjp(lambda x, y: (pnt(x, y), (x, y)), lambda res, ct: (pmm(ct, res[1]), ptn(ct, res[0])))
    ptn.defvjp(lambda x, y: (ptn(x, y), (x, y)), lambda res, ct: (pnt(res[1], ct), pmm(res[0], ct)))
    return pmm, pnt, ptn


_pmm, _pnt, _ptn = _make_packed(_dlo)


@jax.custom_vjp
def _inv_packed(ms):
    c = ms[0].shape[0]
    ii = lax.broadcasted_iota(jnp.int32, ms[0].shape, 0)
    jj = lax.broadcasted_iota(jnp.int32, ms[0].shape, 1) % c
    ts = [jnp.where(ii == jj, 1.0, 0.0) - m for m in ms]
    ps = [(-m).astype(bf16) for m in ms]
    for _ in range(int(math.log2(c)) - 1):
        ps = [_dlo(p, _bd(p, c), NN).astype(bf16) for p in ps]
        ts = [t + _dlo(t, _bd(p, c), NN) for t, p in zip(ts, ps)]
    return tuple(ts)


def _inv_packed_fwd(ms):
    ts = _inv_packed(ms)
    return ts, ts


def _inv_packed_bwd(ts, cts):
    c = ts[0].shape[0]
    ys = [_diag(_dlo(t, ct, TN), c, c) for t, ct in zip(ts, cts)]
    return (tuple(-_dlo(y, _bd(t.astype(bf16), c), NT) for y, t in zip(ys, ts)),)


_inv_packed.defvjp(_inv_packed_fwd, _inv_packed_bwd)


def _gdn_prep(q2, k2, v4, bcols, gcols, grows):
    c, d = v4.shape[0], GDN_D
    q4 = jnp.concatenate([q2[:, :d], q2[:, :d], q2[:, d:], q2[:, d:]], axis=1)
    k4 = jnp.concatenate([k2[:, :d], k2[:, :d], k2[:, d:], k2[:, d:]], axis=1)
    beta4 = jnp.concatenate([jnp.broadcast_to(b, (c, d)) for b in bcols], axis=1)
    gc4 = jnp.concatenate([jnp.broadcast_to(g, (c, d)) for g in gcols], axis=1)
    low = lax.broadcasted_iota(jnp.int32, (c, 128), 1) < c
    gi = jnp.concatenate([jnp.where(low, gcols[0], gcols[1]), jnp.where(low, gcols[2], gcols[3])], axis=1)
    gj = jnp.concatenate([jnp.where(low, grows[0], grows[1]), jnp.where(low, grows[2], grows[3])], axis=1)
    ii = lax.broadcasted_iota(jnp.int32, gi.shape, 0)
    jj = lax.broadcasted_iota(jnp.int32, gi.shape, 1) % c
    dec = jnp.exp(jnp.where(ii >= jj, gi - gj, NEG))
    rid = lax.broadcasted_iota(jnp.int32, gc4.shape, 0)
    glast = jnp.sum(jnp.where(rid == c - 1, gc4, 0.0), axis=0, keepdims=True)
    eg = jnp.exp(gc4)
    kb = k4 * beta4
    return dict(q=q4, k=k4, kb=kb, vb=v4 * beta4, kbe=kb * eg, qe=q4 * eg, dec=dec, dec_strict=jnp.where(ii > jj, dec, 0.0),
                sdecay=jnp.exp(glast), kd=k4 * jnp.exp(glast - gc4))


@jax.custom_vjp
def _inv_packed_known(ms, ts):
    return ts


_inv_packed_known.defvjp(lambda ms, ts: (ts, ts),
                         lambda ts, cts: (_inv_packed_bwd(ts, cts)[0], tuple(jnp.zeros_like(t) for t in ts)))


def _gdn_groups(groups, known_inverses=None, with_inverses=False):
    c = groups[0][3].shape[0]
    ss = [g[0] for g in groups]
    e = [_gdn_prep(*g[1:]) for g in groups]
    ms = tuple(_pnt(x["kb"], x["k"]) * x["dec_strict"] for x in e)
    ts = _inv_packed(ms) if known_inverses is None else _inv_packed_known(ms, tuple(known_inverses))
    us = [_pmm(t, x["vb"]) for t, x in zip(ts, e)]
    ws = [_pmm(t, x["kbe"]) for t, x in zip(ts, e)]
    attns = [_pnt(x["q"], x["k"]) * x["dec"] for x in e]
    ws_qs = [_pmm(jnp.concatenate([w, x["qe"]], axis=0), s) for w, x, s in zip(ws, e, ss)]
    v_news = [u - y[:c] for u, y in zip(us, ws_qs)]
    os = [y[c:] + _pmm(a, vn) for y, a, vn in zip(ws_qs, attns, v_news)]
    s_news = [s * x["sdecay"] + _ptn(x["kd"], vn) for s, x, vn in zip(ss, e, v_news)]
    if with_inverses:
        return list(zip(os, s_news)), list(ts)
    return list(zip(os, s_news))


def _lane_pick(x, h):
    lane = lax.broadcasted_iota(jnp.int32, x.shape, 1)
    return jnp.sum(jnp.where(lane == h, x, 0.0), axis=1, keepdims=True)


def _cum_log_decay(g):
    c = g.shape[0]
    lower = (lax.broadcasted_iota(jnp.int32, (c, c), 0) >= lax.broadcasted_iota(jnp.int32, (c, c), 1)).astype(f32)
    upper2 = (lax.broadcasted_iota(jnp.int32, (c, 128), 0) <= lax.broadcasted_iota(jnp.int32, (c, 128), 1) % c).astype(f32)
    return _dsel(lower, g, NN, 0), _dsel(g, upper2, TN, 1)


def _group_operands(gi, s_ref, q_ref, k_ref, v_ref, bv, gcv, gct_s):
    heads = [gi * GDN_PACK + u for u in range(GDN_PACK)]
    qk_off = pl.multiple_of(gi * 2 * GDN_D, 2 * GDN_D)
    v_off = pl.multiple_of(gi * GDN_PACK * GDN_D, GDN_PACK * GDN_D)
    return (s_ref[gi], q_ref[:, pl.ds(qk_off, 2 * GDN_D)].astype(f32), k_ref[:, pl.ds(qk_off, 2 * GDN_D)].astype(f32),
            v_ref[:, pl.ds(v_off, GDN_PACK * GDN_D)].astype(f32),
            [_lane_pick(bv, h) for h in heads], [_lane_pick(gcv, h) for h in heads],
            [gct_s[pl.ds(h, 1), :] for h in heads]), heads, qk_off, v_off


def _gdn_fwd(qn, kn, v, beta, g, *, n_real_chunks, name):
    lp = qn.shape[0]
    nchunk = lp // GDN_CHUNK
    C, D = GDN_CHUNK, GDN_D
    NG, SW = GDN_V_HEADS // GDN_PACK, GDN_PACK * GDN_D

    def body(q_ref, k_ref, v_ref, b_ref, g_ref, o_ref, st_ref, inv_ref, s_s, gc_s, gct_s):
        ci = pl.program_id(0)

        @pl.when(ci == 0)
        def _():
            s_s[...] = jnp.zeros_like(s_s)

        @pl.when(ci >= n_real_chunks)
        def _():
            o_ref[...] = jnp.zeros_like(o_ref)
            st_ref[...] = jnp.zeros_like(st_ref)
            inv_ref[...] = jnp.zeros_like(inv_ref)

        @pl.when(ci < n_real_chunks)
        def _():
            gc, gct = _cum_log_decay(g_ref[...])
            gc_s[...] = gc
            gct_s[...] = gct

            def some_groups(it, carry):
                ids = [it * GDN_FWD_INTERLEAVE + u for u in range(GDN_FWD_INTERLEAVE)]
                ops = [_group_operands(gi, s_s, q_ref, k_ref, v_ref, b_ref[...], gc_s[...], gct_s) for gi in ids]
                res, inverses = _gdn_groups([op[0] for op in ops], with_inverses=True)
                for gi, op, (o, s_new), t in zip(ids, ops, res, inverses):
                    st_ref[gi] = op[0][0]
                    inv_ref[gi] = t
                    s_s[gi] = s_new
                    o_ref[:, pl.ds(op[3], SW)] = o
                return carry

            lax.fori_loop(0, NG // GDN_FWD_INTERLEAVE, some_groups, 0)

    return pl.pallas_call(
        body, name=name, grid=(nchunk,),
        in_specs=[pl.BlockSpec((C, GDN_QK_W), lambda c: (c, 0)), pl.BlockSpec((C, GDN_QK_W), lambda c: (c, 0)),
                  pl.BlockSpec((C, GDN_V_W), lambda c: (c, 0)), pl.BlockSpec((C, 128), lambda c: (c, 0)),
                  pl.BlockSpec((C, 128), lambda c: (c, 0))],
        out_specs=[pl.BlockSpec((C, GDN_V_W), lambda c: (c, 0)),
                   pl.BlockSpec((None, NG, D, SW), lambda c: (c, 0, 0, 0)),
                   pl.BlockSpec((None, NG, C, GDN_PACK * C), lambda c: (c, 0, 0, 0))],
        out_shape=[jax.ShapeDtypeStruct((lp, GDN_V_W), f32), jax.ShapeDtypeStruct((nchunk, NG, D, SW), f32),
                   jax.ShapeDtypeStruct((nchunk, NG, C, GDN_PACK * C), f32)],
        scratch_shapes=[pltpu.VMEM((NG, D, SW), f32), pltpu.VMEM((C, 128), f32), pltpu.VMEM((128, 128), f32)],
        compiler_params=pltpu.CompilerParams(dimension_semantics=("arbitrary",)),
    )(qn, kn, v, beta, g)


def _gdn_bwd(qn, kn, v, beta, g, states, inverses, do, *, n_real_chunks, name):
    lp = qn.shape[0]
    nchunk = lp // GDN_CHUNK
    C, D = GDN_CHUNK, GDN_D
    NG, SW = GDN_V_HEADS // GDN_PACK, GDN_PACK * GDN_D
    rev = lambda i: (nchunk - 1 - i, 0)

    def body(q_ref, k_ref, v_ref, b_ref, g_ref, st_ref, inv_ref, do_ref,
             dq_ref, dk_ref, dv_ref, db_ref, dg_ref, ds_s, gc_s, gct_s, dgc_s, dgct_s, dbeta_s):
        step = pl.program_id(0)
        ci = nchunk - 1 - step

        @pl.when(step == 0)
        def _():
            ds_s[...] = jnp.zeros_like(ds_s)

        @pl.when(ci >= n_real_chunks)
        def _():
            for r in (dq_ref, dk_ref, dv_ref, db_ref, dg_ref):
                r[...] = jnp.zeros_like(r)

        @pl.when(ci < n_real_chunks)
        def _():
            gc, gct = _cum_log_decay(g_ref[...])
            gc_s[...] = gc
            gct_s[...] = gct
            dgc_s[...] = jnp.zeros_like(dgc_s)
            dgct_s[...] = jnp.zeros_like(dgct_s)
            dbeta_s[...] = jnp.zeros_like(dbeta_s)

            def some_groups(it, carry):
                ids = [it * GDN_BWD_INTERLEAVE + u for u in range(GDN_BWD_INTERLEAVE)]
                ops = [_group_operands(gi, st_ref, q_ref, k_ref, v_ref, b_ref[...], gc_s[...], gct_s) for gi in ids]
                cts = [(do_ref[:, pl.ds(op[3], SW)], ds_s[gi]) for gi, op in zip(ids, ops)]
                known = [inv_ref[gi] for gi in ids]
                _, vjp_fn = jax.vjp(lambda gs: _gdn_groups(gs, known_inverses=known), [op[0] for op in ops])
                (grads,) = vjp_fn(cts)
                lane = lax.broadcasted_iota(jnp.int32, (C, 128), 1)
                dbeta_acc, dgc_acc = dbeta_s[...], dgc_s[...]
                for gi, (_, heads, qk_off, v_off), (dsp, dq2, dk2, dv4, dbcols, dgcols, dgrows) in zip(ids, ops, grads):
                    ds_s[gi] = dsp
                    dq_ref[:, pl.ds(qk_off, 2 * D)] = dq2
                    dk_ref[:, pl.ds(qk_off, 2 * D)] = dk2
                    dv_ref[:, pl.ds(v_off, SW)] = dv4
                    for h, dbcol, dgcol, dgrow in zip(heads, dbcols, dgcols, dgrows):
                        dbeta_acc = dbeta_acc + jnp.where(lane == h, dbcol, 0.0)
                        dgc_acc = dgc_acc + jnp.where(lane == h, dgcol, 0.0)
                        dgct_s[pl.ds(h, 1), :] = dgrow
                dbeta_s[...] = dbeta_acc
                dgc_s[...] = dgc_acc
                return carry

            lax.fori_loop(0, NG // GDN_BWD_INTERLEAVE, some_groups, 0)
            fold = (lax.broadcasted_iota(jnp.int32, (128, C), 0) % C == lax.broadcasted_iota(jnp.int32, (128, C), 1)).astype(f32)
            eye = (lax.broadcasted_iota(jnp.int32, (128, 128), 0) == lax.broadcasted_iota(jnp.int32, (128, 128), 1)).astype(f32)
            dgc = dgc_s[...] + _dsel(_dsel(dgct_s[...], fold, NN, 1), eye, TN, 1)
            upper = (lax.broadcasted_iota(jnp.int32, (C, C), 0) <= lax.broadcasted_iota(jnp.int32, (C, C), 1)).astype(f32)
            dg_ref[...] = _dsel(upper, dgc, NN, 0)
            db_ref[...] = dbeta_s[...]

    return pl.pallas_call(
        body, name=name, grid=(nchunk,),
        in_specs=[pl.BlockSpec((C, GDN_QK_W), rev), pl.BlockSpec((C, GDN_QK_W), rev), pl.BlockSpec((C, GDN_V_W), rev),
                  pl.BlockSpec((C, 128), rev), pl.BlockSpec((C, 128), rev),
                  pl.BlockSpec((None, NG, D, SW), lambda i: (nchunk - 1 - i, 0, 0, 0)),
                  pl.BlockSpec((None, NG, C, GDN_PACK * C), lambda i: (nchunk - 1 - i, 0, 0, 0)), pl.BlockSpec((C, GDN_V_W), rev)],
        out_specs=[pl.BlockSpec((C, GDN_QK_W), rev), pl.BlockSpec((C, GDN_QK_W), rev), pl.BlockSpec((C, GDN_V_W), rev),
                   pl.BlockSpec((C, 128), rev), pl.BlockSpec((C, 128), rev)],
        out_shape=[jax.ShapeDtypeStruct((lp, GDN_QK_W), f32)] * 2 + [jax.ShapeDtypeStruct((lp, GDN_V_W), f32)]
        + [jax.ShapeDtypeStruct((lp, 128), f32)] * 2,
        scratch_shapes=[pltpu.VMEM((NG, D, SW), f32), pltpu.VMEM((C, 128), f32), pltpu.VMEM((128, 128), f32),
                        pltpu.VMEM((C, 128), f32), pltpu.VMEM((128, 128), f32), pltpu.VMEM((C, 128), f32)],
        compiler_params=pltpu.CompilerParams(dimension_semantics=("arbitrary",)),
    )(qn, kn, v, beta, g, states, inverses, do)


def _attention_bias(tq, tk):
    kk = lax.broadcasted_iota(jnp.int32, (tk, tq), 0)
    qq = lax.broadcasted_iota(jnp.int32, (tk, tq), 1)
    pad = jnp.where(kk < PAD_FRONT, NEG, 0.0)
    diag = [jnp.where(qq >= kk + d * tk, 0.0, NEG) for d in range(tq // tk)]
    return jnp.stack([pad] + diag + [jnp.minimum(pad, diag[0])]).astype(f32)


def _att_tiles(lp):
    tq = _pick(lp, (768, 512, 256))
    return tq, _pick(tq, ATT_KEY_TILES)


def _attention_fwd(q, kvu, kr, proj2, *, name):
    lp = q.shape[0]
    H = MLA_HEADS
    tq, tk = _att_tiles(lp)
    r = tq // tk

    def body(q_ref, kn_ref, kr_ref, v_ref, z_ref, bias_ref, o_ref, lse_ref, og_ref, m_s, l_s, acc_s, sa_s, sb_s):
        qi = pl.program_id(1)
        m_s[...] = jnp.full_like(m_s, NEG)
        l_s[...] = jnp.zeros_like(l_s)
        acc_s[...] = jnp.zeros_like(acc_s)

        def scores(ki):
            k0 = pl.multiple_of(ki * tk, tk)
            k = jnp.concatenate([kn_ref[pl.ds(k0, tk), :], kr_ref[pl.ds(k0, tk), :]], axis=1)
            return lax.dot_general(k, q_ref[...], (NT, ((), ())), preferred_element_type=f32)

        def consume(st, ki, mask):
            k0 = pl.multiple_of(ki * tk, tk)
            if mask is not None:
                st = st + bias_ref[mask]
            m_prev = m_s[...]
            m_new = jnp.maximum(m_prev, jnp.max(st, axis=0, keepdims=True))
            alpha = jnp.exp(m_prev - m_new)
            p = jnp.exp(st - m_new)
            l_s[...] = alpha * l_s[...] + jnp.sum(p, axis=0, keepdims=True)
            acc_s[...] = alpha * acc_s[...] + lax.dot_general(v_ref[pl.ds(k0, tk), :], p.astype(bf16), (TN, ((), ())),
                                                              preferred_element_type=f32)
            m_s[...] = m_new

        n_full = qi * r

        def chain(blocks):
            bufs = (sa_s, sb_s)
            for j, (ki, masked) in enumerate(blocks):
                if j + 1 < len(blocks):
                    bufs[(j + 1) % 2][...] = scores(blocks[j + 1][0])
                consume(bufs[j % 2][...], ki, masked)

        diagonal = [(n_full + d, 1 + d) for d in range(r)]

        @pl.when(qi == 0)
        def _():
            sa_s[...] = scores(0)
            chain([(0, r + 1)] + diagonal[1:])

        @pl.when(qi > 0)
        def _():
            sb_s[...] = scores(0)
            sa_s[...] = scores(1)
            consume(sb_s[...], 0, 0)
            n_pairs = (n_full - 1) // 2

            def two(pi, carry):
                ki = 1 + 2 * pi
                sb_s[...] = scores(ki + 1)
                consume(sa_s[...], ki, None)
                sa_s[...] = scores(ki + 2)
                consume(sb_s[...], ki + 1, None)
                return carry

            lax.fori_loop(0, n_pairs, two, 0)
            nxt = 1 + 2 * n_pairs

            @pl.when(nxt < n_full)
            def _():
                chain([(nxt, None)] + diagonal)

            @pl.when(nxt == n_full)
            def _():
                chain(diagonal)
        o = jnp.transpose(acc_s[...] / l_s[...])
        o_ref[...] = o
        og_ref[...] = (o * _silu(z_ref[...].astype(f32))).astype(og_ref.dtype)
        lse_ref[...] = m_s[...] + jnp.log(l_s[...])

    return pl.pallas_call(
        body, name=name, grid=(H, lp // tq),
        in_specs=[pl.BlockSpec((tq, MLA_QKP), lambda h, qi: (qi, h)),
                  pl.BlockSpec((lp, 128), lambda h, qi: (0, h)),
                  pl.BlockSpec((lp, 128), lambda h, qi: (0, 0)),
                  pl.BlockSpec((lp, 128), lambda h, qi: (0, H + h)),
                  pl.BlockSpec((tq, 128), lambda h, qi: (qi, MLA_Q_RANK // 128 + h)),
                  pl.BlockSpec((r + 2, tk, tq), lambda h, qi: (0, 0, 0))],
        out_specs=[pl.BlockSpec((tq, 128), lambda h, qi: (qi, h)),
                   pl.BlockSpec((None, 1, tq), lambda h, qi: (h, 0, qi)),
                   pl.BlockSpec((tq, 128), lambda h, qi: (qi, h))],
        out_shape=[jax.ShapeDtypeStruct((lp, MLA_V_W), f32), jax.ShapeDtypeStruct((H, 1, lp), f32),
                   jax.ShapeDtypeStruct((lp, MLA_V_W), bf16)],
        scratch_shapes=[pltpu.VMEM((1, tq), f32), pltpu.VMEM((1, tq), f32), pltpu.VMEM((128, tq), f32),
                        pltpu.VMEM((tk, tq), f32), pltpu.VMEM((tk, tq), f32)],
        compiler_params=pltpu.CompilerParams(dimension_semantics=("arbitrary", "arbitrary")),
    )(q, kvu, kr, kvu, proj2, _attention_bias(tq, tk))


def _gate_bwd(o, proj2, dgated, *, name):
    lp = o.shape[0]
    H, w = MLA_HEADS, 2 * MLA_V
    tq = _pick(lp, (768, 512, 256))

    def body(o_ref, z_ref, g_ref, do_ref, dz_ref, dl_ref):
        ov, z, g = o_ref[...], z_ref[...].astype(f32), g_ref[...]
        s = _sigmoid(z)
        do = (g * (z * s)).astype(bf16)
        do_ref[...] = do
        dz_ref[...] = (g * ov * (s * (1.0 + z * (1.0 - s)))).astype(dz_ref.dtype)
        prod = ov * do.astype(f32)
        for u in range(2):
            dl_ref[u] = jnp.sum(jnp.transpose(prod[:, u * MLA_V:(u + 1) * MLA_V]), axis=0, keepdims=True)

    blk = pl.BlockSpec((tq, w), lambda j, qi: (qi, j))
    return pl.pallas_call(
        body, name=name, grid=(H // 2, lp // tq),
        in_specs=[blk, pl.BlockSpec((tq, w), lambda j, qi: (qi, j + MLA_Q_RANK // w)), blk],
        out_specs=[blk, blk, pl.BlockSpec((2, 1, tq), lambda j, qi: (j, 0, qi))],
        out_shape=[jax.ShapeDtypeStruct((lp, MLA_V_W), bf16), jax.ShapeDtypeStruct((lp, MLA_V_W), bf16),
                   jax.ShapeDtypeStruct((H, 1, lp), f32)],
    )(o, proj2, dgated)


def _attention_bwd(q, kvu, kr, lse, delta, do, *, name):
    lp = q.shape[0]
    tq, t = _att_tiles(lp)
    H, nb = MLA_HEADS, lp // t
    r, nq = tq // t, lp // tq

    def body(q_ref, kn_ref, kr_ref, v_ref, lse_ref, dl_ref, do_ref, bias_ref, dq_ref, dkn_ref, dv_ref, dkr_ref, dk_s, dv_s,
             sa_s, da_s, sb_s, db_s):
        ki = pl.program_id(1)
        k = jnp.concatenate([kn_ref[...], kr_ref[...]], axis=1)
        vv = v_ref[...]
        dk_s[...] = jnp.zeros_like(dk_s)
        dv_s[...] = jnp.zeros_like(dv_s)

        def products(qi, s_ref, d_ref):
            q0 = pl.multiple_of(qi * tq, tq)
            s_ref[...] = lax.dot_general(k, q_ref[pl.ds(q0, tq), :], (NT, ((), ())), preferred_element_type=f32)
            d_ref[...] = lax.dot_general(vv, do_ref[pl.ds(q0, tq), :], (NT, ((), ())), preferred_element_type=f32)

        def accumulate(s_ref, d_ref, qi, mask, first):
            q0 = pl.multiple_of(qi * tq, tq)
            qv = q_ref[pl.ds(q0, tq), :]
            dob = do_ref[pl.ds(q0, tq), :]
            st = s_ref[...]
            if mask is not None:
                st = st + bias_ref[mask]
            p = jnp.exp(st - lse_ref[:, pl.ds(q0, tq)])
            dv_s[...] += jnp.dot(p.astype(bf16), dob, preferred_element_type=f32)
            ds = (p * (d_ref[...] - dl_ref[:, pl.ds(q0, tq)])).astype(bf16)
            dk_s[...] += jnp.dot(ds, qv, preferred_element_type=f32)
            dq = lax.dot_general(ds, k, (TN, ((), ())), preferred_element_type=f32)
            if first:
                dq_ref[pl.ds(q0, tq), :] = dq
            else:
                dq_ref[pl.ds(q0, tq), :] += dq

        def sweep(qd, first_mask, other_mask, first):
            last = nq - 1
            products(qd, sa_s, da_s)
            products(jnp.minimum(qd + 1, last), sb_s, db_s)
            accumulate(sa_s, da_s, qd, first_mask, first)
            n = last - qd

            def two(pi, carry):
                i = qd + 1 + 2 * pi
                products(i + 1, sa_s, da_s)
                accumulate(sb_s, db_s, i, other_mask, first)
                products(jnp.minimum(i + 2, last), sb_s, db_s)
                accumulate(sa_s, da_s, i + 1, other_mask, first)
                return carry

            lax.fori_loop(0, n // 2, two, 0)

            @pl.when(n % 2 == 1)
            def _():
                accumulate(sb_s, db_s, last, other_mask, first)

        @pl.when(ki == 0)
        def _():
            sweep(0, r + 1, 0, True)

        @pl.when(ki > 0)
        def _():
            sweep(ki // r, 1 + ki % r, None, False)

        dkn_ref[...] = dk_s[:, :128].astype(dkn_ref.dtype)
        dkr_ref[...] = dk_s[:, 128:]
        dv_ref[...] = dv_s[...].astype(dv_ref.dtype)

    return pl.pallas_call(
        body, name=name, grid=(H, nb),
        in_specs=[pl.BlockSpec((lp, MLA_QKP), lambda h, ki: (0, h)),
                  pl.BlockSpec((t, 128), lambda h, ki: (ki, h)),
                  pl.BlockSpec((t, 128), lambda h, ki: (ki, 0)),
                  pl.BlockSpec((t, 128), lambda h, ki: (ki, H + h)),
                  pl.BlockSpec((None, 1, lp), lambda h, ki: (h, 0, 0)),
                  pl.BlockSpec((None, 1, lp), lambda h, ki: (h, 0, 0)),
                  pl.BlockSpec((lp, 128), lambda h, ki: (0, h)),
                  pl.BlockSpec((r + 2, t, tq), lambda h, ki: (0, 0, 0))],
        out_specs=[pl.BlockSpec((lp, MLA_QKP), lambda h, ki: (0, h)),
                   pl.BlockSpec((t, 128), lambda h, ki: (ki, h)),
                   pl.BlockSpec((t, 128), lambda h, ki: (ki, h)),
                   pl.BlockSpec((t, 128), lambda h, ki: (ki, h))],
        out_shape=[jax.ShapeDtypeStruct((lp, H * MLA_QKP), f32), jax.ShapeDtypeStruct((lp, MLA_V_W), bf16),
                   jax.ShapeDtypeStruct((lp, MLA_V_W), bf16), jax.ShapeDtypeStruct((lp, MLA_V_W), f32)],
        scratch_shapes=[pltpu.VMEM((t, MLA_QKP), f32), pltpu.VMEM((t, 128), f32)] + [pltpu.VMEM((t, tq), f32)] * 4,
        compiler_params=pltpu.CompilerParams(dimension_semantics=("arbitrary", "arbitrary")),
    )(q, kvu, kr, kvu, lse, delta, do, _attention_bias(tq, t))


def _q_proj_fwd(c_q, wq2, cos_q, sin_q, *, name, tm=ROW_TILE):
    lp, rank = c_q.shape
    gw = Q_GROUP * MLA_QKP
    ng = wq2.shape[1] // (2 * gw)

    def body(c_ref, w_ref, cos_ref, sin_ref, o_ref):
        qq = jnp.dot(c_ref[...], w_ref[...], preferred_element_type=f32)
        o_ref[...] = _st_q_rope([qq], [cos_ref[...], sin_ref[...]], [])[0].astype(o_ref.dtype)

    tab = pl.BlockSpec((tm, MLA_QKP), lambda j, i: (i, 0))
    return pl.pallas_call(
        body, name=name, grid=(ng, lp // tm),
        in_specs=[pl.BlockSpec((tm, rank), lambda j, i: (i, 0)), pl.BlockSpec((rank, 2 * gw), lambda j, i: (0, j)), tab, tab],
        out_specs=pl.BlockSpec((tm, gw), lambda j, i: (i, j)),
        out_shape=jax.ShapeDtypeStruct((lp, ng * gw), bf16),
    )(c_q, wq2, cos_q, sin_q)


def _q_proj_bwd(dq, c_q, wq2, cos_q, sin_q, *, name, tm=ROW_TILE):
    lp, rank = c_q.shape
    gw = Q_GROUP * MLA_QKP
    ng = wq2.shape[1] // (2 * gw)

    def body(dq_ref, c_ref, w_ref, cos_ref, sin_ref, dc_ref, dw_ref):
        dqq = _st_q_rope_t([dq_ref[...]], [cos_ref[...], sin_ref[...]], [])[0].astype(bf16)
        dc_ref[...] = lax.dot_general(dqq, w_ref[...], (NT, ((), ())), preferred_element_type=f32)
        dw = lax.dot_general(c_ref[...], dqq, (TN, ((), ())), preferred_element_type=f32)

        @pl.when(pl.program_id(1) == 0)
        def _():
            dw_ref[...] = dw

        @pl.when(pl.program_id(1) > 0)
        def _():
            dw_ref[...] += dw

    tab = pl.BlockSpec((tm, MLA_QKP), lambda j, i: (i, 0))
    return pl.pallas_call(
        body, name=name, grid=(ng, lp // tm),
        in_specs=[pl.BlockSpec((tm, gw), lambda j, i: (i, j)), pl.BlockSpec((tm, rank), lambda j, i: (i, 0)),
                  pl.BlockSpec((rank, 2 * gw), lambda j, i: (0, j)), tab, tab],
        out_specs=[pl.BlockSpec((tm, rank), lambda j, i: (i, j)), pl.BlockSpec((rank, 2 * gw), lambda j, i: (0, j))],
        out_shape=[jax.ShapeDtypeStruct((lp, ng * rank), f32), jax.ShapeDtypeStruct(wq2.shape, f32)],
        compiler_params=pltpu.CompilerParams(dimension_semantics=("arbitrary", "arbitrary")),
    )(dq, c_q, wq2, cos_q, sin_q)


def _exchange_copies(x_ref, o_ref, send_sems, recv_sems, local_sem, gather):
    mx, my, mc = lax.axis_index("x"), lax.axis_index("y"), lax.axis_index("c")
    me = 4 * mx + 2 * my + mc
    own = pltpu.make_async_copy(x_ref if gather else x_ref.at[me], o_ref.at[me], local_sem)
    sends, arrivals = [], []
    for k in range(1, N_DEV):
        px = 1 - mx if k & 4 else mx
        py = 1 - my if k & 2 else my
        pc = 1 - mc if k & 1 else mc
        peer = 4 * px + 2 * py + pc
        sends.append(pltpu.make_async_remote_copy(
            src_ref=x_ref if gather else x_ref.at[peer], dst_ref=o_ref.at[me],
            send_sem=send_sems.at[k - 1], recv_sem=recv_sems.at[k - 1],
            device_id=(px, py, pc), device_id_type=MESH_ID))
        arrivals.append(pltpu.make_async_remote_copy(
            src_ref=o_ref.at[peer], dst_ref=o_ref.at[peer],
            send_sem=send_sems.at[k - 1], recv_sem=recv_sems.at[k - 1],
            device_id=(mx, my, mc), device_id_type=MESH_ID))
    return own, sends, arrivals


def _exchange_start(copies):
    own, sends, _ = copies
    own.start()
    for cp in sends:
        cp.start()


def _exchange_wait(copies):
    own, sends, arrivals = copies
    for cp in arrivals:
        cp.wait_recv()
    for cp in sends:
        cp.wait_send()
    own.wait()


_EXCHANGE_SCRATCH = [pltpu.SemaphoreType.DMA((N_DEV - 1,)), pltpu.SemaphoreType.DMA((N_DEV - 1,)), pltpu.SemaphoreType.DMA]


def _exchange(x, *, gather, name):
    blk = x.shape if gather else x.shape[1:]

    def body(x_ref, o_ref, send_sems, recv_sems, local_sem):
        copies = _exchange_copies(x_ref, o_ref, send_sems, recv_sems, local_sem, gather)
        _exchange_start(copies)
        _exchange_wait(copies)

    return pl.pallas_call(
        body, name=name,
        in_specs=[pl.BlockSpec(memory_space=pltpu.HBM)], out_specs=pl.BlockSpec(memory_space=pltpu.HBM),
        out_shape=jax.ShapeDtypeStruct((N_DEV,) + tuple(blk), x.dtype),
        scratch_shapes=list(_EXCHANGE_SCRATCH),
    )(x)


def _reduce_adamw(parts, w, m, v, *, name):
    r = w.shape[0]
    tr = max(d for d in range(16, 3201, 16) if r % d == 0)

    def body(p_ref, w_ref, m_ref, v_ref, g_ref, d_ref, nm_ref, nv_ref):
        g = p_ref[0].astype(f32)
        for s in range(1, N_DEV):
            g = g + p_ref[s].astype(f32)
        mm = ADAM_B1 * m_ref[...] + (1.0 - ADAM_B1) * g
        vv = ADAM_B2 * v_ref[...] + (1.0 - ADAM_B2) * (g * g)
        m_hat = mm / (1.0 - ADAM_B1 ** ADAM_STEP)
        v_hat = vv / (1.0 - ADAM_B2 ** ADAM_STEP)
        g_ref[...] = g
        d_ref[...] = -ADAM_LR * (m_hat / (jnp.sqrt(v_hat) + ADAM_EPS) + ADAM_WD * w_ref[...])
        nm_ref[...] = mm
        nv_ref[...] = vv

    spec = pl.BlockSpec((tr, 128), lambda i: (i, 0))
    return pl.pallas_call(
        body, name=name, grid=(r // tr,),
        in_specs=[pl.BlockSpec((N_DEV, tr, 128), lambda i: (0, i, 0)), spec, spec, spec],
        out_specs=[spec] * 4, out_shape=[jax.ShapeDtypeStruct((r, 128), f32)] * 4,
    )(parts, w, m, v)


_SHARDED = ("gdn_w_in", "gdn_w_out", "kv_w_down", "kv_w_up", "mla_w_in", "mla_w_q_up", "mla_w_out", "meta_tokens", "gdn_conv_w")
_COL_SHARDED = {"gdn_w_in", "kv_w_up", "mla_w_in", "mla_w_q_up", "meta_tokens", "gdn_conv_w"}
_GATHER_FIRST = ("gdn_w_in",)
_GATHER_F32 = ("meta_tokens", "gdn_conv_w")
_GATHER_REST = ("gdn_w_out", "kv_w_down", "kv_w_up", "mla_w_in", "mla_w_q_up", "mla_w_out")
_SCATTER_EARLY = ("gdn_w_out", "kv_w_down", "kv_w_up", "mla_w_in", "mla_w_q_up", "mla_w_out")
_SCATTER_LATE = ("gdn_w_in", "gdn_conv_w")
_SCATTER_LAST = ("meta_tokens",)
_REPLICATED = ("pre_norm", "post_norm", "gdn_a_log", "gdn_dt_bias", "gdn_out_norm", "kv_norm", "kv_latent_norm",
               "mla_q_latent_norm")


def _rows128(a):
    flat = a.reshape(-1)
    pad = (-flat.shape[0]) % 128
    if pad:
        flat = jnp.pad(flat, (0, pad))
    return flat.reshape(-1, 128)


def _pack(arrs, row_multiple):
    parts = [_rows128(a) for a in arrs]
    buf = jnp.concatenate(parts, axis=0)
    pad = (-buf.shape[0]) % row_multiple
    if pad:
        buf = jnp.pad(buf, ((0, pad), (0, 0)))
    return buf


def _unpack(buf, shapes):
    out, r = [], 0
    for shp in shapes:
        n = math.prod(shp)
        rows = -(-n // 128)
        out.append(buf[r:r + rows].reshape(-1)[:n].reshape(shp))
        r += rows
    return out


def _unshard(g, full_shape, col):
    if col:
        return jnp.transpose(g, (1, 0, 2)).reshape(full_shape)
    return g.reshape(full_shape)


def _to_shards(a, col):
    r, c = a.shape
    if col:
        return jnp.transpose(a.reshape(r, N_DEV, c // N_DEV), (1, 0, 2))
    return a.reshape(N_DEV, r // N_DEV, c)


def _pad_cols(a, width):
    return jnp.pad(a, ((0, 0), (0, width - a.shape[1])))


def _rope_tables(lp):
    inv = ROPE_THETA ** (-jnp.arange(0, MLA_ROPE, 2, dtype=f32) / MLA_ROPE)
    pos = (jnp.arange(lp, dtype=jnp.int32) - PAD_FRONT).astype(f32)
    ang = pos[:, None] * inv[None, :]
    cos, sin = jnp.cos(ang), jnp.sin(ang)
    z = jnp.zeros((lp, 64), f32)
    return jnp.concatenate([cos, cos, z], axis=1), jnp.concatenate([-sin, sin, z], axis=1)


def kernel(x, meta_tokens, pre_norm, post_norm, gdn_w_in, gdn_conv_w, gdn_a_log, gdn_dt_bias, gdn_out_norm, gdn_w_out, kv_norm, kv_w_down, kv_latent_norm, kv_w_up, mla_w_in, mla_q_latent_norm, mla_w_q_up, mla_w_out, loss_target, m_meta_tokens, m_pre_norm, m_post_norm, m_gdn_w_in, m_gdn_conv_w, m_gdn_a_log, m_gdn_dt_bias, m_gdn_out_norm, m_gdn_w_out, m_kv_norm, m_kv_w_down, m_kv_latent_norm, m_kv_w_up, m_mla_w_in, m_mla_q_latent_norm, m_mla_w_q_up, m_mla_w_out, v_meta_tokens, v_pre_norm, v_post_norm, v_gdn_w_in, v_gdn_conv_w, v_gdn_a_log, v_gdn_dt_bias, v_gdn_out_norm, v_gdn_w_out, v_kv_norm, v_kv_w_down, v_kv_latent_norm, v_kv_w_up, v_mla_w_in, v_mla_q_latent_norm, v_mla_w_q_up, v_mla_w_out):
    W = dict(meta_tokens=meta_tokens, pre_norm=pre_norm, post_norm=post_norm, gdn_w_in=gdn_w_in, gdn_conv_w=gdn_conv_w,
             gdn_a_log=gdn_a_log, gdn_dt_bias=gdn_dt_bias, gdn_out_norm=gdn_out_norm, gdn_w_out=gdn_w_out, kv_norm=kv_norm,
             kv_w_down=kv_w_down, kv_latent_norm=kv_latent_norm, kv_w_up=kv_w_up, mla_w_in=mla_w_in,
             mla_q_latent_norm=mla_q_latent_norm, mla_w_q_up=mla_w_q_up, mla_w_out=mla_w_out)
    M = dict(meta_tokens=m_meta_tokens, pre_norm=m_pre_norm, post_norm=m_post_norm, gdn_w_in=m_gdn_w_in, gdn_conv_w=m_gdn_conv_w,
             gdn_a_log=m_gdn_a_log, gdn_dt_bias=m_gdn_dt_bias, gdn_out_norm=m_gdn_out_norm, gdn_w_out=m_gdn_w_out, kv_norm=m_kv_norm,
             kv_w_down=m_kv_w_down, kv_latent_norm=m_kv_latent_norm, kv_w_up=m_kv_w_up, mla_w_in=m_mla_w_in,
             mla_q_latent_norm=m_mla_q_latent_norm, mla_w_q_up=m_mla_w_q_up, mla_w_out=m_mla_w_out)
    V = dict(meta_tokens=v_meta_tokens, pre_norm=v_pre_norm, post_norm=v_post_norm, gdn_w_in=v_gdn_w_in, gdn_conv_w=v_gdn_conv_w,
             gdn_a_log=v_gdn_a_log, gdn_dt_bias=v_gdn_dt_bias, gdn_out_norm=v_gdn_out_norm, gdn_w_out=v_gdn_w_out, kv_norm=v_kv_norm,
             kv_w_down=v_kv_w_down, kv_latent_norm=v_kv_latent_norm, kv_w_up=v_kv_w_up, mla_w_in=v_mla_w_in,
             mla_q_latent_norm=v_mla_q_latent_norm, mla_w_q_up=v_mla_w_q_up, mla_w_out=v_mla_w_out)
    order = list(W)

    n_tok = x.shape[1]
    assert n_tok % GDN_CHUNK == 0
    n_real = ROW0 + n_tok
    lp = -(-n_real // ROW_TILE) * ROW_TILE
    n_real_chunks = n_real // GDN_CHUNK

    shard2d = {n: W[n].reshape(W[n].shape[-2:]) for n in _SHARDED}
    full_shape = {n: ((s.shape[0], s.shape[1] * N_DEV) if n in _COL_SHARDED else (s.shape[0] * N_DEV, s.shape[1]))
                  for n, s in shard2d.items()}
    full = {}

    def unpack_gathered(names, buf):
        r = 0
        for n in names:
            shp = shard2d[n].shape
            rows = math.prod(shp) // 128
            blocks = buf[:, r:r + rows].reshape((N_DEV,) + shp)
            full[n] = _unshard(blocks, full_shape[n], n in _COL_SHARDED)
            r += rows

    unpack_gathered(_GATHER_FIRST, _exchange(_pack([shard2d[n].astype(bf16) for n in _GATHER_FIRST], 16), gather=True,
                                             name="gather_w_in"))
    unpack_gathered(_GATHER_F32, _exchange(_pack([shard2d[n] for n in _GATHER_F32], 8), gather=True, name="gather_meta_conv"))
    rest_shards = _pack([shard2d[n].astype(bf16) for n in _GATHER_REST], 16)

    h0 = jnp.concatenate([jnp.zeros((PAD_FRONT, D_MODEL), f32), full["meta_tokens"], x[0],
                          jnp.zeros((lp - n_real, D_MODEL), f32)], axis=0)
    tgt = jnp.concatenate([jnp.zeros((ROW0, D_MODEL), f32), loss_target[0], jnp.zeros((lp - n_real, D_MODEL), f32)], axis=0)
    cos_k, sin_k = _rope_tables(lp)
    one = jnp.ones((lp, 128), f32)
    cos_q = jnp.concatenate([one, cos_k], axis=1)
    sin_q = jnp.concatenate([jnp.zeros((lp, 128), f32), sin_k], axis=1)

    w_in = full["gdn_w_in"]
    s1 = GDN_CONV_W + GDN_V_W
    w_in_p = jnp.concatenate([w_in[:, :s1], _pad_cols(w_in[:, s1:s1 + 16], 128), _pad_cols(w_in[:, s1 + 16:], 128)], axis=1)
    pre0, pre1 = pre_norm[0:1], pre_norm[1:2]
    post0, post1 = post_norm[0:1], post_norm[1:2]
    (hn0,) = _rowwise(_st_prenorm, [(h0, None, 0)], [], [pre0], [(D_MODEL, bf16, None)], name="f_prenorm0")
    proj, g_rest = _matmul(hn0, w_in_p, out_dtype=bf16, name="f_gdn_in", side=(rest_shards, True))
    unpack_gathered(_GATHER_REST, g_rest)

    wd = full["kv_w_down"]
    zc = jnp.zeros((D_MODEL, 64), bf16)
    wd2 = jnp.concatenate([wd, zc, jnp.zeros((D_MODEL, 128), bf16), wd[:, 160:192], wd[:, 128:160], zc], axis=1)
    wup_p = jnp.transpose(full["kv_w_up"].reshape(MLA_KV_RANK, MLA_HEADS, 2, 128), (0, 2, 1, 3)).reshape(MLA_KV_RANK, 2 * MLA_V_W)
    wq = full["mla_w_q_up"].reshape(MLA_Q_RANK, MLA_HEADS, MLA_QK)
    zq64 = jnp.zeros((MLA_Q_RANK, MLA_HEADS, 64), bf16)
    wq_plain = jnp.concatenate([wq, zq64], axis=2).reshape(MLA_Q_RANK, MLA_HEADS * MLA_QKP)
    wq_swap = jnp.concatenate([jnp.zeros((MLA_Q_RANK, MLA_HEADS, 128), bf16), wq[:, :, 160:192], wq[:, :, 128:160], zq64],
                              axis=2).reshape(MLA_Q_RANK, MLA_HEADS * MLA_QKP)
    q_half = Q_GROUP * MLA_QKP
    wq2 = jnp.concatenate([wq_plain[:, :q_half], wq_swap[:, :q_half], wq_plain[:, q_half:], wq_swap[:, q_half:]], axis=1)
    w_mla_in, w_gdn_out, w_mla_out = full["mla_w_in"], full["gdn_w_out"], full["mla_w_out"]
    conv_w = full["gdn_conv_w"]
    alog_p, dtb_p = _pad_cols(gdn_a_log, 128), _pad_cols(gdn_dt_bias, 128)
    kvn, kvln = kv_norm.reshape(1, -1), kv_latent_norm.reshape(1, -1)

    conv = _conv_fwd(proj, conv_w, col_blocks=GDN_CONV_W // CONV_BC, name="f_conv")
    (qn,) = _rowwise(_st_gdn_q, [(conv, GDN_QK_W, 0)], [], [], [(GDN_QK_W, bf16, GDN_QK_W)], heads=GDN_QK_HEADS, name="f_gdn_q")
    (kn,) = _rowwise(_st_gdn_k, [(conv, GDN_QK_W, 1)], [], [], [(GDN_QK_W, bf16, GDN_QK_W)], heads=GDN_QK_HEADS, name="f_gdn_k")
    (vv,) = _rowwise(_st_gdn_v, [(conv, GDN_V_W, 1)], [], [], [(GDN_V_W, bf16, GDN_V_W)], name="f_gdn_v")
    gate_rows = [(proj, 128, s1 // 128), (proj, 128, s1 // 128 + 1)]
    beta, gdec = _rowwise(_st_gdn_gate, gate_rows, [], [alog_p, dtb_p], [(128, f32, None)] * 2, name="f_gdn_gate")
    o_gdn, states, inverses = _gdn_fwd(qn, kn, vv, beta, gdec, n_real_chunks=n_real_chunks, name="f_gdn")
    out_rows = [(o_gdn, GDN_V_W, 0), (proj, GDN_V_W, GDN_CONV_W // GDN_V_W)]
    (og,) = _rowwise(_st_gdn_out, out_rows, [], [gdn_out_norm], [(GDN_V_W, bf16, GDN_V_W)], heads=GDN_V_HEADS, name="f_gdn_out")
    y0 = _matmul(og, w_gdn_out, name="f_gdn_wout")
    mid_rows = [(h0, None, 0), (y0, None, 0)]
    h1, hn1, hkv = _rowwise(_st_mid, mid_rows, [], [post0, pre1, kvn],
                            [(D_MODEL, f32, None), (D_MODEL, bf16, None), (D_MODEL, bf16, None)], name="f_mid")
    ckr = _matmul(hkv, wd2, name="f_kv_down")
    proj2 = _matmul(hn1, w_mla_in, out_dtype=bf16, name="f_mla_in")
    lat_rows = [(ckr, None, 0), (proj2, MLA_Q_RANK, 0)]
    lat_nd = [(cos_k, None, 0), (sin_k, None, 0)]
    c_kv, k_rope, c_q = _rowwise(_st_latent, lat_rows, lat_nd, [kvln, mla_q_latent_norm],
                                 [(128, bf16, None), (128, bf16, None), (MLA_Q_RANK, bf16, None)], name="f_latent")
    kvu = _matmul(c_kv, wup_p, out_dtype=bf16, name="f_kv_up")
    q_att = _q_proj_fwd(c_q, wq2, cos_q, sin_q, name="f_q_proj")
    o_att, lse, og2 = _attention_fwd(q_att, kvu, k_rope, proj2, name="f_attention")
    y1 = _matmul(og2, w_mla_out, name="f_mla_wout")
    st_loss = _make_st_loss(n_tok)
    loss_rows_in = [(h1, None, 0), (y1, None, 0)]
    (loss_rows,) = _rowwise(st_loss, loss_rows_in, [(tgt, None, 0)], [post1], [(1, f32, None)], name="f_loss")
    loss_here = jnp.sum(loss_rows)

    ones_ct = jnp.ones((lp, 1), f32)
    (dh1_a, dy1), (dpost1,) = _rowwise_vjp(st_loss, loss_rows_in, [(tgt, None, 0)], [post1], [(ones_ct, None, 0)], name="b_loss")
    dog2 = _matmul(dy1, w_mla_out, tb=True, name="b_mla_wout_x")
    dw_mla_out = _matmul(og2, dy1, ta=True, name="b_mla_wout_w")
    do_att, dz2, delta = _gate_bwd(o_att, proj2, dog2, name="b_mla_gate")
    dq_att, dkn, dvv, dkr_h = _attention_bwd(q_att, kvu, k_rope, lse, delta, do_att, name="b_attention")
    dc_q, dwq2 = _q_proj_bwd(dq_att, c_q, wq2, cos_q, sin_q, name="b_q_proj")
    dkvu = jnp.concatenate([dkn, dvv], axis=1)
    dc_kv = _matmul(dkvu, wup_p, tb=True, name="b_kv_up_x")
    dwup_p = _matmul(c_kv, dkvu, ta=True, name="b_kv_up_w")

    def lat_ct(cv):
        dkr = cv[1][:, 0:128]
        for h in range(1, MLA_HEADS):
            dkr = dkr + cv[1][:, h * 128:(h + 1) * 128]
        return [cv[0], dkr, cv[2][:, :MLA_Q_RANK] + cv[2][:, MLA_Q_RANK:]]

    (dckr, dcq_pre), (dkvln, dqln) = _rowwise_vjp(
        _st_latent, lat_rows, lat_nd, [kvln, mla_q_latent_norm],
        [(dc_kv, None, 0), (dkr_h, None, 0), (dc_q, None, 0)], ct_pre=lat_ct, name="b_latent", grad_dtypes=[bf16, bf16])
    dproj2 = jnp.concatenate([dcq_pre, dz2], axis=1)
    dhn1 = _matmul(dproj2, w_mla_in, tb=True, name="b_mla_in_x")
    dw_mla_in = _matmul(hn1, dproj2, ta=True, name="b_mla_in_w")
    dhkv = _matmul(dckr, wd2, tb=True, name="b_kv_down_x")
    dwd2 = _matmul(hkv, dckr, ta=True, name="b_kv_down_w")
    (dh0_a, dy0), (dpost0, dpre1, dkvn) = _rowwise_vjp(
        _st_mid, mid_rows, [], [post0, pre1, kvn], [(dh1_a, None, 0), (dhn1, None, 0), (dhkv, None, 0)], name="b_mid")
    dog = _matmul(dy0, w_gdn_out, tb=True, name="b_gdn_wout_x")
    dw_gdn_out = _matmul(og, dy0, ta=True, name="b_gdn_wout_w")
    (do_gdn, dz), (dout_norm,) = _rowwise_vjp(_st_gdn_out, out_rows, [], [gdn_out_norm], [(dog, GDN_V_W, 0)], heads=GDN_V_HEADS,
                                              name="b_gdn_out", grad_dtypes=[f32, bf16])
    dq_g, dk_g, dv_g, dbeta, dgdec = _gdn_bwd(qn, kn, vv, beta, gdec, states, inverses, do_gdn, n_real_chunks=n_real_chunks,
                                              name="b_gdn")
    (db_col, da_col), (dalog_p, ddtb_p) = _rowwise_vjp(
        _st_gdn_gate, gate_rows, [], [alog_p, dtb_p], [(dbeta, None, 0), (dgdec, None, 0)], name="b_gdn_gate",
        grad_dtypes=[bf16, bf16])
    (dconv_q,), _ = _rowwise_vjp(_st_gdn_q, [(conv, GDN_QK_W, 0)], [], [], [(dq_g, GDN_QK_W, 0)], heads=GDN_QK_HEADS, name="b_gdn_q")
    (dconv_k,), _ = _rowwise_vjp(_st_gdn_k, [(conv, GDN_QK_W, 1)], [], [], [(dk_g, GDN_QK_W, 0)], heads=GDN_QK_HEADS, name="b_gdn_k")
    (dconv_v,), _ = _rowwise_vjp(_st_gdn_v, [(conv, GDN_V_W, 1)], [], [], [(dv_g, GDN_V_W, 0)], name="b_gdn_v")
    nq_b = GDN_QK_W // CONV_BC
    dpre_q, dcw_q = _conv_bwd(dconv_q, proj, conv_w, x_off=0, w_off=0, name="b_conv_q")
    dpre_k, dcw_k = _conv_bwd(dconv_k, proj, conv_w, x_off=nq_b, w_off=nq_b, name="b_conv_k")
    dpre_v, dcw_v = _conv_bwd(dconv_v, proj, conv_w, x_off=2 * nq_b, w_off=2 * nq_b, name="b_conv_v")
    dproj = jnp.concatenate([dpre_q, dpre_k, dpre_v, dz, db_col, da_col], axis=1)
    G = {}
    G["kv_w_down"] = jnp.concatenate([dwd2[:, :128], dwd2[:, 128:160] + dwd2[:, 416:448], dwd2[:, 160:192] + dwd2[:, 384:416]], axis=1)
    G["kv_w_up"] = jnp.transpose(dwup_p.reshape(MLA_KV_RANK, 2, MLA_HEADS, 128), (0, 2, 1, 3)).reshape(MLA_KV_RANK, 2 * MLA_V_W)
    G["mla_w_in"] = dw_mla_in
    dq4 = dwq2.reshape(MLA_Q_RANK, 2, 2, Q_GROUP, MLA_QKP)
    dqp = dq4[:, :, 0].reshape(MLA_Q_RANK, MLA_HEADS, MLA_QKP)
    dqs = dq4[:, :, 1].reshape(MLA_Q_RANK, MLA_HEADS, MLA_QKP)
    G["mla_w_q_up"] = jnp.concatenate([dqp[:, :, :128], dqp[:, :, 128:160] + dqs[:, :, 160:192],
                                       dqp[:, :, 160:192] + dqs[:, :, 128:160]], axis=2).reshape(MLA_Q_RANK, MLA_HEADS * MLA_QK)
    G["mla_w_out"] = dw_mla_out
    G["gdn_w_out"] = dw_gdn_out

    def shards_to_send(names):
        return jnp.concatenate([_to_shards(G[n], n in _COL_SHARDED).reshape(N_DEV, -1, 128).astype(bf16) for n in names], axis=1)

    dw_in_p, parts_early = _matmul(hn0, dproj, ta=True, name="b_gdn_in_w", side=(shards_to_send(_SCATTER_EARLY), False))
    G["gdn_w_in"] = jnp.concatenate([dw_in_p[:, :s1 + 16], dw_in_p[:, s1 + 128:s1 + 144]], axis=1)
    G["gdn_conv_w"] = jnp.concatenate([dcw_q, dcw_k, dcw_v], axis=1)
    dhn0, parts_late = _matmul(dproj, w_in_p, tb=True, name="b_gdn_in_x", side=(shards_to_send(_SCATTER_LATE), False))
    (dh0,), (dpre0,) = _rowwise_vjp(_st_prenorm, [(h0, None, 0)], [], [pre0], [(dhn0, None, 0)], extra=[dh0_a], name="b_prenorm0")

    grad_x = dh0[ROW0:n_real][None]
    G["meta_tokens"] = dh0[PAD_FRONT:ROW0]
    G["pre_norm"] = jnp.concatenate([dpre0, dpre1], axis=0)
    G["post_norm"] = jnp.concatenate([dpost0, dpost1], axis=0)
    G["gdn_a_log"] = dalog_p[:, :GDN_V_HEADS]
    G["gdn_dt_bias"] = ddtb_p[:, :GDN_V_HEADS]
    G["gdn_out_norm"] = dout_norm
    G["kv_norm"] = dkvn.reshape(-1)
    G["kv_latent_norm"] = dkvln.reshape(-1)
    G["mla_q_latent_norm"] = dqln

    parts_last = _exchange(shards_to_send(_SCATTER_LAST), gather=False, name="scatter_meta_grads")
    G["loss"] = loss_here.reshape(1, 1)
    for d in (W, M, V):
        d["loss"] = jnp.zeros((1, 1), f32)
    replicated = _REPLICATED + ("loss",)
    parts_r = _exchange(_pack([G[n] for n in replicated], 8), gather=True, name="gather_small_grads")
    outs = {}
    for names, parts, tag in ((_SCATTER_EARLY, parts_early, "early"), (_SCATTER_LATE, parts_late, "late"),
                              (_SCATTER_LAST, parts_last, "last"), (replicated, parts_r, "replicated")):
        w_p, m_p, v_p = (_pack([d[n] for n in names], 8) for d in (W, M, V))
        res = _reduce_adamw(parts, w_p, m_p, v_p, name="adamw_" + tag)
        for kind, buf in zip(("grad", "delta", "new_m", "new_v"), res):
            for n, a in zip(names, _unpack(buf, [W[n].shape for n in names])):
                outs[kind, n] = a
    loss = outs["grad", "loss"].reshape(())
    return (loss, grad_x, *[outs[k, n] for k in ("grad", "delta", "new_m", "new_v") for n in order])
```
